```python
import math
import jax, jax.numpy as jnp
from jax import lax
import numpy as np

D_MODEL = 1024
BATCH = 8
SEQ = 16384
DEPTH = 1

N_META = 16
D_MIX = D_MODEL
EPS = 1e-6
MLA_HEADS = 8
QK_NOPE = 64
QK_ROPE = 32
V_HEAD = 64
Q_LORA = 256
KV_LORA = 128
D_ATTN = MLA_HEADS * V_HEAD
ROPE_THETA = 10000.0
Q_BLOCK = 128
D_SSM = D_MIX - D_ATTN
SSM_GROUP = 16
N_SSM_GROUPS = D_SSM // SSM_GROUP
SSM_STATE = 64
DT_MIN = 1e-3
DT_MAX = 1e-1
IN_SPLITS = (Q_LORA, KV_LORA, QK_ROPE, D_ATTN, D_SSM, D_SSM)
D_IN = sum(IN_SPLITS)

kernel_name = "hymba_mla_s5_bidir_block"


def rmsnorm(x, w):
    xf = x.astype(jnp.float32)
    y = xf * lax.rsqrt(jnp.mean(xf * xf, axis=-1, keepdims=True) + EPS)
    return (y * w.astype(jnp.float32)).astype(x.dtype)


def rope(x, pos):
    d = x.shape[-1]
    half = d // 2
    inv = ROPE_THETA ** (-jnp.arange(half, dtype=jnp.float32) / half)
    ang = pos.astype(jnp.float32)[:, None] * inv[None, :]
    cos = jnp.cos(ang)[None, :, None, :]
    sin = jnp.sin(ang)[None, :, None, :]
    xf = x.astype(jnp.float32)
    x1, x2 = xf[..., :half], xf[..., half:]
    out = jnp.concatenate([x1 * cos - x2 * sin, x1 * sin + x2 * cos], axis=-1)
    return out.astype(x.dtype)


def block_attention(q, k, v):
    b, h, L, dqk = q.shape
    dv = v.shape[-1]
    n_blk = -(-L // Q_BLOCK)
    pad = n_blk * Q_BLOCK - L
    qp = jnp.pad(q, ((0, 0), (0, 0), (0, pad), (0, 0)))
    qb = qp.reshape(b, h, n_blk, Q_BLOCK, dqk).transpose(2, 0, 1, 3, 4)
    scale = 1.0 / math.sqrt(dqk)

    def one_block(qblk):
        s = jnp.einsum('bhqd,bhkd->bhqk', qblk, k).astype(jnp.float32) * scale
        p = jax.nn.softmax(s, axis=-1)
        return jnp.einsum('bhqk,bhkd->bhqd', p.astype(v.dtype), v)

    out = lax.map(one_block, qb)
    out = out.transpose(1, 0, 3, 2, 4).reshape(b, n_blk * Q_BLOCK, h * dv)
    return out[:, :L]


def s5_direction(u, a_re, a_im, log_dt, b_re, b_im, c_re, c_im, reverse):
    dt = jnp.exp(log_dt.astype(jnp.float32))[:, None]
    a_re = a_re.astype(jnp.float32)
    a_im = a_im.astype(jnp.float32)
    mag = jnp.exp(a_re * dt)
    abar_re = mag * jnp.cos(a_im * dt)
    abar_im = mag * jnp.sin(a_im * dt)
    num_re = abar_re - 1.0
    num_im = abar_im
    den = a_re * a_re + a_im * a_im
    coef_re = (num_re * a_re + num_im * a_im) / den
    coef_im = (num_im * a_re - num_re * a_im) / den
    b_re = b_re.astype(jnp.float32)
    b_im = b_im.astype(jnp.float32)
    bbar_re = coef_re[..., None] * b_re - coef_im[..., None] * b_im
    bbar_im = coef_re[..., None] * b_im + coef_im[..., None] * b_re
    bu_re = jnp.einsum('blgh,gph->blgp', u, bbar_re)
    bu_im = jnp.einsum('blgh,gph->blgp', u, bbar_im)
    L = u.shape[1]
    g, p = abar_re.shape
    as_re = jnp.broadcast_to(abar_re[None, None], (1, L, g, p))
    as_im = jnp.broadcast_to(abar_im[None, None], (1, L, g, p))

    def combine(left, right):
        ar1, ai1, br1, bi1 = left
        ar2, ai2, br2, bi2 = right
        ar = ar2 * ar1 - ai2 * ai1
        ai = ar2 * ai1 + ai2 * ar1
        br = ar2 * br1 - ai2 * bi1 + br2
        bi = ar2 * bi1 + ai2 * br1 + bi2
        return (ar, ai, br, bi)

    _, _, x_re, x_im = lax.associative_scan(
        combine, (as_re, as_im, bu_re, bu_im), axis=1, reverse=reverse)
    return (jnp.einsum('blgp,ghp->blgh', x_re, c_re.astype(jnp.float32))
            - jnp.einsum('blgp,ghp->blgh', x_im, c_im.astype(jnp.float32)))


def hybrid_layer(h, pos, pre_norm_w, post_norm_w, w_in, q_norm_w, w_q_up, kv_norm_w,
                 w_kv_up, attn_out_norm_w, ssm_a_re, ssm_a_im, ssm_log_dt, ssm_b_re,
                 ssm_b_im, ssm_c_re, ssm_c_im, ssm_d, w_glu, b_glu, ssm_out_norm_w, w_out):
    b, L, _ = h.shape
    xn = rmsnorm(h, pre_norm_w)
    proj = jnp.einsum('bld,de->ble', xn, w_in)
    offs = np.cumsum(IN_SPLITS)[:-1].tolist()
    q_lat, kv_lat, k_rope, attn_gate, ssm_u, ssm_gate = jnp.split(proj, offs, axis=-1)

    q = jnp.einsum('blr,re->ble', rmsnorm(q_lat, q_norm_w), w_q_up)
    q = q.reshape(b, L, MLA_HEADS, QK_NOPE + QK_ROPE)
    q_nope, q_rope = q[..., :QK_NOPE], rope(q[..., QK_NOPE:], pos)
    kv = jnp.einsum('blr,re->ble', rmsnorm(kv_lat, kv_norm_w), w_kv_up)
    kv = kv.reshape(b, L, MLA_HEADS, QK_NOPE + V_HEAD)
    k_nope, v = kv[..., :QK_NOPE], kv[..., QK_NOPE:]
    k_r = rope(k_rope[:, :, None, :], pos)
    k_r = jnp.broadcast_to(k_r, (b, L, MLA_HEADS, QK_ROPE))
    qf = jnp.concatenate([q_nope, q_rope], axis=-1).transpose(0, 2, 1, 3)
    kf = jnp.concatenate([k_nope, k_r], axis=-1).transpose(0, 2, 1, 3)
    vf = v.transpose(0, 2, 1, 3)
    y_attn = block_attention(qf, kf, vf)
    y_attn = rmsnorm(y_attn * jax.nn.silu(attn_gate), attn_out_norm_w)

    u = ssm_u.astype(jnp.float32).reshape(b, L, N_SSM_GROUPS, SSM_GROUP)
    y_f = s5_direction(u, ssm_a_re[0], ssm_a_im[0], ssm_log_dt[0], ssm_b_re[0],
                       ssm_b_im[0], ssm_c_re[0], ssm_c_im[0], reverse=False)
    y_b = s5_direction(u, ssm_a_re[1], ssm_a_im[1], ssm_log_dt[1], ssm_b_re[1],
                       ssm_b_im[1], ssm_c_re[1], ssm_c_im[1], reverse=True)
    y_ssm = (y_f + y_b).reshape(b, L, D_SSM) + ssm_d.astype(jnp.float32) * ssm_u.astype(jnp.float32)
    y_ssm = jax.nn.gelu(y_ssm).astype(h.dtype)
    glu = jnp.einsum('ble,ef->blf', y_ssm, w_glu) + b_glu
    y_ssm = glu[..., :D_SSM] * jax.nn.sigmoid(glu[..., D_SSM:])
    y_ssm = rmsnorm(y_ssm * jax.nn.silu(ssm_gate), ssm_out_norm_w)

    y = jnp.concatenate([y_attn, y_ssm], axis=-1)
    y = jnp.einsum('ble,ed->bld', y, w_out)
    return h + rmsnorm(y, post_norm_w)


def _fwd_setup_inputs(seed: int = 0) -> dict:
    key = jax.random.key(seed)
    ks = jax.random.split(key, 24)
    f32 = jnp.float32

    def nrm(k, shape, fan_in):
        return jax.random.normal(k, shape, f32) * (fan_in ** -0.5)

    def gain(k, shape):
        return 1.0 + 0.02 * jax.random.normal(k, shape, f32)

    G, P, H = N_SSM_GROUPS, SSM_STATE, SSM_GROUP
    a_re = -0.5 + 0.01 * jax.random.normal(ks[10], (DEPTH, 2, G, P), f32)
    a_im = (jnp.pi * jnp.arange(P, dtype=f32))[None, None, None, :] \
        + 0.01 * jax.random.normal(ks[11], (DEPTH, 2, G, P), f32)
    log_dt = jax.random.uniform(ks[12], (DEPTH, 2, G), f32,
                                minval=math.log(DT_MIN), maxval=math.log(DT_MAX))
    return {
        "x": jax.random.normal(ks[0], (BATCH, SEQ, D_MODEL), f32),
        "meta_tokens": jax.random.normal(ks[1], (N_META, D_MODEL), f32),
        "pre_norm_w": gain(ks[2], (DEPTH, D_MODEL)),
        "post_norm_w": gain(ks[3], (DEPTH, D_MODEL)),
        "w_in": nrm(ks[4], (DEPTH, D_MODEL, D_IN), D_MODEL),
        "q_norm_w": gain(ks[5], (DEPTH, Q_LORA)),
        "w_q_up": nrm(ks[6], (DEPTH, Q_LORA, MLA_HEADS * (QK_NOPE + QK_ROPE)), Q_LORA),
        "kv_norm_w": gain(ks[7], (DEPTH, KV_LORA)),
        "w_kv_up": nrm(ks[8], (DEPTH, KV_LORA, MLA_HEADS * (QK_NOPE + V_HEAD)), KV_LORA),
        "attn_out_norm_w": gain(ks[9], (DEPTH, D_ATTN)),
        "ssm_a_re": a_re,
        "ssm_a_im": a_im,
        "ssm_log_dt": log_dt,
        "ssm_b_re": nrm(ks[13], (DEPTH, 2, G, P, H), 2 * H),
        "ssm_b_im": nrm(ks[14], (DEPTH, 2, G, P, H), 2 * H),
        "ssm_c_re": nrm(ks[15], (DEPTH, 2, G, H, P), 2 * P),
        "ssm_c_im": nrm(ks[16], (DEPTH, 2, G, H, P), 2 * P),
        "ssm_d": jax.random.normal(ks[17], (DEPTH, D_SSM), f32),
        "w_glu": nrm(ks[18], (DEPTH, D_SSM, 2 * D_SSM), D_SSM),
        "b_glu": 0.01 * jax.random.normal(ks[19], (DEPTH, 2 * D_SSM), f32),
        "ssm_out_norm_w": gain(ks[20], (DEPTH, D_SSM)),
        "w_out": nrm(ks[21], (DEPTH, D_MIX, D_MODEL), D_MIX),
    }


def _fwd_reference(x, meta_tokens, pre_norm_w, post_norm_w, w_in, q_norm_w, w_q_up, kv_norm_w,
              w_kv_up, attn_out_norm_w, ssm_a_re, ssm_a_im, ssm_log_dt, ssm_b_re, ssm_b_im,
              ssm_c_re, ssm_c_im, ssm_d, w_glu, b_glu, ssm_out_norm_w, w_out):
    b = x.shape[0]
    meta = jnp.broadcast_to(meta_tokens[None].astype(x.dtype), (b, N_META, x.shape[-1]))
    h = jnp.concatenate([meta, x], axis=1)
    pos = jnp.arange(h.shape[1], dtype=jnp.int32)
    for i in range(DEPTH):
        h = hybrid_layer(h, pos, pre_norm_w[i], post_norm_w[i], w_in[i], q_norm_w[i],
                         w_q_up[i], kv_norm_w[i], w_kv_up[i], attn_out_norm_w[i],
                         ssm_a_re[i], ssm_a_im[i], ssm_log_dt[i], ssm_b_re[i], ssm_b_im[i],
                         ssm_c_re[i], ssm_c_im[i], ssm_d[i], w_glu[i], b_glu[i],
                         ssm_out_norm_w[i], w_out[i])
    return h[:, N_META:]


import jax as _jax
import jax.numpy as _jnp

TWIN_FORMAT = 'train_step'
FWD_PARAMS = ['x', 'meta_tokens', 'pre_norm_w', 'post_norm_w', 'w_in', 'q_norm_w', 'w_q_up', 'kv_norm_w', 'w_kv_up', 'attn_out_norm_w', 'ssm_a_re', 'ssm_a_im', 'ssm_log_dt', 'ssm_b_re', 'ssm_b_im', 'ssm_c_re', 'ssm_c_im', 'ssm_d', 'w_glu', 'b_glu', 'ssm_out_norm_w', 'w_out']
TWIN_WEIGHTS = ['meta_tokens', 'pre_norm_w', 'post_norm_w', 'w_in', 'q_norm_w', 'w_q_up', 'kv_norm_w', 'w_kv_up', 'attn_out_norm_w', 'ssm_a_re', 'ssm_a_im', 'ssm_log_dt', 'ssm_b_re', 'ssm_b_im', 'ssm_c_re', 'ssm_c_im', 'ssm_d', 'w_glu', 'b_glu', 'ssm_out_norm_w', 'w_out']
TWIN_DIFF_INPUT = 'x'
TWIN_INPUTS = ['x', 'meta_tokens', 'pre_norm_w', 'post_norm_w', 'w_in', 'q_norm_w', 'w_q_up', 'kv_norm_w', 'w_kv_up', 'attn_out_norm_w', 'ssm_a_re', 'ssm_a_im', 'ssm_log_dt', 'ssm_b_re', 'ssm_b_im', 'ssm_c_re', 'ssm_c_im', 'ssm_d', 'w_glu', 'b_glu', 'ssm_out_norm_w', 'w_out', 'loss_target', 'm_meta_tokens', 'm_pre_norm_w', 'm_post_norm_w', 'm_w_in', 'm_q_norm_w', 'm_w_q_up', 'm_kv_norm_w', 'm_w_kv_up', 'm_attn_out_norm_w', 'm_ssm_a_re', 'm_ssm_a_im', 'm_ssm_log_dt', 'm_ssm_b_re', 'm_ssm_b_im', 'm_ssm_c_re', 'm_ssm_c_im', 'm_ssm_d', 'm_w_glu', 'm_b_glu', 'm_ssm_out_norm_w', 'm_w_out', 'v_meta_tokens', 'v_pre_norm_w', 'v_post_norm_w', 'v_w_in', 'v_q_norm_w', 'v_w_q_up', 'v_kv_norm_w', 'v_w_kv_up', 'v_attn_out_norm_w', 'v_ssm_a_re', 'v_ssm_a_im', 'v_ssm_log_dt', 'v_ssm_b_re', 'v_ssm_b_im', 'v_ssm_c_re', 'v_ssm_c_im', 'v_ssm_d', 'v_w_glu', 'v_b_glu', 'v_ssm_out_norm_w', 'v_w_out']
TWIN_OUTPUTS = ['loss', 'grad_x', 'grad_meta_tokens', 'grad_pre_norm_w', 'grad_post_norm_w', 'grad_w_in', 'grad_q_norm_w', 'grad_w_q_up', 'grad_kv_norm_w', 'grad_w_kv_up', 'grad_attn_out_norm_w', 'grad_ssm_a_re', 'grad_ssm_a_im', 'grad_ssm_log_dt', 'grad_ssm_b_re', 'grad_ssm_b_im', 'grad_ssm_c_re', 'grad_ssm_c_im', 'grad_ssm_d', 'grad_w_glu', 'grad_b_glu', 'grad_ssm_out_norm_w', 'grad_w_out', 'delta_meta_tokens', 'delta_pre_norm_w', 'delta_post_norm_w', 'delta_w_in', 'delta_q_norm_w', 'delta_w_q_up', 'delta_kv_norm_w', 'delta_w_kv_up', 'delta_attn_out_norm_w', 'delta_ssm_a_re', 'delta_ssm_a_im', 'delta_ssm_log_dt', 'delta_ssm_b_re', 'delta_ssm_b_im', 'delta_ssm_c_re', 'delta_ssm_c_im', 'delta_ssm_d', 'delta_w_glu', 'delta_b_glu', 'delta_ssm_out_norm_w', 'delta_w_out', 'new_m_meta_tokens', 'new_m_pre_norm_w', 'new_m_post_norm_w', 'new_m_w_in', 'new_m_q_norm_w', 'new_m_w_q_up', 'new_m_kv_norm_w', 'new_m_w_kv_up', 'new_m_attn_out_norm_w', 'new_m_ssm_a_re', 'new_m_ssm_a_im', 'new_m_ssm_log_dt', 'new_m_ssm_b_re', 'new_m_ssm_b_im', 'new_m_ssm_c_re', 'new_m_ssm_c_im', 'new_m_ssm_d', 'new_m_w_glu', 'new_m_b_glu', 'new_m_ssm_out_norm_w', 'new_m_w_out', 'new_v_meta_tokens', 'new_v_pre_norm_w', 'new_v_post_norm_w', 'new_v_w_in', 'new_v_q_norm_w', 'new_v_w_q_up', 'new_v_kv_norm_w', 'new_v_w_kv_up', 'new_v_attn_out_norm_w', 'new_v_ssm_a_re', 'new_v_ssm_a_im', 'new_v_ssm_log_dt', 'new_v_ssm_b_re', 'new_v_ssm_b_im', 'new_v_ssm_c_re', 'new_v_ssm_c_im', 'new_v_ssm_d', 'new_v_w_glu', 'new_v_b_glu', 'new_v_ssm_out_norm_w', 'new_v_w_out']
TWIN_LEAF_KINDS = {'loss': 'loss', 'grad_x': 'grad_x', 'grad_meta_tokens': 'grad_w', 'grad_pre_norm_w': 'grad_w', 'grad_post_norm_w': 'grad_w', 'grad_w_in': 'grad_w', 'grad_q_norm_w': 'grad_w', 'grad_w_q_up': 'grad_w', 'grad_kv_norm_w': 'grad_w', 'grad_w_kv_up': 'grad_w', 'grad_attn_out_norm_w': 'grad_w', 'grad_ssm_a_re': 'grad_w', 'grad_ssm_a_im': 'grad_w', 'grad_ssm_log_dt': 'grad_w', 'grad_ssm_b_re': 'grad_w', 'grad_ssm_b_im': 'grad_w', 'grad_ssm_c_re': 'grad_w', 'grad_ssm_c_im': 'grad_w', 'grad_ssm_d': 'grad_w', 'grad_w_glu': 'grad_w', 'grad_b_glu': 'grad_w', 'grad_ssm_out_norm_w': 'grad_w', 'grad_w_out': 'grad_w', 'delta_meta_tokens': 'delta_w', 'delta_pre_norm_w': 'delta_w', 'delta_post_norm_w': 'delta_w', 'delta_w_in': 'delta_w', 'delta_q_norm_w': 'delta_w', 'delta_w_q_up': 'delta_w', 'delta_kv_norm_w': 'delta_w', 'delta_w_kv_up': 'delta_w', 'delta_attn_out_norm_w': 'delta_w', 'delta_ssm_a_re': 'delta_w', 'delta_ssm_a_im': 'delta_w', 'delta_ssm_log_dt': 'delta_w', 'delta_ssm_b_re': 'delta_w', 'delta_ssm_b_im': 'delta_w', 'delta_ssm_c_re': 'delta_w', 'delta_ssm_c_im': 'delta_w', 'delta_ssm_d': 'delta_w', 'delta_w_glu': 'delta_w', 'delta_b_glu': 'delta_w', 'delta_ssm_out_norm_w': 'delta_w', 'delta_w_out': 'delta_w', 'new_m_meta_tokens': 'new_m', 'new_m_pre_norm_w': 'new_m', 'new_m_post_norm_w': 'new_m', 'new_m_w_in': 'new_m', 'new_m_q_norm_w': 'new_m', 'new_m_w_q_up': 'new_m', 'new_m_kv_norm_w': 'new_m', 'new_m_w_kv_up': 'new_m', 'new_m_attn_out_norm_w': 'new_m', 'new_m_ssm_a_re': 'new_m', 'new_m_ssm_a_im': 'new_m', 'new_m_ssm_log_dt': 'new_m', 'new_m_ssm_b_re': 'new_m', 'new_m_ssm_b_im': 'new_m', 'new_m_ssm_c_re': 'new_m', 'new_m_ssm_c_im': 'new_m', 'new_m_ssm_d': 'new_m', 'new_m_w_glu': 'new_m', 'new_m_b_glu': 'new_m', 'new_m_ssm_out_norm_w': 'new_m', 'new_m_w_out': 'new_m', 'new_v_meta_tokens': 'new_v', 'new_v_pre_norm_w': 'new_v', 'new_v_post_norm_w': 'new_v', 'new_v_w_in': 'new_v', 'new_v_q_norm_w': 'new_v', 'new_v_w_q_up': 'new_v', 'new_v_kv_norm_w': 'new_v', 'new_v_w_kv_up': 'new_v', 'new_v_attn_out_norm_w': 'new_v', 'new_v_ssm_a_re': 'new_v', 'new_v_ssm_a_im': 'new_v', 'new_v_ssm_log_dt': 'new_v', 'new_v_ssm_b_re': 'new_v', 'new_v_ssm_b_im': 'new_v', 'new_v_ssm_c_re': 'new_v', 'new_v_ssm_c_im': 'new_v', 'new_v_ssm_d': 'new_v', 'new_v_w_glu': 'new_v', 'new_v_b_glu': 'new_v', 'new_v_ssm_out_norm_w': 'new_v', 'new_v_w_out': 'new_v'}


def _forward(args):
    return _fwd_reference(*[args[k] for k in FWD_PARAMS])


def _output_shape():
    def fwd():
        inp = _fwd_setup_inputs(0)
        return _fwd_reference(*[inp[k] for k in FWD_PARAMS])
    out = _jax.eval_shape(fwd)
    return out.shape, out.dtype

N_MICROBATCH = 1
ADAM_LR = 0.001
ADAM_B1 = 0.9
ADAM_B2 = 0.999
ADAM_EPS = 1e-08
ADAM_WD = 0.01
ADAM_STEP = 10
PER_EXAMPLE_BATCH_AXIS = {'x': 0, 'loss_target': 0}
SHARED_INPUTS = []
_WEIGHT_DTYPES = {'meta_tokens': _jnp.float32, 'pre_norm_w': _jnp.float32, 'post_norm_w': _jnp.float32, 'w_in': _jnp.float32, 'q_norm_w': _jnp.float32, 'w_q_up': _jnp.float32, 'kv_norm_w': _jnp.float32, 'w_kv_up': _jnp.float32, 'attn_out_norm_w': _jnp.float32, 'ssm_a_re': _jnp.float32, 'ssm_a_im': _jnp.float32, 'ssm_log_dt': _jnp.float32, 'ssm_b_re': _jnp.float32, 'ssm_b_im': _jnp.float32, 'ssm_c_re': _jnp.float32, 'ssm_c_im': _jnp.float32, 'ssm_d': _jnp.float32, 'w_glu': _jnp.float32, 'b_glu': _jnp.float32, 'ssm_out_norm_w': _jnp.float32, 'w_out': _jnp.float32}
MOMENT_SCALE = {'meta_tokens': 2.538238e-02, 'pre_norm_w': 1.052202e+00, 'post_norm_w': 1.280919e+02, 'w_in': 7.054801e-01, 'q_norm_w': 1.027567e+00, 'w_q_up': 6.166566e-01, 'kv_norm_w': 3.802857e+00, 'w_kv_up': 7.694510e-01, 'attn_out_norm_w': 6.989436e-01, 'ssm_a_re': 3.287350e-02, 'ssm_a_im': 3.647832e-02, 'ssm_log_dt': 2.119916e+01, 'ssm_b_re': 2.040140e-02, 'ssm_b_im': 2.054406e-02, 'ssm_c_re': 4.200059e-02, 'ssm_c_im': 4.295784e-02, 'ssm_d': 1.174546e+00, 'w_glu': 7.691764e-01, 'b_glu': 2.593848e+00, 'ssm_out_norm_w': 9.956098e-01, 'w_out': 8.898773e-01}


def _to_microbatches(a, axis):
    t = _jnp.moveaxis(a, axis, 0)
    t = t.reshape((N_MICROBATCH, t.shape[0] // N_MICROBATCH) + t.shape[1:])
    return _jnp.moveaxis(t, 1, axis + 1)


def setup_inputs(seed: int = 0) -> dict:
    inp = _fwd_setup_inputs(seed)
    key = _jax.random.fold_in(_jax.random.key(seed), 7919)
    shape, _ = _output_shape()
    out = dict(inp)
    out["loss_target"] = _jax.random.normal(_jax.random.fold_in(key, 0), shape, _jnp.float32)
    for i, name in enumerate(TWIN_WEIGHTS):
        w = inp[name].astype(_jnp.float32)
        if MOMENT_SCALE is None:
            s = _jnp.sqrt(_jnp.mean(_jnp.square(w)) + 1e-30)
        else:
            s = MOMENT_SCALE[name]
        km, kv = _jax.random.split(_jax.random.fold_in(key, i + 1))
        out[name] = w
        out["m_" + name] = s * _jax.random.normal(km, w.shape, _jnp.float32)
        out["v_" + name] = (s * s) * _jax.random.uniform(kv, w.shape, _jnp.float32, 0.5, 1.5)
    if N_MICROBATCH > 1:
        for name, axis in PER_EXAMPLE_BATCH_AXIS.items():
            out[name] = _to_microbatches(out[name], axis)
    return {'x': out['x'], 'meta_tokens': out['meta_tokens'], 'pre_norm_w': out['pre_norm_w'], 'post_norm_w': out['post_norm_w'], 'w_in': out['w_in'], 'q_norm_w': out['q_norm_w'], 'w_q_up': out['w_q_up'], 'kv_norm_w': out['kv_norm_w'], 'w_kv_up': out['w_kv_up'], 'attn_out_norm_w': out['attn_out_norm_w'], 'ssm_a_re': out['ssm_a_re'], 'ssm_a_im': out['ssm_a_im'], 'ssm_log_dt': out['ssm_log_dt'], 'ssm_b_re': out['ssm_b_re'], 'ssm_b_im': out['ssm_b_im'], 'ssm_c_re': out['ssm_c_re'], 'ssm_c_im': out['ssm_c_im'], 'ssm_d': out['ssm_d'], 'w_glu': out['w_glu'], 'b_glu': out['b_glu'], 'ssm_out_norm_w': out['ssm_out_norm_w'], 'w_out': out['w_out'], 'loss_target': out['loss_target'], 'm_meta_tokens': out['m_meta_tokens'], 'm_pre_norm_w': out['m_pre_norm_w'], 'm_post_norm_w': out['m_post_norm_w'], 'm_w_in': out['m_w_in'], 'm_q_norm_w': out['m_q_norm_w'], 'm_w_q_up': out['m_w_q_up'], 'm_kv_norm_w': out['m_kv_norm_w'], 'm_w_kv_up': out['m_w_kv_up'], 'm_attn_out_norm_w': out['m_attn_out_norm_w'], 'm_ssm_a_re': out['m_ssm_a_re'], 'm_ssm_a_im': out['m_ssm_a_im'], 'm_ssm_log_dt': out['m_ssm_log_dt'], 'm_ssm_b_re': out['m_ssm_b_re'], 'm_ssm_b_im': out['m_ssm_b_im'], 'm_ssm_c_re': out['m_ssm_c_re'], 'm_ssm_c_im': out['m_ssm_c_im'], 'm_ssm_d': out['m_ssm_d'], 'm_w_glu': out['m_w_glu'], 'm_b_glu': out['m_b_glu'], 'm_ssm_out_norm_w': out['m_ssm_out_norm_w'], 'm_w_out': out['m_w_out'], 'v_meta_tokens': out['v_meta_tokens'], 'v_pre_norm_w': out['v_pre_norm_w'], 'v_post_norm_w': out['v_post_norm_w'], 'v_w_in': out['v_w_in'], 'v_q_norm_w': out['v_q_norm_w'], 'v_w_q_up': out['v_w_q_up'], 'v_kv_norm_w': out['v_kv_norm_w'], 'v_w_kv_up': out['v_w_kv_up'], 'v_attn_out_norm_w': out['v_attn_out_norm_w'], 'v_ssm_a_re': out['v_ssm_a_re'], 'v_ssm_a_im': out['v_ssm_a_im'], 'v_ssm_log_dt': out['v_ssm_log_dt'], 'v_ssm_b_re': out['v_ssm_b_re'], 'v_ssm_b_im': out['v_ssm_b_im'], 'v_ssm_c_re': out['v_ssm_c_re'], 'v_ssm_c_im': out['v_ssm_c_im'], 'v_ssm_d': out['v_ssm_d'], 'v_w_glu': out['v_w_glu'], 'v_b_glu': out['v_b_glu'], 'v_ssm_out_norm_w': out['v_ssm_out_norm_w'], 'v_w_out': out['v_w_out']}


def _loss(weights, diff, rest, loss_target):
    with _jax.named_scope("forward"):
        args = {**rest, TWIN_DIFF_INPUT: diff, **{k: w.astype(_WEIGHT_DTYPES[k]) for k, w in weights.items()}}
        y = _forward(args)
    with _jax.named_scope("loss_head"):
        err = _jnp.square(y.astype(_jnp.float32) - loss_target)
        return 0.5 * _jnp.sum(_jnp.mean(err, axis=-1)) if err.ndim else 0.5 * err


def _adamw(w, g, m, v):
    m = ADAM_B1 * m + (1.0 - ADAM_B1) * g
    v = ADAM_B2 * v + (1.0 - ADAM_B2) * _jnp.square(g)
    m_hat = m / (1.0 - ADAM_B1 ** ADAM_STEP)
    v_hat = v / (1.0 - ADAM_B2 ** ADAM_STEP)
    delta = -ADAM_LR * (m_hat / (_jnp.sqrt(v_hat) + ADAM_EPS) + ADAM_WD * w)
    return delta, m, v


def reference(x, meta_tokens, pre_norm_w, post_norm_w, w_in, q_norm_w, w_q_up, kv_norm_w, w_kv_up, attn_out_norm_w, ssm_a_re, ssm_a_im, ssm_log_dt, ssm_b_re, ssm_b_im, ssm_c_re, ssm_c_im, ssm_d, w_glu, b_glu, ssm_out_norm_w, w_out, loss_target, m_meta_tokens, m_pre_norm_w, m_post_norm_w, m_w_in, m_q_norm_w, m_w_q_up, m_kv_norm_w, m_w_kv_up, m_attn_out_norm_w, m_ssm_a_re, m_ssm_a_im, m_ssm_log_dt, m_ssm_b_re, m_ssm_b_im, m_ssm_c_re, m_ssm_c_im, m_ssm_d, m_w_glu, m_b_glu, m_ssm_out_norm_w, m_w_out, v_meta_tokens, v_pre_norm_w, v_post_norm_w, v_w_in, v_q_norm_w, v_w_q_up, v_kv_norm_w, v_w_kv_up, v_attn_out_norm_w, v_ssm_a_re, v_ssm_a_im, v_ssm_log_dt, v_ssm_b_re, v_ssm_b_im, v_ssm_c_re, v_ssm_c_im, v_ssm_d, v_w_glu, v_b_glu, v_ssm_out_norm_w, v_w_out):
    given = dict(x=x, meta_tokens=meta_tokens, pre_norm_w=pre_norm_w, post_norm_w=post_norm_w, w_in=w_in, q_norm_w=q_norm_w, w_q_up=w_q_up, kv_norm_w=kv_norm_w, w_kv_up=w_kv_up, attn_out_norm_w=attn_out_norm_w, ssm_a_re=ssm_a_re, ssm_a_im=ssm_a_im, ssm_log_dt=ssm_log_dt, ssm_b_re=ssm_b_re, ssm_b_im=ssm_b_im, ssm_c_re=ssm_c_re, ssm_c_im=ssm_c_im, ssm_d=ssm_d, w_glu=w_glu, b_glu=b_glu, ssm_out_norm_w=ssm_out_norm_w, w_out=w_out, loss_target=loss_target, m_meta_tokens=m_meta_tokens, m_pre_norm_w=m_pre_norm_w, m_post_norm_w=m_post_norm_w, m_w_in=m_w_in, m_q_norm_w=m_q_norm_w, m_w_q_up=m_w_q_up, m_kv_norm_w=m_kv_norm_w, m_w_kv_up=m_w_kv_up, m_attn_out_norm_w=m_attn_out_norm_w, m_ssm_a_re=m_ssm_a_re, m_ssm_a_im=m_ssm_a_im, m_ssm_log_dt=m_ssm_log_dt, m_ssm_b_re=m_ssm_b_re, m_ssm_b_im=m_ssm_b_im, m_ssm_c_re=m_ssm_c_re, m_ssm_c_im=m_ssm_c_im, m_ssm_d=m_ssm_d, m_w_glu=m_w_glu, m_b_glu=m_b_glu, m_ssm_out_norm_w=m_ssm_out_norm_w, m_w_out=m_w_out, v_meta_tokens=v_meta_tokens, v_pre_norm_w=v_pre_norm_w, v_post_norm_w=v_post_norm_w, v_w_in=v_w_in, v_q_norm_w=v_q_norm_w, v_w_q_up=v_w_q_up, v_kv_norm_w=v_kv_norm_w, v_w_kv_up=v_w_kv_up, v_attn_out_norm_w=v_attn_out_norm_w, v_ssm_a_re=v_ssm_a_re, v_ssm_a_im=v_ssm_a_im, v_ssm_log_dt=v_ssm_log_dt, v_ssm_b_re=v_ssm_b_re, v_ssm_b_im=v_ssm_b_im, v_ssm_c_re=v_ssm_c_re, v_ssm_c_im=v_ssm_c_im, v_ssm_d=v_ssm_d, v_w_glu=v_w_glu, v_b_glu=v_b_glu, v_ssm_out_norm_w=v_ssm_out_norm_w, v_w_out=v_w_out)
    weights = {n: given[n] for n in TWIN_WEIGHTS}
    shared = {n: given[n] for n in SHARED_INPUTS}
    per_example = {n: given[n] for n in ['x']}
    grad_fn = _jax.value_and_grad(_loss, argnums=(0, 1))

    def one_microbatch(ex, loss_target):
        ex = dict(ex)
        diff = ex.pop(TWIN_DIFF_INPUT)
        return grad_fn(weights, diff, {**shared, **ex}, loss_target)

    if N_MICROBATCH == 1:
        loss, (grad_w, grad_x) = one_microbatch(per_example, given["loss_target"])
    else:
        def body(carry, xs):
            loss_sum, grad_sum = carry
            l_k, (gw_k, gx_k) = one_microbatch(xs[0], xs[1])
            with _jax.named_scope("update"):
                return (loss_sum + l_k, _jax.tree.map(_jnp.add, grad_sum, gw_k)), gx_k

        init = (_jnp.zeros((), _jnp.float32), _jax.tree.map(_jnp.zeros_like, weights))
        (loss, grad_w), grad_x = _jax.lax.scan(body, init, (per_example, given["loss_target"]))
    with _jax.named_scope("update"):
        delta_w, new_m, new_v = {}, {}, {}
        for n in TWIN_WEIGHTS:
            delta_w[n], new_m[n], new_v[n] = _adamw(weights[n], grad_w[n], given["m_" + n], given["v_" + n])
    return (loss, grad_x, *[grad_w[n] for n in TWIN_WEIGHTS], *[delta_w[n] for n in TWIN_WEIGHTS],
            *[new_m[n] for n in TWIN_WEIGHTS], *[new_v[n] for n in TWIN_WEIGHTS])
```

```python
import functools
import math

import jax
import jax.numpy as jnp
from jax import lax
from jax.experimental import pallas as pl
from jax.experimental.pallas import tpu as pltpu

F32 = jnp.float32
BF16 = jnp.bfloat16
MESH = pl.DeviceIdType.MESH

D_MODEL = 1024
N_META = 16
EPS = 1e-6
HEADS = 8
QK_NOPE = 64
QK_ROPE = 32
V_HEAD = 64
Q_LORA = 256
KV_LORA = 128
D_ATTN = 512
D_SSM = 512
SSM_GROUP = 16
N_GROUPS = 32
SSM_STATE = 64
N_STATE = N_GROUPS * SSM_STATE
ROPE_THETA = 10000.0
HEAD_PAD = 128
D_EXP = HEADS * HEAD_PAD
MASK_LANE = QK_NOPE + QK_ROPE
NEG_BIG = -1e30
SCALE = 1.0 / math.sqrt(QK_NOPE + QK_ROPE)
SCAN_COLS = 512
SSM_BLOCKS = 4
BLK_CH = D_SSM // SSM_BLOCKS
BLK_ST = N_STATE // SSM_BLOCKS

P_GATE_A = (0, 1024)
P_U = (1024, 512)
P_GATE_S = (1536, 512)
P_QLAT = (2048, 256)
P_KVLAT = (2304, 128)
P_KROPE = (2432, 128)
D_PROJ = 2560

ADAM_LR = 0.001
ADAM_B1 = 0.9
ADAM_B2 = 0.999
ADAM_EPS = 1e-08
ADAM_WD = 0.01
ADAM_STEP = 10

VMEM_LIMIT = 60 * 1024 * 1024

BIG = ("w_in", "w_q_up", "w_kv_up", "w_glu", "w_out", "meta_tokens")
SMALL = ("pre_norm_w", "post_norm_w", "q_norm_w", "kv_norm_w", "attn_out_norm_w", "ssm_a_re", "ssm_a_im",
         "ssm_log_dt", "ssm_b_re", "ssm_b_im", "ssm_c_re", "ssm_c_im", "ssm_d", "b_glu", "ssm_out_norm_w")
WEIGHTS = ("meta_tokens", "pre_norm_w", "post_norm_w", "w_in", "q_norm_w", "w_q_up", "kv_norm_w", "w_kv_up",
           "attn_out_norm_w", "ssm_a_re", "ssm_a_im", "ssm_log_dt", "ssm_b_re", "ssm_b_im", "ssm_c_re", "ssm_c_im",
           "ssm_d", "w_glu", "b_glu", "ssm_out_norm_w", "w_out")


def _cparams(sem=None):
    return pltpu.CompilerParams(dimension_semantics=sem, vmem_limit_bytes=VMEM_LIMIT)


def _dot(a, b):
    return jnp.dot(a, b, preferred_element_type=F32)


def _dot_nt(a, b):
    return lax.dot_general(a, b, (((1,), (1,)), ((), ())), preferred_element_type=F32)


def _dot_tn(a, b):
    return lax.dot_general(a, b, (((0,), (0,)), ((), ())), preferred_element_type=F32)


def _sigmoid(x):
    return 1.0 / (1.0 + jnp.exp(-x))


def _rms_fwd(x, w, n):
    r = lax.rsqrt(jnp.sum(x * x, axis=-1, keepdims=True) * (1.0 / n) + EPS)
    return x * r * w, r


def _rms_bwd(x, r, w, dy, n):
    dyw = dy * w
    dx = r * dyw - x * (r * r * r) * (jnp.sum(dyw * x, axis=-1, keepdims=True) * (1.0 / n))
    dw = jnp.sum(dy * (x * r), axis=0, keepdims=True)
    return dx, dw


def _rope_apply(x, cos, sina, sinb):
    return x * cos + pltpu.roll(x, 16, 1) * sina + pltpu.roll(x, HEAD_PAD - 16, 1) * sinb


def _rope_transpose(g, cos, sina, sinb):
    return g * cos + pltpu.roll(g * sina, HEAD_PAD - 16, 1) + pltpu.roll(g * sinb, 16, 1)


def _row_tile(lp):
    return 640 if lp % 640 == 0 else 128


def _ssm_tile(lp):
    return 320 if lp % 320 == 0 else 128


def _rows(tm, off_width):
    off, width = off_width
    return pl.BlockSpec((tm, width), lambda i: (i, off // width))


def _whole(shape, single=True):
    nd = len(shape)
    if single:
        return pl.BlockSpec(shape, lambda *_: (0,) * nd, pipeline_mode=pl.Buffered(1))
    return pl.BlockSpec(shape, lambda *_: (0,) * nd)


def _out_whole(shape):
    return _whole(shape, single=False)


def _pick_tile(rows, cap):
    best = 8
    for t in range(8, cap + 1, 8):
        if rows % t == 0:
            best = t
    return best


def _in_proj_fwd(h, pre_w, w_in_p):
    lp = h.shape[0]
    tm = _row_tile(lp)

    def body(h_ref, w_ref, win_ref, proj_ref):
        xn, _ = _rms_fwd(h_ref[...], w_ref[...], D_MODEL)
        proj_ref[...] = _dot(xn.astype(BF16), win_ref[...])

    return pl.pallas_call(
        body, name="in_proj_fwd", grid=(lp // tm,),
        in_specs=[_rows(tm, (0, D_MODEL)), _whole((1, D_MODEL)), _whole((D_MODEL, D_PROJ))],
        out_specs=_rows(tm, (0, D_PROJ)),
        out_shape=jax.ShapeDtypeStruct((lp, D_PROJ), F32),
        compiler_params=_cparams(("parallel",)),
    )(h, pre_w, w_in_p)


def _attn_prep_fwd(proj, q_norm_w, kv_norm_w, wq_p, wk_p, wv_p, cos, sina, sinb, l_real):
    lp = proj.shape[0]
    tm = _row_tile(lp)

    def body(ql_ref, kvl_ref, kr_ref, qw_ref, kw_ref, wq_ref, wk_ref, wv_ref, cos_ref, sa_ref, sb_ref,
             q_ref, k_ref, v_ref):
        cos_t, sa_t, sb_t = cos_ref[...], sa_ref[...], sb_ref[...]
        qn, _ = _rms_fwd(ql_ref[...], qw_ref[...], Q_LORA)
        kvn, _ = _rms_fwd(kvl_ref[...], kw_ref[...], KV_LORA)
        qp = _dot(qn.astype(BF16), wq_ref[...])
        kp = _dot(kvn.astype(BF16), wk_ref[...])
        v_ref[...] = _dot(kvn.astype(BF16), wv_ref[...]).astype(BF16)
        lane = lax.broadcasted_iota(jnp.int32, (tm, HEAD_PAD), 1)
        row = lax.broadcasted_iota(jnp.int32, (tm, HEAD_PAD), 0) + pl.program_id(0) * tm
        q_one = jnp.where(lane == MASK_LANE, 1.0, 0.0)
        k_add = _rope_apply(kr_ref[...], cos_t, sa_t, sb_t) + jnp.where((lane == MASK_LANE) & (row >= l_real), NEG_BIG, 0.0)
        for hd in range(HEADS):
            blk = slice(hd * HEAD_PAD, (hd + 1) * HEAD_PAD)
            q_ref[:, blk] = (_rope_apply(qp[:, blk], cos_t, sa_t, sb_t) + q_one).astype(BF16)
            k_ref[:, blk] = (kp[:, blk] + k_add).astype(BF16)

    tab = _rows(tm, (0, HEAD_PAD))
    out = jax.ShapeDtypeStruct((lp, D_EXP), BF16)
    return pl.pallas_call(
        body, name="attn_prep_fwd", grid=(lp // tm,),
        in_specs=[_rows(tm, P_QLAT), _rows(tm, P_KVLAT), _rows(tm, P_KROPE), _whole((1, Q_LORA)), _whole((1, KV_LORA)),
                  _whole((Q_LORA, D_EXP)), _whole((KV_LORA, D_EXP)), _whole((KV_LORA, D_EXP)), tab, tab, tab],
        out_specs=[_rows(tm, (0, D_EXP))] * 3,
        out_shape=[out, out, out],
        compiler_params=_cparams(("parallel",)),
    )(proj, proj, proj, q_norm_w, kv_norm_w, wq_p, wk_p, wv_p, cos, sina, sinb)


def _flash_fwd(q, k, v):
    lp = q.shape[0]
    tq = tk = _row_tile(lp)
    nk = lp // tk

    def body(q_ref, k_ref, v_ref, o_ref, lse_ref):
        qt = q_ref[...]

        def step(j, carry):
            m, l, acc = carry
            ks = pl.multiple_of(j * tk, tk)
            s = _dot_nt(qt, k_ref[pl.ds(ks, tk), :]) * SCALE
            m_new = jnp.maximum(m, jnp.max(s, axis=-1, keepdims=True))
            alpha = jnp.exp(m - m_new)
            p = jnp.exp(s - m_new)
            l = alpha * l + jnp.sum(p, axis=-1, keepdims=True)
            acc = alpha * acc + _dot(p.astype(BF16), v_ref[pl.ds(ks, tk), :])
            return m_new, l, acc

        m0 = jnp.full((tq, 1), NEG_BIG, F32)
        m, l, acc = lax.fori_loop(0, nk, step, (m0, jnp.zeros((tq, 1), F32), jnp.zeros((tq, HEAD_PAD), F32)))
        o_ref[...] = acc / l
        lse_ref[...] = m + jnp.log(l)

    return pl.pallas_call(
        body, name="flash_fwd", grid=(HEADS, lp // tq),
        in_specs=[pl.BlockSpec((tq, HEAD_PAD), lambda hd, i: (i, hd)),
                  pl.BlockSpec((lp, HEAD_PAD), lambda hd, i: (0, hd)),
                  pl.BlockSpec((lp, HEAD_PAD), lambda hd, i: (0, hd))],
        out_specs=[pl.BlockSpec((tq, HEAD_PAD), lambda hd, i: (i, hd)),
                   pl.BlockSpec((None, tq, 1), lambda hd, i: (hd, i, 0))],
        out_shape=[jax.ShapeDtypeStruct((lp, D_EXP), F32), jax.ShapeDtypeStruct((HEADS, lp, 1), F32)],
        compiler_params=_cparams(("parallel", "parallel")),
    )(q, k, v)


def _scan_rows(xr_ref, xi_ref, base, n_rows, coef_ref, carry_ref, reverse, tile_fn=None, acc_refs=()):
    n_tiles = n_rows // 8
    shifts = (7, 6, 4) if reverse else (1, 2, 4)
    for cg in range(N_STATE // SCAN_COLS):
        cols = slice(cg * SCAN_COLS, (cg + 1) * SCAN_COLS)
        co = [coef_ref[k, :, cols] for k in range(8)]

        def step(t, carry, cols=cols, co=co):
            cr, ci = carry[0], carry[1]
            tt = (n_tiles - 1 - t) if reverse else t
            start = pl.multiple_of(base + tt * 8, 8)
            tr = xr_ref[pl.ds(start, 8), cols]
            ti = xi_ref[pl.ds(start, 8), cols]
            for lvl in range(3):
                ar, ai = co[2 * lvl], co[2 * lvl + 1]
                sr = pltpu.roll(tr, shifts[lvl], 0)
                si = pltpu.roll(ti, shifts[lvl], 0)
                tr, ti = tr + ar * sr - ai * si, ti + ar * si + ai * sr
            tr, ti = tr + co[6] * cr - co[7] * ci, ti + co[6] * ci + co[7] * cr
            xr_ref[pl.ds(start, 8), cols] = tr
            xi_ref[pl.ds(start, 8), cols] = ti
            accs = carry[2:]
            if tile_fn is not None:
                accs = tuple(a + d for a, d in zip(accs, tile_fn(start, cols, tr, ti)))
            new_c = (tr[0:1], ti[0:1]) if reverse else (tr[7:8], ti[7:8])
            return new_c + accs

        init = (carry_ref[0:1, cols], carry_ref[1:2, cols]) + tuple(a[:, cols] for a in acc_refs)
        out = lax.fori_loop(0, n_tiles, step, init)
        carry_ref[0:1, cols] = out[0]
        carry_ref[1:2, cols] = out[1]
        for a, val in zip(acc_refs, out[2:]):
            a[:, cols] = val


def _ssm_fwd(proj, coef, b_re, b_im, c_re, c_im_neg, reverse):
    lp = proj.shape[0]
    t = _ssm_tile(lp)
    n = lp // t
    order = (lambda i: n - 1 - i) if reverse else (lambda i: i)

    def body(u_ref, coef_ref, bre_ref, bim_ref, cre_ref, cim_ref, y_ref, st_ref, xr, xi, carry):
        @pl.when(pl.program_id(0) == 0)
        def _():
            carry[...] = jnp.zeros_like(carry)

        st_ref[...] = carry[0:2, :]
        ub = u_ref[...].astype(BF16)
        for j in range(SSM_BLOCKS):
            ch, stt = slice(j * BLK_CH, (j + 1) * BLK_CH), slice(j * BLK_ST, (j + 1) * BLK_ST)
            xr[:, stt] = _dot(ub[:, ch], bre_ref[j])
            xi[:, stt] = _dot(ub[:, ch], bim_ref[j])
        _scan_rows(xr, xi, 0, t, coef_ref, carry, reverse)
        for j in range(SSM_BLOCKS):
            ch, stt = slice(j * BLK_CH, (j + 1) * BLK_CH), slice(j * BLK_ST, (j + 1) * BLK_ST)
            y_ref[:, ch] = _dot(xr[:, stt].astype(BF16), cre_ref[j]) + _dot(xi[:, stt].astype(BF16), cim_ref[j])

    wb, wc = _whole((SSM_BLOCKS, BLK_CH, BLK_ST)), _whole((SSM_BLOCKS, BLK_ST, BLK_CH))
    return pl.pallas_call(
        body, name="ssm_fwd_rev" if reverse else "ssm_fwd", grid=(n,),
        in_specs=[pl.BlockSpec((t, D_SSM), lambda i: (order(i), P_U[0] // D_SSM)), _whole((8, 8, N_STATE)), wb, wb, wc, wc],
        out_specs=[pl.BlockSpec((t, D_SSM), lambda i: (order(i), 0)),
                   pl.BlockSpec((None, 2, N_STATE), lambda i: (order(i), 0, 0))],
        out_shape=[jax.ShapeDtypeStruct((lp, D_SSM), F32), jax.ShapeDtypeStruct((n, 2, N_STATE), F32)],
        scratch_shapes=[pltpu.VMEM((t, N_STATE), F32), pltpu.VMEM((t, N_STATE), F32), pltpu.VMEM((8, N_STATE), F32)],
        compiler_params=_cparams(("arbitrary",)),
    )(proj, coef, b_re, b_im, c_re, c_im_neg)


GELU_C0 = math.sqrt(2.0 / math.pi)
GELU_C1 = 0.044715


def _mid(h, tgt, o_exp, proj, y0, y1, ssm_d, w_glu, w_glu_t, b_glu, ssm_norm_w, attn_norm_w_e, w_out_a, w_out_s,
         w_out_a_t, w_out_s_t, post_w, l_real):
    lp = h.shape[0]
    tm = 128

    def body(h_ref, tgt_ref, o_ref, ga_ref, u_ref, sg_ref, y0_ref, y1_ref, d_ref, wg_ref, wgt_ref, bg_ref, ws_ref, wa_ref,
             woa_ref, wos_ref, woat_ref, wost_ref, pw_ref,
             do_ref, delta_ref, dga_ref, dyp_ref, dsg_ref, dres_ref, dwoa_ref, dwos_ref, dwg_ref, vec_ref):
        @pl.when(pl.program_id(0) == 0)
        def _():
            dwoa_ref[...] = jnp.zeros_like(dwoa_ref)
            dwos_ref[...] = jnp.zeros_like(dwos_ref)
            dwg_ref[...] = jnp.zeros_like(dwg_ref)
            vec_ref[...] = jnp.zeros_like(vec_ref)

        u = u_ref[...]
        ypre = y0_ref[...] + y1_ref[...] + d_ref[...] * u
        th = jnp.tanh(GELU_C0 * (ypre + GELU_C1 * ypre * ypre * ypre))
        gel = 0.5 * ypre * (1.0 + th)
        gel_b = gel.astype(BF16)
        glu = _dot(gel_b, wg_ref[...]) + bg_ref[...]
        g1, g2 = glu[:, :D_SSM], glu[:, D_SSM:]
        sig2 = _sigmoid(g2)
        z = g1 * sig2
        sg = sg_ref[...]
        sgs = _sigmoid(sg)
        sil_s = sg * sgs
        s = z * sil_s
        ys, r_s = _rms_fwd(s, ws_ref[...], D_SSM)

        o = o_ref[...]
        ga = ga_ref[...]
        gas = _sigmoid(ga)
        sil_a = ga * gas
        a = o * sil_a
        ya, r_a = _rms_fwd(a, wa_ref[...], D_ATTN)

        ya_b, ys_b = ya.astype(BF16), ys.astype(BF16)
        y = _dot(ya_b, woa_ref[...]) + _dot(ys_b, wos_ref[...])
        yn, r_y = _rms_fwd(y, pw_ref[...], D_MODEL)
        row = lax.broadcasted_iota(jnp.int32, (tm, 1), 0) + pl.program_id(0) * tm
        valid = ((row >= N_META) & (row < l_real)).astype(F32)
        err = (h_ref[...] + yn - tgt_ref[...]) * valid
        loss = 0.5 * jnp.sum(jnp.sum(err * err, axis=-1, keepdims=True), axis=0, keepdims=True) * (1.0 / D_MODEL)
        dout = err * (1.0 / D_MODEL)
        dres_ref[...] = dout

        dy, d_pw = _rms_bwd(y, r_y, pw_ref[...], dout, D_MODEL)
        dy_b = dy.astype(BF16)
        dya = _dot(dy_b, woat_ref[...])
        dys = _dot(dy_b, wost_ref[...])
        dwoa_ref[...] += _dot_tn(ya_b, dy_b)
        dwos_ref[...] += _dot_tn(ys_b, dy_b)

        da, d_wa = _rms_bwd(a, r_a, wa_ref[...], dya, D_ATTN)
        d_o = da * sil_a
        dga_ref[...] = da * o * (gas * (1.0 + ga * (1.0 - gas)))
        do_ref[...] = d_o.astype(BF16)
        prod = d_o * o
        lane8 = lax.broadcasted_iota(jnp.int32, (tm, HEADS), 1)
        delta = jnp.zeros((tm, HEADS), F32)
        for hd in range(HEADS):
            delta = jnp.where(lane8 == hd, jnp.sum(prod[:, hd * HEAD_PAD:(hd + 1) * HEAD_PAD], axis=-1, keepdims=True), delta)
        delta_ref[...] = delta

        ds, d_ws = _rms_bwd(s, r_s, ws_ref[...], dys, D_SSM)
        dz = ds * sil_s
        dsg_ref[...] = ds * z * (sgs * (1.0 + sg * (1.0 - sgs)))
        dglu = jnp.concatenate([dz * sig2, dz * g1 * sig2 * (1.0 - sig2)], axis=-1)
        dglu_b = dglu.astype(BF16)
        dwg_ref[...] += _dot_tn(gel_b, dglu_b)
        dgel = _dot(dglu_b, wgt_ref[...])
        dgelu = 0.5 * (1.0 + th) + 0.5 * ypre * (1.0 - th * th) * (GELU_C0 * (1.0 + 3.0 * GELU_C1 * ypre * ypre))
        dyp = dgel * dgelu
        dyp_ref[...] = dyp

        vec_ref[0:1, :] += d_pw
        vec_ref[1:2, :] += d_wa
        vec_ref[2:3, 0:D_SSM] += d_ws
        vec_ref[3:4, 0:D_SSM] += jnp.sum(dyp * u, axis=0, keepdims=True)
        vec_ref[4:5, :] += jnp.sum(dglu, axis=0, keepdims=True)
        vec_ref[5:6, :] += jnp.broadcast_to(loss, (1, D_MODEL))

    full = lambda off: _rows(tm, (off, D_MODEL))
    half = lambda off: _rows(tm, (off, D_SSM))
    return pl.pallas_call(
        body, name="mid", grid=(lp // tm,),
        in_specs=[full(0), full(0), full(0), _rows(tm, P_GATE_A), _rows(tm, P_U), _rows(tm, P_GATE_S), half(0), half(0),
                  _whole((1, D_SSM)), _whole((D_SSM, 2 * D_SSM)), _whole((2 * D_SSM, D_SSM)), _whole((1, 2 * D_SSM)),
                  _whole((1, D_SSM)), _whole((1, D_EXP)), _whole((D_EXP, D_MODEL)), _whole((D_SSM, D_MODEL)),
                  _whole((D_MODEL, D_EXP)), _whole((D_MODEL, D_SSM)), _whole((1, D_MODEL))],
        out_specs=[full(0), _rows(tm, (0, HEADS)), full(0), half(0), half(0), full(0),
                   _out_whole((D_EXP, D_MODEL)), _out_whole((D_SSM, D_MODEL)), _out_whole((D_SSM, 2 * D_SSM)),
                   _out_whole((8, D_MODEL))],
        out_shape=[jax.ShapeDtypeStruct((lp, D_EXP), BF16), jax.ShapeDtypeStruct((lp, HEADS), F32),
                   jax.ShapeDtypeStruct((lp, D_EXP), F32), jax.ShapeDtypeStruct((lp, D_SSM), F32),
                   jax.ShapeDtypeStruct((lp, D_SSM), F32), jax.ShapeDtypeStruct((lp, D_MODEL), F32),
                   jax.ShapeDtypeStruct((D_EXP, D_MODEL), F32), jax.ShapeDtypeStruct((D_SSM, D_MODEL), F32),
                   jax.ShapeDtypeStruct((D_SSM, 2 * D_SSM), F32), jax.ShapeDtypeStruct((8, D_MODEL), F32)],
        compiler_params=_cparams(("arbitrary",)),
    )(h, tgt, o_exp, proj, proj, proj, y0, y1, ssm_d, w_glu, w_glu_t, b_glu, ssm_norm_w, attn_norm_w_e, w_out_a, w_out_s,
      w_out_a_t, w_out_s_t, post_w)


def _ssm_bwd(proj, dyp, states, coef, coef_adj, b_re, b_im, b_re_t, b_im_t, c_re_t, c_im_neg_t, reverse):
    lp = proj.shape[0]
    t = _ssm_tile(lp)
    n = lp // t
    order = (lambda i: i) if reverse else (lambda i: n - 1 - i)
    edge = (t + 8) if reverse else 7

    def body(u_ref, dy_ref, st_ref, coef_ref, coefa_ref, bre_ref, bim_ref, bret_ref, bimt_ref, cret_ref, cimt_ref,
             du_ref, dbre_ref, dbim_ref, dcre_ref, dcim_ref, da_ref, xr, xi, gr, gi, carry_x, carry_g):
        @pl.when(pl.program_id(0) == 0)
        def _():
            carry_g[...] = jnp.zeros_like(carry_g)
            carry_x[...] = jnp.zeros_like(carry_x)
            dbre_ref[...] = jnp.zeros_like(dbre_ref)
            dbim_ref[...] = jnp.zeros_like(dbim_ref)
            dcre_ref[...] = jnp.zeros_like(dcre_ref)
            dcim_ref[...] = jnp.zeros_like(dcim_ref)
            da_ref[...] = jnp.zeros_like(da_ref)
            for halo in (slice(0, 8), slice(t + 8, t + 16)):
                xr[halo, :] = jnp.zeros((8, N_STATE), F32)
                xi[halo, :] = jnp.zeros((8, N_STATE), F32)

        ub = u_ref[...].astype(BF16)
        dyb = dy_ref[...].astype(BF16)
        carry_x[0:2, :] = st_ref[...]
        xr[edge:edge + 1, :] = st_ref[0:1, :]
        xi[edge:edge + 1, :] = st_ref[1:2, :]
        blocks = [(slice(j * BLK_CH, (j + 1) * BLK_CH), slice(j * BLK_ST, (j + 1) * BLK_ST)) for j in range(SSM_BLOCKS)]
        for j, (ch, stt) in enumerate(blocks):
            xr[8:t + 8, stt] = _dot(ub[:, ch], bre_ref[j])
            xi[8:t + 8, stt] = _dot(ub[:, ch], bim_ref[j])
            gr[:, stt] = _dot(dyb[:, ch], cret_ref[j])
            gi[:, stt] = _dot(dyb[:, ch], cimt_ref[j])
        _scan_rows(xr, xi, 8, t, coef_ref, carry_x, reverse)

        row8 = lax.broadcasted_iota(jnp.int32, (8, SCAN_COLS), 0)

        def tile_fn(start, cols, g_re, g_im):
            xs = pl.multiple_of(start + 8, 8)
            if reverse:
                nb = pl.multiple_of(start + 16, 8)
                xn_r = jnp.where(row8 == 7, xr[pl.ds(nb, 8), cols][0:1], pltpu.roll(xr[pl.ds(xs, 8), cols], 7, 0))
                xn_i = jnp.where(row8 == 7, xi[pl.ds(nb, 8), cols][0:1], pltpu.roll(xi[pl.ds(xs, 8), cols], 7, 0))
            else:
                nb = pl.multiple_of(start, 8)
                xn_r = jnp.where(row8 == 0, xr[pl.ds(nb, 8), cols][7:8], pltpu.roll(xr[pl.ds(xs, 8), cols], 1, 0))
                xn_i = jnp.where(row8 == 0, xi[pl.ds(nb, 8), cols][7:8], pltpu.roll(xi[pl.ds(xs, 8), cols], 1, 0))
            return g_re * xn_r + g_im * xn_i, g_im * xn_r - g_re * xn_i

        _scan_rows(gr, gi, 0, t, coefa_ref, carry_g, not reverse, tile_fn=tile_fn, acc_refs=(da_ref.at[0], da_ref.at[1]))

        for j, (ch, stt) in enumerate(blocks):
            g_re_b, g_im_b = gr[:, stt].astype(BF16), gi[:, stt].astype(BF16)
            du_ref[:, ch] = _dot(g_re_b, bret_ref[j]) + _dot(g_im_b, bimt_ref[j])
            dbre_ref[j] += _dot_tn(ub[:, ch], g_re_b)
            dbim_ref[j] += _dot_tn(ub[:, ch], g_im_b)
            dcre_ref[j] += _dot_tn(dyb[:, ch], xr[8:t + 8, stt].astype(BF16))
            dcim_ref[j] -= _dot_tn(dyb[:, ch], xi[8:t + 8, stt].astype(BF16))

    dense = jax.ShapeDtypeStruct((SSM_BLOCKS, BLK_CH, BLK_ST), F32)
    wb, wc = _whole((SSM_BLOCKS, BLK_CH, BLK_ST)), _whole((SSM_BLOCKS, BLK_ST, BLK_CH))
    acc = _out_whole((SSM_BLOCKS, BLK_CH, BLK_ST))
    return pl.pallas_call(
        body, name="ssm_bwd_rev" if reverse else "ssm_bwd", grid=(n,),
        in_specs=[pl.BlockSpec((t, D_SSM), lambda i: (order(i), P_U[0] // D_SSM)),
                  pl.BlockSpec((t, D_SSM), lambda i: (order(i), 0)),
                  pl.BlockSpec((None, 2, N_STATE), lambda i: (order(i), 0, 0)),
                  _whole((8, 8, N_STATE)), _whole((8, 8, N_STATE)), wb, wb, wc, wc, wb, wb],
        out_specs=[pl.BlockSpec((t, D_SSM), lambda i: (order(i), 0)), acc, acc, acc, acc, _out_whole((2, 8, N_STATE))],
        out_shape=[jax.ShapeDtypeStruct((lp, D_SSM), F32), dense, dense, dense, dense,
                   jax.ShapeDtypeStruct((2, 8, N_STATE), F32)],
        scratch_shapes=[pltpu.VMEM((t + 16, N_STATE), F32), pltpu.VMEM((t + 16, N_STATE), F32),
                        pltpu.VMEM((t, N_STATE), F32), pltpu.VMEM((t, N_STATE), F32),
                        pltpu.VMEM((8, N_STATE), F32), pltpu.VMEM((8, N_STATE), F32)],
        compiler_params=_cparams(("arbitrary",)),
    )(proj, dyp, states, coef, coef_adj, b_re, b_im, b_re_t, b_im_t, c_re_t, c_im_neg_t)


def _flash_bwd(q, k, v, d_o, lse_row, delta_row):
    lp = q.shape[0]
    tq = tk = _row_tile(lp)
    nq = lp // tq

    def body(k_ref, v_ref, q_ref, do_ref, lse_ref, delta_ref, dk_ref, dv_ref, dq_ref):
        @pl.when(pl.program_id(1) == 0)
        def _():
            dq_ref[...] = jnp.zeros_like(dq_ref)

        kt = k_ref[...]
        vt = v_ref[...]

        def step(i, carry):
            dk, dv = carry
            qs = pl.multiple_of(i * tq, tq)
            qt = q_ref[pl.ds(qs, tq), :]
            dot_ = do_ref[pl.ds(qs, tq), :]
            st = _dot_nt(kt, qt) * SCALE
            pt = jnp.exp(st - lse_ref[:, pl.ds(qs, tq)])
            dpt = _dot_nt(vt, dot_)
            dst = (pt * (dpt - delta_ref[:, pl.ds(qs, tq)]) * SCALE).astype(BF16)
            dv = dv + _dot(pt.astype(BF16), dot_)
            dk = dk + _dot(dst, qt)
            dq_ref[pl.ds(qs, tq), :] += _dot_tn(dst, kt)
            return dk, dv

        zero = jnp.zeros((tk, HEAD_PAD), F32)
        dk, dv = lax.fori_loop(0, nq, step, (zero, zero))
        dk_ref[...] = dk
        dv_ref[...] = dv

    tile = pl.BlockSpec((tk, HEAD_PAD), lambda hd, j: (j, hd))
    head = pl.BlockSpec((lp, HEAD_PAD), lambda hd, j: (0, hd))
    rowv = pl.BlockSpec((None, 1, lp), lambda hd, j: (hd, 0, 0))
    out = jax.ShapeDtypeStruct((lp, D_EXP), F32)
    return pl.pallas_call(
        body, name="flash_bwd", grid=(HEADS, lp // tk),
        in_specs=[tile, tile, head, head, rowv, rowv],
        out_specs=[tile, tile, head],
        out_shape=[out, out, out],
        compiler_params=_cparams(("parallel", "arbitrary")),
    )(k, v, q, d_o, lse_row, delta_row)


def _attn_prep_bwd(dq, dk, dv, proj, q_norm_w, kv_norm_w, wq_pt, wk_pt, wv_pt, cos, sina, sinb):
    lp = proj.shape[0]
    tm = _row_tile(lp)

    def body(dq_ref, dk_ref, dv_ref, ql_ref, kvl_ref, qw_ref, kw_ref, wqt_ref, wkt_ref, wvt_ref, cos_ref, sa_ref, sb_ref,
             dql_ref, dkvl_ref, dkr_ref, dwq_ref, dwk_ref, dwv_ref, vec_ref):
        @pl.when(pl.program_id(0) == 0)
        def _():
            dwq_ref[...] = jnp.zeros_like(dwq_ref)
            dwk_ref[...] = jnp.zeros_like(dwk_ref)
            dwv_ref[...] = jnp.zeros_like(dwv_ref)
            vec_ref[...] = jnp.zeros_like(vec_ref)

        cos_t, sa_t, sb_t = cos_ref[...], sa_ref[...], sb_ref[...]
        dkp = dk_ref[...]
        dqp = jnp.concatenate(
            [_rope_transpose(dq_ref[:, hd * HEAD_PAD:(hd + 1) * HEAD_PAD], cos_t, sa_t, sb_t) for hd in range(HEADS)], axis=-1)
        dkr = dkp[:, 0:HEAD_PAD]
        for hd in range(1, HEADS):
            dkr = dkr + dkp[:, hd * HEAD_PAD:(hd + 1) * HEAD_PAD]
        dkr_ref[...] = _rope_transpose(dkr, cos_t, sa_t, sb_t)

        qn, r_q = _rms_fwd(ql_ref[...], qw_ref[...], Q_LORA)
        kvn, r_kv = _rms_fwd(kvl_ref[...], kw_ref[...], KV_LORA)
        dqp_b, dkp_b, dv_b = dqp.astype(BF16), dkp.astype(BF16), dv_ref[...].astype(BF16)
        dqn = _dot(dqp_b, wqt_ref[...])
        dkvn = _dot(dkp_b, wkt_ref[...]) + _dot(dv_b, wvt_ref[...])
        dwq_ref[...] += _dot_tn(qn.astype(BF16), dqp_b)
        dwk_ref[...] += _dot_tn(kvn.astype(BF16), dkp_b)
        dwv_ref[...] += _dot_tn(kvn.astype(BF16), dv_b)
        dql, d_qw = _rms_bwd(ql_ref[...], r_q, qw_ref[...], dqn, Q_LORA)
        dkvl, d_kw = _rms_bwd(kvl_ref[...], r_kv, kw_ref[...], dkvn, KV_LORA)
        dql_ref[...] = dql
        dkvl_ref[...] = dkvl
        vec_ref[0:1, :] += d_qw
        vec_ref[1:2, 0:KV_LORA] += d_kw

    tab = _rows(tm, (0, HEAD_PAD))
    full = _rows(tm, (0, D_EXP))
    return pl.pallas_call(
        body, name="attn_prep_bwd", grid=(lp // tm,),
        in_specs=[full, full, full, _rows(tm, P_QLAT), _rows(tm, P_KVLAT), _whole((1, Q_LORA)), _whole((1, KV_LORA)),
                  _whole((D_EXP, Q_LORA)), _whole((D_EXP, KV_LORA)), _whole((D_EXP, KV_LORA)), tab, tab, tab],
        out_specs=[_rows(tm, (0, Q_LORA)), _rows(tm, (0, KV_LORA)), _rows(tm, (0, HEAD_PAD)),
                   _out_whole((Q_LORA, D_EXP)), _out_whole((KV_LORA, D_EXP)), _out_whole((KV_LORA, D_EXP)),
                   _out_whole((8, Q_LORA))],
        out_shape=[jax.ShapeDtypeStruct((lp, Q_LORA), F32), jax.ShapeDtypeStruct((lp, KV_LORA), F32),
                   jax.ShapeDtypeStruct((lp, HEAD_PAD), F32), jax.ShapeDtypeStruct((Q_LORA, D_EXP), F32),
                   jax.ShapeDtypeStruct((KV_LORA, D_EXP), F32), jax.ShapeDtypeStruct((KV_LORA, D_EXP), F32),
                   jax.ShapeDtypeStruct((8, Q_LORA), F32)],
        compiler_params=_cparams(("arbitrary",)),
    )(dq, dk, dv, proj, proj, q_norm_w, kv_norm_w, wq_pt, wk_pt, wv_pt, cos, sina, sinb)


def _in_proj_bwd(h, pre_w, dres, dga, du0, du1, dyp, ssm_d, dsg, dql, dkvl, dkr, w_in_pt):
    lp = h.shape[0]
    tm = 128
    pieces = (P_GATE_A, P_U, P_GATE_S, P_QLAT, P_KVLAT, P_KROPE)

    def body(h_ref, w_ref, dres_ref, dga_ref, du0_ref, du1_ref, dyp_ref, d_ref, dsg_ref, dql_ref, dkvl_ref, dkr_ref, wt_ref,
             dh_ref, dw_ref, vec_ref):
        @pl.when(pl.program_id(0) == 0)
        def _():
            dw_ref[...] = jnp.zeros_like(dw_ref)
            vec_ref[...] = jnp.zeros_like(vec_ref)

        hv = h_ref[...]
        xn, r = _rms_fwd(hv, w_ref[...], D_MODEL)
        xn_b = xn.astype(BF16)
        du = du0_ref[...] + du1_ref[...] + dyp_ref[...] * d_ref[...]
        grads = (dga_ref[...], du, dsg_ref[...], dql_ref[...], dkvl_ref[...], dkr_ref[...])
        dxn = jnp.zeros((tm, D_MODEL), F32)
        for (off, width), g in zip(pieces, grads):
            g_b = g.astype(BF16)
            dxn = dxn + _dot(g_b, wt_ref[off:off + width, :])
            dw_ref[:, off:off + width] += _dot_tn(xn_b, g_b)
        dx, d_w = _rms_bwd(hv, r, w_ref[...], dxn, D_MODEL)
        dh_ref[...] = dres_ref[...] + dx
        vec_ref[0:1, :] += d_w

    full = _rows(tm, (0, D_MODEL))
    half = _rows(tm, (0, D_SSM))
    return pl.pallas_call(
        body, name="in_proj_bwd", grid=(lp // tm,),
        in_specs=[full, _whole((1, D_MODEL)), full, full, half, half, half, _whole((1, D_SSM)), half,
                  _rows(tm, (0, Q_LORA)), _rows(tm, (0, KV_LORA)), _rows(tm, (0, HEAD_PAD)), _whole((D_PROJ, D_MODEL))],
        out_specs=[full, _out_whole((D_MODEL, D_PROJ)), _out_whole((8, D_MODEL))],
        out_shape=[jax.ShapeDtypeStruct((lp, D_MODEL), F32), jax.ShapeDtypeStruct((D_MODEL, D_PROJ), F32),
                   jax.ShapeDtypeStruct((8, D_MODEL), F32)],
        compiler_params=_cparams(("arbitrary",)),
    )(h, pre_w, dres, dga, du0, du1, dyp, ssm_d, dsg, dql, dkvl, dkr, w_in_pt)


def _other_chips(x, y):
    return [(1 - x, y), (x, 1 - y), (1 - x, 1 - y)]


def _gather_weights(w_bf16, meta):
    any_spec = pl.BlockSpec(memory_space=pl.ANY)

    def body(w_ref, m_ref, wout_ref, mout_ref, send_sems, recv_sems, local_sems):
        x, y, c = lax.axis_index("x"), lax.axis_index("y"), lax.axis_index("c")
        me = 2 * x + y
        own = [pltpu.make_async_copy(w_ref, wout_ref.at[me], local_sems.at[0]),
               pltpu.make_async_copy(m_ref, mout_ref.at[me], local_sems.at[1])]
        for cp in own:
            cp.start()
        sends = []
        for j, (tx, ty) in enumerate(_other_chips(x, y)):
            for n, (src, dst) in enumerate(((w_ref, wout_ref), (m_ref, mout_ref))):
                sends.append(pltpu.make_async_remote_copy(
                    src_ref=src, dst_ref=dst.at[me], send_sem=send_sems.at[2 * j + n], recv_sem=recv_sems.at[2 * j + n],
                    device_id=(tx, ty, c), device_id_type=MESH))
        for cp in sends:
            cp.start()
        for j, (tx, ty) in enumerate(_other_chips(x, y)):
            for n, (src, dst) in enumerate(((w_ref, wout_ref), (m_ref, mout_ref))):
                pltpu.make_async_remote_copy(
                    src_ref=src, dst_ref=dst.at[2 * tx + ty], send_sem=send_sems.at[2 * j + n],
                    recv_sem=recv_sems.at[2 * j + n], device_id=(tx, ty, c), device_id_type=MESH).wait_recv()
        for cp in sends:
            cp.wait_send()
        for cp in own:
            cp.wait()

    return pl.pallas_call(
        body, name="gather_weights",
        in_specs=[any_spec, any_spec], out_specs=[any_spec, any_spec],
        out_shape=[jax.ShapeDtypeStruct((4,) + w_bf16.shape, w_bf16.dtype), jax.ShapeDtypeStruct((4,) + meta.shape, meta.dtype)],
        scratch_shapes=[pltpu.SemaphoreType.DMA((6,)), pltpu.SemaphoreType.DMA((6,)), pltpu.SemaphoreType.DMA((2,))],
    )(w_bf16, meta)


def _swap_sibling(g):
    any_spec = pl.BlockSpec(memory_space=pl.ANY)

    def body(g_ref, out_ref, send_sem, recv_sem):
        x, y, c = lax.axis_index("x"), lax.axis_index("y"), lax.axis_index("c")
        cp = pltpu.make_async_remote_copy(src_ref=g_ref, dst_ref=out_ref, send_sem=send_sem, recv_sem=recv_sem,
                                          device_id=(x, y, 1 - c), device_id_type=MESH)
        cp.start()
        cp.wait()

    return pl.pallas_call(
        body, name="swap_sibling", in_specs=[any_spec], out_specs=any_spec,
        out_shape=jax.ShapeDtypeStruct(g.shape, g.dtype),
        scratch_shapes=[pltpu.SemaphoreType.DMA(()), pltpu.SemaphoreType.DMA(())],
    )(g)


def _pair_sum(a, b):
    rows = a.shape[0]
    tm = _pick_tile(rows, 1024)

    def body(a_ref, b_ref, o_ref):
        o_ref[...] = a_ref[...] + b_ref[...]

    spec = pl.BlockSpec((tm, 1024), lambda i: (i, 0))
    return pl.pallas_call(body, name="pair_sum", grid=(rows // tm,), in_specs=[spec, spec], out_specs=spec,
                          out_shape=jax.ShapeDtypeStruct(a.shape, F32), compiler_params=_cparams(("parallel",)))(a, b)


def _scatter_chips(s, rs, rsm):
    any_spec = pl.BlockSpec(memory_space=pl.ANY)

    def body(s_ref, out_ref, send_sems, recv_sems, local_sems):
        x, y, c = lax.axis_index("x"), lax.axis_index("y"), lax.axis_index("c")
        me = 2 * x + y
        small = s_ref.at[pl.ds(4 * rs, rsm)]

        def pieces(target):
            return ((s_ref.at[pl.ds(pl.multiple_of(target * rs, 8), rs)], pl.ds(0, rs)), (small, pl.ds(rs, rsm)))

        own = [pltpu.make_async_copy(src, out_ref.at[me, rows], local_sems.at[n]) for n, (src, rows) in enumerate(pieces(me))]
        for cp in own:
            cp.start()
        sends = []
        for j, (tx, ty) in enumerate(_other_chips(x, y)):
            for n, (src, rows) in enumerate(pieces(2 * tx + ty)):
                sends.append(pltpu.make_async_remote_copy(
                    src_ref=src, dst_ref=out_ref.at[me, rows], send_sem=send_sems.at[2 * j + n],
                    recv_sem=recv_sems.at[2 * j + n], device_id=(tx, ty, c), device_id_type=MESH))
        for cp in sends:
            cp.start()
        for j, (tx, ty) in enumerate(_other_chips(x, y)):
            for n, (src, rows) in enumerate(pieces(me)):
                pltpu.make_async_remote_copy(
                    src_ref=src, dst_ref=out_ref.at[2 * tx + ty, rows], send_sem=send_sems.at[2 * j + n],
                    recv_sem=recv_sems.at[2 * j + n], device_id=(tx, ty, c), device_id_type=MESH).wait_recv()
        for cp in sends:
            cp.wait_send()
        for cp in own:
            cp.wait()

    return pl.pallas_call(
        body, name="scatter_chips", in_specs=[any_spec], out_specs=any_spec,
        out_shape=jax.ShapeDtypeStruct((4, rs + rsm, 1024), F32),
        scratch_shapes=[pltpu.SemaphoreType.DMA((6,)), pltpu.SemaphoreType.DMA((6,)), pltpu.SemaphoreType.DMA((2,))],
    )(s)


def _adamw(parts, w, m, v):
    rows = w.shape[0]
    tm = _pick_tile(rows, 256)
    c1 = 1.0 / (1.0 - ADAM_B1 ** ADAM_STEP)
    c2 = 1.0 / (1.0 - ADAM_B2 ** ADAM_STEP)

    def body(p_ref, w_ref, m_ref, v_ref, g_ref, d_ref, nm_ref, nv_ref):
        g = ((p_ref[0] + p_ref[1]) + p_ref[2]) + p_ref[3]
        nm = ADAM_B1 * m_ref[...] + (1.0 - ADAM_B1) * g
        nv = ADAM_B2 * v_ref[...] + (1.0 - ADAM_B2) * (g * g)
        g_ref[...] = g
        nm_ref[...] = nm
        nv_ref[...] = nv
        d_ref[...] = -ADAM_LR * ((nm * c1) / (jnp.sqrt(nv * c2) + ADAM_EPS) + ADAM_WD * w_ref[...])

    spec = pl.BlockSpec((tm, 1024), lambda i: (i, 0))
    out = jax.ShapeDtypeStruct(w.shape, F32)
    return pl.pallas_call(
        body, name="adamw", grid=(rows // tm,),
        in_specs=[pl.BlockSpec((4, tm, 1024), lambda i: (0, i, 0)), spec, spec, spec],
        out_specs=[spec] * 4, out_shape=[out] * 4, compiler_params=_cparams(("parallel",)),
    )(parts, w, m, v)


def _expand_heads(a, axis, per_head):
    a = jnp.moveaxis(a, axis, -1)
    lead = a.shape[:-1]
    a = a.reshape(lead + (HEADS, per_head))
    a = jnp.pad(a, [(0, 0)] * len(lead) + [(0, 0), (0, HEAD_PAD - per_head)])
    return jnp.moveaxis(a.reshape(lead + (D_EXP,)), -1, axis)


def _compact_heads(a, axis, start, size):
    a = jnp.moveaxis(a, axis, -1)
    lead = a.shape[:-1]
    a = a.reshape(lead + (HEADS, HEAD_PAD))[..., start:start + size]
    return jnp.moveaxis(a.reshape(lead + (HEADS * size,)), -1, axis)


def _block_diag(w):
    g, a, b = w.shape
    per = g // SSM_BLOCKS
    eye = jnp.eye(per, dtype=w.dtype)
    return jnp.einsum("jgab,gk->jgakb", w.reshape(SSM_BLOCKS, per, a, b), eye).reshape(SSM_BLOCKS, per * a, per * b)


def _block_diag_extract(dense, a, b):
    per = N_GROUPS // SSM_BLOCKS
    d5 = dense.reshape(SSM_BLOCKS, per, a, per, b)
    return jnp.einsum("jgakb,gk->jgab", d5, jnp.eye(per, dtype=dense.dtype)).reshape(N_GROUPS, a, b)


def _discretise(a_re, a_im, log_dt, b_re, b_im):
    dt = jnp.exp(log_dt)[:, None]
    mag = jnp.exp(a_re * dt)
    abar_re = mag * jnp.cos(a_im * dt)
    abar_im = mag * jnp.sin(a_im * dt)
    num_re = abar_re - 1.0
    num_im = abar_im
    den = a_re * a_re + a_im * a_im
    coef_re = (num_re * a_re + num_im * a_im) / den
    coef_im = (num_im * a_re - num_re * a_im) / den
    bbar_re = coef_re[..., None] * b_re - coef_im[..., None] * b_im
    bbar_im = coef_re[..., None] * b_im + coef_im[..., None] * b_re
    return abar_re, abar_im, bbar_re, bbar_im


def _scan_coef(ar, ai, reverse):
    ar, ai = ar.reshape(1, N_STATE), ai.reshape(1, N_STATE)
    pows = [(ar, ai)]
    for _ in range(7):
        pr, pi_ = pows[-1]
        pows.append((pr * ar - pi_ * ai, pr * ai + pi_ * ar))
    row = jnp.arange(8)[:, None]
    out = []
    for k in (1, 2, 4):
        keep = (row < 8 - k) if reverse else (row >= k)
        out += [jnp.where(keep, pows[k - 1][0], 0.0), jnp.where(keep, pows[k - 1][1], 0.0)]
    order = list(range(7, -1, -1)) if reverse else list(range(8))
    out += [jnp.concatenate([pows[k][0] for k in order], axis=0), jnp.concatenate([pows[k][1] for k in order], axis=0)]
    return jnp.stack(out).astype(F32)


def _flat_rows(a, rows):
    flat = a.reshape(-1)
    return jnp.pad(flat, (0, rows * 1024 - flat.shape[0])).reshape(rows, 1024)


def _pack(named, order):
    rows = [-(-math.prod(named[n].shape) // 1024) for n in order]
    total = -(-sum(rows) // 8) * 8
    parts = [_flat_rows(named[n], r) for n, r in zip(order, rows)]
    if total > sum(rows):
        parts.append(jnp.zeros((total - sum(rows), 1024), parts[0].dtype))
    return jnp.concatenate(parts, axis=0)


def _unpack(packed, shapes, order):
    out, at = {}, 0
    for n in order:
        size = math.prod(shapes[n])
        rows = -(-size // 1024)
        out[n] = packed[at:at + rows].reshape(-1)[:size].reshape(shapes[n])
        at += rows
    return out


def _shard_cols(a, k):
    w = a.shape[-1] // 4
    return a[..., k * w:(k + 1) * w]


def kernel(x, meta_tokens, pre_norm_w, post_norm_w, w_in, q_norm_w, w_q_up, kv_norm_w, w_kv_up, attn_out_norm_w, ssm_a_re, ssm_a_im, ssm_log_dt, ssm_b_re, ssm_b_im, ssm_c_re, ssm_c_im, ssm_d, w_glu, b_glu, ssm_out_norm_w, w_out, loss_target, m_meta_tokens, m_pre_norm_w, m_post_norm_w, m_w_in, m_q_norm_w, m_w_q_up, m_kv_norm_w, m_w_kv_up, m_attn_out_norm_w, m_ssm_a_re, m_ssm_a_im, m_ssm_log_dt, m_ssm_b_re, m_ssm_b_im, m_ssm_c_re, m_ssm_c_im, m_ssm_d, m_w_glu, m_b_glu, m_ssm_out_norm_w, m_w_out, v_meta_tokens, v_pre_norm_w, v_post_norm_w, v_w_in, v_q_norm_w, v_w_q_up, v_kv_norm_w, v_w_kv_up, v_attn_out_norm_w, v_ssm_a_re, v_ssm_a_im, v_ssm_log_dt, v_ssm_b_re, v_ssm_b_im, v_ssm_c_re, v_ssm_c_im, v_ssm_d, v_w_glu, v_b_glu, v_ssm_out_norm_w, v_w_out):
    local = dict(meta_tokens=meta_tokens, pre_norm_w=pre_norm_w, post_norm_w=post_norm_w, w_in=w_in, q_norm_w=q_norm_w,
                 w_q_up=w_q_up, kv_norm_w=kv_norm_w, w_kv_up=w_kv_up, attn_out_norm_w=attn_out_norm_w, ssm_a_re=ssm_a_re,
                 ssm_a_im=ssm_a_im, ssm_log_dt=ssm_log_dt, ssm_b_re=ssm_b_re, ssm_b_im=ssm_b_im, ssm_c_re=ssm_c_re,
                 ssm_c_im=ssm_c_im, ssm_d=ssm_d, w_glu=w_glu, b_glu=b_glu, ssm_out_norm_w=ssm_out_norm_w, w_out=w_out)
    mom_m = dict(meta_tokens=m_meta_tokens, pre_norm_w=m_pre_norm_w, post_norm_w=m_post_norm_w, w_in=m_w_in,
                 q_norm_w=m_q_norm_w, w_q_up=m_w_q_up, kv_norm_w=m_kv_norm_w, w_kv_up=m_w_kv_up,
                 attn_out_norm_w=m_attn_out_norm_w, ssm_a_re=m_ssm_a_re, ssm_a_im=m_ssm_a_im, ssm_log_dt=m_ssm_log_dt,
                 ssm_b_re=m_ssm_b_re, ssm_b_im=m_ssm_b_im, ssm_c_re=m_ssm_c_re, ssm_c_im=m_ssm_c_im, ssm_d=m_ssm_d,
                 w_glu=m_w_glu, b_glu=m_b_glu, ssm_out_norm_w=m_ssm_out_norm_w, w_out=m_w_out)
    mom_v = dict(meta_tokens=v_meta_tokens, pre_norm_w=v_pre_norm_w, post_norm_w=v_post_norm_w, w_in=v_w_in,
                 q_norm_w=v_q_norm_w, w_q_up=v_w_q_up, kv_norm_w=v_kv_norm_w, w_kv_up=v_w_kv_up,
                 attn_out_norm_w=v_attn_out_norm_w, ssm_a_re=v_ssm_a_re, ssm_a_im=v_ssm_a_im, ssm_log_dt=v_ssm_log_dt,
                 ssm_b_re=v_ssm_b_re, ssm_b_im=v_ssm_b_im, ssm_c_re=v_ssm_c_re, ssm_c_im=v_ssm_c_im, ssm_d=v_ssm_d,
                 w_glu=v_w_glu, b_glu=v_b_glu, ssm_out_norm_w=v_ssm_out_norm_w, w_out=v_w_out)
    shapes = {n: local[n].shape for n in WEIGHTS}
    mat = ("w_in", "w_q_up", "w_kv_up", "w_glu", "w_out")

    seq = x.shape[1]
    l_real = N_META + seq
    lp = -(-l_real // 640) * 640 if l_real > 1024 else -(-l_real // 128) * 128

    w_shard = _pack({n: local[n].astype(BF16) for n in mat}, mat)
    w_shard = jnp.pad(w_shard, ((0, -w_shard.shape[0] % 16), (0, 0)))
    w_all, meta_all = _gather_weights(w_shard, meta_tokens)
    mat_shapes = {n: shapes[n] for n in mat}
    per_chip = [_unpack(w_all[k], mat_shapes, mat) for k in range(4)]
    w_in_f = jnp.concatenate([p["w_in"][0] for p in per_chip], axis=1)
    w_q_f = jnp.concatenate([p["w_q_up"][0] for p in per_chip], axis=1)
    w_kv_f = jnp.concatenate([p["w_kv_up"][0] for p in per_chip], axis=1)
    w_glu_f = jnp.concatenate([p["w_glu"][0] for p in per_chip], axis=1)
    w_out_f = jnp.concatenate([p["w_out"][0] for p in per_chip], axis=0)
    meta_f = jnp.concatenate([meta_all[k] for k in range(4)], axis=1)

    o_q, o_kv, o_kr, o_ga, o_u, o_gs = 0, 256, 384, 416, 928, 1440
    krope_cols = jnp.pad(w_in_f[:, o_kr:o_ga], ((0, 0), (QK_NOPE, HEAD_PAD - QK_NOPE - QK_ROPE)))
    w_in_p = jnp.concatenate([_expand_heads(w_in_f[:, o_ga:o_u], 1, V_HEAD), w_in_f[:, o_u:o_gs], w_in_f[:, o_gs:],
                              w_in_f[:, o_q:o_kv], w_in_f[:, o_kv:o_kr], krope_cols], axis=1)
    wq_p = _expand_heads(w_q_f, 1, QK_NOPE + QK_ROPE)
    kv3 = w_kv_f.reshape(KV_LORA, HEADS, QK_NOPE + V_HEAD)
    wk_p = _expand_heads(kv3[:, :, :QK_NOPE].reshape(KV_LORA, HEADS * QK_NOPE), 1, QK_NOPE)
    wv_p = _expand_heads(kv3[:, :, QK_NOPE:].reshape(KV_LORA, HEADS * V_HEAD), 1, V_HEAD)
    w_out_a = _expand_heads(w_out_f[:D_ATTN], 0, V_HEAD)
    w_out_s = w_out_f[D_ATTN:]
    attn_norm_e = _expand_heads(attn_out_norm_w, 1, V_HEAD)

    pos = jnp.arange(lp, dtype=jnp.int32)
    half = QK_ROPE // 2
    inv = ROPE_THETA ** (-jnp.arange(half, dtype=F32) / half)
    ang = pos.astype(F32)[:, None] * inv[None, :]
    cos16, sin16 = jnp.cos(ang), jnp.sin(ang)
    ones, zeros = jnp.ones((lp, QK_NOPE), F32), jnp.zeros((lp, QK_NOPE), F32)
    tail1, tail0 = jnp.ones((lp, HEAD_PAD - MASK_LANE), F32), jnp.zeros((lp, HEAD_PAD - MASK_LANE), F32)
    z16 = jnp.zeros((lp, half), F32)
    cos = jnp.concatenate([ones, cos16, cos16, tail1], axis=1)
    sina = jnp.concatenate([zeros, z16, sin16, tail0], axis=1)
    sinb = jnp.concatenate([zeros, -sin16, z16, tail0], axis=1)

    disc_in = (ssm_a_re[0], ssm_a_im[0], ssm_log_dt[0], ssm_b_re[0], ssm_b_im[0])
    disc = lambda a_re, a_im, ldt, b_re, b_im: jax.vmap(_discretise)(a_re, a_im, ldt, b_re, b_im)
    (abar_re, abar_im, bbar_re, bbar_im), disc_vjp = jax.vjp(disc, *disc_in)
    ssm = []
    for d in range(2):
        rev = d == 1
        b_re_bd = _block_diag(jnp.swapaxes(bbar_re[d], 1, 2)).astype(BF16)
        b_im_bd = _block_diag(jnp.swapaxes(bbar_im[d], 1, 2)).astype(BF16)
        c_re_bd = _block_diag(jnp.swapaxes(ssm_c_re[0, d], 1, 2)).astype(BF16)
        c_im_bd = _block_diag(jnp.swapaxes(-ssm_c_im[0, d], 1, 2)).astype(BF16)
        ssm.append(dict(rev=rev, coef=_scan_coef(abar_re[d], abar_im[d], rev),
                        coef_adj=_scan_coef(abar_re[d], -abar_im[d], not rev),
                        b_re=b_re_bd, b_im=b_im_bd, c_re=c_re_bd, c_im=c_im_bd))

    h = jnp.concatenate([meta_f, x[0], jnp.zeros((lp - l_real, D_MODEL), F32)], axis=0)
    tgt = jnp.concatenate([jnp.zeros((N_META, D_MODEL), F32), loss_target[0], jnp.zeros((lp - l_real, D_MODEL), F32)], axis=0)
    proj = _in_proj_fwd(h, pre_norm_w, w_in_p)
    q, k, v = _attn_prep_fwd(proj, q_norm_w, kv_norm_w, wq_p, wk_p, wv_p, cos, sina, sinb, l_real)
    o_exp, lse = _flash_fwd(q, k, v)
    ys, states = [], []
    for s in ssm:
        y_d, st_d = _ssm_fwd(proj, s["coef"], s["b_re"], s["b_im"], s["c_re"], s["c_im"], s["rev"])
        ys.append(y_d)
        states.append(st_d)

    (d_o, delta, dga, dyp, dsg, dres, dwoa, dwos, dwglu, vec_mid) = _mid(
        h, tgt, o_exp, proj, ys[0], ys[1], ssm_d, w_glu_f, w_glu_f.T, b_glu, ssm_out_norm_w, attn_norm_e, w_out_a, w_out_s,
        w_out_a.T, w_out_s.T, post_norm_w, l_real)
    dus, dssm = [], []
    tr = lambda a: jnp.swapaxes(a, 1, 2)
    for s, st_d in zip(ssm, states):
        du_d, dbre, dbim, dcre, dcim, da = _ssm_bwd(proj, dyp, st_d, s["coef"], s["coef_adj"], s["b_re"], s["b_im"],
                                                    tr(s["b_re"]), tr(s["b_im"]), tr(s["c_re"]), tr(s["c_im"]), s["rev"])
        dus.append(du_d)
        dssm.append((dbre, dbim, dcre, dcim, da))
    dk, dv, dq = _flash_bwd(q, k, v, d_o, lse.reshape(HEADS, 1, lp), delta.T.reshape(HEADS, 1, lp))
    dql, dkvl, dkr, dwq_p, dwk_p, dwv_p, vec_prep = _attn_prep_bwd(
        dq, dk, dv, proj, q_norm_w, kv_norm_w, wq_p.T, wk_p.T, wv_p.T, cos, sina, sinb)
    dh, dwin_p, vec_in = _in_proj_bwd(h, pre_norm_w, dres, dga, dus[0], dus[1], dyp, ssm_d, dsg, dql, dkvl, dkr, w_in_p.T)

    grads = {}
    grads["w_in"] = jnp.concatenate([
        dwin_p[:, P_QLAT[0]:P_QLAT[0] + 256], dwin_p[:, P_KVLAT[0]:P_KVLAT[0] + 128],
        dwin_p[:, P_KROPE[0] + QK_NOPE:P_KROPE[0] + QK_NOPE + QK_ROPE], _compact_heads(dwin_p[:, 0:D_EXP], 1, 0, V_HEAD),
        dwin_p[:, P_U[0]:P_U[0] + 512], dwin_p[:, P_GATE_S[0]:P_GATE_S[0] + 512]], axis=1)[None]
    grads["w_q_up"] = _compact_heads(dwq_p, 1, 0, QK_NOPE + QK_ROPE)[None]
    dwk3 = _compact_heads(dwk_p, 1, 0, QK_NOPE).reshape(KV_LORA, HEADS, QK_NOPE)
    dwv3 = _compact_heads(dwv_p, 1, 0, V_HEAD).reshape(KV_LORA, HEADS, V_HEAD)
    grads["w_kv_up"] = jnp.concatenate([dwk3, dwv3], axis=2).reshape(1, KV_LORA, HEADS * (QK_NOPE + V_HEAD))
    grads["w_glu"] = dwglu[None]
    grads["w_out"] = jnp.concatenate([_compact_heads(dwoa, 0, 0, V_HEAD), dwos], axis=0)[None]
    grads["meta_tokens"] = dh[:N_META]
    grads["pre_norm_w"] = vec_in[0:1]
    grads["post_norm_w"] = vec_mid[0:1]
    grads["q_norm_w"] = vec_prep[0:1]
    grads["kv_norm_w"] = vec_prep[1:2, :KV_LORA]
    grads["attn_out_norm_w"] = _compact_heads(vec_mid[1:2], 1, 0, V_HEAD)
    grads["ssm_out_norm_w"] = vec_mid[2:3, :D_SSM]
    grads["ssm_d"] = vec_mid[3:4, :D_SSM]
    grads["b_glu"] = vec_mid[4:5]
    d_abar_re = jnp.stack([dssm[d][4][0].sum(axis=0).reshape(N_GROUPS, SSM_STATE) for d in range(2)])
    d_abar_im = jnp.stack([dssm[d][4][1].sum(axis=0).reshape(N_GROUPS, SSM_STATE) for d in range(2)])
    d_bbar_re = jnp.stack([jnp.swapaxes(_block_diag_extract(dssm[d][0], SSM_GROUP, SSM_STATE), 1, 2) for d in range(2)])
    d_bbar_im = jnp.stack([jnp.swapaxes(_block_diag_extract(dssm[d][1], SSM_GROUP, SSM_STATE), 1, 2) for d in range(2)])
    da_re, da_im, dlog_dt, db_re, db_im = disc_vjp((d_abar_re, d_abar_im, d_bbar_re, d_bbar_im))
    grads["ssm_a_re"], grads["ssm_a_im"], grads["ssm_log_dt"] = da_re[None], da_im[None], dlog_dt[None]
    grads["ssm_b_re"], grads["ssm_b_im"] = db_re[None], db_im[None]
    grads["ssm_c_re"] = jnp.stack([_block_diag_extract(dssm[d][2], SSM_GROUP, SSM_STATE) for d in range(2)])[None]
    grads["ssm_c_im"] = jnp.stack([_block_diag_extract(dssm[d][3], SSM_GROUP, SSM_STATE) for d in range(2)])[None]

    def shard_of(n, a, kk):
        return a[:, kk * 256:(kk + 1) * 256] if n == "w_out" else _shard_cols(a, kk)

    slices = [_pack({n: shard_of(n, grads[n], kk) for n in BIG}, BIG) for kk in range(4)]
    small = _pack({n: grads[n] for n in SMALL}, SMALL)
    rs, rsm = slices[0].shape[0], small.shape[0]
    g_pack = jnp.concatenate(slices + [small], axis=0)
    g_pair = _pair_sum(g_pack, _swap_sibling(g_pack))
    parts = _scatter_chips(g_pair, rs, rsm)

    order = BIG + SMALL
    big_shapes = {n: shapes[n] for n in BIG}
    small_shapes = {n: shapes[n] for n in SMALL}

    def pack_state(named):
        return jnp.concatenate([_pack({n: named[n] for n in BIG}, BIG), _pack({n: named[n] for n in SMALL}, SMALL)], axis=0)

    g_out, d_out, m_out, v_out = _adamw(parts, pack_state(local), pack_state(mom_m), pack_state(mom_v))

    def unpack_state(p):
        out = _unpack(p[:rs], big_shapes, BIG)
        out.update(_unpack(p[rs:], small_shapes, SMALL))
        return out

    g_fin, d_fin, m_fin, v_fin = unpack_state(g_out), unpack_state(d_out), unpack_state(m_out), unpack_state(v_out)
    loss = lax.psum(vec_mid[5, 0], ("x", "y", "c"))
    grad_x = dh[N_META:l_real][None]
    return (loss, grad_x, *[g_fin[n] for n in WEIGHTS], *[d_fin[n] for n in WEIGHTS], *[m_fin[n] for n in WEIGHTS],
            *[v_fin[n] for n in WEIGHTS])
```

```python
import functools
import math

import jax
import jax.numpy as jnp
from jax import lax
from jax.experimental import pallas as pl
from jax.experimental.pallas import tpu as pltpu

F32 = jnp.float32
BF16 = jnp.bfloat16
MESH = pl.DeviceIdType.MESH

D_MODEL = 1024
N_META = 16
EPS = 1e-6
HEADS = 8
QK_NOPE = 64
QK_ROPE = 32
V_HEAD = 64
Q_LORA = 256
KV_LORA = 128
D_ATTN = 512
D_SSM = 512
SSM_GROUP = 16
N_GROUPS = 32
SSM_STATE = 64
N_STATE = N_GROUPS * SSM_STATE
ROPE_THETA = 10000.0
HEAD_PAD = 128
D_EXP = HEADS * HEAD_PAD
MASK_LANE = QK_NOPE + QK_ROPE
NEG_BIG = -1e30
SCALE = 1.0 / math.sqrt(QK_NOPE + QK_ROPE)
LOG2E = math.log2(math.e)
SCALE2 = SCALE * LOG2E
QBLK = 256
SCAN_COLS = 512
SSM_BLOCKS = 4
BLK_CH = D_SSM // SSM_BLOCKS
BLK_ST = N_STATE // SSM_BLOCKS

P_GATE_A = (0, 1024)
P_U = (1024, 512)
P_GATE_S = (1536, 512)
P_QLAT = (2048, 256)
P_KVLAT = (2304, 128)
P_KROPE = (2432, 128)
D_PROJ = 2560

ADAM_LR = 0.001
ADAM_B1 = 0.9
ADAM_B2 = 0.999
ADAM_EPS = 1e-08
ADAM_WD = 0.01
ADAM_STEP = 10

VMEM_LIMIT = 60 * 1024 * 1024

BIG = ("w_in", "w_q_up", "w_kv_up", "w_glu", "w_out", "meta_tokens")
SMALL = ("pre_norm_w", "post_norm_w", "q_norm_w", "kv_norm_w", "attn_out_norm_w", "ssm_a_re", "ssm_a_im",
         "ssm_log_dt", "ssm_b_re", "ssm_b_im", "ssm_c_re", "ssm_c_im", "ssm_d", "b_glu", "ssm_out_norm_w")
WEIGHTS = ("meta_tokens", "pre_norm_w", "post_norm_w", "w_in", "q_norm_w", "w_q_up", "kv_norm_w", "w_kv_up",
           "attn_out_norm_w", "ssm_a_re", "ssm_a_im", "ssm_log_dt", "ssm_b_re", "ssm_b_im", "ssm_c_re", "ssm_c_im",
           "ssm_d", "w_glu", "b_glu", "ssm_out_norm_w", "w_out")


def _cparams(sem=None):
    return pltpu.CompilerParams(dimension_semantics=sem, vmem_limit_bytes=VMEM_LIMIT)


def _dot(a, b):
    return jnp.dot(a, b, preferred_element_type=F32)


def _dot_nt(a, b):
    return lax.dot_general(a, b, (((1,), (1,)), ((), ())), preferred_element_type=F32)


def _dot_tn(a, b):
    return lax.dot_general(a, b, (((0,), (0,)), ((), ())), preferred_element_type=F32)


def _sigmoid(x):
    return 1.0 / (1.0 + jnp.exp(-x))


def _rms_fwd(x, w, n):
    r = lax.rsqrt(jnp.sum(x * x, axis=-1, keepdims=True) * (1.0 / n) + EPS)
    return x * r * w, r


def _rms_bwd(x, r, w, dy, n):
    dyw = dy * w
    dx = r * dyw - x * (r * r * r) * (jnp.sum(dyw * x, axis=-1, keepdims=True) * (1.0 / n))
    dw = jnp.sum(dy * (x * r), axis=0, keepdims=True)
    return dx, dw


def _rope_apply(x, cos, sina, sinb):
    return x * cos + pltpu.roll(x, 16, 1) * sina + pltpu.roll(x, HEAD_PAD - 16, 1) * sinb


def _rope_transpose(g, cos, sina, sinb):
    return g * cos + pltpu.roll(g * sina, HEAD_PAD - 16, 1) + pltpu.roll(g * sinb, 16, 1)


def _row_tile(lp):
    return 640 if lp % 640 == 0 else 128


def _ssm_tile(lp):
    return 320 if lp % 320 == 0 else 128


def _rows(tm, off_width):
    off, width = off_width
    return pl.BlockSpec((tm, width), lambda i: (i, off // width))


def _whole(shape, single=True):
    nd = len(shape)
    if single:
        return pl.BlockSpec(shape, lambda *_: (0,) * nd, pipeline_mode=pl.Buffered(1))
    return pl.BlockSpec(shape, lambda *_: (0,) * nd)


def _out_whole(shape):
    return _whole(shape, single=False)


def _pick_tile(rows, cap):
    best = 8
    for t in range(8, cap + 1, 8):
        if rows % t == 0:
            best = t
    return best


def _in_proj_fwd(h, pre_w, w_in_p):
    lp = h.shape[0]
    tm = _row_tile(lp)

    def body(h_ref, w_ref, win_ref, proj_ref):
        xn, _ = _rms_fwd(h_ref[...], w_ref[...], D_MODEL)
        proj_ref[...] = _dot(xn.astype(BF16), win_ref[...])

    return pl.pallas_call(
        body, name="in_proj_fwd", grid=(lp // tm,),
        in_specs=[_rows(tm, (0, D_MODEL)), _whole((1, D_MODEL)), _whole((D_MODEL, D_PROJ))],
        out_specs=_rows(tm, (0, D_PROJ)),
        out_shape=jax.ShapeDtypeStruct((lp, D_PROJ), F32),
        compiler_params=_cparams(("parallel",)),
    )(h, pre_w, w_in_p)


def _attn_prep_fwd(proj, q_norm_w, kv_norm_w, wq_p, wk_p, wv_p, wv_t, cos, sina, sinb, l_real):
    lp = proj.shape[0]
    tm = _row_tile(lp)

    def body(ql_ref, kvl_ref, kr_ref, qw_ref, kw_ref, wq_ref, wk_ref, wv_ref, wvt_ref, cos_ref, sa_ref, sb_ref,
             q_ref, k_ref, v_ref, vt_ref):
        cos_t, sa_t, sb_t = cos_ref[...], sa_ref[...], sb_ref[...]
        qn, _ = _rms_fwd(ql_ref[...], qw_ref[...], Q_LORA)
        kvn, _ = _rms_fwd(kvl_ref[...], kw_ref[...], KV_LORA)
        kvn_b = kvn.astype(BF16)
        qp = _dot(qn.astype(BF16), wq_ref[...])
        kp = _dot(kvn_b, wk_ref[...])
        v_ref[...] = _dot(kvn_b, wv_ref[...]).astype(BF16)
        vt_ref[...] = _dot_nt(wvt_ref[...], kvn_b).astype(BF16)
        lane = lax.broadcasted_iota(jnp.int32, (tm, HEAD_PAD), 1)
        row = lax.broadcasted_iota(jnp.int32, (tm, HEAD_PAD), 0) + pl.program_id(0) * tm
        q_one = jnp.where(lane == MASK_LANE, 1.0, 0.0)
        k_add = _rope_apply(kr_ref[...], cos_t, sa_t, sb_t) + jnp.where((lane == MASK_LANE) & (row >= l_real), NEG_BIG, 0.0)
        for hd in range(HEADS):
            blk = slice(hd * HEAD_PAD, (hd + 1) * HEAD_PAD)
            q_ref[:, blk] = (_rope_apply(qp[:, blk], cos_t, sa_t, sb_t) * SCALE2 + q_one).astype(BF16)
            k_ref[:, blk] = (kp[:, blk] + k_add).astype(BF16)

    tab = _rows(tm, (0, HEAD_PAD))
    out = jax.ShapeDtypeStruct((lp, D_EXP), BF16)
    return pl.pallas_call(
        body, name="attn_prep_fwd", grid=(lp // tm,),
        in_specs=[_rows(tm, P_QLAT), _rows(tm, P_KVLAT), _rows(tm, P_KROPE), _whole((1, Q_LORA)), _whole((1, KV_LORA)),
                  _whole((Q_LORA, D_EXP)), _whole((KV_LORA, D_EXP)), _whole((KV_LORA, D_EXP)), _whole((D_ATTN, KV_LORA)),
                  tab, tab, tab],
        out_specs=[_rows(tm, (0, D_EXP))] * 3 + [pl.BlockSpec((D_ATTN, tm), lambda i: (0, i))],
        out_shape=[out, out, out, jax.ShapeDtypeStruct((D_ATTN, lp), BF16)],
        compiler_params=_cparams(("parallel",)),
    )(proj, proj, proj, q_norm_w, kv_norm_w, wq_p, wk_p, wv_p, wv_t, cos, sina, sinb)


def _flash_fwd(q, k, vt):
    lp = q.shape[0]
    tq = 1280 if lp % 1280 == 0 else 256
    tk = QBLK
    nk = lp // tk

    def body(q_ref, k_ref, vt_ref, o_ref, lse_ref, acc, m_s, l_s):
        acc[...] = jnp.zeros_like(acc)
        m_s[...] = jnp.full(m_s.shape, NEG_BIG, F32)
        l_s[...] = jnp.zeros_like(l_s)

        def step(j, _):
            ks = pl.multiple_of(j * tk, tk)
            kt = k_ref[pl.ds(ks, tk), :]
            vt_t = vt_ref[:, pl.ds(ks, tk)]
            m_old, l_old, acc_old = m_s[...], l_s[...], acc[...]
            blocks = [slice(c * QBLK, (c + 1) * QBLK) for c in range(tq // QBLK)]
            s = [_dot_nt(kt, q_ref[cols, :]) for cols in blocks]
            m_new = [jnp.maximum(m_old[:, cols], jnp.max(s_c, axis=0, keepdims=True)) for cols, s_c in zip(blocks, s)]
            p = [jnp.exp2(s_c - m_c) for s_c, m_c in zip(s, m_new)]
            pv = [_dot(vt_t, p_c.astype(BF16)) for p_c in p]
            m_new = jnp.concatenate(m_new, axis=1)
            alpha = jnp.exp2(m_old - m_new)
            l_s[...] = alpha * l_old + jnp.concatenate([jnp.sum(p_c, axis=0, keepdims=True) for p_c in p], axis=1)
            acc[...] = alpha * acc_old + jnp.concatenate(pv, axis=1)
            m_s[...] = m_new
            return 0

        lax.fori_loop(0, nk, step, 0, unroll=5 if nk % 5 == 0 else 1)
        o_t = acc[...] / l_s[...]
        o_ref[...] = jnp.concatenate([o_t, jnp.zeros_like(o_t)], axis=0).T
        lse_ref[...] = m_s[...] + jnp.log2(l_s[...])

    return pl.pallas_call(
        body, name="flash_fwd", grid=(HEADS, lp // tq),
        in_specs=[pl.BlockSpec((tq, HEAD_PAD), lambda hd, i: (i, hd)),
                  pl.BlockSpec((lp, HEAD_PAD), lambda hd, i: (0, hd)),
                  pl.BlockSpec((V_HEAD, lp), lambda hd, i: (hd, 0))],
        out_specs=[pl.BlockSpec((tq, HEAD_PAD), lambda hd, i: (i, hd)),
                   pl.BlockSpec((None, 1, tq), lambda hd, i: (hd, 0, i))],
        out_shape=[jax.ShapeDtypeStruct((lp, D_EXP), F32), jax.ShapeDtypeStruct((HEADS, 1, lp), F32)],
        scratch_shapes=[pltpu.VMEM((V_HEAD, tq), F32), pltpu.VMEM((1, tq), F32), pltpu.VMEM((1, tq), F32)],
        compiler_params=_cparams(("parallel", "parallel")),
    )(q, k, vt)


def _scan_rows(xr_ref, xi_ref, base, n_rows, coef_ref, carry_ref, reverse, tile_fn=None, acc_refs=()):
    n_tiles = n_rows // 8
    shifts = (7, 6, 4) if reverse else (1, 2, 4)
    for cg in range(N_STATE // SCAN_COLS):
        cols = slice(cg * SCAN_COLS, (cg + 1) * SCAN_COLS)
        co = [coef_ref[k, :, cols] for k in range(8)]

        def step(t, carry, cols=cols, co=co):
            cr, ci = carry[0], carry[1]
            tt = (n_tiles - 1 - t) if reverse else t
            start = pl.multiple_of(base + tt * 8, 8)
            tr = xr_ref[pl.ds(start, 8), cols]
            ti = xi_ref[pl.ds(start, 8), cols]
            for lvl in range(3):
                ar, ai = co[2 * lvl], co[2 * lvl + 1]
                sr = pltpu.roll(tr, shifts[lvl], 0)
                si = pltpu.roll(ti, shifts[lvl], 0)
                tr, ti = tr + ar * sr - ai * si, ti + ar * si + ai * sr
            tr, ti = tr + co[6] * cr - co[7] * ci, ti + co[6] * ci + co[7] * cr
            xr_ref[pl.ds(start, 8), cols] = tr
            xi_ref[pl.ds(start, 8), cols] = ti
            accs = carry[2:]
            if tile_fn is not None:
                accs = tuple(a + d for a, d in zip(accs, tile_fn(start, cols, tr, ti)))
            new_c = (tr[0:1], ti[0:1]) if reverse else (tr[7:8], ti[7:8])
            return new_c + accs

        init = (carry_ref[0:1, cols], carry_ref[1:2, cols]) + tuple(a[:, cols] for a in acc_refs)
        out = lax.fori_loop(0, n_tiles, step, init)
        carry_ref[0:1, cols] = out[0]
        carry_ref[1:2, cols] = out[1]
        for a, val in zip(acc_refs, out[2:]):
            a[:, cols] = val


def _ssm_fwd(proj, coef, b_re, b_im, c_re, c_im_neg, reverse):
    lp = proj.shape[0]
    t = _ssm_tile(lp)
    n = lp // t
    order = (lambda i: n - 1 - i) if reverse else (lambda i: i)

    def body(u_ref, coef_ref, bre_ref, bim_ref, cre_ref, cim_ref, y_ref, st_ref, xr, xi, carry):
        @pl.when(pl.program_id(0) == 0)
        def _():
            carry[...] = jnp.zeros_like(carry)

        st_ref[...] = carry[0:2, :]
        ub = u_ref[...].astype(BF16)
        for j in range(SSM_BLOCKS):
            ch, stt = slice(j * BLK_CH, (j + 1) * BLK_CH), slice(j * BLK_ST, (j + 1) * BLK_ST)
            xr[:, stt] = _dot(ub[:, ch], bre_ref[j])
            xi[:, stt] = _dot(ub[:, ch], bim_ref[j])
        _scan_rows(xr, xi, 0, t, coef_ref, carry, reverse)
        for j in range(SSM_BLOCKS):
            ch, stt = slice(j * BLK_CH, (j + 1) * BLK_CH), slice(j * BLK_ST, (j + 1) * BLK_ST)
            y_ref[:, ch] = _dot(xr[:, stt].astype(BF16), cre_ref[j]) + _dot(xi[:, stt].astype(BF16), cim_ref[j])

    wb, wc = _whole((SSM_BLOCKS, BLK_CH, BLK_ST)), _whole((SSM_BLOCKS, BLK_ST, BLK_CH))
    return pl.pallas_call(
        body, name="ssm_fwd_rev" if reverse else "ssm_fwd", grid=(n,),
        in_specs=[pl.BlockSpec((t, D_SSM), lambda i: (order(i), P_U[0] // D_SSM)), _whole((8, 8, N_STATE)), wb, wb, wc, wc],
        out_specs=[pl.BlockSpec((t, D_SSM), lambda i: (order(i), 0)),
                   pl.BlockSpec((None, 2, N_STATE), lambda i: (order(i), 0, 0))],
        out_shape=[jax.ShapeDtypeStruct((lp, D_SSM), F32), jax.ShapeDtypeStruct((n, 2, N_STATE), F32)],
        scratch_shapes=[pltpu.VMEM((t, N_STATE), F32), pltpu.VMEM((t, N_STATE), F32), pltpu.VMEM((8, N_STATE), F32)],
        compiler_params=_cparams(("arbitrary",)),
    )(proj, coef, b_re, b_im, c_re, c_im_neg)


GELU_C0 = math.sqrt(2.0 / math.pi)
GELU_C1 = 0.044715


def _mid(h, tgt, o_exp, proj, y0, y1, ssm_d, w_glu, w_glu_t, b_glu, ssm_norm_w, attn_norm_w_e, w_out_a, w_out_s,
         w_out_a_t, w_out_s_t, post_w, l_real):
    lp = h.shape[0]
    tm = 128

    def body(h_ref, tgt_ref, o_ref, ga_ref, u_ref, sg_ref, y0_ref, y1_ref, d_ref, wg_ref, wgt_ref, bg_ref, ws_ref, wa_ref,
             woa_ref, wos_ref, woat_ref, wost_ref, pw_ref,
             do_ref, delta_ref, dga_ref, dyp_ref, dsg_ref, dres_ref, dwoa_ref, dwos_ref, dwg_ref, vec_ref):
        @pl.when(pl.program_id(0) == 0)
        def _():
            dwoa_ref[...] = jnp.zeros_like(dwoa_ref)
            dwos_ref[...] = jnp.zeros_like(dwos_ref)
            dwg_ref[...] = jnp.zeros_like(dwg_ref)
            vec_ref[...] = jnp.zeros_like(vec_ref)

        u = u_ref[...]
        ypre = y0_ref[...] + y1_ref[...] + d_ref[...] * u
        th = jnp.tanh(GELU_C0 * (ypre + GELU_C1 * ypre * ypre * ypre))
        gel = 0.5 * ypre * (1.0 + th)
        gel_b = gel.astype(BF16)
        glu = _dot(gel_b, wg_ref[...]) + bg_ref[...]
        g1, g2 = glu[:, :D_SSM], glu[:, D_SSM:]
        sig2 = _sigmoid(g2)
        z = g1 * sig2
        sg = sg_ref[...]
        sgs = _sigmoid(sg)
        sil_s = sg * sgs
        s = z * sil_s
        ys, r_s = _rms_fwd(s, ws_ref[...], D_SSM)

        o = o_ref[...]
        ga = ga_ref[...]
        gas = _sigmoid(ga)
        sil_a = ga * gas
        a = o * sil_a
        ya, r_a = _rms_fwd(a, wa_ref[...], D_ATTN)

        ya_b, ys_b = ya.astype(BF16), ys.astype(BF16)
        y = _dot(ya_b, woa_ref[...]) + _dot(ys_b, wos_ref[...])
        yn, r_y = _rms_fwd(y, pw_ref[...], D_MODEL)
        row = lax.broadcasted_iota(jnp.int32, (tm, 1), 0) + pl.program_id(0) * tm
        valid = ((row >= N_META) & (row < l_real)).astype(F32)
        err = (h_ref[...] + yn - tgt_ref[...]) * valid
        loss = 0.5 * jnp.sum(jnp.sum(err * err, axis=-1, keepdims=True), axis=0, keepdims=True) * (1.0 / D_MODEL)
        dout = err * (1.0 / D_MODEL)
        dres_ref[...] = dout

        dy, d_pw = _rms_bwd(y, r_y, pw_ref[...], dout, D_MODEL)
        dy_b = dy.astype(BF16)
        dya = _dot(dy_b, woat_ref[...])
        dys = _dot(dy_b, wost_ref[...])
        dwoa_ref[...] += _dot_tn(ya_b, dy_b)
        dwos_ref[...] += _dot_tn(ys_b, dy_b)

        da, d_wa = _rms_bwd(a, r_a, wa_ref[...], dya, D_ATTN)
        d_o = da * sil_a
        dga_ref[...] = da * o * (gas * (1.0 + ga * (1.0 - gas)))
        do_ref[...] = d_o.astype(BF16)
        prod = d_o * o
        lane8 = lax.broadcasted_iota(jnp.int32, (tm, HEADS), 1)
        delta = jnp.zeros((tm, HEADS), F32)
        for hd in range(HEADS):
            delta = jnp.where(lane8 == hd, jnp.sum(prod[:, hd * HEAD_PAD:(hd + 1) * HEAD_PAD], axis=-1, keepdims=True), delta)
        delta_ref[...] = delta

        ds, d_ws = _rms_bwd(s, r_s, ws_ref[...], dys, D_SSM)
        dz = ds * sil_s
        dsg_ref[...] = ds * z * (sgs * (1.0 + sg * (1.0 - sgs)))
        dglu = jnp.concatenate([dz * sig2, dz * g1 * sig2 * (1.0 - sig2)], axis=-1)
        dglu_b = dglu.astype(BF16)
        dwg_ref[...] += _dot_tn(gel_b, dglu_b)
        dgel = _dot(dglu_b, wgt_ref[...])
        dgelu = 0.5 * (1.0 + th) + 0.5 * ypre * (1.0 - th * th) * (GELU_C0 * (1.0 + 3.0 * GELU_C1 * ypre * ypre))
        dyp = dgel * dgelu
        dyp_ref[...] = dyp

        vec_ref[0:1, :] += d_pw
        vec_ref[1:2, :] += d_wa
        vec_ref[2:3, 0:D_SSM] += d_ws
        vec_ref[3:4, 0:D_SSM] += jnp.sum(dyp * u, axis=0, keepdims=True)
        vec_ref[4:5, :] += jnp.sum(dglu, axis=0, keepdims=True)
        vec_ref[5:6, :] += jnp.broadcast_to(loss, (1, D_MODEL))

    full = lambda off: _rows(tm, (off, D_MODEL))
    half = lambda off: _rows(tm, (off, D_SSM))
    return pl.pallas_call(
        body, name="mid", grid=(lp // tm,),
        in_specs=[full(0), full(0), full(0), _rows(tm, P_GATE_A), _rows(tm, P_U), _rows(tm, P_GATE_S), half(0), half(0),
                  _whole((1, D_SSM)), _whole((D_SSM, 2 * D_SSM)), _whole((2 * D_SSM, D_SSM)), _whole((1, 2 * D_SSM)),
                  _whole((1, D_SSM)), _whole((1, D_EXP)), _whole((D_EXP, D_MODEL)), _whole((D_SSM, D_MODEL)),
                  _whole((D_MODEL, D_EXP)), _whole((D_MODEL, D_SSM)), _whole((1, D_MODEL))],
        out_specs=[full(0), _rows(tm, (0, HEADS)), full(0), half(0), half(0), full(0),
                   _out_whole((D_EXP, D_MODEL)), _out_whole((D_SSM, D_MODEL)), _out_whole((D_SSM, 2 * D_SSM)),
                   _out_whole((8, D_MODEL))],
        out_shape=[jax.ShapeDtypeStruct((lp, D_EXP), BF16), jax.ShapeDtypeStruct((lp, HEADS), F32),
                   jax.ShapeDtypeStruct((lp, D_EXP), F32), jax.ShapeDtypeStruct((lp, D_SSM), F32),
                   jax.ShapeDtypeStruct((lp, D_SSM), F32), jax.ShapeDtypeStruct((lp, D_MODEL), F32),
                   jax.ShapeDtypeStruct((D_EXP, D_MODEL), F32), jax.ShapeDtypeStruct((D_SSM, D_MODEL), F32),
                   jax.ShapeDtypeStruct((D_SSM, 2 * D_SSM), F32), jax.ShapeDtypeStruct((8, D_MODEL), F32)],
        compiler_params=_cparams(("arbitrary",)),
    )(h, tgt, o_exp, proj, proj, proj, y0, y1, ssm_d, w_glu, w_glu_t, b_glu, ssm_norm_w, attn_norm_w_e, w_out_a, w_out_s,
      w_out_a_t, w_out_s_t, post_w)


def _ssm_bwd(proj, dyp, states, coef, coef_adj, b_re, b_im, b_re_t, b_im_t, c_re_t, c_im_neg_t, reverse):
    lp = proj.shape[0]
    t = _ssm_tile(lp)
    n = lp // t
    order = (lambda i: i) if reverse else (lambda i: n - 1 - i)
    edge = (t + 8) if reverse else 7

    def body(u_ref, dy_ref, st_ref, coef_ref, coefa_ref, bre_ref, bim_ref, bret_ref, bimt_ref, cret_ref, cimt_ref,
             du_ref, dbre_ref, dbim_ref, dcre_ref, dcim_ref, da_ref, xr, xi, gr, gi, carry_x, carry_g):
        @pl.when(pl.program_id(0) == 0)
        def _():
            carry_g[...] = jnp.zeros_like(carry_g)
            carry_x[...] = jnp.zeros_like(carry_x)
            dbre_ref[...] = jnp.zeros_like(dbre_ref)
            dbim_ref[...] = jnp.zeros_like(dbim_ref)
            dcre_ref[...] = jnp.zeros_like(dcre_ref)
            dcim_ref[...] = jnp.zeros_like(dcim_ref)
            da_ref[...] = jnp.zeros_like(da_ref)
            for halo in (slice(0, 8), slice(t + 8, t + 16)):
                xr[halo, :] = jnp.zeros((8, N_STATE), F32)
                xi[halo, :] = jnp.zeros((8, N_STATE), F32)

        ub = u_ref[...].astype(BF16)
        dyb = dy_ref[...].astype(BF16)
        carry_x[0:2, :] = st_ref[...]
        xr[edge:edge + 1, :] = st_ref[0:1, :]
        xi[edge:edge + 1, :] = st_ref[1:2, :]
        blocks = [(slice(j * BLK_CH, (j + 1) * BLK_CH), slice(j * BLK_ST, (j + 1) * BLK_ST)) for j in range(SSM_BLOCKS)]
        for j, (ch, stt) in enumerate(blocks):
            xr[8:t + 8, stt] = _dot(ub[:, ch], bre_ref[j])
            xi[8:t + 8, stt] = _dot(ub[:, ch], bim_ref[j])
            gr[:, stt] = _dot(dyb[:, ch], cret_ref[j])
            gi[:, stt] = _dot(dyb[:, ch], cimt_ref[j])
        _scan_rows(xr, xi, 8, t, coef_ref, carry_x, reverse)

        row8 = lax.broadcasted_iota(jnp.int32, (8, SCAN_COLS), 0)

        def tile_fn(start, cols, g_re, g_im):
            xs = pl.multiple_of(start + 8, 8)
            if reverse:
                nb = pl.multiple_of(start + 16, 8)
                xn_r = jnp.where(row8 == 7, xr[pl.ds(nb, 8), cols][0:1], pltpu.roll(xr[pl.ds(xs, 8), cols], 7, 0))
                xn_i = jnp.where(row8 == 7, xi[pl.ds(nb, 8), cols][0:1], pltpu.roll(xi[pl.ds(xs, 8), cols], 7, 0))
            else:
                nb = pl.multiple_of(start, 8)
                xn_r = jnp.where(row8 == 0, xr[pl.ds(nb, 8), cols][7:8], pltpu.roll(xr[pl.ds(xs, 8), cols], 1, 0))
                xn_i = jnp.where(row8 == 0, xi[pl.ds(nb, 8), cols][7:8], pltpu.roll(xi[pl.ds(xs, 8), cols], 1, 0))
            return g_re * xn_r + g_im * xn_i, g_im * xn_r - g_re * xn_i

        _scan_rows(gr, gi, 0, t, coefa_ref, carry_g, not reverse, tile_fn=tile_fn, acc_refs=(da_ref.at[0], da_ref.at[1]))

        for j, (ch, stt) in enumerate(blocks):
            g_re_b, g_im_b = gr[:, stt].astype(BF16), gi[:, stt].astype(BF16)
            du_ref[:, ch] = _dot(g_re_b, bret_ref[j]) + _dot(g_im_b, bimt_ref[j])
            dbre_ref[j] += _dot_tn(ub[:, ch], g_re_b)
            dbim_ref[j] += _dot_tn(ub[:, ch], g_im_b)
            dcre_ref[j] += _dot_tn(dyb[:, ch], xr[8:t + 8, stt].astype(BF16))
            dcim_ref[j] -= _dot_tn(dyb[:, ch], xi[8:t + 8, stt].astype(BF16))

    dense = jax.ShapeDtypeStruct((SSM_BLOCKS, BLK_CH, BLK_ST), F32)
    wb, wc = _whole((SSM_BLOCKS, BLK_CH, BLK_ST)), _whole((SSM_BLOCKS, BLK_ST, BLK_CH))
    acc = _out_whole((SSM_BLOCKS, BLK_CH, BLK_ST))
    return pl.pallas_call(
        body, name="ssm_bwd_rev" if reverse else "ssm_bwd", grid=(n,),
        in_specs=[pl.BlockSpec((t, D_SSM), lambda i: (order(i), P_U[0] // D_SSM)),
                  pl.BlockSpec((t, D_SSM), lambda i: (order(i), 0)),
                  pl.BlockSpec((None, 2, N_STATE), lambda i: (order(i), 0, 0)),
                  _whole((8, 8, N_STATE)), _whole((8, 8, N_STATE)), wb, wb, wc, wc, wb, wb],
        out_specs=[pl.BlockSpec((t, D_SSM), lambda i: (order(i), 0)), acc, acc, acc, acc, _out_whole((2, 8, N_STATE))],
        out_shape=[jax.ShapeDtypeStruct((lp, D_SSM), F32), dense, dense, dense, dense,
                   jax.ShapeDtypeStruct((2, 8, N_STATE), F32)],
        scratch_shapes=[pltpu.VMEM((t + 16, N_STATE), F32), pltpu.VMEM((t + 16, N_STATE), F32),
                        pltpu.VMEM((t, N_STATE), F32), pltpu.VMEM((t, N_STATE), F32),
                        pltpu.VMEM((8, N_STATE), F32), pltpu.VMEM((8, N_STATE), F32)],
        compiler_params=_cparams(("arbitrary",)),
    )(proj, dyp, states, coef, coef_adj, b_re, b_im, b_re_t, b_im_t, c_re_t, c_im_neg_t)


def _flash_bwd(q, k, v, d_o, lse_row, delta_row):
    lp = q.shape[0]
    tq = tk = _row_tile(lp)
    nq = lp // tq

    def body(k_ref, v_ref, q_ref, do_ref, lse_ref, delta_ref, dk_ref, dv_ref, dq_ref):
        @pl.when(pl.program_id(1) == 0)
        def _():
            dq_ref[...] = jnp.zeros_like(dq_ref)

        kt = k_ref[...]
        vt = v_ref[...]

        def step(i, carry):
            dk, dv = carry
            qs = pl.multiple_of(i * tq, tq)
            qt = q_ref[pl.ds(qs, tq), :]
            dot_ = do_ref[pl.ds(qs, tq), :]
            st = _dot_nt(kt, qt)
            pt = jnp.exp2(st - lse_ref[:, pl.ds(qs, tq)])
            dpt = _dot_nt(vt, dot_)
            dst = (pt * (dpt - delta_ref[:, pl.ds(qs, tq)])).astype(BF16)
            dv = dv + _dot(pt.astype(BF16), dot_)
            dk = dk + _dot(dst, qt)
            dq_ref[pl.ds(qs, tq), :] += _dot_tn(dst, kt)
            return dk, dv

        zero = jnp.zeros((tk, HEAD_PAD), F32)
        dk, dv = lax.fori_loop(0, nq, step, (zero, zero))
        dk_ref[...] = dk * (1.0 / LOG2E)
        dv_ref[...] = dv

    tile = pl.BlockSpec((tk, HEAD_PAD), lambda hd, j: (j, hd))
    head = pl.BlockSpec((lp, HEAD_PAD), lambda hd, j: (0, hd))
    rowv = pl.BlockSpec((None, 1, lp), lambda hd, j: (hd, 0, 0))
    out = jax.ShapeDtypeStruct((lp, D_EXP), F32)
    return pl.pallas_call(
        body, name="flash_bwd", grid=(HEADS, lp // tk),
        in_specs=[tile, tile, head, head, rowv, rowv],
        out_specs=[tile, tile, head],
        out_shape=[out, out, out],
        compiler_params=_cparams(("parallel", "arbitrary")),
    )(k, v, q, d_o, lse_row, delta_row)


def _attn_prep_bwd(dq, dk, dv, proj, q_norm_w, kv_norm_w, wq_pt, wk_pt, wv_pt, cos, sina, sinb):
    lp = proj.shape[0]
    tm = _row_tile(lp)

    def body(dq_ref, dk_ref, dv_ref, ql_ref, kvl_ref, qw_ref, kw_ref, wqt_ref, wkt_ref, wvt_ref, cos_ref, sa_ref, sb_ref,
             dql_ref, dkvl_ref, dkr_ref, dwq_ref, dwk_ref, dwv_ref, vec_ref):
        @pl.when(pl.program_id(0) == 0)
        def _():
            dwq_ref[...] = jnp.zeros_like(dwq_ref)
            dwk_ref[...] = jnp.zeros_like(dwk_ref)
            dwv_ref[...] = jnp.zeros_like(dwv_ref)
            vec_ref[...] = jnp.zeros_like(vec_ref)

        cos_t, sa_t, sb_t = cos_ref[...], sa_ref[...], sb_ref[...]
        dkp = dk_ref[...]
        dqp = jnp.concatenate(
            [_rope_transpose(dq_ref[:, hd * HEAD_PAD:(hd + 1) * HEAD_PAD] * SCALE, cos_t, sa_t, sb_t) for hd in range(HEADS)],
            axis=-1)
        dkr = dkp[:, 0:HEAD_PAD]
        for hd in range(1, HEADS):
            dkr = dkr + dkp[:, hd * HEAD_PAD:(hd + 1) * HEAD_PAD]
        dkr_ref[...] = _rope_transpose(dkr, cos_t, sa_t, sb_t)

        qn, r_q = _rms_fwd(ql_ref[...], qw_ref[...], Q_LORA)
        kvn, r_kv = _rms_fwd(kvl_ref[...], kw_ref[...], KV_LORA)
        dqp_b, dkp_b, dv_b = dqp.astype(BF16), dkp.astype(BF16), dv_ref[...].astype(BF16)
        dqn = _dot(dqp_b, wqt_ref[...])
        dkvn = _dot(dkp_b, wkt_ref[...]) + _dot(dv_b, wvt_ref[...])
        dwq_ref[...] += _dot_tn(qn.astype(BF16), dqp_b)
        dwk_ref[...] += _dot_tn(kvn.astype(BF16), dkp_b)
        dwv_ref[...] += _dot_tn(kvn.astype(BF16), dv_b)
        dql, d_qw = _rms_bwd(ql_ref[...], r_q, qw_ref[...], dqn, Q_LORA)
        dkvl, d_kw = _rms_bwd(kvl_ref[...], r_kv, kw_ref[...], dkvn, KV_LORA)
        dql_ref[...] = dql
        dkvl_ref[...] = dkvl
        vec_ref[0:1, :] += d_qw
        vec_ref[1:2, 0:KV_LORA] += d_kw

    tab = _rows(tm, (0, HEAD_PAD))
    full = _rows(tm, (0, D_EXP))
    return pl.pallas_call(
        body, name="attn_prep_bwd", grid=(lp // tm,),
        in_specs=[full, full, full, _rows(tm, P_QLAT), _rows(tm, P_KVLAT), _whole((1, Q_LORA)), _whole((1, KV_LORA)),
                  _whole((D_EXP, Q_LORA)), _whole((D_EXP, KV_LORA)), _whole((D_EXP, KV_LORA)), tab, tab, tab],
        out_specs=[_rows(tm, (0, Q_LORA)), _rows(tm, (0, KV_LORA)), _rows(tm, (0, HEAD_PAD)),
                   _out_whole((Q_LORA, D_EXP)), _out_whole((KV_LORA, D_EXP)), _out_whole((KV_LORA, D_EXP)),
                   _out_whole((8, Q_LORA))],
        out_shape=[jax.ShapeDtypeStruct((lp, Q_LORA), F32), jax.ShapeDtypeStruct((lp, KV_LORA), F32),
                   jax.ShapeDtypeStruct((lp, HEAD_PAD), F32), jax.ShapeDtypeStruct((Q_LORA, D_EXP), F32),
                   jax.ShapeDtypeStruct((KV_LORA, D_EXP), F32), jax.ShapeDtypeStruct((KV_LORA, D_EXP), F32),
                   jax.ShapeDtypeStruct((8, Q_LORA), F32)],
        compiler_params=_cparams(("arbitrary",)),
    )(dq, dk, dv, proj, proj, q_norm_w, kv_norm_w, wq_pt, wk_pt, wv_pt, cos, sina, sinb)


def _in_proj_bwd(h, pre_w, dres, dga, du0, du1, dyp, ssm_d, dsg, dql, dkvl, dkr, w_in_pt):
    lp = h.shape[0]
    tm = 128
    pieces = (P_GATE_A, P_U, P_GATE_S, P_QLAT, P_KVLAT, P_KROPE)

    def body(h_ref, w_ref, dres_ref, dga_ref, du0_ref, du1_ref, dyp_ref, d_ref, dsg_ref, dql_ref, dkvl_ref, dkr_ref, wt_ref,
             dh_ref, dw_ref, vec_ref):
        @pl.when(pl.program_id(0) == 0)
        def _():
            dw_ref[...] = jnp.zeros_like(dw_ref)
            vec_ref[...] = jnp.zeros_like(vec_ref)

        hv = h_ref[...]
        xn, r = _rms_fwd(hv, w_ref[...], D_MODEL)
        xn_b = xn.astype(BF16)
        du = du0_ref[...] + du1_ref[...] + dyp_ref[...] * d_ref[...]
        grads = (dga_ref[...], du, dsg_ref[...], dql_ref[...], dkvl_ref[...], dkr_ref[...])
        dxn = jnp.zeros((tm, D_MODEL), F32)
        for (off, width), g in zip(pieces, grads):
            g_b = g.astype(BF16)
            dxn = dxn + _dot(g_b, wt_ref[off:off + width, :])
            dw_ref[:, off:off + width] += _dot_tn(xn_b, g_b)
        dx, d_w = _rms_bwd(hv, r, w_ref[...], dxn, D_MODEL)
        dh_ref[...] = dres_ref[...] + dx
        vec_ref[0:1, :] += d_w

    full = _rows(tm, (0, D_MODEL))
    half = _rows(tm, (0, D_SSM))
    return pl.pallas_call(
        body, name="in_proj_bwd", grid=(lp // tm,),
        in_specs=[full, _whole((1, D_MODEL)), full, full, half, half, half, _whole((1, D_SSM)), half,
                  _rows(tm, (0, Q_LORA)), _rows(tm, (0, KV_LORA)), _rows(tm, (0, HEAD_PAD)), _whole((D_PROJ, D_MODEL))],
        out_specs=[full, _out_whole((D_MODEL, D_PROJ)), _out_whole((8, D_MODEL))],
        out_shape=[jax.ShapeDtypeStruct((lp, D_MODEL), F32), jax.ShapeDtypeStruct((D_MODEL, D_PROJ), F32),
                   jax.ShapeDtypeStruct((8, D_MODEL), F32)],
        compiler_params=_cparams(("arbitrary",)),
    )(h, pre_w, dres, dga, du0, du1, dyp, ssm_d, dsg, dql, dkvl, dkr, w_in_pt)


def _other_chips(x, y):
    return [(1 - x, y), (x, 1 - y), (1 - x, 1 - y)]


def _gather_weights(w_bf16, meta):
    any_spec = pl.BlockSpec(memory_space=pl.ANY)

    def body(w_ref, m_ref, wout_ref, mout_ref, send_sems, recv_sems, local_sems):
        x, y, c = lax.axis_index("x"), lax.axis_index("y"), lax.axis_index("c")
        me = 2 * x + y
        own = [pltpu.make_async_copy(w_ref, wout_ref.at[me], local_sems.at[0]),
               pltpu.make_async_copy(m_ref, mout_ref.at[me], local_sems.at[1])]
        for cp in own:
            cp.start()
        sends = []
        for j, (tx, ty) in enumerate(_other_chips(x, y)):
            for n, (src, dst) in enumerate(((w_ref, wout_ref), (m_ref, mout_ref))):
                sends.append(pltpu.make_async_remote_copy(
                    src_ref=src, dst_ref=dst.at[me], send_sem=send_sems.at[2 * j + n], recv_sem=recv_sems.at[2 * j + n],
                    device_id=(tx, ty, c), device_id_type=MESH))
        for cp in sends:
            cp.start()
        for j, (tx, ty) in enumerate(_other_chips(x, y)):
            for n, (src, dst) in enumerate(((w_ref, wout_ref), (m_ref, mout_ref))):
                pltpu.make_async_remote_copy(
                    src_ref=src, dst_ref=dst.at[2 * tx + ty], send_sem=send_sems.at[2 * j + n],
                    recv_sem=recv_sems.at[2 * j + n], device_id=(tx, ty, c), device_id_type=MESH).wait_recv()
        for cp in sends:
            cp.wait_send()
        for cp in own:
            cp.wait()

    return pl.pallas_call(
        body, name="gather_weights",
        in_specs=[any_spec, any_spec], out_specs=[any_spec, any_spec],
        out_shape=[jax.ShapeDtypeStruct((4,) + w_bf16.shape, w_bf16.dtype), jax.ShapeDtypeStruct((4,) + meta.shape, meta.dtype)],
        scratch_shapes=[pltpu.SemaphoreType.DMA((6,)), pltpu.SemaphoreType.DMA((6,)), pltpu.SemaphoreType.DMA((2,))],
    )(w_bf16, meta)


def _swap_sibling(g):
    any_spec = pl.BlockSpec(memory_space=pl.ANY)

    def body(g_ref, out_ref, send_sem, recv_sem):
        x, y, c = lax.axis_index("x"), lax.axis_index("y"), lax.axis_index("c")
        cp = pltpu.make_async_remote_copy(src_ref=g_ref, dst_ref=out_ref, send_sem=send_sem, recv_sem=recv_sem,
                                          device_id=(x, y, 1 - c), device_id_type=MESH)
        cp.start()
        cp.wait()

    return pl.pallas_call(
        body, name="swap_sibling", in_specs=[any_spec], out_specs=any_spec,
        out_shape=jax.ShapeDtypeStruct(g.shape, g.dtype),
        scratch_shapes=[pltpu.SemaphoreType.DMA(()), pltpu.SemaphoreType.DMA(())],
    )(g)


def _pair_sum(a, b):
    rows = a.shape[0]
    tm = _pick_tile(rows, 1024)

    def body(a_ref, b_ref, o_ref):
        o_ref[...] = a_ref[...] + b_ref[...]

    spec = pl.BlockSpec((tm, 1024), lambda i: (i, 0))
    return pl.pallas_call(body, name="pair_sum", grid=(rows // tm,), in_specs=[spec, spec], out_specs=spec,
                          out_shape=jax.ShapeDtypeStruct(a.shape, F32), compiler_params=_cparams(("parallel",)))(a, b)


def _scatter_chips(s, rs, rsm):
    any_spec = pl.BlockSpec(memory_space=pl.ANY)

    def body(s_ref, out_ref, send_sems, recv_sems, local_sems):
        x, y, c = lax.axis_index("x"), lax.axis_index("y"), lax.axis_index("c")
        me = 2 * x + y
        small = s_ref.at[pl.ds(4 * rs, rsm)]

        def pieces(target):
            return ((s_ref.at[pl.ds(pl.multiple_of(target * rs, 8), rs)], pl.ds(0, rs)), (small, pl.ds(rs, rsm)))

        own = [pltpu.make_async_copy(src, out_ref.at[me, rows], local_sems.at[n]) for n, (src, rows) in enumerate(pieces(me))]
        for cp in own:
            cp.start()
        sends = []
        for j, (tx, ty) in enumerate(_other_chips(x, y)):
            for n, (src, rows) in enumerate(pieces(2 * tx + ty)):
                sends.append(pltpu.make_async_remote_copy(
                    src_ref=src, dst_ref=out_ref.at[me, rows], send_sem=send_sems.at[2 * j + n],
                    recv_sem=recv_sems.at[2 * j + n], device_id=(tx, ty, c), device_id_type=MESH))
        for cp in sends:
            cp.start()
        for j, (tx, ty) in enumerate(_other_chips(x, y)):
            for n, (src, rows) in enumerate(pieces(me)):
                pltpu.make_async_remote_copy(
                    src_ref=src, dst_ref=out_ref.at[2 * tx + ty, rows], send_sem=send_sems.at[2 * j + n],
                    recv_sem=recv_sems.at[2 * j + n], device_id=(tx, ty, c), device_id_type=MESH).wait_recv()
        for cp in sends:
            cp.wait_send()
        for cp in own:
            cp.wait()

    return pl.pallas_call(
        body, name="scatter_chips", in_specs=[any_spec], out_specs=any_spec,
        out_shape=jax.ShapeDtypeStruct((4, rs + rsm, 1024), F32),
        scratch_shapes=[pltpu.SemaphoreType.DMA((6,)), pltpu.SemaphoreType.DMA((6,)), pltpu.SemaphoreType.DMA((2,))],
    )(s)


def _adamw(parts, w, m, v):
    rows = w.shape[0]
    tm = _pick_tile(rows, 256)
    c1 = 1.0 / (1.0 - ADAM_B1 ** ADAM_STEP)
    c2 = 1.0 / (1.0 - ADAM_B2 ** ADAM_STEP)

    def body(p_ref, w_ref, m_ref, v_ref, g_ref, d_ref, nm_ref, nv_ref):
        g = ((p_ref[0] + p_ref[1]) + p_ref[2]) + p_ref[3]
        nm = ADAM_B1 * m_ref[...] + (1.0 - ADAM_B1) * g
        nv = ADAM_B2 * v_ref[...] + (1.0 - ADAM_B2) * (g * g)
        g_ref[...] = g
        nm_ref[...] = nm
        nv_ref[...] = nv
        d_ref[...] = -ADAM_LR * ((nm * c1) / (jnp.sqrt(nv * c2) + ADAM_EPS) + ADAM_WD * w_ref[...])

    spec = pl.BlockSpec((tm, 1024), lambda i: (i, 0))
    out = jax.ShapeDtypeStruct(w.shape, F32)
    return pl.pallas_call(
        body, name="adamw", grid=(rows // tm,),
        in_specs=[pl.BlockSpec((4, tm, 1024), lambda i: (0, i, 0)), spec, spec, spec],
        out_specs=[spec] * 4, out_shape=[out] * 4, compiler_params=_cparams(("parallel",)),
    )(parts, w, m, v)


def _expand_heads(a, axis, per_head):
    a = jnp.moveaxis(a, axis, -1)
    lead = a.shape[:-1]
    a = a.reshape(lead + (HEADS, per_head))
    a = jnp.pad(a, [(0, 0)] * len(lead) + [(0, 0), (0, HEAD_PAD - per_head)])
    return jnp.moveaxis(a.reshape(lead + (D_EXP,)), -1, axis)


def _compact_heads(a, axis, start, size):
    a = jnp.moveaxis(a, axis, -1)
    lead = a.shape[:-1]
    a = a.reshape(lead + (HEADS, HEAD_PAD))[..., start:start + size]
    return jnp.moveaxis(a.reshape(lead + (HEADS * size,)), -1, axis)


def _block_diag(w):
    g, a, b = w.shape
    per = g // SSM_BLOCKS
    eye = jnp.eye(per, dtype=w.dtype)
    return jnp.einsum("jgab,gk->jgakb", w.reshape(SSM_BLOCKS, per, a, b), eye).reshape(SSM_BLOCKS, per * a, per * b)


def _block_diag_extract(dense, a, b):
    per = N_GROUPS // SSM_BLOCKS
    d5 = dense.reshape(SSM_BLOCKS, per, a, per, b)
    return jnp.einsum("jgakb,gk->jgab", d5, jnp.eye(per, dtype=dense.dtype)).reshape(N_GROUPS, a, b)


def _discretise(a_re, a_im, log_dt, b_re, b_im):
    dt = jnp.exp(log_dt)[:, None]
    mag = jnp.exp(a_re * dt)
    abar_re = mag * jnp.cos(a_im * dt)
    abar_im = mag * jnp.sin(a_im * dt)
    num_re = abar_re - 1.0
    num_im = abar_im
    den = a_re * a_re + a_im * a_im
    coef_re = (num_re * a_re + num_im * a_im) / den
    coef_im = (num_im * a_re - num_re * a_im) / den
    bbar_re = coef_re[..., None] * b_re - coef_im[..., None] * b_im
    bbar_im = coef_re[..., None] * b_im + coef_im[..., None] * b_re
    return abar_re, abar_im, bbar_re, bbar_im


def _scan_coef(ar, ai, reverse):
    ar, ai = ar.reshape(1, N_STATE), ai.reshape(1, N_STATE)
    pows = [(ar, ai)]
    for _ in range(7):
        pr, pi_ = pows[-1]
        pows.append((pr * ar - pi_ * ai, pr * ai + pi_ * ar))
    row = jnp.arange(8)[:, None]
    out = []
    for k in (1, 2, 4):
        keep = (row < 8 - k) if reverse else (row >= k)
        out += [jnp.where(keep, pows[k - 1][0], 0.0), jnp.where(keep, pows[k - 1][1], 0.0)]
    order = list(range(7, -1, -1)) if reverse else list(range(8))
    out += [jnp.concatenate([pows[k][0] for k in order], axis=0), jnp.concatenate([pows[k][1] for k in order], axis=0)]
    return jnp.stack(out).astype(F32)


def _flat_rows(a, rows):
    flat = a.reshape(-1)
    return jnp.pad(flat, (0, rows * 1024 - flat.shape[0])).reshape(rows, 1024)


def _pack(named, order):
    rows = [-(-math.prod(named[n].shape) // 1024) for n in order]
    total = -(-sum(rows) // 8) * 8
    parts = [_flat_rows(named[n], r) for n, r in zip(order, rows)]
    if total > sum(rows):
        parts.append(jnp.zeros((total - sum(rows), 1024), parts[0].dtype))
    return jnp.concatenate(parts, axis=0)


def _unpack(packed, shapes, order):
    out, at = {}, 0
    for n in order:
        size = math.prod(shapes[n])
        rows = -(-size // 1024)
        out[n] = packed[at:at + rows].reshape(-1)[:size].reshape(shapes[n])
        at += rows
    return out


def _shard_cols(a, k):
    w = a.shape[-1] // 4
    return a[..., k * w:(k + 1) * w]


def kernel(x, meta_tokens, pre_norm_w, post_norm_w, w_in, q_norm_w, w_q_up, kv_norm_w, w_kv_up, attn_out_norm_w, ssm_a_re, ssm_a_im, ssm_log_dt, ssm_b_re, ssm_b_im, ssm_c_re, ssm_c_im, ssm_d, w_glu, b_glu, ssm_out_norm_w, w_out, loss_target, m_meta_tokens, m_pre_norm_w, m_post_norm_w, m_w_in, m_q_norm_w, m_w_q_up, m_kv_norm_w, m_w_kv_up, m_attn_out_norm_w, m_ssm_a_re, m_ssm_a_im, m_ssm_log_dt, m_ssm_b_re, m_ssm_b_im, m_ssm_c_re, m_ssm_c_im, m_ssm_d, m_w_glu, m_b_glu, m_ssm_out_norm_w, m_w_out, v_meta_tokens, v_pre_norm_w, v_post_norm_w, v_w_in, v_q_norm_w, v_w_q_up, v_kv_norm_w, v_w_kv_up, v_attn_out_norm_w, v_ssm_a_re, v_ssm_a_im, v_ssm_log_dt, v_ssm_b_re, v_ssm_b_im, v_ssm_c_re, v_ssm_c_im, v_ssm_d, v_w_glu, v_b_glu, v_ssm_out_norm_w, v_w_out):
    local = dict(meta_tokens=meta_tokens, pre_norm_w=pre_norm_w, post_norm_w=post_norm_w, w_in=w_in, q_norm_w=q_norm_w,
                 w_q_up=w_q_up, kv_norm_w=kv_norm_w, w_kv_up=w_kv_up, attn_out_norm_w=attn_out_norm_w, ssm_a_re=ssm_a_re,
                 ssm_a_im=ssm_a_im, ssm_log_dt=ssm_log_dt, ssm_b_re=ssm_b_re, ssm_b_im=ssm_b_im, ssm_c_re=ssm_c_re,
                 ssm_c_im=ssm_c_im, ssm_d=ssm_d, w_glu=w_glu, b_glu=b_glu, ssm_out_norm_w=ssm_out_norm_w, w_out=w_out)
    mom_m = dict(meta_tokens=m_meta_tokens, pre_norm_w=m_pre_norm_w, post_norm_w=m_post_norm_w, w_in=m_w_in,
                 q_norm_w=m_q_norm_w, w_q_up=m_w_q_up, kv_norm_w=m_kv_norm_w, w_kv_up=m_w_kv_up,
                 attn_out_norm_w=m_attn_out_norm_w, ssm_a_re=m_ssm_a_re, ssm_a_im=m_ssm_a_im, ssm_log_dt=m_ssm_log_dt,
                 ssm_b_re=m_ssm_b_re, ssm_b_im=m_ssm_b_im, ssm_c_re=m_ssm_c_re, ssm_c_im=m_ssm_c_im, ssm_d=m_ssm_d,
                 w_glu=m_w_glu, b_glu=m_b_glu, ssm_out_norm_w=m_ssm_out_norm_w, w_out=m_w_out)
    mom_v = dict(meta_tokens=v_meta_tokens, pre_norm_w=v_pre_norm_w, post_norm_w=v_post_norm_w, w_in=v_w_in,
                 q_norm_w=v_q_norm_w, w_q_up=v_w_q_up, kv_norm_w=v_kv_norm_w, w_kv_up=v_w_kv_up,
                 attn_out_norm_w=v_attn_out_norm_w, ssm_a_re=v_ssm_a_re, ssm_a_im=v_ssm_a_im, ssm_log_dt=v_ssm_log_dt,
                 ssm_b_re=v_ssm_b_re, ssm_b_im=v_ssm_b_im, ssm_c_re=v_ssm_c_re, ssm_c_im=v_ssm_c_im, ssm_d=v_ssm_d,
                 w_glu=v_w_glu, b_glu=v_b_glu, ssm_out_norm_w=v_ssm_out_norm_w, w_out=v_w_out)
    shapes = {n: local[n].shape for n in WEIGHTS}
    mat = ("w_in", "w_q_up", "w_kv_up", "w_glu", "w_out")

    seq = x.shape[1]
    l_real = N_META + seq
    lp = -(-l_real // 640) * 640 if l_real > 1024 else -(-l_real // 128) * 128

    w_shard = _pack({n: local[n].astype(BF16) for n in mat}, mat)
    w_shard = jnp.pad(w_shard, ((0, -w_shard.shape[0] % 16), (0, 0)))
    w_all, meta_all = _gather_weights(w_shard, meta_tokens)
    mat_shapes = {n: shapes[n] for n in mat}
    per_chip = [_unpack(w_all[k], mat_shapes, mat) for k in range(4)]
    w_in_f = jnp.concatenate([p["w_in"][0] for p in per_chip], axis=1)
    w_q_f = jnp.concatenate([p["w_q_up"][0] for p in per_chip], axis=1)
    w_kv_f = jnp.concatenate([p["w_kv_up"][0] for p in per_chip], axis=1)
    w_glu_f = jnp.concatenate([p["w_glu"][0] for p in per_chip], axis=1)
    w_out_f = jnp.concatenate([p["w_out"][0] for p in per_chip], axis=0)
    meta_f = jnp.concatenate([meta_all[k] for k in range(4)], axis=1)

    o_q, o_kv, o_kr, o_ga, o_u, o_gs = 0, 256, 384, 416, 928, 1440
    krope_cols = jnp.pad(w_in_f[:, o_kr:o_ga], ((0, 0), (QK_NOPE, HEAD_PAD - QK_NOPE - QK_ROPE)))
    w_in_p = jnp.concatenate([_expand_heads(w_in_f[:, o_ga:o_u], 1, V_HEAD), w_in_f[:, o_u:o_gs], w_in_f[:, o_gs:],
                              w_in_f[:, o_q:o_kv], w_in_f[:, o_kv:o_kr], krope_cols], axis=1)
    wq_p = _expand_heads(w_q_f, 1, QK_NOPE + QK_ROPE)
    kv3 = w_kv_f.reshape(KV_LORA, HEADS, QK_NOPE + V_HEAD)
    wk_p = _expand_heads(kv3[:, :, :QK_NOPE].reshape(KV_LORA, HEADS * QK_NOPE), 1, QK_NOPE)
    wv_c = kv3[:, :, QK_NOPE:].reshape(KV_LORA, HEADS * V_HEAD)
    wv_p = _expand_heads(wv_c, 1, V_HEAD)
    w_out_a = _expand_heads(w_out_f[:D_ATTN], 0, V_HEAD)
    w_out_s = w_out_f[D_ATTN:]
    attn_norm_e = _expand_heads(attn_out_norm_w, 1, V_HEAD)

    pos = jnp.arange(lp, dtype=jnp.int32)
    half = QK_ROPE // 2
    inv = ROPE_THETA ** (-jnp.arange(half, dtype=F32) / half)
    ang = pos.astype(F32)[:, None] * inv[None, :]
    cos16, sin16 = jnp.cos(ang), jnp.sin(ang)
    ones, zeros = jnp.ones((lp, QK_NOPE), F32), jnp.zeros((lp, QK_NOPE), F32)
    tail1, tail0 = jnp.ones((lp, HEAD_PAD - MASK_LANE), F32), jnp.zeros((lp, HEAD_PAD - MASK_LANE), F32)
    z16 = jnp.zeros((lp, half), F32)
    cos = jnp.concatenate([ones, cos16, cos16, tail1], axis=1)
    sina = jnp.concatenate([zeros, z16, sin16, tail0], axis=1)
    sinb = jnp.concatenate([zeros, -sin16, z16, tail0], axis=1)

    disc_in = (ssm_a_re[0], ssm_a_im[0], ssm_log_dt[0], ssm_b_re[0], ssm_b_im[0])
    disc = lambda a_re, a_im, ldt, b_re, b_im: jax.vmap(_discretise)(a_re, a_im, ldt, b_re, b_im)
    (abar_re, abar_im, bbar_re, bbar_im), disc_vjp = jax.vjp(disc, *disc_in)
    ssm = []
    for d in range(2):
        rev = d == 1
        b_re_bd = _block_diag(jnp.swapaxes(bbar_re[d], 1, 2)).astype(BF16)
        b_im_bd = _block_diag(jnp.swapaxes(bbar_im[d], 1, 2)).astype(BF16)
        c_re_bd = _block_diag(jnp.swapaxes(ssm_c_re[0, d], 1, 2)).astype(BF16)
        c_im_bd = _block_diag(jnp.swapaxes(-ssm_c_im[0, d], 1, 2)).astype(BF16)
        ssm.append(dict(rev=rev, coef=_scan_coef(abar_re[d], abar_im[d], rev),
                        coef_adj=_scan_coef(abar_re[d], -abar_im[d], not rev),
                        b_re=b_re_bd, b_im=b_im_bd, c_re=c_re_bd, c_im=c_im_bd))

    h = jnp.concatenate([meta_f, x[0], jnp.zeros((lp - l_real, D_MODEL), F32)], axis=0)
    tgt = jnp.concatenate([jnp.zeros((N_META, D_MODEL), F32), loss_target[0], jnp.zeros((lp - l_real, D_MODEL), F32)], axis=0)
    proj = _in_proj_fwd(h, pre_norm_w, w_in_p)
    q, k, v, vt = _attn_prep_fwd(proj, q_norm_w, kv_norm_w, wq_p, wk_p, wv_p, wv_c.T, cos, sina, sinb, l_real)
    o_exp, lse = _flash_fwd(q, k, vt)
    ys, states = [], []
    for s in ssm:
        y_d, st_d = _ssm_fwd(proj, s["coef"], s["b_re"], s["b_im"], s["c_re"], s["c_im"], s["rev"])
        ys.append(y_d)
        states.append(st_d)

    (d_o, delta, dga, dyp, dsg, dres, dwoa, dwos, dwglu, vec_mid) = _mid(
        h, tgt, o_exp, proj, ys[0], ys[1], ssm_d, w_glu_f, w_glu_f.T, b_glu, ssm_out_norm_w, attn_norm_e, w_out_a, w_out_s,
        w_out_a.T, w_out_s.T, post_norm_w, l_real)
    dus, dssm = [], []
    tr = lambda a: jnp.swapaxes(a, 1, 2)
    for s, st_d in zip(ssm, states):
        du_d, dbre, dbim, dcre, dcim, da = _ssm_bwd(proj, dyp, st_d, s["coef"], s["coef_adj"], s["b_re"], s["b_im"],
                                                    tr(s["b_re"]), tr(s["b_im"]), tr(s["c_re"]), tr(s["c_im"]), s["rev"])
        dus.append(du_d)
        dssm.append((dbre, dbim, dcre, dcim, da))
    dk, dv, dq = _flash_bwd(q, k, v, d_o, lse, delta.T.reshape(HEADS, 1, lp))
    dql, dkvl, dkr, dwq_p, dwk_p, dwv_p, vec_prep = _attn_prep_bwd(
        dq, dk, dv, proj, q_norm_w, kv_norm_w, wq_p.T, wk_p.T, wv_p.T, cos, sina, sinb)
    dh, dwin_p, vec_in = _in_proj_bwd(h, pre_norm_w, dres, dga, dus[0], dus[1], dyp, ssm_d, dsg, dql, dkvl, dkr, w_in_p.T)

    grads = {}
    grads["w_in"] = jnp.concatenate([
        dwin_p[:, P_QLAT[0]:P_QLAT[0] + 256], dwin_p[:, P_KVLAT[0]:P_KVLAT[0] + 128],
        dwin_p[:, P_KROPE[0] + QK_NOPE:P_KROPE[0] + QK_NOPE + QK_ROPE], _compact_heads(dwin_p[:, 0:D_EXP], 1, 0, V_HEAD),
        dwin_p[:, P_U[0]:P_U[0] + 512], dwin_p[:, P_GATE_S[0]:P_GATE_S[0] + 512]], axis=1)[None]
    grads["w_q_up"] = _compact_heads(dwq_p, 1, 0, QK_NOPE + QK_ROPE)[None]
    dwk3 = _compact_heads(dwk_p, 1, 0, QK_NOPE).reshape(KV_LORA, HEADS, QK_NOPE)
    dwv3 = _compact_heads(dwv_p, 1, 0, V_HEAD).reshape(KV_LORA, HEADS, V_HEAD)
    grads["w_kv_up"] = jnp.concatenate([dwk3, dwv3], axis=2).reshape(1, KV_LORA, HEADS * (QK_NOPE + V_HEAD))
    grads["w_glu"] = dwglu[None]
    grads["w_out"] = jnp.concatenate([_compact_heads(dwoa, 0, 0, V_HEAD), dwos], axis=0)[None]
    grads["meta_tokens"] = dh[:N_META]
    grads["pre_norm_w"] = vec_in[0:1]
    grads["post_norm_w"] = vec_mid[0:1]
    grads["q_norm_w"] = vec_prep[0:1]
    grads["kv_norm_w"] = vec_prep[1:2, :KV_LORA]
    grads["attn_out_norm_w"] = _compact_heads(vec_mid[1:2], 1, 0, V_HEAD)
    grads["ssm_out_norm_w"] = vec_mid[2:3, :D_SSM]
    grads["ssm_d"] = vec_mid[3:4, :D_SSM]
    grads["b_glu"] = vec_mid[4:5]
    d_abar_re = jnp.stack([dssm[d][4][0].sum(axis=0).reshape(N_GROUPS, SSM_STATE) for d in range(2)])
    d_abar_im = jnp.stack([dssm[d][4][1].sum(axis=0).reshape(N_GROUPS, SSM_STATE) for d in range(2)])
    d_bbar_re = jnp.stack([jnp.swapaxes(_block_diag_extract(dssm[d][0], SSM_GROUP, SSM_STATE), 1, 2) for d in range(2)])
    d_bbar_im = jnp.stack([jnp.swapaxes(_block_diag_extract(dssm[d][1], SSM_GROUP, SSM_STATE), 1, 2) for d in range(2)])
    da_re, da_im, dlog_dt, db_re, db_im = disc_vjp((d_abar_re, d_abar_im, d_bbar_re, d_bbar_im))
    grads["ssm_a_re"], grads["ssm_a_im"], grads["ssm_log_dt"] = da_re[None], da_im[None], dlog_dt[None]
    grads["ssm_b_re"], grads["ssm_b_im"] = db_re[None], db_im[None]
    grads["ssm_c_re"] = jnp.stack([_block_diag_extract(dssm[d][2], SSM_GROUP, SSM_STATE) for d in range(2)])[None]
    grads["ssm_c_im"] = jnp.stack([_block_diag_extract(dssm[d][3], SSM_GROUP, SSM_STATE) for d in range(2)])[None]

    def shard_of(n, a, kk):
        return a[:, kk * 256:(kk + 1) * 256] if n == "w_out" else _shard_cols(a, kk)

    slices = [_pack({n: shard_of(n, grads[n], kk) for n in BIG}, BIG) for kk in range(4)]
    small = _pack({n: grads[n] for n in SMALL}, SMALL)
    rs, rsm = slices[0].shape[0], small.shape[0]
    g_pack = jnp.concatenate(slices + [small], axis=0)
    g_pair = _pair_sum(g_pack, _swap_sibling(g_pack))
    parts = _scatter_chips(g_pair, rs, rsm)

    order = BIG + SMALL
    big_shapes = {n: shapes[n] for n in BIG}
    small_shapes = {n: shapes[n] for n in SMALL}

    def pack_state(named):
        return jnp.concatenate([_pack({n: named[n] for n in BIG}, BIG), _pack({n: named[n] for n in SMALL}, SMALL)], axis=0)

    g_out, d_out, m_out, v_out = _adamw(parts, pack_state(local), pack_state(mom_m), pack_state(mom_v))

    def unpack_state(p):
        out = _unpack(p[:rs], big_shapes, BIG)
        out.update(_unpack(p[rs:], small_shapes, SMALL))
        return out

    g_fin, d_fin, m_fin, v_fin = unpack_state(g_out), unpack_state(d_out), unpack_state(m_out), unpack_state(v_out)
    loss = lax.psum(vec_mid[5, 0], ("x", "y", "c"))
    grad_x = dh[N_META:l_real][None]
    return (loss, grad_x, *[g_fin[n] for n in WEIGHTS], *[d_fin[n] for n in WEIGHTS], *[m_fin[n] for n in WEIGHTS],
            *[v_fin[n] for n in WEIGHTS])
```

```python
import functools
import math

import jax
import jax.numpy as jnp
from jax import lax
from jax.experimental import pallas as pl
from jax.experimental.pallas import tpu as pltpu

F32 = jnp.float32
BF16 = jnp.bfloat16
MESH = pl.DeviceIdType.MESH

D_MODEL = 1024
N_META = 16
EPS = 1e-6
HEADS = 8
QK_NOPE = 64
QK_ROPE = 32
V_HEAD = 64
Q_LORA = 256
KV_LORA = 128
D_ATTN = 512
D_SSM = 512
SSM_GROUP = 16
N_GROUPS = 32
SSM_STATE = 64
N_STATE = N_GROUPS * SSM_STATE
ROPE_THETA = 10000.0
HEAD_PAD = 128
D_EXP = HEADS * HEAD_PAD
MASK_LANE = QK_NOPE + QK_ROPE
NEG_BIG = -1e30
SCALE = 1.0 / math.sqrt(QK_NOPE + QK_ROPE)
LOG2E = math.log2(math.e)
SCALE2 = SCALE * LOG2E
QBLK = 256
SCAN_COLS = 512
SSM_BLOCKS = 4
BLK_CH = D_SSM // SSM_BLOCKS
BLK_ST = N_STATE // SSM_BLOCKS

P_GATE_A = (0, 1024)
P_U = (1024, 512)
P_GATE_S = (1536, 512)
P_QLAT = (2048, 256)
P_KVLAT = (2304, 128)
P_KROPE = (2432, 128)
D_PROJ = 2560

ADAM_LR = 0.001
ADAM_B1 = 0.9
ADAM_B2 = 0.999
ADAM_EPS = 1e-08
ADAM_WD = 0.01
ADAM_STEP = 10

VMEM_LIMIT = 60 * 1024 * 1024

BIG = ("w_in", "w_q_up", "w_kv_up", "w_glu", "w_out", "meta_tokens")
SMALL = ("pre_norm_w", "post_norm_w", "q_norm_w", "kv_norm_w", "attn_out_norm_w", "ssm_a_re", "ssm_a_im",
         "ssm_log_dt", "ssm_b_re", "ssm_b_im", "ssm_c_re", "ssm_c_im", "ssm_d", "b_glu", "ssm_out_norm_w")
WEIGHTS = ("meta_tokens", "pre_norm_w", "post_norm_w", "w_in", "q_norm_w", "w_q_up", "kv_norm_w", "w_kv_up",
           "attn_out_norm_w", "ssm_a_re", "ssm_a_im", "ssm_log_dt", "ssm_b_re", "ssm_b_im", "ssm_c_re", "ssm_c_im",
           "ssm_d", "w_glu", "b_glu", "ssm_out_norm_w", "w_out")


def _cparams(sem=None):
    return pltpu.CompilerParams(dimension_semantics=sem, vmem_limit_bytes=VMEM_LIMIT)


def _dot(a, b):
    return jnp.dot(a, b, preferred_element_type=F32)


def _dot_nt(a, b):
    return lax.dot_general(a, b, (((1,), (1,)), ((), ())), preferred_element_type=F32)


def _dot_tn(a, b):
    return lax.dot_general(a, b, (((0,), (0,)), ((), ())), preferred_element_type=F32)


def _sigmoid(x):
    return 1.0 / (1.0 + jnp.exp(-x))


def _rms_fwd(x, w, n):
    r = lax.rsqrt(jnp.sum(x * x, axis=-1, keepdims=True) * (1.0 / n) + EPS)
    return x * r * w, r


def _rms_bwd(x, r, w, dy, n):
    dyw = dy * w
    dx = r * dyw - x * (r * r * r) * (jnp.sum(dyw * x, axis=-1, keepdims=True) * (1.0 / n))
    dw = jnp.sum(dy * (x * r), axis=0, keepdims=True)
    return dx, dw


def _rope_apply(x, cos, sina, sinb):
    return x * cos + pltpu.roll(x, 16, 1) * sina + pltpu.roll(x, HEAD_PAD - 16, 1) * sinb


def _rope_transpose(g, cos, sina, sinb):
    return g * cos + pltpu.roll(g * sina, HEAD_PAD - 16, 1) + pltpu.roll(g * sinb, 16, 1)


def _row_tile(lp):
    return 640 if lp % 640 == 0 else 128


def _ssm_tile(lp):
    return 320 if lp % 320 == 0 else 128


def _rows(tm, off_width):
    off, width = off_width
    return pl.BlockSpec((tm, width), lambda i: (i, off // width))


def _whole(shape, single=True):
    nd = len(shape)
    if single:
        return pl.BlockSpec(shape, lambda *_: (0,) * nd, pipeline_mode=pl.Buffered(1))
    return pl.BlockSpec(shape, lambda *_: (0,) * nd)


def _out_whole(shape):
    return _whole(shape, single=False)


def _pick_tile(rows, cap):
    best = 8
    for t in range(8, cap + 1, 8):
        if rows % t == 0:
            best = t
    return best


def _in_proj_fwd(h, pre_w, w_in_p):
    lp = h.shape[0]
    tm = _row_tile(lp)

    def body(h_ref, w_ref, win_ref, proj_ref):
        xn, _ = _rms_fwd(h_ref[...], w_ref[...], D_MODEL)
        proj_ref[...] = _dot(xn.astype(BF16), win_ref[...])

    return pl.pallas_call(
        body, name="in_proj_fwd", grid=(lp // tm,),
        in_specs=[_rows(tm, (0, D_MODEL)), _whole((1, D_MODEL)), _whole((D_MODEL, D_PROJ))],
        out_specs=_rows(tm, (0, D_PROJ)),
        out_shape=jax.ShapeDtypeStruct((lp, D_PROJ), F32),
        compiler_params=_cparams(("parallel",)),
    )(h, pre_w, w_in_p)


def _attn_prep_fwd(proj, q_norm_w, kv_norm_w, wq_p, wk_p, wv_p, wv_t, cos, sina, sinb, l_real):
    lp = proj.shape[0]
    tm = _row_tile(lp)

    def body(ql_ref, kvl_ref, kr_ref, qw_ref, kw_ref, wq_ref, wk_ref, wv_ref, wvt_ref, cos_ref, sa_ref, sb_ref,
             q_ref, k_ref, v_ref, vt_ref):
        cos_t, sa_t, sb_t = cos_ref[...], sa_ref[...], sb_ref[...]
        qn, _ = _rms_fwd(ql_ref[...], qw_ref[...], Q_LORA)
        kvn, _ = _rms_fwd(kvl_ref[...], kw_ref[...], KV_LORA)
        kvn_b = kvn.astype(BF16)
        qp = _dot(qn.astype(BF16), wq_ref[...])
        kp = _dot(kvn_b, wk_ref[...])
        v_ref[...] = _dot(kvn_b, wv_ref[...]).astype(BF16)
        vt_ref[...] = _dot_nt(wvt_ref[...], kvn_b).astype(BF16)
        lane = lax.broadcasted_iota(jnp.int32, (tm, HEAD_PAD), 1)
        row = lax.broadcasted_iota(jnp.int32, (tm, HEAD_PAD), 0) + pl.program_id(0) * tm
        q_one = jnp.where(lane == MASK_LANE, 1.0, 0.0)
        k_add = _rope_apply(kr_ref[...], cos_t, sa_t, sb_t) + jnp.where((lane == MASK_LANE) & (row >= l_real), NEG_BIG, 0.0)
        for hd in range(HEADS):
            blk = slice(hd * HEAD_PAD, (hd + 1) * HEAD_PAD)
            q_ref[:, blk] = (_rope_apply(qp[:, blk], cos_t, sa_t, sb_t) * SCALE2 + q_one).astype(BF16)
            k_ref[:, blk] = (kp[:, blk] + k_add).astype(BF16)

    tab = _rows(tm, (0, HEAD_PAD))
    out = jax.ShapeDtypeStruct((lp, D_EXP), BF16)
    return pl.pallas_call(
        body, name="attn_prep_fwd", grid=(lp // tm,),
        in_specs=[_rows(tm, P_QLAT), _rows(tm, P_KVLAT), _rows(tm, P_KROPE), _whole((1, Q_LORA)), _whole((1, KV_LORA)),
                  _whole((Q_LORA, D_EXP)), _whole((KV_LORA, D_EXP)), _whole((KV_LORA, D_EXP)), _whole((D_ATTN, KV_LORA)),
                  tab, tab, tab],
        out_specs=[_rows(tm, (0, D_EXP))] * 3 + [pl.BlockSpec((D_ATTN, tm), lambda i: (0, i))],
        out_shape=[out, out, out, jax.ShapeDtypeStruct((D_ATTN, lp), BF16)],
        compiler_params=_cparams(("parallel",)),
    )(proj, proj, proj, q_norm_w, kv_norm_w, wq_p, wk_p, wv_p, wv_t, cos, sina, sinb)


def _flash_fwd(q, k, vt):
    lp = q.shape[0]
    tq = 1280 if lp % 1280 == 0 else 256
    tk = QBLK
    nk = lp // tk

    def body(q_ref, k_ref, vt_ref, o_ref, lse_ref, acc, m_s, l_s):
        acc[...] = jnp.zeros_like(acc)
        m_s[...] = jnp.full(m_s.shape, NEG_BIG, F32)
        l_s[...] = jnp.zeros_like(l_s)

        def step(j, _):
            ks = pl.multiple_of(j * tk, tk)
            kt = k_ref[pl.ds(ks, tk), :]
            vt_t = vt_ref[:, pl.ds(ks, tk)]
            m_old, l_old, acc_old = m_s[...], l_s[...], acc[...]
            blocks = [slice(c * QBLK, (c + 1) * QBLK) for c in range(tq // QBLK)]
            s = [_dot_nt(kt, q_ref[cols, :]) for cols in blocks]
            m_new = [jnp.maximum(m_old[:, cols], jnp.max(s_c, axis=0, keepdims=True)) for cols, s_c in zip(blocks, s)]
            p = [jnp.exp2(s_c - m_c) for s_c, m_c in zip(s, m_new)]
            pv = [_dot(vt_t, p_c.astype(BF16)) for p_c in p]
            m_new = jnp.concatenate(m_new, axis=1)
            alpha = jnp.exp2(m_old - m_new)
            l_s[...] = alpha * l_old + jnp.concatenate([jnp.sum(p_c, axis=0, keepdims=True) for p_c in p], axis=1)
            acc[...] = alpha * acc_old + jnp.concatenate(pv, axis=1)
            m_s[...] = m_new
            return 0

        lax.fori_loop(0, nk, step, 0, unroll=5 if nk % 5 == 0 else 1)
        o_t = acc[...] / l_s[...]
        o_ref[...] = jnp.concatenate([o_t, jnp.zeros_like(o_t)], axis=0).T
        lse_ref[...] = m_s[...] + jnp.log2(l_s[...])

    return pl.pallas_call(
        body, name="flash_fwd", grid=(HEADS, lp // tq),
        in_specs=[pl.BlockSpec((tq, HEAD_PAD), lambda hd, i: (i, hd)),
                  pl.BlockSpec((lp, HEAD_PAD), lambda hd, i: (0, hd)),
                  pl.BlockSpec((V_HEAD, lp), lambda hd, i: (hd, 0))],
        out_specs=[pl.BlockSpec((tq, HEAD_PAD), lambda hd, i: (i, hd)),
                   pl.BlockSpec((None, 1, tq), lambda hd, i: (hd, 0, i))],
        out_shape=[jax.ShapeDtypeStruct((lp, D_EXP), F32), jax.ShapeDtypeStruct((HEADS, 1, lp), F32)],
        scratch_shapes=[pltpu.VMEM((V_HEAD, tq), F32), pltpu.VMEM((1, tq), F32), pltpu.VMEM((1, tq), F32)],
        compiler_params=_cparams(("parallel", "parallel")),
    )(q, k, vt)


def _scan_rows(xr_ref, xi_ref, base, n_rows, coef_ref, carry_ref, reverse, tile_fn=None, acc_refs=()):
    n_tiles = n_rows // 8
    shifts = (7, 6, 4) if reverse else (1, 2, 4)
    for cg in range(N_STATE // SCAN_COLS):
        cols = slice(cg * SCAN_COLS, (cg + 1) * SCAN_COLS)
        co = [coef_ref[k, :, cols] for k in range(8)]

        def step(t, carry, cols=cols, co=co):
            cr, ci = carry[0], carry[1]
            tt = (n_tiles - 1 - t) if reverse else t
            start = pl.multiple_of(base + tt * 8, 8)
            tr = xr_ref[pl.ds(start, 8), cols]
            ti = xi_ref[pl.ds(start, 8), cols]
            for lvl in range(3):
                ar, ai = co[2 * lvl], co[2 * lvl + 1]
                sr = pltpu.roll(tr, shifts[lvl], 0)
                si = pltpu.roll(ti, shifts[lvl], 0)
                tr, ti = tr + ar * sr - ai * si, ti + ar * si + ai * sr
            tr, ti = tr + co[6] * cr - co[7] * ci, ti + co[6] * ci + co[7] * cr
            xr_ref[pl.ds(start, 8), cols] = tr
            xi_ref[pl.ds(start, 8), cols] = ti
            accs = carry[2:]
            if tile_fn is not None:
                accs = tuple(a + d for a, d in zip(accs, tile_fn(start, cols, tr, ti)))
            new_c = (tr[0:1], ti[0:1]) if reverse else (tr[7:8], ti[7:8])
            return new_c + accs

        init = (carry_ref[0:1, cols], carry_ref[1:2, cols]) + tuple(a[:, cols] for a in acc_refs)
        out = lax.fori_loop(0, n_tiles, step, init)
        carry_ref[0:1, cols] = out[0]
        carry_ref[1:2, cols] = out[1]
        for a, val in zip(acc_refs, out[2:]):
            a[:, cols] = val


def _ssm_fwd(proj, coef, b_re, b_im, c_re, c_im_neg, reverse):
    lp = proj.shape[0]
    t = _ssm_tile(lp)
    n = lp // t
    order = (lambda i: n - 1 - i) if reverse else (lambda i: i)

    def body(u_ref, coef_ref, bre_ref, bim_ref, cre_ref, cim_ref, y_ref, st_ref, xr, xi, carry):
        @pl.when(pl.program_id(0) == 0)
        def _():
            carry[...] = jnp.zeros_like(carry)

        st_ref[...] = carry[0:2, :]
        ub = u_ref[...].astype(BF16)
        for j in range(SSM_BLOCKS):
            ch, stt = slice(j * BLK_CH, (j + 1) * BLK_CH), slice(j * BLK_ST, (j + 1) * BLK_ST)
            xr[:, stt] = _dot(ub[:, ch], bre_ref[j])
            xi[:, stt] = _dot(ub[:, ch], bim_ref[j])
        _scan_rows(xr, xi, 0, t, coef_ref, carry, reverse)
        for j in range(SSM_BLOCKS):
            ch, stt = slice(j * BLK_CH, (j + 1) * BLK_CH), slice(j * BLK_ST, (j + 1) * BLK_ST)
            y_ref[:, ch] = _dot(xr[:, stt].astype(BF16), cre_ref[j]) + _dot(xi[:, stt].astype(BF16), cim_ref[j])

    wb, wc = _whole((SSM_BLOCKS, BLK_CH, BLK_ST)), _whole((SSM_BLOCKS, BLK_ST, BLK_CH))
    return pl.pallas_call(
        body, name="ssm_fwd_rev" if reverse else "ssm_fwd", grid=(n,),
        in_specs=[pl.BlockSpec((t, D_SSM), lambda i: (order(i), P_U[0] // D_SSM)), _whole((8, 8, N_STATE)), wb, wb, wc, wc],
        out_specs=[pl.BlockSpec((t, D_SSM), lambda i: (order(i), 0)),
                   pl.BlockSpec((None, 2, N_STATE), lambda i: (order(i), 0, 0))],
        out_shape=[jax.ShapeDtypeStruct((lp, D_SSM), F32), jax.ShapeDtypeStruct((n, 2, N_STATE), F32)],
        scratch_shapes=[pltpu.VMEM((t, N_STATE), F32), pltpu.VMEM((t, N_STATE), F32), pltpu.VMEM((8, N_STATE), F32)],
        compiler_params=_cparams(("arbitrary",)),
    )(proj, coef, b_re, b_im, c_re, c_im_neg)


GELU_C0 = math.sqrt(2.0 / math.pi)
GELU_C1 = 0.044715


def _mid(h, tgt, o_exp, proj, y0, y1, ssm_d, w_glu, w_glu_t, b_glu, ssm_norm_w, attn_norm_w_e, w_out_a, w_out_s,
         w_out_a_t, w_out_s_t, post_w, l_real):
    lp = h.shape[0]
    tm = 128

    def body(h_ref, tgt_ref, o_ref, ga_ref, u_ref, sg_ref, y0_ref, y1_ref, d_ref, wg_ref, wgt_ref, bg_ref, ws_ref, wa_ref,
             woa_ref, wos_ref, woat_ref, wost_ref, pw_ref,
             do_ref, delta_ref, dga_ref, dyp_ref, dsg_ref, dres_ref, dwoa_ref, dwos_ref, dwg_ref, vec_ref):
        @pl.when(pl.program_id(0) == 0)
        def _():
            dwoa_ref[...] = jnp.zeros_like(dwoa_ref)
            dwos_ref[...] = jnp.zeros_like(dwos_ref)
            dwg_ref[...] = jnp.zeros_like(dwg_ref)
            vec_ref[...] = jnp.zeros_like(vec_ref)

        u = u_ref[...]
        ypre = y0_ref[...] + y1_ref[...] + d_ref[...] * u
        th = jnp.tanh(GELU_C0 * (ypre + GELU_C1 * ypre * ypre * ypre))
        gel = 0.5 * ypre * (1.0 + th)
        gel_b = gel.astype(BF16)
        glu = _dot(gel_b, wg_ref[...]) + bg_ref[...]
        g1, g2 = glu[:, :D_SSM], glu[:, D_SSM:]
        sig2 = _sigmoid(g2)
        z = g1 * sig2
        sg = sg_ref[...]
        sgs = _sigmoid(sg)
        sil_s = sg * sgs
        s = z * sil_s
        ys, r_s = _rms_fwd(s, ws_ref[...], D_SSM)

        o = o_ref[...]
        ga = ga_ref[...]
        gas = _sigmoid(ga)
        sil_a = ga * gas
        a = o * sil_a
        ya, r_a = _rms_fwd(a, wa_ref[...], D_ATTN)

        ya_b, ys_b = ya.astype(BF16), ys.astype(BF16)
        y = _dot(ya_b, woa_ref[...]) + _dot(ys_b, wos_ref[...])
        yn, r_y = _rms_fwd(y, pw_ref[...], D_MODEL)
        row = lax.broadcasted_iota(jnp.int32, (tm, 1), 0) + pl.program_id(0) * tm
        valid = ((row >= N_META) & (row < l_real)).astype(F32)
        err = (h_ref[...] + yn - tgt_ref[...]) * valid
        loss = 0.5 * jnp.sum(jnp.sum(err * err, axis=-1, keepdims=True), axis=0, keepdims=True) * (1.0 / D_MODEL)
        dout = err * (1.0 / D_MODEL)
        dres_ref[...] = dout

        dy, d_pw = _rms_bwd(y, r_y, pw_ref[...], dout, D_MODEL)
        dy_b = dy.astype(BF16)
        dya = _dot(dy_b, woat_ref[...])
        dys = _dot(dy_b, wost_ref[...])
        dwoa_ref[...] += _dot_tn(ya_b, dy_b)
        dwos_ref[...] += _dot_tn(ys_b, dy_b)

        da, d_wa = _rms_bwd(a, r_a, wa_ref[...], dya, D_ATTN)
        d_o = da * sil_a
        dga_ref[...] = da * o * (gas * (1.0 + ga * (1.0 - gas)))
        do_ref[...] = d_o.astype(BF16)
        prod = d_o * o
        lane8 = lax.broadcasted_iota(jnp.int32, (tm, HEADS), 1)
        delta = jnp.zeros((tm, HEADS), F32)
        for hd in range(HEADS):
            delta = jnp.where(lane8 == hd, jnp.sum(prod[:, hd * HEAD_PAD:(hd + 1) * HEAD_PAD], axis=-1, keepdims=True), delta)
        delta_ref[...] = delta

        ds, d_ws = _rms_bwd(s, r_s, ws_ref[...], dys, D_SSM)
        dz = ds * sil_s
        dsg_ref[...] = ds * z * (sgs * (1.0 + sg * (1.0 - sgs)))
        dglu = jnp.concatenate([dz * sig2, dz * g1 * sig2 * (1.0 - sig2)], axis=-1)
        dglu_b = dglu.astype(BF16)
        dwg_ref[...] += _dot_tn(gel_b, dglu_b)
        dgel = _dot(dglu_b, wgt_ref[...])
        dgelu = 0.5 * (1.0 + th) + 0.5 * ypre * (1.0 - th * th) * (GELU_C0 * (1.0 + 3.0 * GELU_C1 * ypre * ypre))
        dyp = dgel * dgelu
        dyp_ref[...] = dyp

        vec_ref[0:1, :] += d_pw
        vec_ref[1:2, :] += d_wa
        vec_ref[2:3, 0:D_SSM] += d_ws
        vec_ref[3:4, 0:D_SSM] += jnp.sum(dyp * u, axis=0, keepdims=True)
        vec_ref[4:5, :] += jnp.sum(dglu, axis=0, keepdims=True)
        vec_ref[5:6, :] += jnp.broadcast_to(loss, (1, D_MODEL))

    full = lambda off: _rows(tm, (off, D_MODEL))
    half = lambda off: _rows(tm, (off, D_SSM))
    return pl.pallas_call(
        body, name="mid", grid=(lp // tm,),
        in_specs=[full(0), full(0), full(0), _rows(tm, P_GATE_A), _rows(tm, P_U), _rows(tm, P_GATE_S), half(0), half(0),
                  _whole((1, D_SSM)), _whole((D_SSM, 2 * D_SSM)), _whole((2 * D_SSM, D_SSM)), _whole((1, 2 * D_SSM)),
                  _whole((1, D_SSM)), _whole((1, D_EXP)), _whole((D_EXP, D_MODEL)), _whole((D_SSM, D_MODEL)),
                  _whole((D_MODEL, D_EXP)), _whole((D_MODEL, D_SSM)), _whole((1, D_MODEL))],
        out_specs=[full(0), _rows(tm, (0, HEADS)), full(0), half(0), half(0), full(0),
                   _out_whole((D_EXP, D_MODEL)), _out_whole((D_SSM, D_MODEL)), _out_whole((D_SSM, 2 * D_SSM)),
                   _out_whole((8, D_MODEL))],
        out_shape=[jax.ShapeDtypeStruct((lp, D_EXP), BF16), jax.ShapeDtypeStruct((lp, HEADS), F32),
                   jax.ShapeDtypeStruct((lp, D_EXP), F32), jax.ShapeDtypeStruct((lp, D_SSM), F32),
                   jax.ShapeDtypeStruct((lp, D_SSM), F32), jax.ShapeDtypeStruct((lp, D_MODEL), F32),
                   jax.ShapeDtypeStruct((D_EXP, D_MODEL), F32), jax.ShapeDtypeStruct((D_SSM, D_MODEL), F32),
                   jax.ShapeDtypeStruct((D_SSM, 2 * D_SSM), F32), jax.ShapeDtypeStruct((8, D_MODEL), F32)],
        compiler_params=_cparams(("arbitrary",)),
    )(h, tgt, o_exp, proj, proj, proj, y0, y1, ssm_d, w_glu, w_glu_t, b_glu, ssm_norm_w, attn_norm_w_e, w_out_a, w_out_s,
      w_out_a_t, w_out_s_t, post_w)


def _ssm_bwd(proj, dyp, states, coef, coef_adj, b_re, b_im, b_re_t, b_im_t, c_re_t, c_im_neg_t, reverse):
    lp = proj.shape[0]
    t = _ssm_tile(lp)
    n = lp // t
    order = (lambda i: i) if reverse else (lambda i: n - 1 - i)
    edge = (t + 8) if reverse else 7

    def body(u_ref, dy_ref, st_ref, coef_ref, coefa_ref, bre_ref, bim_ref, bret_ref, bimt_ref, cret_ref, cimt_ref,
             du_ref, dbre_ref, dbim_ref, dcre_ref, dcim_ref, da_ref, xr, xi, gr, gi, carry_x, carry_g):
        @pl.when(pl.program_id(0) == 0)
        def _():
            carry_g[...] = jnp.zeros_like(carry_g)
            carry_x[...] = jnp.zeros_like(carry_x)
            dbre_ref[...] = jnp.zeros_like(dbre_ref)
            dbim_ref[...] = jnp.zeros_like(dbim_ref)
            dcre_ref[...] = jnp.zeros_like(dcre_ref)
            dcim_ref[...] = jnp.zeros_like(dcim_ref)
            da_ref[...] = jnp.zeros_like(da_ref)
            for halo in (slice(0, 8), slice(t + 8, t + 16)):
                xr[halo, :] = jnp.zeros((8, N_STATE), F32)
                xi[halo, :] = jnp.zeros((8, N_STATE), F32)

        ub = u_ref[...].astype(BF16)
        dyb = dy_ref[...].astype(BF16)
        carry_x[0:2, :] = st_ref[...]
        xr[edge:edge + 1, :] = st_ref[0:1, :]
        xi[edge:edge + 1, :] = st_ref[1:2, :]
        blocks = [(slice(j * BLK_CH, (j + 1) * BLK_CH), slice(j * BLK_ST, (j + 1) * BLK_ST)) for j in range(SSM_BLOCKS)]
        for j, (ch, stt) in enumerate(blocks):
            xr[8:t + 8, stt] = _dot(ub[:, ch], bre_ref[j])
            xi[8:t + 8, stt] = _dot(ub[:, ch], bim_ref[j])
            gr[:, stt] = _dot(dyb[:, ch], cret_ref[j])
            gi[:, stt] = _dot(dyb[:, ch], cimt_ref[j])
        _scan_rows(xr, xi, 8, t, coef_ref, carry_x, reverse)

        row8 = lax.broadcasted_iota(jnp.int32, (8, SCAN_COLS), 0)

        def tile_fn(start, cols, g_re, g_im):
            xs = pl.multiple_of(start + 8, 8)
            if reverse:
                nb = pl.multiple_of(start + 16, 8)
                xn_r = jnp.where(row8 == 7, xr[pl.ds(nb, 8), cols][0:1], pltpu.roll(xr[pl.ds(xs, 8), cols], 7, 0))
                xn_i = jnp.where(row8 == 7, xi[pl.ds(nb, 8), cols][0:1], pltpu.roll(xi[pl.ds(xs, 8), cols], 7, 0))
            else:
                nb = pl.multiple_of(start, 8)
                xn_r = jnp.where(row8 == 0, xr[pl.ds(nb, 8), cols][7:8], pltpu.roll(xr[pl.ds(xs, 8), cols], 1, 0))
                xn_i = jnp.where(row8 == 0, xi[pl.ds(nb, 8), cols][7:8], pltpu.roll(xi[pl.ds(xs, 8), cols], 1, 0))
            return g_re * xn_r + g_im * xn_i, g_im * xn_r - g_re * xn_i

        _scan_rows(gr, gi, 0, t, coefa_ref, carry_g, not reverse, tile_fn=tile_fn, acc_refs=(da_ref.at[0], da_ref.at[1]))

        for j, (ch, stt) in enumerate(blocks):
            g_re_b, g_im_b = gr[:, stt].astype(BF16), gi[:, stt].astype(BF16)
            du_ref[:, ch] = _dot(g_re_b, bret_ref[j]) + _dot(g_im_b, bimt_ref[j])
            dbre_ref[j] += _dot_tn(ub[:, ch], g_re_b)
            dbim_ref[j] += _dot_tn(ub[:, ch], g_im_b)
            dcre_ref[j] += _dot_tn(dyb[:, ch], xr[8:t + 8, stt].astype(BF16))
            dcim_ref[j] -= _dot_tn(dyb[:, ch], xi[8:t + 8, stt].astype(BF16))

    dense = jax.ShapeDtypeStruct((SSM_BLOCKS, BLK_CH, BLK_ST), F32)
    wb, wc = _whole((SSM_BLOCKS, BLK_CH, BLK_ST)), _whole((SSM_BLOCKS, BLK_ST, BLK_CH))
    acc = _out_whole((SSM_BLOCKS, BLK_CH, BLK_ST))
    return pl.pallas_call(
        body, name="ssm_bwd_rev" if reverse else "ssm_bwd", grid=(n,),
        in_specs=[pl.BlockSpec((t, D_SSM), lambda i: (order(i), P_U[0] // D_SSM)),
                  pl.BlockSpec((t, D_SSM), lambda i: (order(i), 0)),
                  pl.BlockSpec((None, 2, N_STATE), lambda i: (order(i), 0, 0)),
                  _whole((8, 8, N_STATE)), _whole((8, 8, N_STATE)), wb, wb, wc, wc, wb, wb],
        out_specs=[pl.BlockSpec((t, D_SSM), lambda i: (order(i), 0)), acc, acc, acc, acc, _out_whole((2, 8, N_STATE))],
        out_shape=[jax.ShapeDtypeStruct((lp, D_SSM), F32), dense, dense, dense, dense,
                   jax.ShapeDtypeStruct((2, 8, N_STATE), F32)],
        scratch_shapes=[pltpu.VMEM((t + 16, N_STATE), F32), pltpu.VMEM((t + 16, N_STATE), F32),
                        pltpu.VMEM((t, N_STATE), F32), pltpu.VMEM((t, N_STATE), F32),
                        pltpu.VMEM((8, N_STATE), F32), pltpu.VMEM((8, N_STATE), F32)],
        compiler_params=_cparams(("arbitrary",)),
    )(proj, dyp, states, coef, coef_adj, b_re, b_im, b_re_t, b_im_t, c_re_t, c_im_neg_t)


def _flash_bwd(q, k, v, d_o, q_t, k_t, do_t, lse_row, delta_row):
    lp = q.shape[0]
    tq = 1280 if lp % 1280 == 0 else 256
    tk = QBLK
    nk = lp // tk
    d_qk = QK_NOPE + QK_ROPE

    def body(q_ref, do_ref, qt_ref, dot_ref, lse_ref, delta_ref, k_ref, v_ref, kt_ref, dq_ref, dk_ref, dv_ref, dq_acc):
        @pl.when(pl.program_id(1) == 0)
        def _():
            dk_ref[...] = jnp.zeros_like(dk_ref)
            dv_ref[...] = jnp.zeros_like(dv_ref)

        dq_acc[...] = jnp.zeros_like(dq_acc)
        lse, delta = lse_ref[...], delta_ref[...]
        q_cols, do_cols = qt_ref[...], dot_ref[...]
        blocks = [slice(c * QBLK, (c + 1) * QBLK) for c in range(tq // QBLK)]

        def step(j, _):
            ks = pl.multiple_of(j * tk, tk)
            k_rows, v_rows = k_ref[pl.ds(ks, tk), :], v_ref[pl.ds(ks, tk), :]
            dq_old = dq_acc[...]
            st = [_dot_nt(k_rows, q_ref[cols, :]) for cols in blocks]
            dpt = [_dot_nt(v_rows, do_ref[cols, :]) for cols in blocks]
            pt = [jnp.exp2(s_c - lse[:, cols]) for s_c, cols in zip(st, blocks)]
            dst = [p_c * (dp_c - delta[:, cols]) for p_c, dp_c, cols in zip(pt, dpt, blocks)]
            pt_b = jnp.concatenate([p_c.astype(BF16) for p_c in pt], axis=1)
            dst_b = jnp.concatenate([d_c.astype(BF16) for d_c in dst], axis=1)
            dv_ref[:, pl.ds(ks, tk)] += _dot_nt(do_cols, pt_b)
            dk_ref[:, pl.ds(ks, tk)] += _dot_nt(q_cols, dst_b) * (1.0 / LOG2E)
            dq_acc[...] = dq_old + _dot(kt_ref[:, pl.ds(ks, tk)], dst_b)
            return 0

        lax.fori_loop(0, nk, step, 0, unroll=5 if nk % 5 == 0 else 1)
        dq_ref[...] = dq_acc[...]

    tile = pl.BlockSpec((tq, HEAD_PAD), lambda hd, i: (i, hd))
    head = pl.BlockSpec((lp, HEAD_PAD), lambda hd, i: (0, hd))
    rowv = pl.BlockSpec((None, 1, tq), lambda hd, i: (hd, 0, i))
    return pl.pallas_call(
        body, name="flash_bwd", grid=(HEADS, lp // tq),
        in_specs=[tile, tile, pl.BlockSpec((d_qk, tq), lambda hd, i: (hd, i)), pl.BlockSpec((V_HEAD, tq), lambda hd, i: (hd, i)),
                  rowv, rowv, head, head, pl.BlockSpec((d_qk, lp), lambda hd, i: (hd, 0))],
        out_specs=[pl.BlockSpec((d_qk, tq), lambda hd, i: (hd, i)), pl.BlockSpec((d_qk, lp), lambda hd, i: (hd, 0)),
                   pl.BlockSpec((V_HEAD, lp), lambda hd, i: (hd, 0))],
        out_shape=[jax.ShapeDtypeStruct((HEADS * d_qk, lp), F32), jax.ShapeDtypeStruct((HEADS * d_qk, lp), F32),
                   jax.ShapeDtypeStruct((HEADS * V_HEAD, lp), F32)],
        scratch_shapes=[pltpu.VMEM((d_qk, tq), F32)],
        compiler_params=_cparams(("parallel", "arbitrary")),
    )(q, d_o, q_t, do_t, lse_row, delta_row, k, v, k_t)


def _attn_prep_bwd(dq, dk, dv, proj, q_norm_w, kv_norm_w, wq_pt, wk_pt, wv_pt, cos, sina, sinb):
    lp = proj.shape[0]
    tm = _row_tile(lp)

    def body(dq_ref, dk_ref, dv_ref, ql_ref, kvl_ref, qw_ref, kw_ref, wqt_ref, wkt_ref, wvt_ref, cos_ref, sa_ref, sb_ref,
             dql_ref, dkvl_ref, dkr_ref, dwq_ref, dwk_ref, dwv_ref, vec_ref):
        @pl.when(pl.program_id(0) == 0)
        def _():
            dwq_ref[...] = jnp.zeros_like(dwq_ref)
            dwk_ref[...] = jnp.zeros_like(dwk_ref)
            dwv_ref[...] = jnp.zeros_like(dwv_ref)
            vec_ref[...] = jnp.zeros_like(vec_ref)

        cos_t, sa_t, sb_t = cos_ref[...], sa_ref[...], sb_ref[...]
        dkp = dk_ref[...]
        dqp = jnp.concatenate(
            [_rope_transpose(dq_ref[:, hd * HEAD_PAD:(hd + 1) * HEAD_PAD] * SCALE, cos_t, sa_t, sb_t) for hd in range(HEADS)],
            axis=-1)
        dkr = dkp[:, 0:HEAD_PAD]
        for hd in range(1, HEADS):
            dkr = dkr + dkp[:, hd * HEAD_PAD:(hd + 1) * HEAD_PAD]
        dkr_ref[...] = _rope_transpose(dkr, cos_t, sa_t, sb_t)

        qn, r_q = _rms_fwd(ql_ref[...], qw_ref[...], Q_LORA)
        kvn, r_kv = _rms_fwd(kvl_ref[...], kw_ref[...], KV_LORA)
        dqp_b, dkp_b, dv_b = dqp.astype(BF16), dkp.astype(BF16), dv_ref[...].astype(BF16)
        dqn = _dot(dqp_b, wqt_ref[...])
        dkvn = _dot(dkp_b, wkt_ref[...]) + _dot(dv_b, wvt_ref[...])
        dwq_ref[...] += _dot_tn(qn.astype(BF16), dqp_b)
        dwk_ref[...] += _dot_tn(kvn.astype(BF16), dkp_b)
        dwv_ref[...] += _dot_tn(kvn.astype(BF16), dv_b)
        dql, d_qw = _rms_bwd(ql_ref[...], r_q, qw_ref[...], dqn, Q_LORA)
        dkvl, d_kw = _rms_bwd(kvl_ref[...], r_kv, kw_ref[...], dkvn, KV_LORA)
        dql_ref[...] = dql
        dkvl_ref[...] = dkvl
        vec_ref[0:1, :] += d_qw
        vec_ref[1:2, 0:KV_LORA] += d_kw

    tab = _rows(tm, (0, HEAD_PAD))
    full = _rows(tm, (0, D_EXP))
    return pl.pallas_call(
        body, name="attn_prep_bwd", grid=(lp // tm,),
        in_specs=[full, full, full, _rows(tm, P_QLAT), _rows(tm, P_KVLAT), _whole((1, Q_LORA)), _whole((1, KV_LORA)),
                  _whole((D_EXP, Q_LORA)), _whole((D_EXP, KV_LORA)), _whole((D_EXP, KV_LORA)), tab, tab, tab],
        out_specs=[_rows(tm, (0, Q_LORA)), _rows(tm, (0, KV_LORA)), _rows(tm, (0, HEAD_PAD)),
                   _out_whole((Q_LORA, D_EXP)), _out_whole((KV_LORA, D_EXP)), _out_whole((KV_LORA, D_EXP)),
                   _out_whole((8, Q_LORA))],
        out_shape=[jax.ShapeDtypeStruct((lp, Q_LORA), F32), jax.ShapeDtypeStruct((lp, KV_LORA), F32),
                   jax.ShapeDtypeStruct((lp, HEAD_PAD), F32), jax.ShapeDtypeStruct((Q_LORA, D_EXP), F32),
                   jax.ShapeDtypeStruct((KV_LORA, D_EXP), F32), jax.ShapeDtypeStruct((KV_LORA, D_EXP), F32),
                   jax.ShapeDtypeStruct((8, Q_LORA), F32)],
        compiler_params=_cparams(("arbitrary",)),
    )(dq, dk, dv, proj, proj, q_norm_w, kv_norm_w, wq_pt, wk_pt, wv_pt, cos, sina, sinb)


def _in_proj_bwd(h, pre_w, dres, dga, du0, du1, dyp, ssm_d, dsg, dql, dkvl, dkr, w_in_pt):
    lp = h.shape[0]
    tm = 128
    pieces = (P_GATE_A, P_U, P_GATE_S, P_QLAT, P_KVLAT, P_KROPE)

    def body(h_ref, w_ref, dres_ref, dga_ref, du0_ref, du1_ref, dyp_ref, d_ref, dsg_ref, dql_ref, dkvl_ref, dkr_ref, wt_ref,
             dh_ref, dw_ref, vec_ref):
        @pl.when(pl.program_id(0) == 0)
        def _():
            dw_ref[...] = jnp.zeros_like(dw_ref)
            vec_ref[...] = jnp.zeros_like(vec_ref)

        hv = h_ref[...]
        xn, r = _rms_fwd(hv, w_ref[...], D_MODEL)
        xn_b = xn.astype(BF16)
        du = du0_ref[...] + du1_ref[...] + dyp_ref[...] * d_ref[...]
        grads = (dga_ref[...], du, dsg_ref[...], dql_ref[...], dkvl_ref[...], dkr_ref[...])
        dxn = jnp.zeros((tm, D_MODEL), F32)
        for (off, width), g in zip(pieces, grads):
            g_b = g.astype(BF16)
            dxn = dxn + _dot(g_b, wt_ref[off:off + width, :])
            dw_ref[:, off:off + width] += _dot_tn(xn_b, g_b)
        dx, d_w = _rms_bwd(hv, r, w_ref[...], dxn, D_MODEL)
        dh_ref[...] = dres_ref[...] + dx
        vec_ref[0:1, :] += d_w

    full = _rows(tm, (0, D_MODEL))
    half = _rows(tm, (0, D_SSM))
    return pl.pallas_call(
        body, name="in_proj_bwd", grid=(lp // tm,),
        in_specs=[full, _whole((1, D_MODEL)), full, full, half, half, half, _whole((1, D_SSM)), half,
                  _rows(tm, (0, Q_LORA)), _rows(tm, (0, KV_LORA)), _rows(tm, (0, HEAD_PAD)), _whole((D_PROJ, D_MODEL))],
        out_specs=[full, _out_whole((D_MODEL, D_PROJ)), _out_whole((8, D_MODEL))],
        out_shape=[jax.ShapeDtypeStruct((lp, D_MODEL), F32), jax.ShapeDtypeStruct((D_MODEL, D_PROJ), F32),
                   jax.ShapeDtypeStruct((8, D_MODEL), F32)],
        compiler_params=_cparams(("arbitrary",)),
    )(h, pre_w, dres, dga, du0, du1, dyp, ssm_d, dsg, dql, dkvl, dkr, w_in_pt)


def _other_chips(x, y):
    return [(1 - x, y), (x, 1 - y), (1 - x, 1 - y)]


def _gather_weights(w_bf16, meta):
    any_spec = pl.BlockSpec(memory_space=pl.ANY)

    def body(w_ref, m_ref, wout_ref, mout_ref, send_sems, recv_sems, local_sems):
        x, y, c = lax.axis_index("x"), lax.axis_index("y"), lax.axis_index("c")
        me = 2 * x + y
        own = [pltpu.make_async_copy(w_ref, wout_ref.at[me], local_sems.at[0]),
               pltpu.make_async_copy(m_ref, mout_ref.at[me], local_sems.at[1])]
        for cp in own:
            cp.start()
        sends = []
        for j, (tx, ty) in enumerate(_other_chips(x, y)):
            for n, (src, dst) in enumerate(((w_ref, wout_ref), (m_ref, mout_ref))):
                sends.append(pltpu.make_async_remote_copy(
                    src_ref=src, dst_ref=dst.at[me], send_sem=send_sems.at[2 * j + n], recv_sem=recv_sems.at[2 * j + n],
                    device_id=(tx, ty, c), device_id_type=MESH))
        for cp in sends:
            cp.start()
        for j, (tx, ty) in enumerate(_other_chips(x, y)):
            for n, (src, dst) in enumerate(((w_ref, wout_ref), (m_ref, mout_ref))):
                pltpu.make_async_remote_copy(
                    src_ref=src, dst_ref=dst.at[2 * tx + ty], send_sem=send_sems.at[2 * j + n],
                    recv_sem=recv_sems.at[2 * j + n], device_id=(tx, ty, c), device_id_type=MESH).wait_recv()
        for cp in sends:
            cp.wait_send()
        for cp in own:
            cp.wait()

    return pl.pallas_call(
        body, name="gather_weights",
        in_specs=[any_spec, any_spec], out_specs=[any_spec, any_spec],
        out_shape=[jax.ShapeDtypeStruct((4,) + w_bf16.shape, w_bf16.dtype), jax.ShapeDtypeStruct((4,) + meta.shape, meta.dtype)],
        scratch_shapes=[pltpu.SemaphoreType.DMA((6,)), pltpu.SemaphoreType.DMA((6,)), pltpu.SemaphoreType.DMA((2,))],
    )(w_bf16, meta)


def _swap_sibling(g):
    any_spec = pl.BlockSpec(memory_space=pl.ANY)

    def body(g_ref, out_ref, send_sem, recv_sem):
        x, y, c = lax.axis_index("x"), lax.axis_index("y"), lax.axis_index("c")
        cp = pltpu.make_async_remote_copy(src_ref=g_ref, dst_ref=out_ref, send_sem=send_sem, recv_sem=recv_sem,
                                          device_id=(x, y, 1 - c), device_id_type=MESH)
        cp.start()
        cp.wait()

    return pl.pallas_call(
        body, name="swap_sibling", in_specs=[any_spec], out_specs=any_spec,
        out_shape=jax.ShapeDtypeStruct(g.shape, g.dtype),
        scratch_shapes=[pltpu.SemaphoreType.DMA(()), pltpu.SemaphoreType.DMA(())],
    )(g)


def _pair_sum(a, b):
    rows = a.shape[0]
    tm = _pick_tile(rows, 1024)

    def body(a_ref, b_ref, o_ref):
        o_ref[...] = a_ref[...] + b_ref[...]

    spec = pl.BlockSpec((tm, 1024), lambda i: (i, 0))
    return pl.pallas_call(body, name="pair_sum", grid=(rows // tm,), in_specs=[spec, spec], out_specs=spec,
                          out_shape=jax.ShapeDtypeStruct(a.shape, F32), compiler_params=_cparams(("parallel",)))(a, b)


def _scatter_chips(s, rs, rsm):
    any_spec = pl.BlockSpec(memory_space=pl.ANY)

    def body(s_ref, out_ref, send_sems, recv_sems, local_sems):
        x, y, c = lax.axis_index("x"), lax.axis_index("y"), lax.axis_index("c")
        me = 2 * x + y
        small = s_ref.at[pl.ds(4 * rs, rsm)]

        def pieces(target):
            return ((s_ref.at[pl.ds(pl.multiple_of(target * rs, 8), rs)], pl.ds(0, rs)), (small, pl.ds(rs, rsm)))

        own = [pltpu.make_async_copy(src, out_ref.at[me, rows], local_sems.at[n]) for n, (src, rows) in enumerate(pieces(me))]
        for cp in own:
            cp.start()
        sends = []
        for j, (tx, ty) in enumerate(_other_chips(x, y)):
            for n, (src, rows) in enumerate(pieces(2 * tx + ty)):
                sends.append(pltpu.make_async_remote_copy(
                    src_ref=src, dst_ref=out_ref.at[me, rows], send_sem=send_sems.at[2 * j + n],
                    recv_sem=recv_sems.at[2 * j + n], device_id=(tx, ty, c), device_id_type=MESH))
        for cp in sends:
            cp.start()
        for j, (tx, ty) in enumerate(_other_chips(x, y)):
            for n, (src, rows) in enumerate(pieces(me)):
                pltpu.make_async_remote_copy(
                    src_ref=src, dst_ref=out_ref.at[2 * tx + ty, rows], send_sem=send_sems.at[2 * j + n],
                    recv_sem=recv_sems.at[2 * j + n], device_id=(tx, ty, c), device_id_type=MESH).wait_recv()
        for cp in sends:
            cp.wait_send()
        for cp in own:
            cp.wait()

    return pl.pallas_call(
        body, name="scatter_chips", in_specs=[any_spec], out_specs=any_spec,
        out_shape=jax.ShapeDtypeStruct((4, rs + rsm, 1024), F32),
        scratch_shapes=[pltpu.SemaphoreType.DMA((6,)), pltpu.SemaphoreType.DMA((6,)), pltpu.SemaphoreType.DMA((2,))],
    )(s)


def _adamw(parts, w, m, v):
    rows = w.shape[0]
    tm = _pick_tile(rows, 256)
    c1 = 1.0 / (1.0 - ADAM_B1 ** ADAM_STEP)
    c2 = 1.0 / (1.0 - ADAM_B2 ** ADAM_STEP)

    def body(p_ref, w_ref, m_ref, v_ref, g_ref, d_ref, nm_ref, nv_ref):
        g = ((p_ref[0] + p_ref[1]) + p_ref[2]) + p_ref[3]
        nm = ADAM_B1 * m_ref[...] + (1.0 - ADAM_B1) * g
        nv = ADAM_B2 * v_ref[...] + (1.0 - ADAM_B2) * (g * g)
        g_ref[...] = g
        nm_ref[...] = nm
        nv_ref[...] = nv
        d_ref[...] = -ADAM_LR * ((nm * c1) / (jnp.sqrt(nv * c2) + ADAM_EPS) + ADAM_WD * w_ref[...])

    spec = pl.BlockSpec((tm, 1024), lambda i: (i, 0))
    out = jax.ShapeDtypeStruct(w.shape, F32)
    return pl.pallas_call(
        body, name="adamw", grid=(rows // tm,),
        in_specs=[pl.BlockSpec((4, tm, 1024), lambda i: (0, i, 0)), spec, spec, spec],
        out_specs=[spec] * 4, out_shape=[out] * 4, compiler_params=_cparams(("parallel",)),
    )(parts, w, m, v)


def _expand_heads(a, axis, per_head):
    a = jnp.moveaxis(a, axis, -1)
    lead = a.shape[:-1]
    a = a.reshape(lead + (HEADS, per_head))
    a = jnp.pad(a, [(0, 0)] * len(lead) + [(0, 0), (0, HEAD_PAD - per_head)])
    return jnp.moveaxis(a.reshape(lead + (D_EXP,)), -1, axis)


def _compact_heads(a, axis, start, size):
    a = jnp.moveaxis(a, axis, -1)
    lead = a.shape[:-1]
    a = a.reshape(lead + (HEADS, HEAD_PAD))[..., start:start + size]
    return jnp.moveaxis(a.reshape(lead + (HEADS * size,)), -1, axis)


def _cols_of_heads(a, per):
    lp = a.shape[0]
    return jnp.transpose(a.reshape(lp, HEADS, HEAD_PAD)[:, :, :per], (1, 2, 0)).reshape(HEADS * per, lp)


def _rows_of_heads(a_t, per):
    lp = a_t.shape[1]
    a = jnp.transpose(a_t.reshape(HEADS, per, lp), (2, 0, 1))
    return jnp.pad(a, ((0, 0), (0, 0), (0, HEAD_PAD - per))).reshape(lp, D_EXP)


def _block_diag(w):
    g, a, b = w.shape
    per = g // SSM_BLOCKS
    eye = jnp.eye(per, dtype=w.dtype)
    return jnp.einsum("jgab,gk->jgakb", w.reshape(SSM_BLOCKS, per, a, b), eye).reshape(SSM_BLOCKS, per * a, per * b)


def _block_diag_extract(dense, a, b):
    per = N_GROUPS // SSM_BLOCKS
    d5 = dense.reshape(SSM_BLOCKS, per, a, per, b)
    return jnp.einsum("jgakb,gk->jgab", d5, jnp.eye(per, dtype=dense.dtype)).reshape(N_GROUPS, a, b)


def _discretise(a_re, a_im, log_dt, b_re, b_im):
    dt = jnp.exp(log_dt)[:, None]
    mag = jnp.exp(a_re * dt)
    abar_re = mag * jnp.cos(a_im * dt)
    abar_im = mag * jnp.sin(a_im * dt)
    num_re = abar_re - 1.0
    num_im = abar_im
    den = a_re * a_re + a_im * a_im
    coef_re = (num_re * a_re + num_im * a_im) / den
    coef_im = (num_im * a_re - num_re * a_im) / den
    bbar_re = coef_re[..., None] * b_re - coef_im[..., None] * b_im
    bbar_im = coef_re[..., None] * b_im + coef_im[..., None] * b_re
    return abar_re, abar_im, bbar_re, bbar_im


def _scan_coef(ar, ai, reverse):
    ar, ai = ar.reshape(1, N_STATE), ai.reshape(1, N_STATE)
    pows = [(ar, ai)]
    for _ in range(7):
        pr, pi_ = pows[-1]
        pows.append((pr * ar - pi_ * ai, pr * ai + pi_ * ar))
    row = jnp.arange(8)[:, None]
    out = []
    for k in (1, 2, 4):
        keep = (row < 8 - k) if reverse else (row >= k)
        out += [jnp.where(keep, pows[k - 1][0], 0.0), jnp.where(keep, pows[k - 1][1], 0.0)]
    order = list(range(7, -1, -1)) if reverse else list(range(8))
    out += [jnp.concatenate([pows[k][0] for k in order], axis=0), jnp.concatenate([pows[k][1] for k in order], axis=0)]
    return jnp.stack(out).astype(F32)


def _flat_rows(a, rows):
    flat = a.reshape(-1)
    return jnp.pad(flat, (0, rows * 1024 - flat.shape[0])).reshape(rows, 1024)


def _pack(named, order):
    rows = [-(-math.prod(named[n].shape) // 1024) for n in order]
    total = -(-sum(rows) // 8) * 8
    parts = [_flat_rows(named[n], r) for n, r in zip(order, rows)]
    if total > sum(rows):
        parts.append(jnp.zeros((total - sum(rows), 1024), parts[0].dtype))
    return jnp.concatenate(parts, axis=0)


def _unpack(packed, shapes, order):
    out, at = {}, 0
    for n in order:
        size = math.prod(shapes[n])
        rows = -(-size // 1024)
        out[n] = packed[at:at + rows].reshape(-1)[:size].reshape(shapes[n])
        at += rows
    return out


def _shard_cols(a, k):
    w = a.shape[-1] // 4
    return a[..., k * w:(k + 1) * w]


def kernel(x, meta_tokens, pre_norm_w, post_norm_w, w_in, q_norm_w, w_q_up, kv_norm_w, w_kv_up, attn_out_norm_w, ssm_a_re, ssm_a_im, ssm_log_dt, ssm_b_re, ssm_b_im, ssm_c_re, ssm_c_im, ssm_d, w_glu, b_glu, ssm_out_norm_w, w_out, loss_target, m_meta_tokens, m_pre_norm_w, m_post_norm_w, m_w_in, m_q_norm_w, m_w_q_up, m_kv_norm_w, m_w_kv_up, m_attn_out_norm_w, m_ssm_a_re, m_ssm_a_im, m_ssm_log_dt, m_ssm_b_re, m_ssm_b_im, m_ssm_c_re, m_ssm_c_im, m_ssm_d, m_w_glu, m_b_glu, m_ssm_out_norm_w, m_w_out, v_meta_tokens, v_pre_norm_w, v_post_norm_w, v_w_in, v_q_norm_w, v_w_q_up, v_kv_norm_w, v_w_kv_up, v_attn_out_norm_w, v_ssm_a_re, v_ssm_a_im, v_ssm_log_dt, v_ssm_b_re, v_ssm_b_im, v_ssm_c_re, v_ssm_c_im, v_ssm_d, v_w_glu, v_b_glu, v_ssm_out_norm_w, v_w_out):
    local = dict(meta_tokens=meta_tokens, pre_norm_w=pre_norm_w, post_norm_w=post_norm_w, w_in=w_in, q_norm_w=q_norm_w,
                 w_q_up=w_q_up, kv_norm_w=kv_norm_w, w_kv_up=w_kv_up, attn_out_norm_w=attn_out_norm_w, ssm_a_re=ssm_a_re,
                 ssm_a_im=ssm_a_im, ssm_log_dt=ssm_log_dt, ssm_b_re=ssm_b_re, ssm_b_im=ssm_b_im, ssm_c_re=ssm_c_re,
                 ssm_c_im=ssm_c_im, ssm_d=ssm_d, w_glu=w_glu, b_glu=b_glu, ssm_out_norm_w=ssm_out_norm_w, w_out=w_out)
    mom_m = dict(meta_tokens=m_meta_tokens, pre_norm_w=m_pre_norm_w, post_norm_w=m_post_norm_w, w_in=m_w_in,
                 q_norm_w=m_q_norm_w, w_q_up=m_w_q_up, kv_norm_w=m_kv_norm_w, w_kv_up=m_w_kv_up,
                 attn_out_norm_w=m_attn_out_norm_w, ssm_a_re=m_ssm_a_re, ssm_a_im=m_ssm_a_im, ssm_log_dt=m_ssm_log_dt,
                 ssm_b_re=m_ssm_b_re, ssm_b_im=m_ssm_b_im, ssm_c_re=m_ssm_c_re, ssm_c_im=m_ssm_c_im, ssm_d=m_ssm_d,
                 w_glu=m_w_glu, b_glu=m_b_glu, ssm_out_norm_w=m_ssm_out_norm_w, w_out=m_w_out)
    mom_v = dict(meta_tokens=v_meta_tokens, pre_norm_w=v_pre_norm_w, post_norm_w=v_post_norm_w, w_in=v_w_in,
                 q_norm_w=v_q_norm_w, w_q_up=v_w_q_up, kv_norm_w=v_kv_norm_w, w_kv_up=v_w_kv_up,
                 attn_out_norm_w=v_attn_out_norm_w, ssm_a_re=v_ssm_a_re, ssm_a_im=v_ssm_a_im, ssm_log_dt=v_ssm_log_dt,
                 ssm_b_re=v_ssm_b_re, ssm_b_im=v_ssm_b_im, ssm_c_re=v_ssm_c_re, ssm_c_im=v_ssm_c_im, ssm_d=v_ssm_d,
                 w_glu=v_w_glu, b_glu=v_b_glu, ssm_out_norm_w=v_ssm_out_norm_w, w_out=v_w_out)
    shapes = {n: local[n].shape for n in WEIGHTS}
    mat = ("w_in", "w_q_up", "w_kv_up", "w_glu", "w_out")

    seq = x.shape[1]
    l_real = N_META + seq
    lp = -(-l_real // 640) * 640 if l_real > 1024 else -(-l_real // 128) * 128

    w_shard = _pack({n: local[n].astype(BF16) for n in mat}, mat)
    w_shard = jnp.pad(w_shard, ((0, -w_shard.shape[0] % 16), (0, 0)))
    w_all, meta_all = _gather_weights(w_shard, meta_tokens)
    mat_shapes = {n: shapes[n] for n in mat}
    per_chip = [_unpack(w_all[k], mat_shapes, mat) for k in range(4)]
    w_in_f = jnp.concatenate([p["w_in"][0] for p in per_chip], axis=1)
    w_q_f = jnp.concatenate([p["w_q_up"][0] for p in per_chip], axis=1)
    w_kv_f = jnp.concatenate([p["w_kv_up"][0] for p in per_chip], axis=1)
    w_glu_f = jnp.concatenate([p["w_glu"][0] for p in per_chip], axis=1)
    w_out_f = jnp.concatenate([p["w_out"][0] for p in per_chip], axis=0)
    meta_f = jnp.concatenate([meta_all[k] for k in range(4)], axis=1)

    o_q, o_kv, o_kr, o_ga, o_u, o_gs = 0, 256, 384, 416, 928, 1440
    krope_cols = jnp.pad(w_in_f[:, o_kr:o_ga], ((0, 0), (QK_NOPE, HEAD_PAD - QK_NOPE - QK_ROPE)))
    w_in_p = jnp.concatenate([_expand_heads(w_in_f[:, o_ga:o_u], 1, V_HEAD), w_in_f[:, o_u:o_gs], w_in_f[:, o_gs:],
                              w_in_f[:, o_q:o_kv], w_in_f[:, o_kv:o_kr], krope_cols], axis=1)
    wq_p = _expand_heads(w_q_f, 1, QK_NOPE + QK_ROPE)
    kv3 = w_kv_f.reshape(KV_LORA, HEADS, QK_NOPE + V_HEAD)
    wk_p = _expand_heads(kv3[:, :, :QK_NOPE].reshape(KV_LORA, HEADS * QK_NOPE), 1, QK_NOPE)
    wv_c = kv3[:, :, QK_NOPE:].reshape(KV_LORA, HEADS * V_HEAD)
    wv_p = _expand_heads(wv_c, 1, V_HEAD)
    w_out_a = _expand_heads(w_out_f[:D_ATTN], 0, V_HEAD)
    w_out_s = w_out_f[D_ATTN:]
    attn_norm_e = _expand_heads(attn_out_norm_w, 1, V_HEAD)

    pos = jnp.arange(lp, dtype=jnp.int32)
    half = QK_ROPE // 2
    inv = ROPE_THETA ** (-jnp.arange(half, dtype=F32) / half)
    ang = pos.astype(F32)[:, None] * inv[None, :]
    cos16, sin16 = jnp.cos(ang), jnp.sin(ang)
    ones, zeros = jnp.ones((lp, QK_NOPE), F32), jnp.zeros((lp, QK_NOPE), F32)
    tail1, tail0 = jnp.ones((lp, HEAD_PAD - MASK_LANE), F32), jnp.zeros((lp, HEAD_PAD - MASK_LANE), F32)
    z16 = jnp.zeros((lp, half), F32)
    cos = jnp.concatenate([ones, cos16, cos16, tail1], axis=1)
    sina = jnp.concatenate([zeros, z16, sin16, tail0], axis=1)
    sinb = jnp.concatenate([zeros, -sin16, z16, tail0], axis=1)

    disc_in = (ssm_a_re[0], ssm_a_im[0], ssm_log_dt[0], ssm_b_re[0], ssm_b_im[0])
    disc = lambda a_re, a_im, ldt, b_re, b_im: jax.vmap(_discretise)(a_re, a_im, ldt, b_re, b_im)
    (abar_re, abar_im, bbar_re, bbar_im), disc_vjp = jax.vjp(disc, *disc_in)
    ssm = []
    for d in range(2):
        rev = d == 1
        b_re_bd = _block_diag(jnp.swapaxes(bbar_re[d], 1, 2)).astype(BF16)
        b_im_bd = _block_diag(jnp.swapaxes(bbar_im[d], 1, 2)).astype(BF16)
        c_re_bd = _block_diag(jnp.swapaxes(ssm_c_re[0, d], 1, 2)).astype(BF16)
        c_im_bd = _block_diag(jnp.swapaxes(-ssm_c_im[0, d], 1, 2)).astype(BF16)
        ssm.append(dict(rev=rev, coef=_scan_coef(abar_re[d], abar_im[d], rev),
                        coef_adj=_scan_coef(abar_re[d], -abar_im[d], not rev),
                        b_re=b_re_bd, b_im=b_im_bd, c_re=c_re_bd, c_im=c_im_bd))

    h = jnp.concatenate([meta_f, x[0], jnp.zeros((lp - l_real, D_MODEL), F32)], axis=0)
    tgt = jnp.concatenate([jnp.zeros((N_META, D_MODEL), F32), loss_target[0], jnp.zeros((lp - l_real, D_MODEL), F32)], axis=0)
    proj = _in_proj_fwd(h, pre_norm_w, w_in_p)
    q, k, v, vt = _attn_prep_fwd(proj, q_norm_w, kv_norm_w, wq_p, wk_p, wv_p, wv_c.T, cos, sina, sinb, l_real)
    o_exp, lse = _flash_fwd(q, k, vt)
    ys, states = [], []
    for s in ssm:
        y_d, st_d = _ssm_fwd(proj, s["coef"], s["b_re"], s["b_im"], s["c_re"], s["c_im"], s["rev"])
        ys.append(y_d)
        states.append(st_d)

    (d_o, delta, dga, dyp, dsg, dres, dwoa, dwos, dwglu, vec_mid) = _mid(
        h, tgt, o_exp, proj, ys[0], ys[1], ssm_d, w_glu_f, w_glu_f.T, b_glu, ssm_out_norm_w, attn_norm_e, w_out_a, w_out_s,
        w_out_a.T, w_out_s.T, post_norm_w, l_real)
    dus, dssm = [], []
    tr = lambda a: jnp.swapaxes(a, 1, 2)
    for s, st_d in zip(ssm, states):
        du_d, dbre, dbim, dcre, dcim, da = _ssm_bwd(proj, dyp, st_d, s["coef"], s["coef_adj"], s["b_re"], s["b_im"],
                                                    tr(s["b_re"]), tr(s["b_im"]), tr(s["c_re"]), tr(s["c_im"]), s["rev"])
        dus.append(du_d)
        dssm.append((dbre, dbim, dcre, dcim, da))
    d_qk = QK_NOPE + QK_ROPE
    dq_t, dk_t, dv_t = _flash_bwd(q, k, v, d_o, _cols_of_heads(q, d_qk), _cols_of_heads(k, d_qk), _cols_of_heads(d_o, V_HEAD),
                                  lse, delta.T.reshape(HEADS, 1, lp))
    dq, dk, dv = _rows_of_heads(dq_t, d_qk), _rows_of_heads(dk_t, d_qk), _rows_of_heads(dv_t, V_HEAD)
    dql, dkvl, dkr, dwq_p, dwk_p, dwv_p, vec_prep = _attn_prep_bwd(
        dq, dk, dv, proj, q_norm_w, kv_norm_w, wq_p.T, wk_p.T, wv_p.T, cos, sina, sinb)
    dh, dwin_p, vec_in = _in_proj_bwd(h, pre_norm_w, dres, dga, dus[0], dus[1], dyp, ssm_d, dsg, dql, dkvl, dkr, w_in_p.T)

    grads = {}
    grads["w_in"] = jnp.concatenate([
        dwin_p[:, P_QLAT[0]:P_QLAT[0] + 256], dwin_p[:, P_KVLAT[0]:P_KVLAT[0] + 128],
        dwin_p[:, P_KROPE[0] + QK_NOPE:P_KROPE[0] + QK_NOPE + QK_ROPE], _compact_heads(dwin_p[:, 0:D_EXP], 1, 0, V_HEAD),
        dwin_p[:, P_U[0]:P_U[0] + 512], dwin_p[:, P_GATE_S[0]:P_GATE_S[0] + 512]], axis=1)[None]
    grads["w_q_up"] = _compact_heads(dwq_p, 1, 0, QK_NOPE + QK_ROPE)[None]
    dwk3 = _compact_heads(dwk_p, 1, 0, QK_NOPE).reshape(KV_LORA, HEADS, QK_NOPE)
    dwv3 = _compact_heads(dwv_p, 1, 0, V_HEAD).reshape(KV_LORA, HEADS, V_HEAD)
    grads["w_kv_up"] = jnp.concatenate([dwk3, dwv3], axis=2).reshape(1, KV_LORA, HEADS * (QK_NOPE + V_HEAD))
    grads["w_glu"] = dwglu[None]
    grads["w_out"] = jnp.concatenate([_compact_heads(dwoa, 0, 0, V_HEAD), dwos], axis=0)[None]
    grads["meta_tokens"] = dh[:N_META]
    grads["pre_norm_w"] = vec_in[0:1]
    grads["post_norm_w"] = vec_mid[0:1]
    grads["q_norm_w"] = vec_prep[0:1]
    grads["kv_norm_w"] = vec_prep[1:2, :KV_LORA]
    grads["attn_out_norm_w"] = _compact_heads(vec_mid[1:2], 1, 0, V_HEAD)
    grads["ssm_out_norm_w"] = vec_mid[2:3, :D_SSM]
    grads["ssm_d"] = vec_mid[3:4, :D_SSM]
    grads["b_glu"] = vec_mid[4:5]
    d_abar_re = jnp.stack([dssm[d][4][0].sum(axis=0).reshape(N_GROUPS, SSM_STATE) for d in range(2)])
    d_abar_im = jnp.stack([dssm[d][4][1].sum(axis=0).reshape(N_GROUPS, SSM_STATE) for d in range(2)])
    d_bbar_re = jnp.stack([jnp.swapaxes(_block_diag_extract(dssm[d][0], SSM_GROUP, SSM_STATE), 1, 2) for d in range(2)])
    d_bbar_im = jnp.stack([jnp.swapaxes(_block_diag_extract(dssm[d][1], SSM_GROUP, SSM_STATE), 1, 2) for d in range(2)])
    da_re, da_im, dlog_dt, db_re, db_im = disc_vjp((d_abar_re, d_abar_im, d_bbar_re, d_bbar_im))
    grads["ssm_a_re"], grads["ssm_a_im"], grads["ssm_log_dt"] = da_re[None], da_im[None], dlog_dt[None]
    grads["ssm_b_re"], grads["ssm_b_im"] = db_re[None], db_im[None]
    grads["ssm_c_re"] = jnp.stack([_block_diag_extract(dssm[d][2], SSM_GROUP, SSM_STATE) for d in range(2)])[None]
    grads["ssm_c_im"] = jnp.stack([_block_diag_extract(dssm[d][3], SSM_GROUP, SSM_STATE) for d in range(2)])[None]

    def shard_of(n, a, kk):
        return a[:, kk * 256:(kk + 1) * 256] if n == "w_out" else _shard_cols(a, kk)

    slices = [_pack({n: shard_of(n, grads[n], kk) for n in BIG}, BIG) for kk in range(4)]
    small = _pack({n: grads[n] for n in SMALL}, SMALL)
    rs, rsm = slices[0].shape[0], small.shape[0]
    g_pack = jnp.concatenate(slices + [small], axis=0)
    g_pair = _pair_sum(g_pack, _swap_sibling(g_pack))
    parts = _scatter_chips(g_pair, rs, rsm)

    order = BIG + SMALL
    big_shapes = {n: shapes[n] for n in BIG}
    small_shapes = {n: shapes[n] for n in SMALL}

    def pack_state(named):
        return jnp.concatenate([_pack({n: named[n] for n in BIG}, BIG), _pack({n: named[n] for n in SMALL}, SMALL)], axis=0)

    g_out, d_out, m_out, v_out = _adamw(parts, pack_state(local), pack_state(mom_m), pack_state(mom_v))

    def unpack_state(p):
        out = _unpack(p[:rs], big_shapes, BIG)
        out.update(_unpack(p[rs:], small_shapes, SMALL))
        return out

    g_fin, d_fin, m_fin, v_fin = unpack_state(g_out), unpack_state(d_out), unpack_state(m_out), unpack_state(v_out)
    loss = lax.psum(vec_mid[5, 0], ("x", "y", "c"))
    grad_x = dh[N_META:l_real][None]
    return (loss, grad_x, *[g_fin[n] for n in WEIGHTS], *[d_fin[n] for n in WEIGHTS], *[m_fin[n] for n in WEIGHTS],
            *[v_fin[n] for n in WEIGHTS])
```

```python
import functools
import math

import jax
import jax.numpy as jnp
from jax import lax
from jax.experimental import pallas as pl
from jax.experimental.pallas import tpu as pltpu

F32 = jnp.float32
BF16 = jnp.bfloat16
MESH = pl.DeviceIdType.MESH

D_MODEL = 1024
N_META = 16
EPS = 1e-6
HEADS = 8
QK_NOPE = 64
QK_ROPE = 32
V_HEAD = 64
Q_LORA = 256
KV_LORA = 128
D_ATTN = 512
D_SSM = 512
SSM_GROUP = 16
N_GROUPS = 32
SSM_STATE = 64
N_STATE = N_GROUPS * SSM_STATE
ROPE_THETA = 10000.0
HEAD_PAD = 128
D_EXP = HEADS * HEAD_PAD
D_QK = QK_NOPE + QK_ROPE
MASK_LANE = D_QK
NEG_BIG = -1e30
SCALE = 1.0 / math.sqrt(QK_NOPE + QK_ROPE)
LOG2E = math.log2(math.e)
SCALE2 = SCALE * LOG2E
QBLK = 256
SCAN_COLS = 512
SSM_BLOCKS = 4
BLK_CH = D_SSM // SSM_BLOCKS
BLK_ST = N_STATE // SSM_BLOCKS

P_GATE_A = (0, 1024)
P_U = (1024, 512)
P_GATE_S = (1536, 512)
P_QLAT = (2048, 256)
P_KVLAT = (2304, 128)
P_KROPE = (2432, 128)
D_PROJ = 2560

ADAM_LR = 0.001
ADAM_B1 = 0.9
ADAM_B2 = 0.999
ADAM_EPS = 1e-08
ADAM_WD = 0.01
ADAM_STEP = 10

VMEM_LIMIT = 60 * 1024 * 1024

BIG = ("w_in", "w_q_up", "w_kv_up", "w_glu", "w_out", "meta_tokens")
SMALL = ("pre_norm_w", "post_norm_w", "q_norm_w", "kv_norm_w", "attn_out_norm_w", "ssm_a_re", "ssm_a_im",
         "ssm_log_dt", "ssm_b_re", "ssm_b_im", "ssm_c_re", "ssm_c_im", "ssm_d", "b_glu", "ssm_out_norm_w")
WEIGHTS = ("meta_tokens", "pre_norm_w", "post_norm_w", "w_in", "q_norm_w", "w_q_up", "kv_norm_w", "w_kv_up",
           "attn_out_norm_w", "ssm_a_re", "ssm_a_im", "ssm_log_dt", "ssm_b_re", "ssm_b_im", "ssm_c_re", "ssm_c_im",
           "ssm_d", "w_glu", "b_glu", "ssm_out_norm_w", "w_out")


def _cparams(sem=None):
    return pltpu.CompilerParams(dimension_semantics=sem, vmem_limit_bytes=VMEM_LIMIT)


def _dot(a, b):
    return jnp.dot(a, b, preferred_element_type=F32)


def _dot_nt(a, b):
    return lax.dot_general(a, b, (((1,), (1,)), ((), ())), preferred_element_type=F32)


def _dot_tn(a, b):
    return lax.dot_general(a, b, (((0,), (0,)), ((), ())), preferred_element_type=F32)


def _sigmoid(x):
    return 1.0 / (1.0 + jnp.exp(-x))


def _rms_fwd(x, w, n):
    r = lax.rsqrt(jnp.sum(x * x, axis=-1, keepdims=True) * (1.0 / n) + EPS)
    return x * r * w, r


def _rms_bwd(x, r, w, dy, n):
    dyw = dy * w
    dx = r * dyw - x * (r * r * r) * (jnp.sum(dyw * x, axis=-1, keepdims=True) * (1.0 / n))
    dw = jnp.sum(dy * (x * r), axis=0, keepdims=True)
    return dx, dw


def _rope_apply(x, cos, sina, sinb):
    return x * cos + pltpu.roll(x, 16, 1) * sina + pltpu.roll(x, HEAD_PAD - 16, 1) * sinb


def _rope_transpose(g, cos, sina, sinb):
    return g * cos + pltpu.roll(g * sina, HEAD_PAD - 16, 1) + pltpu.roll(g * sinb, 16, 1)


def _row_tile(lp):
    return 640 if lp % 640 == 0 else 128


def _ssm_tile(lp):
    return 320 if lp % 320 == 0 else 128


def _rows(tm, off_width):
    off, width = off_width
    return pl.BlockSpec((tm, width), lambda i: (i, off // width))


def _whole(shape, single=True):
    nd = len(shape)
    if single:
        return pl.BlockSpec(shape, lambda *_: (0,) * nd, pipeline_mode=pl.Buffered(1))
    return pl.BlockSpec(shape, lambda *_: (0,) * nd)


def _out_whole(shape):
    return _whole(shape, single=False)


def _pick_tile(rows, cap):
    best = 8
    for t in range(8, cap + 1, 8):
        if rows % t == 0:
            best = t
    return best


def _in_proj_fwd(h, pre_w, w_in_p):
    lp = h.shape[0]
    tm = _row_tile(lp)

    def body(h_ref, w_ref, win_ref, proj_ref):
        xn, _ = _rms_fwd(h_ref[...], w_ref[...], D_MODEL)
        proj_ref[...] = _dot(xn.astype(BF16), win_ref[...])

    return pl.pallas_call(
        body, name="in_proj_fwd", grid=(lp // tm,),
        in_specs=[_rows(tm, (0, D_MODEL)), _whole((1, D_MODEL)), _whole((D_MODEL, D_PROJ))],
        out_specs=_rows(tm, (0, D_PROJ)),
        out_shape=jax.ShapeDtypeStruct((lp, D_PROJ), F32),
        compiler_params=_cparams(("parallel",)),
    )(h, pre_w, w_in_p)


def _attn_prep_fwd(proj, q_norm_w, kv_norm_w, wq_p, wk_p, wv_p, wv_t, cos, sina, sinb, l_real):
    lp = proj.shape[0]
    tm = _row_tile(lp)

    def body(ql_ref, kvl_ref, kr_ref, qw_ref, kw_ref, wq_ref, wk_ref, wv_ref, wvt_ref, cos_ref, sa_ref, sb_ref,
             q_ref, k_ref, v_ref, vt_ref, qt_ref, kt_ref):
        cos_t, sa_t, sb_t = cos_ref[...], sa_ref[...], sb_ref[...]
        qn, _ = _rms_fwd(ql_ref[...], qw_ref[...], Q_LORA)
        kvn, _ = _rms_fwd(kvl_ref[...], kw_ref[...], KV_LORA)
        kvn_b = kvn.astype(BF16)
        qp = _dot(qn.astype(BF16), wq_ref[...])
        kp = _dot(kvn_b, wk_ref[...])
        v_ref[...] = _dot(kvn_b, wv_ref[...]).astype(BF16)
        vt_ref[...] = _dot_nt(wvt_ref[...], kvn_b).astype(BF16)
        lane = lax.broadcasted_iota(jnp.int32, (tm, HEAD_PAD), 1)
        row = lax.broadcasted_iota(jnp.int32, (tm, HEAD_PAD), 0) + pl.program_id(0) * tm
        q_one = jnp.where(lane == MASK_LANE, 1.0, 0.0)
        k_add = _rope_apply(kr_ref[...], cos_t, sa_t, sb_t) + jnp.where((lane == MASK_LANE) & (row >= l_real), NEG_BIG, 0.0)
        for hd in range(HEADS):
            blk = slice(hd * HEAD_PAD, (hd + 1) * HEAD_PAD)
            q_h = _rope_apply(qp[:, blk], cos_t, sa_t, sb_t) * SCALE2 + q_one
            k_h = kp[:, blk] + k_add
            q_ref[:, blk] = q_h.astype(BF16)
            k_ref[:, blk] = k_h.astype(BF16)
            qt_ref[hd * D_QK:(hd + 1) * D_QK, :] = q_h.T[:D_QK].astype(BF16)
            kt_ref[hd * D_QK:(hd + 1) * D_QK, :] = k_h.T[:D_QK].astype(BF16)

    tab = _rows(tm, (0, HEAD_PAD))
    out = jax.ShapeDtypeStruct((lp, D_EXP), BF16)
    out_t = jax.ShapeDtypeStruct((HEADS * D_QK, lp), BF16)
    cols_t = pl.BlockSpec((HEADS * D_QK, tm), lambda i: (0, i))
    return pl.pallas_call(
        body, name="attn_prep_fwd", grid=(lp // tm,),
        in_specs=[_rows(tm, P_QLAT), _rows(tm, P_KVLAT), _rows(tm, P_KROPE), _whole((1, Q_LORA)), _whole((1, KV_LORA)),
                  _whole((Q_LORA, D_EXP)), _whole((KV_LORA, D_EXP)), _whole((KV_LORA, D_EXP)), _whole((D_ATTN, KV_LORA)),
                  tab, tab, tab],
        out_specs=[_rows(tm, (0, D_EXP))] * 3 + [pl.BlockSpec((D_ATTN, tm), lambda i: (0, i)), cols_t, cols_t],
        out_shape=[out, out, out, jax.ShapeDtypeStruct((D_ATTN, lp), BF16), out_t, out_t],
        compiler_params=_cparams(("parallel",)),
    )(proj, proj, proj, q_norm_w, kv_norm_w, wq_p, wk_p, wv_p, wv_t, cos, sina, sinb)


def _flash_fwd(q, k, vt):
    lp = q.shape[0]
    tq = 1280 if lp % 1280 == 0 else 256
    tk = QBLK
    nk = lp // tk

    def body(q_ref, k_ref, vt_ref, o_ref, lse_ref, acc, m_s, l_s):
        acc[...] = jnp.zeros_like(acc)
        m_s[...] = jnp.full(m_s.shape, NEG_BIG, F32)
        l_s[...] = jnp.zeros_like(l_s)

        def step(j, _):
            ks = pl.multiple_of(j * tk, tk)
            kt = k_ref[pl.ds(ks, tk), :]
            vt_t = vt_ref[:, pl.ds(ks, tk)]
            m_old, l_old, acc_old = m_s[...], l_s[...], acc[...]
            blocks = [slice(c * QBLK, (c + 1) * QBLK) for c in range(tq // QBLK)]
            s = [_dot_nt(kt, q_ref[cols, :]) for cols in blocks]
            m_new = [jnp.maximum(m_old[:, cols], jnp.max(s_c, axis=0, keepdims=True)) for cols, s_c in zip(blocks, s)]
            p = [jnp.exp2(s_c - m_c) for s_c, m_c in zip(s, m_new)]
            pv = [_dot(vt_t, p_c.astype(BF16)) for p_c in p]
            m_new = jnp.concatenate(m_new, axis=1)
            alpha = jnp.exp2(m_old - m_new)
            l_s[...] = alpha * l_old + jnp.concatenate([jnp.sum(p_c, axis=0, keepdims=True) for p_c in p], axis=1)
            acc[...] = alpha * acc_old + jnp.concatenate(pv, axis=1)
            m_s[...] = m_new
            return 0

        lax.fori_loop(0, nk, step, 0, unroll=5 if nk % 5 == 0 else 1)
        o_t = acc[...] / l_s[...]
        o_ref[...] = jnp.concatenate([o_t, jnp.zeros_like(o_t)], axis=0).T
        lse_ref[...] = m_s[...] + jnp.log2(l_s[...])

    return pl.pallas_call(
        body, name="flash_fwd", grid=(HEADS, lp // tq),
        in_specs=[pl.BlockSpec((tq, HEAD_PAD), lambda hd, i: (i, hd)),
                  pl.BlockSpec((lp, HEAD_PAD), lambda hd, i: (0, hd)),
                  pl.BlockSpec((V_HEAD, lp), lambda hd, i: (hd, 0))],
        out_specs=[pl.BlockSpec((tq, HEAD_PAD), lambda hd, i: (i, hd)),
                   pl.BlockSpec((None, 1, tq), lambda hd, i: (hd, 0, i))],
        out_shape=[jax.ShapeDtypeStruct((lp, D_EXP), F32), jax.ShapeDtypeStruct((HEADS, 1, lp), F32)],
        scratch_shapes=[pltpu.VMEM((V_HEAD, tq), F32), pltpu.VMEM((1, tq), F32), pltpu.VMEM((1, tq), F32)],
        compiler_params=_cparams(("parallel", "parallel")),
    )(q, k, vt)


def _scan_rows(xr_ref, xi_ref, base, n_rows, coef_ref, carry_ref, reverse, tile_fn=None, acc_refs=()):
    n_tiles = n_rows // 8
    shifts = (7, 6, 4) if reverse else (1, 2, 4)
    for cg in range(N_STATE // SCAN_COLS):
        cols = slice(cg * SCAN_COLS, (cg + 1) * SCAN_COLS)
        co = [coef_ref[k, :, cols] for k in range(8)]

        def step(t, carry, cols=cols, co=co):
            cr, ci = carry[0], carry[1]
            tt = (n_tiles - 1 - t) if reverse else t
            start = pl.multiple_of(base + tt * 8, 8)
            tr = xr_ref[pl.ds(start, 8), cols]
            ti = xi_ref[pl.ds(start, 8), cols]
            for lvl in range(3):
                ar, ai = co[2 * lvl], co[2 * lvl + 1]
                sr = pltpu.roll(tr, shifts[lvl], 0)
                si = pltpu.roll(ti, shifts[lvl], 0)
                tr, ti = tr + ar * sr - ai * si, ti + ar * si + ai * sr
            tr, ti = tr + co[6] * cr - co[7] * ci, ti + co[6] * ci + co[7] * cr
            xr_ref[pl.ds(start, 8), cols] = tr
            xi_ref[pl.ds(start, 8), cols] = ti
            accs = carry[2:]
            if tile_fn is not None:
                accs = tuple(a + d for a, d in zip(accs, tile_fn(start, cols, tr, ti)))
            new_c = (tr[0:1], ti[0:1]) if reverse else (tr[7:8], ti[7:8])
            return new_c + accs

        init = (carry_ref[0:1, cols], carry_ref[1:2, cols]) + tuple(a[:, cols] for a in acc_refs)
        out = lax.fori_loop(0, n_tiles, step, init)
        carry_ref[0:1, cols] = out[0]
        carry_ref[1:2, cols] = out[1]
        for a, val in zip(acc_refs, out[2:]):
            a[:, cols] = val


def _ssm_fwd(proj, coef, b_re, b_im, c_re, c_im_neg, reverse):
    lp = proj.shape[0]
    t = _ssm_tile(lp)
    n = lp // t
    order = (lambda i: n - 1 - i) if reverse else (lambda i: i)

    def body(u_ref, coef_ref, bre_ref, bim_ref, cre_ref, cim_ref, y_ref, st_ref, xr, xi, carry):
        @pl.when(pl.program_id(0) == 0)
        def _():
            carry[...] = jnp.zeros_like(carry)

        st_ref[...] = carry[0:2, :]
        ub = u_ref[...].astype(BF16)
        for j in range(SSM_BLOCKS):
            ch, stt = slice(j * BLK_CH, (j + 1) * BLK_CH), slice(j * BLK_ST, (j + 1) * BLK_ST)
            xr[:, stt] = _dot(ub[:, ch], bre_ref[j])
            xi[:, stt] = _dot(ub[:, ch], bim_ref[j])
        _scan_rows(xr, xi, 0, t, coef_ref, carry, reverse)
        for j in range(SSM_BLOCKS):
            ch, stt = slice(j * BLK_CH, (j + 1) * BLK_CH), slice(j * BLK_ST, (j + 1) * BLK_ST)
            y_ref[:, ch] = _dot(xr[:, stt].astype(BF16), cre_ref[j]) + _dot(xi[:, stt].astype(BF16), cim_ref[j])

    wb, wc = _whole((SSM_BLOCKS, BLK_CH, BLK_ST)), _whole((SSM_BLOCKS, BLK_ST, BLK_CH))
    return pl.pallas_call(
        body, name="ssm_fwd_rev" if reverse else "ssm_fwd", grid=(n,),
        in_specs=[pl.BlockSpec((t, D_SSM), lambda i: (order(i), P_U[0] // D_SSM)), _whole((8, 8, N_STATE)), wb, wb, wc, wc],
        out_specs=[pl.BlockSpec((t, D_SSM), lambda i: (order(i), 0)),
                   pl.BlockSpec((None, 2, N_STATE), lambda i: (order(i), 0, 0))],
        out_shape=[jax.ShapeDtypeStruct((lp, D_SSM), F32), jax.ShapeDtypeStruct((n, 2, N_STATE), F32)],
        scratch_shapes=[pltpu.VMEM((t, N_STATE), F32), pltpu.VMEM((t, N_STATE), F32), pltpu.VMEM((8, N_STATE), F32)],
        compiler_params=_cparams(("arbitrary",)),
    )(proj, coef, b_re, b_im, c_re, c_im_neg)


GELU_C0 = math.sqrt(2.0 / math.pi)
GELU_C1 = 0.044715


def _mid(h, tgt, o_exp, proj, y0, y1, ssm_d, w_glu, w_glu_t, b_glu, ssm_norm_w, attn_norm_w_e, w_out_a, w_out_s,
         w_out_a_t, w_out_s_t, post_w, l_real):
    lp = h.shape[0]
    tm = 128

    def body(h_ref, tga_ref, tgb_ref, o_ref, ga_ref, u_ref, sg_ref, y0_ref, y1_ref, d_ref, wg_ref, wgt_ref, bg_ref, ws_ref,
             wa_ref, woa_ref, wos_ref, woat_ref, wost_ref, pw_ref,
             do_ref, dot_ref, delta_ref, dga_ref, dyp_ref, dsg_ref, dres_ref, dwoa_ref, dwos_ref, dwg_ref, vec_ref):
        @pl.when(pl.program_id(0) == 0)
        def _():
            dwoa_ref[...] = jnp.zeros_like(dwoa_ref)
            dwos_ref[...] = jnp.zeros_like(dwos_ref)
            dwg_ref[...] = jnp.zeros_like(dwg_ref)
            vec_ref[...] = jnp.zeros_like(vec_ref)

        u = u_ref[...]
        ypre = y0_ref[...] + y1_ref[...] + d_ref[...] * u
        th = jnp.tanh(GELU_C0 * (ypre + GELU_C1 * ypre * ypre * ypre))
        gel = 0.5 * ypre * (1.0 + th)
        gel_b = gel.astype(BF16)
        glu = _dot(gel_b, wg_ref[...]) + bg_ref[...]
        g1, g2 = glu[:, :D_SSM], glu[:, D_SSM:]
        sig2 = _sigmoid(g2)
        z = g1 * sig2
        sg = sg_ref[...]
        sgs = _sigmoid(sg)
        sil_s = sg * sgs
        s = z * sil_s
        ys, r_s = _rms_fwd(s, ws_ref[...], D_SSM)

        o = o_ref[...]
        ga = ga_ref[...]
        gas = _sigmoid(ga)
        sil_a = ga * gas
        a = o * sil_a
        ya, r_a = _rms_fwd(a, wa_ref[...], D_ATTN)

        ya_b, ys_b = ya.astype(BF16), ys.astype(BF16)
        y = _dot(ya_b, woa_ref[...]) + _dot(ys_b, wos_ref[...])
        yn, r_y = _rms_fwd(y, pw_ref[...], D_MODEL)
        row = lax.broadcasted_iota(jnp.int32, (tm, 1), 0) + pl.program_id(0) * tm
        valid = (row >= N_META) & (row < l_real)
        tgt = jnp.concatenate([tga_ref[tm - N_META:, :], tgb_ref[:tm - N_META, :]], axis=0)
        err = jnp.where(valid, h_ref[...] + yn - tgt, 0.0)
        loss = 0.5 * jnp.sum(jnp.sum(err * err, axis=-1, keepdims=True), axis=0, keepdims=True) * (1.0 / D_MODEL)
        dout = err * (1.0 / D_MODEL)
        dres_ref[...] = dout

        dy, d_pw = _rms_bwd(y, r_y, pw_ref[...], dout, D_MODEL)
        dy_b = dy.astype(BF16)
        dya = _dot(dy_b, woat_ref[...])
        dys = _dot(dy_b, wost_ref[...])
        dwoa_ref[...] += _dot_tn(ya_b, dy_b)
        dwos_ref[...] += _dot_tn(ys_b, dy_b)

        da, d_wa = _rms_bwd(a, r_a, wa_ref[...], dya, D_ATTN)
        d_o = da * sil_a
        dga_ref[...] = da * o * (gas * (1.0 + ga * (1.0 - gas)))
        do_ref[...] = d_o.astype(BF16)
        for hd in range(HEADS):
            dot_ref[hd * V_HEAD:(hd + 1) * V_HEAD, :] = d_o[:, hd * HEAD_PAD:(hd + 1) * HEAD_PAD].T[:V_HEAD].astype(BF16)
        prod = d_o * o
        lane8 = lax.broadcasted_iota(jnp.int32, (tm, HEADS), 1)
        delta = jnp.zeros((tm, HEADS), F32)
        for hd in range(HEADS):
            delta = jnp.where(lane8 == hd, jnp.sum(prod[:, hd * HEAD_PAD:(hd + 1) * HEAD_PAD], axis=-1, keepdims=True), delta)
        delta_ref[...] = delta

        ds, d_ws = _rms_bwd(s, r_s, ws_ref[...], dys, D_SSM)
        dz = ds * sil_s
        dsg_ref[...] = ds * z * (sgs * (1.0 + sg * (1.0 - sgs)))
        dglu = jnp.concatenate([dz * sig2, dz * g1 * sig2 * (1.0 - sig2)], axis=-1)
        dglu_b = dglu.astype(BF16)
        dwg_ref[...] += _dot_tn(gel_b, dglu_b)
        dgel = _dot(dglu_b, wgt_ref[...])
        dgelu = 0.5 * (1.0 + th) + 0.5 * ypre * (1.0 - th * th) * (GELU_C0 * (1.0 + 3.0 * GELU_C1 * ypre * ypre))
        dyp = dgel * dgelu
        dyp_ref[...] = dyp

        vec_ref[0:1, :] += d_pw
        vec_ref[1:2, :] += d_wa
        vec_ref[2:3, 0:D_SSM] += d_ws
        vec_ref[3:4, 0:D_SSM] += jnp.sum(dyp * u, axis=0, keepdims=True)
        vec_ref[4:5, :] += jnp.sum(dglu, axis=0, keepdims=True)
        vec_ref[5:6, :] += jnp.broadcast_to(loss, (1, D_MODEL))

    full = lambda off: _rows(tm, (off, D_MODEL))
    half = lambda off: _rows(tm, (off, D_SSM))
    last = tgt.shape[0] // tm - 1
    tg_a = pl.BlockSpec((tm, D_MODEL), lambda i: (jnp.clip(i - 1, 0, last), 0))
    tg_b = pl.BlockSpec((tm, D_MODEL), lambda i: (jnp.minimum(i, last), 0))
    return pl.pallas_call(
        body, name="mid", grid=(lp // tm,),
        in_specs=[full(0), tg_a, tg_b, full(0), _rows(tm, P_GATE_A), _rows(tm, P_U), _rows(tm, P_GATE_S), half(0), half(0),
                  _whole((1, D_SSM)), _whole((D_SSM, 2 * D_SSM)), _whole((2 * D_SSM, D_SSM)), _whole((1, 2 * D_SSM)),
                  _whole((1, D_SSM)), _whole((1, D_EXP)), _whole((D_EXP, D_MODEL)), _whole((D_SSM, D_MODEL)),
                  _whole((D_MODEL, D_EXP)), _whole((D_MODEL, D_SSM)), _whole((1, D_MODEL))],
        out_specs=[full(0), pl.BlockSpec((D_ATTN, tm), lambda i: (0, i)), _rows(tm, (0, HEADS)), full(0), half(0), half(0), full(0),
                   _out_whole((D_EXP, D_MODEL)), _out_whole((D_SSM, D_MODEL)), _out_whole((D_SSM, 2 * D_SSM)),
                   _out_whole((8, D_MODEL))],
        out_shape=[jax.ShapeDtypeStruct((lp, D_EXP), BF16), jax.ShapeDtypeStruct((D_ATTN, lp), BF16),
                   jax.ShapeDtypeStruct((lp, HEADS), F32),
                   jax.ShapeDtypeStruct((lp, D_EXP), F32), jax.ShapeDtypeStruct((lp, D_SSM), F32),
                   jax.ShapeDtypeStruct((lp, D_SSM), F32), jax.ShapeDtypeStruct((lp, D_MODEL), F32),
                   jax.ShapeDtypeStruct((D_EXP, D_MODEL), F32), jax.ShapeDtypeStruct((D_SSM, D_MODEL), F32),
                   jax.ShapeDtypeStruct((D_SSM, 2 * D_SSM), F32), jax.ShapeDtypeStruct((8, D_MODEL), F32)],
        compiler_params=_cparams(("arbitrary",)),
    )(h, tgt, tgt, o_exp, proj, proj, proj, y0, y1, ssm_d, w_glu, w_glu_t, b_glu, ssm_norm_w, attn_norm_w_e, w_out_a, w_out_s,
      w_out_a_t, w_out_s_t, post_w)


def _ssm_bwd(proj, dyp, states, coef, coef_adj, b_re, b_im, b_re_t, b_im_t, c_re_t, c_im_neg_t, reverse):
    lp = proj.shape[0]
    t = _ssm_tile(lp)
    n = lp // t
    order = (lambda i: i) if reverse else (lambda i: n - 1 - i)
    edge = (t + 8) if reverse else 7

    def body(u_ref, dy_ref, st_ref, coef_ref, coefa_ref, bre_ref, bim_ref, bret_ref, bimt_ref, cret_ref, cimt_ref,
             du_ref, dbre_ref, dbim_ref, dcre_ref, dcim_ref, da_ref, xr, xi, gr, gi, carry_x, carry_g):
        @pl.when(pl.program_id(0) == 0)
        def _():
            carry_g[...] = jnp.zeros_like(carry_g)
            carry_x[...] = jnp.zeros_like(carry_x)
            dbre_ref[...] = jnp.zeros_like(dbre_ref)
            dbim_ref[...] = jnp.zeros_like(dbim_ref)
            dcre_ref[...] = jnp.zeros_like(dcre_ref)
            dcim_ref[...] = jnp.zeros_like(dcim_ref)
            da_ref[...] = jnp.zeros_like(da_ref)
            for halo in (slice(0, 8), slice(t + 8, t + 16)):
                xr[halo, :] = jnp.zeros((8, N_STATE), F32)
                xi[halo, :] = jnp.zeros((8, N_STATE), F32)

        ub = u_ref[...].astype(BF16)
        dyb = dy_ref[...].astype(BF16)
        carry_x[0:2, :] = st_ref[...]
        xr[edge:edge + 1, :] = st_ref[0:1, :]
        xi[edge:edge + 1, :] = st_ref[1:2, :]
        blocks = [(slice(j * BLK_CH, (j + 1) * BLK_CH), slice(j * BLK_ST, (j + 1) * BLK_ST)) for j in range(SSM_BLOCKS)]
        for j, (ch, stt) in enumerate(blocks):
            xr[8:t + 8, stt] = _dot(ub[:, ch], bre_ref[j])
            xi[8:t + 8, stt] = _dot(ub[:, ch], bim_ref[j])
            gr[:, stt] = _dot(dyb[:, ch], cret_ref[j])
            gi[:, stt] = _dot(dyb[:, ch], cimt_ref[j])
        _scan_rows(xr, xi, 8, t, coef_ref, carry_x, reverse)

        row8 = lax.broadcasted_iota(jnp.int32, (8, SCAN_COLS), 0)

        def tile_fn(start, cols, g_re, g_im):
            xs = pl.multiple_of(start + 8, 8)
            if reverse:
                nb = pl.multiple_of(start + 16, 8)
                xn_r = jnp.where(row8 == 7, xr[pl.ds(nb, 8), cols][0:1], pltpu.roll(xr[pl.ds(xs, 8), cols], 7, 0))
                xn_i = jnp.where(row8 == 7, xi[pl.ds(nb, 8), cols][0:1], pltpu.roll(xi[pl.ds(xs, 8), cols], 7, 0))
            else:
                nb = pl.multiple_of(start, 8)
                xn_r = jnp.where(row8 == 0, xr[pl.ds(nb, 8), cols][7:8], pltpu.roll(xr[pl.ds(xs, 8), cols], 1, 0))
                xn_i = jnp.where(row8 == 0, xi[pl.ds(nb, 8), cols][7:8], pltpu.roll(xi[pl.ds(xs, 8), cols], 1, 0))
            return g_re * xn_r + g_im * xn_i, g_im * xn_r - g_re * xn_i

        _scan_rows(gr, gi, 0, t, coefa_ref, carry_g, not reverse, tile_fn=tile_fn, acc_refs=(da_ref.at[0], da_ref.at[1]))

        for j, (ch, stt) in enumerate(blocks):
            g_re_b, g_im_b = gr[:, stt].astype(BF16), gi[:, stt].astype(BF16)
            du_ref[:, ch] = _dot(g_re_b, bret_ref[j]) + _dot(g_im_b, bimt_ref[j])
            dbre_ref[j] += _dot_tn(ub[:, ch], g_re_b)
            dbim_ref[j] += _dot_tn(ub[:, ch], g_im_b)
            dcre_ref[j] += _dot_tn(dyb[:, ch], xr[8:t + 8, stt].astype(BF16))
            dcim_ref[j] -= _dot_tn(dyb[:, ch], xi[8:t + 8, stt].astype(BF16))

    dense = jax.ShapeDtypeStruct((SSM_BLOCKS, BLK_CH, BLK_ST), F32)
    wb, wc = _whole((SSM_BLOCKS, BLK_CH, BLK_ST)), _whole((SSM_BLOCKS, BLK_ST, BLK_CH))
    acc = _out_whole((SSM_BLOCKS, BLK_CH, BLK_ST))
    return pl.pallas_call(
        body, name="ssm_bwd_rev" if reverse else "ssm_bwd", grid=(n,),
        in_specs=[pl.BlockSpec((t, D_SSM), lambda i: (order(i), P_U[0] // D_SSM)),
                  pl.BlockSpec((t, D_SSM), lambda i: (order(i), 0)),
                  pl.BlockSpec((None, 2, N_STATE), lambda i: (order(i), 0, 0)),
                  _whole((8, 8, N_STATE)), _whole((8, 8, N_STATE)), wb, wb, wc, wc, wb, wb],
        out_specs=[pl.BlockSpec((t, D_SSM), lambda i: (order(i), 0)), acc, acc, acc, acc, _out_whole((2, 8, N_STATE))],
        out_shape=[jax.ShapeDtypeStruct((lp, D_SSM), F32), dense, dense, dense, dense,
                   jax.ShapeDtypeStruct((2, 8, N_STATE), F32)],
        scratch_shapes=[pltpu.VMEM((t + 16, N_STATE), F32), pltpu.VMEM((t + 16, N_STATE), F32),
                        pltpu.VMEM((t, N_STATE), F32), pltpu.VMEM((t, N_STATE), F32),
                        pltpu.VMEM((8, N_STATE), F32), pltpu.VMEM((8, N_STATE), F32)],
        compiler_params=_cparams(("arbitrary",)),
    )(proj, dyp, states, coef, coef_adj, b_re, b_im, b_re_t, b_im_t, c_re_t, c_im_neg_t)


def _flash_bwd(q, k, v, d_o, q_t, k_t, do_t, lse_row, delta_row):
    lp = q.shape[0]
    tq = 1280 if lp % 1280 == 0 else 256
    tk = QBLK
    nk = lp // tk
    d_qk = QK_NOPE + QK_ROPE

    def body(q_ref, do_ref, qt_ref, dot_ref, lse_ref, delta_ref, k_ref, v_ref, kt_ref, dq_ref, dk_ref, dv_ref, dq_acc):
        @pl.when(pl.program_id(1) == 0)
        def _():
            dk_ref[...] = jnp.zeros_like(dk_ref)
            dv_ref[...] = jnp.zeros_like(dv_ref)

        dq_acc[...] = jnp.zeros_like(dq_acc)
        lse, delta = lse_ref[...], delta_ref[...]
        q_cols, do_cols = qt_ref[...], dot_ref[...]
        blocks = [slice(c * QBLK, (c + 1) * QBLK) for c in range(tq // QBLK)]

        def step(j, _):
            ks = pl.multiple_of(j * tk, tk)
            k_rows, v_rows = k_ref[pl.ds(ks, tk), :], v_ref[pl.ds(ks, tk), :]
            dq_old = dq_acc[...]
            st = [_dot_nt(k_rows, q_ref[cols, :]) for cols in blocks]
            dpt = [_dot_nt(v_rows, do_ref[cols, :]) for cols in blocks]
            pt = [jnp.exp2(s_c - lse[:, cols]) for s_c, cols in zip(st, blocks)]
            dst = [p_c * (dp_c - delta[:, cols]) for p_c, dp_c, cols in zip(pt, dpt, blocks)]
            pt_b = jnp.concatenate([p_c.astype(BF16) for p_c in pt], axis=1)
            dst_b = jnp.concatenate([d_c.astype(BF16) for d_c in dst], axis=1)
            dv_ref[:, pl.ds(ks, tk)] += _dot_nt(do_cols, pt_b)
            dk_ref[:, pl.ds(ks, tk)] += _dot_nt(q_cols, dst_b) * (1.0 / LOG2E)
            dq_acc[...] = dq_old + _dot(kt_ref[:, pl.ds(ks, tk)], dst_b)
            return 0

        lax.fori_loop(0, nk, step, 0, unroll=5 if nk % 5 == 0 else 1)
        dq_ref[...] = jnp.concatenate([dq_acc[...], jnp.zeros((HEAD_PAD - d_qk, tq), F32)], axis=0).T

    tile = pl.BlockSpec((tq, HEAD_PAD), lambda hd, i: (i, hd))
    head = pl.BlockSpec((lp, HEAD_PAD), lambda hd, i: (0, hd))
    rowv = pl.BlockSpec((None, 1, tq), lambda hd, i: (hd, 0, i))
    return pl.pallas_call(
        body, name="flash_bwd", grid=(HEADS, lp // tq),
        in_specs=[tile, tile, pl.BlockSpec((d_qk, tq), lambda hd, i: (hd, i)), pl.BlockSpec((V_HEAD, tq), lambda hd, i: (hd, i)),
                  rowv, rowv, head, head, pl.BlockSpec((d_qk, lp), lambda hd, i: (hd, 0))],
        out_specs=[tile, pl.BlockSpec((d_qk, lp), lambda hd, i: (hd, 0)), pl.BlockSpec((V_HEAD, lp), lambda hd, i: (hd, 0))],
        out_shape=[jax.ShapeDtypeStruct((lp, D_EXP), F32), jax.ShapeDtypeStruct((HEADS * d_qk, lp), F32),
                   jax.ShapeDtypeStruct((HEADS * V_HEAD, lp), F32)],
        scratch_shapes=[pltpu.VMEM((d_qk, tq), F32)],
        compiler_params=_cparams(("parallel", "arbitrary")),
    )(q, d_o, q_t, do_t, lse_row, delta_row, k, v, k_t)


def _attn_prep_bwd(dq, dk_t, dv_t, proj, q_norm_w, kv_norm_w, wq_pt, wk_pt, wv_pt, cos, sina, sinb):
    lp = proj.shape[0]
    tm = _row_tile(lp)

    def body(dq_ref, dk_ref, dv_ref, ql_ref, kvl_ref, qw_ref, kw_ref, wqt_ref, wkt_ref, wvt_ref, cos_ref, sa_ref, sb_ref,
             dql_ref, dkvl_ref, dkr_ref, dwq_ref, dwk_ref, dwv_ref, vec_ref):
        @pl.when(pl.program_id(0) == 0)
        def _():
            dwq_ref[...] = jnp.zeros_like(dwq_ref)
            dwk_ref[...] = jnp.zeros_like(dwk_ref)
            dwv_ref[...] = jnp.zeros_like(dwv_ref)
            vec_ref[...] = jnp.zeros_like(vec_ref)

        cos_t, sa_t, sb_t = cos_ref[...], sa_ref[...], sb_ref[...]

        def head_rows(t_ref, per):
            pad = jnp.zeros((HEAD_PAD - per, tm), F32)
            return jnp.concatenate(
                [jnp.concatenate([t_ref[hd * per:(hd + 1) * per, :], pad], axis=0).T for hd in range(HEADS)], axis=-1)

        dkp = head_rows(dk_ref, D_QK)
        dqp = jnp.concatenate(
            [_rope_transpose(dq_ref[:, hd * HEAD_PAD:(hd + 1) * HEAD_PAD] * SCALE, cos_t, sa_t, sb_t) for hd in range(HEADS)],
            axis=-1)
        dkr = dkp[:, 0:HEAD_PAD]
        for hd in range(1, HEADS):
            dkr = dkr + dkp[:, hd * HEAD_PAD:(hd + 1) * HEAD_PAD]
        dkr_ref[...] = _rope_transpose(dkr, cos_t, sa_t, sb_t)

        qn, r_q = _rms_fwd(ql_ref[...], qw_ref[...], Q_LORA)
        kvn, r_kv = _rms_fwd(kvl_ref[...], kw_ref[...], KV_LORA)
        dqp_b, dkp_b, dv_b = dqp.astype(BF16), dkp.astype(BF16), head_rows(dv_ref, V_HEAD).astype(BF16)
        dqn = _dot(dqp_b, wqt_ref[...])
        dkvn = _dot(dkp_b, wkt_ref[...]) + _dot(dv_b, wvt_ref[...])
        dwq_ref[...] += _dot_tn(qn.astype(BF16), dqp_b)
        dwk_ref[...] += _dot_tn(kvn.astype(BF16), dkp_b)
        dwv_ref[...] += _dot_tn(kvn.astype(BF16), dv_b)
        dql, d_qw = _rms_bwd(ql_ref[...], r_q, qw_ref[...], dqn, Q_LORA)
        dkvl, d_kw = _rms_bwd(kvl_ref[...], r_kv, kw_ref[...], dkvn, KV_LORA)
        dql_ref[...] = dql
        dkvl_ref[...] = dkvl
        vec_ref[0:1, :] += d_qw
        vec_ref[1:2, 0:KV_LORA] += d_kw

    tab = _rows(tm, (0, HEAD_PAD))
    full = _rows(tm, (0, D_EXP))
    return pl.pallas_call(
        body, name="attn_prep_bwd", grid=(lp // tm,),
        in_specs=[full, pl.BlockSpec((HEADS * D_QK, tm), lambda i: (0, i)), pl.BlockSpec((D_ATTN, tm), lambda i: (0, i)),
                  _rows(tm, P_QLAT), _rows(tm, P_KVLAT), _whole((1, Q_LORA)), _whole((1, KV_LORA)),
                  _whole((D_EXP, Q_LORA)), _whole((D_EXP, KV_LORA)), _whole((D_EXP, KV_LORA)), tab, tab, tab],
        out_specs=[_rows(tm, (0, Q_LORA)), _rows(tm, (0, KV_LORA)), _rows(tm, (0, HEAD_PAD)),
                   _out_whole((Q_LORA, D_EXP)), _out_whole((KV_LORA, D_EXP)), _out_whole((KV_LORA, D_EXP)),
                   _out_whole((8, Q_LORA))],
        out_shape=[jax.ShapeDtypeStruct((lp, Q_LORA), F32), jax.ShapeDtypeStruct((lp, KV_LORA), F32),
                   jax.ShapeDtypeStruct((lp, HEAD_PAD), F32), jax.ShapeDtypeStruct((Q_LORA, D_EXP), F32),
                   jax.ShapeDtypeStruct((KV_LORA, D_EXP), F32), jax.ShapeDtypeStruct((KV_LORA, D_EXP), F32),
                   jax.ShapeDtypeStruct((8, Q_LORA), F32)],
        compiler_params=_cparams(("arbitrary",)),
    )(dq, dk_t, dv_t, proj, proj, q_norm_w, kv_norm_w, wq_pt, wk_pt, wv_pt, cos, sina, sinb)


def _in_proj_bwd(h, pre_w, dres, dga, du0, du1, dyp, ssm_d, dsg, dql, dkvl, dkr, w_in_pt):
    lp = h.shape[0]
    tm = 128
    pieces = (P_GATE_A, P_U, P_GATE_S, P_QLAT, P_KVLAT, P_KROPE)

    def body(h_ref, w_ref, dres_ref, dga_ref, du0_ref, du1_ref, dyp_ref, d_ref, dsg_ref, dql_ref, dkvl_ref, dkr_ref, wt_ref,
             dh_ref, dw_ref, vec_ref):
        @pl.when(pl.program_id(0) == 0)
        def _():
            dw_ref[...] = jnp.zeros_like(dw_ref)
            vec_ref[...] = jnp.zeros_like(vec_ref)

        hv = h_ref[...]
        xn, r = _rms_fwd(hv, w_ref[...], D_MODEL)
        xn_b = xn.astype(BF16)
        du = du0_ref[...] + du1_ref[...] + dyp_ref[...] * d_ref[...]
        grads = (dga_ref[...], du, dsg_ref[...], dql_ref[...], dkvl_ref[...], dkr_ref[...])
        dxn = jnp.zeros((tm, D_MODEL), F32)
        for (off, width), g in zip(pieces, grads):
            g_b = g.astype(BF16)
            dxn = dxn + _dot(g_b, wt_ref[off:off + width, :])
            dw_ref[:, off:off + width] += _dot_tn(xn_b, g_b)
        dx, d_w = _rms_bwd(hv, r, w_ref[...], dxn, D_MODEL)
        dh_ref[...] = dres_ref[...] + dx
        vec_ref[0:1, :] += d_w

    full = _rows(tm, (0, D_MODEL))
    half = _rows(tm, (0, D_SSM))
    return pl.pallas_call(
        body, name="in_proj_bwd", grid=(lp // tm,),
        in_specs=[full, _whole((1, D_MODEL)), full, full, half, half, half, _whole((1, D_SSM)), half,
                  _rows(tm, (0, Q_LORA)), _rows(tm, (0, KV_LORA)), _rows(tm, (0, HEAD_PAD)), _whole((D_PROJ, D_MODEL))],
        out_specs=[full, _out_whole((D_MODEL, D_PROJ)), _out_whole((8, D_MODEL))],
        out_shape=[jax.ShapeDtypeStruct((lp, D_MODEL), F32), jax.ShapeDtypeStruct((D_MODEL, D_PROJ), F32),
                   jax.ShapeDtypeStruct((8, D_MODEL), F32)],
        compiler_params=_cparams(("arbitrary",)),
    )(h, pre_w, dres, dga, du0, du1, dyp, ssm_d, dsg, dql, dkvl, dkr, w_in_pt)


def _other_chips(x, y):
    return [(1 - x, y), (x, 1 - y), (1 - x, 1 - y)]


def _gather_weights(w_bf16, meta):
    any_spec = pl.BlockSpec(memory_space=pl.ANY)

    def body(w_ref, m_ref, wout_ref, mout_ref, send_sems, recv_sems, local_sems):
        x, y, c = lax.axis_index("x"), lax.axis_index("y"), lax.axis_index("c")
        me = 2 * x + y
        own = [pltpu.make_async_copy(w_ref, wout_ref.at[me], local_sems.at[0]),
               pltpu.make_async_copy(m_ref, mout_ref.at[me], local_sems.at[1])]
        for cp in own:
            cp.start()
        sends = []
        for j, (tx, ty) in enumerate(_other_chips(x, y)):
            for n, (src, dst) in enumerate(((w_ref, wout_ref), (m_ref, mout_ref))):
                sends.append(pltpu.make_async_remote_copy(
                    src_ref=src, dst_ref=dst.at[me], send_sem=send_sems.at[2 * j + n], recv_sem=recv_sems.at[2 * j + n],
                    device_id=(tx, ty, c), device_id_type=MESH))
        for cp in sends:
            cp.start()
        for j, (tx, ty) in enumerate(_other_chips(x, y)):
            for n, (src, dst) in enumerate(((w_ref, wout_ref), (m_ref, mout_ref))):
                pltpu.make_async_remote_copy(
                    src_ref=src, dst_ref=dst.at[2 * tx + ty], send_sem=send_sems.at[2 * j + n],
                    recv_sem=recv_sems.at[2 * j + n], device_id=(tx, ty, c), device_id_type=MESH).wait_recv()
        for cp in sends:
            cp.wait_send()
        for cp in own:
            cp.wait()

    return pl.pallas_call(
        body, name="gather_weights",
        in_specs=[any_spec, any_spec], out_specs=[any_spec, any_spec],
        out_shape=[jax.ShapeDtypeStruct((4,) + w_bf16.shape, w_bf16.dtype), jax.ShapeDtypeStruct((4,) + meta.shape, meta.dtype)],
        scratch_shapes=[pltpu.SemaphoreType.DMA((6,)), pltpu.SemaphoreType.DMA((6,)), pltpu.SemaphoreType.DMA((2,))],
    )(w_bf16, meta)


def _swap_sibling(g):
    any_spec = pl.BlockSpec(memory_space=pl.ANY)

    def body(g_ref, out_ref, send_sem, recv_sem):
        x, y, c = lax.axis_index("x"), lax.axis_index("y"), lax.axis_index("c")
        cp = pltpu.make_async_remote_copy(src_ref=g_ref, dst_ref=out_ref, send_sem=send_sem, recv_sem=recv_sem,
                                          device_id=(x, y, 1 - c), device_id_type=MESH)
        cp.start()
        cp.wait()

    return pl.pallas_call(
        body, name="swap_sibling", in_specs=[any_spec], out_specs=any_spec,
        out_shape=jax.ShapeDtypeStruct(g.shape, g.dtype),
        scratch_shapes=[pltpu.SemaphoreType.DMA(()), pltpu.SemaphoreType.DMA(())],
    )(g)


def _pair_sum(a, b):
    rows = a.shape[0]
    tm = _pick_tile(rows, 1024)

    def body(a_ref, b_ref, o_ref):
        o_ref[...] = a_ref[...] + b_ref[...]

    spec = pl.BlockSpec((tm, 1024), lambda i: (i, 0))
    return pl.pallas_call(body, name="pair_sum", grid=(rows // tm,), in_specs=[spec, spec], out_specs=spec,
                          out_shape=jax.ShapeDtypeStruct(a.shape, F32), compiler_params=_cparams(("parallel",)))(a, b)


def _scatter_chips(s, rs, rsm):
    any_spec = pl.BlockSpec(memory_space=pl.ANY)

    def body(s_ref, out_ref, send_sems, recv_sems, local_sems):
        x, y, c = lax.axis_index("x"), lax.axis_index("y"), lax.axis_index("c")
        me = 2 * x + y
        small = s_ref.at[pl.ds(4 * rs, rsm)]

        def pieces(target):
            return ((s_ref.at[pl.ds(pl.multiple_of(target * rs, 8), rs)], pl.ds(0, rs)), (small, pl.ds(rs, rsm)))

        own = [pltpu.make_async_copy(src, out_ref.at[me, rows], local_sems.at[n]) for n, (src, rows) in enumerate(pieces(me))]
        for cp in own:
            cp.start()
        sends = []
        for j, (tx, ty) in enumerate(_other_chips(x, y)):
            for n, (src, rows) in enumerate(pieces(2 * tx + ty)):
                sends.append(pltpu.make_async_remote_copy(
                    src_ref=src, dst_ref=out_ref.at[me, rows], send_sem=send_sems.at[2 * j + n],
                    recv_sem=recv_sems.at[2 * j + n], device_id=(tx, ty, c), device_id_type=MESH))
        for cp in sends:
            cp.start()
        for j, (tx, ty) in enumerate(_other_chips(x, y)):
            for n, (src, rows) in enumerate(pieces(me)):
                pltpu.make_async_remote_copy(
                    src_ref=src, dst_ref=out_ref.at[2 * tx + ty, rows], send_sem=send_sems.at[2 * j + n],
                    recv_sem=recv_sems.at[2 * j + n], device_id=(tx, ty, c), device_id_type=MESH).wait_recv()
        for cp in sends:
            cp.wait_send()
        for cp in own:
            cp.wait()

    return pl.pallas_call(
        body, name="scatter_chips", in_specs=[any_spec], out_specs=any_spec,
        out_shape=jax.ShapeDtypeStruct((4, rs + rsm, 1024), F32),
        scratch_shapes=[pltpu.SemaphoreType.DMA((6,)), pltpu.SemaphoreType.DMA((6,)), pltpu.SemaphoreType.DMA((2,))],
    )(s)


def _adamw(parts, w, m, v):
    rows = w.shape[0]
    tm = _pick_tile(rows, 256)
    c1 = 1.0 / (1.0 - ADAM_B1 ** ADAM_STEP)
    c2 = 1.0 / (1.0 - ADAM_B2 ** ADAM_STEP)

    def body(p_ref, w_ref, m_ref, v_ref, g_ref, d_ref, nm_ref, nv_ref):
        g = ((p_ref[0] + p_ref[1]) + p_ref[2]) + p_ref[3]
        nm = ADAM_B1 * m_ref[...] + (1.0 - ADAM_B1) * g
        nv = ADAM_B2 * v_ref[...] + (1.0 - ADAM_B2) * (g * g)
        g_ref[...] = g
        nm_ref[...] = nm
        nv_ref[...] = nv
        d_ref[...] = -ADAM_LR * ((nm * c1) / (jnp.sqrt(nv * c2) + ADAM_EPS) + ADAM_WD * w_ref[...])

    spec = pl.BlockSpec((tm, 1024), lambda i: (i, 0))
    out = jax.ShapeDtypeStruct(w.shape, F32)
    return pl.pallas_call(
        body, name="adamw", grid=(rows // tm,),
        in_specs=[pl.BlockSpec((4, tm, 1024), lambda i: (0, i, 0)), spec, spec, spec],
        out_specs=[spec] * 4, out_shape=[out] * 4, compiler_params=_cparams(("parallel",)),
    )(parts, w, m, v)


def _expand_heads(a, axis, per_head):
    a = jnp.moveaxis(a, axis, -1)
    lead = a.shape[:-1]
    a = a.reshape(lead + (HEADS, per_head))
    a = jnp.pad(a, [(0, 0)] * len(lead) + [(0, 0), (0, HEAD_PAD - per_head)])
    return jnp.moveaxis(a.reshape(lead + (D_EXP,)), -1, axis)


def _compact_heads(a, axis, start, size):
    a = jnp.moveaxis(a, axis, -1)
    lead = a.shape[:-1]
    a = a.reshape(lead + (HEADS, HEAD_PAD))[..., start:start + size]
    return jnp.moveaxis(a.reshape(lead + (HEADS * size,)), -1, axis)


def _block_diag(w):
    g, a, b = w.shape
    per = g // SSM_BLOCKS
    eye = jnp.eye(per, dtype=w.dtype)
    return jnp.einsum("jgab,gk->jgakb", w.reshape(SSM_BLOCKS, per, a, b), eye).reshape(SSM_BLOCKS, per * a, per * b)


def _block_diag_extract(dense, a, b):
    per = N_GROUPS // SSM_BLOCKS
    d5 = dense.reshape(SSM_BLOCKS, per, a, per, b)
    return jnp.einsum("jgakb,gk->jgab", d5, jnp.eye(per, dtype=dense.dtype)).reshape(N_GROUPS, a, b)


def _discretise(a_re, a_im, log_dt, b_re, b_im):
    dt = jnp.exp(log_dt)[:, None]
    mag = jnp.exp(a_re * dt)
    abar_re = mag * jnp.cos(a_im * dt)
    abar_im = mag * jnp.sin(a_im * dt)
    num_re = abar_re - 1.0
    num_im = abar_im
    den = a_re * a_re + a_im * a_im
    coef_re = (num_re * a_re + num_im * a_im) / den
    coef_im = (num_im * a_re - num_re * a_im) / den
    bbar_re = coef_re[..., None] * b_re - coef_im[..., None] * b_im
    bbar_im = coef_re[..., None] * b_im + coef_im[..., None] * b_re
    return abar_re, abar_im, bbar_re, bbar_im


def _scan_coef(ar, ai, reverse):
    ar, ai = ar.reshape(1, N_STATE), ai.reshape(1, N_STATE)
    pows = [(ar, ai)]
    for _ in range(7):
        pr, pi_ = pows[-1]
        pows.append((pr * ar - pi_ * ai, pr * ai + pi_ * ar))
    row = jnp.arange(8)[:, None]
    out = []
    for k in (1, 2, 4):
        keep = (row < 8 - k) if reverse else (row >= k)
        out += [jnp.where(keep, pows[k - 1][0], 0.0), jnp.where(keep, pows[k - 1][1], 0.0)]
    order = list(range(7, -1, -1)) if reverse else list(range(8))
    out += [jnp.concatenate([pows[k][0] for k in order], axis=0), jnp.concatenate([pows[k][1] for k in order], axis=0)]
    return jnp.stack(out).astype(F32)


def _flat_rows(a, rows):
    flat = a.reshape(-1)
    return jnp.pad(flat, (0, rows * 1024 - flat.shape[0])).reshape(rows, 1024)


def _pack(named, order):
    rows = [-(-math.prod(named[n].shape) // 1024) for n in order]
    total = -(-sum(rows) // 8) * 8
    parts = [_flat_rows(named[n], r) for n, r in zip(order, rows)]
    if total > sum(rows):
        parts.append(jnp.zeros((total - sum(rows), 1024), parts[0].dtype))
    return jnp.concatenate(parts, axis=0)


def _unpack(packed, shapes, order):
    out, at = {}, 0
    for n in order:
        size = math.prod(shapes[n])
        rows = -(-size // 1024)
        out[n] = packed[at:at + rows].reshape(-1)[:size].reshape(shapes[n])
        at += rows
    return out


def _shard_cols(a, k):
    w = a.shape[-1] // 4
    return a[..., k * w:(k + 1) * w]


def kernel(x, meta_tokens, pre_norm_w, post_norm_w, w_in, q_norm_w, w_q_up, kv_norm_w, w_kv_up, attn_out_norm_w, ssm_a_re, ssm_a_im, ssm_log_dt, ssm_b_re, ssm_b_im, ssm_c_re, ssm_c_im, ssm_d, w_glu, b_glu, ssm_out_norm_w, w_out, loss_target, m_meta_tokens, m_pre_norm_w, m_post_norm_w, m_w_in, m_q_norm_w, m_w_q_up, m_kv_norm_w, m_w_kv_up, m_attn_out_norm_w, m_ssm_a_re, m_ssm_a_im, m_ssm_log_dt, m_ssm_b_re, m_ssm_b_im, m_ssm_c_re, m_ssm_c_im, m_ssm_d, m_w_glu, m_b_glu, m_ssm_out_norm_w, m_w_out, v_meta_tokens, v_pre_norm_w, v_post_norm_w, v_w_in, v_q_norm_w, v_w_q_up, v_kv_norm_w, v_w_kv_up, v_attn_out_norm_w, v_ssm_a_re, v_ssm_a_im, v_ssm_log_dt, v_ssm_b_re, v_ssm_b_im, v_ssm_c_re, v_ssm_c_im, v_ssm_d, v_w_glu, v_b_glu, v_ssm_out_norm_w, v_w_out):
    local = dict(meta_tokens=meta_tokens, pre_norm_w=pre_norm_w, post_norm_w=post_norm_w, w_in=w_in, q_norm_w=q_norm_w,
                 w_q_up=w_q_up, kv_norm_w=kv_norm_w, w_kv_up=w_kv_up, attn_out_norm_w=attn_out_norm_w, ssm_a_re=ssm_a_re,
                 ssm_a_im=ssm_a_im, ssm_log_dt=ssm_log_dt, ssm_b_re=ssm_b_re, ssm_b_im=ssm_b_im, ssm_c_re=ssm_c_re,
                 ssm_c_im=ssm_c_im, ssm_d=ssm_d, w_glu=w_glu, b_glu=b_glu, ssm_out_norm_w=ssm_out_norm_w, w_out=w_out)
    mom_m = dict(meta_tokens=m_meta_tokens, pre_norm_w=m_pre_norm_w, post_norm_w=m_post_norm_w, w_in=m_w_in,
                 q_norm_w=m_q_norm_w, w_q_up=m_w_q_up, kv_norm_w=m_kv_norm_w, w_kv_up=m_w_kv_up,
                 attn_out_norm_w=m_attn_out_norm_w, ssm_a_re=m_ssm_a_re, ssm_a_im=m_ssm_a_im, ssm_log_dt=m_ssm_log_dt,
                 ssm_b_re=m_ssm_b_re, ssm_b_im=m_ssm_b_im, ssm_c_re=m_ssm_c_re, ssm_c_im=m_ssm_c_im, ssm_d=m_ssm_d,
                 w_glu=m_w_glu, b_glu=m_b_glu, ssm_out_norm_w=m_ssm_out_norm_w, w_out=m_w_out)
    mom_v = dict(meta_tokens=v_meta_tokens, pre_norm_w=v_pre_norm_w, post_norm_w=v_post_norm_w, w_in=v_w_in,
                 q_norm_w=v_q_norm_w, w_q_up=v_w_q_up, kv_norm_w=v_kv_norm_w, w_kv_up=v_w_kv_up,
                 attn_out_norm_w=v_attn_out_norm_w, ssm_a_re=v_ssm_a_re, ssm_a_im=v_ssm_a_im, ssm_log_dt=v_ssm_log_dt,
                 ssm_b_re=v_ssm_b_re, ssm_b_im=v_ssm_b_im, ssm_c_re=v_ssm_c_re, ssm_c_im=v_ssm_c_im, ssm_d=v_ssm_d,
                 w_glu=v_w_glu, b_glu=v_b_glu, ssm_out_norm_w=v_ssm_out_norm_w, w_out=v_w_out)
    shapes = {n: local[n].shape for n in WEIGHTS}
    mat = ("w_in", "w_q_up", "w_kv_up", "w_glu", "w_out")

    seq = x.shape[1]
    l_real = N_META + seq
    lp = -(-l_real // 1280) * 1280 if l_real > 1280 else -(-l_real // QBLK) * QBLK
    assert seq % 128 == 0 and lp % QBLK == 0

    w_shard = _pack({n: local[n].astype(BF16) for n in mat}, mat)
    w_shard = jnp.pad(w_shard, ((0, -w_shard.shape[0] % 16), (0, 0)))
    w_all, meta_all = _gather_weights(w_shard, meta_tokens)
    mat_shapes = {n: shapes[n] for n in mat}
    per_chip = [_unpack(w_all[k], mat_shapes, mat) for k in range(4)]
    w_in_f = jnp.concatenate([p["w_in"][0] for p in per_chip], axis=1)
    w_q_f = jnp.concatenate([p["w_q_up"][0] for p in per_chip], axis=1)
    w_kv_f = jnp.concatenate([p["w_kv_up"][0] for p in per_chip], axis=1)
    w_glu_f = jnp.concatenate([p["w_glu"][0] for p in per_chip], axis=1)
    w_out_f = jnp.concatenate([p["w_out"][0] for p in per_chip], axis=0)
    meta_f = jnp.concatenate([meta_all[k] for k in range(4)], axis=1)

    o_q, o_kv, o_kr, o_ga, o_u, o_gs = 0, 256, 384, 416, 928, 1440
    krope_cols = jnp.pad(w_in_f[:, o_kr:o_ga], ((0, 0), (QK_NOPE, HEAD_PAD - QK_NOPE - QK_ROPE)))
    w_in_p = jnp.concatenate([_expand_heads(w_in_f[:, o_ga:o_u], 1, V_HEAD), w_in_f[:, o_u:o_gs], w_in_f[:, o_gs:],
                              w_in_f[:, o_q:o_kv], w_in_f[:, o_kv:o_kr], krope_cols], axis=1)
    wq_p = _expand_heads(w_q_f, 1, QK_NOPE + QK_ROPE)
    kv3 = w_kv_f.reshape(KV_LORA, HEADS, QK_NOPE + V_HEAD)
    wk_p = _expand_heads(kv3[:, :, :QK_NOPE].reshape(KV_LORA, HEADS * QK_NOPE), 1, QK_NOPE)
    wv_c = kv3[:, :, QK_NOPE:].reshape(KV_LORA, HEADS * V_HEAD)
    wv_p = _expand_heads(wv_c, 1, V_HEAD)
    w_out_a = _expand_heads(w_out_f[:D_ATTN], 0, V_HEAD)
    w_out_s = w_out_f[D_ATTN:]
    attn_norm_e = _expand_heads(attn_out_norm_w, 1, V_HEAD)

    pos = jnp.arange(lp, dtype=jnp.int32)
    half = QK_ROPE // 2
    inv = ROPE_THETA ** (-jnp.arange(half, dtype=F32) / half)
    ang = pos.astype(F32)[:, None] * inv[None, :]
    cos16, sin16 = jnp.cos(ang), jnp.sin(ang)
    ones, zeros = jnp.ones((lp, QK_NOPE), F32), jnp.zeros((lp, QK_NOPE), F32)
    tail1, tail0 = jnp.ones((lp, HEAD_PAD - MASK_LANE), F32), jnp.zeros((lp, HEAD_PAD - MASK_LANE), F32)
    z16 = jnp.zeros((lp, half), F32)
    cos = jnp.concatenate([ones, cos16, cos16, tail1], axis=1)
    sina = jnp.concatenate([zeros, z16, sin16, tail0], axis=1)
    sinb = jnp.concatenate([zeros, -sin16, z16, tail0], axis=1)

    disc_in = (ssm_a_re[0], ssm_a_im[0], ssm_log_dt[0], ssm_b_re[0], ssm_b_im[0])
    disc = lambda a_re, a_im, ldt, b_re, b_im: jax.vmap(_discretise)(a_re, a_im, ldt, b_re, b_im)
    (abar_re, abar_im, bbar_re, bbar_im), disc_vjp = jax.vjp(disc, *disc_in)
    ssm = []
    for d in range(2):
        rev = d == 1
        b_re_bd = _block_diag(jnp.swapaxes(bbar_re[d], 1, 2)).astype(BF16)
        b_im_bd = _block_diag(jnp.swapaxes(bbar_im[d], 1, 2)).astype(BF16)
        c_re_bd = _block_diag(jnp.swapaxes(ssm_c_re[0, d], 1, 2)).astype(BF16)
        c_im_bd = _block_diag(jnp.swapaxes(-ssm_c_im[0, d], 1, 2)).astype(BF16)
        ssm.append(dict(rev=rev, coef=_scan_coef(abar_re[d], abar_im[d], rev),
                        coef_adj=_scan_coef(abar_re[d], -abar_im[d], not rev),
                        b_re=b_re_bd, b_im=b_im_bd, c_re=c_re_bd, c_im=c_im_bd))

    h = jnp.concatenate([meta_f, x[0], jnp.zeros((lp - l_real, D_MODEL), F32)], axis=0)
    proj = _in_proj_fwd(h, pre_norm_w, w_in_p)
    q, k, v, vt, q_t, k_t = _attn_prep_fwd(proj, q_norm_w, kv_norm_w, wq_p, wk_p, wv_p, wv_c.T, cos, sina, sinb, l_real)
    o_exp, lse = _flash_fwd(q, k, vt)
    ys, states = [], []
    for s in ssm:
        y_d, st_d = _ssm_fwd(proj, s["coef"], s["b_re"], s["b_im"], s["c_re"], s["c_im"], s["rev"])
        ys.append(y_d)
        states.append(st_d)

    (d_o, do_t, delta, dga, dyp, dsg, dres, dwoa, dwos, dwglu, vec_mid) = _mid(
        h, loss_target[0], o_exp, proj, ys[0], ys[1], ssm_d, w_glu_f, w_glu_f.T, b_glu, ssm_out_norm_w, attn_norm_e, w_out_a, w_out_s,
        w_out_a.T, w_out_s.T, post_norm_w, l_real)
    dus, dssm = [], []
    tr = lambda a: jnp.swapaxes(a, 1, 2)
    for s, st_d in zip(ssm, states):
        du_d, dbre, dbim, dcre, dcim, da = _ssm_bwd(proj, dyp, st_d, s["coef"], s["coef_adj"], s["b_re"], s["b_im"],
                                                    tr(s["b_re"]), tr(s["b_im"]), tr(s["c_re"]), tr(s["c_im"]), s["rev"])
        dus.append(du_d)
        dssm.append((dbre, dbim, dcre, dcim, da))
    dq, dk_t, dv_t = _flash_bwd(q, k, v, d_o, q_t, k_t, do_t, lse, delta.T.reshape(HEADS, 1, lp))
    dql, dkvl, dkr, dwq_p, dwk_p, dwv_p, vec_prep = _attn_prep_bwd(
        dq, dk_t, dv_t, proj, q_norm_w, kv_norm_w, wq_p.T, wk_p.T, wv_p.T, cos, sina, sinb)
    dh, dwin_p, vec_in = _in_proj_bwd(h, pre_norm_w, dres, dga, dus[0], dus[1], dyp, ssm_d, dsg, dql, dkvl, dkr, w_in_p.T)

    grads = {}
    grads["w_in"] = jnp.concatenate([
        dwin_p[:, P_QLAT[0]:P_QLAT[0] + 256], dwin_p[:, P_KVLAT[0]:P_KVLAT[0] + 128],
        dwin_p[:, P_KROPE[0] + QK_NOPE:P_KROPE[0] + QK_NOPE + QK_ROPE], _compact_heads(dwin_p[:, 0:D_EXP], 1, 0, V_HEAD),
        dwin_p[:, P_U[0]:P_U[0] + 512], dwin_p[:, P_GATE_S[0]:P_GATE_S[0] + 512]], axis=1)[None]
    grads["w_q_up"] = _compact_heads(dwq_p, 1, 0, QK_NOPE + QK_ROPE)[None]
    dwk3 = _compact_heads(dwk_p, 1, 0, QK_NOPE).reshape(KV_LORA, HEADS, QK_NOPE)
    dwv3 = _compact_heads(dwv_p, 1, 0, V_HEAD).reshape(KV_LORA, HEADS, V_HEAD)
    grads["w_kv_up"] = jnp.concatenate([dwk3, dwv3], axis=2).reshape(1, KV_LORA, HEADS * (QK_NOPE + V_HEAD))
    grads["w_glu"] = dwglu[None]
    grads["w_out"] = jnp.concatenate([_compact_heads(dwoa, 0, 0, V_HEAD), dwos], axis=0)[None]
    grads["meta_tokens"] = dh[:N_META]
    grads["pre_norm_w"] = vec_in[0:1]
    grads["post_norm_w"] = vec_mid[0:1]
    grads["q_norm_w"] = vec_prep[0:1]
    grads["kv_norm_w"] = vec_prep[1:2, :KV_LORA]
    grads["attn_out_norm_w"] = _compact_heads(vec_mid[1:2], 1, 0, V_HEAD)
    grads["ssm_out_norm_w"] = vec_mid[2:3, :D_SSM]
    grads["ssm_d"] = vec_mid[3:4, :D_SSM]
    grads["b_glu"] = vec_mid[4:5]
    d_abar_re = jnp.stack([dssm[d][4][0].sum(axis=0).reshape(N_GROUPS, SSM_STATE) for d in range(2)])
    d_abar_im = jnp.stack([dssm[d][4][1].sum(axis=0).reshape(N_GROUPS, SSM_STATE) for d in range(2)])
    d_bbar_re = jnp.stack([jnp.swapaxes(_block_diag_extract(dssm[d][0], SSM_GROUP, SSM_STATE), 1, 2) for d in range(2)])
    d_bbar_im = jnp.stack([jnp.swapaxes(_block_diag_extract(dssm[d][1], SSM_GROUP, SSM_STATE), 1, 2) for d in range(2)])
    da_re, da_im, dlog_dt, db_re, db_im = disc_vjp((d_abar_re, d_abar_im, d_bbar_re, d_bbar_im))
    grads["ssm_a_re"], grads["ssm_a_im"], grads["ssm_log_dt"] = da_re[None], da_im[None], dlog_dt[None]
    grads["ssm_b_re"], grads["ssm_b_im"] = db_re[None], db_im[None]
    grads["ssm_c_re"] = jnp.stack([_block_diag_extract(dssm[d][2], SSM_GROUP, SSM_STATE) for d in range(2)])[None]
    grads["ssm_c_im"] = jnp.stack([_block_diag_extract(dssm[d][3], SSM_GROUP, SSM_STATE) for d in range(2)])[None]

    def shard_of(n, a, kk):
        return a[:, kk * 256:(kk + 1) * 256] if n == "w_out" else _shard_cols(a, kk)

    slices = [_pack({n: shard_of(n, grads[n], kk) for n in BIG}, BIG) for kk in range(4)]
    grads["loss"] = vec_mid[5:6, 0:1]
    small = _pack({n: grads[n] for n in SMALL + ("loss",)}, SMALL + ("loss",))
    loss_row = slices[0].shape[0] + sum(-(-math.prod(shapes[n]) // 1024) for n in SMALL)
    rs, rsm = slices[0].shape[0], small.shape[0]
    g_pack = jnp.concatenate(slices + [small], axis=0)
    g_pair = _pair_sum(g_pack, _swap_sibling(g_pack))
    parts = _scatter_chips(g_pair, rs, rsm)

    order = BIG + SMALL
    big_shapes = {n: shapes[n] for n in BIG}
    small_shapes = {n: shapes[n] for n in SMALL}

    def pack_state(named):
        return jnp.concatenate([_pack({n: named[n] for n in BIG}, BIG), _pack({n: named[n] for n in SMALL}, SMALL)], axis=0)

    g_out, d_out, m_out, v_out = _adamw(parts, pack_state(local), pack_state(mom_m), pack_state(mom_v))

    def unpack_state(p):
        out = _unpack(p[:rs], big_shapes, BIG)
        out.update(_unpack(p[rs:], small_shapes, SMALL))
        return out

    g_fin, d_fin, m_fin, v_fin = unpack_state(g_out), unpack_state(d_out), unpack_state(m_out), unpack_state(v_out)
    loss = g_out[loss_row, 0]
    grad_x = dh[N_META:l_real][None]
    return (loss, grad_x, *[g_fin[n] for n in WEIGHTS], *[d_fin[n] for n in WEIGHTS], *[m_fin[n] for n in WEIGHTS],
            *[v_fin[n] for n in WEIGHTS])
```

```python
import functools
import math

import jax
import jax.numpy as jnp
from jax import lax
from jax.experimental import pallas as pl
from jax.experimental.pallas import tpu as pltpu

F32 = jnp.float32
BF16 = jnp.bfloat16
MESH = pl.DeviceIdType.MESH

D_MODEL = 1024
N_META = 16
EPS = 1e-6
HEADS = 8
QK_NOPE = 64
QK_ROPE = 32
V_HEAD = 64
VT_ROWS = 80
Q_LORA = 256
KV_LORA = 128
D_ATTN = 512
D_SSM = 512
SSM_GROUP = 16
N_GROUPS = 32
SSM_STATE = 64
N_STATE = N_GROUPS * SSM_STATE
ROPE_THETA = 10000.0
HEAD_PAD = 128
D_EXP = HEADS * HEAD_PAD
D_QK = QK_NOPE + QK_ROPE
MASK_LANE = D_QK
NEG_BIG = -1e30
SCALE = 1.0 / math.sqrt(QK_NOPE + QK_ROPE)
LOG2E = math.log2(math.e)
SCALE2 = SCALE * LOG2E
QBLK = 256
SCAN_COLS = 512
SSM_BLOCKS = 4
BLK_CH = D_SSM // SSM_BLOCKS
BLK_ST = N_STATE // SSM_BLOCKS

P_GATE_A = (0, 1024)
P_U = (1024, 512)
P_GATE_S = (1536, 512)
P_QLAT = (2048, 256)
P_KVLAT = (2304, 128)
P_KROPE = (2432, 128)
D_PROJ = 2560

ADAM_LR = 0.001
ADAM_B1 = 0.9
ADAM_B2 = 0.999
ADAM_EPS = 1e-08
ADAM_WD = 0.01
ADAM_STEP = 10

VMEM_LIMIT = 60 * 1024 * 1024

BIG = ("w_in", "w_q_up", "w_kv_up", "w_glu", "w_out", "meta_tokens")
SMALL = ("pre_norm_w", "post_norm_w", "q_norm_w", "kv_norm_w", "attn_out_norm_w", "ssm_a_re", "ssm_a_im",
         "ssm_log_dt", "ssm_b_re", "ssm_b_im", "ssm_c_re", "ssm_c_im", "ssm_d", "b_glu", "ssm_out_norm_w")
WEIGHTS = ("meta_tokens", "pre_norm_w", "post_norm_w", "w_in", "q_norm_w", "w_q_up", "kv_norm_w", "w_kv_up",
           "attn_out_norm_w", "ssm_a_re", "ssm_a_im", "ssm_log_dt", "ssm_b_re", "ssm_b_im", "ssm_c_re", "ssm_c_im",
           "ssm_d", "w_glu", "b_glu", "ssm_out_norm_w", "w_out")


def _cparams(sem=None):
    return pltpu.CompilerParams(dimension_semantics=sem, vmem_limit_bytes=VMEM_LIMIT)


def _dot(a, b):
    return jnp.dot(a, b, preferred_element_type=F32)


def _dot_nt(a, b):
    return lax.dot_general(a, b, (((1,), (1,)), ((), ())), preferred_element_type=F32)


def _dot_tn(a, b):
    return lax.dot_general(a, b, (((0,), (0,)), ((), ())), preferred_element_type=F32)


def _sigmoid(x):
    return 1.0 / (1.0 + jnp.exp(-x))


def _rms_fwd(x, w, n):
    r = lax.rsqrt(jnp.sum(x * x, axis=-1, keepdims=True) * (1.0 / n) + EPS)
    return x * r * w, r


def _rms_bwd(x, r, w, dy, n):
    dyw = dy * w
    dx = r * dyw - x * (r * r * r) * (jnp.sum(dyw * x, axis=-1, keepdims=True) * (1.0 / n))
    dw = jnp.sum(dy * (x * r), axis=0, keepdims=True)
    return dx, dw


def _rope_apply(x, cos, sina, sinb):
    return x * cos + pltpu.roll(x, 16, 1) * sina + pltpu.roll(x, HEAD_PAD - 16, 1) * sinb


def _rope_transpose(g, cos, sina, sinb):
    return g * cos + pltpu.roll(g * sina, HEAD_PAD - 16, 1) + pltpu.roll(g * sinb, 16, 1)


def _row_tile(lp):
    return 640 if lp % 640 == 0 else 128


def _ssm_tile(lp):
    return 320 if lp % 320 == 0 else 128


def _rows(tm, off_width):
    off, width = off_width
    return pl.BlockSpec((tm, width), lambda i: (i, off // width))


def _whole(shape, single=True):
    nd = len(shape)
    if single:
        return pl.BlockSpec(shape, lambda *_: (0,) * nd, pipeline_mode=pl.Buffered(1))
    return pl.BlockSpec(shape, lambda *_: (0,) * nd)


def _out_whole(shape):
    return _whole(shape, single=False)


def _pick_tile(rows, cap):
    best = 8
    for t in range(8, cap + 1, 8):
        if rows % t == 0:
            best = t
    return best


def _in_proj_fwd(h, pre_w, w_in_p):
    lp = h.shape[0]
    tm = _row_tile(lp)

    def body(h_ref, w_ref, win_ref, proj_ref):
        xn, _ = _rms_fwd(h_ref[...], w_ref[...], D_MODEL)
        proj_ref[...] = _dot(xn.astype(BF16), win_ref[...])

    return pl.pallas_call(
        body, name="in_proj_fwd", grid=(lp // tm,),
        in_specs=[_rows(tm, (0, D_MODEL)), _whole((1, D_MODEL)), _whole((D_MODEL, D_PROJ))],
        out_specs=_rows(tm, (0, D_PROJ)),
        out_shape=jax.ShapeDtypeStruct((lp, D_PROJ), F32),
        compiler_params=_cparams(("parallel",)),
    )(h, pre_w, w_in_p)


def _attn_prep_fwd(proj, q_norm_w, kv_norm_w, wq_p, wk_p, wv_p, wv_t, cos, sina, sinb, l_real):
    lp = proj.shape[0]
    tm = _row_tile(lp)

    def body(ql_ref, kvl_ref, kr_ref, qw_ref, kw_ref, wq_ref, wk_ref, wv_ref, wvt_ref, cos_ref, sa_ref, sb_ref,
             q_ref, k_ref, v_ref, vt_ref, qt_ref, kt_ref):
        cos_t, sa_t, sb_t = cos_ref[...], sa_ref[...], sb_ref[...]
        qn, _ = _rms_fwd(ql_ref[...], qw_ref[...], Q_LORA)
        kvn, _ = _rms_fwd(kvl_ref[...], kw_ref[...], KV_LORA)
        kvn_b = kvn.astype(BF16)
        qp = _dot(qn.astype(BF16), wq_ref[...])
        kp = _dot(kvn_b, wk_ref[...])
        v_ref[...] = _dot(kvn_b, wv_ref[...]).astype(BF16)
        ones_row = lax.broadcasted_iota(jnp.int32, (HEADS * VT_ROWS, 1), 0) % VT_ROWS == V_HEAD
        vt_ref[...] = jnp.where(ones_row, 1.0, _dot_nt(wvt_ref[...], kvn_b)).astype(BF16)
        lane = lax.broadcasted_iota(jnp.int32, (tm, HEAD_PAD), 1)
        row = lax.broadcasted_iota(jnp.int32, (tm, HEAD_PAD), 0) + pl.program_id(0) * tm
        q_one = jnp.where(lane == MASK_LANE, 1.0, 0.0)
        k_add = _rope_apply(kr_ref[...], cos_t, sa_t, sb_t) + jnp.where((lane == MASK_LANE) & (row >= l_real), NEG_BIG, 0.0)
        for hd in range(HEADS):
            blk = slice(hd * HEAD_PAD, (hd + 1) * HEAD_PAD)
            q_h = _rope_apply(qp[:, blk], cos_t, sa_t, sb_t) * SCALE2 + q_one
            k_h = kp[:, blk] + k_add
            q_ref[:, blk] = q_h.astype(BF16)
            k_ref[:, blk] = k_h.astype(BF16)
            qt_ref[hd * D_QK:(hd + 1) * D_QK, :] = q_h.T[:D_QK].astype(BF16)
            kt_ref[hd * D_QK:(hd + 1) * D_QK, :] = k_h.T[:D_QK].astype(BF16)

    tab = _rows(tm, (0, HEAD_PAD))
    out = jax.ShapeDtypeStruct((lp, D_EXP), BF16)
    out_t = jax.ShapeDtypeStruct((HEADS * D_QK, lp), BF16)
    cols_t = pl.BlockSpec((HEADS * D_QK, tm), lambda i: (0, i))
    return pl.pallas_call(
        body, name="attn_prep_fwd", grid=(lp // tm,),
        in_specs=[_rows(tm, P_QLAT), _rows(tm, P_KVLAT), _rows(tm, P_KROPE), _whole((1, Q_LORA)), _whole((1, KV_LORA)),
                  _whole((Q_LORA, D_EXP)), _whole((KV_LORA, D_EXP)), _whole((KV_LORA, D_EXP)),
                  _whole((HEADS * VT_ROWS, KV_LORA)), tab, tab, tab],
        out_specs=[_rows(tm, (0, D_EXP))] * 3 + [pl.BlockSpec((HEADS * VT_ROWS, tm), lambda i: (0, i)), cols_t, cols_t],
        out_shape=[out, out, out, jax.ShapeDtypeStruct((HEADS * VT_ROWS, lp), BF16), out_t, out_t],
        compiler_params=_cparams(("parallel",)),
    )(proj, proj, proj, q_norm_w, kv_norm_w, wq_p, wk_p, wv_p, wv_t, cos, sina, sinb)


def _flash_fwd(q, k, vt):
    lp = q.shape[0]
    tq = 1280 if lp % 1280 == 0 else 256
    tk = QBLK
    nk = lp // tk

    def body(q_ref, k_ref, vt_ref, o_ref, lse_ref, acc, m_s, s_a, s_b):
        acc[...] = jnp.zeros_like(acc)
        m_s[...] = jnp.full(m_s.shape, NEG_BIG, F32)
        blocks = [slice(c * QBLK, (c + 1) * QBLK) for c in range(tq // QBLK)]

        def scores(j, buf):
            kt = k_ref[pl.ds(pl.multiple_of(j * tk, tk), tk), :]
            for cols in blocks:
                buf[:, cols] = _dot_nt(kt, q_ref[cols, :])

        def consume(j, buf):
            vt_t = vt_ref[:, pl.ds(pl.multiple_of(j * tk, tk), tk)]
            m_old, acc_old = m_s[...], acc[...]
            s = [buf[:, cols] for cols in blocks]
            m_new = [jnp.maximum(m_old[:, cols], jnp.max(s_c, axis=0, keepdims=True)) for cols, s_c in zip(blocks, s)]
            p = [jnp.exp2(s_c - m_c) for s_c, m_c in zip(s, m_new)]
            pv = [_dot(vt_t, p_c.astype(BF16)) for p_c in p]
            m_new = jnp.concatenate(m_new, axis=1)
            alpha = jnp.exp2(m_old - m_new)
            acc[...] = alpha * acc_old + jnp.concatenate(pv, axis=1)
            m_s[...] = m_new

        n_pairs = (nk - 1) // 2
        scores(0, s_a)

        def pair(t, _):
            j = 2 * t
            scores(j + 1, s_b)
            consume(j, s_a)
            scores(j + 2, s_a)
            consume(j + 1, s_b)
            return 0

        lax.fori_loop(0, n_pairs, pair, 0, unroll=2 if n_pairs % 2 == 0 else 1)
        if nk - 2 * n_pairs == 2:
            scores(nk - 1, s_b)
            consume(nk - 2, s_a)
            consume(nk - 1, s_b)
        else:
            consume(nk - 1, s_a)
        l = acc[V_HEAD:V_HEAD + 1, :]
        o_t = acc[0:V_HEAD, :] / l
        o_ref[...] = jnp.concatenate([o_t, jnp.zeros_like(o_t)], axis=0).T
        lse_ref[...] = m_s[...] + jnp.log2(l)

    return pl.pallas_call(
        body, name="flash_fwd", grid=(HEADS, lp // tq),
        in_specs=[pl.BlockSpec((tq, HEAD_PAD), lambda hd, i: (i, hd)),
                  pl.BlockSpec((lp, HEAD_PAD), lambda hd, i: (0, hd)),
                  pl.BlockSpec((VT_ROWS, lp), lambda hd, i: (hd, 0))],
        out_specs=[pl.BlockSpec((tq, HEAD_PAD), lambda hd, i: (i, hd)),
                   pl.BlockSpec((None, 1, tq), lambda hd, i: (hd, 0, i))],
        out_shape=[jax.ShapeDtypeStruct((lp, D_EXP), F32), jax.ShapeDtypeStruct((HEADS, 1, lp), F32)],
        scratch_shapes=[pltpu.VMEM((VT_ROWS, tq), F32), pltpu.VMEM((1, tq), F32),
                        pltpu.VMEM((tk, tq), F32), pltpu.VMEM((tk, tq), F32)],
        compiler_params=_cparams(("parallel", "parallel")),
    )(q, k, vt)


def _scan_rows(xr_ref, xi_ref, base, n_rows, coef_ref, carry_ref, reverse, tile_fn=None, acc_refs=()):
    n_tiles = n_rows // 8
    shifts = (7, 6, 4) if reverse else (1, 2, 4)
    for cg in range(N_STATE // SCAN_COLS):
        cols = slice(cg * SCAN_COLS, (cg + 1) * SCAN_COLS)
        co = [coef_ref[k, :, cols] for k in range(8)]

        def step(t, carry, cols=cols, co=co):
            cr, ci = carry[0], carry[1]
            tt = (n_tiles - 1 - t) if reverse else t
            start = pl.multiple_of(base + tt * 8, 8)
            tr = xr_ref[pl.ds(start, 8), cols]
            ti = xi_ref[pl.ds(start, 8), cols]
            for lvl in range(3):
                ar, ai = co[2 * lvl], co[2 * lvl + 1]
                sr = pltpu.roll(tr, shifts[lvl], 0)
                si = pltpu.roll(ti, shifts[lvl], 0)
                tr, ti = tr + ar * sr - ai * si, ti + ar * si + ai * sr
            tr, ti = tr + co[6] * cr - co[7] * ci, ti + co[6] * ci + co[7] * cr
            xr_ref[pl.ds(start, 8), cols] = tr
            xi_ref[pl.ds(start, 8), cols] = ti
            accs = carry[2:]
            if tile_fn is not None:
                accs = tuple(a + d for a, d in zip(accs, tile_fn(start, cols, tr, ti)))
            new_c = (tr[0:1], ti[0:1]) if reverse else (tr[7:8], ti[7:8])
            return new_c + accs

        init = (carry_ref[0:1, cols], carry_ref[1:2, cols]) + tuple(a[:, cols] for a in acc_refs)
        out = lax.fori_loop(0, n_tiles, step, init)
        carry_ref[0:1, cols] = out[0]
        carry_ref[1:2, cols] = out[1]
        for a, val in zip(acc_refs, out[2:]):
            a[:, cols] = val


def _ssm_fwd(proj, coef, b_re, b_im, c_re, c_im_neg, reverse):
    lp = proj.shape[0]
    t = _ssm_tile(lp)
    n = lp // t
    order = (lambda i: n - 1 - i) if reverse else (lambda i: i)

    def body(u_ref, coef_ref, bre_ref, bim_ref, cre_ref, cim_ref, y_ref, st_ref, xr, xi, carry):
        @pl.when(pl.program_id(0) == 0)
        def _():
            carry[...] = jnp.zeros_like(carry)

        st_ref[...] = carry[0:2, :]
        ub = u_ref[...].astype(BF16)
        for j in range(SSM_BLOCKS):
            ch, stt = slice(j * BLK_CH, (j + 1) * BLK_CH), slice(j * BLK_ST, (j + 1) * BLK_ST)
            xr[:, stt] = _dot(ub[:, ch], bre_ref[j])
            xi[:, stt] = _dot(ub[:, ch], bim_ref[j])
        _scan_rows(xr, xi, 0, t, coef_ref, carry, reverse)
        for j in range(SSM_BLOCKS):
            ch, stt = slice(j * BLK_CH, (j + 1) * BLK_CH), slice(j * BLK_ST, (j + 1) * BLK_ST)
            y_ref[:, ch] = _dot(xr[:, stt].astype(BF16), cre_ref[j]) + _dot(xi[:, stt].astype(BF16), cim_ref[j])

    wb, wc = _whole((SSM_BLOCKS, BLK_CH, BLK_ST)), _whole((SSM_BLOCKS, BLK_ST, BLK_CH))
    return pl.pallas_call(
        body, name="ssm_fwd_rev" if reverse else "ssm_fwd", grid=(n,),
        in_specs=[pl.BlockSpec((t, D_SSM), lambda i: (order(i), P_U[0] // D_SSM)), _whole((8, 8, N_STATE)), wb, wb, wc, wc],
        out_specs=[pl.BlockSpec((t, D_SSM), lambda i: (order(i), 0)),
                   pl.BlockSpec((None, 2, N_STATE), lambda i: (order(i), 0, 0))],
        out_shape=[jax.ShapeDtypeStruct((lp, D_SSM), F32), jax.ShapeDtypeStruct((n, 2, N_STATE), F32)],
        scratch_shapes=[pltpu.VMEM((t, N_STATE), F32), pltpu.VMEM((t, N_STATE), F32), pltpu.VMEM((8, N_STATE), F32)],
        compiler_params=_cparams(("arbitrary",)),
    )(proj, coef, b_re, b_im, c_re, c_im_neg)


GELU_C0 = math.sqrt(2.0 / math.pi)
GELU_C1 = 0.044715


def _mid(h, tgt, o_exp, proj, y0, y1, ssm_d, w_glu, w_glu_t, b_glu, ssm_norm_w, attn_norm_w_e, w_out_a, w_out_s,
         w_out_a_t, w_out_s_t, post_w, l_real):
    lp = h.shape[0]
    tm = 128

    def body(h_ref, tga_ref, tgb_ref, o_ref, ga_ref, u_ref, sg_ref, y0_ref, y1_ref, d_ref, wg_ref, wgt_ref, bg_ref, ws_ref,
             wa_ref, woa_ref, wos_ref, woat_ref, wost_ref, pw_ref,
             do_ref, dot_ref, delta_ref, dga_ref, dyp_ref, dsg_ref, dres_ref, dwoa_ref, dwos_ref, dwg_ref, vec_ref):
        @pl.when(pl.program_id(0) == 0)
        def _():
            dwoa_ref[...] = jnp.zeros_like(dwoa_ref)
            dwos_ref[...] = jnp.zeros_like(dwos_ref)
            dwg_ref[...] = jnp.zeros_like(dwg_ref)
            vec_ref[...] = jnp.zeros_like(vec_ref)

        u = u_ref[...]
        ypre = y0_ref[...] + y1_ref[...] + d_ref[...] * u
        th = jnp.tanh(GELU_C0 * (ypre + GELU_C1 * ypre * ypre * ypre))
        gel = 0.5 * ypre * (1.0 + th)
        gel_b = gel.astype(BF16)
        glu = _dot(gel_b, wg_ref[...]) + bg_ref[...]
        g1, g2 = glu[:, :D_SSM], glu[:, D_SSM:]
        sig2 = _sigmoid(g2)
        z = g1 * sig2
        sg = sg_ref[...]
        sgs = _sigmoid(sg)
        sil_s = sg * sgs
        s = z * sil_s
        ys, r_s = _rms_fwd(s, ws_ref[...], D_SSM)

        o = o_ref[...]
        ga = ga_ref[...]
        gas = _sigmoid(ga)
        sil_a = ga * gas
        a = o * sil_a
        ya, r_a = _rms_fwd(a, wa_ref[...], D_ATTN)

        ya_b, ys_b = ya.astype(BF16), ys.astype(BF16)
        y = _dot(ya_b, woa_ref[...]) + _dot(ys_b, wos_ref[...])
        yn, r_y = _rms_fwd(y, pw_ref[...], D_MODEL)
        row = lax.broadcasted_iota(jnp.int32, (tm, 1), 0) + pl.program_id(0) * tm
        valid = (row >= N_META) & (row < l_real)
        tgt = jnp.concatenate([tga_ref[tm - N_META:, :], tgb_ref[:tm - N_META, :]], axis=0)
        err = jnp.where(valid, h_ref[...] + yn - tgt, 0.0)
        loss = 0.5 * jnp.sum(jnp.sum(err * err, axis=-1, keepdims=True), axis=0, keepdims=True) * (1.0 / D_MODEL)
        dout = err * (1.0 / D_MODEL)
        dres_ref[...] = dout

        dy, d_pw = _rms_bwd(y, r_y, pw_ref[...], dout, D_MODEL)
        dy_b = dy.astype(BF16)
        dya = _dot(dy_b, woat_ref[...])
        dys = _dot(dy_b, wost_ref[...])
        dwoa_ref[...] += _dot_tn(ya_b, dy_b)
        dwos_ref[...] += _dot_tn(ys_b, dy_b)

        da, d_wa = _rms_bwd(a, r_a, wa_ref[...], dya, D_ATTN)
        d_o = da * sil_a
        dga_ref[...] = da * o * (gas * (1.0 + ga * (1.0 - gas)))
        do_ref[...] = d_o.astype(BF16)
        for hd in range(HEADS):
            dot_ref[hd * V_HEAD:(hd + 1) * V_HEAD, :] = d_o[:, hd * HEAD_PAD:(hd + 1) * HEAD_PAD].T[:V_HEAD].astype(BF16)
        prod = d_o * o
        lane8 = lax.broadcasted_iota(jnp.int32, (tm, HEADS), 1)
        delta = jnp.zeros((tm, HEADS), F32)
        for hd in range(HEADS):
            delta = jnp.where(lane8 == hd, jnp.sum(prod[:, hd * HEAD_PAD:(hd + 1) * HEAD_PAD], axis=-1, keepdims=True), delta)
        delta_ref[...] = delta

        ds, d_ws = _rms_bwd(s, r_s, ws_ref[...], dys, D_SSM)
        dz = ds * sil_s
        dsg_ref[...] = ds * z * (sgs * (1.0 + sg * (1.0 - sgs)))
        dglu = jnp.concatenate([dz * sig2, dz * g1 * sig2 * (1.0 - sig2)], axis=-1)
        dglu_b = dglu.astype(BF16)
        dwg_ref[...] += _dot_tn(gel_b, dglu_b)
        dgel = _dot(dglu_b, wgt_ref[...])
        dgelu = 0.5 * (1.0 + th) + 0.5 * ypre * (1.0 - th * th) * (GELU_C0 * (1.0 + 3.0 * GELU_C1 * ypre * ypre))
        dyp = dgel * dgelu
        dyp_ref[...] = dyp

        vec_ref[0:1, :] += d_pw
        vec_ref[1:2, :] += d_wa
        vec_ref[2:3, 0:D_SSM] += d_ws
        vec_ref[3:4, 0:D_SSM] += jnp.sum(dyp * u, axis=0, keepdims=True)
        vec_ref[4:5, :] += jnp.sum(dglu, axis=0, keepdims=True)
        vec_ref[5:6, :] += jnp.broadcast_to(loss, (1, D_MODEL))

    full = lambda off: _rows(tm, (off, D_MODEL))
    half = lambda off: _rows(tm, (off, D_SSM))
    last = tgt.shape[0] // tm - 1
    tg_a = pl.BlockSpec((tm, D_MODEL), lambda i: (jnp.clip(i - 1, 0, last), 0))
    tg_b = pl.BlockSpec((tm, D_MODEL), lambda i: (jnp.minimum(i, last), 0))
    return pl.pallas_call(
        body, name="mid", grid=(lp // tm,),
        in_specs=[full(0), tg_a, tg_b, full(0), _rows(tm, P_GATE_A), _rows(tm, P_U), _rows(tm, P_GATE_S), half(0), half(0),
                  _whole((1, D_SSM)), _whole((D_SSM, 2 * D_SSM)), _whole((2 * D_SSM, D_SSM)), _whole((1, 2 * D_SSM)),
                  _whole((1, D_SSM)), _whole((1, D_EXP)), _whole((D_EXP, D_MODEL)), _whole((D_SSM, D_MODEL)),
                  _whole((D_MODEL, D_EXP)), _whole((D_MODEL, D_SSM)), _whole((1, D_MODEL))],
        out_specs=[full(0), pl.BlockSpec((D_ATTN, tm), lambda i: (0, i)), _rows(tm, (0, HEADS)), full(0), half(0), half(0), full(0),
                   _out_whole((D_EXP, D_MODEL)), _out_whole((D_SSM, D_MODEL)), _out_whole((D_SSM, 2 * D_SSM)),
                   _out_whole((8, D_MODEL))],
        out_shape=[jax.ShapeDtypeStruct((lp, D_EXP), BF16), jax.ShapeDtypeStruct((D_ATTN, lp), BF16),
                   jax.ShapeDtypeStruct((lp, HEADS), F32),
                   jax.ShapeDtypeStruct((lp, D_EXP), F32), jax.ShapeDtypeStruct((lp, D_SSM), F32),
                   jax.ShapeDtypeStruct((lp, D_SSM), F32), jax.ShapeDtypeStruct((lp, D_MODEL), F32),
                   jax.ShapeDtypeStruct((D_EXP, D_MODEL), F32), jax.ShapeDtypeStruct((D_SSM, D_MODEL), F32),
                   jax.ShapeDtypeStruct((D_SSM, 2 * D_SSM), F32), jax.ShapeDtypeStruct((8, D_MODEL), F32)],
        compiler_params=_cparams(("arbitrary",)),
    )(h, tgt, tgt, o_exp, proj, proj, proj, y0, y1, ssm_d, w_glu, w_glu_t, b_glu, ssm_norm_w, attn_norm_w_e, w_out_a, w_out_s,
      w_out_a_t, w_out_s_t, post_w)


def _ssm_bwd(proj, dyp, states, coef, coef_adj, b_re, b_im, b_re_t, b_im_t, c_re_t, c_im_neg_t, reverse):
    lp = proj.shape[0]
    t = _ssm_tile(lp)
    n = lp // t
    order = (lambda i: i) if reverse else (lambda i: n - 1 - i)
    edge = (t + 8) if reverse else 7

    def body(u_ref, dy_ref, st_ref, coef_ref, coefa_ref, bre_ref, bim_ref, bret_ref, bimt_ref, cret_ref, cimt_ref,
             du_ref, dbre_ref, dbim_ref, dcre_ref, dcim_ref, da_ref, xr, xi, gr, gi, carry_x, carry_g):
        @pl.when(pl.program_id(0) == 0)
        def _():
            carry_g[...] = jnp.zeros_like(carry_g)
            carry_x[...] = jnp.zeros_like(carry_x)
            dbre_ref[...] = jnp.zeros_like(dbre_ref)
            dbim_ref[...] = jnp.zeros_like(dbim_ref)
            dcre_ref[...] = jnp.zeros_like(dcre_ref)
            dcim_ref[...] = jnp.zeros_like(dcim_ref)
            da_ref[...] = jnp.zeros_like(da_ref)
            for halo in (slice(0, 8), slice(t + 8, t + 16)):
                xr[halo, :] = jnp.zeros((8, N_STATE), F32)
                xi[halo, :] = jnp.zeros((8, N_STATE), F32)

        ub = u_ref[...].astype(BF16)
        dyb = dy_ref[...].astype(BF16)
        carry_x[0:2, :] = st_ref[...]
        xr[edge:edge + 1, :] = st_ref[0:1, :]
        xi[edge:edge + 1, :] = st_ref[1:2, :]
        blocks = [(slice(j * BLK_CH, (j + 1) * BLK_CH), slice(j * BLK_ST, (j + 1) * BLK_ST)) for j in range(SSM_BLOCKS)]
        for j, (ch, stt) in enumerate(blocks):
            xr[8:t + 8, stt] = _dot(ub[:, ch], bre_ref[j])
            xi[8:t + 8, stt] = _dot(ub[:, ch], bim_ref[j])
            gr[:, stt] = _dot(dyb[:, ch], cret_ref[j])
            gi[:, stt] = _dot(dyb[:, ch], cimt_ref[j])
        _scan_rows(xr, xi, 8, t, coef_ref, carry_x, reverse)

        row8 = lax.broadcasted_iota(jnp.int32, (8, SCAN_COLS), 0)

        def tile_fn(start, cols, g_re, g_im):
            xs = pl.multiple_of(start + 8, 8)
            if reverse:
                nb = pl.multiple_of(start + 16, 8)
                xn_r = jnp.where(row8 == 7, xr[pl.ds(nb, 8), cols][0:1], pltpu.roll(xr[pl.ds(xs, 8), cols], 7, 0))
                xn_i = jnp.where(row8 == 7, xi[pl.ds(nb, 8), cols][0:1], pltpu.roll(xi[pl.ds(xs, 8), cols], 7, 0))
            else:
                nb = pl.multiple_of(start, 8)
                xn_r = jnp.where(row8 == 0, xr[pl.ds(nb, 8), cols][7:8], pltpu.roll(xr[pl.ds(xs, 8), cols], 1, 0))
                xn_i = jnp.where(row8 == 0, xi[pl.ds(nb, 8), cols][7:8], pltpu.roll(xi[pl.ds(xs, 8), cols], 1, 0))
            return g_re * xn_r + g_im * xn_i, g_im * xn_r - g_re * xn_i

        _scan_rows(gr, gi, 0, t, coefa_ref, carry_g, not reverse, tile_fn=tile_fn, acc_refs=(da_ref.at[0], da_ref.at[1]))

        for j, (ch, stt) in enumerate(blocks):
            g_re_b, g_im_b = gr[:, stt].astype(BF16), gi[:, stt].astype(BF16)
            du_ref[:, ch] = _dot(g_re_b, bret_ref[j]) + _dot(g_im_b, bimt_ref[j])
            dbre_ref[j] += _dot_tn(ub[:, ch], g_re_b)
            dbim_ref[j] += _dot_tn(ub[:, ch], g_im_b)
            dcre_ref[j] += _dot_tn(dyb[:, ch], xr[8:t + 8, stt].astype(BF16))
            dcim_ref[j] -= _dot_tn(dyb[:, ch], xi[8:t + 8, stt].astype(BF16))

    dense = jax.ShapeDtypeStruct((SSM_BLOCKS, BLK_CH, BLK_ST), F32)
    wb, wc = _whole((SSM_BLOCKS, BLK_CH, BLK_ST)), _whole((SSM_BLOCKS, BLK_ST, BLK_CH))
    acc = _out_whole((SSM_BLOCKS, BLK_CH, BLK_ST))
    return pl.pallas_call(
        body, name="ssm_bwd_rev" if reverse else "ssm_bwd", grid=(n,),
        in_specs=[pl.BlockSpec((t, D_SSM), lambda i: (order(i), P_U[0] // D_SSM)),
                  pl.BlockSpec((t, D_SSM), lambda i: (order(i), 0)),
                  pl.BlockSpec((None, 2, N_STATE), lambda i: (order(i), 0, 0)),
                  _whole((8, 8, N_STATE)), _whole((8, 8, N_STATE)), wb, wb, wc, wc, wb, wb],
        out_specs=[pl.BlockSpec((t, D_SSM), lambda i: (order(i), 0)), acc, acc, acc, acc, _out_whole((2, 8, N_STATE))],
        out_shape=[jax.ShapeDtypeStruct((lp, D_SSM), F32), dense, dense, dense, dense,
                   jax.ShapeDtypeStruct((2, 8, N_STATE), F32)],
        scratch_shapes=[pltpu.VMEM((t + 16, N_STATE), F32), pltpu.VMEM((t + 16, N_STATE), F32),
                        pltpu.VMEM((t, N_STATE), F32), pltpu.VMEM((t, N_STATE), F32),
                        pltpu.VMEM((8, N_STATE), F32), pltpu.VMEM((8, N_STATE), F32)],
        compiler_params=_cparams(("arbitrary",)),
    )(proj, dyp, states, coef, coef_adj, b_re, b_im, b_re_t, b_im_t, c_re_t, c_im_neg_t)


def _flash_bwd(q, k, v, d_o, q_t, k_t, do_t, lse_row, delta_row):
    lp = q.shape[0]
    tq = 1280 if lp % 1280 == 0 else 256
    tk = QBLK
    nk = lp // tk
    d_qk = QK_NOPE + QK_ROPE

    def body(q_ref, do_ref, qt_ref, dot_ref, lse_ref, delta_ref, k_ref, v_ref, kt_ref, dq_ref, dk_ref, dv_ref, dq_acc,
             s_a, dp_a, s_b, dp_b):
        @pl.when(pl.program_id(1) == 0)
        def _():
            dk_ref[...] = jnp.zeros_like(dk_ref)
            dv_ref[...] = jnp.zeros_like(dv_ref)

        dq_acc[...] = jnp.zeros_like(dq_acc)
        lse, delta = lse_ref[...], delta_ref[...]
        q_cols, do_cols = qt_ref[...], dot_ref[...]
        blocks = [slice(c * QBLK, (c + 1) * QBLK) for c in range(tq // QBLK)]

        def scores(j, s_buf, dp_buf):
            ks = pl.multiple_of(j * tk, tk)
            k_rows, v_rows = k_ref[pl.ds(ks, tk), :], v_ref[pl.ds(ks, tk), :]
            for cols in blocks:
                s_buf[:, cols] = _dot_nt(k_rows, q_ref[cols, :])
                dp_buf[:, cols] = _dot_nt(v_rows, do_ref[cols, :])

        def consume(j, s_buf, dp_buf):
            ks = pl.multiple_of(j * tk, tk)
            dq_old = dq_acc[...]
            pt = [jnp.exp2(s_buf[:, cols] - lse[:, cols]) for cols in blocks]
            dst = [p_c * (dp_buf[:, cols] - delta[:, cols]) for p_c, cols in zip(pt, blocks)]
            pt_b = jnp.concatenate([p_c.astype(BF16) for p_c in pt], axis=1)
            dst_b = jnp.concatenate([d_c.astype(BF16) for d_c in dst], axis=1)
            dv_ref[:, pl.ds(ks, tk)] += _dot_nt(do_cols, pt_b)
            dk_ref[:, pl.ds(ks, tk)] += _dot_nt(q_cols, dst_b) * (1.0 / LOG2E)
            dq_acc[...] = dq_old + _dot(kt_ref[:, pl.ds(ks, tk)], dst_b)

        n_pairs = (nk - 1) // 2
        scores(0, s_a, dp_a)

        def pair(t, _):
            j = 2 * t
            scores(j + 1, s_b, dp_b)
            consume(j, s_a, dp_a)
            scores(j + 2, s_a, dp_a)
            consume(j + 1, s_b, dp_b)
            return 0

        lax.fori_loop(0, n_pairs, pair, 0, unroll=2 if n_pairs % 2 == 0 else 1)
        if nk - 2 * n_pairs == 2:
            scores(nk - 1, s_b, dp_b)
            consume(nk - 2, s_a, dp_a)
            consume(nk - 1, s_b, dp_b)
        else:
            consume(nk - 1, s_a, dp_a)
        dq_ref[...] = jnp.concatenate([dq_acc[...], jnp.zeros((HEAD_PAD - d_qk, tq), F32)], axis=0).T

    tile = pl.BlockSpec((tq, HEAD_PAD), lambda hd, i: (i, hd))
    head = pl.BlockSpec((lp, HEAD_PAD), lambda hd, i: (0, hd))
    rowv = pl.BlockSpec((None, 1, tq), lambda hd, i: (hd, 0, i))
    return pl.pallas_call(
        body, name="flash_bwd", grid=(HEADS, lp // tq),
        in_specs=[tile, tile, pl.BlockSpec((d_qk, tq), lambda hd, i: (hd, i)), pl.BlockSpec((V_HEAD, tq), lambda hd, i: (hd, i)),
                  rowv, rowv, head, head, pl.BlockSpec((d_qk, lp), lambda hd, i: (hd, 0))],
        out_specs=[tile, pl.BlockSpec((d_qk, lp), lambda hd, i: (hd, 0)), pl.BlockSpec((V_HEAD, lp), lambda hd, i: (hd, 0))],
        out_shape=[jax.ShapeDtypeStruct((lp, D_EXP), F32), jax.ShapeDtypeStruct((HEADS * d_qk, lp), F32),
                   jax.ShapeDtypeStruct((HEADS * V_HEAD, lp), F32)],
        scratch_shapes=[pltpu.VMEM((d_qk, tq), F32)] + [pltpu.VMEM((tk, tq), F32)] * 4,
        compiler_params=_cparams(("parallel", "arbitrary")),
    )(q, d_o, q_t, do_t, lse_row, delta_row, k, v, k_t)


def _attn_prep_bwd(dq, dk_t, dv_t, proj, q_norm_w, kv_norm_w, wq_pt, wk_pt, wv_pt, cos, sina, sinb):
    lp = proj.shape[0]
    tm = _row_tile(lp)

    def body(dq_ref, dk_ref, dv_ref, ql_ref, kvl_ref, qw_ref, kw_ref, wqt_ref, wkt_ref, wvt_ref, cos_ref, sa_ref, sb_ref,
             dql_ref, dkvl_ref, dkr_ref, dwq_ref, dwk_ref, dwv_ref, vec_ref):
        @pl.when(pl.program_id(0) == 0)
        def _():
            dwq_ref[...] = jnp.zeros_like(dwq_ref)
            dwk_ref[...] = jnp.zeros_like(dwk_ref)
            dwv_ref[...] = jnp.zeros_like(dwv_ref)
            vec_ref[...] = jnp.zeros_like(vec_ref)

        cos_t, sa_t, sb_t = cos_ref[...], sa_ref[...], sb_ref[...]

        def head_rows(t_ref, per):
            pad = jnp.zeros((HEAD_PAD - per, tm), F32)
            return jnp.concatenate(
                [jnp.concatenate([t_ref[hd * per:(hd + 1) * per, :], pad], axis=0).T for hd in range(HEADS)], axis=-1)

        dkp = head_rows(dk_ref, D_QK)
        dqp = jnp.concatenate(
            [_rope_transpose(dq_ref[:, hd * HEAD_PAD:(hd + 1) * HEAD_PAD] * SCALE, cos_t, sa_t, sb_t) for hd in range(HEADS)],
            axis=-1)
        dkr = dkp[:, 0:HEAD_PAD]
        for hd in range(1, HEADS):
            dkr = dkr + dkp[:, hd * HEAD_PAD:(hd + 1) * HEAD_PAD]
        dkr_ref[...] = _rope_transpose(dkr, cos_t, sa_t, sb_t)

        qn, r_q = _rms_fwd(ql_ref[...], qw_ref[...], Q_LORA)
        kvn, r_kv = _rms_fwd(kvl_ref[...], kw_ref[...], KV_LORA)
        dqp_b, dkp_b, dv_b = dqp.astype(BF16), dkp.astype(BF16), head_rows(dv_ref, V_HEAD).astype(BF16)
        dqn = _dot(dqp_b, wqt_ref[...])
        dkvn = _dot(dkp_b, wkt_ref[...]) + _dot(dv_b, wvt_ref[...])
        dwq_ref[...] += _dot_tn(qn.astype(BF16), dqp_b)
        dwk_ref[...] += _dot_tn(kvn.astype(BF16), dkp_b)
        dwv_ref[...] += _dot_tn(kvn.astype(BF16), dv_b)
        dql, d_qw = _rms_bwd(ql_ref[...], r_q, qw_ref[...], dqn, Q_LORA)
        dkvl, d_kw = _rms_bwd(kvl_ref[...], r_kv, kw_ref[...], dkvn, KV_LORA)
        dql_ref[...] = dql
        dkvl_ref[...] = dkvl
        vec_ref[0:1, :] += d_qw
        vec_ref[1:2, 0:KV_LORA] += d_kw

    tab = _rows(tm, (0, HEAD_PAD))
    full = _rows(tm, (0, D_EXP))
    return pl.pallas_call(
        body, name="attn_prep_bwd", grid=(lp // tm,),
        in_specs=[full, pl.BlockSpec((HEADS * D_QK, tm), lambda i: (0, i)), pl.BlockSpec((D_ATTN, tm), lambda i: (0, i)),
                  _rows(tm, P_QLAT), _rows(tm, P_KVLAT), _whole((1, Q_LORA)), _whole((1, KV_LORA)),
                  _whole((D_EXP, Q_LORA)), _whole((D_EXP, KV_LORA)), _whole((D_EXP, KV_LORA)), tab, tab, tab],
        out_specs=[_rows(tm, (0, Q_LORA)), _rows(tm, (0, KV_LORA)), _rows(tm, (0, HEAD_PAD)),
                   _out_whole((Q_LORA, D_EXP)), _out_whole((KV_LORA, D_EXP)), _out_whole((KV_LORA, D_EXP)),
                   _out_whole((8, Q_LORA))],
        out_shape=[jax.ShapeDtypeStruct((lp, Q_LORA), F32), jax.ShapeDtypeStruct((lp, KV_LORA), F32),
                   jax.ShapeDtypeStruct((lp, HEAD_PAD), F32), jax.ShapeDtypeStruct((Q_LORA, D_EXP), F32),
                   jax.ShapeDtypeStruct((KV_LORA, D_EXP), F32), jax.ShapeDtypeStruct((KV_LORA, D_EXP), F32),
                   jax.ShapeDtypeStruct((8, Q_LORA), F32)],
        compiler_params=_cparams(("arbitrary",)),
    )(dq, dk_t, dv_t, proj, proj, q_norm_w, kv_norm_w, wq_pt, wk_pt, wv_pt, cos, sina, sinb)


def _in_proj_bwd(h, pre_w, dres, dga, du0, du1, dyp, ssm_d, dsg, dql, dkvl, dkr, w_in_pt):
    lp = h.shape[0]
    tm = 128
    pieces = (P_GATE_A, P_U, P_GATE_S, P_QLAT, P_KVLAT, P_KROPE)

    def body(h_ref, w_ref, dres_ref, dga_ref, du0_ref, du1_ref, dyp_ref, d_ref, dsg_ref, dql_ref, dkvl_ref, dkr_ref, wt_ref,
             dh_ref, dw_ref, vec_ref):
        @pl.when(pl.program_id(0) == 0)
        def _():
            dw_ref[...] = jnp.zeros_like(dw_ref)
            vec_ref[...] = jnp.zeros_like(vec_ref)

        hv = h_ref[...]
        xn, r = _rms_fwd(hv, w_ref[...], D_MODEL)
        xn_b = xn.astype(BF16)
        du = du0_ref[...] + du1_ref[...] + dyp_ref[...] * d_ref[...]
        grads = (dga_ref[...], du, dsg_ref[...], dql_ref[...], dkvl_ref[...], dkr_ref[...])
        dxn = jnp.zeros((tm, D_MODEL), F32)
        for (off, width), g in zip(pieces, grads):
            g_b = g.astype(BF16)
            dxn = dxn + _dot(g_b, wt_ref[off:off + width, :])
            dw_ref[:, off:off + width] += _dot_tn(xn_b, g_b)
        dx, d_w = _rms_bwd(hv, r, w_ref[...], dxn, D_MODEL)
        dh_ref[...] = dres_ref[...] + dx
        vec_ref[0:1, :] += d_w

    full = _rows(tm, (0, D_MODEL))
    half = _rows(tm, (0, D_SSM))
    return pl.pallas_call(
        body, name="in_proj_bwd", grid=(lp // tm,),
        in_specs=[full, _whole((1, D_MODEL)), full, full, half, half, half, _whole((1, D_SSM)), half,
                  _rows(tm, (0, Q_LORA)), _rows(tm, (0, KV_LORA)), _rows(tm, (0, HEAD_PAD)), _whole((D_PROJ, D_MODEL))],
        out_specs=[full, _out_whole((D_MODEL, D_PROJ)), _out_whole((8, D_MODEL))],
        out_shape=[jax.ShapeDtypeStruct((lp, D_MODEL), F32), jax.ShapeDtypeStruct((D_MODEL, D_PROJ), F32),
                   jax.ShapeDtypeStruct((8, D_MODEL), F32)],
        compiler_params=_cparams(("arbitrary",)),
    )(h, pre_w, dres, dga, du0, du1, dyp, ssm_d, dsg, dql, dkvl, dkr, w_in_pt)


def _other_chips(x, y):
    return [(1 - x, y), (x, 1 - y), (1 - x, 1 - y)]


def _gather_weights(w_bf16, meta):
    any_spec = pl.BlockSpec(memory_space=pl.ANY)

    def body(w_ref, m_ref, wout_ref, mout_ref, send_sems, recv_sems, local_sems):
        x, y, c = lax.axis_index("x"), lax.axis_index("y"), lax.axis_index("c")
        me = 2 * x + y
        own = [pltpu.make_async_copy(w_ref, wout_ref.at[me], local_sems.at[0]),
               pltpu.make_async_copy(m_ref, mout_ref.at[me], local_sems.at[1])]
        for cp in own:
            cp.start()
        sends = []
        for j, (tx, ty) in enumerate(_other_chips(x, y)):
            for n, (src, dst) in enumerate(((w_ref, wout_ref), (m_ref, mout_ref))):
                sends.append(pltpu.make_async_remote_copy(
                    src_ref=src, dst_ref=dst.at[me], send_sem=send_sems.at[2 * j + n], recv_sem=recv_sems.at[2 * j + n],
                    device_id=(tx, ty, c), device_id_type=MESH))
        for cp in sends:
            cp.start()
        for j, (tx, ty) in enumerate(_other_chips(x, y)):
            for n, (src, dst) in enumerate(((w_ref, wout_ref), (m_ref, mout_ref))):
                pltpu.make_async_remote_copy(
                    src_ref=src, dst_ref=dst.at[2 * tx + ty], send_sem=send_sems.at[2 * j + n],
                    recv_sem=recv_sems.at[2 * j + n], device_id=(tx, ty, c), device_id_type=MESH).wait_recv()
        for cp in sends:
            cp.wait_send()
        for cp in own:
            cp.wait()

    return pl.pallas_call(
        body, name="gather_weights",
        in_specs=[any_spec, any_spec], out_specs=[any_spec, any_spec],
        out_shape=[jax.ShapeDtypeStruct((4,) + w_bf16.shape, w_bf16.dtype), jax.ShapeDtypeStruct((4,) + meta.shape, meta.dtype)],
        scratch_shapes=[pltpu.SemaphoreType.DMA((6,)), pltpu.SemaphoreType.DMA((6,)), pltpu.SemaphoreType.DMA((2,))],
    )(w_bf16, meta)


def _swap_sibling(g):
    any_spec = pl.BlockSpec(memory_space=pl.ANY)

    def body(g_ref, out_ref, send_sem, recv_sem):
        x, y, c = lax.axis_index("x"), lax.axis_index("y"), lax.axis_index("c")
        cp = pltpu.make_async_remote_copy(src_ref=g_ref, dst_ref=out_ref, send_sem=send_sem, recv_sem=recv_sem,
                                          device_id=(x, y, 1 - c), device_id_type=MESH)
        cp.start()
        cp.wait()

    return pl.pallas_call(
        body, name="swap_sibling", in_specs=[any_spec], out_specs=any_spec,
        out_shape=jax.ShapeDtypeStruct(g.shape, g.dtype),
        scratch_shapes=[pltpu.SemaphoreType.DMA(()), pltpu.SemaphoreType.DMA(())],
    )(g)


def _pair_sum(a, b):
    rows = a.shape[0]
    tm = _pick_tile(rows, 1024)

    def body(a_ref, b_ref, o_ref):
        o_ref[...] = a_ref[...] + b_ref[...]

    spec = pl.BlockSpec((tm, 1024), lambda i: (i, 0))
    return pl.pallas_call(body, name="pair_sum", grid=(rows // tm,), in_specs=[spec, spec], out_specs=spec,
                          out_shape=jax.ShapeDtypeStruct(a.shape, F32), compiler_params=_cparams(("parallel",)))(a, b)


def _scatter_chips(s, rs, rsm):
    any_spec = pl.BlockSpec(memory_space=pl.ANY)

    def body(s_ref, out_ref, send_sems, recv_sems, local_sems):
        x, y, c = lax.axis_index("x"), lax.axis_index("y"), lax.axis_index("c")
        me = 2 * x + y
        small = s_ref.at[pl.ds(4 * rs, rsm)]

        def pieces(target):
            return ((s_ref.at[pl.ds(pl.multiple_of(target * rs, 8), rs)], pl.ds(0, rs)), (small, pl.ds(rs, rsm)))

        own = [pltpu.make_async_copy(src, out_ref.at[me, rows], local_sems.at[n]) for n, (src, rows) in enumerate(pieces(me))]
        for cp in own:
            cp.start()
        sends = []
        for j, (tx, ty) in enumerate(_other_chips(x, y)):
            for n, (src, rows) in enumerate(pieces(2 * tx + ty)):
                sends.append(pltpu.make_async_remote_copy(
                    src_ref=src, dst_ref=out_ref.at[me, rows], send_sem=send_sems.at[2 * j + n],
                    recv_sem=recv_sems.at[2 * j + n], device_id=(tx, ty, c), device_id_type=MESH))
        for cp in sends:
            cp.start()
        for j, (tx, ty) in enumerate(_other_chips(x, y)):
            for n, (src, rows) in enumerate(pieces(me)):
                pltpu.make_async_remote_copy(
                    src_ref=src, dst_ref=out_ref.at[2 * tx + ty, rows], send_sem=send_sems.at[2 * j + n],
                    recv_sem=recv_sems.at[2 * j + n], device_id=(tx, ty, c), device_id_type=MESH).wait_recv()
        for cp in sends:
            cp.wait_send()
        for cp in own:
            cp.wait()

    return pl.pallas_call(
        body, name="scatter_chips", in_specs=[any_spec], out_specs=any_spec,
        out_shape=jax.ShapeDtypeStruct((4, rs + rsm, 1024), F32),
        scratch_shapes=[pltpu.SemaphoreType.DMA((6,)), pltpu.SemaphoreType.DMA((6,)), pltpu.SemaphoreType.DMA((2,))],
    )(s)


def _adamw(parts, w, m, v):
    rows = w.shape[0]
    tm = _pick_tile(rows, 256)
    c1 = 1.0 / (1.0 - ADAM_B1 ** ADAM_STEP)
    c2 = 1.0 / (1.0 - ADAM_B2 ** ADAM_STEP)

    def body(p_ref, w_ref, m_ref, v_ref, g_ref, d_ref, nm_ref, nv_ref):
        g = ((p_ref[0] + p_ref[1]) + p_ref[2]) + p_ref[3]
        nm = ADAM_B1 * m_ref[...] + (1.0 - ADAM_B1) * g
        nv = ADAM_B2 * v_ref[...] + (1.0 - ADAM_B2) * (g * g)
        g_ref[...] = g
        nm_ref[...] = nm
        nv_ref[...] = nv
        d_ref[...] = -ADAM_LR * ((nm * c1) / (jnp.sqrt(nv * c2) + ADAM_EPS) + ADAM_WD * w_ref[...])

    spec = pl.BlockSpec((tm, 1024), lambda i: (i, 0))
    out = jax.ShapeDtypeStruct(w.shape, F32)
    return pl.pallas_call(
        body, name="adamw", grid=(rows // tm,),
        in_specs=[pl.BlockSpec((4, tm, 1024), lambda i: (0, i, 0)), spec, spec, spec],
        out_specs=[spec] * 4, out_shape=[out] * 4, compiler_params=_cparams(("parallel",)),
    )(parts, w, m, v)


def _expand_heads(a, axis, per_head):
    a = jnp.moveaxis(a, axis, -1)
    lead = a.shape[:-1]
    a = a.reshape(lead + (HEADS, per_head))
    a = jnp.pad(a, [(0, 0)] * len(lead) + [(0, 0), (0, HEAD_PAD - per_head)])
    return jnp.moveaxis(a.reshape(lead + (D_EXP,)), -1, axis)


def _compact_heads(a, axis, start, size):
    a = jnp.moveaxis(a, axis, -1)
    lead = a.shape[:-1]
    a = a.reshape(lead + (HEADS, HEAD_PAD))[..., start:start + size]
    return jnp.moveaxis(a.reshape(lead + (HEADS * size,)), -1, axis)


def _block_diag(w):
    g, a, b = w.shape
    per = g // SSM_BLOCKS
    eye = jnp.eye(per, dtype=w.dtype)
    return jnp.einsum("jgab,gk->jgakb", w.reshape(SSM_BLOCKS, per, a, b), eye).reshape(SSM_BLOCKS, per * a, per * b)


def _block_diag_extract(dense, a, b):
    per = N_GROUPS // SSM_BLOCKS
    d5 = dense.reshape(SSM_BLOCKS, per, a, per, b)
    return jnp.einsum("jgakb,gk->jgab", d5, jnp.eye(per, dtype=dense.dtype)).reshape(N_GROUPS, a, b)


def _discretise(a_re, a_im, log_dt, b_re, b_im):
    dt = jnp.exp(log_dt)[:, None]
    mag = jnp.exp(a_re * dt)
    abar_re = mag * jnp.cos(a_im * dt)
    abar_im = mag * jnp.sin(a_im * dt)
    num_re = abar_re - 1.0
    num_im = abar_im
    den = a_re * a_re + a_im * a_im
    coef_re = (num_re * a_re + num_im * a_im) / den
    coef_im = (num_im * a_re - num_re * a_im) / den
    bbar_re = coef_re[..., None] * b_re - coef_im[..., None] * b_im
    bbar_im = coef_re[..., None] * b_im + coef_im[..., None] * b_re
    return abar_re, abar_im, bbar_re, bbar_im


def _scan_coef(ar, ai, reverse):
    ar, ai = ar.reshape(1, N_STATE), ai.reshape(1, N_STATE)
    pows = [(ar, ai)]
    for _ in range(7):
        pr, pi_ = pows[-1]
        pows.append((pr * ar - pi_ * ai, pr * ai + pi_ * ar))
    row = jnp.arange(8)[:, None]
    out = []
    for k in (1, 2, 4):
        keep = (row < 8 - k) if reverse else (row >= k)
        out += [jnp.where(keep, pows[k - 1][0], 0.0), jnp.where(keep, pows[k - 1][1], 0.0)]
    order = list(range(7, -1, -1)) if reverse else list(range(8))
    out += [jnp.concatenate([pows[k][0] for k in order], axis=0), jnp.concatenate([pows[k][1] for k in order], axis=0)]
    return jnp.stack(out).astype(F32)


def _flat_rows(a, rows):
    flat = a.reshape(-1)
    return jnp.pad(flat, (0, rows * 1024 - flat.shape[0])).reshape(rows, 1024)


def _pack(named, order):
    rows = [-(-math.prod(named[n].shape) // 1024) for n in order]
    total = -(-sum(rows) // 8) * 8
    parts = [_flat_rows(named[n], r) for n, r in zip(order, rows)]
    if total > sum(rows):
        parts.append(jnp.zeros((total - sum(rows), 1024), parts[0].dtype))
    return jnp.concatenate(parts, axis=0)


def _unpack(packed, shapes, order):
    out, at = {}, 0
    for n in order:
        size = math.prod(shapes[n])
        rows = -(-size // 1024)
        out[n] = packed[at:at + rows].reshape(-1)[:size].reshape(shapes[n])
        at += rows
    return out


def _shard_cols(a, k):
    w = a.shape[-1] // 4
    return a[..., k * w:(k + 1) * w]


def kernel(x, meta_tokens, pre_norm_w, post_norm_w, w_in, q_norm_w, w_q_up, kv_norm_w, w_kv_up, attn_out_norm_w, ssm_a_re, ssm_a_im, ssm_log_dt, ssm_b_re, ssm_b_im, ssm_c_re, ssm_c_im, ssm_d, w_glu, b_glu, ssm_out_norm_w, w_out, loss_target, m_meta_tokens, m_pre_norm_w, m_post_norm_w, m_w_in, m_q_norm_w, m_w_q_up, m_kv_norm_w, m_w_kv_up, m_attn_out_norm_w, m_ssm_a_re, m_ssm_a_im, m_ssm_log_dt, m_ssm_b_re, m_ssm_b_im, m_ssm_c_re, m_ssm_c_im, m_ssm_d, m_w_glu, m_b_glu, m_ssm_out_norm_w, m_w_out, v_meta_tokens, v_pre_norm_w, v_post_norm_w, v_w_in, v_q_norm_w, v_w_q_up, v_kv_norm_w, v_w_kv_up, v_attn_out_norm_w, v_ssm_a_re, v_ssm_a_im, v_ssm_log_dt, v_ssm_b_re, v_ssm_b_im, v_ssm_c_re, v_ssm_c_im, v_ssm_d, v_w_glu, v_b_glu, v_ssm_out_norm_w, v_w_out):
    local = dict(meta_tokens=meta_tokens, pre_norm_w=pre_norm_w, post_norm_w=post_norm_w, w_in=w_in, q_norm_w=q_norm_w,
                 w_q_up=w_q_up, kv_norm_w=kv_norm_w, w_kv_up=w_kv_up, attn_out_norm_w=attn_out_norm_w, ssm_a_re=ssm_a_re,
                 ssm_a_im=ssm_a_im, ssm_log_dt=ssm_log_dt, ssm_b_re=ssm_b_re, ssm_b_im=ssm_b_im, ssm_c_re=ssm_c_re,
                 ssm_c_im=ssm_c_im, ssm_d=ssm_d, w_glu=w_glu, b_glu=b_glu, ssm_out_norm_w=ssm_out_norm_w, w_out=w_out)
    mom_m = dict(meta_tokens=m_meta_tokens, pre_norm_w=m_pre_norm_w, post_norm_w=m_post_norm_w, w_in=m_w_in,
                 q_norm_w=m_q_norm_w, w_q_up=m_w_q_up, kv_norm_w=m_kv_norm_w, w_kv_up=m_w_kv_up,
                 attn_out_norm_w=m_attn_out_norm_w, ssm_a_re=m_ssm_a_re, ssm_a_im=m_ssm_a_im, ssm_log_dt=m_ssm_log_dt,
                 ssm_b_re=m_ssm_b_re, ssm_b_im=m_ssm_b_im, ssm_c_re=m_ssm_c_re, ssm_c_im=m_ssm_c_im, ssm_d=m_ssm_d,
                 w_glu=m_w_glu, b_glu=m_b_glu, ssm_out_norm_w=m_ssm_out_norm_w, w_out=m_w_out)
    mom_v = dict(meta_tokens=v_meta_tokens, pre_norm_w=v_pre_norm_w, post_norm_w=v_post_norm_w, w_in=v_w_in,
                 q_norm_w=v_q_norm_w, w_q_up=v_w_q_up, kv_norm_w=v_kv_norm_w, w_kv_up=v_w_kv_up,
                 attn_out_norm_w=v_attn_out_norm_w, ssm_a_re=v_ssm_a_re, ssm_a_im=v_ssm_a_im, ssm_log_dt=v_ssm_log_dt,
                 ssm_b_re=v_ssm_b_re, ssm_b_im=v_ssm_b_im, ssm_c_re=v_ssm_c_re, ssm_c_im=v_ssm_c_im, ssm_d=v_ssm_d,
                 w_glu=v_w_glu, b_glu=v_b_glu, ssm_out_norm_w=v_ssm_out_norm_w, w_out=v_w_out)
    shapes = {n: local[n].shape for n in WEIGHTS}
    mat = ("w_in", "w_q_up", "w_kv_up", "w_glu", "w_out")

    seq = x.shape[1]
    l_real = N_META + seq
    lp = -(-l_real // 1280) * 1280 if l_real > 1280 else -(-l_real // QBLK) * QBLK
    assert seq % 128 == 0 and lp % QBLK == 0

    w_shard = _pack({n: local[n].astype(BF16) for n in mat}, mat)
    w_shard = jnp.pad(w_shard, ((0, -w_shard.shape[0] % 16), (0, 0)))
    w_all, meta_all = _gather_weights(w_shard, meta_tokens)
    mat_shapes = {n: shapes[n] for n in mat}
    per_chip = [_unpack(w_all[k], mat_shapes, mat) for k in range(4)]
    w_in_f = jnp.concatenate([p["w_in"][0] for p in per_chip], axis=1)
    w_q_f = jnp.concatenate([p["w_q_up"][0] for p in per_chip], axis=1)
    w_kv_f = jnp.concatenate([p["w_kv_up"][0] for p in per_chip], axis=1)
    w_glu_f = jnp.concatenate([p["w_glu"][0] for p in per_chip], axis=1)
    w_out_f = jnp.concatenate([p["w_out"][0] for p in per_chip], axis=0)
    meta_f = jnp.concatenate([meta_all[k] for k in range(4)], axis=1)

    o_q, o_kv, o_kr, o_ga, o_u, o_gs = 0, 256, 384, 416, 928, 1440
    krope_cols = jnp.pad(w_in_f[:, o_kr:o_ga], ((0, 0), (QK_NOPE, HEAD_PAD - QK_NOPE - QK_ROPE)))
    w_in_p = jnp.concatenate([_expand_heads(w_in_f[:, o_ga:o_u], 1, V_HEAD), w_in_f[:, o_u:o_gs], w_in_f[:, o_gs:],
                              w_in_f[:, o_q:o_kv], w_in_f[:, o_kv:o_kr], krope_cols], axis=1)
    wq_p = _expand_heads(w_q_f, 1, QK_NOPE + QK_ROPE)
    kv3 = w_kv_f.reshape(KV_LORA, HEADS, QK_NOPE + V_HEAD)
    wk_p = _expand_heads(kv3[:, :, :QK_NOPE].reshape(KV_LORA, HEADS * QK_NOPE), 1, QK_NOPE)
    wv_c = kv3[:, :, QK_NOPE:].reshape(KV_LORA, HEADS * V_HEAD)
    wv_p = _expand_heads(wv_c, 1, V_HEAD)
    wv_t = jnp.pad(wv_c.T.reshape(HEADS, V_HEAD, KV_LORA), ((0, 0), (0, VT_ROWS - V_HEAD), (0, 0))).reshape(HEADS * VT_ROWS, KV_LORA)
    w_out_a = _expand_heads(w_out_f[:D_ATTN], 0, V_HEAD)
    w_out_s = w_out_f[D_ATTN:]
    attn_norm_e = _expand_heads(attn_out_norm_w, 1, V_HEAD)

    pos = jnp.arange(lp, dtype=jnp.int32)
    half = QK_ROPE // 2
    inv = ROPE_THETA ** (-jnp.arange(half, dtype=F32) / half)
    ang = pos.astype(F32)[:, None] * inv[None, :]
    cos16, sin16 = jnp.cos(ang), jnp.sin(ang)
    ones, zeros = jnp.ones((lp, QK_NOPE), F32), jnp.zeros((lp, QK_NOPE), F32)
    tail1, tail0 = jnp.ones((lp, HEAD_PAD - MASK_LANE), F32), jnp.zeros((lp, HEAD_PAD - MASK_LANE), F32)
    z16 = jnp.zeros((lp, half), F32)
    cos = jnp.concatenate([ones, cos16, cos16, tail1], axis=1)
    sina = jnp.concatenate([zeros, z16, sin16, tail0], axis=1)
    sinb = jnp.concatenate([zeros, -sin16, z16, tail0], axis=1)

    disc_in = (ssm_a_re[0], ssm_a_im[0], ssm_log_dt[0], ssm_b_re[0], ssm_b_im[0])
    disc = lambda a_re, a_im, ldt, b_re, b_im: jax.vmap(_discretise)(a_re, a_im, ldt, b_re, b_im)
    (abar_re, abar_im, bbar_re, bbar_im), disc_vjp = jax.vjp(disc, *disc_in)
    ssm = []
    for d in range(2):
        rev = d == 1
        b_re_bd = _block_diag(jnp.swapaxes(bbar_re[d], 1, 2)).astype(BF16)
        b_im_bd = _block_diag(jnp.swapaxes(bbar_im[d], 1, 2)).astype(BF16)
        c_re_bd = _block_diag(jnp.swapaxes(ssm_c_re[0, d], 1, 2)).astype(BF16)
        c_im_bd = _block_diag(jnp.swapaxes(-ssm_c_im[0, d], 1, 2)).astype(BF16)
        ssm.append(dict(rev=rev, coef=_scan_coef(abar_re[d], abar_im[d], rev),
                        coef_adj=_scan_coef(abar_re[d], -abar_im[d], not rev),
                        b_re=b_re_bd, b_im=b_im_bd, c_re=c_re_bd, c_im=c_im_bd))

    h = jnp.concatenate([meta_f, x[0], jnp.zeros((lp - l_real, D_MODEL), F32)], axis=0)
    proj = _in_proj_fwd(h, pre_norm_w, w_in_p)
    q, k, v, vt, q_t, k_t = _attn_prep_fwd(proj, q_norm_w, kv_norm_w, wq_p, wk_p, wv_p, wv_t, cos, sina, sinb, l_real)
    o_exp, lse = _flash_fwd(q, k, vt)
    ys, states = [], []
    for s in ssm:
        y_d, st_d = _ssm_fwd(proj, s["coef"], s["b_re"], s["b_im"], s["c_re"], s["c_im"], s["rev"])
        ys.append(y_d)
        states.append(st_d)

    (d_o, do_t, delta, dga, dyp, dsg, dres, dwoa, dwos, dwglu, vec_mid) = _mid(
        h, loss_target[0], o_exp, proj, ys[0], ys[1], ssm_d, w_glu_f, w_glu_f.T, b_glu, ssm_out_norm_w, attn_norm_e, w_out_a, w_out_s,
        w_out_a.T, w_out_s.T, post_norm_w, l_real)
    dus, dssm = [], []
    tr = lambda a: jnp.swapaxes(a, 1, 2)
    for s, st_d in zip(ssm, states):
        du_d, dbre, dbim, dcre, dcim, da = _ssm_bwd(proj, dyp, st_d, s["coef"], s["coef_adj"], s["b_re"], s["b_im"],
                                                    tr(s["b_re"]), tr(s["b_im"]), tr(s["c_re"]), tr(s["c_im"]), s["rev"])
        dus.append(du_d)
        dssm.append((dbre, dbim, dcre, dcim, da))
    dq, dk_t, dv_t = _flash_bwd(q, k, v, d_o, q_t, k_t, do_t, lse, delta.T.reshape(HEADS, 1, lp))
    dql, dkvl, dkr, dwq_p, dwk_p, dwv_p, vec_prep = _attn_prep_bwd(
        dq, dk_t, dv_t, proj, q_norm_w, kv_norm_w, wq_p.T, wk_p.T, wv_p.T, cos, sina, sinb)
    dh, dwin_p, vec_in = _in_proj_bwd(h, pre_norm_w, dres, dga, dus[0], dus[1], dyp, ssm_d, dsg, dql, dkvl, dkr, w_in_p.T)

    grads = {}
    grads["w_in"] = jnp.concatenate([
        dwin_p[:, P_QLAT[0]:P_QLAT[0] + 256], dwin_p[:, P_KVLAT[0]:P_KVLAT[0] + 128],
        dwin_p[:, P_KROPE[0] + QK_NOPE:P_KROPE[0] + QK_NOPE + QK_ROPE], _compact_heads(dwin_p[:, 0:D_EXP], 1, 0, V_HEAD),
        dwin_p[:, P_U[0]:P_U[0] + 512], dwin_p[:, P_GATE_S[0]:P_GATE_S[0] + 512]], axis=1)[None]
    grads["w_q_up"] = _compact_heads(dwq_p, 1, 0, QK_NOPE + QK_ROPE)[None]
    dwk3 = _compact_heads(dwk_p, 1, 0, QK_NOPE).reshape(KV_LORA, HEADS, QK_NOPE)
    dwv3 = _compact_heads(dwv_p, 1, 0, V_HEAD).reshape(KV_LORA, HEADS, V_HEAD)
    grads["w_kv_up"] = jnp.concatenate([dwk3, dwv3], axis=2).reshape(1, KV_LORA, HEADS * (QK_NOPE + V_HEAD))
    grads["w_glu"] = dwglu[None]
    grads["w_out"] = jnp.concatenate([_compact_heads(dwoa, 0, 0, V_HEAD), dwos], axis=0)[None]
    grads["meta_tokens"] = dh[:N_META]
    grads["pre_norm_w"] = vec_in[0:1]
    grads["post_norm_w"] = vec_mid[0:1]
    grads["q_norm_w"] = vec_prep[0:1]
    grads["kv_norm_w"] = vec_prep[1:2, :KV_LORA]
    grads["attn_out_norm_w"] = _compact_heads(vec_mid[1:2], 1, 0, V_HEAD)
    grads["ssm_out_norm_w"] = vec_mid[2:3, :D_SSM]
    grads["ssm_d"] = vec_mid[3:4, :D_SSM]
    grads["b_glu"] = vec_mid[4:5]
    d_abar_re = jnp.stack([dssm[d][4][0].sum(axis=0).reshape(N_GROUPS, SSM_STATE) for d in range(2)])
    d_abar_im = jnp.stack([dssm[d][4][1].sum(axis=0).reshape(N_GROUPS, SSM_STATE) for d in range(2)])
    d_bbar_re = jnp.stack([jnp.swapaxes(_block_diag_extract(dssm[d][0], SSM_GROUP, SSM_STATE), 1, 2) for d in range(2)])
    d_bbar_im = jnp.stack([jnp.swapaxes(_block_diag_extract(dssm[d][1], SSM_GROUP, SSM_STATE), 1, 2) for d in range(2)])
    da_re, da_im, dlog_dt, db_re, db_im = disc_vjp((d_abar_re, d_abar_im, d_bbar_re, d_bbar_im))
    grads["ssm_a_re"], grads["ssm_a_im"], grads["ssm_log_dt"] = da_re[None], da_im[None], dlog_dt[None]
    grads["ssm_b_re"], grads["ssm_b_im"] = db_re[None], db_im[None]
    grads["ssm_c_re"] = jnp.stack([_block_diag_extract(dssm[d][2], SSM_GROUP, SSM_STATE) for d in range(2)])[None]
    grads["ssm_c_im"] = jnp.stack([_block_diag_extract(dssm[d][3], SSM_GROUP, SSM_STATE) for d in range(2)])[None]

    def shard_of(n, a, kk):
        return a[:, kk * 256:(kk + 1) * 256] if n == "w_out" else _shard_cols(a, kk)

    slices = [_pack({n: shard_of(n, grads[n], kk) for n in BIG}, BIG) for kk in range(4)]
    grads["loss"] = vec_mid[5:6, 0:1]
    small = _pack({n: grads[n] for n in SMALL + ("loss",)}, SMALL + ("loss",))
    loss_row = slices[0].shape[0] + sum(-(-math.prod(shapes[n]) // 1024) for n in SMALL)
    rs, rsm = slices[0].shape[0], small.shape[0]
    g_pack = jnp.concatenate(slices + [small], axis=0)
    g_pair = _pair_sum(g_pack, _swap_sibling(g_pack))
    parts = _scatter_chips(g_pair, rs, rsm)

    order = BIG + SMALL
    big_shapes = {n: shapes[n] for n in BIG}
    small_shapes = {n: shapes[n] for n in SMALL}

    def pack_state(named):
        return jnp.concatenate([_pack({n: named[n] for n in BIG}, BIG), _pack({n: named[n] for n in SMALL}, SMALL)], axis=0)

    g_out, d_out, m_out, v_out = _adamw(parts, pack_state(local), pack_state(mom_m), pack_state(mom_v))

    def unpack_state(p):
        out = _unpack(p[:rs], big_shapes, BIG)
        out.update(_unpack(p[rs:], small_shapes, SMALL))
        return out

    g_fin, d_fin, m_fin, v_fin = unpack_state(g_out), unpack_state(d_out), unpack_state(m_out), unpack_state(v_out)
    loss = g_out[loss_row, 0]
    grad_x = dh[N_META:l_real][None]
    return (loss, grad_x, *[g_fin[n] for n in WEIGHTS], *[d_fin[n] for n in WEIGHTS], *[m_fin[n] for n in WEIGHTS],
            *[v_fin[n] for n in WEIGHTS])
```

```python
import functools
import math

import jax
import jax.numpy as jnp
from jax import lax
from jax.experimental import pallas as pl
from jax.experimental.pallas import tpu as pltpu

F32 = jnp.float32
BF16 = jnp.bfloat16
MESH = pl.DeviceIdType.MESH

D_MODEL = 1024
N_META = 16
EPS = 1e-6
HEADS = 8
QK_NOPE = 64
QK_ROPE = 32
V_HEAD = 64
VT_ROWS = 80
Q_LORA = 256
KV_LORA = 128
D_ATTN = 512
D_SSM = 512
SSM_GROUP = 16
N_GROUPS = 32
SSM_STATE = 64
N_STATE = N_GROUPS * SSM_STATE
ROPE_THETA = 10000.0
HEAD_PAD = 128
D_EXP = HEADS * HEAD_PAD
D_QK = QK_NOPE + QK_ROPE
MASK_LANE = D_QK
NEG_BIG = -1e30
SCALE = 1.0 / math.sqrt(QK_NOPE + QK_ROPE)
LOG2E = math.log2(math.e)
SCALE2 = SCALE * LOG2E
QBLK = 256
SCAN_COLS = 1024
SCAN_UNROLL = 2
SSM_BLOCKS = 4
BLK_CH = D_SSM // SSM_BLOCKS
BLK_ST = N_STATE // SSM_BLOCKS

P_GATE_A = (0, 1024)
P_U = (1024, 512)
P_GATE_S = (1536, 512)
P_QLAT = (2048, 256)
P_KVLAT = (2304, 128)
P_KROPE = (2432, 128)
D_PROJ = 2560

ADAM_LR = 0.001
ADAM_B1 = 0.9
ADAM_B2 = 0.999
ADAM_EPS = 1e-08
ADAM_WD = 0.01
ADAM_STEP = 10

VMEM_LIMIT = 60 * 1024 * 1024

BIG = ("w_in", "w_q_up", "w_kv_up", "w_glu", "w_out", "meta_tokens")
SMALL = ("pre_norm_w", "post_norm_w", "q_norm_w", "kv_norm_w", "attn_out_norm_w", "ssm_a_re", "ssm_a_im",
         "ssm_log_dt", "ssm_b_re", "ssm_b_im", "ssm_c_re", "ssm_c_im", "ssm_d", "b_glu", "ssm_out_norm_w")
WEIGHTS = ("meta_tokens", "pre_norm_w", "post_norm_w", "w_in", "q_norm_w", "w_q_up", "kv_norm_w", "w_kv_up",
           "attn_out_norm_w", "ssm_a_re", "ssm_a_im", "ssm_log_dt", "ssm_b_re", "ssm_b_im", "ssm_c_re", "ssm_c_im",
           "ssm_d", "w_glu", "b_glu", "ssm_out_norm_w", "w_out")


def _cparams(sem=None):
    return pltpu.CompilerParams(dimension_semantics=sem, vmem_limit_bytes=VMEM_LIMIT)


def _dot(a, b):
    return jnp.dot(a, b, preferred_element_type=F32)


def _dot_nt(a, b):
    return lax.dot_general(a, b, (((1,), (1,)), ((), ())), preferred_element_type=F32)


def _dot_tn(a, b):
    return lax.dot_general(a, b, (((0,), (0,)), ((), ())), preferred_element_type=F32)


def _sigmoid(x):
    return 1.0 / (1.0 + jnp.exp(-x))


def _rms_fwd(x, w, n):
    r = lax.rsqrt(jnp.sum(x * x, axis=-1, keepdims=True) * (1.0 / n) + EPS)
    return x * r * w, r


def _rms_bwd(x, r, w, dy, n):
    dyw = dy * w
    dx = r * dyw - x * (r * r * r) * (jnp.sum(dyw * x, axis=-1, keepdims=True) * (1.0 / n))
    dw = jnp.sum(dy * (x * r), axis=0, keepdims=True)
    return dx, dw


def _rope_apply(x, cos, sina, sinb):
    return x * cos + pltpu.roll(x, 16, 1) * sina + pltpu.roll(x, HEAD_PAD - 16, 1) * sinb


def _rope_transpose(g, cos, sina, sinb):
    return g * cos + pltpu.roll(g * sina, HEAD_PAD - 16, 1) + pltpu.roll(g * sinb, 16, 1)


def _row_tile(lp):
    return 640 if lp % 640 == 0 else 128


def _ssm_tile(lp):
    return 320 if lp % 320 == 0 else 128


def _rows(tm, off_width):
    off, width = off_width
    return pl.BlockSpec((tm, width), lambda i: (i, off // width))


def _whole(shape, single=True):
    nd = len(shape)
    if single:
        return pl.BlockSpec(shape, lambda *_: (0,) * nd, pipeline_mode=pl.Buffered(1))
    return pl.BlockSpec(shape, lambda *_: (0,) * nd)


def _out_whole(shape):
    return _whole(shape, single=False)


def _pick_tile(rows, cap):
    best = 8
    for t in range(8, cap + 1, 8):
        if rows % t == 0:
            best = t
    return best


def _in_proj_fwd(h, pre_w, w_in_p):
    lp = h.shape[0]
    tm = _row_tile(lp)

    def body(h_ref, w_ref, win_ref, proj_ref):
        xn, _ = _rms_fwd(h_ref[...], w_ref[...], D_MODEL)
        proj_ref[...] = _dot(xn.astype(BF16), win_ref[...])

    return pl.pallas_call(
        body, name="in_proj_fwd", grid=(lp // tm,),
        in_specs=[_rows(tm, (0, D_MODEL)), _whole((1, D_MODEL)), _whole((D_MODEL, D_PROJ))],
        out_specs=_rows(tm, (0, D_PROJ)),
        out_shape=jax.ShapeDtypeStruct((lp, D_PROJ), F32),
        compiler_params=_cparams(("parallel",)),
    )(h, pre_w, w_in_p)


def _attn_prep_fwd(proj, q_norm_w, kv_norm_w, wq_p, wk_p, wv_p, wv_t, cos, sina, sinb, l_real):
    lp = proj.shape[0]
    tm = _row_tile(lp)

    def body(ql_ref, kvl_ref, kr_ref, qw_ref, kw_ref, wq_ref, wk_ref, wv_ref, wvt_ref, cos_ref, sa_ref, sb_ref,
             q_ref, k_ref, v_ref, vt_ref, qt_ref, kt_ref):
        cos_t, sa_t, sb_t = cos_ref[...], sa_ref[...], sb_ref[...]
        qn, _ = _rms_fwd(ql_ref[...], qw_ref[...], Q_LORA)
        kvn, _ = _rms_fwd(kvl_ref[...], kw_ref[...], KV_LORA)
        kvn_b = kvn.astype(BF16)
        qp = _dot(qn.astype(BF16), wq_ref[...])
        kp = _dot(kvn_b, wk_ref[...])
        v_ref[...] = _dot(kvn_b, wv_ref[...]).astype(BF16)
        ones_row = lax.broadcasted_iota(jnp.int32, (HEADS * VT_ROWS, 1), 0) % VT_ROWS == V_HEAD
        vt_ref[...] = jnp.where(ones_row, 1.0, _dot_nt(wvt_ref[...], kvn_b)).astype(BF16)
        lane = lax.broadcasted_iota(jnp.int32, (tm, HEAD_PAD), 1)
        row = lax.broadcasted_iota(jnp.int32, (tm, HEAD_PAD), 0) + pl.program_id(0) * tm
        q_one = jnp.where(lane == MASK_LANE, 1.0, 0.0)
        k_add = _rope_apply(kr_ref[...], cos_t, sa_t, sb_t) + jnp.where((lane == MASK_LANE) & (row >= l_real), NEG_BIG, 0.0)
        for hd in range(HEADS):
            blk = slice(hd * HEAD_PAD, (hd + 1) * HEAD_PAD)
            q_h = _rope_apply(qp[:, blk], cos_t, sa_t, sb_t) * SCALE2 + q_one
            k_h = kp[:, blk] + k_add
            q_ref[:, blk] = q_h.astype(BF16)
            k_ref[:, blk] = k_h.astype(BF16)
            qt_ref[hd * D_QK:(hd + 1) * D_QK, :] = q_h.T[:D_QK].astype(BF16)
            kt_ref[hd * D_QK:(hd + 1) * D_QK, :] = k_h.T[:D_QK].astype(BF16)

    tab = _rows(tm, (0, HEAD_PAD))
    out = jax.ShapeDtypeStruct((lp, D_EXP), BF16)
    out_t = jax.ShapeDtypeStruct((HEADS * D_QK, lp), BF16)
    cols_t = pl.BlockSpec((HEADS * D_QK, tm), lambda i: (0, i))
    return pl.pallas_call(
        body, name="attn_prep_fwd", grid=(lp // tm,),
        in_specs=[_rows(tm, P_QLAT), _rows(tm, P_KVLAT), _rows(tm, P_KROPE), _whole((1, Q_LORA)), _whole((1, KV_LORA)),
                  _whole((Q_LORA, D_EXP)), _whole((KV_LORA, D_EXP)), _whole((KV_LORA, D_EXP)),
                  _whole((HEADS * VT_ROWS, KV_LORA)), tab, tab, tab],
        out_specs=[_rows(tm, (0, D_EXP))] * 3 + [pl.BlockSpec((HEADS * VT_ROWS, tm), lambda i: (0, i)), cols_t, cols_t],
        out_shape=[out, out, out, jax.ShapeDtypeStruct((HEADS * VT_ROWS, lp), BF16), out_t, out_t],
        compiler_params=_cparams(("parallel",)),
    )(proj, proj, proj, q_norm_w, kv_norm_w, wq_p, wk_p, wv_p, wv_t, cos, sina, sinb)


def _flash_fwd(q, k, vt):
    lp = q.shape[0]
    tq = 1280 if lp % 1280 == 0 else 256
    tk = QBLK
    nk = lp // tk

    def body(q_ref, k_ref, vt_ref, o_ref, lse_ref, acc, m_s, s_a, s_b):
        acc[...] = jnp.zeros_like(acc)
        m_s[...] = jnp.full(m_s.shape, NEG_BIG, F32)
        blocks = [slice(c * QBLK, (c + 1) * QBLK) for c in range(tq // QBLK)]

        def scores(j, buf):
            kt = k_ref[pl.ds(pl.multiple_of(j * tk, tk), tk), :]
            for cols in blocks:
                buf[:, cols] = _dot_nt(kt, q_ref[cols, :])

        def consume(j, buf):
            vt_t = vt_ref[:, pl.ds(pl.multiple_of(j * tk, tk), tk)]
            m_old, acc_old = m_s[...], acc[...]
            s = [buf[:, cols] for cols in blocks]
            m_new = [jnp.maximum(m_old[:, cols], jnp.max(s_c, axis=0, keepdims=True)) for cols, s_c in zip(blocks, s)]
            p = [jnp.exp2(s_c - m_c) for s_c, m_c in zip(s, m_new)]
            pv = [_dot(vt_t, p_c.astype(BF16)) for p_c in p]
            m_new = jnp.concatenate(m_new, axis=1)
            alpha = jnp.exp2(m_old - m_new)
            acc[...] = alpha * acc_old + jnp.concatenate(pv, axis=1)
            m_s[...] = m_new

        n_pairs = (nk - 1) // 2
        scores(0, s_a)

        def pair(t, _):
            j = 2 * t
            scores(j + 1, s_b)
            consume(j, s_a)
            scores(j + 2, s_a)
            consume(j + 1, s_b)
            return 0

        lax.fori_loop(0, n_pairs, pair, 0, unroll=2 if n_pairs % 2 == 0 else 1)
        if nk - 2 * n_pairs == 2:
            scores(nk - 1, s_b)
            consume(nk - 2, s_a)
            consume(nk - 1, s_b)
        else:
            consume(nk - 1, s_a)
        l = acc[V_HEAD:V_HEAD + 1, :]
        o_t = acc[0:V_HEAD, :] / l
        o_ref[...] = jnp.concatenate([o_t, jnp.zeros_like(o_t)], axis=0).T
        lse_ref[...] = m_s[...] + jnp.log2(l)

    return pl.pallas_call(
        body, name="flash_fwd", grid=(HEADS, lp // tq),
        in_specs=[pl.BlockSpec((tq, HEAD_PAD), lambda hd, i: (i, hd)),
                  pl.BlockSpec((lp, HEAD_PAD), lambda hd, i: (0, hd)),
                  pl.BlockSpec((VT_ROWS, lp), lambda hd, i: (hd, 0))],
        out_specs=[pl.BlockSpec((tq, HEAD_PAD), lambda hd, i: (i, hd)),
                   pl.BlockSpec((None, 1, tq), lambda hd, i: (hd, 0, i))],
        out_shape=[jax.ShapeDtypeStruct((lp, D_EXP), F32), jax.ShapeDtypeStruct((HEADS, 1, lp), F32)],
        scratch_shapes=[pltpu.VMEM((VT_ROWS, tq), F32), pltpu.VMEM((1, tq), F32),
                        pltpu.VMEM((tk, tq), F32), pltpu.VMEM((tk, tq), F32)],
        compiler_params=_cparams(("parallel", "parallel")),
    )(q, k, vt)


def _unpermute_rows(val, scr, out_ref, seg):
    for c in range(val.shape[1] // 128):
        scr[c] = val[:, c * 128:(c + 1) * 128]
    for k in range(8):
        for c in range(val.shape[1] // 128):
            out_ref[k * seg:(k + 1) * seg, c * 128:(c + 1) * 128] = scr[c, pl.ds(k, seg, stride=8), :]


def _scan_rows(xr_ref, xi_ref, base, n_rows, coef_ref, carry_ref, reverse, tile_fn=None, acc_refs=(), halo=False):
    seg = n_rows // 8
    shifts = (7, 6, 4) if reverse else (1, 2, 4)
    row8 = lax.broadcasted_iota(jnp.int32, (8, SCAN_COLS), 0)
    edge, shift = (7, 7) if reverse else (0, 1)
    for cg in range(N_STATE // SCAN_COLS):
        cols = slice(cg * SCAN_COLS, (cg + 1) * SCAN_COLS)
        ar, ai = coef_ref[8, :, cols], coef_ref[9, :, cols]

        def rows_at(i):
            tau = (seg - 1 - i) if reverse else i
            return tau, pl.ds(pl.multiple_of(base + tau * 8, 8), 8)

        def local(i, carry, cols=cols, ar=ar, ai=ai):
            pr, pi_ = carry
            _, rows = rows_at(i)
            nr = ar * pr - ai * pi_ + xr_ref[rows, cols]
            ni = ar * pi_ + ai * pr + xi_ref[rows, cols]
            xr_ref[rows, cols] = nr
            xi_ref[rows, cols] = ni
            return nr, ni

        zero = jnp.zeros((8, SCAN_COLS), F32)
        fr, fi = lax.fori_loop(0, seg, local, (zero, zero), unroll=SCAN_UNROLL)
        co = [coef_ref[k, :, cols] for k in range(8)]
        for lvl in range(3):
            pr, pi_ = co[2 * lvl], co[2 * lvl + 1]
            sr = pltpu.roll(fr, shifts[lvl], 0)
            si = pltpu.roll(fi, shifts[lvl], 0)
            fr, fi = fr + pr * sr - pi_ * si, fi + pr * si + pi_ * sr
        cr, ci = carry_ref[0:1, cols], carry_ref[1:2, cols]
        fr, fi = fr + co[6] * cr - co[7] * ci, fi + co[6] * ci + co[7] * cr
        carry_ref[0:1, cols] = fr[0:1] if reverse else fr[7:8]
        carry_ref[1:2, cols] = fi[0:1] if reverse else fi[7:8]
        in_r = jnp.where(row8 == edge, cr, pltpu.roll(fr, shift, 0))
        in_i = jnp.where(row8 == edge, ci, pltpu.roll(fi, shift, 0))
        if halo:
            rows = pl.ds(base + n_rows, 8) if reverse else pl.ds(base - 8, 8)
            xr_ref[rows, cols] = in_r
            xi_ref[rows, cols] = in_i

        def fix(i, carry, cols=cols, ar=ar, ai=ai):
            c_r, c_i = carry[0], carry[1]
            tau, rows = rows_at(i)
            nr = xr_ref[rows, cols] + c_r
            ni = xi_ref[rows, cols] + c_i
            xr_ref[rows, cols] = nr
            xi_ref[rows, cols] = ni
            accs = carry[2:]
            if tile_fn is not None:
                accs = tuple(a + d for a, d in zip(accs, tile_fn(tau, cols, nr, ni)))
            return (ar * c_r - ai * c_i, ar * c_i + ai * c_r) + accs

        init = (ar * in_r - ai * in_i, ar * in_i + ai * in_r) + tuple(a[:, cols] for a in acc_refs)
        out = lax.fori_loop(0, seg, fix, init, unroll=SCAN_UNROLL)
        for a, val in zip(acc_refs, out[2:]):
            a[:, cols] = val


def _ssm_fwd(proj, perm, coef, b_re, b_im, c_re, c_im_neg, reverse):
    lp = proj.shape[0]
    t = _ssm_tile(lp)
    n = lp // t
    order = (lambda i: n - 1 - i) if reverse else (lambda i: i)

    def body(u_ref, pm_ref, coef_ref, bre_ref, bim_ref, cre_ref, cim_ref, y_ref, st_ref, xr, xi, carry, stage):
        @pl.when(pl.program_id(0) == 0)
        def _():
            carry[...] = jnp.zeros_like(carry)

        st_ref[...] = carry[0:2, :]
        ub = _dot(pm_ref[...], u_ref[...].astype(BF16)).astype(BF16)
        for j in range(SSM_BLOCKS):
            ch, stt = slice(j * BLK_CH, (j + 1) * BLK_CH), slice(j * BLK_ST, (j + 1) * BLK_ST)
            xr[:, stt] = _dot(ub[:, ch], bre_ref[j])
            xi[:, stt] = _dot(ub[:, ch], bim_ref[j])
        _scan_rows(xr, xi, 0, t, coef_ref, carry, reverse)
        y = jnp.concatenate(
            [_dot(xr[:, j * BLK_ST:(j + 1) * BLK_ST].astype(BF16), cre_ref[j])
             + _dot(xi[:, j * BLK_ST:(j + 1) * BLK_ST].astype(BF16), cim_ref[j]) for j in range(SSM_BLOCKS)], axis=1)
        _unpermute_rows(y, stage, y_ref, t // 8)

    wb, wc = _whole((SSM_BLOCKS, BLK_CH, BLK_ST)), _whole((SSM_BLOCKS, BLK_ST, BLK_CH))
    return pl.pallas_call(
        body, name="ssm_fwd_rev" if reverse else "ssm_fwd", grid=(n,),
        in_specs=[pl.BlockSpec((t, D_SSM), lambda i: (order(i), P_U[0] // D_SSM)), _whole((t, t)), _whole((10, 8, N_STATE)),
                  wb, wb, wc, wc],
        out_specs=[pl.BlockSpec((t, D_SSM), lambda i: (order(i), 0)),
                   pl.BlockSpec((None, 2, N_STATE), lambda i: (order(i), 0, 0))],
        out_shape=[jax.ShapeDtypeStruct((lp, D_SSM), F32), jax.ShapeDtypeStruct((n, 2, N_STATE), F32)],
        scratch_shapes=[pltpu.VMEM((t, N_STATE), F32), pltpu.VMEM((t, N_STATE), F32), pltpu.VMEM((8, N_STATE), F32),
                        pltpu.VMEM((D_SSM // 128, t, 128), F32)],
        compiler_params=_cparams(("arbitrary",)),
    )(proj, perm, coef, b_re, b_im, c_re, c_im_neg)


GELU_C0 = math.sqrt(2.0 / math.pi)
GELU_C1 = 0.044715


def _mid(h, tgt, o_exp, proj, y0, y1, ssm_d, w_glu, w_glu_t, b_glu, ssm_norm_w, attn_norm_w_e, w_out_a, w_out_s,
         w_out_a_t, w_out_s_t, post_w, l_real):
    lp = h.shape[0]
    tm = 128

    def body(h_ref, tga_ref, tgb_ref, o_ref, ga_ref, u_ref, sg_ref, y0_ref, y1_ref, d_ref, wg_ref, wgt_ref, bg_ref, ws_ref,
             wa_ref, woa_ref, wos_ref, woat_ref, wost_ref, pw_ref,
             do_ref, dot_ref, delta_ref, dga_ref, dyp_ref, dsg_ref, dres_ref, dwoa_ref, dwos_ref, dwg_ref, vec_ref):
        @pl.when(pl.program_id(0) == 0)
        def _():
            dwoa_ref[...] = jnp.zeros_like(dwoa_ref)
            dwos_ref[...] = jnp.zeros_like(dwos_ref)
            dwg_ref[...] = jnp.zeros_like(dwg_ref)
            vec_ref[...] = jnp.zeros_like(vec_ref)

        u = u_ref[...]
        ypre = y0_ref[...] + y1_ref[...] + d_ref[...] * u
        th = jnp.tanh(GELU_C0 * (ypre + GELU_C1 * ypre * ypre * ypre))
        gel = 0.5 * ypre * (1.0 + th)
        gel_b = gel.astype(BF16)
        glu = _dot(gel_b, wg_ref[...]) + bg_ref[...]
        g1, g2 = glu[:, :D_SSM], glu[:, D_SSM:]
        sig2 = _sigmoid(g2)
        z = g1 * sig2
        sg = sg_ref[...]
        sgs = _sigmoid(sg)
        sil_s = sg * sgs
        s = z * sil_s
        ys, r_s = _rms_fwd(s, ws_ref[...], D_SSM)

        o = o_ref[...]
        ga = ga_ref[...]
        gas = _sigmoid(ga)
        sil_a = ga * gas
        a = o * sil_a
        ya, r_a = _rms_fwd(a, wa_ref[...], D_ATTN)

        ya_b, ys_b = ya.astype(BF16), ys.astype(BF16)
        y = _dot(ya_b, woa_ref[...]) + _dot(ys_b, wos_ref[...])
        yn, r_y = _rms_fwd(y, pw_ref[...], D_MODEL)
        row = lax.broadcasted_iota(jnp.int32, (tm, 1), 0) + pl.program_id(0) * tm
        valid = (row >= N_META) & (row < l_real)
        tgt = jnp.concatenate([tga_ref[tm - N_META:, :], tgb_ref[:tm - N_META, :]], axis=0)
        err = jnp.where(valid, h_ref[...] + yn - tgt, 0.0)
        loss = 0.5 * jnp.sum(jnp.sum(err * err, axis=-1, keepdims=True), axis=0, keepdims=True) * (1.0 / D_MODEL)
        dout = err * (1.0 / D_MODEL)
        dres_ref[...] = dout

        dy, d_pw = _rms_bwd(y, r_y, pw_ref[...], dout, D_MODEL)
        dy_b = dy.astype(BF16)
        dya = _dot(dy_b, woat_ref[...])
        dys = _dot(dy_b, wost_ref[...])
        dwoa_ref[...] += _dot_tn(ya_b, dy_b)
        dwos_ref[...] += _dot_tn(ys_b, dy_b)

        da, d_wa = _rms_bwd(a, r_a, wa_ref[...], dya, D_ATTN)
        d_o = da * sil_a
        dga_ref[...] = da * o * (gas * (1.0 + ga * (1.0 - gas)))
        do_ref[...] = d_o.astype(BF16)
        for hd in range(HEADS):
            dot_ref[hd * V_HEAD:(hd + 1) * V_HEAD, :] = d_o[:, hd * HEAD_PAD:(hd + 1) * HEAD_PAD].T[:V_HEAD].astype(BF16)
        prod = d_o * o
        lane8 = lax.broadcasted_iota(jnp.int32, (tm, HEADS), 1)
        delta = jnp.zeros((tm, HEADS), F32)
        for hd in range(HEADS):
            delta = jnp.where(lane8 == hd, jnp.sum(prod[:, hd * HEAD_PAD:(hd + 1) * HEAD_PAD], axis=-1, keepdims=True), delta)
        delta_ref[...] = delta

        ds, d_ws = _rms_bwd(s, r_s, ws_ref[...], dys, D_SSM)
        dz = ds * sil_s
        dsg_ref[...] = ds * z * (sgs * (1.0 + sg * (1.0 - sgs)))
        dglu = jnp.concatenate([dz * sig2, dz * g1 * sig2 * (1.0 - sig2)], axis=-1)
        dglu_b = dglu.astype(BF16)
        dwg_ref[...] += _dot_tn(gel_b, dglu_b)
        dgel = _dot(dglu_b, wgt_ref[...])
        dgelu = 0.5 * (1.0 + th) + 0.5 * ypre * (1.0 - th * th) * (GELU_C0 * (1.0 + 3.0 * GELU_C1 * ypre * ypre))
        dyp = dgel * dgelu
        dyp_ref[...] = dyp

        vec_ref[0:1, :] += d_pw
        vec_ref[1:2, :] += d_wa
        vec_ref[2:3, 0:D_SSM] += d_ws
        vec_ref[3:4, 0:D_SSM] += jnp.sum(dyp * u, axis=0, keepdims=True)
        vec_ref[4:5, :] += jnp.sum(dglu, axis=0, keepdims=True)
        vec_ref[5:6, :] += jnp.broadcast_to(loss, (1, D_MODEL))

    full = lambda off: _rows(tm, (off, D_MODEL))
    half = lambda off: _rows(tm, (off, D_SSM))
    last = tgt.shape[0] // tm - 1
    tg_a = pl.BlockSpec((tm, D_MODEL), lambda i: (jnp.clip(i - 1, 0, last), 0))
    tg_b = pl.BlockSpec((tm, D_MODEL), lambda i: (jnp.minimum(i, last), 0))
    return pl.pallas_call(
        body, name="mid", grid=(lp // tm,),
        in_specs=[full(0), tg_a, tg_b, full(0), _rows(tm, P_GATE_A), _rows(tm, P_U), _rows(tm, P_GATE_S), half(0), half(0),
                  _whole((1, D_SSM)), _whole((D_SSM, 2 * D_SSM)), _whole((2 * D_SSM, D_SSM)), _whole((1, 2 * D_SSM)),
                  _whole((1, D_SSM)), _whole((1, D_EXP)), _whole((D_EXP, D_MODEL)), _whole((D_SSM, D_MODEL)),
                  _whole((D_MODEL, D_EXP)), _whole((D_MODEL, D_SSM)), _whole((1, D_MODEL))],
        out_specs=[full(0), pl.BlockSpec((D_ATTN, tm), lambda i: (0, i)), _rows(tm, (0, HEADS)), full(0), half(0), half(0), full(0),
                   _out_whole((D_EXP, D_MODEL)), _out_whole((D_SSM, D_MODEL)), _out_whole((D_SSM, 2 * D_SSM)),
                   _out_whole((8, D_MODEL))],
        out_shape=[jax.ShapeDtypeStruct((lp, D_EXP), BF16), jax.ShapeDtypeStruct((D_ATTN, lp), BF16),
                   jax.ShapeDtypeStruct((lp, HEADS), F32),
                   jax.ShapeDtypeStruct((lp, D_EXP), F32), jax.ShapeDtypeStruct((lp, D_SSM), F32),
                   jax.ShapeDtypeStruct((lp, D_SSM), F32), jax.ShapeDtypeStruct((lp, D_MODEL), F32),
                   jax.ShapeDtypeStruct((D_EXP, D_MODEL), F32), jax.ShapeDtypeStruct((D_SSM, D_MODEL), F32),
                   jax.ShapeDtypeStruct((D_SSM, 2 * D_SSM), F32), jax.ShapeDtypeStruct((8, D_MODEL), F32)],
        compiler_params=_cparams(("arbitrary",)),
    )(h, tgt, tgt, o_exp, proj, proj, proj, y0, y1, ssm_d, w_glu, w_glu_t, b_glu, ssm_norm_w, attn_norm_w_e, w_out_a, w_out_s,
      w_out_a_t, w_out_s_t, post_w)


def _ssm_bwd(proj, dyp, states, perm, coef, coef_adj, b_re, b_im, b_re_t, b_im_t, c_re_t, c_im_neg_t, reverse):
    lp = proj.shape[0]
    t = _ssm_tile(lp)
    n = lp // t
    order = (lambda i: i) if reverse else (lambda i: n - 1 - i)

    def body(u_ref, dy_ref, st_ref, pm_ref, coef_ref, coefa_ref, bre_ref, bim_ref, bret_ref, bimt_ref, cret_ref, cimt_ref,
             du_ref, dbre_ref, dbim_ref, dcre_ref, dcim_ref, da_ref, xr, xi, gr, gi, carry_x, carry_g, stage):
        @pl.when(pl.program_id(0) == 0)
        def _():
            carry_g[...] = jnp.zeros_like(carry_g)
            carry_x[...] = jnp.zeros_like(carry_x)
            dbre_ref[...] = jnp.zeros_like(dbre_ref)
            dbim_ref[...] = jnp.zeros_like(dbim_ref)
            dcre_ref[...] = jnp.zeros_like(dcre_ref)
            dcim_ref[...] = jnp.zeros_like(dcim_ref)
            da_ref[...] = jnp.zeros_like(da_ref)
            for halo in (slice(0, 8), slice(t + 8, t + 16)):
                xr[halo, :] = jnp.zeros((8, N_STATE), F32)
                xi[halo, :] = jnp.zeros((8, N_STATE), F32)

        ub = _dot(pm_ref[...], u_ref[...].astype(BF16)).astype(BF16)
        dyb = _dot(pm_ref[...], dy_ref[...].astype(BF16)).astype(BF16)
        carry_x[0:2, :] = st_ref[...]
        blocks = [(slice(j * BLK_CH, (j + 1) * BLK_CH), slice(j * BLK_ST, (j + 1) * BLK_ST)) for j in range(SSM_BLOCKS)]
        for j, (ch, stt) in enumerate(blocks):
            xr[8:t + 8, stt] = _dot(ub[:, ch], bre_ref[j])
            xi[8:t + 8, stt] = _dot(ub[:, ch], bim_ref[j])
            gr[:, stt] = _dot(dyb[:, ch], cret_ref[j])
            gi[:, stt] = _dot(dyb[:, ch], cimt_ref[j])
        _scan_rows(xr, xi, 8, t, coef_ref, carry_x, reverse, halo=True)

        def tile_fn(tau, cols, g_re, g_im):
            nb = pl.ds(pl.multiple_of((tau + 2) * 8 if reverse else tau * 8, 8), 8)
            xn_r, xn_i = xr[nb, cols], xi[nb, cols]
            return g_re * xn_r + g_im * xn_i, g_im * xn_r - g_re * xn_i

        _scan_rows(gr, gi, 0, t, coefa_ref, carry_g, not reverse, tile_fn=tile_fn, acc_refs=(da_ref.at[0], da_ref.at[1]))

        du = []
        for j, (ch, stt) in enumerate(blocks):
            g_re_b, g_im_b = gr[:, stt].astype(BF16), gi[:, stt].astype(BF16)
            du.append(_dot(g_re_b, bret_ref[j]) + _dot(g_im_b, bimt_ref[j]))
            dbre_ref[j] += _dot_tn(ub[:, ch], g_re_b)
            dbim_ref[j] += _dot_tn(ub[:, ch], g_im_b)
            dcre_ref[j] += _dot_tn(dyb[:, ch], xr[8:t + 8, stt].astype(BF16))
            dcim_ref[j] -= _dot_tn(dyb[:, ch], xi[8:t + 8, stt].astype(BF16))
        _unpermute_rows(jnp.concatenate(du, axis=1), stage, du_ref, t // 8)

    dense = jax.ShapeDtypeStruct((SSM_BLOCKS, BLK_CH, BLK_ST), F32)
    wb, wc = _whole((SSM_BLOCKS, BLK_CH, BLK_ST)), _whole((SSM_BLOCKS, BLK_ST, BLK_CH))
    acc = _out_whole((SSM_BLOCKS, BLK_CH, BLK_ST))
    return pl.pallas_call(
        body, name="ssm_bwd_rev" if reverse else "ssm_bwd", grid=(n,),
        in_specs=[pl.BlockSpec((t, D_SSM), lambda i: (order(i), P_U[0] // D_SSM)),
                  pl.BlockSpec((t, D_SSM), lambda i: (order(i), 0)),
                  pl.BlockSpec((None, 2, N_STATE), lambda i: (order(i), 0, 0)), _whole((t, t)),
                  _whole((10, 8, N_STATE)), _whole((10, 8, N_STATE)), wb, wb, wc, wc, wb, wb],
        out_specs=[pl.BlockSpec((t, D_SSM), lambda i: (order(i), 0)), acc, acc, acc, acc, _out_whole((2, 8, N_STATE))],
        out_shape=[jax.ShapeDtypeStruct((lp, D_SSM), F32), dense, dense, dense, dense,
                   jax.ShapeDtypeStruct((2, 8, N_STATE), F32)],
        scratch_shapes=[pltpu.VMEM((t + 16, N_STATE), F32), pltpu.VMEM((t + 16, N_STATE), F32),
                        pltpu.VMEM((t, N_STATE), F32), pltpu.VMEM((t, N_STATE), F32),
                        pltpu.VMEM((8, N_STATE), F32), pltpu.VMEM((8, N_STATE), F32),
                        pltpu.VMEM((D_SSM // 128, t, 128), F32)],
        compiler_params=_cparams(("arbitrary",)),
    )(proj, dyp, states, perm, coef, coef_adj, b_re, b_im, b_re_t, b_im_t, c_re_t, c_im_neg_t)


def _flash_bwd(q, k, v, d_o, q_t, k_t, do_t, lse_row, delta_row):
    lp = q.shape[0]
    tq = 1280 if lp % 1280 == 0 else 256
    tk = QBLK
    nk = lp // tk
    d_qk = QK_NOPE + QK_ROPE

    def body(q_ref, do_ref, qt_ref, dot_ref, lse_ref, delta_ref, k_ref, v_ref, kt_ref, dq_ref, dk_ref, dv_ref, dq_acc,
             s_a, dp_a, s_b, dp_b):
        @pl.when(pl.program_id(1) == 0)
        def _():
            dk_ref[...] = jnp.zeros_like(dk_ref)
            dv_ref[...] = jnp.zeros_like(dv_ref)

        dq_acc[...] = jnp.zeros_like(dq_acc)
        lse, delta = lse_ref[...], delta_ref[...]
        q_cols, do_cols = qt_ref[...], dot_ref[...]
        blocks = [slice(c * QBLK, (c + 1) * QBLK) for c in range(tq // QBLK)]

        def scores(j, s_buf, dp_buf):
            ks = pl.multiple_of(j * tk, tk)
            k_rows, v_rows = k_ref[pl.ds(ks, tk), :], v_ref[pl.ds(ks, tk), :]
            for cols in blocks:
                s_buf[:, cols] = _dot_nt(k_rows, q_ref[cols, :])
                dp_buf[:, cols] = _dot_nt(v_rows, do_ref[cols, :])

        def consume(j, s_buf, dp_buf):
            ks = pl.multiple_of(j * tk, tk)
            dq_old = dq_acc[...]
            pt = [jnp.exp2(s_buf[:, cols] - lse[:, cols]) for cols in blocks]
            dst = [p_c * (dp_buf[:, cols] - delta[:, cols]) for p_c, cols in zip(pt, blocks)]
            pt_b = jnp.concatenate([p_c.astype(BF16) for p_c in pt], axis=1)
            dst_b = jnp.concatenate([d_c.astype(BF16) for d_c in dst], axis=1)
            dv_ref[:, pl.ds(ks, tk)] += _dot_nt(do_cols, pt_b)
            dk_ref[:, pl.ds(ks, tk)] += _dot_nt(q_cols, dst_b) * (1.0 / LOG2E)
            dq_acc[...] = dq_old + _dot(kt_ref[:, pl.ds(ks, tk)], dst_b)

        n_pairs = (nk - 1) // 2
        scores(0, s_a, dp_a)

        def pair(t, _):
            j = 2 * t
            scores(j + 1, s_b, dp_b)
            consume(j, s_a, dp_a)
            scores(j + 2, s_a, dp_a)
            consume(j + 1, s_b, dp_b)
            return 0

        lax.fori_loop(0, n_pairs, pair, 0, unroll=2 if n_pairs % 2 == 0 else 1)
        if nk - 2 * n_pairs == 2:
            scores(nk - 1, s_b, dp_b)
            consume(nk - 2, s_a, dp_a)
            consume(nk - 1, s_b, dp_b)
        else:
            consume(nk - 1, s_a, dp_a)
        dq_ref[...] = jnp.concatenate([dq_acc[...], jnp.zeros((HEAD_PAD - d_qk, tq), F32)], axis=0).T

    tile = pl.BlockSpec((tq, HEAD_PAD), lambda hd, i: (i, hd))
    head = pl.BlockSpec((lp, HEAD_PAD), lambda hd, i: (0, hd))
    rowv = pl.BlockSpec((None, 1, tq), lambda hd, i: (hd, 0, i))
    return pl.pallas_call(
        body, name="flash_bwd", grid=(HEADS, lp // tq),
        in_specs=[tile, tile, pl.BlockSpec((d_qk, tq), lambda hd, i: (hd, i)), pl.BlockSpec((V_HEAD, tq), lambda hd, i: (hd, i)),
                  rowv, rowv, head, head, pl.BlockSpec((d_qk, lp), lambda hd, i: (hd, 0))],
        out_specs=[tile, pl.BlockSpec((d_qk, lp), lambda hd, i: (hd, 0)), pl.BlockSpec((V_HEAD, lp), lambda hd, i: (hd, 0))],
        out_shape=[jax.ShapeDtypeStruct((lp, D_EXP), F32), jax.ShapeDtypeStruct((HEADS * d_qk, lp), F32),
                   jax.ShapeDtypeStruct((HEADS * V_HEAD, lp), F32)],
        scratch_shapes=[pltpu.VMEM((d_qk, tq), F32)] + [pltpu.VMEM((tk, tq), F32)] * 4,
        compiler_params=_cparams(("parallel", "arbitrary")),
    )(q, d_o, q_t, do_t, lse_row, delta_row, k, v, k_t)


def _attn_prep_bwd(dq, dk_t, dv_t, proj, q_norm_w, kv_norm_w, wq_pt, wk_pt, wv_pt, cos, sina, sinb):
    lp = proj.shape[0]
    tm = _row_tile(lp)

    def body(dq_ref, dk_ref, dv_ref, ql_ref, kvl_ref, qw_ref, kw_ref, wqt_ref, wkt_ref, wvt_ref, cos_ref, sa_ref, sb_ref,
             dql_ref, dkvl_ref, dkr_ref, dwq_ref, dwk_ref, dwv_ref, vec_ref):
        @pl.when(pl.program_id(0) == 0)
        def _():
            dwq_ref[...] = jnp.zeros_like(dwq_ref)
            dwk_ref[...] = jnp.zeros_like(dwk_ref)
            dwv_ref[...] = jnp.zeros_like(dwv_ref)
            vec_ref[...] = jnp.zeros_like(vec_ref)

        cos_t, sa_t, sb_t = cos_ref[...], sa_ref[...], sb_ref[...]

        def head_rows(t_ref, per):
            pad = jnp.zeros((HEAD_PAD - per, tm), F32)
            return jnp.concatenate(
                [jnp.concatenate([t_ref[hd * per:(hd + 1) * per, :], pad], axis=0).T for hd in range(HEADS)], axis=-1)

        dkp = head_rows(dk_ref, D_QK)
        dqp = jnp.concatenate(
            [_rope_transpose(dq_ref[:, hd * HEAD_PAD:(hd + 1) * HEAD_PAD] * SCALE, cos_t, sa_t, sb_t) for hd in range(HEADS)],
            axis=-1)
        dkr = dkp[:, 0:HEAD_PAD]
        for hd in range(1, HEADS):
            dkr = dkr + dkp[:, hd * HEAD_PAD:(hd + 1) * HEAD_PAD]
        dkr_ref[...] = _rope_transpose(dkr, cos_t, sa_t, sb_t)

        qn, r_q = _rms_fwd(ql_ref[...], qw_ref[...], Q_LORA)
        kvn, r_kv = _rms_fwd(kvl_ref[...], kw_ref[...], KV_LORA)
        dqp_b, dkp_b, dv_b = dqp.astype(BF16), dkp.astype(BF16), head_rows(dv_ref, V_HEAD).astype(BF16)
        dqn = _dot(dqp_b, wqt_ref[...])
        dkvn = _dot(dkp_b, wkt_ref[...]) + _dot(dv_b, wvt_ref[...])
        dwq_ref[...] += _dot_tn(qn.astype(BF16), dqp_b)
        dwk_ref[...] += _dot_tn(kvn.astype(BF16), dkp_b)
        dwv_ref[...] += _dot_tn(kvn.astype(BF16), dv_b)
        dql, d_qw = _rms_bwd(ql_ref[...], r_q, qw_ref[...], dqn, Q_LORA)
        dkvl, d_kw = _rms_bwd(kvl_ref[...], r_kv, kw_ref[...], dkvn, KV_LORA)
        dql_ref[...] = dql
        dkvl_ref[...] = dkvl
        vec_ref[0:1, :] += d_qw
        vec_ref[1:2, 0:KV_LORA] += d_kw

    tab = _rows(tm, (0, HEAD_PAD))
    full = _rows(tm, (0, D_EXP))
    return pl.pallas_call(
        body, name="attn_prep_bwd", grid=(lp // tm,),
        in_specs=[full, pl.BlockSpec((HEADS * D_QK, tm), lambda i: (0, i)), pl.BlockSpec((D_ATTN, tm), lambda i: (0, i)),
                  _rows(tm, P_QLAT), _rows(tm, P_KVLAT), _whole((1, Q_LORA)), _whole((1, KV_LORA)),
                  _whole((D_EXP, Q_LORA)), _whole((D_EXP, KV_LORA)), _whole((D_EXP, KV_LORA)), tab, tab, tab],
        out_specs=[_rows(tm, (0, Q_LORA)), _rows(tm, (0, KV_LORA)), _rows(tm, (0, HEAD_PAD)),
                   _out_whole((Q_LORA, D_EXP)), _out_whole((KV_LORA, D_EXP)), _out_whole((KV_LORA, D_EXP)),
                   _out_whole((8, Q_LORA))],
        out_shape=[jax.ShapeDtypeStruct((lp, Q_LORA), F32), jax.ShapeDtypeStruct((lp, KV_LORA), F32),
                   jax.ShapeDtypeStruct((lp, HEAD_PAD), F32), jax.ShapeDtypeStruct((Q_LORA, D_EXP), F32),
                   jax.ShapeDtypeStruct((KV_LORA, D_EXP), F32), jax.ShapeDtypeStruct((KV_LORA, D_EXP), F32),
                   jax.ShapeDtypeStruct((8, Q_LORA), F32)],
        compiler_params=_cparams(("arbitrary",)),
    )(dq, dk_t, dv_t, proj, proj, q_norm_w, kv_norm_w, wq_pt, wk_pt, wv_pt, cos, sina, sinb)


def _in_proj_bwd(h, pre_w, dres, dga, du0, du1, dyp, ssm_d, dsg, dql, dkvl, dkr, w_in_pt):
    lp = h.shape[0]
    tm = 128
    pieces = (P_GATE_A, P_U, P_GATE_S, P_QLAT, P_KVLAT, P_KROPE)

    def body(h_ref, w_ref, dres_ref, dga_ref, du0_ref, du1_ref, dyp_ref, d_ref, dsg_ref, dql_ref, dkvl_ref, dkr_ref, wt_ref,
             dh_ref, dw_ref, vec_ref):
        @pl.when(pl.program_id(0) == 0)
        def _():
            dw_ref[...] = jnp.zeros_like(dw_ref)
            vec_ref[...] = jnp.zeros_like(vec_ref)

        hv = h_ref[...]
        xn, r = _rms_fwd(hv, w_ref[...], D_MODEL)
        xn_b = xn.astype(BF16)
        du = du0_ref[...] + du1_ref[...] + dyp_ref[...] * d_ref[...]
        grads = (dga_ref[...], du, dsg_ref[...], dql_ref[...], dkvl_ref[...], dkr_ref[...])
        dxn = jnp.zeros((tm, D_MODEL), F32)
        for (off, width), g in zip(pieces, grads):
            g_b = g.astype(BF16)
            dxn = dxn + _dot(g_b, wt_ref[off:off + width, :])
            dw_ref[:, off:off + width] += _dot_tn(xn_b, g_b)
        dx, d_w = _rms_bwd(hv, r, w_ref[...], dxn, D_MODEL)
        dh_ref[...] = dres_ref[...] + dx
        vec_ref[0:1, :] += d_w

    full = _rows(tm, (0, D_MODEL))
    half = _rows(tm, (0, D_SSM))
    return pl.pallas_call(
        body, name="in_proj_bwd", grid=(lp // tm,),
        in_specs=[full, _whole((1, D_MODEL)), full, full, half, half, half, _whole((1, D_SSM)), half,
                  _rows(tm, (0, Q_LORA)), _rows(tm, (0, KV_LORA)), _rows(tm, (0, HEAD_PAD)), _whole((D_PROJ, D_MODEL))],
        out_specs=[full, _out_whole((D_MODEL, D_PROJ)), _out_whole((8, D_MODEL))],
        out_shape=[jax.ShapeDtypeStruct((lp, D_MODEL), F32), jax.ShapeDtypeStruct((D_MODEL, D_PROJ), F32),
                   jax.ShapeDtypeStruct((8, D_MODEL), F32)],
        compiler_params=_cparams(("arbitrary",)),
    )(h, pre_w, dres, dga, du0, du1, dyp, ssm_d, dsg, dql, dkvl, dkr, w_in_pt)


def _other_chips(x, y):
    return [(1 - x, y), (x, 1 - y), (1 - x, 1 - y)]


def _gather_weights(w_bf16, meta):
    any_spec = pl.BlockSpec(memory_space=pl.ANY)

    def body(w_ref, m_ref, wout_ref, mout_ref, send_sems, recv_sems, local_sems):
        x, y, c = lax.axis_index("x"), lax.axis_index("y"), lax.axis_index("c")
        me = 2 * x + y
        own = [pltpu.make_async_copy(w_ref, wout_ref.at[me], local_sems.at[0]),
               pltpu.make_async_copy(m_ref, mout_ref.at[me], local_sems.at[1])]
        for cp in own:
            cp.start()
        sends = []
        for j, (tx, ty) in enumerate(_other_chips(x, y)):
            for n, (src, dst) in enumerate(((w_ref, wout_ref), (m_ref, mout_ref))):
                sends.append(pltpu.make_async_remote_copy(
                    src_ref=src, dst_ref=dst.at[me], send_sem=send_sems.at[2 * j + n], recv_sem=recv_sems.at[2 * j + n],
                    device_id=(tx, ty, c), device_id_type=MESH))
        for cp in sends:
            cp.start()
        for j, (tx, ty) in enumerate(_other_chips(x, y)):
            for n, (src, dst) in enumerate(((w_ref, wout_ref), (m_ref, mout_ref))):
                pltpu.make_async_remote_copy(
                    src_ref=src, dst_ref=dst.at[2 * tx + ty], send_sem=send_sems.at[2 * j + n],
                    recv_sem=recv_sems.at[2 * j + n], device_id=(tx, ty, c), device_id_type=MESH).wait_recv()
        for cp in sends:
            cp.wait_send()
        for cp in own:
            cp.wait()

    return pl.pallas_call(
        body, name="gather_weights",
        in_specs=[any_spec, any_spec], out_specs=[any_spec, any_spec],
        out_shape=[jax.ShapeDtypeStruct((4,) + w_bf16.shape, w_bf16.dtype), jax.ShapeDtypeStruct((4,) + meta.shape, meta.dtype)],
        scratch_shapes=[pltpu.SemaphoreType.DMA((6,)), pltpu.SemaphoreType.DMA((6,)), pltpu.SemaphoreType.DMA((2,))],
    )(w_bf16, meta)


def _swap_sibling(g):
    any_spec = pl.BlockSpec(memory_space=pl.ANY)

    def body(g_ref, out_ref, send_sem, recv_sem):
        x, y, c = lax.axis_index("x"), lax.axis_index("y"), lax.axis_index("c")
        cp = pltpu.make_async_remote_copy(src_ref=g_ref, dst_ref=out_ref, send_sem=send_sem, recv_sem=recv_sem,
                                          device_id=(x, y, 1 - c), device_id_type=MESH)
        cp.start()
        cp.wait()

    return pl.pallas_call(
        body, name="swap_sibling", in_specs=[any_spec], out_specs=any_spec,
        out_shape=jax.ShapeDtypeStruct(g.shape, g.dtype),
        scratch_shapes=[pltpu.SemaphoreType.DMA(()), pltpu.SemaphoreType.DMA(())],
    )(g)


def _pair_sum(a, b):
    rows = a.shape[0]
    tm = _pick_tile(rows, 1024)

    def body(a_ref, b_ref, o_ref):
        o_ref[...] = a_ref[...] + b_ref[...]

    spec = pl.BlockSpec((tm, 1024), lambda i: (i, 0))
    return pl.pallas_call(body, name="pair_sum", grid=(rows // tm,), in_specs=[spec, spec], out_specs=spec,
                          out_shape=jax.ShapeDtypeStruct(a.shape, F32), compiler_params=_cparams(("parallel",)))(a, b)


def _scatter_chips(s, rs, rsm):
    any_spec = pl.BlockSpec(memory_space=pl.ANY)

    def body(s_ref, out_ref, send_sems, recv_sems, local_sems):
        x, y, c = lax.axis_index("x"), lax.axis_index("y"), lax.axis_index("c")
        me = 2 * x + y
        small = s_ref.at[pl.ds(4 * rs, rsm)]

        def pieces(target):
            return ((s_ref.at[pl.ds(pl.multiple_of(target * rs, 8), rs)], pl.ds(0, rs)), (small, pl.ds(rs, rsm)))

        own = [pltpu.make_async_copy(src, out_ref.at[me, rows], local_sems.at[n]) for n, (src, rows) in enumerate(pieces(me))]
        for cp in own:
            cp.start()
        sends = []
        for j, (tx, ty) in enumerate(_other_chips(x, y)):
            for n, (src, rows) in enumerate(pieces(2 * tx + ty)):
                sends.append(pltpu.make_async_remote_copy(
                    src_ref=src, dst_ref=out_ref.at[me, rows], send_sem=send_sems.at[2 * j + n],
                    recv_sem=recv_sems.at[2 * j + n], device_id=(tx, ty, c), device_id_type=MESH))
        for cp in sends:
            cp.start()
        for j, (tx, ty) in enumerate(_other_chips(x, y)):
            for n, (src, rows) in enumerate(pieces(me)):
                pltpu.make_async_remote_copy(
                    src_ref=src, dst_ref=out_ref.at[2 * tx + ty, rows], send_sem=send_sems.at[2 * j + n],
                    recv_sem=recv_sems.at[2 * j + n], device_id=(tx, ty, c), device_id_type=MESH).wait_recv()
        for cp in sends:
            cp.wait_send()
        for cp in own:
            cp.wait()

    return pl.pallas_call(
        body, name="scatter_chips", in_specs=[any_spec], out_specs=any_spec,
        out_shape=jax.ShapeDtypeStruct((4, rs + rsm, 1024), F32),
        scratch_shapes=[pltpu.SemaphoreType.DMA((6,)), pltpu.SemaphoreType.DMA((6,)), pltpu.SemaphoreType.DMA((2,))],
    )(s)


def _adamw(parts, w, m, v):
    rows = w.shape[0]
    tm = _pick_tile(rows, 256)
    c1 = 1.0 / (1.0 - ADAM_B1 ** ADAM_STEP)
    c2 = 1.0 / (1.0 - ADAM_B2 ** ADAM_STEP)

    def body(p_ref, w_ref, m_ref, v_ref, g_ref, d_ref, nm_ref, nv_ref):
        g = ((p_ref[0] + p_ref[1]) + p_ref[2]) + p_ref[3]
        nm = ADAM_B1 * m_ref[...] + (1.0 - ADAM_B1) * g
        nv = ADAM_B2 * v_ref[...] + (1.0 - ADAM_B2) * (g * g)
        g_ref[...] = g
        nm_ref[...] = nm
        nv_ref[...] = nv
        d_ref[...] = -ADAM_LR * ((nm * c1) / (jnp.sqrt(nv * c2) + ADAM_EPS) + ADAM_WD * w_ref[...])

    spec = pl.BlockSpec((tm, 1024), lambda i: (i, 0))
    out = jax.ShapeDtypeStruct(w.shape, F32)
    return pl.pallas_call(
        body, name="adamw", grid=(rows // tm,),
        in_specs=[pl.BlockSpec((4, tm, 1024), lambda i: (0, i, 0)), spec, spec, spec],
        out_specs=[spec] * 4, out_shape=[out] * 4, compiler_params=_cparams(("parallel",)),
    )(parts, w, m, v)


def _expand_heads(a, axis, per_head):
    a = jnp.moveaxis(a, axis, -1)
    lead = a.shape[:-1]
    a = a.reshape(lead + (HEADS, per_head))
    a = jnp.pad(a, [(0, 0)] * len(lead) + [(0, 0), (0, HEAD_PAD - per_head)])
    return jnp.moveaxis(a.reshape(lead + (D_EXP,)), -1, axis)


def _compact_heads(a, axis, start, size):
    a = jnp.moveaxis(a, axis, -1)
    lead = a.shape[:-1]
    a = a.reshape(lead + (HEADS, HEAD_PAD))[..., start:start + size]
    return jnp.moveaxis(a.reshape(lead + (HEADS * size,)), -1, axis)


def _block_diag(w):
    g, a, b = w.shape
    per = g // SSM_BLOCKS
    eye = jnp.eye(per, dtype=w.dtype)
    return jnp.einsum("jgab,gk->jgakb", w.reshape(SSM_BLOCKS, per, a, b), eye).reshape(SSM_BLOCKS, per * a, per * b)


def _block_diag_extract(dense, a, b):
    per = N_GROUPS // SSM_BLOCKS
    d5 = dense.reshape(SSM_BLOCKS, per, a, per, b)
    return jnp.einsum("jgakb,gk->jgab", d5, jnp.eye(per, dtype=dense.dtype)).reshape(N_GROUPS, a, b)


def _discretise(a_re, a_im, log_dt, b_re, b_im):
    dt = jnp.exp(log_dt)[:, None]
    mag = jnp.exp(a_re * dt)
    abar_re = mag * jnp.cos(a_im * dt)
    abar_im = mag * jnp.sin(a_im * dt)
    num_re = abar_re - 1.0
    num_im = abar_im
    den = a_re * a_re + a_im * a_im
    coef_re = (num_re * a_re + num_im * a_im) / den
    coef_im = (num_im * a_re - num_re * a_im) / den
    bbar_re = coef_re[..., None] * b_re - coef_im[..., None] * b_im
    bbar_im = coef_re[..., None] * b_im + coef_im[..., None] * b_re
    return abar_re, abar_im, bbar_re, bbar_im


def _scan_coef(ar, ai, reverse, seg):
    ar, ai = ar.reshape(1, N_STATE), ai.reshape(1, N_STATE)
    cmul = lambda x, y: (x[0] * y[0] - x[1] * y[1], x[0] * y[1] + x[1] * y[0])
    p, sq, n = None, (ar, ai), seg
    while n:
        if n & 1:
            p = sq if p is None else cmul(p, sq)
        sq, n = cmul(sq, sq), n >> 1
    pows = [p]
    for _ in range(7):
        pows.append(cmul(pows[-1], p))
    row = jnp.arange(8)[:, None]
    out = []
    for k in (1, 2, 4):
        keep = (row < 8 - k) if reverse else (row >= k)
        out += [jnp.where(keep, pows[k - 1][0], 0.0), jnp.where(keep, pows[k - 1][1], 0.0)]
    order = list(range(7, -1, -1)) if reverse else list(range(8))
    out += [jnp.concatenate([pows[k][0] for k in order], axis=0), jnp.concatenate([pows[k][1] for k in order], axis=0)]
    out += [jnp.broadcast_to(ar, (8, N_STATE)), jnp.broadcast_to(ai, (8, N_STATE))]
    return jnp.stack(out).astype(F32)


def _flat_rows(a, rows):
    flat = a.reshape(-1)
    return jnp.pad(flat, (0, rows * 1024 - flat.shape[0])).reshape(rows, 1024)


def _pack(named, order):
    rows = [-(-math.prod(named[n].shape) // 1024) for n in order]
    total = -(-sum(rows) // 8) * 8
    parts = [_flat_rows(named[n], r) for n, r in zip(order, rows)]
    if total > sum(rows):
        parts.append(jnp.zeros((total - sum(rows), 1024), parts[0].dtype))
    return jnp.concatenate(parts, axis=0)


def _unpack(packed, shapes, order):
    out, at = {}, 0
    for n in order:
        size = math.prod(shapes[n])
        rows = -(-size // 1024)
        out[n] = packed[at:at + rows].reshape(-1)[:size].reshape(shapes[n])
        at += rows
    return out


def _shard_cols(a, k):
    w = a.shape[-1] // 4
    return a[..., k * w:(k + 1) * w]


def kernel(x, meta_tokens, pre_norm_w, post_norm_w, w_in, q_norm_w, w_q_up, kv_norm_w, w_kv_up, attn_out_norm_w, ssm_a_re, ssm_a_im, ssm_log_dt, ssm_b_re, ssm_b_im, ssm_c_re, ssm_c_im, ssm_d, w_glu, b_glu, ssm_out_norm_w, w_out, loss_target, m_meta_tokens, m_pre_norm_w, m_post_norm_w, m_w_in, m_q_norm_w, m_w_q_up, m_kv_norm_w, m_w_kv_up, m_attn_out_norm_w, m_ssm_a_re, m_ssm_a_im, m_ssm_log_dt, m_ssm_b_re, m_ssm_b_im, m_ssm_c_re, m_ssm_c_im, m_ssm_d, m_w_glu, m_b_glu, m_ssm_out_norm_w, m_w_out, v_meta_tokens, v_pre_norm_w, v_post_norm_w, v_w_in, v_q_norm_w, v_w_q_up, v_kv_norm_w, v_w_kv_up, v_attn_out_norm_w, v_ssm_a_re, v_ssm_a_im, v_ssm_log_dt, v_ssm_b_re, v_ssm_b_im, v_ssm_c_re, v_ssm_c_im, v_ssm_d, v_w_glu, v_b_glu, v_ssm_out_norm_w, v_w_out):
    local = dict(meta_tokens=meta_tokens, pre_norm_w=pre_norm_w, post_norm_w=post_norm_w, w_in=w_in, q_norm_w=q_norm_w,
                 w_q_up=w_q_up, kv_norm_w=kv_norm_w, w_kv_up=w_kv_up, attn_out_norm_w=attn_out_norm_w, ssm_a_re=ssm_a_re,
                 ssm_a_im=ssm_a_im, ssm_log_dt=ssm_log_dt, ssm_b_re=ssm_b_re, ssm_b_im=ssm_b_im, ssm_c_re=ssm_c_re,
                 ssm_c_im=ssm_c_im, ssm_d=ssm_d, w_glu=w_glu, b_glu=b_glu, ssm_out_norm_w=ssm_out_norm_w, w_out=w_out)
    mom_m = dict(meta_tokens=m_meta_tokens, pre_norm_w=m_pre_norm_w, post_norm_w=m_post_norm_w, w_in=m_w_in,
                 q_norm_w=m_q_norm_w, w_q_up=m_w_q_up, kv_norm_w=m_kv_norm_w, w_kv_up=m_w_kv_up,
                 attn_out_norm_w=m_attn_out_norm_w, ssm_a_re=m_ssm_a_re, ssm_a_im=m_ssm_a_im, ssm_log_dt=m_ssm_log_dt,
                 ssm_b_re=m_ssm_b_re, ssm_b_im=m_ssm_b_im, ssm_c_re=m_ssm_c_re, ssm_c_im=m_ssm_c_im, ssm_d=m_ssm_d,
                 w_glu=m_w_glu, b_glu=m_b_glu, ssm_out_norm_w=m_ssm_out_norm_w, w_out=m_w_out)
    mom_v = dict(meta_tokens=v_meta_tokens, pre_norm_w=v_pre_norm_w, post_norm_w=v_post_norm_w, w_in=v_w_in,
                 q_norm_w=v_q_norm_w, w_q_up=v_w_q_up, kv_norm_w=v_kv_norm_w, w_kv_up=v_w_kv_up,
                 attn_out_norm_w=v_attn_out_norm_w, ssm_a_re=v_ssm_a_re, ssm_a_im=v_ssm_a_im, ssm_log_dt=v_ssm_log_dt,
                 ssm_b_re=v_ssm_b_re, ssm_b_im=v_ssm_b_im, ssm_c_re=v_ssm_c_re, ssm_c_im=v_ssm_c_im, ssm_d=v_ssm_d,
                 w_glu=v_w_glu, b_glu=v_b_glu, ssm_out_norm_w=v_ssm_out_norm_w, w_out=v_w_out)
    shapes = {n: local[n].shape for n in WEIGHTS}
    mat = ("w_in", "w_q_up", "w_kv_up", "w_glu", "w_out")

    seq = x.shape[1]
    l_real = N_META + seq
    lp = -(-l_real // 1280) * 1280 if l_real > 1280 else -(-l_real // QBLK) * QBLK
    assert seq % 128 == 0 and lp % QBLK == 0

    w_shard = _pack({n: local[n].astype(BF16) for n in mat}, mat)
    w_shard = jnp.pad(w_shard, ((0, -w_shard.shape[0] % 16), (0, 0)))
    w_all, meta_all = _gather_weights(w_shard, meta_tokens)
    mat_shapes = {n: shapes[n] for n in mat}
    per_chip = [_unpack(w_all[k], mat_shapes, mat) for k in range(4)]
    w_in_f = jnp.concatenate([p["w_in"][0] for p in per_chip], axis=1)
    w_q_f = jnp.concatenate([p["w_q_up"][0] for p in per_chip], axis=1)
    w_kv_f = jnp.concatenate([p["w_kv_up"][0] for p in per_chip], axis=1)
    w_glu_f = jnp.concatenate([p["w_glu"][0] for p in per_chip], axis=1)
    w_out_f = jnp.concatenate([p["w_out"][0] for p in per_chip], axis=0)
    meta_f = jnp.concatenate([meta_all[k] for k in range(4)], axis=1)

    o_q, o_kv, o_kr, o_ga, o_u, o_gs = 0, 256, 384, 416, 928, 1440
    krope_cols = jnp.pad(w_in_f[:, o_kr:o_ga], ((0, 0), (QK_NOPE, HEAD_PAD - QK_NOPE - QK_ROPE)))
    w_in_p = jnp.concatenate([_expand_heads(w_in_f[:, o_ga:o_u], 1, V_HEAD), w_in_f[:, o_u:o_gs], w_in_f[:, o_gs:],
                              w_in_f[:, o_q:o_kv], w_in_f[:, o_kv:o_kr], krope_cols], axis=1)
    wq_p = _expand_heads(w_q_f, 1, QK_NOPE + QK_ROPE)
    kv3 = w_kv_f.reshape(KV_LORA, HEADS, QK_NOPE + V_HEAD)
    wk_p = _expand_heads(kv3[:, :, :QK_NOPE].reshape(KV_LORA, HEADS * QK_NOPE), 1, QK_NOPE)
    wv_c = kv3[:, :, QK_NOPE:].reshape(KV_LORA, HEADS * V_HEAD)
    wv_p = _expand_heads(wv_c, 1, V_HEAD)
    wv_t = jnp.pad(wv_c.T.reshape(HEADS, V_HEAD, KV_LORA), ((0, 0), (0, VT_ROWS - V_HEAD), (0, 0))).reshape(HEADS * VT_ROWS, KV_LORA)
    w_out_a = _expand_heads(w_out_f[:D_ATTN], 0, V_HEAD)
    w_out_s = w_out_f[D_ATTN:]
    attn_norm_e = _expand_heads(attn_out_norm_w, 1, V_HEAD)

    pos = jnp.arange(lp, dtype=jnp.int32)
    half = QK_ROPE // 2
    inv = ROPE_THETA ** (-jnp.arange(half, dtype=F32) / half)
    ang = pos.astype(F32)[:, None] * inv[None, :]
    cos16, sin16 = jnp.cos(ang), jnp.sin(ang)
    ones, zeros = jnp.ones((lp, QK_NOPE), F32), jnp.zeros((lp, QK_NOPE), F32)
    tail1, tail0 = jnp.ones((lp, HEAD_PAD - MASK_LANE), F32), jnp.zeros((lp, HEAD_PAD - MASK_LANE), F32)
    z16 = jnp.zeros((lp, half), F32)
    cos = jnp.concatenate([ones, cos16, cos16, tail1], axis=1)
    sina = jnp.concatenate([zeros, z16, sin16, tail0], axis=1)
    sinb = jnp.concatenate([zeros, -sin16, z16, tail0], axis=1)

    disc_in = (ssm_a_re[0], ssm_a_im[0], ssm_log_dt[0], ssm_b_re[0], ssm_b_im[0])
    disc = lambda a_re, a_im, ldt, b_re, b_im: jax.vmap(_discretise)(a_re, a_im, ldt, b_re, b_im)
    (abar_re, abar_im, bbar_re, bbar_im), disc_vjp = jax.vjp(disc, *disc_in)
    ssm = []
    for d in range(2):
        rev = d == 1
        b_re_bd = _block_diag(jnp.swapaxes(bbar_re[d], 1, 2)).astype(BF16)
        b_im_bd = _block_diag(jnp.swapaxes(bbar_im[d], 1, 2)).astype(BF16)
        c_re_bd = _block_diag(jnp.swapaxes(ssm_c_re[0, d], 1, 2)).astype(BF16)
        c_im_bd = _block_diag(jnp.swapaxes(-ssm_c_im[0, d], 1, 2)).astype(BF16)
        ssm.append(dict(rev=rev, coef=_scan_coef(abar_re[d], abar_im[d], rev, _ssm_tile(lp) // 8),
                        coef_adj=_scan_coef(abar_re[d], -abar_im[d], not rev, _ssm_tile(lp) // 8),
                        b_re=b_re_bd, b_im=b_im_bd, c_re=c_re_bd, c_im=c_im_bd))

    t_ssm = _ssm_tile(lp)
    src = (jnp.arange(t_ssm) % 8) * (t_ssm // 8) + jnp.arange(t_ssm) // 8
    perm = (src[:, None] == jnp.arange(t_ssm)[None, :]).astype(BF16)

    h = jnp.concatenate([meta_f, x[0], jnp.zeros((lp - l_real, D_MODEL), F32)], axis=0)
    proj = _in_proj_fwd(h, pre_norm_w, w_in_p)
    q, k, v, vt, q_t, k_t = _attn_prep_fwd(proj, q_norm_w, kv_norm_w, wq_p, wk_p, wv_p, wv_t, cos, sina, sinb, l_real)
    o_exp, lse = _flash_fwd(q, k, vt)
    ys, states = [], []
    for s in ssm:
        y_d, st_d = _ssm_fwd(proj, perm, s["coef"], s["b_re"], s["b_im"], s["c_re"], s["c_im"], s["rev"])
        ys.append(y_d)
        states.append(st_d)

    (d_o, do_t, delta, dga, dyp, dsg, dres, dwoa, dwos, dwglu, vec_mid) = _mid(
        h, loss_target[0], o_exp, proj, ys[0], ys[1], ssm_d, w_glu_f, w_glu_f.T, b_glu, ssm_out_norm_w, attn_norm_e, w_out_a, w_out_s,
        w_out_a.T, w_out_s.T, post_norm_w, l_real)
    dus, dssm = [], []
    tr = lambda a: jnp.swapaxes(a, 1, 2)
    for s, st_d in zip(ssm, states):
        du_d, dbre, dbim, dcre, dcim, da = _ssm_bwd(proj, dyp, st_d, perm, s["coef"], s["coef_adj"], s["b_re"], s["b_im"],
                                                    tr(s["b_re"]), tr(s["b_im"]), tr(s["c_re"]), tr(s["c_im"]), s["rev"])
        dus.append(du_d)
        dssm.append((dbre, dbim, dcre, dcim, da))
    dq, dk_t, dv_t = _flash_bwd(q, k, v, d_o, q_t, k_t, do_t, lse, delta.T.reshape(HEADS, 1, lp))
    dql, dkvl, dkr, dwq_p, dwk_p, dwv_p, vec_prep = _attn_prep_bwd(
        dq, dk_t, dv_t, proj, q_norm_w, kv_norm_w, wq_p.T, wk_p.T, wv_p.T, cos, sina, sinb)
    dh, dwin_p, vec_in = _in_proj_bwd(h, pre_norm_w, dres, dga, dus[0], dus[1], dyp, ssm_d, dsg, dql, dkvl, dkr, w_in_p.T)

    grads = {}
    grads["w_in"] = jnp.concatenate([
        dwin_p[:, P_QLAT[0]:P_QLAT[0] + 256], dwin_p[:, P_KVLAT[0]:P_KVLAT[0] + 128],
        dwin_p[:, P_KROPE[0] + QK_NOPE:P_KROPE[0] + QK_NOPE + QK_ROPE], _compact_heads(dwin_p[:, 0:D_EXP], 1, 0, V_HEAD),
        dwin_p[:, P_U[0]:P_U[0] + 512], dwin_p[:, P_GATE_S[0]:P_GATE_S[0] + 512]], axis=1)[None]
    grads["w_q_up"] = _compact_heads(dwq_p, 1, 0, QK_NOPE + QK_ROPE)[None]
    dwk3 = _compact_heads(dwk_p, 1, 0, QK_NOPE).reshape(KV_LORA, HEADS, QK_NOPE)
    dwv3 = _compact_heads(dwv_p, 1, 0, V_HEAD).reshape(KV_LORA, HEADS, V_HEAD)
    grads["w_kv_up"] = jnp.concatenate([dwk3, dwv3], axis=2).reshape(1, KV_LORA, HEADS * (QK_NOPE + V_HEAD))
    grads["w_glu"] = dwglu[None]
    grads["w_out"] = jnp.concatenate([_compact_heads(dwoa, 0, 0, V_HEAD), dwos], axis=0)[None]
    grads["meta_tokens"] = dh[:N_META]
    grads["pre_norm_w"] = vec_in[0:1]
    grads["post_norm_w"] = vec_mid[0:1]
    grads["q_norm_w"] = vec_prep[0:1]
    grads["kv_norm_w"] = vec_prep[1:2, :KV_LORA]
    grads["attn_out_norm_w"] = _compact_heads(vec_mid[1:2], 1, 0, V_HEAD)
    grads["ssm_out_norm_w"] = vec_mid[2:3, :D_SSM]
    grads["ssm_d"] = vec_mid[3:4, :D_SSM]
    grads["b_glu"] = vec_mid[4:5]
    d_abar_re = jnp.stack([dssm[d][4][0].sum(axis=0).reshape(N_GROUPS, SSM_STATE) for d in range(2)])
    d_abar_im = jnp.stack([dssm[d][4][1].sum(axis=0).reshape(N_GROUPS, SSM_STATE) for d in range(2)])
    d_bbar_re = jnp.stack([jnp.swapaxes(_block_diag_extract(dssm[d][0], SSM_GROUP, SSM_STATE), 1, 2) for d in range(2)])
    d_bbar_im = jnp.stack([jnp.swapaxes(_block_diag_extract(dssm[d][1], SSM_GROUP, SSM_STATE), 1, 2) for d in range(2)])
    da_re, da_im, dlog_dt, db_re, db_im = disc_vjp((d_abar_re, d_abar_im, d_bbar_re, d_bbar_im))
    grads["ssm_a_re"], grads["ssm_a_im"], grads["ssm_log_dt"] = da_re[None], da_im[None], dlog_dt[None]
    grads["ssm_b_re"], grads["ssm_b_im"] = db_re[None], db_im[None]
    grads["ssm_c_re"] = jnp.stack([_block_diag_extract(dssm[d][2], SSM_GROUP, SSM_STATE) for d in range(2)])[None]
    grads["ssm_c_im"] = jnp.stack([_block_diag_extract(dssm[d][3], SSM_GROUP, SSM_STATE) for d in range(2)])[None]

    def shard_of(n, a, kk):
        return a[:, kk * 256:(kk + 1) * 256] if n == "w_out" else _shard_cols(a, kk)

    slices = [_pack({n: shard_of(n, grads[n], kk) for n in BIG}, BIG) for kk in range(4)]
    grads["loss"] = vec_mid[5:6, 0:1]
    small = _pack({n: grads[n] for n in SMALL + ("loss",)}, SMALL + ("loss",))
    loss_row = slices[0].shape[0] + sum(-(-math.prod(shapes[n]) // 1024) for n in SMALL)
    rs, rsm = slices[0].shape[0], small.shape[0]
    g_pack = jnp.concatenate(slices + [small], axis=0)
    g_pair = _pair_sum(g_pack, _swap_sibling(g_pack))
    parts = _scatter_chips(g_pair, rs, rsm)

    order = BIG + SMALL
    big_shapes = {n: shapes[n] for n in BIG}
    small_shapes = {n: shapes[n] for n in SMALL}

    def pack_state(named):
        return jnp.concatenate([_pack({n: named[n] for n in BIG}, BIG), _pack({n: named[n] for n in SMALL}, SMALL)], axis=0)

    g_out, d_out, m_out, v_out = _adamw(parts, pack_state(local), pack_state(mom_m), pack_state(mom_v))

    def unpack_state(p):
        out = _unpack(p[:rs], big_shapes, BIG)
        out.update(_unpack(p[rs:], small_shapes, SMALL))
        return out

    g_fin, d_fin, m_fin, v_fin = unpack_state(g_out), unpack_state(d_out), unpack_state(m_out), unpack_state(v_out)
    loss = g_out[loss_row, 0]
    grad_x = dh[N_META:l_real][None]
    return (loss, grad_x, *[g_fin[n] for n in WEIGHTS], *[d_fin[n] for n in WEIGHTS], *[m_fin[n] for n in WEIGHTS],
            *[v_fin[n] for n in WEIGHTS])
```

```python
import functools
import math

import jax
import jax.numpy as jnp
from jax import lax
from jax.experimental import pallas as pl
from jax.experimental.pallas import tpu as pltpu

F32 = jnp.float32
BF16 = jnp.bfloat16
MESH = pl.DeviceIdType.MESH

D_MODEL = 1024
N_META = 16
EPS = 1e-6
HEADS = 8
QK_NOPE = 64
QK_ROPE = 32
V_HEAD = 64
VT_ROWS = 80
Q_LORA = 256
KV_LORA = 128
D_ATTN = 512
D_SSM = 512
SSM_GROUP = 16
N_GROUPS = 32
SSM_STATE = 64
N_STATE = N_GROUPS * SSM_STATE
ROPE_THETA = 10000.0
HEAD_PAD = 128
D_EXP = HEADS * HEAD_PAD
D_QK = QK_NOPE + QK_ROPE
MASK_LANE = D_QK
NEG_BIG = -1e30
SCALE = 1.0 / math.sqrt(QK_NOPE + QK_ROPE)
LOG2E = math.log2(math.e)
SCALE2 = SCALE * LOG2E
QBLK = 256
SCAN_COLS = 1024
SCAN_UNROLL = 2
SSM_BLOCKS = 4
BLK_CH = D_SSM // SSM_BLOCKS
BLK_ST = N_STATE // SSM_BLOCKS

P_GATE_A = (0, 1024)
P_U = (1024, 512)
P_GATE_S = (1536, 512)
P_QLAT = (2048, 256)
P_KVLAT = (2304, 128)
P_KROPE = (2432, 128)
D_PROJ = 2560

ADAM_LR = 0.001
ADAM_B1 = 0.9
ADAM_B2 = 0.999
ADAM_EPS = 1e-08
ADAM_WD = 0.01
ADAM_STEP = 10

VMEM_LIMIT = 60 * 1024 * 1024

BIG = ("w_in", "w_q_up", "w_kv_up", "w_glu", "w_out", "meta_tokens")
SMALL = ("pre_norm_w", "post_norm_w", "q_norm_w", "kv_norm_w", "attn_out_norm_w", "ssm_a_re", "ssm_a_im",
         "ssm_log_dt", "ssm_b_re", "ssm_b_im", "ssm_c_re", "ssm_c_im", "ssm_d", "b_glu", "ssm_out_norm_w")
WEIGHTS = ("meta_tokens", "pre_norm_w", "post_norm_w", "w_in", "q_norm_w", "w_q_up", "kv_norm_w", "w_kv_up",
           "attn_out_norm_w", "ssm_a_re", "ssm_a_im", "ssm_log_dt", "ssm_b_re", "ssm_b_im", "ssm_c_re", "ssm_c_im",
           "ssm_d", "w_glu", "b_glu", "ssm_out_norm_w", "w_out")


def _cparams(sem=None):
    return pltpu.CompilerParams(dimension_semantics=sem, vmem_limit_bytes=VMEM_LIMIT)


def _dot(a, b):
    return jnp.dot(a, b, preferred_element_type=F32)


def _dot_nt(a, b):
    return lax.dot_general(a, b, (((1,), (1,)), ((), ())), preferred_element_type=F32)


def _dot_tn(a, b):
    return lax.dot_general(a, b, (((0,), (0,)), ((), ())), preferred_element_type=F32)


def _sigmoid(x):
    return 1.0 / (1.0 + jnp.exp(-x))


def _rms_fwd(x, w, n):
    r = lax.rsqrt(jnp.sum(x * x, axis=-1, keepdims=True) * (1.0 / n) + EPS)
    return x * r * w, r


def _rms_bwd(x, r, w, dy, n):
    dyw = dy * w
    dx = r * dyw - x * (r * r * r) * (jnp.sum(dyw * x, axis=-1, keepdims=True) * (1.0 / n))
    dw = jnp.sum(dy * (x * r), axis=0, keepdims=True)
    return dx, dw


def _rope_apply(x, cos, sina, sinb):
    return x * cos + pltpu.roll(x, 16, 1) * sina + pltpu.roll(x, HEAD_PAD - 16, 1) * sinb


def _rope_transpose(g, cos, sina, sinb):
    return g * cos + pltpu.roll(g * sina, HEAD_PAD - 16, 1) + pltpu.roll(g * sinb, 16, 1)


def _row_tile(lp):
    return 640 if lp % 640 == 0 else 128


def _ssm_tile(lp):
    return 320 if lp % 320 == 0 else 128


def _rows(tm, off_width):
    off, width = off_width
    return pl.BlockSpec((tm, width), lambda i: (i, off // width))


def _whole(shape, single=True):
    nd = len(shape)
    if single:
        return pl.BlockSpec(shape, lambda *_: (0,) * nd, pipeline_mode=pl.Buffered(1))
    return pl.BlockSpec(shape, lambda *_: (0,) * nd)


def _out_whole(shape):
    return _whole(shape, single=False)


def _pick_tile(rows, cap):
    best = 8
    for t in range(8, cap + 1, 8):
        if rows % t == 0:
            best = t
    return best


def _in_proj_fwd(h, pre_w, w_in_p):
    lp = h.shape[0]
    tm = _row_tile(lp)

    def body(h_ref, w_ref, win_ref, proj_ref):
        xn, _ = _rms_fwd(h_ref[...], w_ref[...], D_MODEL)
        proj_ref[...] = _dot(xn.astype(BF16), win_ref[...])

    return pl.pallas_call(
        body, name="in_proj_fwd", grid=(lp // tm,),
        in_specs=[_rows(tm, (0, D_MODEL)), _whole((1, D_MODEL)), _whole((D_MODEL, D_PROJ))],
        out_specs=_rows(tm, (0, D_PROJ)),
        out_shape=jax.ShapeDtypeStruct((lp, D_PROJ), F32),
        compiler_params=_cparams(("parallel",)),
    )(h, pre_w, w_in_p)


def _attn_prep_fwd(proj, q_norm_w, kv_norm_w, wq_p, wk_p, wv_p, wv_t, cos, sina, sinb, l_real):
    lp = proj.shape[0]
    tm = _row_tile(lp)

    def body(ql_ref, kvl_ref, kr_ref, qw_ref, kw_ref, wq_ref, wk_ref, wv_ref, wvt_ref, cos_ref, sa_ref, sb_ref,
             q_ref, k_ref, v_ref, vt_ref, qt_ref, kt_ref):
        cos_t, sa_t, sb_t = cos_ref[...], sa_ref[...], sb_ref[...]
        qn, _ = _rms_fwd(ql_ref[...], qw_ref[...], Q_LORA)
        kvn, _ = _rms_fwd(kvl_ref[...], kw_ref[...], KV_LORA)
        kvn_b = kvn.astype(BF16)
        qp = _dot(qn.astype(BF16), wq_ref[...])
        kp = _dot(kvn_b, wk_ref[...])
        v_ref[...] = _dot(kvn_b, wv_ref[...]).astype(BF16)
        ones_row = lax.broadcasted_iota(jnp.int32, (HEADS * VT_ROWS, 1), 0) % VT_ROWS == V_HEAD
        vt_ref[...] = jnp.where(ones_row, 1.0, _dot_nt(wvt_ref[...], kvn_b)).astype(BF16)
        lane = lax.broadcasted_iota(jnp.int32, (tm, HEAD_PAD), 1)
        row = lax.broadcasted_iota(jnp.int32, (tm, HEAD_PAD), 0) + pl.program_id(0) * tm
        q_one = jnp.where(lane == MASK_LANE, 1.0, 0.0)
        k_add = _rope_apply(kr_ref[...], cos_t, sa_t, sb_t) + jnp.where((lane == MASK_LANE) & (row >= l_real), NEG_BIG, 0.0)
        for hd in range(HEADS):
            blk = slice(hd * HEAD_PAD, (hd + 1) * HEAD_PAD)
            q_h = _rope_apply(qp[:, blk], cos_t, sa_t, sb_t) * SCALE2 + q_one
            k_h = kp[:, blk] + k_add
            q_ref[:, blk] = q_h.astype(BF16)
            k_ref[:, blk] = k_h.astype(BF16)
            qt_ref[hd * D_QK:(hd + 1) * D_QK, :] = q_h.T[:D_QK].astype(BF16)
            kt_ref[hd * D_QK:(hd + 1) * D_QK, :] = k_h.T[:D_QK].astype(BF16)

    tab = _rows(tm, (0, HEAD_PAD))
    out = jax.ShapeDtypeStruct((lp, D_EXP), BF16)
    out_t = jax.ShapeDtypeStruct((HEADS * D_QK, lp), BF16)
    cols_t = pl.BlockSpec((HEADS * D_QK, tm), lambda i: (0, i))
    return pl.pallas_call(
        body, name="attn_prep_fwd", grid=(lp // tm,),
        in_specs=[_rows(tm, P_QLAT), _rows(tm, P_KVLAT), _rows(tm, P_KROPE), _whole((1, Q_LORA)), _whole((1, KV_LORA)),
                  _whole((Q_LORA, D_EXP)), _whole((KV_LORA, D_EXP)), _whole((KV_LORA, D_EXP)),
                  _whole((HEADS * VT_ROWS, KV_LORA)), tab, tab, tab],
        out_specs=[_rows(tm, (0, D_EXP))] * 3 + [pl.BlockSpec((HEADS * VT_ROWS, tm), lambda i: (0, i)), cols_t, cols_t],
        out_shape=[out, out, out, jax.ShapeDtypeStruct((HEADS * VT_ROWS, lp), BF16), out_t, out_t],
        compiler_params=_cparams(("parallel",)),
    )(proj, proj, proj, q_norm_w, kv_norm_w, wq_p, wk_p, wv_p, wv_t, cos, sina, sinb)


def _flash_fwd(q, k, vt):
    lp = q.shape[0]
    tq = 1280 if lp % 1280 == 0 else 256
    tk = QBLK
    nk = lp // tk

    def body(q_ref, k_ref, vt_ref, o_ref, lse_ref, acc, m_s, s_a, s_b):
        acc[...] = jnp.zeros_like(acc)
        m_s[...] = jnp.full(m_s.shape, NEG_BIG, F32)
        blocks = [slice(c * QBLK, (c + 1) * QBLK) for c in range(tq // QBLK)]

        def scores(j, buf):
            kt = k_ref[pl.ds(pl.multiple_of(j * tk, tk), tk), :]
            for cols in blocks:
                buf[:, cols] = _dot_nt(kt, q_ref[cols, :])

        def consume(j, buf):
            vt_t = vt_ref[:, pl.ds(pl.multiple_of(j * tk, tk), tk)]
            m_old, acc_old = m_s[...], acc[...]
            s = [buf[:, cols] for cols in blocks]
            m_new = [jnp.maximum(m_old[:, cols], jnp.max(s_c, axis=0, keepdims=True)) for cols, s_c in zip(blocks, s)]
            p = [jnp.exp2(s_c - m_c) for s_c, m_c in zip(s, m_new)]
            pv = [_dot(vt_t, p_c.astype(BF16)) for p_c in p]
            m_new = jnp.concatenate(m_new, axis=1)
            alpha = jnp.exp2(m_old - m_new)
            acc[...] = alpha * acc_old + jnp.concatenate(pv, axis=1)
            m_s[...] = m_new

        n_pairs = (nk - 1) // 2
        scores(0, s_a)

        def pair(t, _):
            j = 2 * t
            scores(j + 1, s_b)
            consume(j, s_a)
            scores(j + 2, s_a)
            consume(j + 1, s_b)
            return 0

        lax.fori_loop(0, n_pairs, pair, 0, unroll=2 if n_pairs % 2 == 0 else 1)
        if nk - 2 * n_pairs == 2:
            scores(nk - 1, s_b)
            consume(nk - 2, s_a)
            consume(nk - 1, s_b)
        else:
            consume(nk - 1, s_a)
        l = acc[V_HEAD:V_HEAD + 1, :]
        o_t = acc[0:V_HEAD, :] / l
        o_ref[...] = jnp.concatenate([o_t, jnp.zeros_like(o_t)], axis=0).T
        lse_ref[...] = m_s[...] + jnp.log2(l)

    return pl.pallas_call(
        body, name="flash_fwd", grid=(HEADS, lp // tq),
        in_specs=[pl.BlockSpec((tq, HEAD_PAD), lambda hd, i: (i, hd)),
                  pl.BlockSpec((lp, HEAD_PAD), lambda hd, i: (0, hd)),
                  pl.BlockSpec((VT_ROWS, lp), lambda hd, i: (hd, 0))],
        out_specs=[pl.BlockSpec((tq, HEAD_PAD), lambda hd, i: (i, hd)),
                   pl.BlockSpec((None, 1, tq), lambda hd, i: (hd, 0, i))],
        out_shape=[jax.ShapeDtypeStruct((lp, D_EXP), F32), jax.ShapeDtypeStruct((HEADS, 1, lp), F32)],
        scratch_shapes=[pltpu.VMEM((VT_ROWS, tq), F32), pltpu.VMEM((1, tq), F32),
                        pltpu.VMEM((tk, tq), F32), pltpu.VMEM((tk, tq), F32)],
        compiler_params=_cparams(("parallel", "parallel")),
    )(q, k, vt)


def _unpermute_rows(val, scr, out_ref, seg):
    for c in range(val.shape[1] // 128):
        scr[c] = val[:, c * 128:(c + 1) * 128]
    for k in range(8):
        for c in range(val.shape[1] // 128):
            out_ref[k * seg:(k + 1) * seg, c * 128:(c + 1) * 128] = scr[c, pl.ds(k, seg, stride=8), :]


def _scan_rows(xr_ref, xi_ref, base, n_rows, coef_ref, carry_ref, reverse, tile_fn=None, acc_refs=(), halo=False):
    seg = n_rows // 8
    shifts = (7, 6, 4) if reverse else (1, 2, 4)
    row8 = lax.broadcasted_iota(jnp.int32, (8, SCAN_COLS), 0)
    edge, shift = (7, 7) if reverse else (0, 1)
    for cg in range(N_STATE // SCAN_COLS):
        cols = slice(cg * SCAN_COLS, (cg + 1) * SCAN_COLS)
        ar, ai = coef_ref[8, :, cols], coef_ref[9, :, cols]

        def rows_at(i):
            tau = (seg - 1 - i) if reverse else i
            return tau, pl.ds(pl.multiple_of(base + tau * 8, 8), 8)

        def local(i, carry, cols=cols, ar=ar, ai=ai):
            pr, pi_ = carry
            _, rows = rows_at(i)
            nr = ar * pr - ai * pi_ + xr_ref[rows, cols]
            ni = ar * pi_ + ai * pr + xi_ref[rows, cols]
            xr_ref[rows, cols] = nr
            xi_ref[rows, cols] = ni
            return nr, ni

        zero = jnp.zeros((8, SCAN_COLS), F32)
        fr, fi = lax.fori_loop(0, seg, local, (zero, zero), unroll=SCAN_UNROLL)
        co = [coef_ref[k, :, cols] for k in range(8)]
        for lvl in range(3):
            pr, pi_ = co[2 * lvl], co[2 * lvl + 1]
            sr = pltpu.roll(fr, shifts[lvl], 0)
            si = pltpu.roll(fi, shifts[lvl], 0)
            fr, fi = fr + pr * sr - pi_ * si, fi + pr * si + pi_ * sr
        cr, ci = carry_ref[0:1, cols], carry_ref[1:2, cols]
        fr, fi = fr + co[6] * cr - co[7] * ci, fi + co[6] * ci + co[7] * cr
        carry_ref[0:1, cols] = fr[0:1] if reverse else fr[7:8]
        carry_ref[1:2, cols] = fi[0:1] if reverse else fi[7:8]
        in_r = jnp.where(row8 == edge, cr, pltpu.roll(fr, shift, 0))
        in_i = jnp.where(row8 == edge, ci, pltpu.roll(fi, shift, 0))
        if halo:
            rows = pl.ds(base + n_rows, 8) if reverse else pl.ds(base - 8, 8)
            xr_ref[rows, cols] = in_r
            xi_ref[rows, cols] = in_i

        def fix(i, carry, cols=cols, ar=ar, ai=ai):
            c_r, c_i = carry[0], carry[1]
            tau, rows = rows_at(i)
            nr = xr_ref[rows, cols] + c_r
            ni = xi_ref[rows, cols] + c_i
            xr_ref[rows, cols] = nr
            xi_ref[rows, cols] = ni
            accs = carry[2:]
            if tile_fn is not None:
                accs = tuple(a + d for a, d in zip(accs, tile_fn(tau, cols, nr, ni)))
            return (ar * c_r - ai * c_i, ar * c_i + ai * c_r) + accs

        init = (ar * in_r - ai * in_i, ar * in_i + ai * in_r) + tuple(a[:, cols] for a in acc_refs)
        out = lax.fori_loop(0, seg, fix, init, unroll=SCAN_UNROLL)
        for a, val in zip(acc_refs, out[2:]):
            a[:, cols] = val


def _ssm_fwd(proj, perm, coef, b_re, b_im, c_re, c_im_neg, reverse):
    lp = proj.shape[0]
    t = _ssm_tile(lp)
    n = lp // t
    order = (lambda i: n - 1 - i) if reverse else (lambda i: i)

    def body(u_ref, pm_ref, coef_ref, bre_ref, bim_ref, cre_ref, cim_ref, y_ref, st_ref, xr, xi, carry, stage):
        @pl.when(pl.program_id(0) == 0)
        def _():
            carry[...] = jnp.zeros_like(carry)

        st_ref[...] = carry[0:2, :]
        ub = _dot(pm_ref[...], u_ref[...].astype(BF16)).astype(BF16)
        for j in range(SSM_BLOCKS):
            ch, stt = slice(j * BLK_CH, (j + 1) * BLK_CH), slice(j * BLK_ST, (j + 1) * BLK_ST)
            xr[:, stt] = _dot(ub[:, ch], bre_ref[j])
            xi[:, stt] = _dot(ub[:, ch], bim_ref[j])
        _scan_rows(xr, xi, 0, t, coef_ref, carry, reverse)
        y = jnp.concatenate(
            [_dot(xr[:, j * BLK_ST:(j + 1) * BLK_ST].astype(BF16), cre_ref[j])
             + _dot(xi[:, j * BLK_ST:(j + 1) * BLK_ST].astype(BF16), cim_ref[j]) for j in range(SSM_BLOCKS)], axis=1)
        _unpermute_rows(y, stage, y_ref, t // 8)

    wb, wc = _whole((SSM_BLOCKS, BLK_CH, BLK_ST)), _whole((SSM_BLOCKS, BLK_ST, BLK_CH))
    return pl.pallas_call(
        body, name="ssm_fwd_rev" if reverse else "ssm_fwd", grid=(n,),
        in_specs=[pl.BlockSpec((t, D_SSM), lambda i: (order(i), P_U[0] // D_SSM)), _whole((t, t)), _whole((10, 8, N_STATE)),
                  wb, wb, wc, wc],
        out_specs=[pl.BlockSpec((t, D_SSM), lambda i: (order(i), 0)),
                   pl.BlockSpec((None, 2, N_STATE), lambda i: (order(i), 0, 0))],
        out_shape=[jax.ShapeDtypeStruct((lp, D_SSM), F32), jax.ShapeDtypeStruct((n, 2, N_STATE), F32)],
        scratch_shapes=[pltpu.VMEM((t, N_STATE), F32), pltpu.VMEM((t, N_STATE), F32), pltpu.VMEM((8, N_STATE), F32),
                        pltpu.VMEM((D_SSM // 128, t, 128), F32)],
        compiler_params=_cparams(("arbitrary",)),
    )(proj, perm, coef, b_re, b_im, c_re, c_im_neg)


GELU_C0 = math.sqrt(2.0 / math.pi)
GELU_C1 = 0.044715


def _mid(h, tgt, o_exp, proj, y0, y1, ssm_d, w_glu, w_glu_t, b_glu, ssm_norm_w, attn_norm_w_e, w_out_a, w_out_s,
         w_out_a_t, w_out_s_t, post_w, l_real):
    lp = h.shape[0]
    tm = 128

    def body(h_ref, tga_ref, tgb_ref, o_ref, ga_ref, u_ref, sg_ref, y0_ref, y1_ref, d_ref, wg_ref, wgt_ref, bg_ref, ws_ref,
             wa_ref, woa_ref, wos_ref, woat_ref, wost_ref, pw_ref,
             do_ref, dot_ref, delta_ref, dga_ref, dyp_ref, dsg_ref, dres_ref, dwoa_ref, dwos_ref, dwg_ref, vec_ref):
        @pl.when(pl.program_id(0) == 0)
        def _():
            dwoa_ref[...] = jnp.zeros_like(dwoa_ref)
            dwos_ref[...] = jnp.zeros_like(dwos_ref)
            dwg_ref[...] = jnp.zeros_like(dwg_ref)
            vec_ref[...] = jnp.zeros_like(vec_ref)

        u = u_ref[...]
        ypre = y0_ref[...] + y1_ref[...] + d_ref[...] * u
        th = jnp.tanh(GELU_C0 * (ypre + GELU_C1 * ypre * ypre * ypre))
        gel = 0.5 * ypre * (1.0 + th)
        gel_b = gel.astype(BF16)
        glu = _dot(gel_b, wg_ref[...]) + bg_ref[...]
        g1, g2 = glu[:, :D_SSM], glu[:, D_SSM:]
        sig2 = _sigmoid(g2)
        z = g1 * sig2
        sg = sg_ref[...]
        sgs = _sigmoid(sg)
        sil_s = sg * sgs
        s = z * sil_s
        ys, r_s = _rms_fwd(s, ws_ref[...], D_SSM)

        o = o_ref[...]
        ga = ga_ref[...]
        gas = _sigmoid(ga)
        sil_a = ga * gas
        a = o * sil_a
        ya, r_a = _rms_fwd(a, wa_ref[...], D_ATTN)

        ya_b, ys_b = ya.astype(BF16), ys.astype(BF16)
        y = _dot(ya_b, woa_ref[...]) + _dot(ys_b, wos_ref[...])
        yn, r_y = _rms_fwd(y, pw_ref[...], D_MODEL)
        row = lax.broadcasted_iota(jnp.int32, (tm, 1), 0) + pl.program_id(0) * tm
        valid = (row >= N_META) & (row < l_real)
        tgt = jnp.concatenate([tga_ref[tm - N_META:, :], tgb_ref[:tm - N_META, :]], axis=0)
        err = jnp.where(valid, h_ref[...] + yn - tgt, 0.0)
        loss = 0.5 * jnp.sum(jnp.sum(err * err, axis=-1, keepdims=True), axis=0, keepdims=True) * (1.0 / D_MODEL)
        dout = err * (1.0 / D_MODEL)
        dres_ref[...] = dout

        dy, d_pw = _rms_bwd(y, r_y, pw_ref[...], dout, D_MODEL)
        dy_b = dy.astype(BF16)
        dya = _dot(dy_b, woat_ref[...])
        dys = _dot(dy_b, wost_ref[...])
        dwoa_ref[...] += _dot_tn(ya_b, dy_b)
        dwos_ref[...] += _dot_tn(ys_b, dy_b)

        da, d_wa = _rms_bwd(a, r_a, wa_ref[...], dya, D_ATTN)
        d_o = da * sil_a
        dga_ref[...] = da * o * (gas * (1.0 + ga * (1.0 - gas)))
        do_ref[...] = d_o.astype(BF16)
        for hd in range(HEADS):
            dot_ref[hd * V_HEAD:(hd + 1) * V_HEAD, :] = d_o[:, hd * HEAD_PAD:(hd + 1) * HEAD_PAD].T[:V_HEAD].astype(BF16)
        prod = d_o * o
        lane8 = lax.broadcasted_iota(jnp.int32, (tm, HEADS), 1)
        delta = jnp.zeros((tm, HEADS), F32)
        for hd in range(HEADS):
            delta = jnp.where(lane8 == hd, jnp.sum(prod[:, hd * HEAD_PAD:(hd + 1) * HEAD_PAD], axis=-1, keepdims=True), delta)
        delta_ref[...] = delta

        ds, d_ws = _rms_bwd(s, r_s, ws_ref[...], dys, D_SSM)
        dz = ds * sil_s
        dsg_ref[...] = ds * z * (sgs * (1.0 + sg * (1.0 - sgs)))
        dglu = jnp.concatenate([dz * sig2, dz * g1 * sig2 * (1.0 - sig2)], axis=-1)
        dglu_b = dglu.astype(BF16)
        dwg_ref[...] += _dot_tn(gel_b, dglu_b)
        dgel = _dot(dglu_b, wgt_ref[...])
        dgelu = 0.5 * (1.0 + th) + 0.5 * ypre * (1.0 - th * th) * (GELU_C0 * (1.0 + 3.0 * GELU_C1 * ypre * ypre))
        dyp = dgel * dgelu
        dyp_ref[...] = dyp

        vec_ref[0:1, :] += d_pw
        vec_ref[1:2, :] += d_wa
        vec_ref[2:3, 0:D_SSM] += d_ws
        vec_ref[3:4, 0:D_SSM] += jnp.sum(dyp * u, axis=0, keepdims=True)
        vec_ref[4:5, :] += jnp.sum(dglu, axis=0, keepdims=True)
        vec_ref[5:6, :] += jnp.broadcast_to(loss, (1, D_MODEL))

    full = lambda off: _rows(tm, (off, D_MODEL))
    half = lambda off: _rows(tm, (off, D_SSM))
    last = tgt.shape[0] // tm - 1
    tg_a = pl.BlockSpec((tm, D_MODEL), lambda i: (jnp.clip(i - 1, 0, last), 0))
    tg_b = pl.BlockSpec((tm, D_MODEL), lambda i: (jnp.minimum(i, last), 0))
    return pl.pallas_call(
        body, name="mid", grid=(lp // tm,),
        in_specs=[full(0), tg_a, tg_b, full(0), _rows(tm, P_GATE_A), _rows(tm, P_U), _rows(tm, P_GATE_S), half(0), half(0),
                  _whole((1, D_SSM)), _whole((D_SSM, 2 * D_SSM)), _whole((2 * D_SSM, D_SSM)), _whole((1, 2 * D_SSM)),
                  _whole((1, D_SSM)), _whole((1, D_EXP)), _whole((D_EXP, D_MODEL)), _whole((D_SSM, D_MODEL)),
                  _whole((D_MODEL, D_EXP)), _whole((D_MODEL, D_SSM)), _whole((1, D_MODEL))],
        out_specs=[full(0), pl.BlockSpec((D_ATTN, tm), lambda i: (0, i)), _rows(tm, (0, HEADS)), full(0), half(0), half(0), full(0),
                   _out_whole((D_EXP, D_MODEL)), _out_whole((D_SSM, D_MODEL)), _out_whole((D_SSM, 2 * D_SSM)),
                   _out_whole((8, D_MODEL))],
        out_shape=[jax.ShapeDtypeStruct((lp, D_EXP), BF16), jax.ShapeDtypeStruct((D_ATTN, lp), BF16),
                   jax.ShapeDtypeStruct((lp, HEADS), F32),
                   jax.ShapeDtypeStruct((lp, D_EXP), F32), jax.ShapeDtypeStruct((lp, D_SSM), F32),
                   jax.ShapeDtypeStruct((lp, D_SSM), F32), jax.ShapeDtypeStruct((lp, D_MODEL), F32),
                   jax.ShapeDtypeStruct((D_EXP, D_MODEL), F32), jax.ShapeDtypeStruct((D_SSM, D_MODEL), F32),
                   jax.ShapeDtypeStruct((D_SSM, 2 * D_SSM), F32), jax.ShapeDtypeStruct((8, D_MODEL), F32)],
        compiler_params=_cparams(("arbitrary",)),
    )(h, tgt, tgt, o_exp, proj, proj, proj, y0, y1, ssm_d, w_glu, w_glu_t, b_glu, ssm_norm_w, attn_norm_w_e, w_out_a, w_out_s,
      w_out_a_t, w_out_s_t, post_w)


def _ssm_bwd(proj, dyp, states, perm, coef, coef_adj, b_re, b_im, b_re_t, b_im_t, c_re_t, c_im_neg_t, reverse):
    lp = proj.shape[0]
    t = _ssm_tile(lp)
    n = lp // t
    order = (lambda i: i) if reverse else (lambda i: n - 1 - i)

    def body(u_ref, dy_ref, st_ref, pm_ref, coef_ref, coefa_ref, bre_ref, bim_ref, bret_ref, bimt_ref, cret_ref, cimt_ref,
             du_ref, dbre_ref, dbim_ref, dcre_ref, dcim_ref, da_ref, xr, xi, gr, gi, carry_x, carry_g, stage):
        @pl.when(pl.program_id(0) == 0)
        def _():
            carry_g[...] = jnp.zeros_like(carry_g)
            carry_x[...] = jnp.zeros_like(carry_x)
            dbre_ref[...] = jnp.zeros_like(dbre_ref)
            dbim_ref[...] = jnp.zeros_like(dbim_ref)
            dcre_ref[...] = jnp.zeros_like(dcre_ref)
            dcim_ref[...] = jnp.zeros_like(dcim_ref)
            da_ref[...] = jnp.zeros_like(da_ref)
            for halo in (slice(0, 8), slice(t + 8, t + 16)):
                xr[halo, :] = jnp.zeros((8, N_STATE), F32)
                xi[halo, :] = jnp.zeros((8, N_STATE), F32)

        ub = _dot(pm_ref[...], u_ref[...].astype(BF16)).astype(BF16)
        dyb = _dot(pm_ref[...], dy_ref[...].astype(BF16)).astype(BF16)
        carry_x[0:2, :] = st_ref[...]
        blocks = [(slice(j * BLK_CH, (j + 1) * BLK_CH), slice(j * BLK_ST, (j + 1) * BLK_ST)) for j in range(SSM_BLOCKS)]
        for j, (ch, stt) in enumerate(blocks):
            xr[8:t + 8, stt] = _dot(ub[:, ch], bre_ref[j])
            xi[8:t + 8, stt] = _dot(ub[:, ch], bim_ref[j])
            gr[:, stt] = _dot(dyb[:, ch], cret_ref[j])
            gi[:, stt] = _dot(dyb[:, ch], cimt_ref[j])
        _scan_rows(xr, xi, 8, t, coef_ref, carry_x, reverse, halo=True)

        def tile_fn(tau, cols, g_re, g_im):
            nb = pl.ds(pl.multiple_of((tau + 2) * 8 if reverse else tau * 8, 8), 8)
            xn_r, xn_i = xr[nb, cols], xi[nb, cols]
            return g_re * xn_r + g_im * xn_i, g_im * xn_r - g_re * xn_i

        _scan_rows(gr, gi, 0, t, coefa_ref, carry_g, not reverse, tile_fn=tile_fn, acc_refs=(da_ref.at[0], da_ref.at[1]))

        du = []
        for j, (ch, stt) in enumerate(blocks):
            g_re_b, g_im_b = gr[:, stt].astype(BF16), gi[:, stt].astype(BF16)
            du.append(_dot(g_re_b, bret_ref[j]) + _dot(g_im_b, bimt_ref[j]))
            dbre_ref[j] += _dot_tn(ub[:, ch], g_re_b)
            dbim_ref[j] += _dot_tn(ub[:, ch], g_im_b)
            dcre_ref[j] += _dot_tn(dyb[:, ch], xr[8:t + 8, stt].astype(BF16))
            dcim_ref[j] -= _dot_tn(dyb[:, ch], xi[8:t + 8, stt].astype(BF16))
        _unpermute_rows(jnp.concatenate(du, axis=1), stage, du_ref, t // 8)

    dense = jax.ShapeDtypeStruct((SSM_BLOCKS, BLK_CH, BLK_ST), F32)
    wb, wc = _whole((SSM_BLOCKS, BLK_CH, BLK_ST)), _whole((SSM_BLOCKS, BLK_ST, BLK_CH))
    acc = _out_whole((SSM_BLOCKS, BLK_CH, BLK_ST))
    return pl.pallas_call(
        body, name="ssm_bwd_rev" if reverse else "ssm_bwd", grid=(n,),
        in_specs=[pl.BlockSpec((t, D_SSM), lambda i: (order(i), P_U[0] // D_SSM)),
                  pl.BlockSpec((t, D_SSM), lambda i: (order(i), 0)),
                  pl.BlockSpec((None, 2, N_STATE), lambda i: (order(i), 0, 0)), _whole((t, t)),
                  _whole((10, 8, N_STATE)), _whole((10, 8, N_STATE)), wb, wb, wc, wc, wb, wb],
        out_specs=[pl.BlockSpec((t, D_SSM), lambda i: (order(i), 0)), acc, acc, acc, acc, _out_whole((2, 8, N_STATE))],
        out_shape=[jax.ShapeDtypeStruct((lp, D_SSM), F32), dense, dense, dense, dense,
                   jax.ShapeDtypeStruct((2, 8, N_STATE), F32)],
        scratch_shapes=[pltpu.VMEM((t + 16, N_STATE), F32), pltpu.VMEM((t + 16, N_STATE), F32),
                        pltpu.VMEM((t, N_STATE), F32), pltpu.VMEM((t, N_STATE), F32),
                        pltpu.VMEM((8, N_STATE), F32), pltpu.VMEM((8, N_STATE), F32),
                        pltpu.VMEM((D_SSM // 128, t, 128), F32)],
        compiler_params=_cparams(("arbitrary",)),
    )(proj, dyp, states, perm, coef, coef_adj, b_re, b_im, b_re_t, b_im_t, c_re_t, c_im_neg_t)


def _flash_bwd(q, k, v, d_o, q_t, k_t, do_t, lse_row, delta_row):
    lp = q.shape[0]
    tq = 1280 if lp % 1280 == 0 else 256
    tk = QBLK
    nk = lp // tk
    d_qk = QK_NOPE + QK_ROPE

    def body(q_ref, do_ref, qt_ref, dot_ref, lse_ref, delta_ref, k_ref, v_ref, kt_ref, dq_ref, dk_ref, dv_ref, dq_acc):
        @pl.when(pl.program_id(1) == 0)
        def _():
            dk_ref[...] = jnp.zeros_like(dk_ref)
            dv_ref[...] = jnp.zeros_like(dv_ref)

        dq_acc[...] = jnp.zeros_like(dq_acc)
        lse, delta = lse_ref[...], delta_ref[...]
        q_cols, do_cols = qt_ref[...], dot_ref[...]
        blocks = [slice(c * QBLK, (c + 1) * QBLK) for c in range(tq // QBLK)]

        def step(j, _):
            ks = pl.multiple_of(j * tk, tk)
            k_rows, v_rows = k_ref[pl.ds(ks, tk), :], v_ref[pl.ds(ks, tk), :]
            dq_old = dq_acc[...]
            st = [_dot_nt(k_rows, q_ref[cols, :]) for cols in blocks]
            dpt = [_dot_nt(v_rows, do_ref[cols, :]) for cols in blocks]
            pt = [jnp.exp2(s_c - lse[:, cols]) for s_c, cols in zip(st, blocks)]
            dst = [p_c * (dp_c - delta[:, cols]) for p_c, dp_c, cols in zip(pt, dpt, blocks)]
            pt_b = jnp.concatenate([p_c.astype(BF16) for p_c in pt], axis=1)
            dst_b = jnp.concatenate([d_c.astype(BF16) for d_c in dst], axis=1)
            dv_ref[:, pl.ds(ks, tk)] += _dot_nt(do_cols, pt_b)
            dk_ref[:, pl.ds(ks, tk)] += _dot_nt(q_cols, dst_b) * (1.0 / LOG2E)
            dq_acc[...] = dq_old + _dot(kt_ref[:, pl.ds(ks, tk)], dst_b)
            return 0

        lax.fori_loop(0, nk, step, 0, unroll=5 if nk % 5 == 0 else 1)
        dq_ref[...] = jnp.concatenate([dq_acc[...], jnp.zeros((HEAD_PAD - d_qk, tq), F32)], axis=0).T

    tile = pl.BlockSpec((tq, HEAD_PAD), lambda hd, i: (i, hd))
    head = pl.BlockSpec((lp, HEAD_PAD), lambda hd, i: (0, hd))
    rowv = pl.BlockSpec((None, 1, tq), lambda hd, i: (hd, 0, i))
    return pl.pallas_call(
        body, name="flash_bwd", grid=(HEADS, lp // tq),
        in_specs=[tile, tile, pl.BlockSpec((d_qk, tq), lambda hd, i: (hd, i)), pl.BlockSpec((V_HEAD, tq), lambda hd, i: (hd, i)),
                  rowv, rowv, head, head, pl.BlockSpec((d_qk, lp), lambda hd, i: (hd, 0))],
        out_specs=[tile, pl.BlockSpec((d_qk, lp), lambda hd, i: (hd, 0)), pl.BlockSpec((V_HEAD, lp), lambda hd, i: (hd, 0))],
        out_shape=[jax.ShapeDtypeStruct((lp, D_EXP), F32), jax.ShapeDtypeStruct((HEADS * d_qk, lp), F32),
                   jax.ShapeDtypeStruct((HEADS * V_HEAD, lp), F32)],
        scratch_shapes=[pltpu.VMEM((d_qk, tq), F32)],
        compiler_params=_cparams(("parallel", "arbitrary")),
    )(q, d_o, q_t, do_t, lse_row, delta_row, k, v, k_t)


def _attn_prep_bwd(dq, dk_t, dv_t, proj, q_norm_w, kv_norm_w, wq_pt, wk_pt, wv_pt, cos, sina, sinb):
    lp = proj.shape[0]
    tm = _row_tile(lp)

    def body(dq_ref, dk_ref, dv_ref, ql_ref, kvl_ref, qw_ref, kw_ref, wqt_ref, wkt_ref, wvt_ref, cos_ref, sa_ref, sb_ref,
             dql_ref, dkvl_ref, dkr_ref, dwq_ref, dwk_ref, dwv_ref, vec_ref):
        @pl.when(pl.program_id(0) == 0)
        def _():
            dwq_ref[...] = jnp.zeros_like(dwq_ref)
            dwk_ref[...] = jnp.zeros_like(dwk_ref)
            dwv_ref[...] = jnp.zeros_like(dwv_ref)
            vec_ref[...] = jnp.zeros_like(vec_ref)

        cos_t, sa_t, sb_t = cos_ref[...], sa_ref[...], sb_ref[...]

        def head_rows(t_ref, per):
            pad = jnp.zeros((HEAD_PAD - per, tm), F32)
            return jnp.concatenate(
                [jnp.concatenate([t_ref[hd * per:(hd + 1) * per, :], pad], axis=0).T for hd in range(HEADS)], axis=-1)

        dkp = head_rows(dk_ref, D_QK)
        dqp = jnp.concatenate(
            [_rope_transpose(dq_ref[:, hd * HEAD_PAD:(hd + 1) * HEAD_PAD] * SCALE, cos_t, sa_t, sb_t) for hd in range(HEADS)],
            axis=-1)
        dkr = dkp[:, 0:HEAD_PAD]
        for hd in range(1, HEADS):
            dkr = dkr + dkp[:, hd * HEAD_PAD:(hd + 1) * HEAD_PAD]
        dkr_ref[...] = _rope_transpose(dkr, cos_t, sa_t, sb_t)

        qn, r_q = _rms_fwd(ql_ref[...], qw_ref[...], Q_LORA)
        kvn, r_kv = _rms_fwd(kvl_ref[...], kw_ref[...], KV_LORA)
        dqp_b, dkp_b, dv_b = dqp.astype(BF16), dkp.astype(BF16), head_rows(dv_ref, V_HEAD).astype(BF16)
        dqn = _dot(dqp_b, wqt_ref[...])
        dkvn = _dot(dkp_b, wkt_ref[...]) + _dot(dv_b, wvt_ref[...])
        dwq_ref[...] += _dot_tn(qn.astype(BF16), dqp_b)
        dwk_ref[...] += _dot_tn(kvn.astype(BF16), dkp_b)
        dwv_ref[...] += _dot_tn(kvn.astype(BF16), dv_b)
        dql, d_qw = _rms_bwd(ql_ref[...], r_q, qw_ref[...], dqn, Q_LORA)
        dkvl, d_kw = _rms_bwd(kvl_ref[...], r_kv, kw_ref[...], dkvn, KV_LORA)
        dql_ref[...] = dql
        dkvl_ref[...] = dkvl
        vec_ref[0:1, :] += d_qw
        vec_ref[1:2, 0:KV_LORA] += d_kw

    tab = _rows(tm, (0, HEAD_PAD))
    full = _rows(tm, (0, D_EXP))
    return pl.pallas_call(
        body, name="attn_prep_bwd", grid=(lp // tm,),
        in_specs=[full, pl.BlockSpec((HEADS * D_QK, tm), lambda i: (0, i)), pl.BlockSpec((D_ATTN, tm), lambda i: (0, i)),
                  _rows(tm, P_QLAT), _rows(tm, P_KVLAT), _whole((1, Q_LORA)), _whole((1, KV_LORA)),
                  _whole((D_EXP, Q_LORA)), _whole((D_EXP, KV_LORA)), _whole((D_EXP, KV_LORA)), tab, tab, tab],
        out_specs=[_rows(tm, (0, Q_LORA)), _rows(tm, (0, KV_LORA)), _rows(tm, (0, HEAD_PAD)),
                   _out_whole((Q_LORA, D_EXP)), _out_whole((KV_LORA, D_EXP)), _out_whole((KV_LORA, D_EXP)),
                   _out_whole((8, Q_LORA))],
        out_shape=[jax.ShapeDtypeStruct((lp, Q_LORA), F32), jax.ShapeDtypeStruct((lp, KV_LORA), F32),
                   jax.ShapeDtypeStruct((lp, HEAD_PAD), F32), jax.ShapeDtypeStruct((Q_LORA, D_EXP), F32),
                   jax.ShapeDtypeStruct((KV_LORA, D_EXP), F32), jax.ShapeDtypeStruct((KV_LORA, D_EXP), F32),
                   jax.ShapeDtypeStruct((8, Q_LORA), F32)],
        compiler_params=_cparams(("arbitrary",)),
    )(dq, dk_t, dv_t, proj, proj, q_norm_w, kv_norm_w, wq_pt, wk_pt, wv_pt, cos, sina, sinb)


def _in_proj_bwd(h, pre_w, dres, dga, du0, du1, dyp, ssm_d, dsg, dql, dkvl, dkr, w_in_pt):
    lp = h.shape[0]
    tm = 128
    pieces = (P_GATE_A, P_U, P_GATE_S, P_QLAT, P_KVLAT, P_KROPE)

    def body(h_ref, w_ref, dres_ref, dga_ref, du0_ref, du1_ref, dyp_ref, d_ref, dsg_ref, dql_ref, dkvl_ref, dkr_ref, wt_ref,
             dh_ref, dw_ref, vec_ref):
        @pl.when(pl.program_id(0) == 0)
        def _():
            dw_ref[...] = jnp.zeros_like(dw_ref)
            vec_ref[...] = jnp.zeros_like(vec_ref)

        hv = h_ref[...]
        xn, r = _rms_fwd(hv, w_ref[...], D_MODEL)
        xn_b = xn.astype(BF16)
        du = du0_ref[...] + du1_ref[...] + dyp_ref[...] * d_ref[...]
        grads = (dga_ref[...], du, dsg_ref[...], dql_ref[...], dkvl_ref[...], dkr_ref[...])
        dxn = jnp.zeros((tm, D_MODEL), F32)
        for (off, width), g in zip(pieces, grads):
            g_b = g.astype(BF16)
            dxn = dxn + _dot(g_b, wt_ref[off:off + width, :])
            dw_ref[:, off:off + width] += _dot_tn(xn_b, g_b)
        dx, d_w = _rms_bwd(hv, r, w_ref[...], dxn, D_MODEL)
        dh_ref[...] = dres_ref[...] + dx
        vec_ref[0:1, :] += d_w

    full = _rows(tm, (0, D_MODEL))
    half = _rows(tm, (0, D_SSM))
    return pl.pallas_call(
        body, name="in_proj_bwd", grid=(lp // tm,),
        in_specs=[full, _whole((1, D_MODEL)), full, full, half, half, half, _whole((1, D_SSM)), half,
                  _rows(tm, (0, Q_LORA)), _rows(tm, (0, KV_LORA)), _rows(tm, (0, HEAD_PAD)), _whole((D_PROJ, D_MODEL))],
        out_specs=[full, _out_whole((D_MODEL, D_PROJ)), _out_whole((8, D_MODEL))],
        out_shape=[jax.ShapeDtypeStruct((lp, D_MODEL), F32), jax.ShapeDtypeStruct((D_MODEL, D_PROJ), F32),
                   jax.ShapeDtypeStruct((8, D_MODEL), F32)],
        compiler_params=_cparams(("arbitrary",)),
    )(h, pre_w, dres, dga, du0, du1, dyp, ssm_d, dsg, dql, dkvl, dkr, w_in_pt)


def _other_chips(x, y):
    return [(1 - x, y), (x, 1 - y), (1 - x, 1 - y)]


def _gather_weights(w_bf16, meta):
    any_spec = pl.BlockSpec(memory_space=pl.ANY)
    halves = (w_bf16.shape[0] // 2, meta.shape[0] // 2)

    def body(w_ref, m_ref, wout_ref, mout_ref, send_sems, recv_sems, local_sems):
        x, y, c = lax.axis_index("x"), lax.axis_index("y"), lax.axis_index("c")
        me, sibling = 2 * x + y, (x, y, 1 - c)
        srcs, dsts = (w_ref, m_ref), (wout_ref, mout_ref)

        def half(n, cc):
            return pl.ds(pl.multiple_of(cc * halves[n], 8), halves[n])

        def copy(n, sem, src, chip, cc, to):
            return pltpu.make_async_remote_copy(src_ref=src, dst_ref=dsts[n].at[chip, half(n, cc)], send_sem=send_sems.at[sem],
                                                recv_sem=recv_sems.at[sem], device_id=to, device_id_type=MESH)

        own = [pltpu.make_async_copy(srcs[n], dsts[n].at[me], local_sems.at[n]) for n in range(2)]
        for cp in own:
            cp.start()
        chips = _other_chips(x, y)
        first = [copy(n, 2 * j + n, srcs[n].at[half(n, c)], me, c, (tx, ty, c)) for j, (tx, ty) in enumerate(chips) for n in range(2)]
        for cp in first:
            cp.start()
        passed = []
        for j, (tx, ty) in enumerate(chips):
            for n in range(2):
                landed = dsts[n].at[2 * tx + ty, half(n, c)]
                copy(n, 2 * j + n, landed, 2 * tx + ty, c, (tx, ty, c)).wait_recv()
                passed.append(copy(n, 6 + 2 * j + n, landed, 2 * tx + ty, c, sibling))
                passed[-1].start()
        for j, (tx, ty) in enumerate(chips):
            for n in range(2):
                copy(n, 6 + 2 * j + n, dsts[n].at[2 * tx + ty, half(n, 1 - c)], 2 * tx + ty, 1 - c, sibling).wait_recv()
        for cp in first + passed:
            cp.wait_send()
        for cp in own:
            cp.wait()

    return pl.pallas_call(
        body, name="gather_weights",
        in_specs=[any_spec, any_spec], out_specs=[any_spec, any_spec],
        out_shape=[jax.ShapeDtypeStruct((4,) + w_bf16.shape, w_bf16.dtype), jax.ShapeDtypeStruct((4,) + meta.shape, meta.dtype)],
        scratch_shapes=[pltpu.SemaphoreType.DMA((12,)), pltpu.SemaphoreType.DMA((12,)), pltpu.SemaphoreType.DMA((2,))],
    )(w_bf16, meta)


def _swap_sibling(g):
    any_spec = pl.BlockSpec(memory_space=pl.ANY)

    def body(g_ref, out_ref, send_sem, recv_sem):
        x, y, c = lax.axis_index("x"), lax.axis_index("y"), lax.axis_index("c")
        cp = pltpu.make_async_remote_copy(src_ref=g_ref, dst_ref=out_ref, send_sem=send_sem, recv_sem=recv_sem,
                                          device_id=(x, y, 1 - c), device_id_type=MESH)
        cp.start()
        cp.wait()

    return pl.pallas_call(
        body, name="swap_sibling", in_specs=[any_spec], out_specs=any_spec,
        out_shape=jax.ShapeDtypeStruct(g.shape, g.dtype),
        scratch_shapes=[pltpu.SemaphoreType.DMA(()), pltpu.SemaphoreType.DMA(())],
    )(g)


def _pair_sum(a, b):
    rows = a.shape[0]
    tm = _pick_tile(rows, 1024)

    def body(a_ref, b_ref, o_ref):
        o_ref[...] = a_ref[...] + b_ref[...]

    spec = pl.BlockSpec((tm, 1024), lambda i: (i, 0))
    return pl.pallas_call(body, name="pair_sum", grid=(rows // tm,), in_specs=[spec, spec], out_specs=spec,
                          out_shape=jax.ShapeDtypeStruct(a.shape, F32), compiler_params=_cparams(("parallel",)))(a, b)


def _scatter_chips(s, rs, rsm):
    any_spec = pl.BlockSpec(memory_space=pl.ANY)
    lens = (rs // 2, rsm // 2)

    def body(s_ref, out_ref, send_sems, recv_sems, local_sems):
        x, y, c = lax.axis_index("x"), lax.axis_index("y"), lax.axis_index("c")
        me, sibling = 2 * x + y, (x, y, 1 - c)

        def src_rows(n, target, cc):
            start = (target * rs if n == 0 else 4 * rs) + cc * lens[n]
            return s_ref.at[pl.ds(pl.multiple_of(start, 8), lens[n])]

        def dst_rows(n, cc):
            return pl.ds(pl.multiple_of((0 if n == 0 else rs) + cc * lens[n], 8), lens[n])

        def copy(n, sem, src, chip, cc, to):
            return pltpu.make_async_remote_copy(src_ref=src, dst_ref=out_ref.at[chip, dst_rows(n, cc)], send_sem=send_sems.at[sem],
                                                recv_sem=recv_sems.at[sem], device_id=to, device_id_type=MESH)

        own = [pltpu.make_async_copy(s_ref.at[pl.ds(pl.multiple_of(me * rs, 8), rs)], out_ref.at[me, pl.ds(0, rs)], local_sems.at[0]),
               pltpu.make_async_copy(s_ref.at[pl.ds(4 * rs, rsm)], out_ref.at[me, pl.ds(rs, rsm)], local_sems.at[1])]
        for cp in own:
            cp.start()
        chips = _other_chips(x, y)
        first = [copy(n, 2 * j + n, src_rows(n, 2 * tx + ty, c), me, c, (tx, ty, c))
                 for j, (tx, ty) in enumerate(chips) for n in range(2)]
        for cp in first:
            cp.start()
        passed = []
        for j, (tx, ty) in enumerate(chips):
            for n in range(2):
                landed = out_ref.at[2 * tx + ty, dst_rows(n, c)]
                copy(n, 2 * j + n, landed, 2 * tx + ty, c, (tx, ty, c)).wait_recv()
                passed.append(copy(n, 6 + 2 * j + n, landed, 2 * tx + ty, c, sibling))
                passed[-1].start()
        for j, (tx, ty) in enumerate(chips):
            for n in range(2):
                copy(n, 6 + 2 * j + n, out_ref.at[2 * tx + ty, dst_rows(n, 1 - c)], 2 * tx + ty, 1 - c, sibling).wait_recv()
        for cp in first + passed:
            cp.wait_send()
        for cp in own:
            cp.wait()

    return pl.pallas_call(
        body, name="scatter_chips", in_specs=[any_spec], out_specs=any_spec,
        out_shape=jax.ShapeDtypeStruct((4, rs + rsm, 1024), F32),
        scratch_shapes=[pltpu.SemaphoreType.DMA((12,)), pltpu.SemaphoreType.DMA((12,)), pltpu.SemaphoreType.DMA((2,))],
    )(s)


def _adamw(parts, w, m, v):
    rows = w.shape[0]
    tm = _pick_tile(rows, 256)
    c1 = 1.0 / (1.0 - ADAM_B1 ** ADAM_STEP)
    c2 = 1.0 / (1.0 - ADAM_B2 ** ADAM_STEP)

    def body(p_ref, w_ref, m_ref, v_ref, g_ref, d_ref, nm_ref, nv_ref):
        g = ((p_ref[0] + p_ref[1]) + p_ref[2]) + p_ref[3]
        nm = ADAM_B1 * m_ref[...] + (1.0 - ADAM_B1) * g
        nv = ADAM_B2 * v_ref[...] + (1.0 - ADAM_B2) * (g * g)
        g_ref[...] = g
        nm_ref[...] = nm
        nv_ref[...] = nv
        d_ref[...] = -ADAM_LR * ((nm * c1) / (jnp.sqrt(nv * c2) + ADAM_EPS) + ADAM_WD * w_ref[...])

    spec = pl.BlockSpec((tm, 1024), lambda i: (i, 0))
    out = jax.ShapeDtypeStruct(w.shape, F32)
    return pl.pallas_call(
        body, name="adamw", grid=(rows // tm,),
        in_specs=[pl.BlockSpec((4, tm, 1024), lambda i: (0, i, 0)), spec, spec, spec],
        out_specs=[spec] * 4, out_shape=[out] * 4, compiler_params=_cparams(("parallel",)),
    )(parts, w, m, v)


def _expand_heads(a, axis, per_head):
    a = jnp.moveaxis(a, axis, -1)
    lead = a.shape[:-1]
    a = a.reshape(lead + (HEADS, per_head))
    a = jnp.pad(a, [(0, 0)] * len(lead) + [(0, 0), (0, HEAD_PAD - per_head)])
    return jnp.moveaxis(a.reshape(lead + (D_EXP,)), -1, axis)


def _compact_heads(a, axis, start, size):
    a = jnp.moveaxis(a, axis, -1)
    lead = a.shape[:-1]
    a = a.reshape(lead + (HEADS, HEAD_PAD))[..., start:start + size]
    return jnp.moveaxis(a.reshape(lead + (HEADS * size,)), -1, axis)


def _block_diag(w):
    g, a, b = w.shape
    per = g // SSM_BLOCKS
    eye = jnp.eye(per, dtype=w.dtype)
    return jnp.einsum("jgab,gk->jgakb", w.reshape(SSM_BLOCKS, per, a, b), eye).reshape(SSM_BLOCKS, per * a, per * b)


def _block_diag_extract(dense, a, b):
    per = N_GROUPS // SSM_BLOCKS
    d5 = dense.reshape(SSM_BLOCKS, per, a, per, b)
    return jnp.einsum("jgakb,gk->jgab", d5, jnp.eye(per, dtype=dense.dtype)).reshape(N_GROUPS, a, b)


def _discretise(a_re, a_im, log_dt, b_re, b_im):
    dt = jnp.exp(log_dt)[:, None]
    mag = jnp.exp(a_re * dt)
    abar_re = mag * jnp.cos(a_im * dt)
    abar_im = mag * jnp.sin(a_im * dt)
    num_re = abar_re - 1.0
    num_im = abar_im
    den = a_re * a_re + a_im * a_im
    coef_re = (num_re * a_re + num_im * a_im) / den
    coef_im = (num_im * a_re - num_re * a_im) / den
    bbar_re = coef_re[..., None] * b_re - coef_im[..., None] * b_im
    bbar_im = coef_re[..., None] * b_im + coef_im[..., None] * b_re
    return abar_re, abar_im, bbar_re, bbar_im


def _scan_coef(ar, ai, reverse, seg):
    ar, ai = ar.reshape(1, N_STATE), ai.reshape(1, N_STATE)
    cmul = lambda x, y: (x[0] * y[0] - x[1] * y[1], x[0] * y[1] + x[1] * y[0])
    p, sq, n = None, (ar, ai), seg
    while n:
        if n & 1:
            p = sq if p is None else cmul(p, sq)
        sq, n = cmul(sq, sq), n >> 1
    pows = [p]
    for _ in range(7):
        pows.append(cmul(pows[-1], p))
    row = jnp.arange(8)[:, None]
    out = []
    for k in (1, 2, 4):
        keep = (row < 8 - k) if reverse else (row >= k)
        out += [jnp.where(keep, pows[k - 1][0], 0.0), jnp.where(keep, pows[k - 1][1], 0.0)]
    order = list(range(7, -1, -1)) if reverse else list(range(8))
    out += [jnp.concatenate([pows[k][0] for k in order], axis=0), jnp.concatenate([pows[k][1] for k in order], axis=0)]
    out += [jnp.broadcast_to(ar, (8, N_STATE)), jnp.broadcast_to(ai, (8, N_STATE))]
    return jnp.stack(out).astype(F32)


def _flat_rows(a, rows):
    flat = a.reshape(-1)
    return jnp.pad(flat, (0, rows * 1024 - flat.shape[0])).reshape(rows, 1024)


def _pack(named, order):
    rows = [-(-math.prod(named[n].shape) // 1024) for n in order]
    total = -(-sum(rows) // 32) * 32
    parts = [_flat_rows(named[n], r) for n, r in zip(order, rows)]
    if total > sum(rows):
        parts.append(jnp.zeros((total - sum(rows), 1024), parts[0].dtype))
    return jnp.concatenate(parts, axis=0)


def _unpack(packed, shapes, order):
    out, at = {}, 0
    for n in order:
        size = math.prod(shapes[n])
        rows = -(-size // 1024)
        out[n] = packed[at:at + rows].reshape(-1)[:size].reshape(shapes[n])
        at += rows
    return out


def _shard_cols(a, k):
    w = a.shape[-1] // 4
    return a[..., k * w:(k + 1) * w]


def kernel(x, meta_tokens, pre_norm_w, post_norm_w, w_in, q_norm_w, w_q_up, kv_norm_w, w_kv_up, attn_out_norm_w, ssm_a_re, ssm_a_im, ssm_log_dt, ssm_b_re, ssm_b_im, ssm_c_re, ssm_c_im, ssm_d, w_glu, b_glu, ssm_out_norm_w, w_out, loss_target, m_meta_tokens, m_pre_norm_w, m_post_norm_w, m_w_in, m_q_norm_w, m_w_q_up, m_kv_norm_w, m_w_kv_up, m_attn_out_norm_w, m_ssm_a_re, m_ssm_a_im, m_ssm_log_dt, m_ssm_b_re, m_ssm_b_im, m_ssm_c_re, m_ssm_c_im, m_ssm_d, m_w_glu, m_b_glu, m_ssm_out_norm_w, m_w_out, v_meta_tokens, v_pre_norm_w, v_post_norm_w, v_w_in, v_q_norm_w, v_w_q_up, v_kv_norm_w, v_w_kv_up, v_attn_out_norm_w, v_ssm_a_re, v_ssm_a_im, v_ssm_log_dt, v_ssm_b_re, v_ssm_b_im, v_ssm_c_re, v_ssm_c_im, v_ssm_d, v_w_glu, v_b_glu, v_ssm_out_norm_w, v_w_out):
    local = dict(meta_tokens=meta_tokens, pre_norm_w=pre_norm_w, post_norm_w=post_norm_w, w_in=w_in, q_norm_w=q_norm_w,
                 w_q_up=w_q_up, kv_norm_w=kv_norm_w, w_kv_up=w_kv_up, attn_out_norm_w=attn_out_norm_w, ssm_a_re=ssm_a_re,
                 ssm_a_im=ssm_a_im, ssm_log_dt=ssm_log_dt, ssm_b_re=ssm_b_re, ssm_b_im=ssm_b_im, ssm_c_re=ssm_c_re,
                 ssm_c_im=ssm_c_im, ssm_d=ssm_d, w_glu=w_glu, b_glu=b_glu, ssm_out_norm_w=ssm_out_norm_w, w_out=w_out)
    mom_m = dict(meta_tokens=m_meta_tokens, pre_norm_w=m_pre_norm_w, post_norm_w=m_post_norm_w, w_in=m_w_in,
                 q_norm_w=m_q_norm_w, w_q_up=m_w_q_up, kv_norm_w=m_kv_norm_w, w_kv_up=m_w_kv_up,
                 attn_out_norm_w=m_attn_out_norm_w, ssm_a_re=m_ssm_a_re, ssm_a_im=m_ssm_a_im, ssm_log_dt=m_ssm_log_dt,
                 ssm_b_re=m_ssm_b_re, ssm_b_im=m_ssm_b_im, ssm_c_re=m_ssm_c_re, ssm_c_im=m_ssm_c_im, ssm_d=m_ssm_d,
                 w_glu=m_w_glu, b_glu=m_b_glu, ssm_out_norm_w=m_ssm_out_norm_w, w_out=m_w_out)
    mom_v = dict(meta_tokens=v_meta_tokens, pre_norm_w=v_pre_norm_w, post_norm_w=v_post_norm_w, w_in=v_w_in,
                 q_norm_w=v_q_norm_w, w_q_up=v_w_q_up, kv_norm_w=v_kv_norm_w, w_kv_up=v_w_kv_up,
                 attn_out_norm_w=v_attn_out_norm_w, ssm_a_re=v_ssm_a_re, ssm_a_im=v_ssm_a_im, ssm_log_dt=v_ssm_log_dt,
                 ssm_b_re=v_ssm_b_re, ssm_b_im=v_ssm_b_im, ssm_c_re=v_ssm_c_re, ssm_c_im=v_ssm_c_im, ssm_d=v_ssm_d,
                 w_glu=v_w_glu, b_glu=v_b_glu, ssm_out_norm_w=v_ssm_out_norm_w, w_out=v_w_out)
    shapes = {n: local[n].shape for n in WEIGHTS}
    mat = ("w_in", "w_q_up", "w_kv_up", "w_glu", "w_out")

    seq = x.shape[1]
    l_real = N_META + seq
    lp = -(-l_real // 1280) * 1280 if l_real > 1280 else -(-l_real // QBLK) * QBLK
    assert seq % 128 == 0 and lp % QBLK == 0

    w_shard = _pack({n: local[n].astype(BF16) for n in mat}, mat)
    w_shard = jnp.pad(w_shard, ((0, -w_shard.shape[0] % 16), (0, 0)))
    w_all, meta_all = _gather_weights(w_shard, meta_tokens)
    mat_shapes = {n: shapes[n] for n in mat}
    per_chip = [_unpack(w_all[k], mat_shapes, mat) for k in range(4)]
    w_in_f = jnp.concatenate([p["w_in"][0] for p in per_chip], axis=1)
    w_q_f = jnp.concatenate([p["w_q_up"][0] for p in per_chip], axis=1)
    w_kv_f = jnp.concatenate([p["w_kv_up"][0] for p in per_chip], axis=1)
    w_glu_f = jnp.concatenate([p["w_glu"][0] for p in per_chip], axis=1)
    w_out_f = jnp.concatenate([p["w_out"][0] for p in per_chip], axis=0)
    meta_f = jnp.concatenate([meta_all[k] for k in range(4)], axis=1)

    o_q, o_kv, o_kr, o_ga, o_u, o_gs = 0, 256, 384, 416, 928, 1440
    krope_cols = jnp.pad(w_in_f[:, o_kr:o_ga], ((0, 0), (QK_NOPE, HEAD_PAD - QK_NOPE - QK_ROPE)))
    w_in_p = jnp.concatenate([_expand_heads(w_in_f[:, o_ga:o_u], 1, V_HEAD), w_in_f[:, o_u:o_gs], w_in_f[:, o_gs:],
                              w_in_f[:, o_q:o_kv], w_in_f[:, o_kv:o_kr], krope_cols], axis=1)
    wq_p = _expand_heads(w_q_f, 1, QK_NOPE + QK_ROPE)
    kv3 = w_kv_f.reshape(KV_LORA, HEADS, QK_NOPE + V_HEAD)
    wk_p = _expand_heads(kv3[:, :, :QK_NOPE].reshape(KV_LORA, HEADS * QK_NOPE), 1, QK_NOPE)
    wv_c = kv3[:, :, QK_NOPE:].reshape(KV_LORA, HEADS * V_HEAD)
    wv_p = _expand_heads(wv_c, 1, V_HEAD)
    wv_t = jnp.pad(wv_c.T.reshape(HEADS, V_HEAD, KV_LORA), ((0, 0), (0, VT_ROWS - V_HEAD), (0, 0))).reshape(HEADS * VT_ROWS, KV_LORA)
    w_out_a = _expand_heads(w_out_f[:D_ATTN], 0, V_HEAD)
    w_out_s = w_out_f[D_ATTN:]
    attn_norm_e = _expand_heads(attn_out_norm_w, 1, V_HEAD)

    pos = jnp.arange(lp, dtype=jnp.int32)
    half = QK_ROPE // 2
    inv = ROPE_THETA ** (-jnp.arange(half, dtype=F32) / half)
    ang = pos.astype(F32)[:, None] * inv[None, :]
    cos16, sin16 = jnp.cos(ang), jnp.sin(ang)
    ones, zeros = jnp.ones((lp, QK_NOPE), F32), jnp.zeros((lp, QK_NOPE), F32)
    tail1, tail0 = jnp.ones((lp, HEAD_PAD - MASK_LANE), F32), jnp.zeros((lp, HEAD_PAD - MASK_LANE), F32)
    z16 = jnp.zeros((lp, half), F32)
    cos = jnp.concatenate([ones, cos16, cos16, tail1], axis=1)
    sina = jnp.concatenate([zeros, z16, sin16, tail0], axis=1)
    sinb = jnp.concatenate([zeros, -sin16, z16, tail0], axis=1)

    disc_in = (ssm_a_re[0], ssm_a_im[0], ssm_log_dt[0], ssm_b_re[0], ssm_b_im[0])
    disc = lambda a_re, a_im, ldt, b_re, b_im: jax.vmap(_discretise)(a_re, a_im, ldt, b_re, b_im)
    (abar_re, abar_im, bbar_re, bbar_im), disc_vjp = jax.vjp(disc, *disc_in)
    ssm = []
    for d in range(2):
        rev = d == 1
        b_re_bd = _block_diag(jnp.swapaxes(bbar_re[d], 1, 2)).astype(BF16)
        b_im_bd = _block_diag(jnp.swapaxes(bbar_im[d], 1, 2)).astype(BF16)
        c_re_bd = _block_diag(jnp.swapaxes(ssm_c_re[0, d], 1, 2)).astype(BF16)
        c_im_bd = _block_diag(jnp.swapaxes(-ssm_c_im[0, d], 1, 2)).astype(BF16)
        ssm.append(dict(rev=rev, coef=_scan_coef(abar_re[d], abar_im[d], rev, _ssm_tile(lp) // 8),
                        coef_adj=_scan_coef(abar_re[d], -abar_im[d], not rev, _ssm_tile(lp) // 8),
                        b_re=b_re_bd, b_im=b_im_bd, c_re=c_re_bd, c_im=c_im_bd))

    t_ssm = _ssm_tile(lp)
    src = (jnp.arange(t_ssm) % 8) * (t_ssm // 8) + jnp.arange(t_ssm) // 8
    perm = (src[:, None] == jnp.arange(t_ssm)[None, :]).astype(BF16)

    h = jnp.concatenate([meta_f, x[0], jnp.zeros((lp - l_real, D_MODEL), F32)], axis=0)
    proj = _in_proj_fwd(h, pre_norm_w, w_in_p)
    q, k, v, vt, q_t, k_t = _attn_prep_fwd(proj, q_norm_w, kv_norm_w, wq_p, wk_p, wv_p, wv_t, cos, sina, sinb, l_real)
    o_exp, lse = _flash_fwd(q, k, vt)
    ys, states = [], []
    for s in ssm:
        y_d, st_d = _ssm_fwd(proj, perm, s["coef"], s["b_re"], s["b_im"], s["c_re"], s["c_im"], s["rev"])
        ys.append(y_d)
        states.append(st_d)

    (d_o, do_t, delta, dga, dyp, dsg, dres, dwoa, dwos, dwglu, vec_mid) = _mid(
        h, loss_target[0], o_exp, proj, ys[0], ys[1], ssm_d, w_glu_f, w_glu_f.T, b_glu, ssm_out_norm_w, attn_norm_e, w_out_a, w_out_s,
        w_out_a.T, w_out_s.T, post_norm_w, l_real)
    dus, dssm = [], []
    tr = lambda a: jnp.swapaxes(a, 1, 2)
    for s, st_d in zip(ssm, states):
        du_d, dbre, dbim, dcre, dcim, da = _ssm_bwd(proj, dyp, st_d, perm, s["coef"], s["coef_adj"], s["b_re"], s["b_im"],
                                                    tr(s["b_re"]), tr(s["b_im"]), tr(s["c_re"]), tr(s["c_im"]), s["rev"])
        dus.append(du_d)
        dssm.append((dbre, dbim, dcre, dcim, da))
    dq, dk_t, dv_t = _flash_bwd(q, k, v, d_o, q_t, k_t, do_t, lse, delta.T.reshape(HEADS, 1, lp))
    dql, dkvl, dkr, dwq_p, dwk_p, dwv_p, vec_prep = _attn_prep_bwd(
        dq, dk_t, dv_t, proj, q_norm_w, kv_norm_w, wq_p.T, wk_p.T, wv_p.T, cos, sina, sinb)
    dh, dwin_p, vec_in = _in_proj_bwd(h, pre_norm_w, dres, dga, dus[0], dus[1], dyp, ssm_d, dsg, dql, dkvl, dkr, w_in_p.T)

    grads = {}
    grads["w_in"] = jnp.concatenate([
        dwin_p[:, P_QLAT[0]:P_QLAT[0] + 256], dwin_p[:, P_KVLAT[0]:P_KVLAT[0] + 128],
        dwin_p[:, P_KROPE[0] + QK_NOPE:P_KROPE[0] + QK_NOPE + QK_ROPE], _compact_heads(dwin_p[:, 0:D_EXP], 1, 0, V_HEAD),
        dwin_p[:, P_U[0]:P_U[0] + 512], dwin_p[:, P_GATE_S[0]:P_GATE_S[0] + 512]], axis=1)[None]
    grads["w_q_up"] = _compact_heads(dwq_p, 1, 0, QK_NOPE + QK_ROPE)[None]
    dwk3 = _compact_heads(dwk_p, 1, 0, QK_NOPE).reshape(KV_LORA, HEADS, QK_NOPE)
    dwv3 = _compact_heads(dwv_p, 1, 0, V_HEAD).reshape(KV_LORA, HEADS, V_HEAD)
    grads["w_kv_up"] = jnp.concatenate([dwk3, dwv3], axis=2).reshape(1, KV_LORA, HEADS * (QK_NOPE + V_HEAD))
    grads["w_glu"] = dwglu[None]
    grads["w_out"] = jnp.concatenate([_compact_heads(dwoa, 0, 0, V_HEAD), dwos], axis=0)[None]
    grads["meta_tokens"] = dh[:N_META]
    grads["pre_norm_w"] = vec_in[0:1]
    grads["post_norm_w"] = vec_mid[0:1]
    grads["q_norm_w"] = vec_prep[0:1]
    grads["kv_norm_w"] = vec_prep[1:2, :KV_LORA]
    grads["attn_out_norm_w"] = _compact_heads(vec_mid[1:2], 1, 0, V_HEAD)
    grads["ssm_out_norm_w"] = vec_mid[2:3, :D_SSM]
    grads["ssm_d"] = vec_mid[3:4, :D_SSM]
    grads["b_glu"] = vec_mid[4:5]
    d_abar_re = jnp.stack([dssm[d][4][0].sum(axis=0).reshape(N_GROUPS, SSM_STATE) for d in range(2)])
    d_abar_im = jnp.stack([dssm[d][4][1].sum(axis=0).reshape(N_GROUPS, SSM_STATE) for d in range(2)])
    d_bbar_re = jnp.stack([jnp.swapaxes(_block_diag_extract(dssm[d][0], SSM_GROUP, SSM_STATE), 1, 2) for d in range(2)])
    d_bbar_im = jnp.stack([jnp.swapaxes(_block_diag_extract(dssm[d][1], SSM_GROUP, SSM_STATE), 1, 2) for d in range(2)])
    da_re, da_im, dlog_dt, db_re, db_im = disc_vjp((d_abar_re, d_abar_im, d_bbar_re, d_bbar_im))
    grads["ssm_a_re"], grads["ssm_a_im"], grads["ssm_log_dt"] = da_re[None], da_im[None], dlog_dt[None]
    grads["ssm_b_re"], grads["ssm_b_im"] = db_re[None], db_im[None]
    grads["ssm_c_re"] = jnp.stack([_block_diag_extract(dssm[d][2], SSM_GROUP, SSM_STATE) for d in range(2)])[None]
    grads["ssm_c_im"] = jnp.stack([_block_diag_extract(dssm[d][3], SSM_GROUP, SSM_STATE) for d in range(2)])[None]

    def shard_of(n, a, kk):
        return a[:, kk * 256:(kk + 1) * 256] if n == "w_out" else _shard_cols(a, kk)

    slices = [_pack({n: shard_of(n, grads[n], kk) for n in BIG}, BIG) for kk in range(4)]
    grads["loss"] = vec_mid[5:6, 0:1]
    small = _pack({n: grads[n] for n in SMALL + ("loss",)}, SMALL + ("loss",))
    loss_row = slices[0].shape[0] + sum(-(-math.prod(shapes[n]) // 1024) for n in SMALL)
    rs, rsm = slices[0].shape[0], small.shape[0]
    g_pack = jnp.concatenate(slices + [small], axis=0)
    g_pair = _pair_sum(g_pack, _swap_sibling(g_pack))
    parts = _scatter_chips(g_pair, rs, rsm)

    order = BIG + SMALL
    big_shapes = {n: shapes[n] for n in BIG}
    small_shapes = {n: shapes[n] for n in SMALL}

    def pack_state(named):
        return jnp.concatenate([_pack({n: named[n] for n in BIG}, BIG), _pack({n: named[n] for n in SMALL}, SMALL)], axis=0)

    g_out, d_out, m_out, v_out = _adamw(parts, pack_state(local), pack_state(mom_m), pack_state(mom_v))

    def unpack_state(p):
        out = _unpack(p[:rs], big_shapes, BIG)
        out.update(_unpack(p[rs:], small_shapes, SMALL))
        return out

    g_fin, d_fin, m_fin, v_fin = unpack_state(g_out), unpack_state(d_out), unpack_state(m_out), unpack_state(v_out)
    loss = g_out[loss_row, 0]
    grad_x = dh[N_META:l_real][None]
    return (loss, grad_x, *[g_fin[n] for n in WEIGHTS], *[d_fin[n] for n in WEIGHTS], *[m_fin[n] for n in WEIGHTS],
            *[v_fin[n] for n in WEIGHTS])
```

```python
import functools
import math

import jax
import jax.numpy as jnp
from jax import lax
from jax.experimental import pallas as pl
from jax.experimental.pallas import tpu as pltpu

F32 = jnp.float32
BF16 = jnp.bfloat16
MESH = pl.DeviceIdType.MESH

D_MODEL = 1024
N_META = 16
EPS = 1e-6
HEADS = 8
QK_NOPE = 64
QK_ROPE = 32
V_HEAD = 64
VT_ROWS = 80
Q_LORA = 256
KV_LORA = 128
D_ATTN = 512
D_SSM = 512
SSM_GROUP = 16
N_GROUPS = 32
SSM_STATE = 64
N_STATE = N_GROUPS * SSM_STATE
ROPE_THETA = 10000.0
HEAD_PAD = 128
D_EXP = HEADS * HEAD_PAD
D_QK = QK_NOPE + QK_ROPE
MASK_LANE = D_QK
NEG_BIG = -1e30
SCALE = 1.0 / math.sqrt(QK_NOPE + QK_ROPE)
LOG2E = math.log2(math.e)
SCALE2 = SCALE * LOG2E
QBLK = 256
SCAN_COLS = 1024
SCAN_UNROLL = 2
SSM_BLOCKS = 4
BLK_CH = D_SSM // SSM_BLOCKS
BLK_ST = N_STATE // SSM_BLOCKS

P_GATE_A = (0, 1024)
P_U = (1024, 512)
P_GATE_S = (1536, 512)
P_QLAT = (2048, 256)
P_KVLAT = (2304, 128)
P_KROPE = (2432, 128)
D_PROJ = 2560

ADAM_LR = 0.001
ADAM_B1 = 0.9
ADAM_B2 = 0.999
ADAM_EPS = 1e-08
ADAM_WD = 0.01
ADAM_STEP = 10

VMEM_LIMIT = 60 * 1024 * 1024

BIG = ("w_in", "w_q_up", "w_kv_up", "w_glu", "w_out", "meta_tokens")
SMALL = ("pre_norm_w", "post_norm_w", "q_norm_w", "kv_norm_w", "attn_out_norm_w", "ssm_a_re", "ssm_a_im",
         "ssm_log_dt", "ssm_b_re", "ssm_b_im", "ssm_c_re", "ssm_c_im", "ssm_d", "b_glu", "ssm_out_norm_w")
WEIGHTS = ("meta_tokens", "pre_norm_w", "post_norm_w", "w_in", "q_norm_w", "w_q_up", "kv_norm_w", "w_kv_up",
           "attn_out_norm_w", "ssm_a_re", "ssm_a_im", "ssm_log_dt", "ssm_b_re", "ssm_b_im", "ssm_c_re", "ssm_c_im",
           "ssm_d", "w_glu", "b_glu", "ssm_out_norm_w", "w_out")


def _cparams(sem=None):
    return pltpu.CompilerParams(dimension_semantics=sem, vmem_limit_bytes=VMEM_LIMIT)


def _dot(a, b):
    return jnp.dot(a, b, preferred_element_type=F32)


def _dot_nt(a, b):
    return lax.dot_general(a, b, (((1,), (1,)), ((), ())), preferred_element_type=F32)


def _dot_tn(a, b):
    return lax.dot_general(a, b, (((0,), (0,)), ((), ())), preferred_element_type=F32)


def _sigmoid(x):
    return 1.0 / (1.0 + jnp.exp(-x))


def _rms_fwd(x, w, n):
    r = lax.rsqrt(jnp.sum(x * x, axis=-1, keepdims=True) * (1.0 / n) + EPS)
    return x * r * w, r


def _rms_bwd(x, r, w, dy, n):
    dyw = dy * w
    dx = r * dyw - x * (r * r * r) * (jnp.sum(dyw * x, axis=-1, keepdims=True) * (1.0 / n))
    dw = jnp.sum(dy * (x * r), axis=0, keepdims=True)
    return dx, dw


def _rope_apply(x, cos, sina, sinb):
    return x * cos + pltpu.roll(x, 16, 1) * sina + pltpu.roll(x, HEAD_PAD - 16, 1) * sinb


def _rope_transpose(g, cos, sina, sinb):
    return g * cos + pltpu.roll(g * sina, HEAD_PAD - 16, 1) + pltpu.roll(g * sinb, 16, 1)


def _row_tile(lp):
    return 640 if lp % 640 == 0 else 128


def _ssm_tile(lp):
    return 320 if lp % 320 == 0 else 128


def _rows(tm, off_width):
    off, width = off_width
    return pl.BlockSpec((tm, width), lambda i: (i, off // width))


def _whole(shape, single=True):
    nd = len(shape)
    if single:
        return pl.BlockSpec(shape, lambda *_: (0,) * nd, pipeline_mode=pl.Buffered(1))
    return pl.BlockSpec(shape, lambda *_: (0,) * nd)


def _out_whole(shape):
    return _whole(shape, single=False)


def _pick_tile(rows, cap):
    best = 8
    for t in range(8, cap + 1, 8):
        if rows % t == 0:
            best = t
    return best


def _in_proj_fwd(h, pre_w, w_in_p):
    lp = h.shape[0]
    tm = _row_tile(lp)

    def body(h_ref, w_ref, win_ref, proj_ref):
        xn, _ = _rms_fwd(h_ref[...], w_ref[...], D_MODEL)
        proj_ref[...] = _dot(xn.astype(BF16), win_ref[...])

    return pl.pallas_call(
        body, name="in_proj_fwd", grid=(lp // tm,),
        in_specs=[_rows(tm, (0, D_MODEL)), _whole((1, D_MODEL)), _whole((D_MODEL, D_PROJ))],
        out_specs=_rows(tm, (0, D_PROJ)),
        out_shape=jax.ShapeDtypeStruct((lp, D_PROJ), F32),
        compiler_params=_cparams(("parallel",)),
    )(h, pre_w, w_in_p)


def _attn_prep_fwd(proj, q_norm_w, kv_norm_w, wq_p, wk_p, wv_p, wv_t, cos, sina, sinb, l_real):
    lp = proj.shape[0]
    tm = _row_tile(lp)

    def body(ql_ref, kvl_ref, kr_ref, qw_ref, kw_ref, wq_ref, wk_ref, wv_ref, wvt_ref, cos_ref, sa_ref, sb_ref,
             q_ref, k_ref, v_ref, vt_ref, qt_ref, kt_ref):
        cos_t, sa_t, sb_t = cos_ref[...], sa_ref[...], sb_ref[...]
        qn, _ = _rms_fwd(ql_ref[...], qw_ref[...], Q_LORA)
        kvn, _ = _rms_fwd(kvl_ref[...], kw_ref[...], KV_LORA)
        kvn_b = kvn.astype(BF16)
        qp = _dot(qn.astype(BF16), wq_ref[...])
        kp = _dot(kvn_b, wk_ref[...])
        v_ref[...] = _dot(kvn_b, wv_ref[...]).astype(BF16)
        ones_row = lax.broadcasted_iota(jnp.int32, (HEADS * VT_ROWS, 1), 0) % VT_ROWS == V_HEAD
        vt_ref[...] = jnp.where(ones_row, 1.0, _dot_nt(wvt_ref[...], kvn_b)).astype(BF16)
        lane = lax.broadcasted_iota(jnp.int32, (tm, HEAD_PAD), 1)
        row = lax.broadcasted_iota(jnp.int32, (tm, HEAD_PAD), 0) + pl.program_id(0) * tm
        q_one = jnp.where(lane == MASK_LANE, 1.0, 0.0)
        k_add = _rope_apply(kr_ref[...], cos_t, sa_t, sb_t) + jnp.where((lane == MASK_LANE) & (row >= l_real), NEG_BIG, 0.0)
        for hd in range(HEADS):
            blk = slice(hd * HEAD_PAD, (hd + 1) * HEAD_PAD)
            q_h = _rope_apply(qp[:, blk], cos_t, sa_t, sb_t) * SCALE2 + q_one
            k_h = kp[:, blk] + k_add
            q_ref[:, blk] = q_h.astype(BF16)
            k_ref[:, blk] = k_h.astype(BF16)
            qt_ref[hd * D_QK:(hd + 1) * D_QK, :] = q_h.T[:D_QK].astype(BF16)
            kt_ref[hd * D_QK:(hd + 1) * D_QK, :] = k_h.T[:D_QK].astype(BF16)

    tab = _rows(tm, (0, HEAD_PAD))
    out = jax.ShapeDtypeStruct((lp, D_EXP), BF16)
    out_t = jax.ShapeDtypeStruct((HEADS * D_QK, lp), BF16)
    cols_t = pl.BlockSpec((HEADS * D_QK, tm), lambda i: (0, i))
    return pl.pallas_call(
        body, name="attn_prep_fwd", grid=(lp // tm,),
        in_specs=[_rows(tm, P_QLAT), _rows(tm, P_KVLAT), _rows(tm, P_KROPE), _whole((1, Q_LORA)), _whole((1, KV_LORA)),
                  _whole((Q_LORA, D_EXP)), _whole((KV_LORA, D_EXP)), _whole((KV_LORA, D_EXP)),
                  _whole((HEADS * VT_ROWS, KV_LORA)), tab, tab, tab],
        out_specs=[_rows(tm, (0, D_EXP))] * 3 + [pl.BlockSpec((HEADS * VT_ROWS, tm), lambda i: (0, i)), cols_t, cols_t],
        out_shape=[out, out, out, jax.ShapeDtypeStruct((HEADS * VT_ROWS, lp), BF16), out_t, out_t],
        compiler_params=_cparams(("parallel",)),
    )(proj, proj, proj, q_norm_w, kv_norm_w, wq_p, wk_p, wv_p, wv_t, cos, sina, sinb)


def _flash_fwd(q, k, vt):
    lp = q.shape[0]
    tq = 1280 if lp % 1280 == 0 else 256
    tk = QBLK
    nk = lp // tk

    def body(q_ref, k_ref, vt_ref, o_ref, lse_ref, acc, m_s, s_a, s_b):
        acc[...] = jnp.zeros_like(acc)
        m_s[...] = jnp.full(m_s.shape, NEG_BIG, F32)
        blocks = [slice(c * QBLK, (c + 1) * QBLK) for c in range(tq // QBLK)]

        def scores(j, buf):
            kt = k_ref[pl.ds(pl.multiple_of(j * tk, tk), tk), :]
            for cols in blocks:
                buf[:, cols] = _dot_nt(kt, q_ref[cols, :])

        def consume(j, buf):
            vt_t = vt_ref[:, pl.ds(pl.multiple_of(j * tk, tk), tk)]
            m_old, acc_old = m_s[...], acc[...]
            s = [buf[:, cols] for cols in blocks]
            m_new = [jnp.maximum(m_old[:, cols], jnp.max(s_c, axis=0, keepdims=True)) for cols, s_c in zip(blocks, s)]
            p = [jnp.exp2(s_c - m_c) for s_c, m_c in zip(s, m_new)]
            pv = [_dot(vt_t, p_c.astype(BF16)) for p_c in p]
            m_new = jnp.concatenate(m_new, axis=1)
            alpha = jnp.exp2(m_old - m_new)
            acc[...] = alpha * acc_old + jnp.concatenate(pv, axis=1)
            m_s[...] = m_new

        n_pairs = (nk - 1) // 2
        scores(0, s_a)

        def pair(t, _):
            j = 2 * t
            scores(j + 1, s_b)
            consume(j, s_a)
            scores(j + 2, s_a)
            consume(j + 1, s_b)
            return 0

        lax.fori_loop(0, n_pairs, pair, 0, unroll=2 if n_pairs % 2 == 0 else 1)
        if nk - 2 * n_pairs == 2:
            scores(nk - 1, s_b)
            consume(nk - 2, s_a)
            consume(nk - 1, s_b)
        else:
            consume(nk - 1, s_a)
        l = acc[V_HEAD:V_HEAD + 1, :]
        o_t = acc[0:V_HEAD, :] / l
        o_ref[...] = jnp.concatenate([o_t, jnp.zeros_like(o_t)], axis=0).T
        lse_ref[...] = m_s[...] + jnp.log2(l)

    return pl.pallas_call(
        body, name="flash_fwd", grid=(HEADS, lp // tq),
        in_specs=[pl.BlockSpec((tq, HEAD_PAD), lambda hd, i: (i, hd)),
                  pl.BlockSpec((lp, HEAD_PAD), lambda hd, i: (0, hd)),
                  pl.BlockSpec((VT_ROWS, lp), lambda hd, i: (hd, 0))],
        out_specs=[pl.BlockSpec((tq, HEAD_PAD), lambda hd, i: (i, hd)),
                   pl.BlockSpec((None, 1, tq), lambda hd, i: (hd, 0, i))],
        out_shape=[jax.ShapeDtypeStruct((lp, D_EXP), F32), jax.ShapeDtypeStruct((HEADS, 1, lp), F32)],
        scratch_shapes=[pltpu.VMEM((VT_ROWS, tq), F32), pltpu.VMEM((1, tq), F32),
                        pltpu.VMEM((tk, tq), F32), pltpu.VMEM((tk, tq), F32)],
        compiler_params=_cparams(("parallel", "parallel")),
    )(q, k, vt)


def _unpermute_rows(val, scr, out_ref, seg):
    for c in range(val.shape[1] // 128):
        scr[c] = val[:, c * 128:(c + 1) * 128]
    for k in range(8):
        for c in range(val.shape[1] // 128):
            out_ref[k * seg:(k + 1) * seg, c * 128:(c + 1) * 128] = scr[c, pl.ds(k, seg, stride=8), :]


def _scan_rows(xr_ref, xi_ref, base, n_rows, coef_ref, carry_ref, reverse, tile_fn=None, acc_refs=(), halo=False):
    seg = n_rows // 8
    shifts = (7, 6, 4) if reverse else (1, 2, 4)
    row8 = lax.broadcasted_iota(jnp.int32, (8, SCAN_COLS), 0)
    edge, shift = (7, 7) if reverse else (0, 1)
    for cg in range(N_STATE // SCAN_COLS):
        cols = slice(cg * SCAN_COLS, (cg + 1) * SCAN_COLS)
        ar, ai = coef_ref[8, :, cols], coef_ref[9, :, cols]

        def rows_at(i):
            tau = (seg - 1 - i) if reverse else i
            return tau, pl.ds(pl.multiple_of(base + tau * 8, 8), 8)

        def local(i, carry, cols=cols, ar=ar, ai=ai):
            pr, pi_ = carry
            _, rows = rows_at(i)
            nr = ar * pr - ai * pi_ + xr_ref[rows, cols]
            ni = ar * pi_ + ai * pr + xi_ref[rows, cols]
            xr_ref[rows, cols] = nr
            xi_ref[rows, cols] = ni
            return nr, ni

        zero = jnp.zeros((8, SCAN_COLS), F32)
        fr, fi = lax.fori_loop(0, seg, local, (zero, zero), unroll=SCAN_UNROLL)
        co = [coef_ref[k, :, cols] for k in range(8)]
        for lvl in range(3):
            pr, pi_ = co[2 * lvl], co[2 * lvl + 1]
            sr = pltpu.roll(fr, shifts[lvl], 0)
            si = pltpu.roll(fi, shifts[lvl], 0)
            fr, fi = fr + pr * sr - pi_ * si, fi + pr * si + pi_ * sr
        cr, ci = carry_ref[0:1, cols], carry_ref[1:2, cols]
        fr, fi = fr + co[6] * cr - co[7] * ci, fi + co[6] * ci + co[7] * cr
        carry_ref[0:1, cols] = fr[0:1] if reverse else fr[7:8]
        carry_ref[1:2, cols] = fi[0:1] if reverse else fi[7:8]
        in_r = jnp.where(row8 == edge, cr, pltpu.roll(fr, shift, 0))
        in_i = jnp.where(row8 == edge, ci, pltpu.roll(fi, shift, 0))
        if halo:
            rows = pl.ds(base + n_rows, 8) if reverse else pl.ds(base - 8, 8)
            xr_ref[rows, cols] = in_r
            xi_ref[rows, cols] = in_i

        def fix(i, carry, cols=cols, ar=ar, ai=ai):
            c_r, c_i = carry[0], carry[1]
            tau, rows = rows_at(i)
            nr = xr_ref[rows, cols] + c_r
            ni = xi_ref[rows, cols] + c_i
            xr_ref[rows, cols] = nr
            xi_ref[rows, cols] = ni
            accs = carry[2:]
            if tile_fn is not None:
                accs = tuple(a + d for a, d in zip(accs, tile_fn(tau, cols, nr, ni)))
            return (ar * c_r - ai * c_i, ar * c_i + ai * c_r) + accs

        init = (ar * in_r - ai * in_i, ar * in_i + ai * in_r) + tuple(a[:, cols] for a in acc_refs)
        out = lax.fori_loop(0, seg, fix, init, unroll=SCAN_UNROLL)
        for a, val in zip(acc_refs, out[2:]):
            a[:, cols] = val


def _ssm_fwd(proj, perm, coef, b_re, b_im, c_re, c_im_neg, reverse):
    lp = proj.shape[0]
    t = _ssm_tile(lp)
    n = lp // t
    order = (lambda i: n - 1 - i) if reverse else (lambda i: i)

    def body(u_ref, pm_ref, coef_ref, bre_ref, bim_ref, cre_ref, cim_ref, y_ref, st_ref, xr, xi, carry, stage):
        @pl.when(pl.program_id(0) == 0)
        def _():
            carry[...] = jnp.zeros_like(carry)

        st_ref[...] = carry[0:2, :]
        ub = _dot(pm_ref[...], u_ref[...].astype(BF16)).astype(BF16)
        for j in range(SSM_BLOCKS):
            ch, stt = slice(j * BLK_CH, (j + 1) * BLK_CH), slice(j * BLK_ST, (j + 1) * BLK_ST)
            xr[:, stt] = _dot(ub[:, ch], bre_ref[j])
            xi[:, stt] = _dot(ub[:, ch], bim_ref[j])
        _scan_rows(xr, xi, 0, t, coef_ref, carry, reverse)
        y = jnp.concatenate(
            [_dot(xr[:, j * BLK_ST:(j + 1) * BLK_ST].astype(BF16), cre_ref[j])
             + _dot(xi[:, j * BLK_ST:(j + 1) * BLK_ST].astype(BF16), cim_ref[j]) for j in range(SSM_BLOCKS)], axis=1)
        _unpermute_rows(y, stage, y_ref, t // 8)

    wb, wc = _whole((SSM_BLOCKS, BLK_CH, BLK_ST)), _whole((SSM_BLOCKS, BLK_ST, BLK_CH))
    return pl.pallas_call(
        body, name="ssm_fwd_rev" if reverse else "ssm_fwd", grid=(n,),
        in_specs=[pl.BlockSpec((t, D_SSM), lambda i: (order(i), P_U[0] // D_SSM)), _whole((t, t)), _whole((10, 8, N_STATE)),
                  wb, wb, wc, wc],
        out_specs=[pl.BlockSpec((t, D_SSM), lambda i: (order(i), 0)),
                   pl.BlockSpec((None, 2, N_STATE), lambda i: (order(i), 0, 0))],
        out_shape=[jax.ShapeDtypeStruct((lp, D_SSM), F32), jax.ShapeDtypeStruct((n, 2, N_STATE), F32)],
        scratch_shapes=[pltpu.VMEM((t, N_STATE), F32), pltpu.VMEM((t, N_STATE), F32), pltpu.VMEM((8, N_STATE), F32),
                        pltpu.VMEM((D_SSM // 128, t, 128), F32)],
        compiler_params=_cparams(("arbitrary",)),
    )(proj, perm, coef, b_re, b_im, c_re, c_im_neg)


GELU_C0 = math.sqrt(2.0 / math.pi)
GELU_C1 = 0.044715


def _mid(h, tgt, o_exp, proj, y0, y1, ssm_d, w_glu, w_glu_t, b_glu, ssm_norm_w, attn_norm_w_e, w_out_a, w_out_s,
         w_out_a_t, w_out_s_t, post_w, l_real):
    lp = h.shape[0]
    tm = 256

    def body(h_ref, tga_ref, tgb_ref, o_ref, ga_ref, u_ref, sg_ref, y0_ref, y1_ref, d_ref, wg_ref, wgt_ref, bg_ref, ws_ref,
             wa_ref, woa_ref, wos_ref, woat_ref, wost_ref, pw_ref,
             do_ref, dot_ref, delta_ref, dga_ref, dyp_ref, dsg_ref, dres_ref, dwoa_ref, dwos_ref, dwg_ref, vec_ref):
        @pl.when(pl.program_id(0) == 0)
        def _():
            dwoa_ref[...] = jnp.zeros_like(dwoa_ref)
            dwos_ref[...] = jnp.zeros_like(dwos_ref)
            dwg_ref[...] = jnp.zeros_like(dwg_ref)
            vec_ref[...] = jnp.zeros_like(vec_ref)

        u = u_ref[...]
        ypre = y0_ref[...] + y1_ref[...] + d_ref[...] * u
        th = jnp.tanh(GELU_C0 * (ypre + GELU_C1 * ypre * ypre * ypre))
        gel = 0.5 * ypre * (1.0 + th)
        gel_b = gel.astype(BF16)
        glu = _dot(gel_b, wg_ref[...]) + bg_ref[...]
        g1, g2 = glu[:, :D_SSM], glu[:, D_SSM:]
        sig2 = _sigmoid(g2)
        z = g1 * sig2
        sg = sg_ref[...]
        sgs = _sigmoid(sg)
        sil_s = sg * sgs
        s = z * sil_s
        ys, r_s = _rms_fwd(s, ws_ref[...], D_SSM)

        o = o_ref[...]
        ga = ga_ref[...]
        gas = _sigmoid(ga)
        sil_a = ga * gas
        a = o * sil_a
        ya, r_a = _rms_fwd(a, wa_ref[...], D_ATTN)

        ya_b, ys_b = ya.astype(BF16), ys.astype(BF16)
        y = _dot(ya_b, woa_ref[...]) + _dot(ys_b, wos_ref[...])
        yn, r_y = _rms_fwd(y, pw_ref[...], D_MODEL)
        row = lax.broadcasted_iota(jnp.int32, (tm, 1), 0) + pl.program_id(0) * tm
        valid = (row >= N_META) & (row < l_real)
        tgt = jnp.concatenate([tga_ref[tm - N_META:, :], tgb_ref[:tm - N_META, :]], axis=0)
        err = jnp.where(valid, h_ref[...] + yn - tgt, 0.0)
        loss = 0.5 * jnp.sum(jnp.sum(err * err, axis=-1, keepdims=True), axis=0, keepdims=True) * (1.0 / D_MODEL)
        dout = err * (1.0 / D_MODEL)
        dres_ref[...] = dout

        dy, d_pw = _rms_bwd(y, r_y, pw_ref[...], dout, D_MODEL)
        dy_b = dy.astype(BF16)
        dya = _dot(dy_b, woat_ref[...])
        dys = _dot(dy_b, wost_ref[...])
        dwoa_ref[...] += _dot_tn(ya_b, dy_b)
        dwos_ref[...] += _dot_tn(ys_b, dy_b)

        da, d_wa = _rms_bwd(a, r_a, wa_ref[...], dya, D_ATTN)
        d_o = da * sil_a
        dga_ref[...] = da * o * (gas * (1.0 + ga * (1.0 - gas)))
        do_ref[...] = d_o.astype(BF16)
        for hd in range(HEADS):
            dot_ref[hd * V_HEAD:(hd + 1) * V_HEAD, :] = d_o[:, hd * HEAD_PAD:(hd + 1) * HEAD_PAD].T[:V_HEAD].astype(BF16)
        prod = d_o * o
        lane8 = lax.broadcasted_iota(jnp.int32, (tm, HEADS), 1)
        delta = jnp.zeros((tm, HEADS), F32)
        for hd in range(HEADS):
            delta = jnp.where(lane8 == hd, jnp.sum(prod[:, hd * HEAD_PAD:(hd + 1) * HEAD_PAD], axis=-1, keepdims=True), delta)
        delta_ref[...] = delta

        ds, d_ws = _rms_bwd(s, r_s, ws_ref[...], dys, D_SSM)
        dz = ds * sil_s
        dsg_ref[...] = ds * z * (sgs * (1.0 + sg * (1.0 - sgs)))
        dglu = jnp.concatenate([dz * sig2, dz * g1 * sig2 * (1.0 - sig2)], axis=-1)
        dglu_b = dglu.astype(BF16)
        dwg_ref[...] += _dot_tn(gel_b, dglu_b)
        dgel = _dot(dglu_b, wgt_ref[...])
        dgelu = 0.5 * (1.0 + th) + 0.5 * ypre * (1.0 - th * th) * (GELU_C0 * (1.0 + 3.0 * GELU_C1 * ypre * ypre))
        dyp = dgel * dgelu
        dyp_ref[...] = dyp

        vec_ref[0:1, :] += d_pw
        vec_ref[1:2, :] += d_wa
        vec_ref[2:3, 0:D_SSM] += d_ws
        vec_ref[3:4, 0:D_SSM] += jnp.sum(dyp * u, axis=0, keepdims=True)
        vec_ref[4:5, :] += jnp.sum(dglu, axis=0, keepdims=True)
        vec_ref[5:6, :] += jnp.broadcast_to(loss, (1, D_MODEL))

    full = lambda off: _rows(tm, (off, D_MODEL))
    half = lambda off: _rows(tm, (off, D_SSM))
    last = tgt.shape[0] // tm - 1
    tg_a = pl.BlockSpec((tm, D_MODEL), lambda i: (jnp.clip(i - 1, 0, last), 0))
    tg_b = pl.BlockSpec((tm, D_MODEL), lambda i: (jnp.minimum(i, last), 0))
    return pl.pallas_call(
        body, name="mid", grid=(lp // tm,),
        in_specs=[full(0), tg_a, tg_b, full(0), _rows(tm, P_GATE_A), _rows(tm, P_U), _rows(tm, P_GATE_S), half(0), half(0),
                  _whole((1, D_SSM)), _whole((D_SSM, 2 * D_SSM)), _whole((2 * D_SSM, D_SSM)), _whole((1, 2 * D_SSM)),
                  _whole((1, D_SSM)), _whole((1, D_EXP)), _whole((D_EXP, D_MODEL)), _whole((D_SSM, D_MODEL)),
                  _whole((D_MODEL, D_EXP)), _whole((D_MODEL, D_SSM)), _whole((1, D_MODEL))],
        out_specs=[full(0), pl.BlockSpec((D_ATTN, tm), lambda i: (0, i)), _rows(tm, (0, HEADS)), full(0), half(0), half(0), full(0),
                   _out_whole((D_EXP, D_MODEL)), _out_whole((D_SSM, D_MODEL)), _out_whole((D_SSM, 2 * D_SSM)),
                   _out_whole((8, D_MODEL))],
        out_shape=[jax.ShapeDtypeStruct((lp, D_EXP), BF16), jax.ShapeDtypeStruct((D_ATTN, lp), BF16),
                   jax.ShapeDtypeStruct((lp, HEADS), F32),
                   jax.ShapeDtypeStruct((lp, D_EXP), F32), jax.ShapeDtypeStruct((lp, D_SSM), F32),
                   jax.ShapeDtypeStruct((lp, D_SSM), F32), jax.ShapeDtypeStruct((lp, D_MODEL), F32),
                   jax.ShapeDtypeStruct((D_EXP, D_MODEL), F32), jax.ShapeDtypeStruct((D_SSM, D_MODEL), F32),
                   jax.ShapeDtypeStruct((D_SSM, 2 * D_SSM), F32), jax.ShapeDtypeStruct((8, D_MODEL), F32)],
        compiler_params=_cparams(("arbitrary",)),
    )(h, tgt, tgt, o_exp, proj, proj, proj, y0, y1, ssm_d, w_glu, w_glu_t, b_glu, ssm_norm_w, attn_norm_w_e, w_out_a, w_out_s,
      w_out_a_t, w_out_s_t, post_w)


def _ssm_bwd(proj, dyp, states, perm, coef, coef_adj, b_re, b_im, b_re_t, b_im_t, c_re_t, c_im_neg_t, reverse):
    lp = proj.shape[0]
    t = _ssm_tile(lp)
    n = lp // t
    order = (lambda i: i) if reverse else (lambda i: n - 1 - i)

    def body(u_ref, dy_ref, st_ref, pm_ref, coef_ref, coefa_ref, bre_ref, bim_ref, bret_ref, bimt_ref, cret_ref, cimt_ref,
             du_ref, dbre_ref, dbim_ref, dcre_ref, dcim_ref, da_ref, xr, xi, gr, gi, carry_x, carry_g, stage):
        @pl.when(pl.program_id(0) == 0)
        def _():
            carry_g[...] = jnp.zeros_like(carry_g)
            carry_x[...] = jnp.zeros_like(carry_x)
            dbre_ref[...] = jnp.zeros_like(dbre_ref)
            dbim_ref[...] = jnp.zeros_like(dbim_ref)
            dcre_ref[...] = jnp.zeros_like(dcre_ref)
            dcim_ref[...] = jnp.zeros_like(dcim_ref)
            da_ref[...] = jnp.zeros_like(da_ref)
            for halo in (slice(0, 8), slice(t + 8, t + 16)):
                xr[halo, :] = jnp.zeros((8, N_STATE), F32)
                xi[halo, :] = jnp.zeros((8, N_STATE), F32)

        ub = _dot(pm_ref[...], u_ref[...].astype(BF16)).astype(BF16)
        dyb = _dot(pm_ref[...], dy_ref[...].astype(BF16)).astype(BF16)
        carry_x[0:2, :] = st_ref[...]
        blocks = [(slice(j * BLK_CH, (j + 1) * BLK_CH), slice(j * BLK_ST, (j + 1) * BLK_ST)) for j in range(SSM_BLOCKS)]
        for j, (ch, stt) in enumerate(blocks):
            xr[8:t + 8, stt] = _dot(ub[:, ch], bre_ref[j])
            xi[8:t + 8, stt] = _dot(ub[:, ch], bim_ref[j])
            gr[:, stt] = _dot(dyb[:, ch], cret_ref[j])
            gi[:, stt] = _dot(dyb[:, ch], cimt_ref[j])
        _scan_rows(xr, xi, 8, t, coef_ref, carry_x, reverse, halo=True)

        def tile_fn(tau, cols, g_re, g_im):
            nb = pl.ds(pl.multiple_of((tau + 2) * 8 if reverse else tau * 8, 8), 8)
            xn_r, xn_i = xr[nb, cols], xi[nb, cols]
            return g_re * xn_r + g_im * xn_i, g_im * xn_r - g_re * xn_i

        _scan_rows(gr, gi, 0, t, coefa_ref, carry_g, not reverse, tile_fn=tile_fn, acc_refs=(da_ref.at[0], da_ref.at[1]))

        du = []
        for j, (ch, stt) in enumerate(blocks):
            g_re_b, g_im_b = gr[:, stt].astype(BF16), gi[:, stt].astype(BF16)
            du.append(_dot(g_re_b, bret_ref[j]) + _dot(g_im_b, bimt_ref[j]))
            dbre_ref[j] += _dot_tn(ub[:, ch], g_re_b)
            dbim_ref[j] += _dot_tn(ub[:, ch], g_im_b)
            dcre_ref[j] += _dot_tn(dyb[:, ch], xr[8:t + 8, stt].astype(BF16))
            dcim_ref[j] -= _dot_tn(dyb[:, ch], xi[8:t + 8, stt].astype(BF16))
        _unpermute_rows(jnp.concatenate(du, axis=1), stage, du_ref, t // 8)

    dense = jax.ShapeDtypeStruct((SSM_BLOCKS, BLK_CH, BLK_ST), F32)
    wb, wc = _whole((SSM_BLOCKS, BLK_CH, BLK_ST)), _whole((SSM_BLOCKS, BLK_ST, BLK_CH))
    acc = _out_whole((SSM_BLOCKS, BLK_CH, BLK_ST))
    return pl.pallas_call(
        body, name="ssm_bwd_rev" if reverse else "ssm_bwd", grid=(n,),
        in_specs=[pl.BlockSpec((t, D_SSM), lambda i: (order(i), P_U[0] // D_SSM)),
                  pl.BlockSpec((t, D_SSM), lambda i: (order(i), 0)),
                  pl.BlockSpec((None, 2, N_STATE), lambda i: (order(i), 0, 0)), _whole((t, t)),
                  _whole((10, 8, N_STATE)), _whole((10, 8, N_STATE)), wb, wb, wc, wc, wb, wb],
        out_specs=[pl.BlockSpec((t, D_SSM), lambda i: (order(i), 0)), acc, acc, acc, acc, _out_whole((2, 8, N_STATE))],
        out_shape=[jax.ShapeDtypeStruct((lp, D_SSM), F32), dense, dense, dense, dense,
                   jax.ShapeDtypeStruct((2, 8, N_STATE), F32)],
        scratch_shapes=[pltpu.VMEM((t + 16, N_STATE), F32), pltpu.VMEM((t + 16, N_STATE), F32),
                        pltpu.VMEM((t, N_STATE), F32), pltpu.VMEM((t, N_STATE), F32),
                        pltpu.VMEM((8, N_STATE), F32), pltpu.VMEM((8, N_STATE), F32),
                        pltpu.VMEM((D_SSM // 128, t, 128), F32)],
        compiler_params=_cparams(("arbitrary",)),
    )(proj, dyp, states, perm, coef, coef_adj, b_re, b_im, b_re_t, b_im_t, c_re_t, c_im_neg_t)


def _flash_bwd(q, k, v, d_o, q_t, k_t, do_t, lse_row, delta_row):
    lp = q.shape[0]
    tq = 1280 if lp % 1280 == 0 else 256
    tk = QBLK
    nk = lp // tk
    d_qk = QK_NOPE + QK_ROPE

    def body(q_ref, do_ref, qt_ref, dot_ref, lse_ref, delta_ref, k_ref, v_ref, kt_ref, dq_ref, dk_ref, dv_ref, dq_acc):
        @pl.when(pl.program_id(1) == 0)
        def _():
            dk_ref[...] = jnp.zeros_like(dk_ref)
            dv_ref[...] = jnp.zeros_like(dv_ref)

        dq_acc[...] = jnp.zeros_like(dq_acc)
        lse, delta = lse_ref[...], delta_ref[...]
        q_cols, do_cols = qt_ref[...], dot_ref[...]
        blocks = [slice(c * QBLK, (c + 1) * QBLK) for c in range(tq // QBLK)]

        grp = 5 if nk % 5 == 0 else 1

        def group(t, _):
            base = pl.multiple_of(t * (grp * tk), grp * tk)
            dq = dq_acc[...]
            dvs, dks = [], []
            for u in range(grp):
                ks = pl.multiple_of(base + u * tk, tk)
                k_rows, v_rows = k_ref[pl.ds(ks, tk), :], v_ref[pl.ds(ks, tk), :]
                st = [_dot_nt(k_rows, q_ref[cols, :]) for cols in blocks]
                dpt = [_dot_nt(v_rows, do_ref[cols, :]) for cols in blocks]
                pt = [jnp.exp2(s_c - lse[:, cols]) for s_c, cols in zip(st, blocks)]
                dst = [p_c * (dp_c - delta[:, cols]) for p_c, dp_c, cols in zip(pt, dpt, blocks)]
                pt_b = jnp.concatenate([p_c.astype(BF16) for p_c in pt], axis=1)
                dst_b = jnp.concatenate([d_c.astype(BF16) for d_c in dst], axis=1)
                dvs.append(_dot_nt(do_cols, pt_b))
                dks.append(_dot_nt(q_cols, dst_b))
                dq = dq + _dot(kt_ref[:, pl.ds(ks, tk)], dst_b)
            dq_acc[...] = dq
            dv_ref[:, pl.ds(base, grp * tk)] += jnp.concatenate(dvs, axis=1)
            dk_ref[:, pl.ds(base, grp * tk)] += jnp.concatenate(dks, axis=1) * (1.0 / LOG2E)
            return 0

        lax.fori_loop(0, nk // grp, group, 0)
        dq_ref[...] = jnp.concatenate([dq_acc[...], jnp.zeros((HEAD_PAD - d_qk, tq), F32)], axis=0).T

    tile = pl.BlockSpec((tq, HEAD_PAD), lambda hd, i: (i, hd))
    head = pl.BlockSpec((lp, HEAD_PAD), lambda hd, i: (0, hd))
    rowv = pl.BlockSpec((None, 1, tq), lambda hd, i: (hd, 0, i))
    return pl.pallas_call(
        body, name="flash_bwd", grid=(HEADS, lp // tq),
        in_specs=[tile, tile, pl.BlockSpec((d_qk, tq), lambda hd, i: (hd, i)), pl.BlockSpec((V_HEAD, tq), lambda hd, i: (hd, i)),
                  rowv, rowv, head, head, pl.BlockSpec((d_qk, lp), lambda hd, i: (hd, 0))],
        out_specs=[tile, pl.BlockSpec((d_qk, lp), lambda hd, i: (hd, 0)), pl.BlockSpec((V_HEAD, lp), lambda hd, i: (hd, 0))],
        out_shape=[jax.ShapeDtypeStruct((lp, D_EXP), F32), jax.ShapeDtypeStruct((HEADS * d_qk, lp), F32),
                   jax.ShapeDtypeStruct((HEADS * V_HEAD, lp), F32)],
        scratch_shapes=[pltpu.VMEM((d_qk, tq), F32)],
        compiler_params=_cparams(("parallel", "arbitrary")),
    )(q, d_o, q_t, do_t, lse_row, delta_row, k, v, k_t)


def _attn_prep_bwd(dq, dk_t, dv_t, proj, q_norm_w, kv_norm_w, wq_pt, wk_pt, wv_pt, cos, sina, sinb):
    lp = proj.shape[0]
    tm = _row_tile(lp)

    def body(dq_ref, dk_ref, dv_ref, ql_ref, kvl_ref, qw_ref, kw_ref, wqt_ref, wkt_ref, wvt_ref, cos_ref, sa_ref, sb_ref,
             dql_ref, dkvl_ref, dkr_ref, dwq_ref, dwk_ref, dwv_ref, vec_ref):
        @pl.when(pl.program_id(0) == 0)
        def _():
            dwq_ref[...] = jnp.zeros_like(dwq_ref)
            dwk_ref[...] = jnp.zeros_like(dwk_ref)
            dwv_ref[...] = jnp.zeros_like(dwv_ref)
            vec_ref[...] = jnp.zeros_like(vec_ref)

        cos_t, sa_t, sb_t = cos_ref[...], sa_ref[...], sb_ref[...]

        def head_rows(t_ref, per):
            pad = jnp.zeros((HEAD_PAD - per, tm), F32)
            return jnp.concatenate(
                [jnp.concatenate([t_ref[hd * per:(hd + 1) * per, :], pad], axis=0).T for hd in range(HEADS)], axis=-1)

        dkp = head_rows(dk_ref, D_QK)
        dqp = jnp.concatenate(
            [_rope_transpose(dq_ref[:, hd * HEAD_PAD:(hd + 1) * HEAD_PAD] * SCALE, cos_t, sa_t, sb_t) for hd in range(HEADS)],
            axis=-1)
        dkr = dkp[:, 0:HEAD_PAD]
        for hd in range(1, HEADS):
            dkr = dkr + dkp[:, hd * HEAD_PAD:(hd + 1) * HEAD_PAD]
        dkr_ref[...] = _rope_transpose(dkr, cos_t, sa_t, sb_t)

        qn, r_q = _rms_fwd(ql_ref[...], qw_ref[...], Q_LORA)
        kvn, r_kv = _rms_fwd(kvl_ref[...], kw_ref[...], KV_LORA)
        dqp_b, dkp_b, dv_b = dqp.astype(BF16), dkp.astype(BF16), head_rows(dv_ref, V_HEAD).astype(BF16)
        dqn = _dot(dqp_b, wqt_ref[...])
        dkvn = _dot(dkp_b, wkt_ref[...]) + _dot(dv_b, wvt_ref[...])
        dwq_ref[...] += _dot_tn(qn.astype(BF16), dqp_b)
        dwk_ref[...] += _dot_tn(kvn.astype(BF16), dkp_b)
        dwv_ref[...] += _dot_tn(kvn.astype(BF16), dv_b)
        dql, d_qw = _rms_bwd(ql_ref[...], r_q, qw_ref[...], dqn, Q_LORA)
        dkvl, d_kw = _rms_bwd(kvl_ref[...], r_kv, kw_ref[...], dkvn, KV_LORA)
        dql_ref[...] = dql
        dkvl_ref[...] = dkvl
        vec_ref[0:1, :] += d_qw
        vec_ref[1:2, 0:KV_LORA] += d_kw

    tab = _rows(tm, (0, HEAD_PAD))
    full = _rows(tm, (0, D_EXP))
    return pl.pallas_call(
        body, name="attn_prep_bwd", grid=(lp // tm,),
        in_specs=[full, pl.BlockSpec((HEADS * D_QK, tm), lambda i: (0, i)), pl.BlockSpec((D_ATTN, tm), lambda i: (0, i)),
                  _rows(tm, P_QLAT), _rows(tm, P_KVLAT), _whole((1, Q_LORA)), _whole((1, KV_LORA)),
                  _whole((D_EXP, Q_LORA)), _whole((D_EXP, KV_LORA)), _whole((D_EXP, KV_LORA)), tab, tab, tab],
        out_specs=[_rows(tm, (0, Q_LORA)), _rows(tm, (0, KV_LORA)), _rows(tm, (0, HEAD_PAD)),
                   _out_whole((Q_LORA, D_EXP)), _out_whole((KV_LORA, D_EXP)), _out_whole((KV_LORA, D_EXP)),
                   _out_whole((8, Q_LORA))],
        out_shape=[jax.ShapeDtypeStruct((lp, Q_LORA), F32), jax.ShapeDtypeStruct((lp, KV_LORA), F32),
                   jax.ShapeDtypeStruct((lp, HEAD_PAD), F32), jax.ShapeDtypeStruct((Q_LORA, D_EXP), F32),
                   jax.ShapeDtypeStruct((KV_LORA, D_EXP), F32), jax.ShapeDtypeStruct((KV_LORA, D_EXP), F32),
                   jax.ShapeDtypeStruct((8, Q_LORA), F32)],
        compiler_params=_cparams(("arbitrary",)),
    )(dq, dk_t, dv_t, proj, proj, q_norm_w, kv_norm_w, wq_pt, wk_pt, wv_pt, cos, sina, sinb)


def _in_proj_bwd(h, pre_w, dres, dga, du0, du1, dyp, ssm_d, dsg, dql, dkvl, dkr, w_in_pt):
    lp = h.shape[0]
    tm = 256
    pieces = (P_GATE_A, P_U, P_GATE_S, P_QLAT, P_KVLAT, P_KROPE)

    def body(h_ref, w_ref, dres_ref, dga_ref, du0_ref, du1_ref, dyp_ref, d_ref, dsg_ref, dql_ref, dkvl_ref, dkr_ref, wt_ref,
             dh_ref, dw_ref, vec_ref):
        @pl.when(pl.program_id(0) == 0)
        def _():
            dw_ref[...] = jnp.zeros_like(dw_ref)
            vec_ref[...] = jnp.zeros_like(vec_ref)

        hv = h_ref[...]
        xn, r = _rms_fwd(hv, w_ref[...], D_MODEL)
        xn_b = xn.astype(BF16)
        du = du0_ref[...] + du1_ref[...] + dyp_ref[...] * d_ref[...]
        grads = (dga_ref[...], du, dsg_ref[...], dql_ref[...], dkvl_ref[...], dkr_ref[...])
        dxn = jnp.zeros((tm, D_MODEL), F32)
        for (off, width), g in zip(pieces, grads):
            g_b = g.astype(BF16)
            dxn = dxn + _dot(g_b, wt_ref[off:off + width, :])
            dw_ref[:, off:off + width] += _dot_tn(xn_b, g_b)
        dx, d_w = _rms_bwd(hv, r, w_ref[...], dxn, D_MODEL)
        dh_ref[...] = dres_ref[...] + dx
        vec_ref[0:1, :] += d_w

    full = _rows(tm, (0, D_MODEL))
    half = _rows(tm, (0, D_SSM))
    return pl.pallas_call(
        body, name="in_proj_bwd", grid=(lp // tm,),
        in_specs=[full, _whole((1, D_MODEL)), full, full, half, half, half, _whole((1, D_SSM)), half,
                  _rows(tm, (0, Q_LORA)), _rows(tm, (0, KV_LORA)), _rows(tm, (0, HEAD_PAD)), _whole((D_PROJ, D_MODEL))],
        out_specs=[full, _out_whole((D_MODEL, D_PROJ)), _out_whole((8, D_MODEL))],
        out_shape=[jax.ShapeDtypeStruct((lp, D_MODEL), F32), jax.ShapeDtypeStruct((D_MODEL, D_PROJ), F32),
                   jax.ShapeDtypeStruct((8, D_MODEL), F32)],
        compiler_params=_cparams(("arbitrary",)),
    )(h, pre_w, dres, dga, du0, du1, dyp, ssm_d, dsg, dql, dkvl, dkr, w_in_pt)


def _other_chips(x, y):
    return [(1 - x, y), (x, 1 - y), (1 - x, 1 - y)]


def _gather_weights(w_bf16, meta):
    any_spec = pl.BlockSpec(memory_space=pl.ANY)
    halves = (w_bf16.shape[0] // 2, meta.shape[0] // 2)

    def body(w_ref, m_ref, wout_ref, mout_ref, send_sems, recv_sems, local_sems):
        x, y, c = lax.axis_index("x"), lax.axis_index("y"), lax.axis_index("c")
        me, sibling = 2 * x + y, (x, y, 1 - c)
        srcs, dsts = (w_ref, m_ref), (wout_ref, mout_ref)

        def half(n, cc):
            return pl.ds(pl.multiple_of(cc * halves[n], 8), halves[n])

        def copy(n, sem, src, chip, cc, to):
            return pltpu.make_async_remote_copy(src_ref=src, dst_ref=dsts[n].at[chip, half(n, cc)], send_sem=send_sems.at[sem],
                                                recv_sem=recv_sems.at[sem], device_id=to, device_id_type=MESH)

        own = [pltpu.make_async_copy(srcs[n], dsts[n].at[me], local_sems.at[n]) for n in range(2)]
        for cp in own:
            cp.start()
        chips = _other_chips(x, y)
        first = [copy(n, 2 * j + n, srcs[n].at[half(n, c)], me, c, (tx, ty, c)) for j, (tx, ty) in enumerate(chips) for n in range(2)]
        for cp in first:
            cp.start()
        passed = []
        for j, (tx, ty) in enumerate(chips):
            for n in range(2):
                landed = dsts[n].at[2 * tx + ty, half(n, c)]
                copy(n, 2 * j + n, landed, 2 * tx + ty, c, (tx, ty, c)).wait_recv()
                passed.append(copy(n, 6 + 2 * j + n, landed, 2 * tx + ty, c, sibling))
                passed[-1].start()
        for j, (tx, ty) in enumerate(chips):
            for n in range(2):
                copy(n, 6 + 2 * j + n, dsts[n].at[2 * tx + ty, half(n, 1 - c)], 2 * tx + ty, 1 - c, sibling).wait_recv()
        for cp in first + passed:
            cp.wait_send()
        for cp in own:
            cp.wait()

    return pl.pallas_call(
        body, name="gather_weights",
        in_specs=[any_spec, any_spec], out_specs=[any_spec, any_spec],
        out_shape=[jax.ShapeDtypeStruct((4,) + w_bf16.shape, w_bf16.dtype), jax.ShapeDtypeStruct((4,) + meta.shape, meta.dtype)],
        scratch_shapes=[pltpu.SemaphoreType.DMA((12,)), pltpu.SemaphoreType.DMA((12,)), pltpu.SemaphoreType.DMA((2,))],
    )(w_bf16, meta)


def _swap_sibling(g):
    any_spec = pl.BlockSpec(memory_space=pl.ANY)

    def body(g_ref, out_ref, send_sem, recv_sem):
        x, y, c = lax.axis_index("x"), lax.axis_index("y"), lax.axis_index("c")
        cp = pltpu.make_async_remote_copy(src_ref=g_ref, dst_ref=out_ref, send_sem=send_sem, recv_sem=recv_sem,
                                          device_id=(x, y, 1 - c), device_id_type=MESH)
        cp.start()
        cp.wait()

    return pl.pallas_call(
        body, name="swap_sibling", in_specs=[any_spec], out_specs=any_spec,
        out_shape=jax.ShapeDtypeStruct(g.shape, g.dtype),
        scratch_shapes=[pltpu.SemaphoreType.DMA(()), pltpu.SemaphoreType.DMA(())],
    )(g)


def _pair_sum(a, b):
    rows = a.shape[0]
    tm = _pick_tile(rows, 1024)

    def body(a_ref, b_ref, o_ref):
        o_ref[...] = a_ref[...] + b_ref[...]

    spec = pl.BlockSpec((tm, 1024), lambda i: (i, 0))
    return pl.pallas_call(body, name="pair_sum", grid=(rows // tm,), in_specs=[spec, spec], out_specs=spec,
                          out_shape=jax.ShapeDtypeStruct(a.shape, F32), compiler_params=_cparams(("parallel",)))(a, b)


def _scatter_chips(s, rs, rsm):
    any_spec = pl.BlockSpec(memory_space=pl.ANY)
    lens = (rs // 2, rsm // 2)

    def body(s_ref, out_ref, send_sems, recv_sems, local_sems):
        x, y, c = lax.axis_index("x"), lax.axis_index("y"), lax.axis_index("c")
        me, sibling = 2 * x + y, (x, y, 1 - c)

        def src_rows(n, target, cc):
            start = (target * rs if n == 0 else 4 * rs) + cc * lens[n]
            return s_ref.at[pl.ds(pl.multiple_of(start, 8), lens[n])]

        def dst_rows(n, cc):
            return pl.ds(pl.multiple_of((0 if n == 0 else rs) + cc * lens[n], 8), lens[n])

        def copy(n, sem, src, chip, cc, to):
            return pltpu.make_async_remote_copy(src_ref=src, dst_ref=out_ref.at[chip, dst_rows(n, cc)], send_sem=send_sems.at[sem],
                                                recv_sem=recv_sems.at[sem], device_id=to, device_id_type=MESH)

        own = [pltpu.make_async_copy(s_ref.at[pl.ds(pl.multiple_of(me * rs, 8), rs)], out_ref.at[me, pl.ds(0, rs)], local_sems.at[0]),
               pltpu.make_async_copy(s_ref.at[pl.ds(4 * rs, rsm)], out_ref.at[me, pl.ds(rs, rsm)], local_sems.at[1])]
        for cp in own:
            cp.start()
        chips = _other_chips(x, y)
        first = [copy(n, 2 * j + n, src_rows(n, 2 * tx + ty, c), me, c, (tx, ty, c))
                 for j, (tx, ty) in enumerate(chips) for n in range(2)]
        for cp in first:
            cp.start()
        passed = []
        for j, (tx, ty) in enumerate(chips):
            for n in range(2):
                landed = out_ref.at[2 * tx + ty, dst_rows(n, c)]
                copy(n, 2 * j + n, landed, 2 * tx + ty, c, (tx, ty, c)).wait_recv()
                passed.append(copy(n, 6 + 2 * j + n, landed, 2 * tx + ty, c, sibling))
                passed[-1].start()
        for j, (tx, ty) in enumerate(chips):
            for n in range(2):
                copy(n, 6 + 2 * j + n, out_ref.at[2 * tx + ty, dst_rows(n, 1 - c)], 2 * tx + ty, 1 - c, sibling).wait_recv()
        for cp in first + passed:
            cp.wait_send()
        for cp in own:
            cp.wait()

    return pl.pallas_call(
        body, name="scatter_chips", in_specs=[any_spec], out_specs=any_spec,
        out_shape=jax.ShapeDtypeStruct((4, rs + rsm, 1024), F32),
        scratch_shapes=[pltpu.SemaphoreType.DMA((12,)), pltpu.SemaphoreType.DMA((12,)), pltpu.SemaphoreType.DMA((2,))],
    )(s)


def _adamw(parts, w, m, v):
    rows = w.shape[0]
    tm = _pick_tile(rows, 256)
    c1 = 1.0 / (1.0 - ADAM_B1 ** ADAM_STEP)
    c2 = 1.0 / (1.0 - ADAM_B2 ** ADAM_STEP)

    def body(p_ref, w_ref, m_ref, v_ref, g_ref, d_ref, nm_ref, nv_ref):
        g = ((p_ref[0] + p_ref[1]) + p_ref[2]) + p_ref[3]
        nm = ADAM_B1 * m_ref[...] + (1.0 - ADAM_B1) * g
        nv = ADAM_B2 * v_ref[...] + (1.0 - ADAM_B2) * (g * g)
        g_ref[...] = g
        nm_ref[...] = nm
        nv_ref[...] = nv
        d_ref[...] = -ADAM_LR * ((nm * c1) / (jnp.sqrt(nv * c2) + ADAM_EPS) + ADAM_WD * w_ref[...])

    spec = pl.BlockSpec((tm, 1024), lambda i: (i, 0))
    out = jax.ShapeDtypeStruct(w.shape, F32)
    return pl.pallas_call(
        body, name="adamw", grid=(rows // tm,),
        in_specs=[pl.BlockSpec((4, tm, 1024), lambda i: (0, i, 0)), spec, spec, spec],
        out_specs=[spec] * 4, out_shape=[out] * 4, compiler_params=_cparams(("parallel",)),
    )(parts, w, m, v)


def _expand_heads(a, axis, per_head):
    a = jnp.moveaxis(a, axis, -1)
    lead = a.shape[:-1]
    a = a.reshape(lead + (HEADS, per_head))
    a = jnp.pad(a, [(0, 0)] * len(lead) + [(0, 0), (0, HEAD_PAD - per_head)])
    return jnp.moveaxis(a.reshape(lead + (D_EXP,)), -1, axis)


def _compact_heads(a, axis, start, size):
    a = jnp.moveaxis(a, axis, -1)
    lead = a.shape[:-1]
    a = a.reshape(lead + (HEADS, HEAD_PAD))[..., start:start + size]
    return jnp.moveaxis(a.reshape(lead + (HEADS * size,)), -1, axis)


def _block_diag(w):
    g, a, b = w.shape
    per = g // SSM_BLOCKS
    eye = jnp.eye(per, dtype=w.dtype)
    return jnp.einsum("jgab,gk->jgakb", w.reshape(SSM_BLOCKS, per, a, b), eye).reshape(SSM_BLOCKS, per * a, per * b)


def _block_diag_extract(dense, a, b):
    per = N_GROUPS // SSM_BLOCKS
    d5 = dense.reshape(SSM_BLOCKS, per, a, per, b)
    return jnp.einsum("jgakb,gk->jgab", d5, jnp.eye(per, dtype=dense.dtype)).reshape(N_GROUPS, a, b)


def _discretise(a_re, a_im, log_dt, b_re, b_im):
    dt = jnp.exp(log_dt)[:, None]
    mag = jnp.exp(a_re * dt)
    abar_re = mag * jnp.cos(a_im * dt)
    abar_im = mag * jnp.sin(a_im * dt)
    num_re = abar_re - 1.0
    num_im = abar_im
    den = a_re * a_re + a_im * a_im
    coef_re = (num_re * a_re + num_im * a_im) / den
    coef_im = (num_im * a_re - num_re * a_im) / den
    bbar_re = coef_re[..., None] * b_re - coef_im[..., None] * b_im
    bbar_im = coef_re[..., None] * b_im + coef_im[..., None] * b_re
    return abar_re, abar_im, bbar_re, bbar_im


def _scan_coef(ar, ai, reverse, seg):
    ar, ai = ar.reshape(1, N_STATE), ai.reshape(1, N_STATE)
    cmul = lambda x, y: (x[0] * y[0] - x[1] * y[1], x[0] * y[1] + x[1] * y[0])
    p, sq, n = None, (ar, ai), seg
    while n:
        if n & 1:
            p = sq if p is None else cmul(p, sq)
        sq, n = cmul(sq, sq), n >> 1
    pows = [p]
    for _ in range(7):
        pows.append(cmul(pows[-1], p))
    row = jnp.arange(8)[:, None]
    out = []
    for k in (1, 2, 4):
        keep = (row < 8 - k) if reverse else (row >= k)
        out += [jnp.where(keep, pows[k - 1][0], 0.0), jnp.where(keep, pows[k - 1][1], 0.0)]
    order = list(range(7, -1, -1)) if reverse else list(range(8))
    out += [jnp.concatenate([pows[k][0] for k in order], axis=0), jnp.concatenate([pows[k][1] for k in order], axis=0)]
    out += [jnp.broadcast_to(ar, (8, N_STATE)), jnp.broadcast_to(ai, (8, N_STATE))]
    return jnp.stack(out).astype(F32)


def _flat_rows(a, rows):
    flat = a.reshape(-1)
    return jnp.pad(flat, (0, rows * 1024 - flat.shape[0])).reshape(rows, 1024)


def _pack(named, order):
    rows = [-(-math.prod(named[n].shape) // 1024) for n in order]
    total = -(-sum(rows) // 32) * 32
    parts = [_flat_rows(named[n], r) for n, r in zip(order, rows)]
    if total > sum(rows):
        parts.append(jnp.zeros((total - sum(rows), 1024), parts[0].dtype))
    return jnp.concatenate(parts, axis=0)


def _unpack(packed, shapes, order):
    out, at = {}, 0
    for n in order:
        size = math.prod(shapes[n])
        rows = -(-size // 1024)
        out[n] = packed[at:at + rows].reshape(-1)[:size].reshape(shapes[n])
        at += rows
    return out


def _shard_cols(a, k):
    w = a.shape[-1] // 4
    return a[..., k * w:(k + 1) * w]


def kernel(x, meta_tokens, pre_norm_w, post_norm_w, w_in, q_norm_w, w_q_up, kv_norm_w, w_kv_up, attn_out_norm_w, ssm_a_re, ssm_a_im, ssm_log_dt, ssm_b_re, ssm_b_im, ssm_c_re, ssm_c_im, ssm_d, w_glu, b_glu, ssm_out_norm_w, w_out, loss_target, m_meta_tokens, m_pre_norm_w, m_post_norm_w, m_w_in, m_q_norm_w, m_w_q_up, m_kv_norm_w, m_w_kv_up, m_attn_out_norm_w, m_ssm_a_re, m_ssm_a_im, m_ssm_log_dt, m_ssm_b_re, m_ssm_b_im, m_ssm_c_re, m_ssm_c_im, m_ssm_d, m_w_glu, m_b_glu, m_ssm_out_norm_w, m_w_out, v_meta_tokens, v_pre_norm_w, v_post_norm_w, v_w_in, v_q_norm_w, v_w_q_up, v_kv_norm_w, v_w_kv_up, v_attn_out_norm_w, v_ssm_a_re, v_ssm_a_im, v_ssm_log_dt, v_ssm_b_re, v_ssm_b_im, v_ssm_c_re, v_ssm_c_im, v_ssm_d, v_w_glu, v_b_glu, v_ssm_out_norm_w, v_w_out):
    local = dict(meta_tokens=meta_tokens, pre_norm_w=pre_norm_w, post_norm_w=post_norm_w, w_in=w_in, q_norm_w=q_norm_w,
                 w_q_up=w_q_up, kv_norm_w=kv_norm_w, w_kv_up=w_kv_up, attn_out_norm_w=attn_out_norm_w, ssm_a_re=ssm_a_re,
                 ssm_a_im=ssm_a_im, ssm_log_dt=ssm_log_dt, ssm_b_re=ssm_b_re, ssm_b_im=ssm_b_im, ssm_c_re=ssm_c_re,
                 ssm_c_im=ssm_c_im, ssm_d=ssm_d, w_glu=w_glu, b_glu=b_glu, ssm_out_norm_w=ssm_out_norm_w, w_out=w_out)
    mom_m = dict(meta_tokens=m_meta_tokens, pre_norm_w=m_pre_norm_w, post_norm_w=m_post_norm_w, w_in=m_w_in,
                 q_norm_w=m_q_norm_w, w_q_up=m_w_q_up, kv_norm_w=m_kv_norm_w, w_kv_up=m_w_kv_up,
                 attn_out_norm_w=m_attn_out_norm_w, ssm_a_re=m_ssm_a_re, ssm_a_im=m_ssm_a_im, ssm_log_dt=m_ssm_log_dt,
                 ssm_b_re=m_ssm_b_re, ssm_b_im=m_ssm_b_im, ssm_c_re=m_ssm_c_re, ssm_c_im=m_ssm_c_im, ssm_d=m_ssm_d,
                 w_glu=m_w_glu, b_glu=m_b_glu, ssm_out_norm_w=m_ssm_out_norm_w, w_out=m_w_out)
    mom_v = dict(meta_tokens=v_meta_tokens, pre_norm_w=v_pre_norm_w, post_norm_w=v_post_norm_w, w_in=v_w_in,
                 q_norm_w=v_q_norm_w, w_q_up=v_w_q_up, kv_norm_w=v_kv_norm_w, w_kv_up=v_w_kv_up,
                 attn_out_norm_w=v_attn_out_norm_w, ssm_a_re=v_ssm_a_re, ssm_a_im=v_ssm_a_im, ssm_log_dt=v_ssm_log_dt,
                 ssm_b_re=v_ssm_b_re, ssm_b_im=v_ssm_b_im, ssm_c_re=v_ssm_c_re, ssm_c_im=v_ssm_c_im, ssm_d=v_ssm_d,
                 w_glu=v_w_glu, b_glu=v_b_glu, ssm_out_norm_w=v_ssm_out_norm_w, w_out=v_w_out)
    shapes = {n: local[n].shape for n in WEIGHTS}
    mat = ("w_in", "w_q_up", "w_kv_up", "w_glu", "w_out")

    seq = x.shape[1]
    l_real = N_META + seq
    lp = -(-l_real // 1280) * 1280 if l_real > 1280 else -(-l_real // QBLK) * QBLK
    assert seq % 128 == 0 and lp % QBLK == 0

    w_shard = _pack({n: local[n].astype(BF16) for n in mat}, mat)
    w_shard = jnp.pad(w_shard, ((0, -w_shard.shape[0] % 16), (0, 0)))
    w_all, meta_all = _gather_weights(w_shard, meta_tokens)
    mat_shapes = {n: shapes[n] for n in mat}
    per_chip = [_unpack(w_all[k], mat_shapes, mat) for k in range(4)]
    w_in_f = jnp.concatenate([p["w_in"][0] for p in per_chip], axis=1)
    w_q_f = jnp.concatenate([p["w_q_up"][0] for p in per_chip], axis=1)
    w_kv_f = jnp.concatenate([p["w_kv_up"][0] for p in per_chip], axis=1)
    w_glu_f = jnp.concatenate([p["w_glu"][0] for p in per_chip], axis=1)
    w_out_f = jnp.concatenate([p["w_out"][0] for p in per_chip], axis=0)
    meta_f = jnp.concatenate([meta_all[k] for k in range(4)], axis=1)

    o_q, o_kv, o_kr, o_ga, o_u, o_gs = 0, 256, 384, 416, 928, 1440
    krope_cols = jnp.pad(w_in_f[:, o_kr:o_ga], ((0, 0), (QK_NOPE, HEAD_PAD - QK_NOPE - QK_ROPE)))
    w_in_p = jnp.concatenate([_expand_heads(w_in_f[:, o_ga:o_u], 1, V_HEAD), w_in_f[:, o_u:o_gs], w_in_f[:, o_gs:],
                              w_in_f[:, o_q:o_kv], w_in_f[:, o_kv:o_kr], krope_cols], axis=1)
    wq_p = _expand_heads(w_q_f, 1, QK_NOPE + QK_ROPE)
    kv3 = w_kv_f.reshape(KV_LORA, HEADS, QK_NOPE + V_HEAD)
    wk_p = _expand_heads(kv3[:, :, :QK_NOPE].reshape(KV_LORA, HEADS * QK_NOPE), 1, QK_NOPE)
    wv_c = kv3[:, :, QK_NOPE:].reshape(KV_LORA, HEADS * V_HEAD)
    wv_p = _expand_heads(wv_c, 1, V_HEAD)
    wv_t = jnp.pad(wv_c.T.reshape(HEADS, V_HEAD, KV_LORA), ((0, 0), (0, VT_ROWS - V_HEAD), (0, 0))).reshape(HEADS * VT_ROWS, KV_LORA)
    w_out_a = _expand_heads(w_out_f[:D_ATTN], 0, V_HEAD)
    w_out_s = w_out_f[D_ATTN:]
    attn_norm_e = _expand_heads(attn_out_norm_w, 1, V_HEAD)

    pos = jnp.arange(lp, dtype=jnp.int32)
    half = QK_ROPE // 2
    inv = ROPE_THETA ** (-jnp.arange(half, dtype=F32) / half)
    ang = pos.astype(F32)[:, None] * inv[None, :]
    cos16, sin16 = jnp.cos(ang), jnp.sin(ang)
    ones, zeros = jnp.ones((lp, QK_NOPE), F32), jnp.zeros((lp, QK_NOPE), F32)
    tail1, tail0 = jnp.ones((lp, HEAD_PAD - MASK_LANE), F32), jnp.zeros((lp, HEAD_PAD - MASK_LANE), F32)
    z16 = jnp.zeros((lp, half), F32)
    cos = jnp.concatenate([ones, cos16, cos16, tail1], axis=1)
    sina = jnp.concatenate([zeros, z16, sin16, tail0], axis=1)
    sinb = jnp.concatenate([zeros, -sin16, z16, tail0], axis=1)

    disc_in = (ssm_a_re[0], ssm_a_im[0], ssm_log_dt[0], ssm_b_re[0], ssm_b_im[0])
    disc = lambda a_re, a_im, ldt, b_re, b_im: jax.vmap(_discretise)(a_re, a_im, ldt, b_re, b_im)
    (abar_re, abar_im, bbar_re, bbar_im), disc_vjp = jax.vjp(disc, *disc_in)
    ssm = []
    for d in range(2):
        rev = d == 1
        b_re_bd = _block_diag(jnp.swapaxes(bbar_re[d], 1, 2)).astype(BF16)
        b_im_bd = _block_diag(jnp.swapaxes(bbar_im[d], 1, 2)).astype(BF16)
        c_re_bd = _block_diag(jnp.swapaxes(ssm_c_re[0, d], 1, 2)).astype(BF16)
        c_im_bd = _block_diag(jnp.swapaxes(-ssm_c_im[0, d], 1, 2)).astype(BF16)
        ssm.append(dict(rev=rev, coef=_scan_coef(abar_re[d], abar_im[d], rev, _ssm_tile(lp) // 8),
                        coef_adj=_scan_coef(abar_re[d], -abar_im[d], not rev, _ssm_tile(lp) // 8),
                        b_re=b_re_bd, b_im=b_im_bd, c_re=c_re_bd, c_im=c_im_bd))

    t_ssm = _ssm_tile(lp)
    src = (jnp.arange(t_ssm) % 8) * (t_ssm // 8) + jnp.arange(t_ssm) // 8
    perm = (src[:, None] == jnp.arange(t_ssm)[None, :]).astype(BF16)

    h = jnp.concatenate([meta_f, x[0], jnp.zeros((lp - l_real, D_MODEL), F32)], axis=0)
    proj = _in_proj_fwd(h, pre_norm_w, w_in_p)
    q, k, v, vt, q_t, k_t = _attn_prep_fwd(proj, q_norm_w, kv_norm_w, wq_p, wk_p, wv_p, wv_t, cos, sina, sinb, l_real)
    o_exp, lse = _flash_fwd(q, k, vt)
    ys, states = [], []
    for s in ssm:
        y_d, st_d = _ssm_fwd(proj, perm, s["coef"], s["b_re"], s["b_im"], s["c_re"], s["c_im"], s["rev"])
        ys.append(y_d)
        states.append(st_d)

    (d_o, do_t, delta, dga, dyp, dsg, dres, dwoa, dwos, dwglu, vec_mid) = _mid(
        h, loss_target[0], o_exp, proj, ys[0], ys[1], ssm_d, w_glu_f, w_glu_f.T, b_glu, ssm_out_norm_w, attn_norm_e, w_out_a, w_out_s,
        w_out_a.T, w_out_s.T, post_norm_w, l_real)
    dus, dssm = [], []
    tr = lambda a: jnp.swapaxes(a, 1, 2)
    for s, st_d in zip(ssm, states):
        du_d, dbre, dbim, dcre, dcim, da = _ssm_bwd(proj, dyp, st_d, perm, s["coef"], s["coef_adj"], s["b_re"], s["b_im"],
                                                    tr(s["b_re"]), tr(s["b_im"]), tr(s["c_re"]), tr(s["c_im"]), s["rev"])
        dus.append(du_d)
        dssm.append((dbre, dbim, dcre, dcim, da))
    dq, dk_t, dv_t = _flash_bwd(q, k, v, d_o, q_t, k_t, do_t, lse, delta.T.reshape(HEADS, 1, lp))
    dql, dkvl, dkr, dwq_p, dwk_p, dwv_p, vec_prep = _attn_prep_bwd(
        dq, dk_t, dv_t, proj, q_norm_w, kv_norm_w, wq_p.T, wk_p.T, wv_p.T, cos, sina, sinb)
    dh, dwin_p, vec_in = _in_proj_bwd(h, pre_norm_w, dres, dga, dus[0], dus[1], dyp, ssm_d, dsg, dql, dkvl, dkr, w_in_p.T)

    grads = {}
    grads["w_in"] = jnp.concatenate([
        dwin_p[:, P_QLAT[0]:P_QLAT[0] + 256], dwin_p[:, P_KVLAT[0]:P_KVLAT[0] + 128],
        dwin_p[:, P_KROPE[0] + QK_NOPE:P_KROPE[0] + QK_NOPE + QK_ROPE], _compact_heads(dwin_p[:, 0:D_EXP], 1, 0, V_HEAD),
        dwin_p[:, P_U[0]:P_U[0] + 512], dwin_p[:, P_GATE_S[0]:P_GATE_S[0] + 512]], axis=1)[None]
    grads["w_q_up"] = _compact_heads(dwq_p, 1, 0, QK_NOPE + QK_ROPE)[None]
    dwk3 = _compact_heads(dwk_p, 1, 0, QK_NOPE).reshape(KV_LORA, HEADS, QK_NOPE)
    dwv3 = _compact_heads(dwv_p, 1, 0, V_HEAD).reshape(KV_LORA, HEADS, V_HEAD)
    grads["w_kv_up"] = jnp.concatenate([dwk3, dwv3], axis=2).reshape(1, KV_LORA, HEADS * (QK_NOPE + V_HEAD))
    grads["w_glu"] = dwglu[None]
    grads["w_out"] = jnp.concatenate([_compact_heads(dwoa, 0, 0, V_HEAD), dwos], axis=0)[None]
    grads["meta_tokens"] = dh[:N_META]
    grads["pre_norm_w"] = vec_in[0:1]
    grads["post_norm_w"] = vec_mid[0:1]
    grads["q_norm_w"] = vec_prep[0:1]
    grads["kv_norm_w"] = vec_prep[1:2, :KV_LORA]
    grads["attn_out_norm_w"] = _compact_heads(vec_mid[1:2], 1, 0, V_HEAD)
    grads["ssm_out_norm_w"] = vec_mid[2:3, :D_SSM]
    grads["ssm_d"] = vec_mid[3:4, :D_SSM]
    grads["b_glu"] = vec_mid[4:5]
    d_abar_re = jnp.stack([dssm[d][4][0].sum(axis=0).reshape(N_GROUPS, SSM_STATE) for d in range(2)])
    d_abar_im = jnp.stack([dssm[d][4][1].sum(axis=0).reshape(N_GROUPS, SSM_STATE) for d in range(2)])
    d_bbar_re = jnp.stack([jnp.swapaxes(_block_diag_extract(dssm[d][0], SSM_GROUP, SSM_STATE), 1, 2) for d in range(2)])
    d_bbar_im = jnp.stack([jnp.swapaxes(_block_diag_extract(dssm[d][1], SSM_GROUP, SSM_STATE), 1, 2) for d in range(2)])
    da_re, da_im, dlog_dt, db_re, db_im = disc_vjp((d_abar_re, d_abar_im, d_bbar_re, d_bbar_im))
    grads["ssm_a_re"], grads["ssm_a_im"], grads["ssm_log_dt"] = da_re[None], da_im[None], dlog_dt[None]
    grads["ssm_b_re"], grads["ssm_b_im"] = db_re[None], db_im[None]
    grads["ssm_c_re"] = jnp.stack([_block_diag_extract(dssm[d][2], SSM_GROUP, SSM_STATE) for d in range(2)])[None]
    grads["ssm_c_im"] = jnp.stack([_block_diag_extract(dssm[d][3], SSM_GROUP, SSM_STATE) for d in range(2)])[None]

    def shard_of(n, a, kk):
        return a[:, kk * 256:(kk + 1) * 256] if n == "w_out" else _shard_cols(a, kk)

    slices = [_pack({n: shard_of(n, grads[n], kk) for n in BIG}, BIG) for kk in range(4)]
    grads["loss"] = vec_mid[5:6, 0:1]
    small = _pack({n: grads[n] for n in SMALL + ("loss",)}, SMALL + ("loss",))
    loss_row = slices[0].shape[0] + sum(-(-math.prod(shapes[n]) // 1024) for n in SMALL)
    rs, rsm = slices[0].shape[0], small.shape[0]
    g_pack = jnp.concatenate(slices + [small], axis=0)
    g_pair = _pair_sum(g_pack, _swap_sibling(g_pack))
    parts = _scatter_chips(g_pair, rs, rsm)

    order = BIG + SMALL
    big_shapes = {n: shapes[n] for n in BIG}
    small_shapes = {n: shapes[n] for n in SMALL}

    def pack_state(named):
        return jnp.concatenate([_pack({n: named[n] for n in BIG}, BIG), _pack({n: named[n] for n in SMALL}, SMALL)], axis=0)

    g_out, d_out, m_out, v_out = _adamw(parts, pack_state(local), pack_state(mom_m), pack_state(mom_v))

    def unpack_state(p):
        out = _unpack(p[:rs], big_shapes, BIG)
        out.update(_unpack(p[rs:], small_shapes, SMALL))
        return out

    g_fin, d_fin, m_fin, v_fin = unpack_state(g_out), unpack_state(d_out), unpack_state(m_out), unpack_state(v_out)
    loss = g_out[loss_row, 0]
    grad_x = dh[N_META:l_real][None]
    return (loss, grad_x, *[g_fin[n] for n in WEIGHTS], *[d_fin[n] for n in WEIGHTS], *[m_fin[n] for n in WEIGHTS],
            *[v_fin[n] for n in WEIGHTS])
```

```python
import functools
import math

import jax
import jax.numpy as jnp
from jax import lax
from jax.experimental import pallas as pl
from jax.experimental.pallas import tpu as pltpu

F32 = jnp.float32
BF16 = jnp.bfloat16
MESH = pl.DeviceIdType.MESH

D_MODEL = 1024
N_META = 16
EPS = 1e-6
HEADS = 8
QK_NOPE = 64
QK_ROPE = 32
V_HEAD = 64
VT_ROWS = 80
Q_LORA = 256
KV_LORA = 128
D_ATTN = 512
D_SSM = 512
SSM_GROUP = 16
N_GROUPS = 32
SSM_STATE = 64
N_STATE = N_GROUPS * SSM_STATE
ROPE_THETA = 10000.0
HEAD_PAD = 128
D_EXP = HEADS * HEAD_PAD
D_QK = QK_NOPE + QK_ROPE
MASK_LANE = D_QK
NEG_BIG = -1e30
SCALE = 1.0 / math.sqrt(QK_NOPE + QK_ROPE)
LOG2E = math.log2(math.e)
SCALE2 = SCALE * LOG2E
QBLK = 256
QUAD = 4
SCAN_COLS = 1024
SCAN_UNROLL = 2
SSM_BLOCKS = 4
BLK_CH = D_SSM // SSM_BLOCKS
BLK_ST = N_STATE // SSM_BLOCKS

P_GATE_A = (0, 1024)
P_U = (1024, 512)
P_GATE_S = (1536, 512)
P_QLAT = (2048, 256)
P_KVLAT = (2304, 128)
P_KROPE = (2432, 128)
D_PROJ = 2560

ADAM_LR = 0.001
ADAM_B1 = 0.9
ADAM_B2 = 0.999
ADAM_EPS = 1e-08
ADAM_WD = 0.01
ADAM_STEP = 10

VMEM_LIMIT = 60 * 1024 * 1024

BIG = ("w_in", "w_q_up", "w_kv_up", "w_glu", "w_out", "meta_tokens")
SMALL = ("pre_norm_w", "post_norm_w", "q_norm_w", "kv_norm_w", "attn_out_norm_w", "ssm_a_re", "ssm_a_im",
         "ssm_log_dt", "ssm_b_re", "ssm_b_im", "ssm_c_re", "ssm_c_im", "ssm_d", "b_glu", "ssm_out_norm_w")
WEIGHTS = ("meta_tokens", "pre_norm_w", "post_norm_w", "w_in", "q_norm_w", "w_q_up", "kv_norm_w", "w_kv_up",
           "attn_out_norm_w", "ssm_a_re", "ssm_a_im", "ssm_log_dt", "ssm_b_re", "ssm_b_im", "ssm_c_re", "ssm_c_im",
           "ssm_d", "w_glu", "b_glu", "ssm_out_norm_w", "w_out")


def _cparams(sem=None):
    return pltpu.CompilerParams(dimension_semantics=sem, vmem_limit_bytes=VMEM_LIMIT)


def _dot(a, b):
    return jnp.dot(a, b, preferred_element_type=F32)


def _dot_nt(a, b):
    return lax.dot_general(a, b, (((1,), (1,)), ((), ())), preferred_element_type=F32)


def _dot_tn(a, b):
    return lax.dot_general(a, b, (((0,), (0,)), ((), ())), preferred_element_type=F32)


def _sigmoid(x):
    return 1.0 / (1.0 + jnp.exp(-x))


def _rms_fwd(x, w, n):
    r = lax.rsqrt(jnp.sum(x * x, axis=-1, keepdims=True) * (1.0 / n) + EPS)
    return x * r * w, r


def _rms_bwd(x, r, w, dy, n):
    dyw = dy * w
    dx = r * dyw - x * (r * r * r) * (jnp.sum(dyw * x, axis=-1, keepdims=True) * (1.0 / n))
    dw = jnp.sum(dy * (x * r), axis=0, keepdims=True)
    return dx, dw


def _rope_apply(x, cos, sina, sinb):
    return x * cos + pltpu.roll(x, 16, 1) * sina + pltpu.roll(x, HEAD_PAD - 16, 1) * sinb


def _rope_transpose(g, cos, sina, sinb):
    return g * cos + pltpu.roll(g * sina, HEAD_PAD - 16, 1) + pltpu.roll(g * sinb, 16, 1)


def _row_tile(lp):
    return 640 if lp % 640 == 0 else 128


def _ssm_tile(lp):
    return 320 if lp % 320 == 0 else 128


def _rows(tm, off_width):
    off, width = off_width
    return pl.BlockSpec((tm, width), lambda i: (i, off // width))


def _whole(shape, single=True):
    nd = len(shape)
    if single:
        return pl.BlockSpec(shape, lambda *_: (0,) * nd, pipeline_mode=pl.Buffered(1))
    return pl.BlockSpec(shape, lambda *_: (0,) * nd)


def _out_whole(shape):
    return _whole(shape, single=False)


def _pick_tile(rows, cap):
    best = 8
    for t in range(8, cap + 1, 8):
        if rows % t == 0:
            best = t
    return best


def _in_proj_fwd(h, pre_w, w_in_p):
    lp = h.shape[0]
    tm = _row_tile(lp)

    def body(h_ref, w_ref, win_ref, proj_ref):
        xn, _ = _rms_fwd(h_ref[...], w_ref[...], D_MODEL)
        proj_ref[...] = _dot(xn.astype(BF16), win_ref[...])

    return pl.pallas_call(
        body, name="in_proj_fwd", grid=(lp // tm,),
        in_specs=[_rows(tm, (0, D_MODEL)), _whole((1, D_MODEL)), _whole((D_MODEL, D_PROJ))],
        out_specs=_rows(tm, (0, D_PROJ)),
        out_shape=jax.ShapeDtypeStruct((lp, D_PROJ), F32),
        compiler_params=_cparams(("parallel",)),
    )(h, pre_w, w_in_p)


def _attn_prep_fwd(proj, q_norm_w, kv_norm_w, wq_p, wk_p, wv_p, wv_t, cos, sina, sinb, l_real):
    lp = proj.shape[0]
    tm = _row_tile(lp)

    def body(ql_ref, kvl_ref, kr_ref, qw_ref, kw_ref, wq_ref, wk_ref, wv_ref, wvt_ref, cos_ref, sa_ref, sb_ref,
             q_ref, k_ref, v_ref, vt_ref, qt_ref, kt_ref):
        cos_t, sa_t, sb_t = cos_ref[...], sa_ref[...], sb_ref[...]
        qn, _ = _rms_fwd(ql_ref[...], qw_ref[...], Q_LORA)
        kvn, _ = _rms_fwd(kvl_ref[...], kw_ref[...], KV_LORA)
        kvn_b = kvn.astype(BF16)
        qp = _dot(qn.astype(BF16), wq_ref[...])
        kp = _dot(kvn_b, wk_ref[...])
        v_ref[...] = _dot(kvn_b, wv_ref[...]).astype(BF16)
        ones_row = lax.broadcasted_iota(jnp.int32, (HEADS * VT_ROWS, 1), 0) % VT_ROWS == V_HEAD
        vt_ref[...] = jnp.where(ones_row, 1.0, _dot_nt(wvt_ref[...], kvn_b)).astype(BF16)
        lane = lax.broadcasted_iota(jnp.int32, (tm, HEAD_PAD), 1)
        row = lax.broadcasted_iota(jnp.int32, (tm, HEAD_PAD), 0) + pl.program_id(0) * tm
        q_one = jnp.where(lane == MASK_LANE, 1.0, 0.0)
        k_add = _rope_apply(kr_ref[...], cos_t, sa_t, sb_t) + jnp.where((lane == MASK_LANE) & (row >= l_real), NEG_BIG, 0.0)
        for hd in range(HEADS):
            blk = slice(hd * HEAD_PAD, (hd + 1) * HEAD_PAD)
            q_h = _rope_apply(qp[:, blk], cos_t, sa_t, sb_t) * SCALE2 + q_one
            k_h = kp[:, blk] + k_add
            q_ref[:, blk] = q_h.astype(BF16)
            k_ref[:, blk] = k_h.astype(BF16)
            qt_ref[hd * D_QK:(hd + 1) * D_QK, :] = q_h.T[:D_QK].astype(BF16)
            kt_ref[hd * D_QK:(hd + 1) * D_QK, :] = k_h.T[:D_QK].astype(BF16)

    tab = _rows(tm, (0, HEAD_PAD))
    out = jax.ShapeDtypeStruct((lp, D_EXP), BF16)
    out_t = jax.ShapeDtypeStruct((HEADS * D_QK, lp), BF16)
    cols_t = pl.BlockSpec((HEADS * D_QK, tm), lambda i: (0, i))
    return pl.pallas_call(
        body, name="attn_prep_fwd", grid=(lp // tm,),
        in_specs=[_rows(tm, P_QLAT), _rows(tm, P_KVLAT), _rows(tm, P_KROPE), _whole((1, Q_LORA)), _whole((1, KV_LORA)),
                  _whole((Q_LORA, D_EXP)), _whole((KV_LORA, D_EXP)), _whole((KV_LORA, D_EXP)),
                  _whole((HEADS * VT_ROWS, KV_LORA)), tab, tab, tab],
        out_specs=[_rows(tm, (0, D_EXP))] * 3 + [pl.BlockSpec((HEADS * VT_ROWS, tm), lambda i: (0, i)), cols_t, cols_t],
        out_shape=[out, out, out, jax.ShapeDtypeStruct((HEADS * VT_ROWS, lp), BF16), out_t, out_t],
        compiler_params=_cparams(("parallel",)),
    )(proj, proj, proj, q_norm_w, kv_norm_w, wq_p, wk_p, wv_p, wv_t, cos, sina, sinb)


def _flash_fwd(q, k, vt):
    lp = q.shape[0]
    tq = 1280 if lp % 1280 == 0 else 256
    tk = QBLK
    nk = lp // tk

    def body(q_ref, k_ref, vt_ref, o_ref, lse_ref, mblk_ref, p_hbm, acc, m_s, s_a, s_b, p_buf, p_sem):
        hd, qi = pl.program_id(0), pl.program_id(1)

        def p_copy(j, n, slot):
            return pltpu.make_async_copy(p_buf.at[slot, pl.ds(0, n)], p_hbm.at[hd, qi, pl.ds(j, n)], p_sem.at[slot])

        acc[...] = jnp.zeros_like(acc)
        m_s[...] = jnp.full(m_s.shape, NEG_BIG, F32)
        blocks = [slice(c * QBLK, (c + 1) * QBLK) for c in range(tq // QBLK)]

        def scores(j, buf):
            kt = k_ref[pl.ds(pl.multiple_of(j * tk, tk), tk), :]
            for cols in blocks:
                buf[:, cols] = _dot_nt(kt, q_ref[cols, :])

        def consume(j, buf, slot, b):
            vt_t = vt_ref[:, pl.ds(pl.multiple_of(j * tk, tk), tk)]
            m_old, acc_old = m_s[...], acc[...]
            s = [buf[:, cols] for cols in blocks]
            m_new = [jnp.maximum(m_old[:, cols], jnp.max(s_c, axis=0, keepdims=True)) for cols, s_c in zip(blocks, s)]
            p = [jnp.exp2(s_c - m_c).astype(BF16) for s_c, m_c in zip(s, m_new)]
            pv = [_dot(vt_t, p_c) for p_c in p]
            m_new = jnp.concatenate(m_new, axis=1)
            alpha = jnp.exp2(m_old - m_new)
            acc[...] = alpha * acc_old + jnp.concatenate(pv, axis=1)
            m_s[...] = m_new
            mblk_ref[j] = m_new
            p_buf[slot, b] = jnp.concatenate(p, axis=1)

        quads = (nk - 1) // QUAD
        scores(0, s_a)

        def quad(t, _):
            j, slot = QUAD * t, t % 2

            @pl.when(t >= 2)
            def _():
                p_copy(j - 2 * QUAD, QUAD, slot).wait()

            for b in range(QUAD):
                scores(j + b + 1, s_a if b % 2 else s_b)
                consume(j + b, s_b if b % 2 else s_a, slot, b)
            p_copy(j, QUAD, slot).start()
            return 0

        lax.fori_loop(0, quads, quad, 0)
        for back in (2, 1):
            if quads >= back:
                p_copy(QUAD * (quads - back), QUAD, (quads - back) % 2).wait()
        rest = nk - QUAD * quads
        for b in range(rest):
            if b + 1 < rest:
                scores(QUAD * quads + b + 1, s_a if b % 2 else s_b)
            consume(QUAD * quads + b, s_b if b % 2 else s_a, 0, b)
        p_copy(QUAD * quads, rest, 0).start()
        p_copy(QUAD * quads, rest, 0).wait()
        l = acc[V_HEAD:V_HEAD + 1, :]
        o_t = acc[0:V_HEAD, :] / l
        o_ref[...] = jnp.concatenate([o_t, jnp.zeros_like(o_t)], axis=0).T
        lse_ref[...] = m_s[...] + jnp.log2(l)

    return pl.pallas_call(
        body, name="flash_fwd", grid=(HEADS, lp // tq),
        in_specs=[pl.BlockSpec((tq, HEAD_PAD), lambda hd, i: (i, hd)),
                  pl.BlockSpec((lp, HEAD_PAD), lambda hd, i: (0, hd)),
                  pl.BlockSpec((VT_ROWS, lp), lambda hd, i: (hd, 0))],
        out_specs=[pl.BlockSpec((tq, HEAD_PAD), lambda hd, i: (i, hd)),
                   pl.BlockSpec((None, 1, tq), lambda hd, i: (hd, 0, i)),
                   pl.BlockSpec((None, None, nk, 1, tq), lambda hd, i: (hd, i, 0, 0, 0)),
                   pl.BlockSpec(memory_space=pl.ANY)],
        out_shape=[jax.ShapeDtypeStruct((lp, D_EXP), F32), jax.ShapeDtypeStruct((HEADS, 1, lp), F32),
                   jax.ShapeDtypeStruct((HEADS, lp // tq, nk, 1, tq), F32),
                   jax.ShapeDtypeStruct((HEADS, lp // tq, nk, tk, tq), BF16)],
        scratch_shapes=[pltpu.VMEM((VT_ROWS, tq), F32), pltpu.VMEM((1, tq), F32),
                        pltpu.VMEM((tk, tq), F32), pltpu.VMEM((tk, tq), F32),
                        pltpu.VMEM((2, QUAD, tk, tq), BF16), pltpu.SemaphoreType.DMA((2,))],
        compiler_params=_cparams(("parallel", "parallel")),
    )(q, k, vt)


def _unpermute_rows(val, scr, out_ref, seg):
    for c in range(val.shape[1] // 128):
        scr[c] = val[:, c * 128:(c + 1) * 128]
    for k in range(8):
        for c in range(val.shape[1] // 128):
            out_ref[k * seg:(k + 1) * seg, c * 128:(c + 1) * 128] = scr[c, pl.ds(k, seg, stride=8), :]


def _scan_rows(xr_ref, xi_ref, base, n_rows, coef_ref, carry_ref, reverse, tile_fn=None, acc_refs=(), halo=False):
    seg = n_rows // 8
    shifts = (7, 6, 4) if reverse else (1, 2, 4)
    row8 = lax.broadcasted_iota(jnp.int32, (8, SCAN_COLS), 0)
    edge, shift = (7, 7) if reverse else (0, 1)
    for cg in range(N_STATE // SCAN_COLS):
        cols = slice(cg * SCAN_COLS, (cg + 1) * SCAN_COLS)
        ar, ai = coef_ref[8, :, cols], coef_ref[9, :, cols]

        def rows_at(i):
            tau = (seg - 1 - i) if reverse else i
            return tau, pl.ds(pl.multiple_of(base + tau * 8, 8), 8)

        def local(i, carry, cols=cols, ar=ar, ai=ai):
            pr, pi_ = carry
            _, rows = rows_at(i)
            nr = ar * pr - ai * pi_ + xr_ref[rows, cols]
            ni = ar * pi_ + ai * pr + xi_ref[rows, cols]
            xr_ref[rows, cols] = nr
            xi_ref[rows, cols] = ni
            return nr, ni

        zero = jnp.zeros((8, SCAN_COLS), F32)
        fr, fi = lax.fori_loop(0, seg, local, (zero, zero), unroll=SCAN_UNROLL)
        co = [coef_ref[k, :, cols] for k in range(8)]
        for lvl in range(3):
            pr, pi_ = co[2 * lvl], co[2 * lvl + 1]
            sr = pltpu.roll(fr, shifts[lvl], 0)
            si = pltpu.roll(fi, shifts[lvl], 0)
            fr, fi = fr + pr * sr - pi_ * si, fi + pr * si + pi_ * sr
        cr, ci = carry_ref[0:1, cols], carry_ref[1:2, cols]
        fr, fi = fr + co[6] * cr - co[7] * ci, fi + co[6] * ci + co[7] * cr
        carry_ref[0:1, cols] = fr[0:1] if reverse else fr[7:8]
        carry_ref[1:2, cols] = fi[0:1] if reverse else fi[7:8]
        in_r = jnp.where(row8 == edge, cr, pltpu.roll(fr, shift, 0))
        in_i = jnp.where(row8 == edge, ci, pltpu.roll(fi, shift, 0))
        if halo:
            rows = pl.ds(base + n_rows, 8) if reverse else pl.ds(base - 8, 8)
            xr_ref[rows, cols] = in_r
            xi_ref[rows, cols] = in_i

        def fix(i, carry, cols=cols, ar=ar, ai=ai):
            c_r, c_i = carry[0], carry[1]
            tau, rows = rows_at(i)
            nr = xr_ref[rows, cols] + c_r
            ni = xi_ref[rows, cols] + c_i
            xr_ref[rows, cols] = nr
            xi_ref[rows, cols] = ni
            accs = carry[2:]
            if tile_fn is not None:
                accs = tuple(a + d for a, d in zip(accs, tile_fn(tau, cols, nr, ni)))
            return (ar * c_r - ai * c_i, ar * c_i + ai * c_r) + accs

        init = (ar * in_r - ai * in_i, ar * in_i + ai * in_r) + tuple(a[:, cols] for a in acc_refs)
        out = lax.fori_loop(0, seg, fix, init, unroll=SCAN_UNROLL)
        for a, val in zip(acc_refs, out[2:]):
            a[:, cols] = val


def _ssm_fwd(proj, perm, coef, b_re, b_im, c_re, c_im_neg, reverse):
    lp = proj.shape[0]
    t = _ssm_tile(lp)
    n = lp // t
    order = (lambda i: n - 1 - i) if reverse else (lambda i: i)

    def body(u_ref, pm_ref, coef_ref, bre_ref, bim_ref, cre_ref, cim_ref, y_ref, st_ref, xr, xi, carry, stage):
        @pl.when(pl.program_id(0) == 0)
        def _():
            carry[...] = jnp.zeros_like(carry)

        st_ref[...] = carry[0:2, :]
        ub = _dot(pm_ref[...], u_ref[...].astype(BF16)).astype(BF16)
        for j in range(SSM_BLOCKS):
            ch, stt = slice(j * BLK_CH, (j + 1) * BLK_CH), slice(j * BLK_ST, (j + 1) * BLK_ST)
            xr[:, stt] = _dot(ub[:, ch], bre_ref[j])
            xi[:, stt] = _dot(ub[:, ch], bim_ref[j])
        _scan_rows(xr, xi, 0, t, coef_ref, carry, reverse)
        y = jnp.concatenate(
            [_dot(xr[:, j * BLK_ST:(j + 1) * BLK_ST].astype(BF16), cre_ref[j])
             + _dot(xi[:, j * BLK_ST:(j + 1) * BLK_ST].astype(BF16), cim_ref[j]) for j in range(SSM_BLOCKS)], axis=1)
        _unpermute_rows(y, stage, y_ref, t // 8)

    wb, wc = _whole((SSM_BLOCKS, BLK_CH, BLK_ST)), _whole((SSM_BLOCKS, BLK_ST, BLK_CH))
    return pl.pallas_call(
        body, name="ssm_fwd_rev" if reverse else "ssm_fwd", grid=(n,),
        in_specs=[pl.BlockSpec((t, D_SSM), lambda i: (order(i), P_U[0] // D_SSM)), _whole((t, t)), _whole((10, 8, N_STATE)),
                  wb, wb, wc, wc],
        out_specs=[pl.BlockSpec((t, D_SSM), lambda i: (order(i), 0)),
                   pl.BlockSpec((None, 2, N_STATE), lambda i: (order(i), 0, 0))],
        out_shape=[jax.ShapeDtypeStruct((lp, D_SSM), F32), jax.ShapeDtypeStruct((n, 2, N_STATE), F32)],
        scratch_shapes=[pltpu.VMEM((t, N_STATE), F32), pltpu.VMEM((t, N_STATE), F32), pltpu.VMEM((8, N_STATE), F32),
                        pltpu.VMEM((D_SSM // 128, t, 128), F32)],
        compiler_params=_cparams(("arbitrary",)),
    )(proj, perm, coef, b_re, b_im, c_re, c_im_neg)


GELU_C0 = math.sqrt(2.0 / math.pi)
GELU_C1 = 0.044715


def _mid(h, tgt, o_exp, proj, y0, y1, ssm_d, w_glu, w_glu_t, b_glu, ssm_norm_w, attn_norm_w_e, w_out_a, w_out_s,
         w_out_a_t, w_out_s_t, post_w, l_real):
    lp = h.shape[0]
    tm = 256

    def body(h_ref, tga_ref, tgb_ref, o_ref, ga_ref, u_ref, sg_ref, y0_ref, y1_ref, d_ref, wg_ref, wgt_ref, bg_ref, ws_ref,
             wa_ref, woa_ref, wos_ref, woat_ref, wost_ref, pw_ref,
             do_ref, dot_ref, delta_ref, dga_ref, dyp_ref, dsg_ref, dres_ref, dwoa_ref, dwos_ref, dwg_ref, vec_ref):
        @pl.when(pl.program_id(0) == 0)
        def _():
            dwoa_ref[...] = jnp.zeros_like(dwoa_ref)
            dwos_ref[...] = jnp.zeros_like(dwos_ref)
            dwg_ref[...] = jnp.zeros_like(dwg_ref)
            vec_ref[...] = jnp.zeros_like(vec_ref)

        u = u_ref[...]
        ypre = y0_ref[...] + y1_ref[...] + d_ref[...] * u
        th = jnp.tanh(GELU_C0 * (ypre + GELU_C1 * ypre * ypre * ypre))
        gel = 0.5 * ypre * (1.0 + th)
        gel_b = gel.astype(BF16)
        glu = _dot(gel_b, wg_ref[...]) + bg_ref[...]
        g1, g2 = glu[:, :D_SSM], glu[:, D_SSM:]
        sig2 = _sigmoid(g2)
        z = g1 * sig2
        sg = sg_ref[...]
        sgs = _sigmoid(sg)
        sil_s = sg * sgs
        s = z * sil_s
        ys, r_s = _rms_fwd(s, ws_ref[...], D_SSM)

        o = o_ref[...]
        ga = ga_ref[...]
        gas = _sigmoid(ga)
        sil_a = ga * gas
        a = o * sil_a
        ya, r_a = _rms_fwd(a, wa_ref[...], D_ATTN)

        ya_b, ys_b = ya.astype(BF16), ys.astype(BF16)
        y = _dot(ya_b, woa_ref[...]) + _dot(ys_b, wos_ref[...])
        yn, r_y = _rms_fwd(y, pw_ref[...], D_MODEL)
        row = lax.broadcasted_iota(jnp.int32, (tm, 1), 0) + pl.program_id(0) * tm
        valid = (row >= N_META) & (row < l_real)
        tgt = jnp.concatenate([tga_ref[tm - N_META:, :], tgb_ref[:tm - N_META, :]], axis=0)
        err = jnp.where(valid, h_ref[...] + yn - tgt, 0.0)
        loss = 0.5 * jnp.sum(jnp.sum(err * err, axis=-1, keepdims=True), axis=0, keepdims=True) * (1.0 / D_MODEL)
        dout = err * (1.0 / D_MODEL)
        dres_ref[...] = dout

        dy, d_pw = _rms_bwd(y, r_y, pw_ref[...], dout, D_MODEL)
        dy_b = dy.astype(BF16)
        dya = _dot(dy_b, woat_ref[...])
        dys = _dot(dy_b, wost_ref[...])
        dwoa_ref[...] += _dot_tn(ya_b, dy_b)
        dwos_ref[...] += _dot_tn(ys_b, dy_b)

        da, d_wa = _rms_bwd(a, r_a, wa_ref[...], dya, D_ATTN)
        d_o = da * sil_a
        dga_ref[...] = da * o * (gas * (1.0 + ga * (1.0 - gas)))
        do_ref[...] = d_o.astype(BF16)
        for hd in range(HEADS):
            dot_ref[hd * V_HEAD:(hd + 1) * V_HEAD, :] = d_o[:, hd * HEAD_PAD:(hd + 1) * HEAD_PAD].T[:V_HEAD].astype(BF16)
        prod = d_o * o
        lane8 = lax.broadcasted_iota(jnp.int32, (tm, HEADS), 1)
        delta = jnp.zeros((tm, HEADS), F32)
        for hd in range(HEADS):
            delta = jnp.where(lane8 == hd, jnp.sum(prod[:, hd * HEAD_PAD:(hd + 1) * HEAD_PAD], axis=-1, keepdims=True), delta)
        delta_ref[...] = delta

        ds, d_ws = _rms_bwd(s, r_s, ws_ref[...], dys, D_SSM)
        dz = ds * sil_s
        dsg_ref[...] = ds * z * (sgs * (1.0 + sg * (1.0 - sgs)))
        dglu = jnp.concatenate([dz * sig2, dz * g1 * sig2 * (1.0 - sig2)], axis=-1)
        dglu_b = dglu.astype(BF16)
        dwg_ref[...] += _dot_tn(gel_b, dglu_b)
        dgel = _dot(dglu_b, wgt_ref[...])
        dgelu = 0.5 * (1.0 + th) + 0.5 * ypre * (1.0 - th * th) * (GELU_C0 * (1.0 + 3.0 * GELU_C1 * ypre * ypre))
        dyp = dgel * dgelu
        dyp_ref[...] = dyp

        vec_ref[0:1, :] += d_pw
        vec_ref[1:2, :] += d_wa
        vec_ref[2:3, 0:D_SSM] += d_ws
        vec_ref[3:4, 0:D_SSM] += jnp.sum(dyp * u, axis=0, keepdims=True)
        vec_ref[4:5, :] += jnp.sum(dglu, axis=0, keepdims=True)
        vec_ref[5:6, :] += jnp.broadcast_to(loss, (1, D_MODEL))

    full = lambda off: _rows(tm, (off, D_MODEL))
    half = lambda off: _rows(tm, (off, D_SSM))
    last = tgt.shape[0] // tm - 1
    tg_a = pl.BlockSpec((tm, D_MODEL), lambda i: (jnp.clip(i - 1, 0, last), 0))
    tg_b = pl.BlockSpec((tm, D_MODEL), lambda i: (jnp.minimum(i, last), 0))
    return pl.pallas_call(
        body, name="mid", grid=(lp // tm,),
        in_specs=[full(0), tg_a, tg_b, full(0), _rows(tm, P_GATE_A), _rows(tm, P_U), _rows(tm, P_GATE_S), half(0), half(0),
                  _whole((1, D_SSM)), _whole((D_SSM, 2 * D_SSM)), _whole((2 * D_SSM, D_SSM)), _whole((1, 2 * D_SSM)),
                  _whole((1, D_SSM)), _whole((1, D_EXP)), _whole((D_EXP, D_MODEL)), _whole((D_SSM, D_MODEL)),
                  _whole((D_MODEL, D_EXP)), _whole((D_MODEL, D_SSM)), _whole((1, D_MODEL))],
        out_specs=[full(0), pl.BlockSpec((D_ATTN, tm), lambda i: (0, i)), _rows(tm, (0, HEADS)), full(0), half(0), half(0), full(0),
                   _out_whole((D_EXP, D_MODEL)), _out_whole((D_SSM, D_MODEL)), _out_whole((D_SSM, 2 * D_SSM)),
                   _out_whole((8, D_MODEL))],
        out_shape=[jax.ShapeDtypeStruct((lp, D_EXP), BF16), jax.ShapeDtypeStruct((D_ATTN, lp), BF16),
                   jax.ShapeDtypeStruct((lp, HEADS), F32),
                   jax.ShapeDtypeStruct((lp, D_EXP), F32), jax.ShapeDtypeStruct((lp, D_SSM), F32),
                   jax.ShapeDtypeStruct((lp, D_SSM), F32), jax.ShapeDtypeStruct((lp, D_MODEL), F32),
                   jax.ShapeDtypeStruct((D_EXP, D_MODEL), F32), jax.ShapeDtypeStruct((D_SSM, D_MODEL), F32),
                   jax.ShapeDtypeStruct((D_SSM, 2 * D_SSM), F32), jax.ShapeDtypeStruct((8, D_MODEL), F32)],
        compiler_params=_cparams(("arbitrary",)),
    )(h, tgt, tgt, o_exp, proj, proj, proj, y0, y1, ssm_d, w_glu, w_glu_t, b_glu, ssm_norm_w, attn_norm_w_e, w_out_a, w_out_s,
      w_out_a_t, w_out_s_t, post_w)


def _ssm_bwd(proj, dyp, states, perm, coef, coef_adj, b_re, b_im, b_re_t, b_im_t, c_re_t, c_im_neg_t, reverse):
    lp = proj.shape[0]
    t = _ssm_tile(lp)
    n = lp // t
    order = (lambda i: i) if reverse else (lambda i: n - 1 - i)

    def body(u_ref, dy_ref, st_ref, pm_ref, coef_ref, coefa_ref, bre_ref, bim_ref, bret_ref, bimt_ref, cret_ref, cimt_ref,
             du_ref, dbre_ref, dbim_ref, dcre_ref, dcim_ref, da_ref, xr, xi, gr, gi, carry_x, carry_g, stage):
        @pl.when(pl.program_id(0) == 0)
        def _():
            carry_g[...] = jnp.zeros_like(carry_g)
            carry_x[...] = jnp.zeros_like(carry_x)
            dbre_ref[...] = jnp.zeros_like(dbre_ref)
            dbim_ref[...] = jnp.zeros_like(dbim_ref)
            dcre_ref[...] = jnp.zeros_like(dcre_ref)
            dcim_ref[...] = jnp.zeros_like(dcim_ref)
            da_ref[...] = jnp.zeros_like(da_ref)
            for halo in (slice(0, 8), slice(t + 8, t + 16)):
                xr[halo, :] = jnp.zeros((8, N_STATE), F32)
                xi[halo, :] = jnp.zeros((8, N_STATE), F32)

        ub = _dot(pm_ref[...], u_ref[...].astype(BF16)).astype(BF16)
        dyb = _dot(pm_ref[...], dy_ref[...].astype(BF16)).astype(BF16)
        carry_x[0:2, :] = st_ref[...]
        blocks = [(slice(j * BLK_CH, (j + 1) * BLK_CH), slice(j * BLK_ST, (j + 1) * BLK_ST)) for j in range(SSM_BLOCKS)]
        for j, (ch, stt) in enumerate(blocks):
            xr[8:t + 8, stt] = _dot(ub[:, ch], bre_ref[j])
            xi[8:t + 8, stt] = _dot(ub[:, ch], bim_ref[j])
            gr[:, stt] = _dot(dyb[:, ch], cret_ref[j])
            gi[:, stt] = _dot(dyb[:, ch], cimt_ref[j])
        _scan_rows(xr, xi, 8, t, coef_ref, carry_x, reverse, halo=True)

        def tile_fn(tau, cols, g_re, g_im):
            nb = pl.ds(pl.multiple_of((tau + 2) * 8 if reverse else tau * 8, 8), 8)
            xn_r, xn_i = xr[nb, cols], xi[nb, cols]
            return g_re * xn_r + g_im * xn_i, g_im * xn_r - g_re * xn_i

        _scan_rows(gr, gi, 0, t, coefa_ref, carry_g, not reverse, tile_fn=tile_fn, acc_refs=(da_ref.at[0], da_ref.at[1]))

        du = []
        for j, (ch, stt) in enumerate(blocks):
            g_re_b, g_im_b = gr[:, stt].astype(BF16), gi[:, stt].astype(BF16)
            du.append(_dot(g_re_b, bret_ref[j]) + _dot(g_im_b, bimt_ref[j]))
            dbre_ref[j] += _dot_tn(ub[:, ch], g_re_b)
            dbim_ref[j] += _dot_tn(ub[:, ch], g_im_b)
            dcre_ref[j] += _dot_tn(dyb[:, ch], xr[8:t + 8, stt].astype(BF16))
            dcim_ref[j] -= _dot_tn(dyb[:, ch], xi[8:t + 8, stt].astype(BF16))
        _unpermute_rows(jnp.concatenate(du, axis=1), stage, du_ref, t // 8)

    dense = jax.ShapeDtypeStruct((SSM_BLOCKS, BLK_CH, BLK_ST), F32)
    wb, wc = _whole((SSM_BLOCKS, BLK_CH, BLK_ST)), _whole((SSM_BLOCKS, BLK_ST, BLK_CH))
    acc = _out_whole((SSM_BLOCKS, BLK_CH, BLK_ST))
    return pl.pallas_call(
        body, name="ssm_bwd_rev" if reverse else "ssm_bwd", grid=(n,),
        in_specs=[pl.BlockSpec((t, D_SSM), lambda i: (order(i), P_U[0] // D_SSM)),
                  pl.BlockSpec((t, D_SSM), lambda i: (order(i), 0)),
                  pl.BlockSpec((None, 2, N_STATE), lambda i: (order(i), 0, 0)), _whole((t, t)),
                  _whole((10, 8, N_STATE)), _whole((10, 8, N_STATE)), wb, wb, wc, wc, wb, wb],
        out_specs=[pl.BlockSpec((t, D_SSM), lambda i: (order(i), 0)), acc, acc, acc, acc, _out_whole((2, 8, N_STATE))],
        out_shape=[jax.ShapeDtypeStruct((lp, D_SSM), F32), dense, dense, dense, dense,
                   jax.ShapeDtypeStruct((2, 8, N_STATE), F32)],
        scratch_shapes=[pltpu.VMEM((t + 16, N_STATE), F32), pltpu.VMEM((t + 16, N_STATE), F32),
                        pltpu.VMEM((t, N_STATE), F32), pltpu.VMEM((t, N_STATE), F32),
                        pltpu.VMEM((8, N_STATE), F32), pltpu.VMEM((8, N_STATE), F32),
                        pltpu.VMEM((D_SSM // 128, t, 128), F32)],
        compiler_params=_cparams(("arbitrary",)),
    )(proj, dyp, states, perm, coef, coef_adj, b_re, b_im, b_re_t, b_im_t, c_re_t, c_im_neg_t)


def _flash_bwd(q, v, d_o, q_t, k_t, do_t, lse_row, delta_row, mblk, p_all):
    lp = q.shape[0]
    tq = 1280 if lp % 1280 == 0 else 256
    tk = QBLK
    nk = lp // tk
    d_qk = QK_NOPE + QK_ROPE
    grp = 5 if nk % 5 == 0 else 1
    n_groups = nk // grp

    def body(do_ref, qt_ref, dot_ref, lse_ref, delta_ref, mblk_ref, v_ref, kt_ref, p_hbm, dq_ref, dk_ref, dv_ref,
             dq_acc, p_buf, p_sem):
        hd, qi = pl.program_id(0), pl.program_id(1)

        def p_copy(t, slot):
            return pltpu.make_async_copy(p_hbm.at[hd, qi, pl.ds(t * grp, grp)], p_buf.at[slot], p_sem.at[slot])

        p_copy(0, 0).start()

        @pl.when(qi == 0)
        def _():
            dk_ref[...] = jnp.zeros_like(dk_ref)
            dv_ref[...] = jnp.zeros_like(dv_ref)

        dq_acc[...] = jnp.zeros_like(dq_acc)
        lse, delta = lse_ref[...], delta_ref[...]
        q_cols, do_cols = qt_ref[...], dot_ref[...]
        blocks = [slice(c * QBLK, (c + 1) * QBLK) for c in range(tq // QBLK)]

        def group(t, _):
            base = pl.multiple_of(t * (grp * tk), grp * tk)
            slot = t % 2
            p_copy(t, slot).wait()

            @pl.when(t + 1 < n_groups)
            def _():
                p_copy(t + 1, 1 - slot).start()

            dq = dq_acc[...]
            dvs, dks = [], []
            for u in range(grp):
                j = t * grp + u
                ks = pl.multiple_of(base + u * tk, tk)
                v_rows = v_ref[pl.ds(ks, tk), :]
                dpt = [_dot_nt(v_rows, do_ref[cols, :]) for cols in blocks]
                pt = p_buf[slot, u].astype(F32) * jnp.exp2(mblk_ref[j] - lse)
                pt_b = pt.astype(BF16)
                dst_b = jnp.concatenate([(pt[:, cols] * (dp_c - delta[:, cols])).astype(BF16)
                                         for dp_c, cols in zip(dpt, blocks)], axis=1)
                dvs.append(_dot_nt(do_cols, pt_b))
                dks.append(_dot_nt(q_cols, dst_b))
                dq = dq + _dot(kt_ref[:, pl.ds(ks, tk)], dst_b)
            dq_acc[...] = dq
            dv_ref[:, pl.ds(base, grp * tk)] += jnp.concatenate(dvs, axis=1)
            dk_ref[:, pl.ds(base, grp * tk)] += jnp.concatenate(dks, axis=1) * (1.0 / LOG2E)
            return 0

        lax.fori_loop(0, n_groups, group, 0)
        dq_ref[...] = jnp.concatenate([dq_acc[...], jnp.zeros((HEAD_PAD - d_qk, tq), F32)], axis=0).T

    tile = pl.BlockSpec((tq, HEAD_PAD), lambda hd, i: (i, hd))
    head = pl.BlockSpec((lp, HEAD_PAD), lambda hd, i: (0, hd))
    rowv = pl.BlockSpec((None, 1, tq), lambda hd, i: (hd, 0, i))
    return pl.pallas_call(
        body, name="flash_bwd", grid=(HEADS, lp // tq),
        in_specs=[tile, pl.BlockSpec((d_qk, tq), lambda hd, i: (hd, i)), pl.BlockSpec((V_HEAD, tq), lambda hd, i: (hd, i)),
                  rowv, rowv, pl.BlockSpec((None, None, nk, 1, tq), lambda hd, i: (hd, i, 0, 0, 0)), head,
                  pl.BlockSpec((d_qk, lp), lambda hd, i: (hd, 0)), pl.BlockSpec(memory_space=pl.ANY)],
        out_specs=[tile, pl.BlockSpec((d_qk, lp), lambda hd, i: (hd, 0)), pl.BlockSpec((V_HEAD, lp), lambda hd, i: (hd, 0))],
        out_shape=[jax.ShapeDtypeStruct((lp, D_EXP), F32), jax.ShapeDtypeStruct((HEADS * d_qk, lp), F32),
                   jax.ShapeDtypeStruct((HEADS * V_HEAD, lp), F32)],
        scratch_shapes=[pltpu.VMEM((d_qk, tq), F32), pltpu.VMEM((2, grp, tk, tq), BF16), pltpu.SemaphoreType.DMA((2,))],
        compiler_params=_cparams(("parallel", "arbitrary")),
    )(d_o, q_t, do_t, lse_row, delta_row, mblk, v, k_t, p_all)


def _attn_prep_bwd(dq, dk_t, dv_t, proj, q_norm_w, kv_norm_w, wq_pt, wk_pt, wv_pt, cos, sina, sinb):
    lp = proj.shape[0]
    tm = _row_tile(lp)

    def body(dq_ref, dk_ref, dv_ref, ql_ref, kvl_ref, qw_ref, kw_ref, wqt_ref, wkt_ref, wvt_ref, cos_ref, sa_ref, sb_ref,
             dql_ref, dkvl_ref, dkr_ref, dwq_ref, dwk_ref, dwv_ref, vec_ref):
        @pl.when(pl.program_id(0) == 0)
        def _():
            dwq_ref[...] = jnp.zeros_like(dwq_ref)
            dwk_ref[...] = jnp.zeros_like(dwk_ref)
            dwv_ref[...] = jnp.zeros_like(dwv_ref)
            vec_ref[...] = jnp.zeros_like(vec_ref)

        cos_t, sa_t, sb_t = cos_ref[...], sa_ref[...], sb_ref[...]

        def head_rows(t_ref, per):
            pad = jnp.zeros((HEAD_PAD - per, tm), F32)
            return jnp.concatenate(
                [jnp.concatenate([t_ref[hd * per:(hd + 1) * per, :], pad], axis=0).T for hd in range(HEADS)], axis=-1)

        dkp = head_rows(dk_ref, D_QK)
        dqp = jnp.concatenate(
            [_rope_transpose(dq_ref[:, hd * HEAD_PAD:(hd + 1) * HEAD_PAD] * SCALE, cos_t, sa_t, sb_t) for hd in range(HEADS)],
            axis=-1)
        dkr = dkp[:, 0:HEAD_PAD]
        for hd in range(1, HEADS):
            dkr = dkr + dkp[:, hd * HEAD_PAD:(hd + 1) * HEAD_PAD]
        dkr_ref[...] = _rope_transpose(dkr, cos_t, sa_t, sb_t)

        qn, r_q = _rms_fwd(ql_ref[...], qw_ref[...], Q_LORA)
        kvn, r_kv = _rms_fwd(kvl_ref[...], kw_ref[...], KV_LORA)
        dqp_b, dkp_b, dv_b = dqp.astype(BF16), dkp.astype(BF16), head_rows(dv_ref, V_HEAD).astype(BF16)
        dqn = _dot(dqp_b, wqt_ref[...])
        dkvn = _dot(dkp_b, wkt_ref[...]) + _dot(dv_b, wvt_ref[...])
        dwq_ref[...] += _dot_tn(qn.astype(BF16), dqp_b)
        dwk_ref[...] += _dot_tn(kvn.astype(BF16), dkp_b)
        dwv_ref[...] += _dot_tn(kvn.astype(BF16), dv_b)
        dql, d_qw = _rms_bwd(ql_ref[...], r_q, qw_ref[...], dqn, Q_LORA)
        dkvl, d_kw = _rms_bwd(kvl_ref[...], r_kv, kw_ref[...], dkvn, KV_LORA)
        dql_ref[...] = dql
        dkvl_ref[...] = dkvl
        vec_ref[0:1, :] += d_qw
        vec_ref[1:2, 0:KV_LORA] += d_kw

    tab = _rows(tm, (0, HEAD_PAD))
    full = _rows(tm, (0, D_EXP))
    return pl.pallas_call(
        body, name="attn_prep_bwd", grid=(lp // tm,),
        in_specs=[full, pl.BlockSpec((HEADS * D_QK, tm), lambda i: (0, i)), pl.BlockSpec((D_ATTN, tm), lambda i: (0, i)),
                  _rows(tm, P_QLAT), _rows(tm, P_KVLAT), _whole((1, Q_LORA)), _whole((1, KV_LORA)),
                  _whole((D_EXP, Q_LORA)), _whole((D_EXP, KV_LORA)), _whole((D_EXP, KV_LORA)), tab, tab, tab],
        out_specs=[_rows(tm, (0, Q_LORA)), _rows(tm, (0, KV_LORA)), _rows(tm, (0, HEAD_PAD)),
                   _out_whole((Q_LORA, D_EXP)), _out_whole((KV_LORA, D_EXP)), _out_whole((KV_LORA, D_EXP)),
                   _out_whole((8, Q_LORA))],
        out_shape=[jax.ShapeDtypeStruct((lp, Q_LORA), F32), jax.ShapeDtypeStruct((lp, KV_LORA), F32),
                   jax.ShapeDtypeStruct((lp, HEAD_PAD), F32), jax.ShapeDtypeStruct((Q_LORA, D_EXP), F32),
                   jax.ShapeDtypeStruct((KV_LORA, D_EXP), F32), jax.ShapeDtypeStruct((KV_LORA, D_EXP), F32),
                   jax.ShapeDtypeStruct((8, Q_LORA), F32)],
        compiler_params=_cparams(("arbitrary",)),
    )(dq, dk_t, dv_t, proj, proj, q_norm_w, kv_norm_w, wq_pt, wk_pt, wv_pt, cos, sina, sinb)


def _in_proj_bwd(h, pre_w, dres, dga, du0, du1, dyp, ssm_d, dsg, dql, dkvl, dkr, w_in_pt):
    lp = h.shape[0]
    tm = 256
    pieces = (P_GATE_A, P_U, P_GATE_S, P_QLAT, P_KVLAT, P_KROPE)

    def body(h_ref, w_ref, dres_ref, dga_ref, du0_ref, du1_ref, dyp_ref, d_ref, dsg_ref, dql_ref, dkvl_ref, dkr_ref, wt_ref,
             dh_ref, dw_ref, vec_ref):
        @pl.when(pl.program_id(0) == 0)
        def _():
            dw_ref[...] = jnp.zeros_like(dw_ref)
            vec_ref[...] = jnp.zeros_like(vec_ref)

        hv = h_ref[...]
        xn, r = _rms_fwd(hv, w_ref[...], D_MODEL)
        xn_b = xn.astype(BF16)
        du = du0_ref[...] + du1_ref[...] + dyp_ref[...] * d_ref[...]
        grads = (dga_ref[...], du, dsg_ref[...], dql_ref[...], dkvl_ref[...], dkr_ref[...])
        dxn = jnp.zeros((tm, D_MODEL), F32)
        for (off, width), g in zip(pieces, grads):
            g_b = g.astype(BF16)
            dxn = dxn + _dot(g_b, wt_ref[off:off + width, :])
            dw_ref[:, off:off + width] += _dot_tn(xn_b, g_b)
        dx, d_w = _rms_bwd(hv, r, w_ref[...], dxn, D_MODEL)
        dh_ref[...] = dres_ref[...] + dx
        vec_ref[0:1, :] += d_w

    full = _rows(tm, (0, D_MODEL))
    half = _rows(tm, (0, D_SSM))
    return pl.pallas_call(
        body, name="in_proj_bwd", grid=(lp // tm,),
        in_specs=[full, _whole((1, D_MODEL)), full, full, half, half, half, _whole((1, D_SSM)), half,
                  _rows(tm, (0, Q_LORA)), _rows(tm, (0, KV_LORA)), _rows(tm, (0, HEAD_PAD)), _whole((D_PROJ, D_MODEL))],
        out_specs=[full, _out_whole((D_MODEL, D_PROJ)), _out_whole((8, D_MODEL))],
        out_shape=[jax.ShapeDtypeStruct((lp, D_MODEL), F32), jax.ShapeDtypeStruct((D_MODEL, D_PROJ), F32),
                   jax.ShapeDtypeStruct((8, D_MODEL), F32)],
        compiler_params=_cparams(("arbitrary",)),
    )(h, pre_w, dres, dga, du0, du1, dyp, ssm_d, dsg, dql, dkvl, dkr, w_in_pt)


def _other_chips(x, y):
    return [(1 - x, y), (x, 1 - y), (1 - x, 1 - y)]


def _gather_weights(w_bf16, meta):
    any_spec = pl.BlockSpec(memory_space=pl.ANY)
    halves = (w_bf16.shape[0] // 2, meta.shape[0] // 2)

    def body(w_ref, m_ref, wout_ref, mout_ref, send_sems, recv_sems, local_sems):
        x, y, c = lax.axis_index("x"), lax.axis_index("y"), lax.axis_index("c")
        me, sibling = 2 * x + y, (x, y, 1 - c)
        srcs, dsts = (w_ref, m_ref), (wout_ref, mout_ref)

        def half(n, cc):
            return pl.ds(pl.multiple_of(cc * halves[n], 8), halves[n])

        def copy(n, sem, src, chip, cc, to):
            return pltpu.make_async_remote_copy(src_ref=src, dst_ref=dsts[n].at[chip, half(n, cc)], send_sem=send_sems.at[sem],
                                                recv_sem=recv_sems.at[sem], device_id=to, device_id_type=MESH)

        own = [pltpu.make_async_copy(srcs[n], dsts[n].at[me], local_sems.at[n]) for n in range(2)]
        for cp in own:
            cp.start()
        chips = _other_chips(x, y)
        first = [copy(n, 2 * j + n, srcs[n].at[half(n, c)], me, c, (tx, ty, c)) for j, (tx, ty) in enumerate(chips) for n in range(2)]
        for cp in first:
            cp.start()
        passed = []
        for j, (tx, ty) in enumerate(chips):
            for n in range(2):
                landed = dsts[n].at[2 * tx + ty, half(n, c)]
                copy(n, 2 * j + n, landed, 2 * tx + ty, c, (tx, ty, c)).wait_recv()
                passed.append(copy(n, 6 + 2 * j + n, landed, 2 * tx + ty, c, sibling))
                passed[-1].start()
        for j, (tx, ty) in enumerate(chips):
            for n in range(2):
                copy(n, 6 + 2 * j + n, dsts[n].at[2 * tx + ty, half(n, 1 - c)], 2 * tx + ty, 1 - c, sibling).wait_recv()
        for cp in first + passed:
            cp.wait_send()
        for cp in own:
            cp.wait()

    return pl.pallas_call(
        body, name="gather_weights",
        in_specs=[any_spec, any_spec], out_specs=[any_spec, any_spec],
        out_shape=[jax.ShapeDtypeStruct((4,) + w_bf16.shape, w_bf16.dtype), jax.ShapeDtypeStruct((4,) + meta.shape, meta.dtype)],
        scratch_shapes=[pltpu.SemaphoreType.DMA((12,)), pltpu.SemaphoreType.DMA((12,)), pltpu.SemaphoreType.DMA((2,))],
    )(w_bf16, meta)


def _swap_sibling(g):
    any_spec = pl.BlockSpec(memory_space=pl.ANY)

    def body(g_ref, out_ref, send_sem, recv_sem):
        x, y, c = lax.axis_index("x"), lax.axis_index("y"), lax.axis_index("c")
        cp = pltpu.make_async_remote_copy(src_ref=g_ref, dst_ref=out_ref, send_sem=send_sem, recv_sem=recv_sem,
                                          device_id=(x, y, 1 - c), device_id_type=MESH)
        cp.start()
        cp.wait()

    return pl.pallas_call(
        body, name="swap_sibling", in_specs=[any_spec], out_specs=any_spec,
        out_shape=jax.ShapeDtypeStruct(g.shape, g.dtype),
        scratch_shapes=[pltpu.SemaphoreType.DMA(()), pltpu.SemaphoreType.DMA(())],
    )(g)


def _pair_sum(a, b):
    rows = a.shape[0]
    tm = _pick_tile(rows, 1024)

    def body(a_ref, b_ref, o_ref):
        o_ref[...] = a_ref[...] + b_ref[...]

    spec = pl.BlockSpec((tm, 1024), lambda i: (i, 0))
    return pl.pallas_call(body, name="pair_sum", grid=(rows // tm,), in_specs=[spec, spec], out_specs=spec,
                          out_shape=jax.ShapeDtypeStruct(a.shape, F32), compiler_params=_cparams(("parallel",)))(a, b)


def _scatter_chips(s, rs, rsm):
    any_spec = pl.BlockSpec(memory_space=pl.ANY)
    lens = (rs // 2, rsm // 2)

    def body(s_ref, out_ref, send_sems, recv_sems, local_sems):
        x, y, c = lax.axis_index("x"), lax.axis_index("y"), lax.axis_index("c")
        me, sibling = 2 * x + y, (x, y, 1 - c)

        def src_rows(n, target, cc):
            start = (target * rs if n == 0 else 4 * rs) + cc * lens[n]
            return s_ref.at[pl.ds(pl.multiple_of(start, 8), lens[n])]

        def dst_rows(n, cc):
            return pl.ds(pl.multiple_of((0 if n == 0 else rs) + cc * lens[n], 8), lens[n])

        def copy(n, sem, src, chip, cc, to):
            return pltpu.make_async_remote_copy(src_ref=src, dst_ref=out_ref.at[chip, dst_rows(n, cc)], send_sem=send_sems.at[sem],
                                                recv_sem=recv_sems.at[sem], device_id=to, device_id_type=MESH)

        own = [pltpu.make_async_copy(s_ref.at[pl.ds(pl.multiple_of(me * rs, 8), rs)], out_ref.at[me, pl.ds(0, rs)], local_sems.at[0]),
               pltpu.make_async_copy(s_ref.at[pl.ds(4 * rs, rsm)], out_ref.at[me, pl.ds(rs, rsm)], local_sems.at[1])]
        for cp in own:
            cp.start()
        chips = _other_chips(x, y)
        first = [copy(n, 2 * j + n, src_rows(n, 2 * tx + ty, c), me, c, (tx, ty, c))
                 for j, (tx, ty) in enumerate(chips) for n in range(2)]
        for cp in first:
            cp.start()
        passed = []
        for j, (tx, ty) in enumerate(chips):
            for n in range(2):
                landed = out_ref.at[2 * tx + ty, dst_rows(n, c)]
                copy(n, 2 * j + n, landed, 2 * tx + ty, c, (tx, ty, c)).wait_recv()
                passed.append(copy(n, 6 + 2 * j + n, landed, 2 * tx + ty, c, sibling))
                passed[-1].start()
        for j, (tx, ty) in enumerate(chips):
            for n in range(2):
                copy(n, 6 + 2 * j + n, out_ref.at[2 * tx + ty, dst_rows(n, 1 - c)], 2 * tx + ty, 1 - c, sibling).wait_recv()
        for cp in first + passed:
            cp.wait_send()
        for cp in own:
            cp.wait()

    return pl.pallas_call(
        body, name="scatter_chips", in_specs=[any_spec], out_specs=any_spec,
        out_shape=jax.ShapeDtypeStruct((4, rs + rsm, 1024), F32),
        scratch_shapes=[pltpu.SemaphoreType.DMA((12,)), pltpu.SemaphoreType.DMA((12,)), pltpu.SemaphoreType.DMA((2,))],
    )(s)


def _adamw(parts, w, m, v):
    rows = w.shape[0]
    tm = _pick_tile(rows, 256)
    c1 = 1.0 / (1.0 - ADAM_B1 ** ADAM_STEP)
    c2 = 1.0 / (1.0 - ADAM_B2 ** ADAM_STEP)

    def body(p_ref, w_ref, m_ref, v_ref, g_ref, d_ref, nm_ref, nv_ref):
        g = ((p_ref[0] + p_ref[1]) + p_ref[2]) + p_ref[3]
        nm = ADAM_B1 * m_ref[...] + (1.0 - ADAM_B1) * g
        nv = ADAM_B2 * v_ref[...] + (1.0 - ADAM_B2) * (g * g)
        g_ref[...] = g
        nm_ref[...] = nm
        nv_ref[...] = nv
        d_ref[...] = -ADAM_LR * ((nm * c1) / (jnp.sqrt(nv * c2) + ADAM_EPS) + ADAM_WD * w_ref[...])

    spec = pl.BlockSpec((tm, 1024), lambda i: (i, 0))
    out = jax.ShapeDtypeStruct(w.shape, F32)
    return pl.pallas_call(
        body, name="adamw", grid=(rows // tm,),
        in_specs=[pl.BlockSpec((4, tm, 1024), lambda i: (0, i, 0)), spec, spec, spec],
        out_specs=[spec] * 4, out_shape=[out] * 4, compiler_params=_cparams(("parallel",)),
    )(parts, w, m, v)


def _expand_heads(a, axis, per_head):
    a = jnp.moveaxis(a, axis, -1)
    lead = a.shape[:-1]
    a = a.reshape(lead + (HEADS, per_head))
    a = jnp.pad(a, [(0, 0)] * len(lead) + [(0, 0), (0, HEAD_PAD - per_head)])
    return jnp.moveaxis(a.reshape(lead + (D_EXP,)), -1, axis)


def _compact_heads(a, axis, start, size):
    a = jnp.moveaxis(a, axis, -1)
    lead = a.shape[:-1]
    a = a.reshape(lead + (HEADS, HEAD_PAD))[..., start:start + size]
    return jnp.moveaxis(a.reshape(lead + (HEADS * size,)), -1, axis)


def _block_diag(w):
    g, a, b = w.shape
    per = g // SSM_BLOCKS
    eye = jnp.eye(per, dtype=w.dtype)
    return jnp.einsum("jgab,gk->jgakb", w.reshape(SSM_BLOCKS, per, a, b), eye).reshape(SSM_BLOCKS, per * a, per * b)


def _block_diag_extract(dense, a, b):
    per = N_GROUPS // SSM_BLOCKS
    d5 = dense.reshape(SSM_BLOCKS, per, a, per, b)
    return jnp.einsum("jgakb,gk->jgab", d5, jnp.eye(per, dtype=dense.dtype)).reshape(N_GROUPS, a, b)


def _discretise(a_re, a_im, log_dt, b_re, b_im):
    dt = jnp.exp(log_dt)[:, None]
    mag = jnp.exp(a_re * dt)
    abar_re = mag * jnp.cos(a_im * dt)
    abar_im = mag * jnp.sin(a_im * dt)
    num_re = abar_re - 1.0
    num_im = abar_im
    den = a_re * a_re + a_im * a_im
    coef_re = (num_re * a_re + num_im * a_im) / den
    coef_im = (num_im * a_re - num_re * a_im) / den
    bbar_re = coef_re[..., None] * b_re - coef_im[..., None] * b_im
    bbar_im = coef_re[..., None] * b_im + coef_im[..., None] * b_re
    return abar_re, abar_im, bbar_re, bbar_im


def _scan_coef(ar, ai, reverse, seg):
    ar, ai = ar.reshape(1, N_STATE), ai.reshape(1, N_STATE)
    cmul = lambda x, y: (x[0] * y[0] - x[1] * y[1], x[0] * y[1] + x[1] * y[0])
    p, sq, n = None, (ar, ai), seg
    while n:
        if n & 1:
            p = sq if p is None else cmul(p, sq)
        sq, n = cmul(sq, sq), n >> 1
    pows = [p]
    for _ in range(7):
        pows.append(cmul(pows[-1], p))
    row = jnp.arange(8)[:, None]
    out = []
    for k in (1, 2, 4):
        keep = (row < 8 - k) if reverse else (row >= k)
        out += [jnp.where(keep, pows[k - 1][0], 0.0), jnp.where(keep, pows[k - 1][1], 0.0)]
    order = list(range(7, -1, -1)) if reverse else list(range(8))
    out += [jnp.concatenate([pows[k][0] for k in order], axis=0), jnp.concatenate([pows[k][1] for k in order], axis=0)]
    out += [jnp.broadcast_to(ar, (8, N_STATE)), jnp.broadcast_to(ai, (8, N_STATE))]
    return jnp.stack(out).astype(F32)


def _flat_rows(a, rows):
    flat = a.reshape(-1)
    return jnp.pad(flat, (0, rows * 1024 - flat.shape[0])).reshape(rows, 1024)


def _pack(named, order):
    rows = [-(-math.prod(named[n].shape) // 1024) for n in order]
    total = -(-sum(rows) // 32) * 32
    parts = [_flat_rows(named[n], r) for n, r in zip(order, rows)]
    if total > sum(rows):
        parts.append(jnp.zeros((total - sum(rows), 1024), parts[0].dtype))
    return jnp.concatenate(parts, axis=0)


def _unpack(packed, shapes, order):
    out, at = {}, 0
    for n in order:
        size = math.prod(shapes[n])
        rows = -(-size // 1024)
        out[n] = packed[at:at + rows].reshape(-1)[:size].reshape(shapes[n])
        at += rows
    return out


def _shard_cols(a, k):
    w = a.shape[-1] // 4
    return a[..., k * w:(k + 1) * w]


def kernel(x, meta_tokens, pre_norm_w, post_norm_w, w_in, q_norm_w, w_q_up, kv_norm_w, w_kv_up, attn_out_norm_w, ssm_a_re, ssm_a_im, ssm_log_dt, ssm_b_re, ssm_b_im, ssm_c_re, ssm_c_im, ssm_d, w_glu, b_glu, ssm_out_norm_w, w_out, loss_target, m_meta_tokens, m_pre_norm_w, m_post_norm_w, m_w_in, m_q_norm_w, m_w_q_up, m_kv_norm_w, m_w_kv_up, m_attn_out_norm_w, m_ssm_a_re, m_ssm_a_im, m_ssm_log_dt, m_ssm_b_re, m_ssm_b_im, m_ssm_c_re, m_ssm_c_im, m_ssm_d, m_w_glu, m_b_glu, m_ssm_out_norm_w, m_w_out, v_meta_tokens, v_pre_norm_w, v_post_norm_w, v_w_in, v_q_norm_w, v_w_q_up, v_kv_norm_w, v_w_kv_up, v_attn_out_norm_w, v_ssm_a_re, v_ssm_a_im, v_ssm_log_dt, v_ssm_b_re, v_ssm_b_im, v_ssm_c_re, v_ssm_c_im, v_ssm_d, v_w_glu, v_b_glu, v_ssm_out_norm_w, v_w_out):
    local = dict(meta_tokens=meta_tokens, pre_norm_w=pre_norm_w, post_norm_w=post_norm_w, w_in=w_in, q_norm_w=q_norm_w,
                 w_q_up=w_q_up, kv_norm_w=kv_norm_w, w_kv_up=w_kv_up, attn_out_norm_w=attn_out_norm_w, ssm_a_re=ssm_a_re,
                 ssm_a_im=ssm_a_im, ssm_log_dt=ssm_log_dt, ssm_b_re=ssm_b_re, ssm_b_im=ssm_b_im, ssm_c_re=ssm_c_re,
                 ssm_c_im=ssm_c_im, ssm_d=ssm_d, w_glu=w_glu, b_glu=b_glu, ssm_out_norm_w=ssm_out_norm_w, w_out=w_out)
    mom_m = dict(meta_tokens=m_meta_tokens, pre_norm_w=m_pre_norm_w, post_norm_w=m_post_norm_w, w_in=m_w_in,
                 q_norm_w=m_q_norm_w, w_q_up=m_w_q_up, kv_norm_w=m_kv_norm_w, w_kv_up=m_w_kv_up,
                 attn_out_norm_w=m_attn_out_norm_w, ssm_a_re=m_ssm_a_re, ssm_a_im=m_ssm_a_im, ssm_log_dt=m_ssm_log_dt,
                 ssm_b_re=m_ssm_b_re, ssm_b_im=m_ssm_b_im, ssm_c_re=m_ssm_c_re, ssm_c_im=m_ssm_c_im, ssm_d=m_ssm_d,
                 w_glu=m_w_glu, b_glu=m_b_glu, ssm_out_norm_w=m_ssm_out_norm_w, w_out=m_w_out)
    mom_v = dict(meta_tokens=v_meta_tokens, pre_norm_w=v_pre_norm_w, post_norm_w=v_post_norm_w, w_in=v_w_in,
                 q_norm_w=v_q_norm_w, w_q_up=v_w_q_up, kv_norm_w=v_kv_norm_w, w_kv_up=v_w_kv_up,
                 attn_out_norm_w=v_attn_out_norm_w, ssm_a_re=v_ssm_a_re, ssm_a_im=v_ssm_a_im, ssm_log_dt=v_ssm_log_dt,
                 ssm_b_re=v_ssm_b_re, ssm_b_im=v_ssm_b_im, ssm_c_re=v_ssm_c_re, ssm_c_im=v_ssm_c_im, ssm_d=v_ssm_d,
                 w_glu=v_w_glu, b_glu=v_b_glu, ssm_out_norm_w=v_ssm_out_norm_w, w_out=v_w_out)
    shapes = {n: local[n].shape for n in WEIGHTS}
    mat = ("w_in", "w_q_up", "w_kv_up", "w_glu", "w_out")

    seq = x.shape[1]
    l_real = N_META + seq
    lp = -(-l_real // 1280) * 1280 if l_real > 1280 else -(-l_real // QBLK) * QBLK
    assert seq % 128 == 0 and lp % QBLK == 0

    w_shard = _pack({n: local[n].astype(BF16) for n in mat}, mat)
    w_shard = jnp.pad(w_shard, ((0, -w_shard.shape[0] % 16), (0, 0)))
    w_all, meta_all = _gather_weights(w_shard, meta_tokens)
    mat_shapes = {n: shapes[n] for n in mat}
    per_chip = [_unpack(w_all[k], mat_shapes, mat) for k in range(4)]
    w_in_f = jnp.concatenate([p["w_in"][0] for p in per_chip], axis=1)
    w_q_f = jnp.concatenate([p["w_q_up"][0] for p in per_chip], axis=1)
    w_kv_f = jnp.concatenate([p["w_kv_up"][0] for p in per_chip], axis=1)
    w_glu_f = jnp.concatenate([p["w_glu"][0] for p in per_chip], axis=1)
    w_out_f = jnp.concatenate([p["w_out"][0] for p in per_chip], axis=0)
    meta_f = jnp.concatenate([meta_all[k] for k in range(4)], axis=1)

    o_q, o_kv, o_kr, o_ga, o_u, o_gs = 0, 256, 384, 416, 928, 1440
    krope_cols = jnp.pad(w_in_f[:, o_kr:o_ga], ((0, 0), (QK_NOPE, HEAD_PAD - QK_NOPE - QK_ROPE)))
    w_in_p = jnp.concatenate([_expand_heads(w_in_f[:, o_ga:o_u], 1, V_HEAD), w_in_f[:, o_u:o_gs], w_in_f[:, o_gs:],
                              w_in_f[:, o_q:o_kv], w_in_f[:, o_kv:o_kr], krope_cols], axis=1)
    wq_p = _expand_heads(w_q_f, 1, QK_NOPE + QK_ROPE)
    kv3 = w_kv_f.reshape(KV_LORA, HEADS, QK_NOPE + V_HEAD)
    wk_p = _expand_heads(kv3[:, :, :QK_NOPE].reshape(KV_LORA, HEADS * QK_NOPE), 1, QK_NOPE)
    wv_c = kv3[:, :, QK_NOPE:].reshape(KV_LORA, HEADS * V_HEAD)
    wv_p = _expand_heads(wv_c, 1, V_HEAD)
    wv_t = jnp.pad(wv_c.T.reshape(HEADS, V_HEAD, KV_LORA), ((0, 0), (0, VT_ROWS - V_HEAD), (0, 0))).reshape(HEADS * VT_ROWS, KV_LORA)
    w_out_a = _expand_heads(w_out_f[:D_ATTN], 0, V_HEAD)
    w_out_s = w_out_f[D_ATTN:]
    attn_norm_e = _expand_heads(attn_out_norm_w, 1, V_HEAD)

    pos = jnp.arange(lp, dtype=jnp.int32)
    half = QK_ROPE // 2
    inv = ROPE_THETA ** (-jnp.arange(half, dtype=F32) / half)
    ang = pos.astype(F32)[:, None] * inv[None, :]
    cos16, sin16 = jnp.cos(ang), jnp.sin(ang)
    ones, zeros = jnp.ones((lp, QK_NOPE), F32), jnp.zeros((lp, QK_NOPE), F32)
    tail1, tail0 = jnp.ones((lp, HEAD_PAD - MASK_LANE), F32), jnp.zeros((lp, HEAD_PAD - MASK_LANE), F32)
    z16 = jnp.zeros((lp, half), F32)
    cos = jnp.concatenate([ones, cos16, cos16, tail1], axis=1)
    sina = jnp.concatenate([zeros, z16, sin16, tail0], axis=1)
    sinb = jnp.concatenate([zeros, -sin16, z16, tail0], axis=1)

    disc_in = (ssm_a_re[0], ssm_a_im[0], ssm_log_dt[0], ssm_b_re[0], ssm_b_im[0])
    disc = lambda a_re, a_im, ldt, b_re, b_im: jax.vmap(_discretise)(a_re, a_im, ldt, b_re, b_im)
    (abar_re, abar_im, bbar_re, bbar_im), disc_vjp = jax.vjp(disc, *disc_in)
    ssm = []
    for d in range(2):
        rev = d == 1
        b_re_bd = _block_diag(jnp.swapaxes(bbar_re[d], 1, 2)).astype(BF16)
        b_im_bd = _block_diag(jnp.swapaxes(bbar_im[d], 1, 2)).astype(BF16)
        c_re_bd = _block_diag(jnp.swapaxes(ssm_c_re[0, d], 1, 2)).astype(BF16)
        c_im_bd = _block_diag(jnp.swapaxes(-ssm_c_im[0, d], 1, 2)).astype(BF16)
        ssm.append(dict(rev=rev, coef=_scan_coef(abar_re[d], abar_im[d], rev, _ssm_tile(lp) // 8),
                        coef_adj=_scan_coef(abar_re[d], -abar_im[d], not rev, _ssm_tile(lp) // 8),
                        b_re=b_re_bd, b_im=b_im_bd, c_re=c_re_bd, c_im=c_im_bd))

    t_ssm = _ssm_tile(lp)
    src = (jnp.arange(t_ssm) % 8) * (t_ssm // 8) + jnp.arange(t_ssm) // 8
    perm = (src[:, None] == jnp.arange(t_ssm)[None, :]).astype(BF16)

    h = jnp.concatenate([meta_f, x[0], jnp.zeros((lp - l_real, D_MODEL), F32)], axis=0)
    proj = _in_proj_fwd(h, pre_norm_w, w_in_p)
    q, k, v, vt, q_t, k_t = _attn_prep_fwd(proj, q_norm_w, kv_norm_w, wq_p, wk_p, wv_p, wv_t, cos, sina, sinb, l_real)
    o_exp, lse, mblk, p_all = _flash_fwd(q, k, vt)
    ys, states = [], []
    for s in ssm:
        y_d, st_d = _ssm_fwd(proj, perm, s["coef"], s["b_re"], s["b_im"], s["c_re"], s["c_im"], s["rev"])
        ys.append(y_d)
        states.append(st_d)

    (d_o, do_t, delta, dga, dyp, dsg, dres, dwoa, dwos, dwglu, vec_mid) = _mid(
        h, loss_target[0], o_exp, proj, ys[0], ys[1], ssm_d, w_glu_f, w_glu_f.T, b_glu, ssm_out_norm_w, attn_norm_e, w_out_a, w_out_s,
        w_out_a.T, w_out_s.T, post_norm_w, l_real)
    dus, dssm = [], []
    tr = lambda a: jnp.swapaxes(a, 1, 2)
    for s, st_d in zip(ssm, states):
        du_d, dbre, dbim, dcre, dcim, da = _ssm_bwd(proj, dyp, st_d, perm, s["coef"], s["coef_adj"], s["b_re"], s["b_im"],
                                                    tr(s["b_re"]), tr(s["b_im"]), tr(s["c_re"]), tr(s["c_im"]), s["rev"])
        dus.append(du_d)
        dssm.append((dbre, dbim, dcre, dcim, da))
    dq, dk_t, dv_t = _flash_bwd(q, v, d_o, q_t, k_t, do_t, lse, delta.T.reshape(HEADS, 1, lp), mblk, p_all)
    dql, dkvl, dkr, dwq_p, dwk_p, dwv_p, vec_prep = _attn_prep_bwd(
        dq, dk_t, dv_t, proj, q_norm_w, kv_norm_w, wq_p.T, wk_p.T, wv_p.T, cos, sina, sinb)
    dh, dwin_p, vec_in = _in_proj_bwd(h, pre_norm_w, dres, dga, dus[0], dus[1], dyp, ssm_d, dsg, dql, dkvl, dkr, w_in_p.T)

    grads = {}
    grads["w_in"] = jnp.concatenate([
        dwin_p[:, P_QLAT[0]:P_QLAT[0] + 256], dwin_p[:, P_KVLAT[0]:P_KVLAT[0] + 128],
        dwin_p[:, P_KROPE[0] + QK_NOPE:P_KROPE[0] + QK_NOPE + QK_ROPE], _compact_heads(dwin_p[:, 0:D_EXP], 1, 0, V_HEAD),
        dwin_p[:, P_U[0]:P_U[0] + 512], dwin_p[:, P_GATE_S[0]:P_GATE_S[0] + 512]], axis=1)[None]
    grads["w_q_up"] = _compact_heads(dwq_p, 1, 0, QK_NOPE + QK_ROPE)[None]
    dwk3 = _compact_heads(dwk_p, 1, 0, QK_NOPE).reshape(KV_LORA, HEADS, QK_NOPE)
    dwv3 = _compact_heads(dwv_p, 1, 0, V_HEAD).reshape(KV_LORA, HEADS, V_HEAD)
    grads["w_kv_up"] = jnp.concatenate([dwk3, dwv3], axis=2).reshape(1, KV_LORA, HEADS * (QK_NOPE + V_HEAD))
    grads["w_glu"] = dwglu[None]
    grads["w_out"] = jnp.concatenate([_compact_heads(dwoa, 0, 0, V_HEAD), dwos], axis=0)[None]
    grads["meta_tokens"] = dh[:N_META]
    grads["pre_norm_w"] = vec_in[0:1]
    grads["post_norm_w"] = vec_mid[0:1]
    grads["q_norm_w"] = vec_prep[0:1]
    grads["kv_norm_w"] = vec_prep[1:2, :KV_LORA]
    grads["attn_out_norm_w"] = _compact_heads(vec_mid[1:2], 1, 0, V_HEAD)
    grads["ssm_out_norm_w"] = vec_mid[2:3, :D_SSM]
    grads["ssm_d"] = vec_mid[3:4, :D_SSM]
    grads["b_glu"] = vec_mid[4:5]
    d_abar_re = jnp.stack([dssm[d][4][0].sum(axis=0).reshape(N_GROUPS, SSM_STATE) for d in range(2)])
    d_abar_im = jnp.stack([dssm[d][4][1].sum(axis=0).reshape(N_GROUPS, SSM_STATE) for d in range(2)])
    d_bbar_re = jnp.stack([jnp.swapaxes(_block_diag_extract(dssm[d][0], SSM_GROUP, SSM_STATE), 1, 2) for d in range(2)])
    d_bbar_im = jnp.stack([jnp.swapaxes(_block_diag_extract(dssm[d][1], SSM_GROUP, SSM_STATE), 1, 2) for d in range(2)])
    da_re, da_im, dlog_dt, db_re, db_im = disc_vjp((d_abar_re, d_abar_im, d_bbar_re, d_bbar_im))
    grads["ssm_a_re"], grads["ssm_a_im"], grads["ssm_log_dt"] = da_re[None], da_im[None], dlog_dt[None]
    grads["ssm_b_re"], grads["ssm_b_im"] = db_re[None], db_im[None]
    grads["ssm_c_re"] = jnp.stack([_block_diag_extract(dssm[d][2], SSM_GROUP, SSM_STATE) for d in range(2)])[None]
    grads["ssm_c_im"] = jnp.stack([_block_diag_extract(dssm[d][3], SSM_GROUP, SSM_STATE) for d in range(2)])[None]

    def shard_of(n, a, kk):
        return a[:, kk * 256:(kk + 1) * 256] if n == "w_out" else _shard_cols(a, kk)

    slices = [_pack({n: shard_of(n, grads[n], kk) for n in BIG}, BIG) for kk in range(4)]
    grads["loss"] = vec_mid[5:6, 0:1]
    small = _pack({n: grads[n] for n in SMALL + ("loss",)}, SMALL + ("loss",))
    loss_row = slices[0].shape[0] + sum(-(-math.prod(shapes[n]) // 1024) for n in SMALL)
    rs, rsm = slices[0].shape[0], small.shape[0]
    g_pack = jnp.concatenate(slices + [small], axis=0)
    g_pair = _pair_sum(g_pack, _swap_sibling(g_pack))
    parts = _scatter_chips(g_pair, rs, rsm)

    order = BIG + SMALL
    big_shapes = {n: shapes[n] for n in BIG}
    small_shapes = {n: shapes[n] for n in SMALL}

    def pack_state(named):
        return jnp.concatenate([_pack({n: named[n] for n in BIG}, BIG), _pack({n: named[n] for n in SMALL}, SMALL)], axis=0)

    g_out, d_out, m_out, v_out = _adamw(parts, pack_state(local), pack_state(mom_m), pack_state(mom_v))

    def unpack_state(p):
        out = _unpack(p[:rs], big_shapes, BIG)
        out.update(_unpack(p[rs:], small_shapes, SMALL))
        return out

    g_fin, d_fin, m_fin, v_fin = unpack_state(g_out), unpack_state(d_out), unpack_state(m_out), unpack_state(v_out)
    loss = g_out[loss_row, 0]
    grad_x = dh[N_META:l_real][None]
    return (loss, grad_x, *[g_fin[n] for n in WEIGHTS], *[d_fin[n] for n in WEIGHTS], *[m_fin[n] for n in WEIGHTS],
            *[v_fin[n] for n in WEIGHTS])
```

```python
import functools
import math

import jax
import jax.numpy as jnp
from jax import lax
from jax.experimental import pallas as pl
from jax.experimental.pallas import tpu as pltpu

F32 = jnp.float32
BF16 = jnp.bfloat16
MESH = pl.DeviceIdType.MESH

D_MODEL = 1024
N_META = 16
EPS = 1e-6
HEADS = 8
QK_NOPE = 64
QK_ROPE = 32
V_HEAD = 64
VT_ROWS = 80
Q_LORA = 256
KV_LORA = 128
D_ATTN = 512
D_SSM = 512
SSM_GROUP = 16
N_GROUPS = 32
SSM_STATE = 64
N_STATE = N_GROUPS * SSM_STATE
ROPE_THETA = 10000.0
HEAD_PAD = 128
D_EXP = HEADS * HEAD_PAD
D_QK = QK_NOPE + QK_ROPE
MASK_LANE = D_QK
NEG_BIG = -1e30
SCALE = 1.0 / math.sqrt(QK_NOPE + QK_ROPE)
LOG2E = math.log2(math.e)
SCALE2 = SCALE * LOG2E
QBLK = 256
QUAD = 4
P_KEEP = 3
SCAN_COLS = 1024
SCAN_UNROLL = 2
SSM_BLOCKS = 4
BLK_CH = D_SSM // SSM_BLOCKS
BLK_ST = N_STATE // SSM_BLOCKS

P_GATE_A = (0, 1024)
P_U = (1024, 512)
P_GATE_S = (1536, 512)
P_QLAT = (2048, 256)
P_KVLAT = (2304, 128)
P_KROPE = (2432, 128)
D_PROJ = 2560

ADAM_LR = 0.001
ADAM_B1 = 0.9
ADAM_B2 = 0.999
ADAM_EPS = 1e-08
ADAM_WD = 0.01
ADAM_STEP = 10

VMEM_LIMIT = 60 * 1024 * 1024

BIG = ("w_in", "w_q_up", "w_kv_up", "w_glu", "w_out", "meta_tokens")
SMALL = ("pre_norm_w", "post_norm_w", "q_norm_w", "kv_norm_w", "attn_out_norm_w", "ssm_a_re", "ssm_a_im",
         "ssm_log_dt", "ssm_b_re", "ssm_b_im", "ssm_c_re", "ssm_c_im", "ssm_d", "b_glu", "ssm_out_norm_w")
WEIGHTS = ("meta_tokens", "pre_norm_w", "post_norm_w", "w_in", "q_norm_w", "w_q_up", "kv_norm_w", "w_kv_up",
           "attn_out_norm_w", "ssm_a_re", "ssm_a_im", "ssm_log_dt", "ssm_b_re", "ssm_b_im", "ssm_c_re", "ssm_c_im",
           "ssm_d", "w_glu", "b_glu", "ssm_out_norm_w", "w_out")


def _cparams(sem=None):
    return pltpu.CompilerParams(dimension_semantics=sem, vmem_limit_bytes=VMEM_LIMIT)


def _dot(a, b):
    return jnp.dot(a, b, preferred_element_type=F32)


def _dot_nt(a, b):
    return lax.dot_general(a, b, (((1,), (1,)), ((), ())), preferred_element_type=F32)


def _dot_tn(a, b):
    return lax.dot_general(a, b, (((0,), (0,)), ((), ())), preferred_element_type=F32)


def _sigmoid(x):
    return 1.0 / (1.0 + jnp.exp(-x))


def _rms_fwd(x, w, n):
    r = lax.rsqrt(jnp.sum(x * x, axis=-1, keepdims=True) * (1.0 / n) + EPS)
    return x * r * w, r


def _rms_bwd(x, r, w, dy, n):
    dyw = dy * w
    dx = r * dyw - x * (r * r * r) * (jnp.sum(dyw * x, axis=-1, keepdims=True) * (1.0 / n))
    dw = jnp.sum(dy * (x * r), axis=0, keepdims=True)
    return dx, dw


def _rope_apply(x, cos, sina, sinb):
    return x * cos + pltpu.roll(x, 16, 1) * sina + pltpu.roll(x, HEAD_PAD - 16, 1) * sinb


def _rope_transpose(g, cos, sina, sinb):
    return g * cos + pltpu.roll(g * sina, HEAD_PAD - 16, 1) + pltpu.roll(g * sinb, 16, 1)


def _row_tile(lp):
    return 640 if lp % 640 == 0 else 128


def _ssm_tile(lp):
    return 320 if lp % 320 == 0 else 128


def _rows(tm, off_width):
    off, width = off_width
    return pl.BlockSpec((tm, width), lambda i: (i, off // width))


def _whole(shape, single=True):
    nd = len(shape)
    if single:
        return pl.BlockSpec(shape, lambda *_: (0,) * nd, pipeline_mode=pl.Buffered(1))
    return pl.BlockSpec(shape, lambda *_: (0,) * nd)


def _out_whole(shape):
    return _whole(shape, single=False)


def _pick_tile(rows, cap):
    best = 8
    for t in range(8, cap + 1, 8):
        if rows % t == 0:
            best = t
    return best


def _in_proj_fwd(h, pre_w, w_in_p):
    lp = h.shape[0]
    tm = _row_tile(lp)

    def body(h_ref, w_ref, win_ref, proj_ref):
        xn, _ = _rms_fwd(h_ref[...], w_ref[...], D_MODEL)
        proj_ref[...] = _dot(xn.astype(BF16), win_ref[...])

    return pl.pallas_call(
        body, name="in_proj_fwd", grid=(lp // tm,),
        in_specs=[_rows(tm, (0, D_MODEL)), _whole((1, D_MODEL)), _whole((D_MODEL, D_PROJ))],
        out_specs=_rows(tm, (0, D_PROJ)),
        out_shape=jax.ShapeDtypeStruct((lp, D_PROJ), F32),
        compiler_params=_cparams(("parallel",)),
    )(h, pre_w, w_in_p)


def _attn_prep_fwd(proj, q_norm_w, kv_norm_w, wq_p, wk_p, wv_p, wv_t, cos, sina, sinb, l_real):
    lp = proj.shape[0]
    tm = _row_tile(lp)

    def body(ql_ref, kvl_ref, kr_ref, qw_ref, kw_ref, wq_ref, wk_ref, wv_ref, wvt_ref, cos_ref, sa_ref, sb_ref,
             q_ref, k_ref, v_ref, vt_ref, qt_ref, kt_ref):
        cos_t, sa_t, sb_t = cos_ref[...], sa_ref[...], sb_ref[...]
        qn, _ = _rms_fwd(ql_ref[...], qw_ref[...], Q_LORA)
        kvn, _ = _rms_fwd(kvl_ref[...], kw_ref[...], KV_LORA)
        kvn_b = kvn.astype(BF16)
        qp = _dot(qn.astype(BF16), wq_ref[...])
        kp = _dot(kvn_b, wk_ref[...])
        v_ref[...] = _dot(kvn_b, wv_ref[...]).astype(BF16)
        ones_row = lax.broadcasted_iota(jnp.int32, (HEADS * VT_ROWS, 1), 0) % VT_ROWS == V_HEAD
        vt_ref[...] = jnp.where(ones_row, 1.0, _dot_nt(wvt_ref[...], kvn_b)).astype(BF16)
        lane = lax.broadcasted_iota(jnp.int32, (tm, HEAD_PAD), 1)
        row = lax.broadcasted_iota(jnp.int32, (tm, HEAD_PAD), 0) + pl.program_id(0) * tm
        q_one = jnp.where(lane == MASK_LANE, 1.0, 0.0)
        k_add = _rope_apply(kr_ref[...], cos_t, sa_t, sb_t) + jnp.where((lane == MASK_LANE) & (row >= l_real), NEG_BIG, 0.0)
        for hd in range(HEADS):
            blk = slice(hd * HEAD_PAD, (hd + 1) * HEAD_PAD)
            q_h = _rope_apply(qp[:, blk], cos_t, sa_t, sb_t) * SCALE2 + q_one
            k_h = kp[:, blk] + k_add
            q_ref[:, blk] = q_h.astype(BF16)
            k_ref[:, blk] = k_h.astype(BF16)
            qt_ref[hd * D_QK:(hd + 1) * D_QK, :] = q_h.T[:D_QK].astype(BF16)
            kt_ref[hd * D_QK:(hd + 1) * D_QK, :] = k_h.T[:D_QK].astype(BF16)

    tab = _rows(tm, (0, HEAD_PAD))
    out = jax.ShapeDtypeStruct((lp, D_EXP), BF16)
    out_t = jax.ShapeDtypeStruct((HEADS * D_QK, lp), BF16)
    cols_t = pl.BlockSpec((HEADS * D_QK, tm), lambda i: (0, i))
    return pl.pallas_call(
        body, name="attn_prep_fwd", grid=(lp // tm,),
        in_specs=[_rows(tm, P_QLAT), _rows(tm, P_KVLAT), _rows(tm, P_KROPE), _whole((1, Q_LORA)), _whole((1, KV_LORA)),
                  _whole((Q_LORA, D_EXP)), _whole((KV_LORA, D_EXP)), _whole((KV_LORA, D_EXP)),
                  _whole((HEADS * VT_ROWS, KV_LORA)), tab, tab, tab],
        out_specs=[_rows(tm, (0, D_EXP))] * 3 + [pl.BlockSpec((HEADS * VT_ROWS, tm), lambda i: (0, i)), cols_t, cols_t],
        out_shape=[out, out, out, jax.ShapeDtypeStruct((HEADS * VT_ROWS, lp), BF16), out_t, out_t],
        compiler_params=_cparams(("parallel",)),
    )(proj, proj, proj, q_norm_w, kv_norm_w, wq_p, wk_p, wv_p, wv_t, cos, sina, sinb)


def _flash_fwd(q, k, vt):
    lp = q.shape[0]
    tq = 1280 if lp % 1280 == 0 else 256
    tk = QBLK
    nk = lp // tk

    def body(q_ref, k_ref, vt_ref, o_ref, lse_ref, mblk_ref, p_hbm, acc, m_s, s_a, s_b, p_buf, p_sem):
        hd, qi = pl.program_id(0), pl.program_id(1)

        def p_copy(t, slot):
            return pltpu.make_async_copy(p_buf.at[slot], p_hbm.at[hd, qi, t], p_sem.at[slot])

        acc[...] = jnp.zeros_like(acc)
        m_s[...] = jnp.full(m_s.shape, NEG_BIG, F32)
        blocks = [slice(c * QBLK, (c + 1) * QBLK) for c in range(tq // QBLK)]

        def scores(j, buf):
            kt = k_ref[pl.ds(pl.multiple_of(j * tk, tk), tk), :]
            for cols in blocks:
                buf[:, cols] = _dot_nt(kt, q_ref[cols, :])

        def consume(j, buf, slot, b):
            vt_t = vt_ref[:, pl.ds(pl.multiple_of(j * tk, tk), tk)]
            m_old, acc_old = m_s[...], acc[...]
            s = [buf[:, cols] for cols in blocks]
            m_new = [jnp.maximum(m_old[:, cols], jnp.max(s_c, axis=0, keepdims=True)) for cols, s_c in zip(blocks, s)]
            p = [jnp.exp2(s_c - m_c).astype(BF16) for s_c, m_c in zip(s, m_new)]
            pv = [_dot(vt_t, p_c) for p_c in p]
            m_new = jnp.concatenate(m_new, axis=1)
            alpha = jnp.exp2(m_old - m_new)
            acc[...] = alpha * acc_old + jnp.concatenate(pv, axis=1)
            m_s[...] = m_new
            mblk_ref[j] = m_new
            if b < P_KEEP:
                p_buf[slot, b] = jnp.concatenate(p, axis=1)

        quads = (nk - 1) // QUAD
        scores(0, s_a)

        def quad(t, _):
            j, slot = QUAD * t, t % 2

            @pl.when(t >= 2)
            def _():
                p_copy(t - 2, slot).wait()

            for b in range(QUAD):
                scores(j + b + 1, s_a if b % 2 else s_b)
                consume(j + b, s_b if b % 2 else s_a, slot, b)
            p_copy(t, slot).start()
            return 0

        lax.fori_loop(0, quads, quad, 0)
        for back in (2, 1):
            if quads >= back:
                p_copy(quads - back, (quads - back) % 2).wait()
        rest = nk - QUAD * quads
        for b in range(rest):
            if b + 1 < rest:
                scores(QUAD * quads + b + 1, s_a if b % 2 else s_b)
            consume(QUAD * quads + b, s_b if b % 2 else s_a, 0, QUAD)
        l = acc[V_HEAD:V_HEAD + 1, :]
        o_t = acc[0:V_HEAD, :] / l
        o_ref[...] = jnp.concatenate([o_t, jnp.zeros_like(o_t)], axis=0).T
        lse_ref[...] = m_s[...] + jnp.log2(l)

    return pl.pallas_call(
        body, name="flash_fwd", grid=(HEADS, lp // tq),
        in_specs=[pl.BlockSpec((tq, HEAD_PAD), lambda hd, i: (i, hd)),
                  pl.BlockSpec((lp, HEAD_PAD), lambda hd, i: (0, hd)),
                  pl.BlockSpec((VT_ROWS, lp), lambda hd, i: (hd, 0))],
        out_specs=[pl.BlockSpec((tq, HEAD_PAD), lambda hd, i: (i, hd)),
                   pl.BlockSpec((None, 1, tq), lambda hd, i: (hd, 0, i)),
                   pl.BlockSpec((None, None, nk, 1, tq), lambda hd, i: (hd, i, 0, 0, 0)),
                   pl.BlockSpec(memory_space=pl.ANY)],
        out_shape=[jax.ShapeDtypeStruct((lp, D_EXP), F32), jax.ShapeDtypeStruct((HEADS, 1, lp), F32),
                   jax.ShapeDtypeStruct((HEADS, lp // tq, nk, 1, tq), F32),
                   jax.ShapeDtypeStruct((HEADS, lp // tq, max((nk - 1) // QUAD, 1), P_KEEP, tk, tq), BF16)],
        scratch_shapes=[pltpu.VMEM((VT_ROWS, tq), F32), pltpu.VMEM((1, tq), F32),
                        pltpu.VMEM((tk, tq), F32), pltpu.VMEM((tk, tq), F32),
                        pltpu.VMEM((2, P_KEEP, tk, tq), BF16), pltpu.SemaphoreType.DMA((2,))],
        compiler_params=_cparams(("parallel", "parallel")),
    )(q, k, vt)


def _unpermute_rows(val, scr, out_ref, seg):
    for c in range(val.shape[1] // 128):
        scr[c] = val[:, c * 128:(c + 1) * 128]
    for k in range(8):
        for c in range(val.shape[1] // 128):
            out_ref[k * seg:(k + 1) * seg, c * 128:(c + 1) * 128] = scr[c, pl.ds(k, seg, stride=8), :]


def _scan_rows(xr_ref, xi_ref, base, n_rows, coef_ref, carry_ref, reverse, tile_fn=None, acc_refs=(), halo=False):
    seg = n_rows // 8
    shifts = (7, 6, 4) if reverse else (1, 2, 4)
    row8 = lax.broadcasted_iota(jnp.int32, (8, SCAN_COLS), 0)
    edge, shift = (7, 7) if reverse else (0, 1)
    for cg in range(N_STATE // SCAN_COLS):
        cols = slice(cg * SCAN_COLS, (cg + 1) * SCAN_COLS)
        ar, ai = coef_ref[8, :, cols], coef_ref[9, :, cols]

        def rows_at(i):
            tau = (seg - 1 - i) if reverse else i
            return tau, pl.ds(pl.multiple_of(base + tau * 8, 8), 8)

        def local(i, carry, cols=cols, ar=ar, ai=ai):
            pr, pi_ = carry
            _, rows = rows_at(i)
            nr = ar * pr - ai * pi_ + xr_ref[rows, cols]
            ni = ar * pi_ + ai * pr + xi_ref[rows, cols]
            xr_ref[rows, cols] = nr
            xi_ref[rows, cols] = ni
            return nr, ni

        zero = jnp.zeros((8, SCAN_COLS), F32)
        fr, fi = lax.fori_loop(0, seg, local, (zero, zero), unroll=SCAN_UNROLL)
        co = [coef_ref[k, :, cols] for k in range(8)]
        for lvl in range(3):
            pr, pi_ = co[2 * lvl], co[2 * lvl + 1]
            sr = pltpu.roll(fr, shifts[lvl], 0)
            si = pltpu.roll(fi, shifts[lvl], 0)
            fr, fi = fr + pr * sr - pi_ * si, fi + pr * si + pi_ * sr
        cr, ci = carry_ref[0:1, cols], carry_ref[1:2, cols]
        fr, fi = fr + co[6] * cr - co[7] * ci, fi + co[6] * ci + co[7] * cr
        carry_ref[0:1, cols] = fr[0:1] if reverse else fr[7:8]
        carry_ref[1:2, cols] = fi[0:1] if reverse else fi[7:8]
        in_r = jnp.where(row8 == edge, cr, pltpu.roll(fr, shift, 0))
        in_i = jnp.where(row8 == edge, ci, pltpu.roll(fi, shift, 0))
        if halo:
            rows = pl.ds(base + n_rows, 8) if reverse else pl.ds(base - 8, 8)
            xr_ref[rows, cols] = in_r
            xi_ref[rows, cols] = in_i

        def fix(i, carry, cols=cols, ar=ar, ai=ai):
            c_r, c_i = carry[0], carry[1]
            tau, rows = rows_at(i)
            nr = xr_ref[rows, cols] + c_r
            ni = xi_ref[rows, cols] + c_i
            xr_ref[rows, cols] = nr
            xi_ref[rows, cols] = ni
            accs = carry[2:]
            if tile_fn is not None:
                accs = tuple(a + d for a, d in zip(accs, tile_fn(tau, cols, nr, ni)))
            return (ar * c_r - ai * c_i, ar * c_i + ai * c_r) + accs

        init = (ar * in_r - ai * in_i, ar * in_i + ai * in_r) + tuple(a[:, cols] for a in acc_refs)
        out = lax.fori_loop(0, seg, fix, init, unroll=SCAN_UNROLL)
        for a, val in zip(acc_refs, out[2:]):
            a[:, cols] = val


def _ssm_fwd(proj, perm, coef, b_re, b_im, c_re, c_im_neg, reverse):
    lp = proj.shape[0]
    t = _ssm_tile(lp)
    n = lp // t
    order = (lambda i: n - 1 - i) if reverse else (lambda i: i)

    def body(u_ref, pm_ref, coef_ref, bre_ref, bim_ref, cre_ref, cim_ref, y_ref, st_ref, xr, xi, carry, stage):
        @pl.when(pl.program_id(0) == 0)
        def _():
            carry[...] = jnp.zeros_like(carry)

        st_ref[...] = carry[0:2, :]
        ub = _dot(pm_ref[...], u_ref[...].astype(BF16)).astype(BF16)
        for j in range(SSM_BLOCKS):
            ch, stt = slice(j * BLK_CH, (j + 1) * BLK_CH), slice(j * BLK_ST, (j + 1) * BLK_ST)
            xr[:, stt] = _dot(ub[:, ch], bre_ref[j])
            xi[:, stt] = _dot(ub[:, ch], bim_ref[j])
        _scan_rows(xr, xi, 0, t, coef_ref, carry, reverse)
        y = jnp.concatenate(
            [_dot(xr[:, j * BLK_ST:(j + 1) * BLK_ST].astype(BF16), cre_ref[j])
             + _dot(xi[:, j * BLK_ST:(j + 1) * BLK_ST].astype(BF16), cim_ref[j]) for j in range(SSM_BLOCKS)], axis=1)
        _unpermute_rows(y, stage, y_ref, t // 8)

    wb, wc = _whole((SSM_BLOCKS, BLK_CH, BLK_ST)), _whole((SSM_BLOCKS, BLK_ST, BLK_CH))
    return pl.pallas_call(
        body, name="ssm_fwd_rev" if reverse else "ssm_fwd", grid=(n,),
        in_specs=[pl.BlockSpec((t, D_SSM), lambda i: (order(i), P_U[0] // D_SSM)), _whole((t, t)), _whole((10, 8, N_STATE)),
                  wb, wb, wc, wc],
        out_specs=[pl.BlockSpec((t, D_SSM), lambda i: (order(i), 0)),
                   pl.BlockSpec((None, 2, N_STATE), lambda i: (order(i), 0, 0))],
        out_shape=[jax.ShapeDtypeStruct((lp, D_SSM), F32), jax.ShapeDtypeStruct((n, 2, N_STATE), F32)],
        scratch_shapes=[pltpu.VMEM((t, N_STATE), F32), pltpu.VMEM((t, N_STATE), F32), pltpu.VMEM((8, N_STATE), F32),
                        pltpu.VMEM((D_SSM // 128, t, 128), F32)],
        compiler_params=_cparams(("arbitrary",)),
    )(proj, perm, coef, b_re, b_im, c_re, c_im_neg)


GELU_C0 = math.sqrt(2.0 / math.pi)
GELU_C1 = 0.044715


def _mid(h, tgt, o_exp, proj, y0, y1, ssm_d, w_glu, w_glu_t, b_glu, ssm_norm_w, attn_norm_w_e, w_out_a, w_out_s,
         w_out_a_t, w_out_s_t, post_w, l_real):
    lp = h.shape[0]
    tm = 256

    def body(h_ref, tga_ref, tgb_ref, o_ref, ga_ref, u_ref, sg_ref, y0_ref, y1_ref, d_ref, wg_ref, wgt_ref, bg_ref, ws_ref,
             wa_ref, woa_ref, wos_ref, woat_ref, wost_ref, pw_ref,
             do_ref, dot_ref, delta_ref, dga_ref, dyp_ref, dsg_ref, dres_ref, dwoa_ref, dwos_ref, dwg_ref, vec_ref):
        @pl.when(pl.program_id(0) == 0)
        def _():
            dwoa_ref[...] = jnp.zeros_like(dwoa_ref)
            dwos_ref[...] = jnp.zeros_like(dwos_ref)
            dwg_ref[...] = jnp.zeros_like(dwg_ref)
            vec_ref[...] = jnp.zeros_like(vec_ref)

        u = u_ref[...]
        ypre = y0_ref[...] + y1_ref[...] + d_ref[...] * u
        th = jnp.tanh(GELU_C0 * (ypre + GELU_C1 * ypre * ypre * ypre))
        gel = 0.5 * ypre * (1.0 + th)
        gel_b = gel.astype(BF16)
        glu = _dot(gel_b, wg_ref[...]) + bg_ref[...]
        g1, g2 = glu[:, :D_SSM], glu[:, D_SSM:]
        sig2 = _sigmoid(g2)
        z = g1 * sig2
        sg = sg_ref[...]
        sgs = _sigmoid(sg)
        sil_s = sg * sgs
        s = z * sil_s
        ys, r_s = _rms_fwd(s, ws_ref[...], D_SSM)

        o = o_ref[...]
        ga = ga_ref[...]
        gas = _sigmoid(ga)
        sil_a = ga * gas
        a = o * sil_a
        ya, r_a = _rms_fwd(a, wa_ref[...], D_ATTN)

        ya_b, ys_b = ya.astype(BF16), ys.astype(BF16)
        y = _dot(ya_b, woa_ref[...]) + _dot(ys_b, wos_ref[...])
        yn, r_y = _rms_fwd(y, pw_ref[...], D_MODEL)
        row = lax.broadcasted_iota(jnp.int32, (tm, 1), 0) + pl.program_id(0) * tm
        valid = (row >= N_META) & (row < l_real)
        tgt = jnp.concatenate([tga_ref[tm - N_META:, :], tgb_ref[:tm - N_META, :]], axis=0)
        err = jnp.where(valid, h_ref[...] + yn - tgt, 0.0)
        loss = 0.5 * jnp.sum(jnp.sum(err * err, axis=-1, keepdims=True), axis=0, keepdims=True) * (1.0 / D_MODEL)
        dout = err * (1.0 / D_MODEL)
        dres_ref[...] = dout

        dy, d_pw = _rms_bwd(y, r_y, pw_ref[...], dout, D_MODEL)
        dy_b = dy.astype(BF16)
        dya = _dot(dy_b, woat_ref[...])
        dys = _dot(dy_b, wost_ref[...])
        dwoa_ref[...] += _dot_tn(ya_b, dy_b)
        dwos_ref[...] += _dot_tn(ys_b, dy_b)

        da, d_wa = _rms_bwd(a, r_a, wa_ref[...], dya, D_ATTN)
        d_o = da * sil_a
        dga_ref[...] = da * o * (gas * (1.0 + ga * (1.0 - gas)))
        do_ref[...] = d_o.astype(BF16)
        for hd in range(HEADS):
            dot_ref[hd * V_HEAD:(hd + 1) * V_HEAD, :] = d_o[:, hd * HEAD_PAD:(hd + 1) * HEAD_PAD].T[:V_HEAD].astype(BF16)
        prod = d_o * o
        lane8 = lax.broadcasted_iota(jnp.int32, (tm, HEADS), 1)
        delta = jnp.zeros((tm, HEADS), F32)
        for hd in range(HEADS):
            delta = jnp.where(lane8 == hd, jnp.sum(prod[:, hd * HEAD_PAD:(hd + 1) * HEAD_PAD], axis=-1, keepdims=True), delta)
        delta_ref[...] = delta

        ds, d_ws = _rms_bwd(s, r_s, ws_ref[...], dys, D_SSM)
        dz = ds * sil_s
        dsg_ref[...] = ds * z * (sgs * (1.0 + sg * (1.0 - sgs)))
        dglu = jnp.concatenate([dz * sig2, dz * g1 * sig2 * (1.0 - sig2)], axis=-1)
        dglu_b = dglu.astype(BF16)
        dwg_ref[...] += _dot_tn(gel_b, dglu_b)
        dgel = _dot(dglu_b, wgt_ref[...])
        dgelu = 0.5 * (1.0 + th) + 0.5 * ypre * (1.0 - th * th) * (GELU_C0 * (1.0 + 3.0 * GELU_C1 * ypre * ypre))
        dyp = dgel * dgelu
        dyp_ref[...] = dyp

        vec_ref[0:1, :] += d_pw
        vec_ref[1:2, :] += d_wa
        vec_ref[2:3, 0:D_SSM] += d_ws
        vec_ref[3:4, 0:D_SSM] += jnp.sum(dyp * u, axis=0, keepdims=True)
        vec_ref[4:5, :] += jnp.sum(dglu, axis=0, keepdims=True)
        vec_ref[5:6, :] += jnp.broadcast_to(loss, (1, D_MODEL))

    full = lambda off: _rows(tm, (off, D_MODEL))
    half = lambda off: _rows(tm, (off, D_SSM))
    last = tgt.shape[0] // tm - 1
    tg_a = pl.BlockSpec((tm, D_MODEL), lambda i: (jnp.clip(i - 1, 0, last), 0))
    tg_b = pl.BlockSpec((tm, D_MODEL), lambda i: (jnp.minimum(i, last), 0))
    return pl.pallas_call(
        body, name="mid", grid=(lp // tm,),
        in_specs=[full(0), tg_a, tg_b, full(0), _rows(tm, P_GATE_A), _rows(tm, P_U), _rows(tm, P_GATE_S), half(0), half(0),
                  _whole((1, D_SSM)), _whole((D_SSM, 2 * D_SSM)), _whole((2 * D_SSM, D_SSM)), _whole((1, 2 * D_SSM)),
                  _whole((1, D_SSM)), _whole((1, D_EXP)), _whole((D_EXP, D_MODEL)), _whole((D_SSM, D_MODEL)),
                  _whole((D_MODEL, D_EXP)), _whole((D_MODEL, D_SSM)), _whole((1, D_MODEL))],
        out_specs=[full(0), pl.BlockSpec((D_ATTN, tm), lambda i: (0, i)), _rows(tm, (0, HEADS)), full(0), half(0), half(0), full(0),
                   _out_whole((D_EXP, D_MODEL)), _out_whole((D_SSM, D_MODEL)), _out_whole((D_SSM, 2 * D_SSM)),
                   _out_whole((8, D_MODEL))],
        out_shape=[jax.ShapeDtypeStruct((lp, D_EXP), BF16), jax.ShapeDtypeStruct((D_ATTN, lp), BF16),
                   jax.ShapeDtypeStruct((lp, HEADS), F32),
                   jax.ShapeDtypeStruct((lp, D_EXP), F32), jax.ShapeDtypeStruct((lp, D_SSM), F32),
                   jax.ShapeDtypeStruct((lp, D_SSM), F32), jax.ShapeDtypeStruct((lp, D_MODEL), F32),
                   jax.ShapeDtypeStruct((D_EXP, D_MODEL), F32), jax.ShapeDtypeStruct((D_SSM, D_MODEL), F32),
                   jax.ShapeDtypeStruct((D_SSM, 2 * D_SSM), F32), jax.ShapeDtypeStruct((8, D_MODEL), F32)],
        compiler_params=_cparams(("arbitrary",)),
    )(h, tgt, tgt, o_exp, proj, proj, proj, y0, y1, ssm_d, w_glu, w_glu_t, b_glu, ssm_norm_w, attn_norm_w_e, w_out_a, w_out_s,
      w_out_a_t, w_out_s_t, post_w)


def _ssm_bwd(proj, dyp, states, perm, coef, coef_adj, b_re, b_im, b_re_t, b_im_t, c_re_t, c_im_neg_t, reverse):
    lp = proj.shape[0]
    t = _ssm_tile(lp)
    n = lp // t
    order = (lambda i: i) if reverse else (lambda i: n - 1 - i)

    def body(u_ref, dy_ref, st_ref, pm_ref, coef_ref, coefa_ref, bre_ref, bim_ref, bret_ref, bimt_ref, cret_ref, cimt_ref,
             du_ref, dbre_ref, dbim_ref, dcre_ref, dcim_ref, da_ref, xr, xi, gr, gi, carry_x, carry_g, stage):
        @pl.when(pl.program_id(0) == 0)
        def _():
            carry_g[...] = jnp.zeros_like(carry_g)
            carry_x[...] = jnp.zeros_like(carry_x)
            dbre_ref[...] = jnp.zeros_like(dbre_ref)
            dbim_ref[...] = jnp.zeros_like(dbim_ref)
            dcre_ref[...] = jnp.zeros_like(dcre_ref)
            dcim_ref[...] = jnp.zeros_like(dcim_ref)
            da_ref[...] = jnp.zeros_like(da_ref)
            for halo in (slice(0, 8), slice(t + 8, t + 16)):
                xr[halo, :] = jnp.zeros((8, N_STATE), F32)
                xi[halo, :] = jnp.zeros((8, N_STATE), F32)

        ub = _dot(pm_ref[...], u_ref[...].astype(BF16)).astype(BF16)
        dyb = _dot(pm_ref[...], dy_ref[...].astype(BF16)).astype(BF16)
        carry_x[0:2, :] = st_ref[...]
        blocks = [(slice(j * BLK_CH, (j + 1) * BLK_CH), slice(j * BLK_ST, (j + 1) * BLK_ST)) for j in range(SSM_BLOCKS)]
        for j, (ch, stt) in enumerate(blocks):
            xr[8:t + 8, stt] = _dot(ub[:, ch], bre_ref[j])
            xi[8:t + 8, stt] = _dot(ub[:, ch], bim_ref[j])
            gr[:, stt] = _dot(dyb[:, ch], cret_ref[j])
            gi[:, stt] = _dot(dyb[:, ch], cimt_ref[j])
        _scan_rows(xr, xi, 8, t, coef_ref, carry_x, reverse, halo=True)

        def tile_fn(tau, cols, g_re, g_im):
            nb = pl.ds(pl.multiple_of((tau + 2) * 8 if reverse else tau * 8, 8), 8)
            xn_r, xn_i = xr[nb, cols], xi[nb, cols]
            return g_re * xn_r + g_im * xn_i, g_im * xn_r - g_re * xn_i

        _scan_rows(gr, gi, 0, t, coefa_ref, carry_g, not reverse, tile_fn=tile_fn, acc_refs=(da_ref.at[0], da_ref.at[1]))

        du = []
        for j, (ch, stt) in enumerate(blocks):
            g_re_b, g_im_b = gr[:, stt].astype(BF16), gi[:, stt].astype(BF16)
            du.append(_dot(g_re_b, bret_ref[j]) + _dot(g_im_b, bimt_ref[j]))
            dbre_ref[j] += _dot_tn(ub[:, ch], g_re_b)
            dbim_ref[j] += _dot_tn(ub[:, ch], g_im_b)
            dcre_ref[j] += _dot_tn(dyb[:, ch], xr[8:t + 8, stt].astype(BF16))
            dcim_ref[j] -= _dot_tn(dyb[:, ch], xi[8:t + 8, stt].astype(BF16))
        _unpermute_rows(jnp.concatenate(du, axis=1), stage, du_ref, t // 8)

    dense = jax.ShapeDtypeStruct((SSM_BLOCKS, BLK_CH, BLK_ST), F32)
    wb, wc = _whole((SSM_BLOCKS, BLK_CH, BLK_ST)), _whole((SSM_BLOCKS, BLK_ST, BLK_CH))
    acc = _out_whole((SSM_BLOCKS, BLK_CH, BLK_ST))
    return pl.pallas_call(
        body, name="ssm_bwd_rev" if reverse else "ssm_bwd", grid=(n,),
        in_specs=[pl.BlockSpec((t, D_SSM), lambda i: (order(i), P_U[0] // D_SSM)),
                  pl.BlockSpec((t, D_SSM), lambda i: (order(i), 0)),
                  pl.BlockSpec((None, 2, N_STATE), lambda i: (order(i), 0, 0)), _whole((t, t)),
                  _whole((10, 8, N_STATE)), _whole((10, 8, N_STATE)), wb, wb, wc, wc, wb, wb],
        out_specs=[pl.BlockSpec((t, D_SSM), lambda i: (order(i), 0)), acc, acc, acc, acc, _out_whole((2, 8, N_STATE))],
        out_shape=[jax.ShapeDtypeStruct((lp, D_SSM), F32), dense, dense, dense, dense,
                   jax.ShapeDtypeStruct((2, 8, N_STATE), F32)],
        scratch_shapes=[pltpu.VMEM((t + 16, N_STATE), F32), pltpu.VMEM((t + 16, N_STATE), F32),
                        pltpu.VMEM((t, N_STATE), F32), pltpu.VMEM((t, N_STATE), F32),
                        pltpu.VMEM((8, N_STATE), F32), pltpu.VMEM((8, N_STATE), F32),
                        pltpu.VMEM((D_SSM // 128, t, 128), F32)],
        compiler_params=_cparams(("arbitrary",)),
    )(proj, dyp, states, perm, coef, coef_adj, b_re, b_im, b_re_t, b_im_t, c_re_t, c_im_neg_t)


def _flash_bwd(q, k, v, d_o, q_t, k_t, do_t, lse_row, delta_row, mblk, p_all):
    lp = q.shape[0]
    tq = 1280 if lp % 1280 == 0 else 256
    tk = QBLK
    nk = lp // tk
    d_qk = QK_NOPE + QK_ROPE
    n_groups = (nk - 1) // QUAD

    def body(q_ref, do_ref, qt_ref, dot_ref, lse_ref, delta_ref, mblk_ref, k_ref, v_ref, kt_ref, p_hbm, dq_ref, dk_ref, dv_ref,
             dq_acc, p_buf, p_sem):
        hd, qi = pl.program_id(0), pl.program_id(1)

        def p_copy(t, slot):
            return pltpu.make_async_copy(p_hbm.at[hd, qi, t], p_buf.at[slot], p_sem.at[slot])

        if n_groups:
            p_copy(0, 0).start()

        @pl.when(qi == 0)
        def _():
            dk_ref[...] = jnp.zeros_like(dk_ref)
            dv_ref[...] = jnp.zeros_like(dv_ref)

        dq_acc[...] = jnp.zeros_like(dq_acc)
        lse, delta = lse_ref[...], delta_ref[...]
        q_cols, do_cols = qt_ref[...], dot_ref[...]
        blocks = [slice(c * QBLK, (c + 1) * QBLK) for c in range(tq // QBLK)]

        def key_blocks(base, count, kept):
            dq = dq_acc[...]
            dvs, dks = [], []
            for u in range(count):
                ks = pl.multiple_of(base + u * tk, tk)
                v_rows = v_ref[pl.ds(ks, tk), :]
                dpt = [_dot_nt(v_rows, do_ref[cols, :]) for cols in blocks]
                stored = kept(u)
                if stored is None:
                    k_rows = k_ref[pl.ds(ks, tk), :]
                    pt = jnp.concatenate([jnp.exp2(_dot_nt(k_rows, q_ref[cols, :]) - lse[:, cols]) for cols in blocks], axis=1)
                else:
                    pt = stored[0].astype(F32) * jnp.exp2(stored[1] - lse)
                pt_b = pt.astype(BF16)
                dst_b = jnp.concatenate([(pt[:, cols] * (dp_c - delta[:, cols])).astype(BF16)
                                         for dp_c, cols in zip(dpt, blocks)], axis=1)
                dvs.append(_dot_nt(do_cols, pt_b))
                dks.append(_dot_nt(q_cols, dst_b))
                dq = dq + _dot(kt_ref[:, pl.ds(ks, tk)], dst_b)
            dq_acc[...] = dq
            dv_ref[:, pl.ds(base, count * tk)] += jnp.concatenate(dvs, axis=1)
            dk_ref[:, pl.ds(base, count * tk)] += jnp.concatenate(dks, axis=1) * (1.0 / LOG2E)

        def group(t, _):
            slot = t % 2
            p_copy(t, slot).wait()

            @pl.when(t + 1 < n_groups)
            def _():
                p_copy(t + 1, 1 - slot).start()

            key_blocks(pl.multiple_of(t * (QUAD * tk), QUAD * tk), QUAD,
                       lambda u: (p_buf[slot, u], mblk_ref[t * QUAD + u]) if u < P_KEEP else None)
            return 0

        lax.fori_loop(0, n_groups, group, 0)
        key_blocks(n_groups * QUAD * tk, nk - n_groups * QUAD, lambda u: None)
        dq_ref[...] = jnp.concatenate([dq_acc[...], jnp.zeros((HEAD_PAD - d_qk, tq), F32)], axis=0).T

    tile = pl.BlockSpec((tq, HEAD_PAD), lambda hd, i: (i, hd))
    head = pl.BlockSpec((lp, HEAD_PAD), lambda hd, i: (0, hd))
    rowv = pl.BlockSpec((None, 1, tq), lambda hd, i: (hd, 0, i))
    return pl.pallas_call(
        body, name="flash_bwd", grid=(HEADS, lp // tq),
        in_specs=[tile, tile, pl.BlockSpec((d_qk, tq), lambda hd, i: (hd, i)), pl.BlockSpec((V_HEAD, tq), lambda hd, i: (hd, i)),
                  rowv, rowv, pl.BlockSpec((None, None, nk, 1, tq), lambda hd, i: (hd, i, 0, 0, 0)), head, head,
                  pl.BlockSpec((d_qk, lp), lambda hd, i: (hd, 0)), pl.BlockSpec(memory_space=pl.ANY)],
        out_specs=[tile, pl.BlockSpec((d_qk, lp), lambda hd, i: (hd, 0)), pl.BlockSpec((V_HEAD, lp), lambda hd, i: (hd, 0))],
        out_shape=[jax.ShapeDtypeStruct((lp, D_EXP), F32), jax.ShapeDtypeStruct((HEADS * d_qk, lp), F32),
                   jax.ShapeDtypeStruct((HEADS * V_HEAD, lp), F32)],
        scratch_shapes=[pltpu.VMEM((d_qk, tq), F32), pltpu.VMEM((2, P_KEEP, tk, tq), BF16), pltpu.SemaphoreType.DMA((2,))],
        compiler_params=_cparams(("parallel", "arbitrary")),
    )(q, d_o, q_t, do_t, lse_row, delta_row, mblk, k, v, k_t, p_all)


def _attn_prep_bwd(dq, dk_t, dv_t, proj, q_norm_w, kv_norm_w, wq_pt, wk_pt, wv_pt, cos, sina, sinb):
    lp = proj.shape[0]
    tm = _row_tile(lp)

    def body(dq_ref, dk_ref, dv_ref, ql_ref, kvl_ref, qw_ref, kw_ref, wqt_ref, wkt_ref, wvt_ref, cos_ref, sa_ref, sb_ref,
             dql_ref, dkvl_ref, dkr_ref, dwq_ref, dwk_ref, dwv_ref, vec_ref):
        @pl.when(pl.program_id(0) == 0)
        def _():
            dwq_ref[...] = jnp.zeros_like(dwq_ref)
            dwk_ref[...] = jnp.zeros_like(dwk_ref)
            dwv_ref[...] = jnp.zeros_like(dwv_ref)
            vec_ref[...] = jnp.zeros_like(vec_ref)

        cos_t, sa_t, sb_t = cos_ref[...], sa_ref[...], sb_ref[...]

        def head_rows(t_ref, per):
            pad = jnp.zeros((HEAD_PAD - per, tm), F32)
            return jnp.concatenate(
                [jnp.concatenate([t_ref[hd * per:(hd + 1) * per, :], pad], axis=0).T for hd in range(HEADS)], axis=-1)

        dkp = head_rows(dk_ref, D_QK)
        dqp = jnp.concatenate(
            [_rope_transpose(dq_ref[:, hd * HEAD_PAD:(hd + 1) * HEAD_PAD] * SCALE, cos_t, sa_t, sb_t) for hd in range(HEADS)],
            axis=-1)
        dkr = dkp[:, 0:HEAD_PAD]
        for hd in range(1, HEADS):
            dkr = dkr + dkp[:, hd * HEAD_PAD:(hd + 1) * HEAD_PAD]
        dkr_ref[...] = _rope_transpose(dkr, cos_t, sa_t, sb_t)

        qn, r_q = _rms_fwd(ql_ref[...], qw_ref[...], Q_LORA)
        kvn, r_kv = _rms_fwd(kvl_ref[...], kw_ref[...], KV_LORA)
        dqp_b, dkp_b, dv_b = dqp.astype(BF16), dkp.astype(BF16), head_rows(dv_ref, V_HEAD).astype(BF16)
        dqn = _dot(dqp_b, wqt_ref[...])
        dkvn = _dot(dkp_b, wkt_ref[...]) + _dot(dv_b, wvt_ref[...])
        dwq_ref[...] += _dot_tn(qn.astype(BF16), dqp_b)
        dwk_ref[...] += _dot_tn(kvn.astype(BF16), dkp_b)
        dwv_ref[...] += _dot_tn(kvn.astype(BF16), dv_b)
        dql, d_qw = _rms_bwd(ql_ref[...], r_q, qw_ref[...], dqn, Q_LORA)
        dkvl, d_kw = _rms_bwd(kvl_ref[...], r_kv, kw_ref[...], dkvn, KV_LORA)
        dql_ref[...] = dql
        dkvl_ref[...] = dkvl
        vec_ref[0:1, :] += d_qw
        vec_ref[1:2, 0:KV_LORA] += d_kw

    tab = _rows(tm, (0, HEAD_PAD))
    full = _rows(tm, (0, D_EXP))
    return pl.pallas_call(
        body, name="attn_prep_bwd", grid=(lp // tm,),
        in_specs=[full, pl.BlockSpec((HEADS * D_QK, tm), lambda i: (0, i)), pl.BlockSpec((D_ATTN, tm), lambda i: (0, i)),
                  _rows(tm, P_QLAT), _rows(tm, P_KVLAT), _whole((1, Q_LORA)), _whole((1, KV_LORA)),
                  _whole((D_EXP, Q_LORA)), _whole((D_EXP, KV_LORA)), _whole((D_EXP, KV_LORA)), tab, tab, tab],
        out_specs=[_rows(tm, (0, Q_LORA)), _rows(tm, (0, KV_LORA)), _rows(tm, (0, HEAD_PAD)),
                   _out_whole((Q_LORA, D_EXP)), _out_whole((KV_LORA, D_EXP)), _out_whole((KV_LORA, D_EXP)),
                   _out_whole((8, Q_LORA))],
        out_shape=[jax.ShapeDtypeStruct((lp, Q_LORA), F32), jax.ShapeDtypeStruct((lp, KV_LORA), F32),
                   jax.ShapeDtypeStruct((lp, HEAD_PAD), F32), jax.ShapeDtypeStruct((Q_LORA, D_EXP), F32),
                   jax.ShapeDtypeStruct((KV_LORA, D_EXP), F32), jax.ShapeDtypeStruct((KV_LORA, D_EXP), F32),
                   jax.ShapeDtypeStruct((8, Q_LORA), F32)],
        compiler_params=_cparams(("arbitrary",)),
    )(dq, dk_t, dv_t, proj, proj, q_norm_w, kv_norm_w, wq_pt, wk_pt, wv_pt, cos, sina, sinb)


def _in_proj_bwd(h, pre_w, dres, dga, du0, du1, dyp, ssm_d, dsg, dql, dkvl, dkr, w_in_pt):
    lp = h.shape[0]
    tm = 256
    pieces = (P_GATE_A, P_U, P_GATE_S, P_QLAT, P_KVLAT, P_KROPE)

    def body(h_ref, w_ref, dres_ref, dga_ref, du0_ref, du1_ref, dyp_ref, d_ref, dsg_ref, dql_ref, dkvl_ref, dkr_ref, wt_ref,
             dh_ref, dw_ref, vec_ref):
        @pl.when(pl.program_id(0) == 0)
        def _():
            dw_ref[...] = jnp.zeros_like(dw_ref)
            vec_ref[...] = jnp.zeros_like(vec_ref)

        hv = h_ref[...]
        xn, r = _rms_fwd(hv, w_ref[...], D_MODEL)
        xn_b = xn.astype(BF16)
        du = du0_ref[...] + du1_ref[...] + dyp_ref[...] * d_ref[...]
        grads = (dga_ref[...], du, dsg_ref[...], dql_ref[...], dkvl_ref[...], dkr_ref[...])
        dxn = jnp.zeros((tm, D_MODEL), F32)
        for (off, width), g in zip(pieces, grads):
            g_b = g.astype(BF16)
            dxn = dxn + _dot(g_b, wt_ref[off:off + width, :])
            dw_ref[:, off:off + width] += _dot_tn(xn_b, g_b)
        dx, d_w = _rms_bwd(hv, r, w_ref[...], dxn, D_MODEL)
        dh_ref[...] = dres_ref[...] + dx
        vec_ref[0:1, :] += d_w

    full = _rows(tm, (0, D_MODEL))
    half = _rows(tm, (0, D_SSM))
    return pl.pallas_call(
        body, name="in_proj_bwd", grid=(lp // tm,),
        in_specs=[full, _whole((1, D_MODEL)), full, full, half, half, half, _whole((1, D_SSM)), half,
                  _rows(tm, (0, Q_LORA)), _rows(tm, (0, KV_LORA)), _rows(tm, (0, HEAD_PAD)), _whole((D_PROJ, D_MODEL))],
        out_specs=[full, _out_whole((D_MODEL, D_PROJ)), _out_whole((8, D_MODEL))],
        out_shape=[jax.ShapeDtypeStruct((lp, D_MODEL), F32), jax.ShapeDtypeStruct((D_MODEL, D_PROJ), F32),
                   jax.ShapeDtypeStruct((8, D_MODEL), F32)],
        compiler_params=_cparams(("arbitrary",)),
    )(h, pre_w, dres, dga, du0, du1, dyp, ssm_d, dsg, dql, dkvl, dkr, w_in_pt)


def _other_chips(x, y):
    return [(1 - x, y), (x, 1 - y), (1 - x, 1 - y)]


def _gather_weights(w_bf16, meta):
    any_spec = pl.BlockSpec(memory_space=pl.ANY)
    halves = (w_bf16.shape[0] // 2, meta.shape[0] // 2)

    def body(w_ref, m_ref, wout_ref, mout_ref, send_sems, recv_sems, local_sems):
        x, y, c = lax.axis_index("x"), lax.axis_index("y"), lax.axis_index("c")
        me, sibling = 2 * x + y, (x, y, 1 - c)
        srcs, dsts = (w_ref, m_ref), (wout_ref, mout_ref)

        def half(n, cc):
            return pl.ds(pl.multiple_of(cc * halves[n], 8), halves[n])

        def copy(n, sem, src, chip, cc, to):
            return pltpu.make_async_remote_copy(src_ref=src, dst_ref=dsts[n].at[chip, half(n, cc)], send_sem=send_sems.at[sem],
                                                recv_sem=recv_sems.at[sem], device_id=to, device_id_type=MESH)

        own = [pltpu.make_async_copy(srcs[n], dsts[n].at[me], local_sems.at[n]) for n in range(2)]
        for cp in own:
            cp.start()
        chips = _other_chips(x, y)
        first = [copy(n, 2 * j + n, srcs[n].at[half(n, c)], me, c, (tx, ty, c)) for j, (tx, ty) in enumerate(chips) for n in range(2)]
        for cp in first:
            cp.start()
        passed = []
        for j, (tx, ty) in enumerate(chips):
            for n in range(2):
                landed = dsts[n].at[2 * tx + ty, half(n, c)]
                copy(n, 2 * j + n, landed, 2 * tx + ty, c, (tx, ty, c)).wait_recv()
                passed.append(copy(n, 6 + 2 * j + n, landed, 2 * tx + ty, c, sibling))
                passed[-1].start()
        for j, (tx, ty) in enumerate(chips):
            for n in range(2):
                copy(n, 6 + 2 * j + n, dsts[n].at[2 * tx + ty, half(n, 1 - c)], 2 * tx + ty, 1 - c, sibling).wait_recv()
        for cp in first + passed:
            cp.wait_send()
        for cp in own:
            cp.wait()

    return pl.pallas_call(
        body, name="gather_weights",
        in_specs=[any_spec, any_spec], out_specs=[any_spec, any_spec],
        out_shape=[jax.ShapeDtypeStruct((4,) + w_bf16.shape, w_bf16.dtype), jax.ShapeDtypeStruct((4,) + meta.shape, meta.dtype)],
        scratch_shapes=[pltpu.SemaphoreType.DMA((12,)), pltpu.SemaphoreType.DMA((12,)), pltpu.SemaphoreType.DMA((2,))],
    )(w_bf16, meta)


def _swap_sibling(g):
    any_spec = pl.BlockSpec(memory_space=pl.ANY)

    def body(g_ref, out_ref, send_sem, recv_sem):
        x, y, c = lax.axis_index("x"), lax.axis_index("y"), lax.axis_index("c")
        cp = pltpu.make_async_remote_copy(src_ref=g_ref, dst_ref=out_ref, send_sem=send_sem, recv_sem=recv_sem,
                                          device_id=(x, y, 1 - c), device_id_type=MESH)
        cp.start()
        cp.wait()

    return pl.pallas_call(
        body, name="swap_sibling", in_specs=[any_spec], out_specs=any_spec,
        out_shape=jax.ShapeDtypeStruct(g.shape, g.dtype),
        scratch_shapes=[pltpu.SemaphoreType.DMA(()), pltpu.SemaphoreType.DMA(())],
    )(g)


def _pair_sum(a, b):
    rows = a.shape[0]
    tm = _pick_tile(rows, 1024)

    def body(a_ref, b_ref, o_ref):
        o_ref[...] = a_ref[...] + b_ref[...]

    spec = pl.BlockSpec((tm, 1024), lambda i: (i, 0))
    return pl.pallas_call(body, name="pair_sum", grid=(rows // tm,), in_specs=[spec, spec], out_specs=spec,
                          out_shape=jax.ShapeDtypeStruct(a.shape, F32), compiler_params=_cparams(("parallel",)))(a, b)


def _scatter_chips(s, rs, rsm):
    any_spec = pl.BlockSpec(memory_space=pl.ANY)
    lens = (rs // 2, rsm // 2)

    def body(s_ref, out_ref, send_sems, recv_sems, local_sems):
        x, y, c = lax.axis_index("x"), lax.axis_index("y"), lax.axis_index("c")
        me, sibling = 2 * x + y, (x, y, 1 - c)

        def src_rows(n, target, cc):
            start = (target * rs if n == 0 else 4 * rs) + cc * lens[n]
            return s_ref.at[pl.ds(pl.multiple_of(start, 8), lens[n])]

        def dst_rows(n, cc):
            return pl.ds(pl.multiple_of((0 if n == 0 else rs) + cc * lens[n], 8), lens[n])

        def copy(n, sem, src, chip, cc, to):
            return pltpu.make_async_remote_copy(src_ref=src, dst_ref=out_ref.at[chip, dst_rows(n, cc)], send_sem=send_sems.at[sem],
                                                recv_sem=recv_sems.at[sem], device_id=to, device_id_type=MESH)

        own = [pltpu.make_async_copy(s_ref.at[pl.ds(pl.multiple_of(me * rs, 8), rs)], out_ref.at[me, pl.ds(0, rs)], local_sems.at[0]),
               pltpu.make_async_copy(s_ref.at[pl.ds(4 * rs, rsm)], out_ref.at[me, pl.ds(rs, rsm)], local_sems.at[1])]
        for cp in own:
            cp.start()
        chips = _other_chips(x, y)
        first = [copy(n, 2 * j + n, src_rows(n, 2 * tx + ty, c), me, c, (tx, ty, c))
                 for j, (tx, ty) in enumerate(chips) for n in range(2)]
        for cp in first:
            cp.start()
        passed = []
        for j, (tx, ty) in enumerate(chips):
            for n in range(2):
                landed = out_ref.at[2 * tx + ty, dst_rows(n, c)]
                copy(n, 2 * j + n, landed, 2 * tx + ty, c, (tx, ty, c)).wait_recv()
                passed.append(copy(n, 6 + 2 * j + n, landed, 2 * tx + ty, c, sibling))
                passed[-1].start()
        for j, (tx, ty) in enumerate(chips):
            for n in range(2):
                copy(n, 6 + 2 * j + n, out_ref.at[2 * tx + ty, dst_rows(n, 1 - c)], 2 * tx + ty, 1 - c, sibling).wait_recv()
        for cp in first + passed:
            cp.wait_send()
        for cp in own:
            cp.wait()

    return pl.pallas_call(
        body, name="scatter_chips", in_specs=[any_spec], out_specs=any_spec,
        out_shape=jax.ShapeDtypeStruct((4, rs + rsm, 1024), F32),
        scratch_shapes=[pltpu.SemaphoreType.DMA((12,)), pltpu.SemaphoreType.DMA((12,)), pltpu.SemaphoreType.DMA((2,))],
    )(s)


def _adamw(parts, w, m, v):
    rows = w.shape[0]
    tm = _pick_tile(rows, 256)
    c1 = 1.0 / (1.0 - ADAM_B1 ** ADAM_STEP)
    c2 = 1.0 / (1.0 - ADAM_B2 ** ADAM_STEP)

    def body(p_ref, w_ref, m_ref, v_ref, g_ref, d_ref, nm_ref, nv_ref):
        g = ((p_ref[0] + p_ref[1]) + p_ref[2]) + p_ref[3]
        nm = ADAM_B1 * m_ref[...] + (1.0 - ADAM_B1) * g
        nv = ADAM_B2 * v_ref[...] + (1.0 - ADAM_B2) * (g * g)
        g_ref[...] = g
        nm_ref[...] = nm
        nv_ref[...] = nv
        d_ref[...] = -ADAM_LR * ((nm * c1) / (jnp.sqrt(nv * c2) + ADAM_EPS) + ADAM_WD * w_ref[...])

    spec = pl.BlockSpec((tm, 1024), lambda i: (i, 0))
    out = jax.ShapeDtypeStruct(w.shape, F32)
    return pl.pallas_call(
        body, name="adamw", grid=(rows // tm,),
        in_specs=[pl.BlockSpec((4, tm, 1024), lambda i: (0, i, 0)), spec, spec, spec],
        out_specs=[spec] * 4, out_shape=[out] * 4, compiler_params=_cparams(("parallel",)),
    )(parts, w, m, v)


def _expand_heads(a, axis, per_head):
    a = jnp.moveaxis(a, axis, -1)
    lead = a.shape[:-1]
    a = a.reshape(lead + (HEADS, per_head))
    a = jnp.pad(a, [(0, 0)] * len(lead) + [(0, 0), (0, HEAD_PAD - per_head)])
    return jnp.moveaxis(a.reshape(lead + (D_EXP,)), -1, axis)


def _compact_heads(a, axis, start, size):
    a = jnp.moveaxis(a, axis, -1)
    lead = a.shape[:-1]
    a = a.reshape(lead + (HEADS, HEAD_PAD))[..., start:start + size]
    return jnp.moveaxis(a.reshape(lead + (HEADS * size,)), -1, axis)


def _block_diag(w):
    g, a, b = w.shape
    per = g // SSM_BLOCKS
    eye = jnp.eye(per, dtype=w.dtype)
    return jnp.einsum("jgab,gk->jgakb", w.reshape(SSM_BLOCKS, per, a, b), eye).reshape(SSM_BLOCKS, per * a, per * b)


def _block_diag_extract(dense, a, b):
    per = N_GROUPS // SSM_BLOCKS
    d5 = dense.reshape(SSM_BLOCKS, per, a, per, b)
    return jnp.einsum("jgakb,gk->jgab", d5, jnp.eye(per, dtype=dense.dtype)).reshape(N_GROUPS, a, b)


def _discretise(a_re, a_im, log_dt, b_re, b_im):
    dt = jnp.exp(log_dt)[:, None]
    mag = jnp.exp(a_re * dt)
    abar_re = mag * jnp.cos(a_im * dt)
    abar_im = mag * jnp.sin(a_im * dt)
    num_re = abar_re - 1.0
    num_im = abar_im
    den = a_re * a_re + a_im * a_im
    coef_re = (num_re * a_re + num_im * a_im) / den
    coef_im = (num_im * a_re - num_re * a_im) / den
    bbar_re = coef_re[..., None] * b_re - coef_im[..., None] * b_im
    bbar_im = coef_re[..., None] * b_im + coef_im[..., None] * b_re
    return abar_re, abar_im, bbar_re, bbar_im


def _scan_coef(ar, ai, reverse, seg):
    ar, ai = ar.reshape(1, N_STATE), ai.reshape(1, N_STATE)
    cmul = lambda x, y: (x[0] * y[0] - x[1] * y[1], x[0] * y[1] + x[1] * y[0])
    p, sq, n = None, (ar, ai), seg
    while n:
        if n & 1:
            p = sq if p is None else cmul(p, sq)
        sq, n = cmul(sq, sq), n >> 1
    pows = [p]
    for _ in range(7):
        pows.append(cmul(pows[-1], p))
    row = jnp.arange(8)[:, None]
    out = []
    for k in (1, 2, 4):
        keep = (row < 8 - k) if reverse else (row >= k)
        out += [jnp.where(keep, pows[k - 1][0], 0.0), jnp.where(keep, pows[k - 1][1], 0.0)]
    order = list(range(7, -1, -1)) if reverse else list(range(8))
    out += [jnp.concatenate([pows[k][0] for k in order], axis=0), jnp.concatenate([pows[k][1] for k in order], axis=0)]
    out += [jnp.broadcast_to(ar, (8, N_STATE)), jnp.broadcast_to(ai, (8, N_STATE))]
    return jnp.stack(out).astype(F32)


def _flat_rows(a, rows):
    flat = a.reshape(-1)
    return jnp.pad(flat, (0, rows * 1024 - flat.shape[0])).reshape(rows, 1024)


def _pack(named, order):
    rows = [-(-math.prod(named[n].shape) // 1024) for n in order]
    total = -(-sum(rows) // 32) * 32
    parts = [_flat_rows(named[n], r) for n, r in zip(order, rows)]
    if total > sum(rows):
        parts.append(jnp.zeros((total - sum(rows), 1024), parts[0].dtype))
    return jnp.concatenate(parts, axis=0)


def _unpack(packed, shapes, order):
    out, at = {}, 0
    for n in order:
        size = math.prod(shapes[n])
        rows = -(-size // 1024)
        out[n] = packed[at:at + rows].reshape(-1)[:size].reshape(shapes[n])
        at += rows
    return out


def _shard_cols(a, k):
    w = a.shape[-1] // 4
    return a[..., k * w:(k + 1) * w]


def kernel(x, meta_tokens, pre_norm_w, post_norm_w, w_in, q_norm_w, w_q_up, kv_norm_w, w_kv_up, attn_out_norm_w, ssm_a_re, ssm_a_im, ssm_log_dt, ssm_b_re, ssm_b_im, ssm_c_re, ssm_c_im, ssm_d, w_glu, b_glu, ssm_out_norm_w, w_out, loss_target, m_meta_tokens, m_pre_norm_w, m_post_norm_w, m_w_in, m_q_norm_w, m_w_q_up, m_kv_norm_w, m_w_kv_up, m_attn_out_norm_w, m_ssm_a_re, m_ssm_a_im, m_ssm_log_dt, m_ssm_b_re, m_ssm_b_im, m_ssm_c_re, m_ssm_c_im, m_ssm_d, m_w_glu, m_b_glu, m_ssm_out_norm_w, m_w_out, v_meta_tokens, v_pre_norm_w, v_post_norm_w, v_w_in, v_q_norm_w, v_w_q_up, v_kv_norm_w, v_w_kv_up, v_attn_out_norm_w, v_ssm_a_re, v_ssm_a_im, v_ssm_log_dt, v_ssm_b_re, v_ssm_b_im, v_ssm_c_re, v_ssm_c_im, v_ssm_d, v_w_glu, v_b_glu, v_ssm_out_norm_w, v_w_out):
    local = dict(meta_tokens=meta_tokens, pre_norm_w=pre_norm_w, post_norm_w=post_norm_w, w_in=w_in, q_norm_w=q_norm_w,
                 w_q_up=w_q_up, kv_norm_w=kv_norm_w, w_kv_up=w_kv_up, attn_out_norm_w=attn_out_norm_w, ssm_a_re=ssm_a_re,
                 ssm_a_im=ssm_a_im, ssm_log_dt=ssm_log_dt, ssm_b_re=ssm_b_re, ssm_b_im=ssm_b_im, ssm_c_re=ssm_c_re,
                 ssm_c_im=ssm_c_im, ssm_d=ssm_d, w_glu=w_glu, b_glu=b_glu, ssm_out_norm_w=ssm_out_norm_w, w_out=w_out)
    mom_m = dict(meta_tokens=m_meta_tokens, pre_norm_w=m_pre_norm_w, post_norm_w=m_post_norm_w, w_in=m_w_in,
                 q_norm_w=m_q_norm_w, w_q_up=m_w_q_up, kv_norm_w=m_kv_norm_w, w_kv_up=m_w_kv_up,
                 attn_out_norm_w=m_attn_out_norm_w, ssm_a_re=m_ssm_a_re, ssm_a_im=m_ssm_a_im, ssm_log_dt=m_ssm_log_dt,
                 ssm_b_re=m_ssm_b_re, ssm_b_im=m_ssm_b_im, ssm_c_re=m_ssm_c_re, ssm_c_im=m_ssm_c_im, ssm_d=m_ssm_d,
                 w_glu=m_w_glu, b_glu=m_b_glu, ssm_out_norm_w=m_ssm_out_norm_w, w_out=m_w_out)
    mom_v = dict(meta_tokens=v_meta_tokens, pre_norm_w=v_pre_norm_w, post_norm_w=v_post_norm_w, w_in=v_w_in,
                 q_norm_w=v_q_norm_w, w_q_up=v_w_q_up, kv_norm_w=v_kv_norm_w, w_kv_up=v_w_kv_up,
                 attn_out_norm_w=v_attn_out_norm_w, ssm_a_re=v_ssm_a_re, ssm_a_im=v_ssm_a_im, ssm_log_dt=v_ssm_log_dt,
                 ssm_b_re=v_ssm_b_re, ssm_b_im=v_ssm_b_im, ssm_c_re=v_ssm_c_re, ssm_c_im=v_ssm_c_im, ssm_d=v_ssm_d,
                 w_glu=v_w_glu, b_glu=v_b_glu, ssm_out_norm_w=v_ssm_out_norm_w, w_out=v_w_out)
    shapes = {n: local[n].shape for n in WEIGHTS}
    mat = ("w_in", "w_q_up", "w_kv_up", "w_glu", "w_out")

    seq = x.shape[1]
    l_real = N_META + seq
    lp = -(-l_real // 1280) * 1280 if l_real > 1280 else -(-l_real // QBLK) * QBLK
    assert seq % 256 == 0 and lp % QBLK == 0

    w_shard = _pack({n: local[n].astype(BF16) for n in mat}, mat)
    w_shard = jnp.pad(w_shard, ((0, -w_shard.shape[0] % 16), (0, 0)))
    w_all, meta_all = _gather_weights(w_shard, meta_tokens)
    mat_shapes = {n: shapes[n] for n in mat}
    per_chip = [_unpack(w_all[k], mat_shapes, mat) for k in range(4)]
    w_in_f = jnp.concatenate([p["w_in"][0] for p in per_chip], axis=1)
    w_q_f = jnp.concatenate([p["w_q_up"][0] for p in per_chip], axis=1)
    w_kv_f = jnp.concatenate([p["w_kv_up"][0] for p in per_chip], axis=1)
    w_glu_f = jnp.concatenate([p["w_glu"][0] for p in per_chip], axis=1)
    w_out_f = jnp.concatenate([p["w_out"][0] for p in per_chip], axis=0)
    meta_f = jnp.concatenate([meta_all[k] for k in range(4)], axis=1)

    o_q, o_kv, o_kr, o_ga, o_u, o_gs = 0, 256, 384, 416, 928, 1440
    krope_cols = jnp.pad(w_in_f[:, o_kr:o_ga], ((0, 0), (QK_NOPE, HEAD_PAD - QK_NOPE - QK_ROPE)))
    w_in_p = jnp.concatenate([_expand_heads(w_in_f[:, o_ga:o_u], 1, V_HEAD), w_in_f[:, o_u:o_gs], w_in_f[:, o_gs:],
                              w_in_f[:, o_q:o_kv], w_in_f[:, o_kv:o_kr], krope_cols], axis=1)
    wq_p = _expand_heads(w_q_f, 1, QK_NOPE + QK_ROPE)
    kv3 = w_kv_f.reshape(KV_LORA, HEADS, QK_NOPE + V_HEAD)
    wk_p = _expand_heads(kv3[:, :, :QK_NOPE].reshape(KV_LORA, HEADS * QK_NOPE), 1, QK_NOPE)
    wv_c = kv3[:, :, QK_NOPE:].reshape(KV_LORA, HEADS * V_HEAD)
    wv_p = _expand_heads(wv_c, 1, V_HEAD)
    wv_t = jnp.pad(wv_c.T.reshape(HEADS, V_HEAD, KV_LORA), ((0, 0), (0, VT_ROWS - V_HEAD), (0, 0))).reshape(HEADS * VT_ROWS, KV_LORA)
    w_out_a = _expand_heads(w_out_f[:D_ATTN], 0, V_HEAD)
    w_out_s = w_out_f[D_ATTN:]
    attn_norm_e = _expand_heads(attn_out_norm_w, 1, V_HEAD)

    pos = jnp.arange(lp, dtype=jnp.int32)
    half = QK_ROPE // 2
    inv = ROPE_THETA ** (-jnp.arange(half, dtype=F32) / half)
    ang = pos.astype(F32)[:, None] * inv[None, :]
    cos16, sin16 = jnp.cos(ang), jnp.sin(ang)
    ones, zeros = jnp.ones((lp, QK_NOPE), F32), jnp.zeros((lp, QK_NOPE), F32)
    tail1, tail0 = jnp.ones((lp, HEAD_PAD - MASK_LANE), F32), jnp.zeros((lp, HEAD_PAD - MASK_LANE), F32)
    z16 = jnp.zeros((lp, half), F32)
    cos = jnp.concatenate([ones, cos16, cos16, tail1], axis=1)
    sina = jnp.concatenate([zeros, z16, sin16, tail0], axis=1)
    sinb = jnp.concatenate([zeros, -sin16, z16, tail0], axis=1)

    disc_in = (ssm_a_re[0], ssm_a_im[0], ssm_log_dt[0], ssm_b_re[0], ssm_b_im[0])
    disc = lambda a_re, a_im, ldt, b_re, b_im: jax.vmap(_discretise)(a_re, a_im, ldt, b_re, b_im)
    (abar_re, abar_im, bbar_re, bbar_im), disc_vjp = jax.vjp(disc, *disc_in)
    ssm = []
    for d in range(2):
        rev = d == 1
        b_re_bd = _block_diag(jnp.swapaxes(bbar_re[d], 1, 2)).astype(BF16)
        b_im_bd = _block_diag(jnp.swapaxes(bbar_im[d], 1, 2)).astype(BF16)
        c_re_bd = _block_diag(jnp.swapaxes(ssm_c_re[0, d], 1, 2)).astype(BF16)
        c_im_bd = _block_diag(jnp.swapaxes(-ssm_c_im[0, d], 1, 2)).astype(BF16)
        ssm.append(dict(rev=rev, coef=_scan_coef(abar_re[d], abar_im[d], rev, _ssm_tile(lp) // 8),
                        coef_adj=_scan_coef(abar_re[d], -abar_im[d], not rev, _ssm_tile(lp) // 8),
                        b_re=b_re_bd, b_im=b_im_bd, c_re=c_re_bd, c_im=c_im_bd))

    t_ssm = _ssm_tile(lp)
    src = (jnp.arange(t_ssm) % 8) * (t_ssm // 8) + jnp.arange(t_ssm) // 8
    perm = (src[:, None] == jnp.arange(t_ssm)[None, :]).astype(BF16)

    h = jnp.concatenate([meta_f, x[0], jnp.zeros((lp - l_real, D_MODEL), F32)], axis=0)
    proj = _in_proj_fwd(h, pre_norm_w, w_in_p)
    q, k, v, vt, q_t, k_t = _attn_prep_fwd(proj, q_norm_w, kv_norm_w, wq_p, wk_p, wv_p, wv_t, cos, sina, sinb, l_real)
    o_exp, lse, mblk, p_all = _flash_fwd(q, k, vt)
    ys, states = [], []
    for s in ssm:
        y_d, st_d = _ssm_fwd(proj, perm, s["coef"], s["b_re"], s["b_im"], s["c_re"], s["c_im"], s["rev"])
        ys.append(y_d)
        states.append(st_d)

    (d_o, do_t, delta, dga, dyp, dsg, dres, dwoa, dwos, dwglu, vec_mid) = _mid(
        h, loss_target[0], o_exp, proj, ys[0], ys[1], ssm_d, w_glu_f, w_glu_f.T, b_glu, ssm_out_norm_w, attn_norm_e, w_out_a, w_out_s,
        w_out_a.T, w_out_s.T, post_norm_w, l_real)
    dus, dssm = [], []
    tr = lambda a: jnp.swapaxes(a, 1, 2)
    for s, st_d in zip(ssm, states):
        du_d, dbre, dbim, dcre, dcim, da = _ssm_bwd(proj, dyp, st_d, perm, s["coef"], s["coef_adj"], s["b_re"], s["b_im"],
                                                    tr(s["b_re"]), tr(s["b_im"]), tr(s["c_re"]), tr(s["c_im"]), s["rev"])
        dus.append(du_d)
        dssm.append((dbre, dbim, dcre, dcim, da))
    dq, dk_t, dv_t = _flash_bwd(q, k, v, d_o, q_t, k_t, do_t, lse, delta.T.reshape(HEADS, 1, lp), mblk, p_all)
    dql, dkvl, dkr, dwq_p, dwk_p, dwv_p, vec_prep = _attn_prep_bwd(
        dq, dk_t, dv_t, proj, q_norm_w, kv_norm_w, wq_p.T, wk_p.T, wv_p.T, cos, sina, sinb)
    dh, dwin_p, vec_in = _in_proj_bwd(h, pre_norm_w, dres, dga, dus[0], dus[1], dyp, ssm_d, dsg, dql, dkvl, dkr, w_in_p.T)

    grads = {}
    grads["w_in"] = jnp.concatenate([
        dwin_p[:, P_QLAT[0]:P_QLAT[0] + 256], dwin_p[:, P_KVLAT[0]:P_KVLAT[0] + 128],
        dwin_p[:, P_KROPE[0] + QK_NOPE:P_KROPE[0] + QK_NOPE + QK_ROPE], _compact_heads(dwin_p[:, 0:D_EXP], 1, 0, V_HEAD),
        dwin_p[:, P_U[0]:P_U[0] + 512], dwin_p[:, P_GATE_S[0]:P_GATE_S[0] + 512]], axis=1)[None]
    grads["w_q_up"] = _compact_heads(dwq_p, 1, 0, QK_NOPE + QK_ROPE)[None]
    dwk3 = _compact_heads(dwk_p, 1, 0, QK_NOPE).reshape(KV_LORA, HEADS, QK_NOPE)
    dwv3 = _compact_heads(dwv_p, 1, 0, V_HEAD).reshape(KV_LORA, HEADS, V_HEAD)
    grads["w_kv_up"] = jnp.concatenate([dwk3, dwv3], axis=2).reshape(1, KV_LORA, HEADS * (QK_NOPE + V_HEAD))
    grads["w_glu"] = dwglu[None]
    grads["w_out"] = jnp.concatenate([_compact_heads(dwoa, 0, 0, V_HEAD), dwos], axis=0)[None]
    grads["meta_tokens"] = dh[:N_META]
    grads["pre_norm_w"] = vec_in[0:1]
    grads["post_norm_w"] = vec_mid[0:1]
    grads["q_norm_w"] = vec_prep[0:1]
    grads["kv_norm_w"] = vec_prep[1:2, :KV_LORA]
    grads["attn_out_norm_w"] = _compact_heads(vec_mid[1:2], 1, 0, V_HEAD)
    grads["ssm_out_norm_w"] = vec_mid[2:3, :D_SSM]
    grads["ssm_d"] = vec_mid[3:4, :D_SSM]
    grads["b_glu"] = vec_mid[4:5]
    d_abar_re = jnp.stack([dssm[d][4][0].sum(axis=0).reshape(N_GROUPS, SSM_STATE) for d in range(2)])
    d_abar_im = jnp.stack([dssm[d][4][1].sum(axis=0).reshape(N_GROUPS, SSM_STATE) for d in range(2)])
    d_bbar_re = jnp.stack([jnp.swapaxes(_block_diag_extract(dssm[d][0], SSM_GROUP, SSM_STATE), 1, 2) for d in range(2)])
    d_bbar_im = jnp.stack([jnp.swapaxes(_block_diag_extract(dssm[d][1], SSM_GROUP, SSM_STATE), 1, 2) for d in range(2)])
    da_re, da_im, dlog_dt, db_re, db_im = disc_vjp((d_abar_re, d_abar_im, d_bbar_re, d_bbar_im))
    grads["ssm_a_re"], grads["ssm_a_im"], grads["ssm_log_dt"] = da_re[None], da_im[None], dlog_dt[None]
    grads["ssm_b_re"], grads["ssm_b_im"] = db_re[None], db_im[None]
    grads["ssm_c_re"] = jnp.stack([_block_diag_extract(dssm[d][2], SSM_GROUP, SSM_STATE) for d in range(2)])[None]
    grads["ssm_c_im"] = jnp.stack([_block_diag_extract(dssm[d][3], SSM_GROUP, SSM_STATE) for d in range(2)])[None]

    def shard_of(n, a, kk):
        return a[:, kk * 256:(kk + 1) * 256] if n == "w_out" else _shard_cols(a, kk)

    slices = [_pack({n: shard_of(n, grads[n], kk) for n in BIG}, BIG) for kk in range(4)]
    grads["loss"] = vec_mid[5:6, 0:1]
    small = _pack({n: grads[n] for n in SMALL + ("loss",)}, SMALL + ("loss",))
    loss_row = slices[0].shape[0] + sum(-(-math.prod(shapes[n]) // 1024) for n in SMALL)
    rs, rsm = slices[0].shape[0], small.shape[0]
    g_pack = jnp.concatenate(slices + [small], axis=0)
    g_pair = _pair_sum(g_pack, _swap_sibling(g_pack))
    parts = _scatter_chips(g_pair, rs, rsm)

    order = BIG + SMALL
    big_shapes = {n: shapes[n] for n in BIG}
    small_shapes = {n: shapes[n] for n in SMALL}

    def pack_state(named):
        return jnp.concatenate([_pack({n: named[n] for n in BIG}, BIG), _pack({n: named[n] for n in SMALL}, SMALL)], axis=0)

    g_out, d_out, m_out, v_out = _adamw(parts, pack_state(local), pack_state(mom_m), pack_state(mom_v))

    def unpack_state(p):
        out = _unpack(p[:rs], big_shapes, BIG)
        out.update(_unpack(p[rs:], small_shapes, SMALL))
        return out

    g_fin, d_fin, m_fin, v_fin = unpack_state(g_out), unpack_state(d_out), unpack_state(m_out), unpack_state(v_out)
    loss = g_out[loss_row, 0]
    grad_x = dh[N_META:l_real][None]
    return (loss, grad_x, *[g_fin[n] for n in WEIGHTS], *[d_fin[n] for n in WEIGHTS], *[m_fin[n] for n in WEIGHTS],
            *[v_fin[n] for n in WEIGHTS])
```

```python
import functools
import math

import jax
import jax.numpy as jnp
from jax import lax
from jax.experimental import pallas as pl
from jax.experimental.pallas import tpu as pltpu

F32 = jnp.float32
BF16 = jnp.bfloat16
MESH = pl.DeviceIdType.MESH

D_MODEL = 1024
N_META = 16
EPS = 1e-6
HEADS = 8
QK_NOPE = 64
QK_ROPE = 32
V_HEAD = 64
VT_ROWS = 80
Q_LORA = 256
KV_LORA = 128
D_ATTN = 512
D_SSM = 512
SSM_GROUP = 16
N_GROUPS = 32
SSM_STATE = 64
N_STATE = N_GROUPS * SSM_STATE
ROPE_THETA = 10000.0
HEAD_PAD = 128
D_EXP = HEADS * HEAD_PAD
D_QK = QK_NOPE + QK_ROPE
MASK_LANE = D_QK
NEG_BIG = -1e30
SCALE = 1.0 / math.sqrt(QK_NOPE + QK_ROPE)
LOG2E = math.log2(math.e)
SCALE2 = SCALE * LOG2E
QBLK = 256
QUAD = 4
P_AHEAD = 2
SCAN_COLS = 1024
SCAN_UNROLL = 2
SSM_BLOCKS = 4
BLK_CH = D_SSM // SSM_BLOCKS
BLK_ST = N_STATE // SSM_BLOCKS

P_GATE_A = (0, 1024)
P_U = (1024, 512)
P_GATE_S = (1536, 512)
P_QLAT = (2048, 256)
P_KVLAT = (2304, 128)
P_KROPE = (2432, 128)
D_PROJ = 2560

ADAM_LR = 0.001
ADAM_B1 = 0.9
ADAM_B2 = 0.999
ADAM_EPS = 1e-08
ADAM_WD = 0.01
ADAM_STEP = 10

VMEM_LIMIT = 60 * 1024 * 1024

BIG = ("w_in", "w_q_up", "w_kv_up", "w_glu", "w_out", "meta_tokens")
SMALL = ("pre_norm_w", "post_norm_w", "q_norm_w", "kv_norm_w", "attn_out_norm_w", "ssm_a_re", "ssm_a_im",
         "ssm_log_dt", "ssm_b_re", "ssm_b_im", "ssm_c_re", "ssm_c_im", "ssm_d", "b_glu", "ssm_out_norm_w")
WEIGHTS = ("meta_tokens", "pre_norm_w", "post_norm_w", "w_in", "q_norm_w", "w_q_up", "kv_norm_w", "w_kv_up",
           "attn_out_norm_w", "ssm_a_re", "ssm_a_im", "ssm_log_dt", "ssm_b_re", "ssm_b_im", "ssm_c_re", "ssm_c_im",
           "ssm_d", "w_glu", "b_glu", "ssm_out_norm_w", "w_out")


def _cparams(sem=None):
    return pltpu.CompilerParams(dimension_semantics=sem, vmem_limit_bytes=VMEM_LIMIT)


def _dot(a, b):
    return jnp.dot(a, b, preferred_element_type=F32)


def _dot_nt(a, b):
    return lax.dot_general(a, b, (((1,), (1,)), ((), ())), preferred_element_type=F32)


def _dot_tn(a, b):
    return lax.dot_general(a, b, (((0,), (0,)), ((), ())), preferred_element_type=F32)


def _sigmoid(x):
    return 1.0 / (1.0 + jnp.exp(-x))


def _rms_fwd(x, w, n):
    r = lax.rsqrt(jnp.sum(x * x, axis=-1, keepdims=True) * (1.0 / n) + EPS)
    return x * r * w, r


def _rms_bwd(x, r, w, dy, n):
    dyw = dy * w
    dx = r * dyw - x * (r * r * r) * (jnp.sum(dyw * x, axis=-1, keepdims=True) * (1.0 / n))
    dw = jnp.sum(dy * (x * r), axis=0, keepdims=True)
    return dx, dw


def _rope_apply(x, cos, sina, sinb):
    return x * cos + pltpu.roll(x, 16, 1) * sina + pltpu.roll(x, HEAD_PAD - 16, 1) * sinb


def _rope_transpose(g, cos, sina, sinb):
    return g * cos + pltpu.roll(g * sina, HEAD_PAD - 16, 1) + pltpu.roll(g * sinb, 16, 1)


def _row_tile(lp):
    return 640 if lp % 640 == 0 else 128


def _ssm_tile(lp):
    return 320 if lp % 320 == 0 else 128


def _rows(tm, off_width):
    off, width = off_width
    return pl.BlockSpec((tm, width), lambda i: (i, off // width))


def _whole(shape, single=True):
    nd = len(shape)
    if single:
        return pl.BlockSpec(shape, lambda *_: (0,) * nd, pipeline_mode=pl.Buffered(1))
    return pl.BlockSpec(shape, lambda *_: (0,) * nd)


def _out_whole(shape):
    return _whole(shape, single=False)


def _pick_tile(rows, cap):
    best = 8
    for t in range(8, cap + 1, 8):
        if rows % t == 0:
            best = t
    return best


def _in_proj_fwd(h, pre_w, w_in_p):
    lp = h.shape[0]
    tm = _row_tile(lp)

    def body(h_ref, w_ref, win_ref, proj_ref):
        xn, _ = _rms_fwd(h_ref[...], w_ref[...], D_MODEL)
        proj_ref[...] = _dot(xn.astype(BF16), win_ref[...])

    return pl.pallas_call(
        body, name="in_proj_fwd", grid=(lp // tm,),
        in_specs=[_rows(tm, (0, D_MODEL)), _whole((1, D_MODEL)), _whole((D_MODEL, D_PROJ))],
        out_specs=_rows(tm, (0, D_PROJ)),
        out_shape=jax.ShapeDtypeStruct((lp, D_PROJ), F32),
        compiler_params=_cparams(("parallel",)),
    )(h, pre_w, w_in_p)


def _attn_prep_fwd(proj, q_norm_w, kv_norm_w, wq_p, wk_p, wv_p, wv_t, cos, sina, sinb, l_real):
    lp = proj.shape[0]
    tm = _row_tile(lp)

    def body(ql_ref, kvl_ref, kr_ref, qw_ref, kw_ref, wq_ref, wk_ref, wv_ref, wvt_ref, cos_ref, sa_ref, sb_ref,
             q_ref, k_ref, v_ref, vt_ref, qt_ref, kt_ref):
        cos_t, sa_t, sb_t = cos_ref[...], sa_ref[...], sb_ref[...]
        qn, _ = _rms_fwd(ql_ref[...], qw_ref[...], Q_LORA)
        kvn, _ = _rms_fwd(kvl_ref[...], kw_ref[...], KV_LORA)
        kvn_b = kvn.astype(BF16)
        qp = _dot(qn.astype(BF16), wq_ref[...])
        kp = _dot(kvn_b, wk_ref[...])
        v_ref[...] = _dot(kvn_b, wv_ref[...]).astype(BF16)
        ones_row = lax.broadcasted_iota(jnp.int32, (HEADS * VT_ROWS, 1), 0) % VT_ROWS == V_HEAD
        vt_ref[...] = jnp.where(ones_row, 1.0, _dot_nt(wvt_ref[...], kvn_b)).astype(BF16)
        lane = lax.broadcasted_iota(jnp.int32, (tm, HEAD_PAD), 1)
        row = lax.broadcasted_iota(jnp.int32, (tm, HEAD_PAD), 0) + pl.program_id(0) * tm
        q_one = jnp.where(lane == MASK_LANE, 1.0, 0.0)
        k_add = _rope_apply(kr_ref[...], cos_t, sa_t, sb_t) + jnp.where((lane == MASK_LANE) & (row >= l_real), NEG_BIG, 0.0)
        for hd in range(HEADS):
            blk = slice(hd * HEAD_PAD, (hd + 1) * HEAD_PAD)
            q_h = _rope_apply(qp[:, blk], cos_t, sa_t, sb_t) * SCALE2 + q_one
            k_h = kp[:, blk] + k_add
            q_ref[:, blk] = q_h.astype(BF16)
            k_ref[:, blk] = k_h.astype(BF16)
            qt_ref[hd * D_QK:(hd + 1) * D_QK, :] = q_h.T[:D_QK].astype(BF16)
            kt_ref[hd * D_QK:(hd + 1) * D_QK, :] = k_h.T[:D_QK].astype(BF16)

    tab = _rows(tm, (0, HEAD_PAD))
    out = jax.ShapeDtypeStruct((lp, D_EXP), BF16)
    out_t = jax.ShapeDtypeStruct((HEADS * D_QK, lp), BF16)
    cols_t = pl.BlockSpec((HEADS * D_QK, tm), lambda i: (0, i))
    return pl.pallas_call(
        body, name="attn_prep_fwd", grid=(lp // tm,),
        in_specs=[_rows(tm, P_QLAT), _rows(tm, P_KVLAT), _rows(tm, P_KROPE), _whole((1, Q_LORA)), _whole((1, KV_LORA)),
                  _whole((Q_LORA, D_EXP)), _whole((KV_LORA, D_EXP)), _whole((KV_LORA, D_EXP)),
                  _whole((HEADS * VT_ROWS, KV_LORA)), tab, tab, tab],
        out_specs=[_rows(tm, (0, D_EXP))] * 3 + [pl.BlockSpec((HEADS * VT_ROWS, tm), lambda i: (0, i)), cols_t, cols_t],
        out_shape=[out, out, out, jax.ShapeDtypeStruct((HEADS * VT_ROWS, lp), BF16), out_t, out_t],
        compiler_params=_cparams(("parallel",)),
    )(proj, proj, proj, q_norm_w, kv_norm_w, wq_p, wk_p, wv_p, wv_t, cos, sina, sinb)


def _flash_fwd(q, k, vt):
    lp = q.shape[0]
    tq = 1280 if lp % 1280 == 0 else 256
    tk = QBLK
    nk = lp // tk

    def body(q_ref, k_ref, vt_ref, o_ref, lse_ref, mblk_ref, p_hbm, acc, m_s, s_a, s_b, p_buf, p_sem):
        hd, qi = pl.program_id(0), pl.program_id(1)

        def p_copy(j, n, slot):
            return pltpu.make_async_copy(p_buf.at[slot, pl.ds(0, n)], p_hbm.at[hd, qi, pl.ds(j, n)], p_sem.at[slot])

        acc[...] = jnp.zeros_like(acc)
        m_s[...] = jnp.full(m_s.shape, NEG_BIG, F32)
        blocks = [slice(c * QBLK, (c + 1) * QBLK) for c in range(tq // QBLK)]

        def scores(j, buf):
            kt = k_ref[pl.ds(pl.multiple_of(j * tk, tk), tk), :]
            for cols in blocks:
                buf[:, cols] = _dot_nt(kt, q_ref[cols, :])

        def consume(j, buf, slot, b):
            vt_t = vt_ref[:, pl.ds(pl.multiple_of(j * tk, tk), tk)]
            m_old, acc_old = m_s[...], acc[...]
            s = [buf[:, cols] for cols in blocks]
            m_new = [jnp.maximum(m_old[:, cols], jnp.max(s_c, axis=0, keepdims=True)) for cols, s_c in zip(blocks, s)]
            p = [jnp.exp2(s_c - m_c).astype(BF16) for s_c, m_c in zip(s, m_new)]
            pv = [_dot(vt_t, p_c) for p_c in p]
            m_new = jnp.concatenate(m_new, axis=1)
            alpha = jnp.exp2(m_old - m_new)
            acc[...] = alpha * acc_old + jnp.concatenate(pv, axis=1)
            m_s[...] = m_new
            mblk_ref[j] = m_new
            p_buf[slot, b] = jnp.concatenate(p, axis=1)

        quads = (nk - 1) // QUAD
        scores(0, s_a)

        def quad(t, _):
            j, slot = QUAD * t, t % 2

            @pl.when(t >= 2)
            def _():
                p_copy(j - 2 * QUAD, QUAD, slot).wait()

            for b in range(QUAD):
                scores(j + b + 1, s_a if b % 2 else s_b)
                consume(j + b, s_b if b % 2 else s_a, slot, b)
            p_copy(j, QUAD, slot).start()
            return 0

        lax.fori_loop(0, quads, quad, 0)
        for back in (2, 1):
            if quads >= back:
                p_copy(QUAD * (quads - back), QUAD, (quads - back) % 2).wait()
        rest = nk - QUAD * quads
        for b in range(rest):
            if b + 1 < rest:
                scores(QUAD * quads + b + 1, s_a if b % 2 else s_b)
            consume(QUAD * quads + b, s_b if b % 2 else s_a, 0, b)
        p_copy(QUAD * quads, rest, 0).start()
        p_copy(QUAD * quads, rest, 0).wait()
        l = acc[V_HEAD:V_HEAD + 1, :]
        o_t = acc[0:V_HEAD, :] / l
        o_ref[...] = jnp.concatenate([o_t, jnp.zeros_like(o_t)], axis=0).T
        lse_ref[...] = m_s[...] + jnp.log2(l)

    return pl.pallas_call(
        body, name="flash_fwd", grid=(HEADS, lp // tq),
        in_specs=[pl.BlockSpec((tq, HEAD_PAD), lambda hd, i: (i, hd)),
                  pl.BlockSpec((lp, HEAD_PAD), lambda hd, i: (0, hd)),
                  pl.BlockSpec((VT_ROWS, lp), lambda hd, i: (hd, 0))],
        out_specs=[pl.BlockSpec((tq, HEAD_PAD), lambda hd, i: (i, hd)),
                   pl.BlockSpec((None, 1, tq), lambda hd, i: (hd, 0, i)),
                   pl.BlockSpec((None, None, nk, 1, tq), lambda hd, i: (hd, i, 0, 0, 0)),
                   pl.BlockSpec(memory_space=pl.ANY)],
        out_shape=[jax.ShapeDtypeStruct((lp, D_EXP), F32), jax.ShapeDtypeStruct((HEADS, 1, lp), F32),
                   jax.ShapeDtypeStruct((HEADS, lp // tq, nk, 1, tq), F32),
                   jax.ShapeDtypeStruct((HEADS, lp // tq, nk, tk, tq), BF16)],
        scratch_shapes=[pltpu.VMEM((VT_ROWS, tq), F32), pltpu.VMEM((1, tq), F32),
                        pltpu.VMEM((tk, tq), F32), pltpu.VMEM((tk, tq), F32),
                        pltpu.VMEM((2, QUAD, tk, tq), BF16), pltpu.SemaphoreType.DMA((2,))],
        compiler_params=_cparams(("parallel", "parallel")),
    )(q, k, vt)


def _unpermute_rows(val, scr, out_ref, seg):
    for c in range(val.shape[1] // 128):
        scr[c] = val[:, c * 128:(c + 1) * 128]
    for k in range(8):
        for c in range(val.shape[1] // 128):
            out_ref[k * seg:(k + 1) * seg, c * 128:(c + 1) * 128] = scr[c, pl.ds(k, seg, stride=8), :]


def _scan_rows(xr_ref, xi_ref, base, n_rows, coef_ref, carry_ref, reverse, tile_fn=None, acc_refs=(), halo=False):
    seg = n_rows // 8
    shifts = (7, 6, 4) if reverse else (1, 2, 4)
    row8 = lax.broadcasted_iota(jnp.int32, (8, SCAN_COLS), 0)
    edge, shift = (7, 7) if reverse else (0, 1)
    for cg in range(N_STATE // SCAN_COLS):
        cols = slice(cg * SCAN_COLS, (cg + 1) * SCAN_COLS)
        ar, ai = coef_ref[8, :, cols], coef_ref[9, :, cols]

        def rows_at(i):
            tau = (seg - 1 - i) if reverse else i
            return tau, pl.ds(pl.multiple_of(base + tau * 8, 8), 8)

        def local(i, carry, cols=cols, ar=ar, ai=ai):
            pr, pi_ = carry
            _, rows = rows_at(i)
            nr = ar * pr - ai * pi_ + xr_ref[rows, cols]
            ni = ar * pi_ + ai * pr + xi_ref[rows, cols]
            xr_ref[rows, cols] = nr
            xi_ref[rows, cols] = ni
            return nr, ni

        zero = jnp.zeros((8, SCAN_COLS), F32)
        fr, fi = lax.fori_loop(0, seg, local, (zero, zero), unroll=SCAN_UNROLL)
        co = [coef_ref[k, :, cols] for k in range(8)]
        for lvl in range(3):
            pr, pi_ = co[2 * lvl], co[2 * lvl + 1]
            sr = pltpu.roll(fr, shifts[lvl], 0)
            si = pltpu.roll(fi, shifts[lvl], 0)
            fr, fi = fr + pr * sr - pi_ * si, fi + pr * si + pi_ * sr
        cr, ci = carry_ref[0:1, cols], carry_ref[1:2, cols]
        fr, fi = fr + co[6] * cr - co[7] * ci, fi + co[6] * ci + co[7] * cr
        carry_ref[0:1, cols] = fr[0:1] if reverse else fr[7:8]
        carry_ref[1:2, cols] = fi[0:1] if reverse else fi[7:8]
        in_r = jnp.where(row8 == edge, cr, pltpu.roll(fr, shift, 0))
        in_i = jnp.where(row8 == edge, ci, pltpu.roll(fi, shift, 0))
        if halo:
            rows = pl.ds(base + n_rows, 8) if reverse else pl.ds(base - 8, 8)
            xr_ref[rows, cols] = in_r
            xi_ref[rows, cols] = in_i

        def fix(i, carry, cols=cols, ar=ar, ai=ai):
            c_r, c_i = carry[0], carry[1]
            tau, rows = rows_at(i)
            nr = xr_ref[rows, cols] + c_r
            ni = xi_ref[rows, cols] + c_i
            xr_ref[rows, cols] = nr
            xi_ref[rows, cols] = ni
            accs = carry[2:]
            if tile_fn is not None:
                accs = tuple(a + d for a, d in zip(accs, tile_fn(tau, cols, nr, ni)))
            return (ar * c_r - ai * c_i, ar * c_i + ai * c_r) + accs

        init = (ar * in_r - ai * in_i, ar * in_i + ai * in_r) + tuple(a[:, cols] for a in acc_refs)
        out = lax.fori_loop(0, seg, fix, init, unroll=SCAN_UNROLL)
        for a, val in zip(acc_refs, out[2:]):
            a[:, cols] = val


def _ssm_fwd(proj, perm, coef, b_re, b_im, c_re, c_im_neg, reverse):
    lp = proj.shape[0]
    t = _ssm_tile(lp)
    n = lp // t
    order = (lambda i: n - 1 - i) if reverse else (lambda i: i)

    def body(u_ref, pm_ref, coef_ref, bre_ref, bim_ref, cre_ref, cim_ref, y_ref, st_ref, xr, xi, carry, stage):
        @pl.when(pl.program_id(0) == 0)
        def _():
            carry[...] = jnp.zeros_like(carry)

        st_ref[...] = carry[0:2, :]
        ub = _dot(pm_ref[...], u_ref[...].astype(BF16)).astype(BF16)
        for j in range(SSM_BLOCKS):
            ch, stt = slice(j * BLK_CH, (j + 1) * BLK_CH), slice(j * BLK_ST, (j + 1) * BLK_ST)
            xr[:, stt] = _dot(ub[:, ch], bre_ref[j])
            xi[:, stt] = _dot(ub[:, ch], bim_ref[j])
        _scan_rows(xr, xi, 0, t, coef_ref, carry, reverse)
        y = jnp.concatenate(
            [_dot(xr[:, j * BLK_ST:(j + 1) * BLK_ST].astype(BF16), cre_ref[j])
             + _dot(xi[:, j * BLK_ST:(j + 1) * BLK_ST].astype(BF16), cim_ref[j]) for j in range(SSM_BLOCKS)], axis=1)
        _unpermute_rows(y, stage, y_ref, t // 8)

    wb, wc = _whole((SSM_BLOCKS, BLK_CH, BLK_ST)), _whole((SSM_BLOCKS, BLK_ST, BLK_CH))
    return pl.pallas_call(
        body, name="ssm_fwd_rev" if reverse else "ssm_fwd", grid=(n,),
        in_specs=[pl.BlockSpec((t, D_SSM), lambda i: (order(i), P_U[0] // D_SSM)), _whole((t, t)), _whole((10, 8, N_STATE)),
                  wb, wb, wc, wc],
        out_specs=[pl.BlockSpec((t, D_SSM), lambda i: (order(i), 0)),
                   pl.BlockSpec((None, 2, N_STATE), lambda i: (order(i), 0, 0))],
        out_shape=[jax.ShapeDtypeStruct((lp, D_SSM), F32), jax.ShapeDtypeStruct((n, 2, N_STATE), F32)],
        scratch_shapes=[pltpu.VMEM((t, N_STATE), F32), pltpu.VMEM((t, N_STATE), F32), pltpu.VMEM((8, N_STATE), F32),
                        pltpu.VMEM((D_SSM // 128, t, 128), F32)],
        compiler_params=_cparams(("arbitrary",)),
    )(proj, perm, coef, b_re, b_im, c_re, c_im_neg)


GELU_C0 = math.sqrt(2.0 / math.pi)
GELU_C1 = 0.044715


def _mid(h, tgt, o_exp, proj, y0, y1, ssm_d, w_glu, w_glu_t, b_glu, ssm_norm_w, attn_norm_w_e, w_out_a, w_out_s,
         w_out_a_t, w_out_s_t, post_w, l_real):
    lp = h.shape[0]
    tm = 256

    def body(h_ref, tga_ref, tgb_ref, o_ref, ga_ref, u_ref, sg_ref, y0_ref, y1_ref, d_ref, wg_ref, wgt_ref, bg_ref, ws_ref,
             wa_ref, woa_ref, wos_ref, woat_ref, wost_ref, pw_ref,
             do_ref, dot_ref, delta_ref, dga_ref, dyp_ref, dsg_ref, dres_ref, dwoa_ref, dwos_ref, dwg_ref, vec_ref):
        @pl.when(pl.program_id(0) == 0)
        def _():
            dwoa_ref[...] = jnp.zeros_like(dwoa_ref)
            dwos_ref[...] = jnp.zeros_like(dwos_ref)
            dwg_ref[...] = jnp.zeros_like(dwg_ref)
            vec_ref[...] = jnp.zeros_like(vec_ref)

        u = u_ref[...]
        ypre = y0_ref[...] + y1_ref[...] + d_ref[...] * u
        th = jnp.tanh(GELU_C0 * (ypre + GELU_C1 * ypre * ypre * ypre))
        gel = 0.5 * ypre * (1.0 + th)
        gel_b = gel.astype(BF16)
        glu = _dot(gel_b, wg_ref[...]) + bg_ref[...]
        g1, g2 = glu[:, :D_SSM], glu[:, D_SSM:]
        sig2 = _sigmoid(g2)
        z = g1 * sig2
        sg = sg_ref[...]
        sgs = _sigmoid(sg)
        sil_s = sg * sgs
        s = z * sil_s
        ys, r_s = _rms_fwd(s, ws_ref[...], D_SSM)

        o = o_ref[...]
        ga = ga_ref[...]
        gas = _sigmoid(ga)
        sil_a = ga * gas
        a = o * sil_a
        ya, r_a = _rms_fwd(a, wa_ref[...], D_ATTN)

        ya_b, ys_b = ya.astype(BF16), ys.astype(BF16)
        y = _dot(ya_b, woa_ref[...]) + _dot(ys_b, wos_ref[...])
        yn, r_y = _rms_fwd(y, pw_ref[...], D_MODEL)
        row = lax.broadcasted_iota(jnp.int32, (tm, 1), 0) + pl.program_id(0) * tm
        valid = (row >= N_META) & (row < l_real)
        tgt = jnp.concatenate([tga_ref[tm - N_META:, :], tgb_ref[:tm - N_META, :]], axis=0)
        err = jnp.where(valid, h_ref[...] + yn - tgt, 0.0)
        loss = 0.5 * jnp.sum(jnp.sum(err * err, axis=-1, keepdims=True), axis=0, keepdims=True) * (1.0 / D_MODEL)
        dout = err * (1.0 / D_MODEL)
        dres_ref[...] = dout

        dy, d_pw = _rms_bwd(y, r_y, pw_ref[...], dout, D_MODEL)
        dy_b = dy.astype(BF16)
        dya = _dot(dy_b, woat_ref[...])
        dys = _dot(dy_b, wost_ref[...])
        dwoa_ref[...] += _dot_tn(ya_b, dy_b)
        dwos_ref[...] += _dot_tn(ys_b, dy_b)

        da, d_wa = _rms_bwd(a, r_a, wa_ref[...], dya, D_ATTN)
        d_o = da * sil_a
        dga_ref[...] = da * o * (gas * (1.0 + ga * (1.0 - gas)))
        do_ref[...] = d_o.astype(BF16)
        for hd in range(HEADS):
            dot_ref[hd * V_HEAD:(hd + 1) * V_HEAD, :] = d_o[:, hd * HEAD_PAD:(hd + 1) * HEAD_PAD].T[:V_HEAD].astype(BF16)
        prod = d_o * o
        lane8 = lax.broadcasted_iota(jnp.int32, (tm, HEADS), 1)
        delta = jnp.zeros((tm, HEADS), F32)
        for hd in range(HEADS):
            delta = jnp.where(lane8 == hd, jnp.sum(prod[:, hd * HEAD_PAD:(hd + 1) * HEAD_PAD], axis=-1, keepdims=True), delta)
        delta_ref[...] = delta

        ds, d_ws = _rms_bwd(s, r_s, ws_ref[...], dys, D_SSM)
        dz = ds * sil_s
        dsg_ref[...] = ds * z * (sgs * (1.0 + sg * (1.0 - sgs)))
        dglu = jnp.concatenate([dz * sig2, dz * g1 * sig2 * (1.0 - sig2)], axis=-1)
        dglu_b = dglu.astype(BF16)
        dwg_ref[...] += _dot_tn(gel_b, dglu_b)
        dgel = _dot(dglu_b, wgt_ref[...])
        dgelu = 0.5 * (1.0 + th) + 0.5 * ypre * (1.0 - th * th) * (GELU_C0 * (1.0 + 3.0 * GELU_C1 * ypre * ypre))
        dyp = dgel * dgelu
        dyp_ref[...] = dyp

        vec_ref[0:1, :] += d_pw
        vec_ref[1:2, :] += d_wa
        vec_ref[2:3, 0:D_SSM] += d_ws
        vec_ref[3:4, 0:D_SSM] += jnp.sum(dyp * u, axis=0, keepdims=True)
        vec_ref[4:5, :] += jnp.sum(dglu, axis=0, keepdims=True)
        vec_ref[5:6, :] += jnp.broadcast_to(loss, (1, D_MODEL))

    full = lambda off: _rows(tm, (off, D_MODEL))
    half = lambda off: _rows(tm, (off, D_SSM))
    last = tgt.shape[0] // tm - 1
    tg_a = pl.BlockSpec((tm, D_MODEL), lambda i: (jnp.clip(i - 1, 0, last), 0))
    tg_b = pl.BlockSpec((tm, D_MODEL), lambda i: (jnp.minimum(i, last), 0))
    return pl.pallas_call(
        body, name="mid", grid=(lp // tm,),
        in_specs=[full(0), tg_a, tg_b, full(0), _rows(tm, P_GATE_A), _rows(tm, P_U), _rows(tm, P_GATE_S), half(0), half(0),
                  _whole((1, D_SSM)), _whole((D_SSM, 2 * D_SSM)), _whole((2 * D_SSM, D_SSM)), _whole((1, 2 * D_SSM)),
                  _whole((1, D_SSM)), _whole((1, D_EXP)), _whole((D_EXP, D_MODEL)), _whole((D_SSM, D_MODEL)),
                  _whole((D_MODEL, D_EXP)), _whole((D_MODEL, D_SSM)), _whole((1, D_MODEL))],
        out_specs=[full(0), pl.BlockSpec((D_ATTN, tm), lambda i: (0, i)), _rows(tm, (0, HEADS)), full(0), half(0), half(0), full(0),
                   _out_whole((D_EXP, D_MODEL)), _out_whole((D_SSM, D_MODEL)), _out_whole((D_SSM, 2 * D_SSM)),
                   _out_whole((8, D_MODEL))],
        out_shape=[jax.ShapeDtypeStruct((lp, D_EXP), BF16), jax.ShapeDtypeStruct((D_ATTN, lp), BF16),
                   jax.ShapeDtypeStruct((lp, HEADS), F32),
                   jax.ShapeDtypeStruct((lp, D_EXP), F32), jax.ShapeDtypeStruct((lp, D_SSM), F32),
                   jax.ShapeDtypeStruct((lp, D_SSM), F32), jax.ShapeDtypeStruct((lp, D_MODEL), F32),
                   jax.ShapeDtypeStruct((D_EXP, D_MODEL), F32), jax.ShapeDtypeStruct((D_SSM, D_MODEL), F32),
                   jax.ShapeDtypeStruct((D_SSM, 2 * D_SSM), F32), jax.ShapeDtypeStruct((8, D_MODEL), F32)],
        compiler_params=_cparams(("arbitrary",)),
    )(h, tgt, tgt, o_exp, proj, proj, proj, y0, y1, ssm_d, w_glu, w_glu_t, b_glu, ssm_norm_w, attn_norm_w_e, w_out_a, w_out_s,
      w_out_a_t, w_out_s_t, post_w)


def _ssm_bwd(proj, dyp, states, perm, coef, coef_adj, b_re, b_im, b_re_t, b_im_t, c_re_t, c_im_neg_t, reverse):
    lp = proj.shape[0]
    t = _ssm_tile(lp)
    n = lp // t
    order = (lambda i: i) if reverse else (lambda i: n - 1 - i)

    def body(u_ref, dy_ref, st_ref, pm_ref, coef_ref, coefa_ref, bre_ref, bim_ref, bret_ref, bimt_ref, cret_ref, cimt_ref,
             du_ref, dbre_ref, dbim_ref, dcre_ref, dcim_ref, da_ref, xr, xi, gr, gi, carry_x, carry_g, stage):
        @pl.when(pl.program_id(0) == 0)
        def _():
            carry_g[...] = jnp.zeros_like(carry_g)
            carry_x[...] = jnp.zeros_like(carry_x)
            dbre_ref[...] = jnp.zeros_like(dbre_ref)
            dbim_ref[...] = jnp.zeros_like(dbim_ref)
            dcre_ref[...] = jnp.zeros_like(dcre_ref)
            dcim_ref[...] = jnp.zeros_like(dcim_ref)
            da_ref[...] = jnp.zeros_like(da_ref)
            for halo in (slice(0, 8), slice(t + 8, t + 16)):
                xr[halo, :] = jnp.zeros((8, N_STATE), F32)
                xi[halo, :] = jnp.zeros((8, N_STATE), F32)

        ub = _dot(pm_ref[...], u_ref[...].astype(BF16)).astype(BF16)
        dyb = _dot(pm_ref[...], dy_ref[...].astype(BF16)).astype(BF16)
        carry_x[0:2, :] = st_ref[...]
        blocks = [(slice(j * BLK_CH, (j + 1) * BLK_CH), slice(j * BLK_ST, (j + 1) * BLK_ST)) for j in range(SSM_BLOCKS)]
        for j, (ch, stt) in enumerate(blocks):
            xr[8:t + 8, stt] = _dot(ub[:, ch], bre_ref[j])
            xi[8:t + 8, stt] = _dot(ub[:, ch], bim_ref[j])
            gr[:, stt] = _dot(dyb[:, ch], cret_ref[j])
            gi[:, stt] = _dot(dyb[:, ch], cimt_ref[j])
        _scan_rows(xr, xi, 8, t, coef_ref, carry_x, reverse, halo=True)

        def tile_fn(tau, cols, g_re, g_im):
            nb = pl.ds(pl.multiple_of((tau + 2) * 8 if reverse else tau * 8, 8), 8)
            xn_r, xn_i = xr[nb, cols], xi[nb, cols]
            return g_re * xn_r + g_im * xn_i, g_im * xn_r - g_re * xn_i

        _scan_rows(gr, gi, 0, t, coefa_ref, carry_g, not reverse, tile_fn=tile_fn, acc_refs=(da_ref.at[0], da_ref.at[1]))

        du = []
        for j, (ch, stt) in enumerate(blocks):
            g_re_b, g_im_b = gr[:, stt].astype(BF16), gi[:, stt].astype(BF16)
            du.append(_dot(g_re_b, bret_ref[j]) + _dot(g_im_b, bimt_ref[j]))
            dbre_ref[j] += _dot_tn(ub[:, ch], g_re_b)
            dbim_ref[j] += _dot_tn(ub[:, ch], g_im_b)
            dcre_ref[j] += _dot_tn(dyb[:, ch], xr[8:t + 8, stt].astype(BF16))
            dcim_ref[j] -= _dot_tn(dyb[:, ch], xi[8:t + 8, stt].astype(BF16))
        _unpermute_rows(jnp.concatenate(du, axis=1), stage, du_ref, t // 8)

    dense = jax.ShapeDtypeStruct((SSM_BLOCKS, BLK_CH, BLK_ST), F32)
    wb, wc = _whole((SSM_BLOCKS, BLK_CH, BLK_ST)), _whole((SSM_BLOCKS, BLK_ST, BLK_CH))
    acc = _out_whole((SSM_BLOCKS, BLK_CH, BLK_ST))
    return pl.pallas_call(
        body, name="ssm_bwd_rev" if reverse else "ssm_bwd", grid=(n,),
        in_specs=[pl.BlockSpec((t, D_SSM), lambda i: (order(i), P_U[0] // D_SSM)),
                  pl.BlockSpec((t, D_SSM), lambda i: (order(i), 0)),
                  pl.BlockSpec((None, 2, N_STATE), lambda i: (order(i), 0, 0)), _whole((t, t)),
                  _whole((10, 8, N_STATE)), _whole((10, 8, N_STATE)), wb, wb, wc, wc, wb, wb],
        out_specs=[pl.BlockSpec((t, D_SSM), lambda i: (order(i), 0)), acc, acc, acc, acc, _out_whole((2, 8, N_STATE))],
        out_shape=[jax.ShapeDtypeStruct((lp, D_SSM), F32), dense, dense, dense, dense,
                   jax.ShapeDtypeStruct((2, 8, N_STATE), F32)],
        scratch_shapes=[pltpu.VMEM((t + 16, N_STATE), F32), pltpu.VMEM((t + 16, N_STATE), F32),
                        pltpu.VMEM((t, N_STATE), F32), pltpu.VMEM((t, N_STATE), F32),
                        pltpu.VMEM((8, N_STATE), F32), pltpu.VMEM((8, N_STATE), F32),
                        pltpu.VMEM((D_SSM // 128, t, 128), F32)],
        compiler_params=_cparams(("arbitrary",)),
    )(proj, dyp, states, perm, coef, coef_adj, b_re, b_im, b_re_t, b_im_t, c_re_t, c_im_neg_t)


def _flash_bwd(q, v, d_o, q_t, k_t, do_t, lse_row, delta_row, mblk, p_all):
    lp = q.shape[0]
    tq = 1280 if lp % 1280 == 0 else 256
    tk = QBLK
    nk = lp // tk
    d_qk = QK_NOPE + QK_ROPE
    grp = 5 if nk % 5 == 0 else 1
    n_groups = nk // grp

    def body(do_ref, qt_ref, dot_ref, lse_ref, delta_ref, mblk_ref, v_ref, kt_ref, p_hbm, dq_ref, dk_ref, dv_ref,
             dq_acc, p_buf, p_sem):
        hd, qi = pl.program_id(0), pl.program_id(1)

        def p_copy(t, slot):
            return pltpu.make_async_copy(p_hbm.at[hd, qi, pl.ds(t * grp, grp)], p_buf.at[slot], p_sem.at[slot])

        for t0 in range(min(P_AHEAD, n_groups)):
            p_copy(t0, t0).start()

        @pl.when(qi == 0)
        def _():
            dk_ref[...] = jnp.zeros_like(dk_ref)
            dv_ref[...] = jnp.zeros_like(dv_ref)

        dq_acc[...] = jnp.zeros_like(dq_acc)
        lse, delta = lse_ref[...], delta_ref[...]
        q_cols, do_cols = qt_ref[...], dot_ref[...]
        blocks = [slice(c * QBLK, (c + 1) * QBLK) for c in range(tq // QBLK)]

        def group(t, _):
            base = pl.multiple_of(t * (grp * tk), grp * tk)
            slot = t % (P_AHEAD + 1)
            p_copy(t, slot).wait()

            @pl.when(t + P_AHEAD < n_groups)
            def _():
                p_copy(t + P_AHEAD, (t + P_AHEAD) % (P_AHEAD + 1)).start()

            dq = dq_acc[...]
            dvs, dks = [], []
            for u in range(grp):
                j = t * grp + u
                ks = pl.multiple_of(base + u * tk, tk)
                v_rows = v_ref[pl.ds(ks, tk), :]
                dpt = [_dot_nt(v_rows, do_ref[cols, :]) for cols in blocks]
                pt = p_buf[slot, u].astype(F32) * jnp.exp2(mblk_ref[j] - lse)
                pt_b = pt.astype(BF16)
                dst_b = jnp.concatenate([(pt[:, cols] * (dp_c - delta[:, cols])).astype(BF16)
                                         for dp_c, cols in zip(dpt, blocks)], axis=1)
                dvs.append(_dot_nt(do_cols, pt_b))
                dks.append(_dot_nt(q_cols, dst_b))
                dq = dq + _dot(kt_ref[:, pl.ds(ks, tk)], dst_b)
            dq_acc[...] = dq
            dv_ref[:, pl.ds(base, grp * tk)] += jnp.concatenate(dvs, axis=1)
            dk_ref[:, pl.ds(base, grp * tk)] += jnp.concatenate(dks, axis=1) * (1.0 / LOG2E)
            return 0

        lax.fori_loop(0, n_groups, group, 0)
        dq_ref[...] = jnp.concatenate([dq_acc[...], jnp.zeros((HEAD_PAD - d_qk, tq), F32)], axis=0).T

    tile = pl.BlockSpec((tq, HEAD_PAD), lambda hd, i: (i, hd))
    head = pl.BlockSpec((lp, HEAD_PAD), lambda hd, i: (0, hd))
    rowv = pl.BlockSpec((None, 1, tq), lambda hd, i: (hd, 0, i))
    return pl.pallas_call(
        body, name="flash_bwd", grid=(HEADS, lp // tq),
        in_specs=[tile, pl.BlockSpec((d_qk, tq), lambda hd, i: (hd, i)), pl.BlockSpec((V_HEAD, tq), lambda hd, i: (hd, i)),
                  rowv, rowv, pl.BlockSpec((None, None, nk, 1, tq), lambda hd, i: (hd, i, 0, 0, 0)), head,
                  pl.BlockSpec((d_qk, lp), lambda hd, i: (hd, 0)), pl.BlockSpec(memory_space=pl.ANY)],
        out_specs=[tile, pl.BlockSpec((d_qk, lp), lambda hd, i: (hd, 0)), pl.BlockSpec((V_HEAD, lp), lambda hd, i: (hd, 0))],
        out_shape=[jax.ShapeDtypeStruct((lp, D_EXP), F32), jax.ShapeDtypeStruct((HEADS * d_qk, lp), F32),
                   jax.ShapeDtypeStruct((HEADS * V_HEAD, lp), F32)],
        scratch_shapes=[pltpu.VMEM((d_qk, tq), F32), pltpu.VMEM((P_AHEAD + 1, grp, tk, tq), BF16),
                        pltpu.SemaphoreType.DMA((P_AHEAD + 1,))],
        compiler_params=_cparams(("parallel", "arbitrary")),
    )(d_o, q_t, do_t, lse_row, delta_row, mblk, v, k_t, p_all)


def _attn_prep_bwd(dq, dk_t, dv_t, proj, q_norm_w, kv_norm_w, wq_pt, wk_pt, wv_pt, cos, sina, sinb):
    lp = proj.shape[0]
    tm = _row_tile(lp)

    def body(dq_ref, dk_ref, dv_ref, ql_ref, kvl_ref, qw_ref, kw_ref, wqt_ref, wkt_ref, wvt_ref, cos_ref, sa_ref, sb_ref,
             dql_ref, dkvl_ref, dkr_ref, dwq_ref, dwk_ref, dwv_ref, vec_ref):
        @pl.when(pl.program_id(0) == 0)
        def _():
            dwq_ref[...] = jnp.zeros_like(dwq_ref)
            dwk_ref[...] = jnp.zeros_like(dwk_ref)
            dwv_ref[...] = jnp.zeros_like(dwv_ref)
            vec_ref[...] = jnp.zeros_like(vec_ref)

        cos_t, sa_t, sb_t = cos_ref[...], sa_ref[...], sb_ref[...]

        def head_rows(t_ref, per):
            pad = jnp.zeros((HEAD_PAD - per, tm), F32)
            return jnp.concatenate(
                [jnp.concatenate([t_ref[hd * per:(hd + 1) * per, :], pad], axis=0).T for hd in range(HEADS)], axis=-1)

        dkp = head_rows(dk_ref, D_QK)
        dqp = jnp.concatenate(
            [_rope_transpose(dq_ref[:, hd * HEAD_PAD:(hd + 1) * HEAD_PAD] * SCALE, cos_t, sa_t, sb_t) for hd in range(HEADS)],
            axis=-1)
        dkr = dkp[:, 0:HEAD_PAD]
        for hd in range(1, HEADS):
            dkr = dkr + dkp[:, hd * HEAD_PAD:(hd + 1) * HEAD_PAD]
        dkr_ref[...] = _rope_transpose(dkr, cos_t, sa_t, sb_t)

        qn, r_q = _rms_fwd(ql_ref[...], qw_ref[...], Q_LORA)
        kvn, r_kv = _rms_fwd(kvl_ref[...], kw_ref[...], KV_LORA)
        dqp_b, dkp_b, dv_b = dqp.astype(BF16), dkp.astype(BF16), head_rows(dv_ref, V_HEAD).astype(BF16)
        dqn = _dot(dqp_b, wqt_ref[...])
        dkvn = _dot(dkp_b, wkt_ref[...]) + _dot(dv_b, wvt_ref[...])
        dwq_ref[...] += _dot_tn(qn.astype(BF16), dqp_b)
        dwk_ref[...] += _dot_tn(kvn.astype(BF16), dkp_b)
        dwv_ref[...] += _dot_tn(kvn.astype(BF16), dv_b)
        dql, d_qw = _rms_bwd(ql_ref[...], r_q, qw_ref[...], dqn, Q_LORA)
        dkvl, d_kw = _rms_bwd(kvl_ref[...], r_kv, kw_ref[...], dkvn, KV_LORA)
        dql_ref[...] = dql
        dkvl_ref[...] = dkvl
        vec_ref[0:1, :] += d_qw
        vec_ref[1:2, 0:KV_LORA] += d_kw

    tab = _rows(tm, (0, HEAD_PAD))
    full = _rows(tm, (0, D_EXP))
    return pl.pallas_call(
        body, name="attn_prep_bwd", grid=(lp // tm,),
        in_specs=[full, pl.BlockSpec((HEADS * D_QK, tm), lambda i: (0, i)), pl.BlockSpec((D_ATTN, tm), lambda i: (0, i)),
                  _rows(tm, P_QLAT), _rows(tm, P_KVLAT), _whole((1, Q_LORA)), _whole((1, KV_LORA)),
                  _whole((D_EXP, Q_LORA)), _whole((D_EXP, KV_LORA)), _whole((D_EXP, KV_LORA)), tab, tab, tab],
        out_specs=[_rows(tm, (0, Q_LORA)), _rows(tm, (0, KV_LORA)), _rows(tm, (0, HEAD_PAD)),
                   _out_whole((Q_LORA, D_EXP)), _out_whole((KV_LORA, D_EXP)), _out_whole((KV_LORA, D_EXP)),
                   _out_whole((8, Q_LORA))],
        out_shape=[jax.ShapeDtypeStruct((lp, Q_LORA), F32), jax.ShapeDtypeStruct((lp, KV_LORA), F32),
                   jax.ShapeDtypeStruct((lp, HEAD_PAD), F32), jax.ShapeDtypeStruct((Q_LORA, D_EXP), F32),
                   jax.ShapeDtypeStruct((KV_LORA, D_EXP), F32), jax.ShapeDtypeStruct((KV_LORA, D_EXP), F32),
                   jax.ShapeDtypeStruct((8, Q_LORA), F32)],
        compiler_params=_cparams(("arbitrary",)),
    )(dq, dk_t, dv_t, proj, proj, q_norm_w, kv_norm_w, wq_pt, wk_pt, wv_pt, cos, sina, sinb)


def _in_proj_bwd(h, pre_w, dres, dga, du0, du1, dyp, ssm_d, dsg, dql, dkvl, dkr, w_in_pt):
    lp = h.shape[0]
    tm = 256
    pieces = (P_GATE_A, P_U, P_GATE_S, P_QLAT, P_KVLAT, P_KROPE)

    def body(h_ref, w_ref, dres_ref, dga_ref, du0_ref, du1_ref, dyp_ref, d_ref, dsg_ref, dql_ref, dkvl_ref, dkr_ref, wt_ref,
             dh_ref, dw_ref, vec_ref):
        @pl.when(pl.program_id(0) == 0)
        def _():
            dw_ref[...] = jnp.zeros_like(dw_ref)
            vec_ref[...] = jnp.zeros_like(vec_ref)

        hv = h_ref[...]
        xn, r = _rms_fwd(hv, w_ref[...], D_MODEL)
        xn_b = xn.astype(BF16)
        du = du0_ref[...] + du1_ref[...] + dyp_ref[...] * d_ref[...]
        grads = (dga_ref[...], du, dsg_ref[...], dql_ref[...], dkvl_ref[...], dkr_ref[...])
        dxn = jnp.zeros((tm, D_MODEL), F32)
        for (off, width), g in zip(pieces, grads):
            g_b = g.astype(BF16)
            dxn = dxn + _dot(g_b, wt_ref[off:off + width, :])
            dw_ref[:, off:off + width] += _dot_tn(xn_b, g_b)
        dx, d_w = _rms_bwd(hv, r, w_ref[...], dxn, D_MODEL)
        dh_ref[...] = dres_ref[...] + dx
        vec_ref[0:1, :] += d_w

    full = _rows(tm, (0, D_MODEL))
    half = _rows(tm, (0, D_SSM))
    return pl.pallas_call(
        body, name="in_proj_bwd", grid=(lp // tm,),
        in_specs=[full, _whole((1, D_MODEL)), full, full, half, half, half, _whole((1, D_SSM)), half,
                  _rows(tm, (0, Q_LORA)), _rows(tm, (0, KV_LORA)), _rows(tm, (0, HEAD_PAD)), _whole((D_PROJ, D_MODEL))],
        out_specs=[full, _out_whole((D_MODEL, D_PROJ)), _out_whole((8, D_MODEL))],
        out_shape=[jax.ShapeDtypeStruct((lp, D_MODEL), F32), jax.ShapeDtypeStruct((D_MODEL, D_PROJ), F32),
                   jax.ShapeDtypeStruct((8, D_MODEL), F32)],
        compiler_params=_cparams(("arbitrary",)),
    )(h, pre_w, dres, dga, du0, du1, dyp, ssm_d, dsg, dql, dkvl, dkr, w_in_pt)


def _other_chips(x, y):
    return [(1 - x, y), (x, 1 - y), (1 - x, 1 - y)]


def _gather_weights(w_bf16, meta):
    any_spec = pl.BlockSpec(memory_space=pl.ANY)
    halves = (w_bf16.shape[0] // 2, meta.shape[0] // 2)

    def body(w_ref, m_ref, wout_ref, mout_ref, send_sems, recv_sems, local_sems):
        x, y, c = lax.axis_index("x"), lax.axis_index("y"), lax.axis_index("c")
        me, sibling = 2 * x + y, (x, y, 1 - c)
        srcs, dsts = (w_ref, m_ref), (wout_ref, mout_ref)

        def half(n, cc):
            return pl.ds(pl.multiple_of(cc * halves[n], 8), halves[n])

        def copy(n, sem, src, chip, cc, to):
            return pltpu.make_async_remote_copy(src_ref=src, dst_ref=dsts[n].at[chip, half(n, cc)], send_sem=send_sems.at[sem],
                                                recv_sem=recv_sems.at[sem], device_id=to, device_id_type=MESH)

        own = [pltpu.make_async_copy(srcs[n], dsts[n].at[me], local_sems.at[n]) for n in range(2)]
        for cp in own:
            cp.start()
        chips = _other_chips(x, y)
        first = [copy(n, 2 * j + n, srcs[n].at[half(n, c)], me, c, (tx, ty, c)) for j, (tx, ty) in enumerate(chips) for n in range(2)]
        for cp in first:
            cp.start()
        passed = []
        for j, (tx, ty) in enumerate(chips):
            for n in range(2):
                landed = dsts[n].at[2 * tx + ty, half(n, c)]
                copy(n, 2 * j + n, landed, 2 * tx + ty, c, (tx, ty, c)).wait_recv()
                passed.append(copy(n, 6 + 2 * j + n, landed, 2 * tx + ty, c, sibling))
                passed[-1].start()
        for j, (tx, ty) in enumerate(chips):
            for n in range(2):
                copy(n, 6 + 2 * j + n, dsts[n].at[2 * tx + ty, half(n, 1 - c)], 2 * tx + ty, 1 - c, sibling).wait_recv()
        for cp in first + passed:
            cp.wait_send()
        for cp in own:
            cp.wait()

    return pl.pallas_call(
        body, name="gather_weights",
        in_specs=[any_spec, any_spec], out_specs=[any_spec, any_spec],
        out_shape=[jax.ShapeDtypeStruct((4,) + w_bf16.shape, w_bf16.dtype), jax.ShapeDtypeStruct((4,) + meta.shape, meta.dtype)],
        scratch_shapes=[pltpu.SemaphoreType.DMA((12,)), pltpu.SemaphoreType.DMA((12,)), pltpu.SemaphoreType.DMA((2,))],
    )(w_bf16, meta)


def _swap_sibling(g):
    any_spec = pl.BlockSpec(memory_space=pl.ANY)

    def body(g_ref, out_ref, send_sem, recv_sem):
        x, y, c = lax.axis_index("x"), lax.axis_index("y"), lax.axis_index("c")
        cp = pltpu.make_async_remote_copy(src_ref=g_ref, dst_ref=out_ref, send_sem=send_sem, recv_sem=recv_sem,
                                          device_id=(x, y, 1 - c), device_id_type=MESH)
        cp.start()
        cp.wait()

    return pl.pallas_call(
        body, name="swap_sibling", in_specs=[any_spec], out_specs=any_spec,
        out_shape=jax.ShapeDtypeStruct(g.shape, g.dtype),
        scratch_shapes=[pltpu.SemaphoreType.DMA(()), pltpu.SemaphoreType.DMA(())],
    )(g)


def _pair_sum(a, b):
    rows = a.shape[0]
    tm = _pick_tile(rows, 1024)

    def body(a_ref, b_ref, o_ref):
        o_ref[...] = a_ref[...] + b_ref[...]

    spec = pl.BlockSpec((tm, 1024), lambda i: (i, 0))
    return pl.pallas_call(body, name="pair_sum", grid=(rows // tm,), in_specs=[spec, spec], out_specs=spec,
                          out_shape=jax.ShapeDtypeStruct(a.shape, F32), compiler_params=_cparams(("parallel",)))(a, b)


def _scatter_chips(s, rs, rsm):
    any_spec = pl.BlockSpec(memory_space=pl.ANY)
    lens = (rs // 2, rsm // 2)

    def body(s_ref, out_ref, send_sems, recv_sems, local_sems):
        x, y, c = lax.axis_index("x"), lax.axis_index("y"), lax.axis_index("c")
        me, sibling = 2 * x + y, (x, y, 1 - c)

        def src_rows(n, target, cc):
            start = (target * rs if n == 0 else 4 * rs) + cc * lens[n]
            return s_ref.at[pl.ds(pl.multiple_of(start, 8), lens[n])]

        def dst_rows(n, cc):
            return pl.ds(pl.multiple_of((0 if n == 0 else rs) + cc * lens[n], 8), lens[n])

        def copy(n, sem, src, chip, cc, to):
            return pltpu.make_async_remote_copy(src_ref=src, dst_ref=out_ref.at[chip, dst_rows(n, cc)], send_sem=send_sems.at[sem],
                                                recv_sem=recv_sems.at[sem], device_id=to, device_id_type=MESH)

        own = [pltpu.make_async_copy(s_ref.at[pl.ds(pl.multiple_of(me * rs, 8), rs)], out_ref.at[me, pl.ds(0, rs)], local_sems.at[0]),
               pltpu.make_async_copy(s_ref.at[pl.ds(4 * rs, rsm)], out_ref.at[me, pl.ds(rs, rsm)], local_sems.at[1])]
        for cp in own:
            cp.start()
        chips = _other_chips(x, y)
        first = [copy(n, 2 * j + n, src_rows(n, 2 * tx + ty, c), me, c, (tx, ty, c))
                 for j, (tx, ty) in enumerate(chips) for n in range(2)]
        for cp in first:
            cp.start()
        passed = []
        for j, (tx, ty) in enumerate(chips):
            for n in range(2):
                landed = out_ref.at[2 * tx + ty, dst_rows(n, c)]
                copy(n, 2 * j + n, landed, 2 * tx + ty, c, (tx, ty, c)).wait_recv()
                passed.append(copy(n, 6 + 2 * j + n, landed, 2 * tx + ty, c, sibling))
                passed[-1].start()
        for j, (tx, ty) in enumerate(chips):
            for n in range(2):
                copy(n, 6 + 2 * j + n, out_ref.at[2 * tx + ty, dst_rows(n, 1 - c)], 2 * tx + ty, 1 - c, sibling).wait_recv()
        for cp in first + passed:
            cp.wait_send()
        for cp in own:
            cp.wait()

    return pl.pallas_call(
        body, name="scatter_chips", in_specs=[any_spec], out_specs=any_spec,
        out_shape=jax.ShapeDtypeStruct((4, rs + rsm, 1024), F32),
        scratch_shapes=[pltpu.SemaphoreType.DMA((12,)), pltpu.SemaphoreType.DMA((12,)), pltpu.SemaphoreType.DMA((2,))],
    )(s)


def _adamw(parts, w, m, v):
    rows = w.shape[0]
    tm = _pick_tile(rows, 256)
    c1 = 1.0 / (1.0 - ADAM_B1 ** ADAM_STEP)
    c2 = 1.0 / (1.0 - ADAM_B2 ** ADAM_STEP)

    def body(p_ref, w_ref, m_ref, v_ref, g_ref, d_ref, nm_ref, nv_ref):
        g = ((p_ref[0] + p_ref[1]) + p_ref[2]) + p_ref[3]
        nm = ADAM_B1 * m_ref[...] + (1.0 - ADAM_B1) * g
        nv = ADAM_B2 * v_ref[...] + (1.0 - ADAM_B2) * (g * g)
        g_ref[...] = g
        nm_ref[...] = nm
        nv_ref[...] = nv
        d_ref[...] = -ADAM_LR * ((nm * c1) / (jnp.sqrt(nv * c2) + ADAM_EPS) + ADAM_WD * w_ref[...])

    spec = pl.BlockSpec((tm, 1024), lambda i: (i, 0))
    out = jax.ShapeDtypeStruct(w.shape, F32)
    return pl.pallas_call(
        body, name="adamw", grid=(rows // tm,),
        in_specs=[pl.BlockSpec((4, tm, 1024), lambda i: (0, i, 0)), spec, spec, spec],
        out_specs=[spec] * 4, out_shape=[out] * 4, compiler_params=_cparams(("parallel",)),
    )(parts, w, m, v)


def _expand_heads(a, axis, per_head):
    a = jnp.moveaxis(a, axis, -1)
    lead = a.shape[:-1]
    a = a.reshape(lead + (HEADS, per_head))
    a = jnp.pad(a, [(0, 0)] * len(lead) + [(0, 0), (0, HEAD_PAD - per_head)])
    return jnp.moveaxis(a.reshape(lead + (D_EXP,)), -1, axis)


def _compact_heads(a, axis, start, size):
    a = jnp.moveaxis(a, axis, -1)
    lead = a.shape[:-1]
    a = a.reshape(lead + (HEADS, HEAD_PAD))[..., start:start + size]
    return jnp.moveaxis(a.reshape(lead + (HEADS * size,)), -1, axis)


def _block_diag(w):
    g, a, b = w.shape
    per = g // SSM_BLOCKS
    eye = jnp.eye(per, dtype=w.dtype)
    return jnp.einsum("jgab,gk->jgakb", w.reshape(SSM_BLOCKS, per, a, b), eye).reshape(SSM_BLOCKS, per * a, per * b)


def _block_diag_extract(dense, a, b):
    per = N_GROUPS // SSM_BLOCKS
    d5 = dense.reshape(SSM_BLOCKS, per, a, per, b)
    return jnp.einsum("jgakb,gk->jgab", d5, jnp.eye(per, dtype=dense.dtype)).reshape(N_GROUPS, a, b)


def _discretise(a_re, a_im, log_dt, b_re, b_im):
    dt = jnp.exp(log_dt)[:, None]
    mag = jnp.exp(a_re * dt)
    abar_re = mag * jnp.cos(a_im * dt)
    abar_im = mag * jnp.sin(a_im * dt)
    num_re = abar_re - 1.0
    num_im = abar_im
    den = a_re * a_re + a_im * a_im
    coef_re = (num_re * a_re + num_im * a_im) / den
    coef_im = (num_im * a_re - num_re * a_im) / den
    bbar_re = coef_re[..., None] * b_re - coef_im[..., None] * b_im
    bbar_im = coef_re[..., None] * b_im + coef_im[..., None] * b_re
    return abar_re, abar_im, bbar_re, bbar_im


def _scan_coef(ar, ai, reverse, seg):
    ar, ai = ar.reshape(1, N_STATE), ai.reshape(1, N_STATE)
    cmul = lambda x, y: (x[0] * y[0] - x[1] * y[1], x[0] * y[1] + x[1] * y[0])
    p, sq, n = None, (ar, ai), seg
    while n:
        if n & 1:
            p = sq if p is None else cmul(p, sq)
        sq, n = cmul(sq, sq), n >> 1
    pows = [p]
    for _ in range(7):
        pows.append(cmul(pows[-1], p))
    row = jnp.arange(8)[:, None]
    out = []
    for k in (1, 2, 4):
        keep = (row < 8 - k) if reverse else (row >= k)
        out += [jnp.where(keep, pows[k - 1][0], 0.0), jnp.where(keep, pows[k - 1][1], 0.0)]
    order = list(range(7, -1, -1)) if reverse else list(range(8))
    out += [jnp.concatenate([pows[k][0] for k in order], axis=0), jnp.concatenate([pows[k][1] for k in order], axis=0)]
    out += [jnp.broadcast_to(ar, (8, N_STATE)), jnp.broadcast_to(ai, (8, N_STATE))]
    return jnp.stack(out).astype(F32)


def _flat_rows(a, rows):
    flat = a.reshape(-1)
    return jnp.pad(flat, (0, rows * 1024 - flat.shape[0])).reshape(rows, 1024)


def _pack(named, order):
    rows = [-(-math.prod(named[n].shape) // 1024) for n in order]
    total = -(-sum(rows) // 32) * 32
    parts = [_flat_rows(named[n], r) for n, r in zip(order, rows)]
    if total > sum(rows):
        parts.append(jnp.zeros((total - sum(rows), 1024), parts[0].dtype))
    return jnp.concatenate(parts, axis=0)


def _unpack(packed, shapes, order):
    out, at = {}, 0
    for n in order:
        size = math.prod(shapes[n])
        rows = -(-size // 1024)
        out[n] = packed[at:at + rows].reshape(-1)[:size].reshape(shapes[n])
        at += rows
    return out


def _shard_cols(a, k):
    w = a.shape[-1] // 4
    return a[..., k * w:(k + 1) * w]


def kernel(x, meta_tokens, pre_norm_w, post_norm_w, w_in, q_norm_w, w_q_up, kv_norm_w, w_kv_up, attn_out_norm_w, ssm_a_re, ssm_a_im, ssm_log_dt, ssm_b_re, ssm_b_im, ssm_c_re, ssm_c_im, ssm_d, w_glu, b_glu, ssm_out_norm_w, w_out, loss_target, m_meta_tokens, m_pre_norm_w, m_post_norm_w, m_w_in, m_q_norm_w, m_w_q_up, m_kv_norm_w, m_w_kv_up, m_attn_out_norm_w, m_ssm_a_re, m_ssm_a_im, m_ssm_log_dt, m_ssm_b_re, m_ssm_b_im, m_ssm_c_re, m_ssm_c_im, m_ssm_d, m_w_glu, m_b_glu, m_ssm_out_norm_w, m_w_out, v_meta_tokens, v_pre_norm_w, v_post_norm_w, v_w_in, v_q_norm_w, v_w_q_up, v_kv_norm_w, v_w_kv_up, v_attn_out_norm_w, v_ssm_a_re, v_ssm_a_im, v_ssm_log_dt, v_ssm_b_re, v_ssm_b_im, v_ssm_c_re, v_ssm_c_im, v_ssm_d, v_w_glu, v_b_glu, v_ssm_out_norm_w, v_w_out):
    local = dict(meta_tokens=meta_tokens, pre_norm_w=pre_norm_w, post_norm_w=post_norm_w, w_in=w_in, q_norm_w=q_norm_w,
                 w_q_up=w_q_up, kv_norm_w=kv_norm_w, w_kv_up=w_kv_up, attn_out_norm_w=attn_out_norm_w, ssm_a_re=ssm_a_re,
                 ssm_a_im=ssm_a_im, ssm_log_dt=ssm_log_dt, ssm_b_re=ssm_b_re, ssm_b_im=ssm_b_im, ssm_c_re=ssm_c_re,
                 ssm_c_im=ssm_c_im, ssm_d=ssm_d, w_glu=w_glu, b_glu=b_glu, ssm_out_norm_w=ssm_out_norm_w, w_out=w_out)
    mom_m = dict(meta_tokens=m_meta_tokens, pre_norm_w=m_pre_norm_w, post_norm_w=m_post_norm_w, w_in=m_w_in,
                 q_norm_w=m_q_norm_w, w_q_up=m_w_q_up, kv_norm_w=m_kv_norm_w, w_kv_up=m_w_kv_up,
                 attn_out_norm_w=m_attn_out_norm_w, ssm_a_re=m_ssm_a_re, ssm_a_im=m_ssm_a_im, ssm_log_dt=m_ssm_log_dt,
                 ssm_b_re=m_ssm_b_re, ssm_b_im=m_ssm_b_im, ssm_c_re=m_ssm_c_re, ssm_c_im=m_ssm_c_im, ssm_d=m_ssm_d,
                 w_glu=m_w_glu, b_glu=m_b_glu, ssm_out_norm_w=m_ssm_out_norm_w, w_out=m_w_out)
    mom_v = dict(meta_tokens=v_meta_tokens, pre_norm_w=v_pre_norm_w, post_norm_w=v_post_norm_w, w_in=v_w_in,
                 q_norm_w=v_q_norm_w, w_q_up=v_w_q_up, kv_norm_w=v_kv_norm_w, w_kv_up=v_w_kv_up,
                 attn_out_norm_w=v_attn_out_norm_w, ssm_a_re=v_ssm_a_re, ssm_a_im=v_ssm_a_im, ssm_log_dt=v_ssm_log_dt,
                 ssm_b_re=v_ssm_b_re, ssm_b_im=v_ssm_b_im, ssm_c_re=v_ssm_c_re, ssm_c_im=v_ssm_c_im, ssm_d=v_ssm_d,
                 w_glu=v_w_glu, b_glu=v_b_glu, ssm_out_norm_w=v_ssm_out_norm_w, w_out=v_w_out)
    shapes = {n: local[n].shape for n in WEIGHTS}
    mat = ("w_in", "w_q_up", "w_kv_up", "w_glu", "w_out")

    seq = x.shape[1]
    l_real = N_META + seq
    lp = -(-l_real // 1280) * 1280 if l_real > 1280 else -(-l_real // QBLK) * QBLK
    assert seq % 128 == 0 and lp % QBLK == 0

    w_shard = _pack({n: local[n].astype(BF16) for n in mat}, mat)
    w_shard = jnp.pad(w_shard, ((0, -w_shard.shape[0] % 16), (0, 0)))
    w_all, meta_all = _gather_weights(w_shard, meta_tokens)
    mat_shapes = {n: shapes[n] for n in mat}
    per_chip = [_unpack(w_all[k], mat_shapes, mat) for k in range(4)]
    w_in_f = jnp.concatenate([p["w_in"][0] for p in per_chip], axis=1)
    w_q_f = jnp.concatenate([p["w_q_up"][0] for p in per_chip], axis=1)
    w_kv_f = jnp.concatenate([p["w_kv_up"][0] for p in per_chip], axis=1)
    w_glu_f = jnp.concatenate([p["w_glu"][0] for p in per_chip], axis=1)
    w_out_f = jnp.concatenate([p["w_out"][0] for p in per_chip], axis=0)
    meta_f = jnp.concatenate([meta_all[k] for k in range(4)], axis=1)

    o_q, o_kv, o_kr, o_ga, o_u, o_gs = 0, 256, 384, 416, 928, 1440
    krope_cols = jnp.pad(w_in_f[:, o_kr:o_ga], ((0, 0), (QK_NOPE, HEAD_PAD - QK_NOPE - QK_ROPE)))
    w_in_p = jnp.concatenate([_expand_heads(w_in_f[:, o_ga:o_u], 1, V_HEAD), w_in_f[:, o_u:o_gs], w_in_f[:, o_gs:],
                              w_in_f[:, o_q:o_kv], w_in_f[:, o_kv:o_kr], krope_cols], axis=1)
    wq_p = _expand_heads(w_q_f, 1, QK_NOPE + QK_ROPE)
    kv3 = w_kv_f.reshape(KV_LORA, HEADS, QK_NOPE + V_HEAD)
    wk_p = _expand_heads(kv3[:, :, :QK_NOPE].reshape(KV_LORA, HEADS * QK_NOPE), 1, QK_NOPE)
    wv_c = kv3[:, :, QK_NOPE:].reshape(KV_LORA, HEADS * V_HEAD)
    wv_p = _expand_heads(wv_c, 1, V_HEAD)
    wv_t = jnp.pad(wv_c.T.reshape(HEADS, V_HEAD, KV_LORA), ((0, 0), (0, VT_ROWS - V_HEAD), (0, 0))).reshape(HEADS * VT_ROWS, KV_LORA)
    w_out_a = _expand_heads(w_out_f[:D_ATTN], 0, V_HEAD)
    w_out_s = w_out_f[D_ATTN:]
    attn_norm_e = _expand_heads(attn_out_norm_w, 1, V_HEAD)

    pos = jnp.arange(lp, dtype=jnp.int32)
    half = QK_ROPE // 2
    inv = ROPE_THETA ** (-jnp.arange(half, dtype=F32) / half)
    ang = pos.astype(F32)[:, None] * inv[None, :]
    cos16, sin16 = jnp.cos(ang), jnp.sin(ang)
    ones, zeros = jnp.ones((lp, QK_NOPE), F32), jnp.zeros((lp, QK_NOPE), F32)
    tail1, tail0 = jnp.ones((lp, HEAD_PAD - MASK_LANE), F32), jnp.zeros((lp, HEAD_PAD - MASK_LANE), F32)
    z16 = jnp.zeros((lp, half), F32)
    cos = jnp.concatenate([ones, cos16, cos16, tail1], axis=1)
    sina = jnp.concatenate([zeros, z16, sin16, tail0], axis=1)
    sinb = jnp.concatenate([zeros, -sin16, z16, tail0], axis=1)

    disc_in = (ssm_a_re[0], ssm_a_im[0], ssm_log_dt[0], ssm_b_re[0], ssm_b_im[0])
    disc = lambda a_re, a_im, ldt, b_re, b_im: jax.vmap(_discretise)(a_re, a_im, ldt, b_re, b_im)
    (abar_re, abar_im, bbar_re, bbar_im), disc_vjp = jax.vjp(disc, *disc_in)
    ssm = []
    for d in range(2):
        rev = d == 1
        b_re_bd = _block_diag(jnp.swapaxes(bbar_re[d], 1, 2)).astype(BF16)
        b_im_bd = _block_diag(jnp.swapaxes(bbar_im[d], 1, 2)).astype(BF16)
        c_re_bd = _block_diag(jnp.swapaxes(ssm_c_re[0, d], 1, 2)).astype(BF16)
        c_im_bd = _block_diag(jnp.swapaxes(-ssm_c_im[0, d], 1, 2)).astype(BF16)
        ssm.append(dict(rev=rev, coef=_scan_coef(abar_re[d], abar_im[d], rev, _ssm_tile(lp) // 8),
                        coef_adj=_scan_coef(abar_re[d], -abar_im[d], not rev, _ssm_tile(lp) // 8),
                        b_re=b_re_bd, b_im=b_im_bd, c_re=c_re_bd, c_im=c_im_bd))

    t_ssm = _ssm_tile(lp)
    src = (jnp.arange(t_ssm) % 8) * (t_ssm // 8) + jnp.arange(t_ssm) // 8
    perm = (src[:, None] == jnp.arange(t_ssm)[None, :]).astype(BF16)

    h = jnp.concatenate([meta_f, x[0], jnp.zeros((lp - l_real, D_MODEL), F32)], axis=0)
    proj = _in_proj_fwd(h, pre_norm_w, w_in_p)
    q, k, v, vt, q_t, k_t = _attn_prep_fwd(proj, q_norm_w, kv_norm_w, wq_p, wk_p, wv_p, wv_t, cos, sina, sinb, l_real)
    o_exp, lse, mblk, p_all = _flash_fwd(q, k, vt)
    ys, states = [], []
    for s in ssm:
        y_d, st_d = _ssm_fwd(proj, perm, s["coef"], s["b_re"], s["b_im"], s["c_re"], s["c_im"], s["rev"])
        ys.append(y_d)
        states.append(st_d)

    (d_o, do_t, delta, dga, dyp, dsg, dres, dwoa, dwos, dwglu, vec_mid) = _mid(
        h, loss_target[0], o_exp, proj, ys[0], ys[1], ssm_d, w_glu_f, w_glu_f.T, b_glu, ssm_out_norm_w, attn_norm_e, w_out_a, w_out_s,
        w_out_a.T, w_out_s.T, post_norm_w, l_real)
    dus, dssm = [], []
    tr = lambda a: jnp.swapaxes(a, 1, 2)
    for s, st_d in zip(ssm, states):
        du_d, dbre, dbim, dcre, dcim, da = _ssm_bwd(proj, dyp, st_d, perm, s["coef"], s["coef_adj"], s["b_re"], s["b_im"],
                                                    tr(s["b_re"]), tr(s["b_im"]), tr(s["c_re"]), tr(s["c_im"]), s["rev"])
        dus.append(du_d)
        dssm.append((dbre, dbim, dcre, dcim, da))
    dq, dk_t, dv_t = _flash_bwd(q, v, d_o, q_t, k_t, do_t, lse, delta.T.reshape(HEADS, 1, lp), mblk, p_all)
    dql, dkvl, dkr, dwq_p, dwk_p, dwv_p, vec_prep = _attn_prep_bwd(
        dq, dk_t, dv_t, proj, q_norm_w, kv_norm_w, wq_p.T, wk_p.T, wv_p.T, cos, sina, sinb)
    dh, dwin_p, vec_in = _in_proj_bwd(h, pre_norm_w, dres, dga, dus[0], dus[1], dyp, ssm_d, dsg, dql, dkvl, dkr, w_in_p.T)

    grads = {}
    grads["w_in"] = jnp.concatenate([
        dwin_p[:, P_QLAT[0]:P_QLAT[0] + 256], dwin_p[:, P_KVLAT[0]:P_KVLAT[0] + 128],
        dwin_p[:, P_KROPE[0] + QK_NOPE:P_KROPE[0] + QK_NOPE + QK_ROPE], _compact_heads(dwin_p[:, 0:D_EXP], 1, 0, V_HEAD),
        dwin_p[:, P_U[0]:P_U[0] + 512], dwin_p[:, P_GATE_S[0]:P_GATE_S[0] + 512]], axis=1)[None]
    grads["w_q_up"] = _compact_heads(dwq_p, 1, 0, QK_NOPE + QK_ROPE)[None]
    dwk3 = _compact_heads(dwk_p, 1, 0, QK_NOPE).reshape(KV_LORA, HEADS, QK_NOPE)
    dwv3 = _compact_heads(dwv_p, 1, 0, V_HEAD).reshape(KV_LORA, HEADS, V_HEAD)
    grads["w_kv_up"] = jnp.concatenate([dwk3, dwv3], axis=2).reshape(1, KV_LORA, HEADS * (QK_NOPE + V_HEAD))
    grads["w_glu"] = dwglu[None]
    grads["w_out"] = jnp.concatenate([_compact_heads(dwoa, 0, 0, V_HEAD), dwos], axis=0)[None]
    grads["meta_tokens"] = dh[:N_META]
    grads["pre_norm_w"] = vec_in[0:1]
    grads["post_norm_w"] = vec_mid[0:1]
    grads["q_norm_w"] = vec_prep[0:1]
    grads["kv_norm_w"] = vec_prep[1:2, :KV_LORA]
    grads["attn_out_norm_w"] = _compact_heads(vec_mid[1:2], 1, 0, V_HEAD)
    grads["ssm_out_norm_w"] = vec_mid[2:3, :D_SSM]
    grads["ssm_d"] = vec_mid[3:4, :D_SSM]
    grads["b_glu"] = vec_mid[4:5]
    d_abar_re = jnp.stack([dssm[d][4][0].sum(axis=0).reshape(N_GROUPS, SSM_STATE) for d in range(2)])
    d_abar_im = jnp.stack([dssm[d][4][1].sum(axis=0).reshape(N_GROUPS, SSM_STATE) for d in range(2)])
    d_bbar_re = jnp.stack([jnp.swapaxes(_block_diag_extract(dssm[d][0], SSM_GROUP, SSM_STATE), 1, 2) for d in range(2)])
    d_bbar_im = jnp.stack([jnp.swapaxes(_block_diag_extract(dssm[d][1], SSM_GROUP, SSM_STATE), 1, 2) for d in range(2)])
    da_re, da_im, dlog_dt, db_re, db_im = disc_vjp((d_abar_re, d_abar_im, d_bbar_re, d_bbar_im))
    grads["ssm_a_re"], grads["ssm_a_im"], grads["ssm_log_dt"] = da_re[None], da_im[None], dlog_dt[None]
    grads["ssm_b_re"], grads["ssm_b_im"] = db_re[None], db_im[None]
    grads["ssm_c_re"] = jnp.stack([_block_diag_extract(dssm[d][2], SSM_GROUP, SSM_STATE) for d in range(2)])[None]
    grads["ssm_c_im"] = jnp.stack([_block_diag_extract(dssm[d][3], SSM_GROUP, SSM_STATE) for d in range(2)])[None]

    def shard_of(n, a, kk):
        return a[:, kk * 256:(kk + 1) * 256] if n == "w_out" else _shard_cols(a, kk)

    slices = [_pack({n: shard_of(n, grads[n], kk) for n in BIG}, BIG) for kk in range(4)]
    grads["loss"] = vec_mid[5:6, 0:1]
    small = _pack({n: grads[n] for n in SMALL + ("loss",)}, SMALL + ("loss",))
    loss_row = slices[0].shape[0] + sum(-(-math.prod(shapes[n]) // 1024) for n in SMALL)
    rs, rsm = slices[0].shape[0], small.shape[0]
    g_pack = jnp.concatenate(slices + [small], axis=0)
    g_pair = _pair_sum(g_pack, _swap_sibling(g_pack))
    parts = _scatter_chips(g_pair, rs, rsm)

    order = BIG + SMALL
    big_shapes = {n: shapes[n] for n in BIG}
    small_shapes = {n: shapes[n] for n in SMALL}

    def pack_state(named):
        return jnp.concatenate([_pack({n: named[n] for n in BIG}, BIG), _pack({n: named[n] for n in SMALL}, SMALL)], axis=0)

    g_out, d_out, m_out, v_out = _adamw(parts, pack_state(local), pack_state(mom_m), pack_state(mom_v))

    def unpack_state(p):
        out = _unpack(p[:rs], big_shapes, BIG)
        out.update(_unpack(p[rs:], small_shapes, SMALL))
        return out

    g_fin, d_fin, m_fin, v_fin = unpack_state(g_out), unpack_state(d_out), unpack_state(m_out), unpack_state(v_out)
    loss = g_out[loss_row, 0]
    grad_x = dh[N_META:l_real][None]
    return (loss, grad_x, *[g_fin[n] for n in WEIGHTS], *[d_fin[n] for n in WEIGHTS], *[m_fin[n] for n in WEIGHTS],
            *[v_fin[n] for n in WEIGHTS])
```

```python
import functools
import math

import jax
import jax.numpy as jnp
from jax import lax
from jax.experimental import pallas as pl
from jax.experimental.pallas import tpu as pltpu

F32 = jnp.float32
BF16 = jnp.bfloat16
MESH = pl.DeviceIdType.MESH

D_MODEL = 1024
N_META = 16
EPS = 1e-6
HEADS = 8
QK_NOPE = 64
QK_ROPE = 32
V_HEAD = 64
VT_ROWS = 80
Q_LORA = 256
KV_LORA = 128
D_ATTN = 512
D_SSM = 512
SSM_GROUP = 16
N_GROUPS = 32
SSM_STATE = 64
N_STATE = N_GROUPS * SSM_STATE
ROPE_THETA = 10000.0
HEAD_PAD = 128
D_EXP = HEADS * HEAD_PAD
D_QK = QK_NOPE + QK_ROPE
MASK_LANE = D_QK
NEG_BIG = -1e30
SCALE = 1.0 / math.sqrt(QK_NOPE + QK_ROPE)
LOG2E = math.log2(math.e)
SCALE2 = SCALE * LOG2E
QBLK = 256
QUAD = 4
P_AHEAD = 2
SCAN_COLS = 1024
SCAN_UNROLL = 2
SSM_BLOCKS = 4
BLK_CH = D_SSM // SSM_BLOCKS
BLK_ST = N_STATE // SSM_BLOCKS

P_GATE_A = (0, 1024)
P_U = (1024, 512)
P_GATE_S = (1536, 512)
P_QLAT = (2048, 256)
P_KVLAT = (2304, 128)
P_KROPE = (2432, 128)
D_PROJ = 2560

ADAM_LR = 0.001
ADAM_B1 = 0.9
ADAM_B2 = 0.999
ADAM_EPS = 1e-08
ADAM_WD = 0.01
ADAM_STEP = 10

VMEM_LIMIT = 60 * 1024 * 1024

BIG = ("w_in", "w_q_up", "w_kv_up", "w_glu", "w_out", "meta_tokens")
SMALL = ("pre_norm_w", "post_norm_w", "q_norm_w", "kv_norm_w", "attn_out_norm_w", "ssm_a_re", "ssm_a_im",
         "ssm_log_dt", "ssm_b_re", "ssm_b_im", "ssm_c_re", "ssm_c_im", "ssm_d", "b_glu", "ssm_out_norm_w")
WEIGHTS = ("meta_tokens", "pre_norm_w", "post_norm_w", "w_in", "q_norm_w", "w_q_up", "kv_norm_w", "w_kv_up",
           "attn_out_norm_w", "ssm_a_re", "ssm_a_im", "ssm_log_dt", "ssm_b_re", "ssm_b_im", "ssm_c_re", "ssm_c_im",
           "ssm_d", "w_glu", "b_glu", "ssm_out_norm_w", "w_out")


def _cparams(sem=None):
    return pltpu.CompilerParams(dimension_semantics=sem, vmem_limit_bytes=VMEM_LIMIT)


def _dot(a, b):
    return jnp.dot(a, b, preferred_element_type=F32)


def _dot_nt(a, b):
    return lax.dot_general(a, b, (((1,), (1,)), ((), ())), preferred_element_type=F32)


def _dot_tn(a, b):
    return lax.dot_general(a, b, (((0,), (0,)), ((), ())), preferred_element_type=F32)


def _sigmoid(x):
    return 1.0 / (1.0 + jnp.exp(-x))


def _rms_fwd(x, w, n):
    r = lax.rsqrt(jnp.sum(x * x, axis=-1, keepdims=True) * (1.0 / n) + EPS)
    return x * r * w, r


def _rms_bwd(x, r, w, dy, n):
    dyw = dy * w
    dx = r * dyw - x * (r * r * r) * (jnp.sum(dyw * x, axis=-1, keepdims=True) * (1.0 / n))
    dw = jnp.sum(dy * (x * r), axis=0, keepdims=True)
    return dx, dw


def _rope_apply(x, cos, sina, sinb):
    return x * cos + pltpu.roll(x, 16, 1) * sina + pltpu.roll(x, HEAD_PAD - 16, 1) * sinb


def _rope_transpose(g, cos, sina, sinb):
    return g * cos + pltpu.roll(g * sina, HEAD_PAD - 16, 1) + pltpu.roll(g * sinb, 16, 1)


def _row_tile(lp):
    return 640 if lp % 640 == 0 else 128


def _ssm_tile(lp):
    return 320 if lp % 320 == 0 else 128


def _rows(tm, off_width):
    off, width = off_width
    return pl.BlockSpec((tm, width), lambda i: (i, off // width))


def _whole(shape, single=True):
    nd = len(shape)
    if single:
        return pl.BlockSpec(shape, lambda *_: (0,) * nd, pipeline_mode=pl.Buffered(1))
    return pl.BlockSpec(shape, lambda *_: (0,) * nd)


def _out_whole(shape):
    return _whole(shape, single=False)


def _pick_tile(rows, cap):
    best = 8
    for t in range(8, cap + 1, 8):
        if rows % t == 0:
            best = t
    return best


def _in_proj_fwd(h, pre_w, w_in_p):
    lp = h.shape[0]
    tm = _row_tile(lp)

    def body(h_ref, w_ref, win_ref, proj_ref):
        xn, _ = _rms_fwd(h_ref[...], w_ref[...], D_MODEL)
        proj_ref[...] = _dot(xn.astype(BF16), win_ref[...])

    return pl.pallas_call(
        body, name="in_proj_fwd", grid=(lp // tm,),
        in_specs=[_rows(tm, (0, D_MODEL)), _whole((1, D_MODEL)), _whole((D_MODEL, D_PROJ))],
        out_specs=_rows(tm, (0, D_PROJ)),
        out_shape=jax.ShapeDtypeStruct((lp, D_PROJ), F32),
        compiler_params=_cparams(("parallel",)),
    )(h, pre_w, w_in_p)


def _attn_prep_fwd(proj, q_norm_w, kv_norm_w, wq_p, wk_p, wv_p, wv_t, cos, sina, sinb, l_real):
    lp = proj.shape[0]
    tm = _row_tile(lp)

    def body(ql_ref, kvl_ref, kr_ref, qw_ref, kw_ref, wq_ref, wk_ref, wv_ref, wvt_ref, cos_ref, sa_ref, sb_ref,
             q_ref, k_ref, v_ref, vt_ref, qt_ref, kt_ref):
        cos_t, sa_t, sb_t = cos_ref[...], sa_ref[...], sb_ref[...]
        qn, _ = _rms_fwd(ql_ref[...], qw_ref[...], Q_LORA)
        kvn, _ = _rms_fwd(kvl_ref[...], kw_ref[...], KV_LORA)
        kvn_b = kvn.astype(BF16)
        qp = _dot(qn.astype(BF16), wq_ref[...])
        kp = _dot(kvn_b, wk_ref[...])
        v_ref[...] = _dot(kvn_b, wv_ref[...]).astype(BF16)
        ones_row = lax.broadcasted_iota(jnp.int32, (HEADS * VT_ROWS, 1), 0) % VT_ROWS == V_HEAD
        vt_ref[...] = jnp.where(ones_row, 1.0, _dot_nt(wvt_ref[...], kvn_b)).astype(BF16)
        lane = lax.broadcasted_iota(jnp.int32, (tm, HEAD_PAD), 1)
        row = lax.broadcasted_iota(jnp.int32, (tm, HEAD_PAD), 0) + pl.program_id(0) * tm
        q_one = jnp.where(lane == MASK_LANE, 1.0, 0.0)
        k_add = _rope_apply(kr_ref[...], cos_t, sa_t, sb_t) + jnp.where((lane == MASK_LANE) & (row >= l_real), NEG_BIG, 0.0)
        for hd in range(HEADS):
            blk = slice(hd * HEAD_PAD, (hd + 1) * HEAD_PAD)
            q_h = _rope_apply(qp[:, blk], cos_t, sa_t, sb_t) * SCALE2 + q_one
            k_h = kp[:, blk] + k_add
            q_ref[:, blk] = q_h.astype(BF16)
            k_ref[:, blk] = k_h.astype(BF16)
            qt_ref[hd * D_QK:(hd + 1) * D_QK, :] = q_h.T[:D_QK].astype(BF16)
            kt_ref[hd * D_QK:(hd + 1) * D_QK, :] = k_h.T[:D_QK].astype(BF16)

    tab = _rows(tm, (0, HEAD_PAD))
    out = jax.ShapeDtypeStruct((lp, D_EXP), BF16)
    out_t = jax.ShapeDtypeStruct((HEADS * D_QK, lp), BF16)
    cols_t = pl.BlockSpec((HEADS * D_QK, tm), lambda i: (0, i))
    return pl.pallas_call(
        body, name="attn_prep_fwd", grid=(lp // tm,),
        in_specs=[_rows(tm, P_QLAT), _rows(tm, P_KVLAT), _rows(tm, P_KROPE), _whole((1, Q_LORA)), _whole((1, KV_LORA)),
                  _whole((Q_LORA, D_EXP)), _whole((KV_LORA, D_EXP)), _whole((KV_LORA, D_EXP)),
                  _whole((HEADS * VT_ROWS, KV_LORA)), tab, tab, tab],
        out_specs=[_rows(tm, (0, D_EXP))] * 3 + [pl.BlockSpec((HEADS * VT_ROWS, tm), lambda i: (0, i)), cols_t, cols_t],
        out_shape=[out, out, out, jax.ShapeDtypeStruct((HEADS * VT_ROWS, lp), BF16), out_t, out_t],
        compiler_params=_cparams(("parallel",)),
    )(proj, proj, proj, q_norm_w, kv_norm_w, wq_p, wk_p, wv_p, wv_t, cos, sina, sinb)


def _flash_fwd(q, k, vt):
    lp = q.shape[0]
    tq = 1280 if lp % 1280 == 0 else 256
    tk = QBLK
    nk = lp // tk

    def body(q_ref, k_ref, vt_ref, o_ref, lse_ref, mblk_ref, p_hbm, acc, m_s, s_a, s_b, p_buf, p_sem):
        hd, qi = pl.program_id(0), pl.program_id(1)

        def p_copy(j, n, slot):
            return [pltpu.make_async_copy(p_buf.at[slot, b], p_hbm.at[hd, qi, j + b], p_sem.at[slot]) for b in range(n)]

        acc[...] = jnp.zeros_like(acc)
        m_s[...] = jnp.full(m_s.shape, NEG_BIG, F32)
        blocks = [slice(c * QBLK, (c + 1) * QBLK) for c in range(tq // QBLK)]

        def scores(j, buf):
            kt = k_ref[pl.ds(pl.multiple_of(j * tk, tk), tk), :]
            for cols in blocks:
                buf[:, cols] = _dot_nt(kt, q_ref[cols, :])

        def consume(j, buf, slot, b):
            vt_t = vt_ref[:, pl.ds(pl.multiple_of(j * tk, tk), tk)]
            m_old, acc_old = m_s[...], acc[...]
            s = [buf[:, cols] for cols in blocks]
            m_new = [jnp.maximum(m_old[:, cols], jnp.max(s_c, axis=0, keepdims=True)) for cols, s_c in zip(blocks, s)]
            p = [jnp.exp2(s_c - m_c).astype(BF16) for s_c, m_c in zip(s, m_new)]
            pv = [_dot(vt_t, p_c) for p_c in p]
            m_new = jnp.concatenate(m_new, axis=1)
            alpha = jnp.exp2(m_old - m_new)
            acc[...] = alpha * acc_old + jnp.concatenate(pv, axis=1)
            m_s[...] = m_new
            mblk_ref[j] = m_new
            p_buf[slot, b] = jnp.concatenate(p, axis=1)

        quads = (nk - 1) // QUAD
        scores(0, s_a)

        def quad(t, _):
            j, slot = QUAD * t, t % 2

            @pl.when(t >= 2)
            def _():
                for cp in p_copy(j - 2 * QUAD, QUAD, slot):
                    cp.wait()

            for b in range(QUAD):
                scores(j + b + 1, s_a if b % 2 else s_b)
                consume(j + b, s_b if b % 2 else s_a, slot, b)
            for cp in p_copy(j, QUAD, slot):
                cp.start()
            return 0

        lax.fori_loop(0, quads, quad, 0)
        for back in (2, 1):
            if quads >= back:
                for cp in p_copy(QUAD * (quads - back), QUAD, (quads - back) % 2):
                    cp.wait()
        rest = nk - QUAD * quads
        for b in range(rest):
            if b + 1 < rest:
                scores(QUAD * quads + b + 1, s_a if b % 2 else s_b)
            consume(QUAD * quads + b, s_b if b % 2 else s_a, 0, b)
        for cp in p_copy(QUAD * quads, rest, 0):
            cp.start()
        for cp in p_copy(QUAD * quads, rest, 0):
            cp.wait()
        l = acc[V_HEAD:V_HEAD + 1, :]
        o_t = acc[0:V_HEAD, :] / l
        o_ref[...] = jnp.concatenate([o_t, jnp.zeros_like(o_t)], axis=0).T
        lse_ref[...] = m_s[...] + jnp.log2(l)

    return pl.pallas_call(
        body, name="flash_fwd", grid=(HEADS, lp // tq),
        in_specs=[pl.BlockSpec((tq, HEAD_PAD), lambda hd, i: (i, hd)),
                  pl.BlockSpec((lp, HEAD_PAD), lambda hd, i: (0, hd)),
                  pl.BlockSpec((VT_ROWS, lp), lambda hd, i: (hd, 0))],
        out_specs=[pl.BlockSpec((tq, HEAD_PAD), lambda hd, i: (i, hd)),
                   pl.BlockSpec((None, 1, tq), lambda hd, i: (hd, 0, i)),
                   pl.BlockSpec((None, None, nk, 1, tq), lambda hd, i: (hd, i, 0, 0, 0)),
                   pl.BlockSpec(memory_space=pl.ANY)],
        out_shape=[jax.ShapeDtypeStruct((lp, D_EXP), F32), jax.ShapeDtypeStruct((HEADS, 1, lp), F32),
                   jax.ShapeDtypeStruct((HEADS, lp // tq, nk, 1, tq), F32),
                   jax.ShapeDtypeStruct((HEADS, lp // tq, nk, tk, tq), BF16)],
        scratch_shapes=[pltpu.VMEM((VT_ROWS, tq), F32), pltpu.VMEM((1, tq), F32),
                        pltpu.VMEM((tk, tq), F32), pltpu.VMEM((tk, tq), F32),
                        pltpu.VMEM((2, QUAD, tk, tq), BF16), pltpu.SemaphoreType.DMA((2,))],
        compiler_params=_cparams(("parallel", "parallel")),
    )(q, k, vt)


def _unpermute_rows(val, scr, out_ref, seg):
    for c in range(val.shape[1] // 128):
        scr[c] = val[:, c * 128:(c + 1) * 128]
    for k in range(8):
        for c in range(val.shape[1] // 128):
            out_ref[k * seg:(k + 1) * seg, c * 128:(c + 1) * 128] = scr[c, pl.ds(k, seg, stride=8), :]


def _scan_rows(xr_ref, xi_ref, base, n_rows, coef_ref, carry_ref, reverse, tile_fn=None, acc_refs=(), halo=False):
    seg = n_rows // 8
    shifts = (7, 6, 4) if reverse else (1, 2, 4)
    row8 = lax.broadcasted_iota(jnp.int32, (8, SCAN_COLS), 0)
    edge, shift = (7, 7) if reverse else (0, 1)
    for cg in range(N_STATE // SCAN_COLS):
        cols = slice(cg * SCAN_COLS, (cg + 1) * SCAN_COLS)
        ar, ai = coef_ref[8, :, cols], coef_ref[9, :, cols]

        def rows_at(i):
            tau = (seg - 1 - i) if reverse else i
            return tau, pl.ds(pl.multiple_of(base + tau * 8, 8), 8)

        def local(i, carry, cols=cols, ar=ar, ai=ai):
            pr, pi_ = carry
            _, rows = rows_at(i)
            nr = ar * pr - ai * pi_ + xr_ref[rows, cols]
            ni = ar * pi_ + ai * pr + xi_ref[rows, cols]
            xr_ref[rows, cols] = nr
            xi_ref[rows, cols] = ni
            return nr, ni

        zero = jnp.zeros((8, SCAN_COLS), F32)
        fr, fi = lax.fori_loop(0, seg, local, (zero, zero), unroll=SCAN_UNROLL)
        co = [coef_ref[k, :, cols] for k in range(8)]
        for lvl in range(3):
            pr, pi_ = co[2 * lvl], co[2 * lvl + 1]
            sr = pltpu.roll(fr, shifts[lvl], 0)
            si = pltpu.roll(fi, shifts[lvl], 0)
            fr, fi = fr + pr * sr - pi_ * si, fi + pr * si + pi_ * sr
        cr, ci = carry_ref[0:1, cols], carry_ref[1:2, cols]
        fr, fi = fr + co[6] * cr - co[7] * ci, fi + co[6] * ci + co[7] * cr
        carry_ref[0:1, cols] = fr[0:1] if reverse else fr[7:8]
        carry_ref[1:2, cols] = fi[0:1] if reverse else fi[7:8]
        in_r = jnp.where(row8 == edge, cr, pltpu.roll(fr, shift, 0))
        in_i = jnp.where(row8 == edge, ci, pltpu.roll(fi, shift, 0))
        if halo:
            rows = pl.ds(base + n_rows, 8) if reverse else pl.ds(base - 8, 8)
            xr_ref[rows, cols] = in_r
            xi_ref[rows, cols] = in_i

        def fix(i, carry, cols=cols, ar=ar, ai=ai):
            c_r, c_i = carry[0], carry[1]
            tau, rows = rows_at(i)
            nr = xr_ref[rows, cols] + c_r
            ni = xi_ref[rows, cols] + c_i
            xr_ref[rows, cols] = nr
            xi_ref[rows, cols] = ni
            accs = carry[2:]
            if tile_fn is not None:
                accs = tuple(a + d for a, d in zip(accs, tile_fn(tau, cols, nr, ni)))
            return (ar * c_r - ai * c_i, ar * c_i + ai * c_r) + accs

        init = (ar * in_r - ai * in_i, ar * in_i + ai * in_r) + tuple(a[:, cols] for a in acc_refs)
        out = lax.fori_loop(0, seg, fix, init, unroll=SCAN_UNROLL)
        for a, val in zip(acc_refs, out[2:]):
            a[:, cols] = val


def _ssm_fwd(proj, perm, coef, b_re, b_im, c_re, c_im_neg, reverse):
    lp = proj.shape[0]
    t = _ssm_tile(lp)
    n = lp // t
    order = (lambda i: n - 1 - i) if reverse else (lambda i: i)

    def body(u_ref, pm_ref, coef_ref, bre_ref, bim_ref, cre_ref, cim_ref, y_ref, st_ref, xr, xi, carry, stage):
        @pl.when(pl.program_id(0) == 0)
        def _():
            carry[...] = jnp.zeros_like(carry)

        st_ref[...] = carry[0:2, :]
        ub = _dot(pm_ref[...], u_ref[...].astype(BF16)).astype(BF16)
        for j in range(SSM_BLOCKS):
            ch, stt = slice(j * BLK_CH, (j + 1) * BLK_CH), slice(j * BLK_ST, (j + 1) * BLK_ST)
            xr[:, stt] = _dot(ub[:, ch], bre_ref[j])
            xi[:, stt] = _dot(ub[:, ch], bim_ref[j])
        _scan_rows(xr, xi, 0, t, coef_ref, carry, reverse)
        y = jnp.concatenate(
            [_dot(xr[:, j * BLK_ST:(j + 1) * BLK_ST].astype(BF16), cre_ref[j])
             + _dot(xi[:, j * BLK_ST:(j + 1) * BLK_ST].astype(BF16), cim_ref[j]) for j in range(SSM_BLOCKS)], axis=1)
        _unpermute_rows(y, stage, y_ref, t // 8)

    wb, wc = _whole((SSM_BLOCKS, BLK_CH, BLK_ST)), _whole((SSM_BLOCKS, BLK_ST, BLK_CH))
    return pl.pallas_call(
        body, name="ssm_fwd_rev" if reverse else "ssm_fwd", grid=(n,),
        in_specs=[pl.BlockSpec((t, D_SSM), lambda i: (order(i), P_U[0] // D_SSM)), _whole((t, t)), _whole((10, 8, N_STATE)),
                  wb, wb, wc, wc],
        out_specs=[pl.BlockSpec((t, D_SSM), lambda i: (order(i), 0)),
                   pl.BlockSpec((None, 2, N_STATE), lambda i: (order(i), 0, 0))],
        out_shape=[jax.ShapeDtypeStruct((lp, D_SSM), F32), jax.ShapeDtypeStruct((n, 2, N_STATE), F32)],
        scratch_shapes=[pltpu.VMEM((t, N_STATE), F32), pltpu.VMEM((t, N_STATE), F32), pltpu.VMEM((8, N_STATE), F32),
                        pltpu.VMEM((D_SSM // 128, t, 128), F32)],
        compiler_params=_cparams(("arbitrary",)),
    )(proj, perm, coef, b_re, b_im, c_re, c_im_neg)


GELU_C0 = math.sqrt(2.0 / math.pi)
GELU_C1 = 0.044715


def _mid(h, tgt, o_exp, proj, y0, y1, ssm_d, w_glu, w_glu_t, b_glu, ssm_norm_w, attn_norm_w_e, w_out_a, w_out_s,
         w_out_a_t, w_out_s_t, post_w, l_real):
    lp = h.shape[0]
    tm = 256

    def body(h_ref, tga_ref, tgb_ref, o_ref, ga_ref, u_ref, sg_ref, y0_ref, y1_ref, d_ref, wg_ref, wgt_ref, bg_ref, ws_ref,
             wa_ref, woa_ref, wos_ref, woat_ref, wost_ref, pw_ref,
             do_ref, dot_ref, delta_ref, dga_ref, dyp_ref, dsg_ref, dres_ref, dwoa_ref, dwos_ref, dwg_ref, vec_ref):
        @pl.when(pl.program_id(0) == 0)
        def _():
            dwoa_ref[...] = jnp.zeros_like(dwoa_ref)
            dwos_ref[...] = jnp.zeros_like(dwos_ref)
            dwg_ref[...] = jnp.zeros_like(dwg_ref)
            vec_ref[...] = jnp.zeros_like(vec_ref)

        u = u_ref[...]
        ypre = y0_ref[...] + y1_ref[...] + d_ref[...] * u
        th = jnp.tanh(GELU_C0 * (ypre + GELU_C1 * ypre * ypre * ypre))
        gel = 0.5 * ypre * (1.0 + th)
        gel_b = gel.astype(BF16)
        glu = _dot(gel_b, wg_ref[...]) + bg_ref[...]
        g1, g2 = glu[:, :D_SSM], glu[:, D_SSM:]
        sig2 = _sigmoid(g2)
        z = g1 * sig2
        sg = sg_ref[...]
        sgs = _sigmoid(sg)
        sil_s = sg * sgs
        s = z * sil_s
        ys, r_s = _rms_fwd(s, ws_ref[...], D_SSM)

        o = o_ref[...]
        ga = ga_ref[...]
        gas = _sigmoid(ga)
        sil_a = ga * gas
        a = o * sil_a
        ya, r_a = _rms_fwd(a, wa_ref[...], D_ATTN)

        ya_b, ys_b = ya.astype(BF16), ys.astype(BF16)
        y = _dot(ya_b, woa_ref[...]) + _dot(ys_b, wos_ref[...])
        yn, r_y = _rms_fwd(y, pw_ref[...], D_MODEL)
        row = lax.broadcasted_iota(jnp.int32, (tm, 1), 0) + pl.program_id(0) * tm
        valid = (row >= N_META) & (row < l_real)
        tgt = jnp.concatenate([tga_ref[tm - N_META:, :], tgb_ref[:tm - N_META, :]], axis=0)
        err = jnp.where(valid, h_ref[...] + yn - tgt, 0.0)
        loss = 0.5 * jnp.sum(jnp.sum(err * err, axis=-1, keepdims=True), axis=0, keepdims=True) * (1.0 / D_MODEL)
        dout = err * (1.0 / D_MODEL)
        dres_ref[...] = dout

        dy, d_pw = _rms_bwd(y, r_y, pw_ref[...], dout, D_MODEL)
        dy_b = dy.astype(BF16)
        dya = _dot(dy_b, woat_ref[...])
        dys = _dot(dy_b, wost_ref[...])
        dwoa_ref[...] += _dot_tn(ya_b, dy_b)
        dwos_ref[...] += _dot_tn(ys_b, dy_b)

        da, d_wa = _rms_bwd(a, r_a, wa_ref[...], dya, D_ATTN)
        d_o = da * sil_a
        dga_ref[...] = da * o * (gas * (1.0 + ga * (1.0 - gas)))
        do_ref[...] = d_o.astype(BF16)
        for hd in range(HEADS):
            dot_ref[hd * V_HEAD:(hd + 1) * V_HEAD, :] = d_o[:, hd * HEAD_PAD:(hd + 1) * HEAD_PAD].T[:V_HEAD].astype(BF16)
        prod = d_o * o
        lane8 = lax.broadcasted_iota(jnp.int32, (tm, HEADS), 1)
        delta = jnp.zeros((tm, HEADS), F32)
        for hd in range(HEADS):
            delta = jnp.where(lane8 == hd, jnp.sum(prod[:, hd * HEAD_PAD:(hd + 1) * HEAD_PAD], axis=-1, keepdims=True), delta)
        delta_ref[...] = delta

        ds, d_ws = _rms_bwd(s, r_s, ws_ref[...], dys, D_SSM)
        dz = ds * sil_s
        dsg_ref[...] = ds * z * (sgs * (1.0 + sg * (1.0 - sgs)))
        dglu = jnp.concatenate([dz * sig2, dz * g1 * sig2 * (1.0 - sig2)], axis=-1)
        dglu_b = dglu.astype(BF16)
        dwg_ref[...] += _dot_tn(gel_b, dglu_b)
        dgel = _dot(dglu_b, wgt_ref[...])
        dgelu = 0.5 * (1.0 + th) + 0.5 * ypre * (1.0 - th * th) * (GELU_C0 * (1.0 + 3.0 * GELU_C1 * ypre * ypre))
        dyp = dgel * dgelu
        dyp_ref[...] = dyp

        vec_ref[0:1, :] += d_pw
        vec_ref[1:2, :] += d_wa
        vec_ref[2:3, 0:D_SSM] += d_ws
        vec_ref[3:4, 0:D_SSM] += jnp.sum(dyp * u, axis=0, keepdims=True)
        vec_ref[4:5, :] += jnp.sum(dglu, axis=0, keepdims=True)
        vec_ref[5:6, :] += jnp.broadcast_to(loss, (1, D_MODEL))

    full = lambda off: _rows(tm, (off, D_MODEL))
    half = lambda off: _rows(tm, (off, D_SSM))
    last = tgt.shape[0] // tm - 1
    tg_a = pl.BlockSpec((tm, D_MODEL), lambda i: (jnp.clip(i - 1, 0, last), 0))
    tg_b = pl.BlockSpec((tm, D_MODEL), lambda i: (jnp.minimum(i, last), 0))
    return pl.pallas_call(
        body, name="mid", grid=(lp // tm,),
        in_specs=[full(0), tg_a, tg_b, full(0), _rows(tm, P_GATE_A), _rows(tm, P_U), _rows(tm, P_GATE_S), half(0), half(0),
                  _whole((1, D_SSM)), _whole((D_SSM, 2 * D_SSM)), _whole((2 * D_SSM, D_SSM)), _whole((1, 2 * D_SSM)),
                  _whole((1, D_SSM)), _whole((1, D_EXP)), _whole((D_EXP, D_MODEL)), _whole((D_SSM, D_MODEL)),
                  _whole((D_MODEL, D_EXP)), _whole((D_MODEL, D_SSM)), _whole((1, D_MODEL))],
        out_specs=[full(0), pl.BlockSpec((D_ATTN, tm), lambda i: (0, i)), _rows(tm, (0, HEADS)), full(0), half(0), half(0), full(0),
                   _out_whole((D_EXP, D_MODEL)), _out_whole((D_SSM, D_MODEL)), _out_whole((D_SSM, 2 * D_SSM)),
                   _out_whole((8, D_MODEL))],
        out_shape=[jax.ShapeDtypeStruct((lp, D_EXP), BF16), jax.ShapeDtypeStruct((D_ATTN, lp), BF16),
                   jax.ShapeDtypeStruct((lp, HEADS), F32),
                   jax.ShapeDtypeStruct((lp, D_EXP), F32), jax.ShapeDtypeStruct((lp, D_SSM), F32),
                   jax.ShapeDtypeStruct((lp, D_SSM), F32), jax.ShapeDtypeStruct((lp, D_MODEL), F32),
                   jax.ShapeDtypeStruct((D_EXP, D_MODEL), F32), jax.ShapeDtypeStruct((D_SSM, D_MODEL), F32),
                   jax.ShapeDtypeStruct((D_SSM, 2 * D_SSM), F32), jax.ShapeDtypeStruct((8, D_MODEL), F32)],
        compiler_params=_cparams(("arbitrary",)),
    )(h, tgt, tgt, o_exp, proj, proj, proj, y0, y1, ssm_d, w_glu, w_glu_t, b_glu, ssm_norm_w, attn_norm_w_e, w_out_a, w_out_s,
      w_out_a_t, w_out_s_t, post_w)


def _ssm_bwd(proj, dyp, states, perm, coef, coef_adj, b_re, b_im, b_re_t, b_im_t, c_re_t, c_im_neg_t, reverse):
    lp = proj.shape[0]
    t = _ssm_tile(lp)
    n = lp // t
    order = (lambda i: i) if reverse else (lambda i: n - 1 - i)

    def body(u_ref, dy_ref, st_ref, pm_ref, coef_ref, coefa_ref, bre_ref, bim_ref, bret_ref, bimt_ref, cret_ref, cimt_ref,
             du_ref, dbre_ref, dbim_ref, dcre_ref, dcim_ref, da_ref, xr, xi, gr, gi, carry_x, carry_g, stage):
        @pl.when(pl.program_id(0) == 0)
        def _():
            carry_g[...] = jnp.zeros_like(carry_g)
            carry_x[...] = jnp.zeros_like(carry_x)
            dbre_ref[...] = jnp.zeros_like(dbre_ref)
            dbim_ref[...] = jnp.zeros_like(dbim_ref)
            dcre_ref[...] = jnp.zeros_like(dcre_ref)
            dcim_ref[...] = jnp.zeros_like(dcim_ref)
            da_ref[...] = jnp.zeros_like(da_ref)
            for halo in (slice(0, 8), slice(t + 8, t + 16)):
                xr[halo, :] = jnp.zeros((8, N_STATE), F32)
                xi[halo, :] = jnp.zeros((8, N_STATE), F32)

        ub = _dot(pm_ref[...], u_ref[...].astype(BF16)).astype(BF16)
        dyb = _dot(pm_ref[...], dy_ref[...].astype(BF16)).astype(BF16)
        carry_x[0:2, :] = st_ref[...]
        blocks = [(slice(j * BLK_CH, (j + 1) * BLK_CH), slice(j * BLK_ST, (j + 1) * BLK_ST)) for j in range(SSM_BLOCKS)]
        for j, (ch, stt) in enumerate(blocks):
            xr[8:t + 8, stt] = _dot(ub[:, ch], bre_ref[j])
            xi[8:t + 8, stt] = _dot(ub[:, ch], bim_ref[j])
            gr[:, stt] = _dot(dyb[:, ch], cret_ref[j])
            gi[:, stt] = _dot(dyb[:, ch], cimt_ref[j])
        _scan_rows(xr, xi, 8, t, coef_ref, carry_x, reverse, halo=True)

        def tile_fn(tau, cols, g_re, g_im):
            nb = pl.ds(pl.multiple_of((tau + 2) * 8 if reverse else tau * 8, 8), 8)
            xn_r, xn_i = xr[nb, cols], xi[nb, cols]
            return g_re * xn_r + g_im * xn_i, g_im * xn_r - g_re * xn_i

        _scan_rows(gr, gi, 0, t, coefa_ref, carry_g, not reverse, tile_fn=tile_fn, acc_refs=(da_ref.at[0], da_ref.at[1]))

        du = []
        for j, (ch, stt) in enumerate(blocks):
            g_re_b, g_im_b = gr[:, stt].astype(BF16), gi[:, stt].astype(BF16)
            du.append(_dot(g_re_b, bret_ref[j]) + _dot(g_im_b, bimt_ref[j]))
            dbre_ref[j] += _dot_tn(ub[:, ch], g_re_b)
            dbim_ref[j] += _dot_tn(ub[:, ch], g_im_b)
            dcre_ref[j] += _dot_tn(dyb[:, ch], xr[8:t + 8, stt].astype(BF16))
            dcim_ref[j] -= _dot_tn(dyb[:, ch], xi[8:t + 8, stt].astype(BF16))
        _unpermute_rows(jnp.concatenate(du, axis=1), stage, du_ref, t // 8)

    dense = jax.ShapeDtypeStruct((SSM_BLOCKS, BLK_CH, BLK_ST), F32)
    wb, wc = _whole((SSM_BLOCKS, BLK_CH, BLK_ST)), _whole((SSM_BLOCKS, BLK_ST, BLK_CH))
    acc = _out_whole((SSM_BLOCKS, BLK_CH, BLK_ST))
    return pl.pallas_call(
        body, name="ssm_bwd_rev" if reverse else "ssm_bwd", grid=(n,),
        in_specs=[pl.BlockSpec((t, D_SSM), lambda i: (order(i), P_U[0] // D_SSM)),
                  pl.BlockSpec((t, D_SSM), lambda i: (order(i), 0)),
                  pl.BlockSpec((None, 2, N_STATE), lambda i: (order(i), 0, 0)), _whole((t, t)),
                  _whole((10, 8, N_STATE)), _whole((10, 8, N_STATE)), wb, wb, wc, wc, wb, wb],
        out_specs=[pl.BlockSpec((t, D_SSM), lambda i: (order(i), 0)), acc, acc, acc, acc, _out_whole((2, 8, N_STATE))],
        out_shape=[jax.ShapeDtypeStruct((lp, D_SSM), F32), dense, dense, dense, dense,
                   jax.ShapeDtypeStruct((2, 8, N_STATE), F32)],
        scratch_shapes=[pltpu.VMEM((t + 16, N_STATE), F32), pltpu.VMEM((t + 16, N_STATE), F32),
                        pltpu.VMEM((t, N_STATE), F32), pltpu.VMEM((t, N_STATE), F32),
                        pltpu.VMEM((8, N_STATE), F32), pltpu.VMEM((8, N_STATE), F32),
                        pltpu.VMEM((D_SSM // 128, t, 128), F32)],
        compiler_params=_cparams(("arbitrary",)),
    )(proj, dyp, states, perm, coef, coef_adj, b_re, b_im, b_re_t, b_im_t, c_re_t, c_im_neg_t)


def _flash_bwd(q, v, d_o, q_t, k_t, do_t, lse_row, delta_row, mblk, p_all):
    lp = q.shape[0]
    tq = 1280 if lp % 1280 == 0 else 256
    tk = QBLK
    nk = lp // tk
    d_qk = QK_NOPE + QK_ROPE
    grp = 5 if nk % 5 == 0 else 1
    n_groups = nk // grp

    def body(do_ref, qt_ref, dot_ref, lse_ref, delta_ref, mblk_ref, v_ref, kt_ref, p_hbm, dq_ref, dk_ref, dv_ref,
             dq_acc, p_buf, p_sem):
        hd, qi = pl.program_id(0), pl.program_id(1)

        def p_copy(t, slot):
            return [pltpu.make_async_copy(p_hbm.at[hd, qi, t * grp + u], p_buf.at[slot, u], p_sem.at[slot]) for u in range(grp)]

        for t0 in range(min(P_AHEAD, n_groups)):
            for cp in p_copy(t0, t0):
                cp.start()

        @pl.when(qi == 0)
        def _():
            dk_ref[...] = jnp.zeros_like(dk_ref)
            dv_ref[...] = jnp.zeros_like(dv_ref)

        dq_acc[...] = jnp.zeros_like(dq_acc)
        lse, delta = lse_ref[...], delta_ref[...]
        q_cols, do_cols = qt_ref[...], dot_ref[...]
        blocks = [slice(c * QBLK, (c + 1) * QBLK) for c in range(tq // QBLK)]

        def group(t, _):
            base = pl.multiple_of(t * (grp * tk), grp * tk)
            slot = t % (P_AHEAD + 1)
            for cp in p_copy(t, slot):
                cp.wait()

            @pl.when(t + P_AHEAD < n_groups)
            def _():
                for cp in p_copy(t + P_AHEAD, (t + P_AHEAD) % (P_AHEAD + 1)):
                    cp.start()

            dq = dq_acc[...]
            dvs, dks = [], []
            for u in range(grp):
                j = t * grp + u
                ks = pl.multiple_of(base + u * tk, tk)
                v_rows = v_ref[pl.ds(ks, tk), :]
                dpt = [_dot_nt(v_rows, do_ref[cols, :]) for cols in blocks]
                pt = p_buf[slot, u].astype(F32) * jnp.exp2(mblk_ref[j] - lse)
                pt_b = pt.astype(BF16)
                dst_b = jnp.concatenate([(pt[:, cols] * (dp_c - delta[:, cols])).astype(BF16)
                                         for dp_c, cols in zip(dpt, blocks)], axis=1)
                dvs.append(_dot_nt(do_cols, pt_b))
                dks.append(_dot_nt(q_cols, dst_b))
                dq = dq + _dot(kt_ref[:, pl.ds(ks, tk)], dst_b)
            dq_acc[...] = dq
            dv_ref[:, pl.ds(base, grp * tk)] += jnp.concatenate(dvs, axis=1)
            dk_ref[:, pl.ds(base, grp * tk)] += jnp.concatenate(dks, axis=1) * (1.0 / LOG2E)
            return 0

        lax.fori_loop(0, n_groups, group, 0)
        dq_ref[...] = jnp.concatenate([dq_acc[...], jnp.zeros((HEAD_PAD - d_qk, tq), F32)], axis=0).T

    tile = pl.BlockSpec((tq, HEAD_PAD), lambda hd, i: (i, hd))
    head = pl.BlockSpec((lp, HEAD_PAD), lambda hd, i: (0, hd))
    rowv = pl.BlockSpec((None, 1, tq), lambda hd, i: (hd, 0, i))
    return pl.pallas_call(
        body, name="flash_bwd", grid=(HEADS, lp // tq),
        in_specs=[tile, pl.BlockSpec((d_qk, tq), lambda hd, i: (hd, i)), pl.BlockSpec((V_HEAD, tq), lambda hd, i: (hd, i)),
                  rowv, rowv, pl.BlockSpec((None, None, nk, 1, tq), lambda hd, i: (hd, i, 0, 0, 0)), head,
                  pl.BlockSpec((d_qk, lp), lambda hd, i: (hd, 0)), pl.BlockSpec(memory_space=pl.ANY)],
        out_specs=[tile, pl.BlockSpec((d_qk, lp), lambda hd, i: (hd, 0)), pl.BlockSpec((V_HEAD, lp), lambda hd, i: (hd, 0))],
        out_shape=[jax.ShapeDtypeStruct((lp, D_EXP), F32), jax.ShapeDtypeStruct((HEADS * d_qk, lp), F32),
                   jax.ShapeDtypeStruct((HEADS * V_HEAD, lp), F32)],
        scratch_shapes=[pltpu.VMEM((d_qk, tq), F32), pltpu.VMEM((P_AHEAD + 1, grp, tk, tq), BF16),
                        pltpu.SemaphoreType.DMA((P_AHEAD + 1,))],
        compiler_params=_cparams(("parallel", "arbitrary")),
    )(d_o, q_t, do_t, lse_row, delta_row, mblk, v, k_t, p_all)


def _attn_prep_bwd(dq, dk_t, dv_t, proj, q_norm_w, kv_norm_w, wq_pt, wk_pt, wv_pt, cos, sina, sinb):
    lp = proj.shape[0]
    tm = _row_tile(lp)

    def body(dq_ref, dk_ref, dv_ref, ql_ref, kvl_ref, qw_ref, kw_ref, wqt_ref, wkt_ref, wvt_ref, cos_ref, sa_ref, sb_ref,
             dql_ref, dkvl_ref, dkr_ref, dwq_ref, dwk_ref, dwv_ref, vec_ref):
        @pl.when(pl.program_id(0) == 0)
        def _():
            dwq_ref[...] = jnp.zeros_like(dwq_ref)
            dwk_ref[...] = jnp.zeros_like(dwk_ref)
            dwv_ref[...] = jnp.zeros_like(dwv_ref)
            vec_ref[...] = jnp.zeros_like(vec_ref)

        cos_t, sa_t, sb_t = cos_ref[...], sa_ref[...], sb_ref[...]

        def head_rows(t_ref, per):
            pad = jnp.zeros((HEAD_PAD - per, tm), F32)
            return jnp.concatenate(
                [jnp.concatenate([t_ref[hd * per:(hd + 1) * per, :], pad], axis=0).T for hd in range(HEADS)], axis=-1)

        dkp = head_rows(dk_ref, D_QK)
        dqp = jnp.concatenate(
            [_rope_transpose(dq_ref[:, hd * HEAD_PAD:(hd + 1) * HEAD_PAD] * SCALE, cos_t, sa_t, sb_t) for hd in range(HEADS)],
            axis=-1)
        dkr = dkp[:, 0:HEAD_PAD]
        for hd in range(1, HEADS):
            dkr = dkr + dkp[:, hd * HEAD_PAD:(hd + 1) * HEAD_PAD]
        dkr_ref[...] = _rope_transpose(dkr, cos_t, sa_t, sb_t)

        qn, r_q = _rms_fwd(ql_ref[...], qw_ref[...], Q_LORA)
        kvn, r_kv = _rms_fwd(kvl_ref[...], kw_ref[...], KV_LORA)
        dqp_b, dkp_b, dv_b = dqp.astype(BF16), dkp.astype(BF16), head_rows(dv_ref, V_HEAD).astype(BF16)
        dqn = _dot(dqp_b, wqt_ref[...])
        dkvn = _dot(dkp_b, wkt_ref[...]) + _dot(dv_b, wvt_ref[...])
        dwq_ref[...] += _dot_tn(qn.astype(BF16), dqp_b)
        dwk_ref[...] += _dot_tn(kvn.astype(BF16), dkp_b)
        dwv_ref[...] += _dot_tn(kvn.astype(BF16), dv_b)
        dql, d_qw = _rms_bwd(ql_ref[...], r_q, qw_ref[...], dqn, Q_LORA)
        dkvl, d_kw = _rms_bwd(kvl_ref[...], r_kv, kw_ref[...], dkvn, KV_LORA)
        dql_ref[...] = dql
        dkvl_ref[...] = dkvl
        vec_ref[0:1, :] += d_qw
        vec_ref[1:2, 0:KV_LORA] += d_kw

    tab = _rows(tm, (0, HEAD_PAD))
    full = _rows(tm, (0, D_EXP))
    return pl.pallas_call(
        body, name="attn_prep_bwd", grid=(lp // tm,),
        in_specs=[full, pl.BlockSpec((HEADS * D_QK, tm), lambda i: (0, i)), pl.BlockSpec((D_ATTN, tm), lambda i: (0, i)),
                  _rows(tm, P_QLAT), _rows(tm, P_KVLAT), _whole((1, Q_LORA)), _whole((1, KV_LORA)),
                  _whole((D_EXP, Q_LORA)), _whole((D_EXP, KV_LORA)), _whole((D_EXP, KV_LORA)), tab, tab, tab],
        out_specs=[_rows(tm, (0, Q_LORA)), _rows(tm, (0, KV_LORA)), _rows(tm, (0, HEAD_PAD)),
                   _out_whole((Q_LORA, D_EXP)), _out_whole((KV_LORA, D_EXP)), _out_whole((KV_LORA, D_EXP)),
                   _out_whole((8, Q_LORA))],
        out_shape=[jax.ShapeDtypeStruct((lp, Q_LORA), F32), jax.ShapeDtypeStruct((lp, KV_LORA), F32),
                   jax.ShapeDtypeStruct((lp, HEAD_PAD), F32), jax.ShapeDtypeStruct((Q_LORA, D_EXP), F32),
                   jax.ShapeDtypeStruct((KV_LORA, D_EXP), F32), jax.ShapeDtypeStruct((KV_LORA, D_EXP), F32),
                   jax.ShapeDtypeStruct((8, Q_LORA), F32)],
        compiler_params=_cparams(("arbitrary",)),
    )(dq, dk_t, dv_t, proj, proj, q_norm_w, kv_norm_w, wq_pt, wk_pt, wv_pt, cos, sina, sinb)


def _in_proj_bwd(h, pre_w, dres, dga, du0, du1, dyp, ssm_d, dsg, dql, dkvl, dkr, w_in_pt):
    lp = h.shape[0]
    tm = 256
    pieces = (P_GATE_A, P_U, P_GATE_S, P_QLAT, P_KVLAT, P_KROPE)

    def body(h_ref, w_ref, dres_ref, dga_ref, du0_ref, du1_ref, dyp_ref, d_ref, dsg_ref, dql_ref, dkvl_ref, dkr_ref, wt_ref,
             dh_ref, dw_ref, vec_ref):
        @pl.when(pl.program_id(0) == 0)
        def _():
            dw_ref[...] = jnp.zeros_like(dw_ref)
            vec_ref[...] = jnp.zeros_like(vec_ref)

        hv = h_ref[...]
        xn, r = _rms_fwd(hv, w_ref[...], D_MODEL)
        xn_b = xn.astype(BF16)
        du = du0_ref[...] + du1_ref[...] + dyp_ref[...] * d_ref[...]
        grads = (dga_ref[...], du, dsg_ref[...], dql_ref[...], dkvl_ref[...], dkr_ref[...])
        dxn = jnp.zeros((tm, D_MODEL), F32)
        for (off, width), g in zip(pieces, grads):
            g_b = g.astype(BF16)
            dxn = dxn + _dot(g_b, wt_ref[off:off + width, :])
            dw_ref[:, off:off + width] += _dot_tn(xn_b, g_b)
        dx, d_w = _rms_bwd(hv, r, w_ref[...], dxn, D_MODEL)
        dh_ref[...] = dres_ref[...] + dx
        vec_ref[0:1, :] += d_w

    full = _rows(tm, (0, D_MODEL))
    half = _rows(tm, (0, D_SSM))
    return pl.pallas_call(
        body, name="in_proj_bwd", grid=(lp // tm,),
        in_specs=[full, _whole((1, D_MODEL)), full, full, half, half, half, _whole((1, D_SSM)), half,
                  _rows(tm, (0, Q_LORA)), _rows(tm, (0, KV_LORA)), _rows(tm, (0, HEAD_PAD)), _whole((D_PROJ, D_MODEL))],
        out_specs=[full, _out_whole((D_MODEL, D_PROJ)), _out_whole((8, D_MODEL))],
        out_shape=[jax.ShapeDtypeStruct((lp, D_MODEL), F32), jax.ShapeDtypeStruct((D_MODEL, D_PROJ), F32),
                   jax.ShapeDtypeStruct((8, D_MODEL), F32)],
        compiler_params=_cparams(("arbitrary",)),
    )(h, pre_w, dres, dga, du0, du1, dyp, ssm_d, dsg, dql, dkvl, dkr, w_in_pt)


def _other_chips(x, y):
    return [(1 - x, y), (x, 1 - y), (1 - x, 1 - y)]


def _gather_weights(w_bf16, meta):
    any_spec = pl.BlockSpec(memory_space=pl.ANY)
    halves = (w_bf16.shape[0] // 2, meta.shape[0] // 2)

    def body(w_ref, m_ref, wout_ref, mout_ref, send_sems, recv_sems, local_sems):
        x, y, c = lax.axis_index("x"), lax.axis_index("y"), lax.axis_index("c")
        me, sibling = 2 * x + y, (x, y, 1 - c)
        srcs, dsts = (w_ref, m_ref), (wout_ref, mout_ref)

        def half(n, cc):
            return pl.ds(pl.multiple_of(cc * halves[n], 8), halves[n])

        def copy(n, sem, src, chip, cc, to):
            return pltpu.make_async_remote_copy(src_ref=src, dst_ref=dsts[n].at[chip, half(n, cc)], send_sem=send_sems.at[sem],
                                                recv_sem=recv_sems.at[sem], device_id=to, device_id_type=MESH)

        own = [pltpu.make_async_copy(srcs[n], dsts[n].at[me], local_sems.at[n]) for n in range(2)]
        for cp in own:
            cp.start()
        chips = _other_chips(x, y)
        first = [copy(n, 2 * j + n, srcs[n].at[half(n, c)], me, c, (tx, ty, c)) for j, (tx, ty) in enumerate(chips) for n in range(2)]
        for cp in first:
            cp.start()
        passed = []
        for j, (tx, ty) in enumerate(chips):
            for n in range(2):
                landed = dsts[n].at[2 * tx + ty, half(n, c)]
                copy(n, 2 * j + n, landed, 2 * tx + ty, c, (tx, ty, c)).wait_recv()
                passed.append(copy(n, 6 + 2 * j + n, landed, 2 * tx + ty, c, sibling))
                passed[-1].start()
        for j, (tx, ty) in enumerate(chips):
            for n in range(2):
                copy(n, 6 + 2 * j + n, dsts[n].at[2 * tx + ty, half(n, 1 - c)], 2 * tx + ty, 1 - c, sibling).wait_recv()
        for cp in first + passed:
            cp.wait_send()
        for cp in own:
            cp.wait()

    return pl.pallas_call(
        body, name="gather_weights",
        in_specs=[any_spec, any_spec], out_specs=[any_spec, any_spec],
        out_shape=[jax.ShapeDtypeStruct((4,) + w_bf16.shape, w_bf16.dtype), jax.ShapeDtypeStruct((4,) + meta.shape, meta.dtype)],
        scratch_shapes=[pltpu.SemaphoreType.DMA((12,)), pltpu.SemaphoreType.DMA((12,)), pltpu.SemaphoreType.DMA((2,))],
    )(w_bf16, meta)


def _swap_sibling(g):
    any_spec = pl.BlockSpec(memory_space=pl.ANY)

    def body(g_ref, out_ref, send_sem, recv_sem):
        x, y, c = lax.axis_index("x"), lax.axis_index("y"), lax.axis_index("c")
        cp = pltpu.make_async_remote_copy(src_ref=g_ref, dst_ref=out_ref, send_sem=send_sem, recv_sem=recv_sem,
                                          device_id=(x, y, 1 - c), device_id_type=MESH)
        cp.start()
        cp.wait()

    return pl.pallas_call(
        body, name="swap_sibling", in_specs=[any_spec], out_specs=any_spec,
        out_shape=jax.ShapeDtypeStruct(g.shape, g.dtype),
        scratch_shapes=[pltpu.SemaphoreType.DMA(()), pltpu.SemaphoreType.DMA(())],
    )(g)


def _pair_sum(a, b):
    rows = a.shape[0]
    tm = _pick_tile(rows, 1024)

    def body(a_ref, b_ref, o_ref):
        o_ref[...] = a_ref[...] + b_ref[...]

    spec = pl.BlockSpec((tm, 1024), lambda i: (i, 0))
    return pl.pallas_call(body, name="pair_sum", grid=(rows // tm,), in_specs=[spec, spec], out_specs=spec,
                          out_shape=jax.ShapeDtypeStruct(a.shape, F32), compiler_params=_cparams(("parallel",)))(a, b)


def _scatter_chips(s, rs, rsm):
    any_spec = pl.BlockSpec(memory_space=pl.ANY)
    lens = (rs // 2, rsm // 2)

    def body(s_ref, out_ref, send_sems, recv_sems, local_sems):
        x, y, c = lax.axis_index("x"), lax.axis_index("y"), lax.axis_index("c")
        me, sibling = 2 * x + y, (x, y, 1 - c)

        def src_rows(n, target, cc):
            start = (target * rs if n == 0 else 4 * rs) + cc * lens[n]
            return s_ref.at[pl.ds(pl.multiple_of(start, 8), lens[n])]

        def dst_rows(n, cc):
            return pl.ds(pl.multiple_of((0 if n == 0 else rs) + cc * lens[n], 8), lens[n])

        def copy(n, sem, src, chip, cc, to):
            return pltpu.make_async_remote_copy(src_ref=src, dst_ref=out_ref.at[chip, dst_rows(n, cc)], send_sem=send_sems.at[sem],
                                                recv_sem=recv_sems.at[sem], device_id=to, device_id_type=MESH)

        own = [pltpu.make_async_copy(s_ref.at[pl.ds(pl.multiple_of(me * rs, 8), rs)], out_ref.at[me, pl.ds(0, rs)], local_sems.at[0]),
               pltpu.make_async_copy(s_ref.at[pl.ds(4 * rs, rsm)], out_ref.at[me, pl.ds(rs, rsm)], local_sems.at[1])]
        for cp in own:
            cp.start()
        chips = _other_chips(x, y)
        first = [copy(n, 2 * j + n, src_rows(n, 2 * tx + ty, c), me, c, (tx, ty, c))
                 for j, (tx, ty) in enumerate(chips) for n in range(2)]
        for cp in first:
            cp.start()
        passed = []
        for j, (tx, ty) in enumerate(chips):
            for n in range(2):
                landed = out_ref.at[2 * tx + ty, dst_rows(n, c)]
                copy(n, 2 * j + n, landed, 2 * tx + ty, c, (tx, ty, c)).wait_recv()
                passed.append(copy(n, 6 + 2 * j + n, landed, 2 * tx + ty, c, sibling))
                passed[-1].start()
        for j, (tx, ty) in enumerate(chips):
            for n in range(2):
                copy(n, 6 + 2 * j + n, out_ref.at[2 * tx + ty, dst_rows(n, 1 - c)], 2 * tx + ty, 1 - c, sibling).wait_recv()
        for cp in first + passed:
            cp.wait_send()
        for cp in own:
            cp.wait()

    return pl.pallas_call(
        body, name="scatter_chips", in_specs=[any_spec], out_specs=any_spec,
        out_shape=jax.ShapeDtypeStruct((4, rs + rsm, 1024), F32),
        scratch_shapes=[pltpu.SemaphoreType.DMA((12,)), pltpu.SemaphoreType.DMA((12,)), pltpu.SemaphoreType.DMA((2,))],
    )(s)


def _adamw(parts, w, m, v):
    rows = w.shape[0]
    tm = _pick_tile(rows, 256)
    c1 = 1.0 / (1.0 - ADAM_B1 ** ADAM_STEP)
    c2 = 1.0 / (1.0 - ADAM_B2 ** ADAM_STEP)

    def body(p_ref, w_ref, m_ref, v_ref, g_ref, d_ref, nm_ref, nv_ref):
        g = ((p_ref[0] + p_ref[1]) + p_ref[2]) + p_ref[3]
        nm = ADAM_B1 * m_ref[...] + (1.0 - ADAM_B1) * g
        nv = ADAM_B2 * v_ref[...] + (1.0 - ADAM_B2) * (g * g)
        g_ref[...] = g
        nm_ref[...] = nm
        nv_ref[...] = nv
        d_ref[...] = -ADAM_LR * ((nm * c1) / (jnp.sqrt(nv * c2) + ADAM_EPS) + ADAM_WD * w_ref[...])

    spec = pl.BlockSpec((tm, 1024), lambda i: (i, 0))
    out = jax.ShapeDtypeStruct(w.shape, F32)
    return pl.pallas_call(
        body, name="adamw", grid=(rows // tm,),
        in_specs=[pl.BlockSpec((4, tm, 1024), lambda i: (0, i, 0)), spec, spec, spec],
        out_specs=[spec] * 4, out_shape=[out] * 4, compiler_params=_cparams(("parallel",)),
    )(parts, w, m, v)


def _expand_heads(a, axis, per_head):
    a = jnp.moveaxis(a, axis, -1)
    lead = a.shape[:-1]
    a = a.reshape(lead + (HEADS, per_head))
    a = jnp.pad(a, [(0, 0)] * len(lead) + [(0, 0), (0, HEAD_PAD - per_head)])
    return jnp.moveaxis(a.reshape(lead + (D_EXP,)), -1, axis)


def _compact_heads(a, axis, start, size):
    a = jnp.moveaxis(a, axis, -1)
    lead = a.shape[:-1]
    a = a.reshape(lead + (HEADS, HEAD_PAD))[..., start:start + size]
    return jnp.moveaxis(a.reshape(lead + (HEADS * size,)), -1, axis)


def _block_diag(w):
    g, a, b = w.shape
    per = g // SSM_BLOCKS
    eye = jnp.eye(per, dtype=w.dtype)
    return jnp.einsum("jgab,gk->jgakb", w.reshape(SSM_BLOCKS, per, a, b), eye).reshape(SSM_BLOCKS, per * a, per * b)


def _block_diag_extract(dense, a, b):
    per = N_GROUPS // SSM_BLOCKS
    d5 = dense.reshape(SSM_BLOCKS, per, a, per, b)
    return jnp.einsum("jgakb,gk->jgab", d5, jnp.eye(per, dtype=dense.dtype)).reshape(N_GROUPS, a, b)


def _discretise(a_re, a_im, log_dt, b_re, b_im):
    dt = jnp.exp(log_dt)[:, None]
    mag = jnp.exp(a_re * dt)
    abar_re = mag * jnp.cos(a_im * dt)
    abar_im = mag * jnp.sin(a_im * dt)
    num_re = abar_re - 1.0
    num_im = abar_im
    den = a_re * a_re + a_im * a_im
    coef_re = (num_re * a_re + num_im * a_im) / den
    coef_im = (num_im * a_re - num_re * a_im) / den
    bbar_re = coef_re[..., None] * b_re - coef_im[..., None] * b_im
    bbar_im = coef_re[..., None] * b_im + coef_im[..., None] * b_re
    return abar_re, abar_im, bbar_re, bbar_im


def _scan_coef(ar, ai, reverse, seg):
    ar, ai = ar.reshape(1, N_STATE), ai.reshape(1, N_STATE)
    cmul = lambda x, y: (x[0] * y[0] - x[1] * y[1], x[0] * y[1] + x[1] * y[0])
    p, sq, n = None, (ar, ai), seg
    while n:
        if n & 1:
            p = sq if p is None else cmul(p, sq)
        sq, n = cmul(sq, sq), n >> 1
    pows = [p]
    for _ in range(7):
        pows.append(cmul(pows[-1], p))
    row = jnp.arange(8)[:, None]
    out = []
    for k in (1, 2, 4):
        keep = (row < 8 - k) if reverse else (row >= k)
        out += [jnp.where(keep, pows[k - 1][0], 0.0), jnp.where(keep, pows[k - 1][1], 0.0)]
    order = list(range(7, -1, -1)) if reverse else list(range(8))
    out += [jnp.concatenate([pows[k][0] for k in order], axis=0), jnp.concatenate([pows[k][1] for k in order], axis=0)]
    out += [jnp.broadcast_to(ar, (8, N_STATE)), jnp.broadcast_to(ai, (8, N_STATE))]
    return jnp.stack(out).astype(F32)


def _flat_rows(a, rows):
    flat = a.reshape(-1)
    return jnp.pad(flat, (0, rows * 1024 - flat.shape[0])).reshape(rows, 1024)


def _pack(named, order):
    rows = [-(-math.prod(named[n].shape) // 1024) for n in order]
    total = -(-sum(rows) // 32) * 32
    parts = [_flat_rows(named[n], r) for n, r in zip(order, rows)]
    if total > sum(rows):
        parts.append(jnp.zeros((total - sum(rows), 1024), parts[0].dtype))
    return jnp.concatenate(parts, axis=0)


def _unpack(packed, shapes, order):
    out, at = {}, 0
    for n in order:
        size = math.prod(shapes[n])
        rows = -(-size // 1024)
        out[n] = packed[at:at + rows].reshape(-1)[:size].reshape(shapes[n])
        at += rows
    return out


def _shard_cols(a, k):
    w = a.shape[-1] // 4
    return a[..., k * w:(k + 1) * w]


def kernel(x, meta_tokens, pre_norm_w, post_norm_w, w_in, q_norm_w, w_q_up, kv_norm_w, w_kv_up, attn_out_norm_w, ssm_a_re, ssm_a_im, ssm_log_dt, ssm_b_re, ssm_b_im, ssm_c_re, ssm_c_im, ssm_d, w_glu, b_glu, ssm_out_norm_w, w_out, loss_target, m_meta_tokens, m_pre_norm_w, m_post_norm_w, m_w_in, m_q_norm_w, m_w_q_up, m_kv_norm_w, m_w_kv_up, m_attn_out_norm_w, m_ssm_a_re, m_ssm_a_im, m_ssm_log_dt, m_ssm_b_re, m_ssm_b_im, m_ssm_c_re, m_ssm_c_im, m_ssm_d, m_w_glu, m_b_glu, m_ssm_out_norm_w, m_w_out, v_meta_tokens, v_pre_norm_w, v_post_norm_w, v_w_in, v_q_norm_w, v_w_q_up, v_kv_norm_w, v_w_kv_up, v_attn_out_norm_w, v_ssm_a_re, v_ssm_a_im, v_ssm_log_dt, v_ssm_b_re, v_ssm_b_im, v_ssm_c_re, v_ssm_c_im, v_ssm_d, v_w_glu, v_b_glu, v_ssm_out_norm_w, v_w_out):
    local = dict(meta_tokens=meta_tokens, pre_norm_w=pre_norm_w, post_norm_w=post_norm_w, w_in=w_in, q_norm_w=q_norm_w,
                 w_q_up=w_q_up, kv_norm_w=kv_norm_w, w_kv_up=w_kv_up, attn_out_norm_w=attn_out_norm_w, ssm_a_re=ssm_a_re,
                 ssm_a_im=ssm_a_im, ssm_log_dt=ssm_log_dt, ssm_b_re=ssm_b_re, ssm_b_im=ssm_b_im, ssm_c_re=ssm_c_re,
                 ssm_c_im=ssm_c_im, ssm_d=ssm_d, w_glu=w_glu, b_glu=b_glu, ssm_out_norm_w=ssm_out_norm_w, w_out=w_out)
    mom_m = dict(meta_tokens=m_meta_tokens, pre_norm_w=m_pre_norm_w, post_norm_w=m_post_norm_w, w_in=m_w_in,
                 q_norm_w=m_q_norm_w, w_q_up=m_w_q_up, kv_norm_w=m_kv_norm_w, w_kv_up=m_w_kv_up,
                 attn_out_norm_w=m_attn_out_norm_w, ssm_a_re=m_ssm_a_re, ssm_a_im=m_ssm_a_im, ssm_log_dt=m_ssm_log_dt,
                 ssm_b_re=m_ssm_b_re, ssm_b_im=m_ssm_b_im, ssm_c_re=m_ssm_c_re, ssm_c_im=m_ssm_c_im, ssm_d=m_ssm_d,
                 w_glu=m_w_glu, b_glu=m_b_glu, ssm_out_norm_w=m_ssm_out_norm_w, w_out=m_w_out)
    mom_v = dict(meta_tokens=v_meta_tokens, pre_norm_w=v_pre_norm_w, post_norm_w=v_post_norm_w, w_in=v_w_in,
                 q_norm_w=v_q_norm_w, w_q_up=v_w_q_up, kv_norm_w=v_kv_norm_w, w_kv_up=v_w_kv_up,
                 attn_out_norm_w=v_attn_out_norm_w, ssm_a_re=v_ssm_a_re, ssm_a_im=v_ssm_a_im, ssm_log_dt=v_ssm_log_dt,
                 ssm_b_re=v_ssm_b_re, ssm_b_im=v_ssm_b_im, ssm_c_re=v_ssm_c_re, ssm_c_im=v_ssm_c_im, ssm_d=v_ssm_d,
                 w_glu=v_w_glu, b_glu=v_b_glu, ssm_out_norm_w=v_ssm_out_norm_w, w_out=v_w_out)
    shapes = {n: local[n].shape for n in WEIGHTS}
    mat = ("w_in", "w_q_up", "w_kv_up", "w_glu", "w_out")

    seq = x.shape[1]
    l_real = N_META + seq
    lp = -(-l_real // 1280) * 1280 if l_real > 1280 else -(-l_real // QBLK) * QBLK
    assert seq % 128 == 0 and lp % QBLK == 0

    w_shard = _pack({n: local[n].astype(BF16) for n in mat}, mat)
    w_shard = jnp.pad(w_shard, ((0, -w_shard.shape[0] % 16), (0, 0)))
    w_all, meta_all = _gather_weights(w_shard, meta_tokens)
    mat_shapes = {n: shapes[n] for n in mat}
    per_chip = [_unpack(w_all[k], mat_shapes, mat) for k in range(4)]
    w_in_f = jnp.concatenate([p["w_in"][0] for p in per_chip], axis=1)
    w_q_f = jnp.concatenate([p["w_q_up"][0] for p in per_chip], axis=1)
    w_kv_f = jnp.concatenate([p["w_kv_up"][0] for p in per_chip], axis=1)
    w_glu_f = jnp.concatenate([p["w_glu"][0] for p in per_chip], axis=1)
    w_out_f = jnp.concatenate([p["w_out"][0] for p in per_chip], axis=0)
    meta_f = jnp.concatenate([meta_all[k] for k in range(4)], axis=1)

    o_q, o_kv, o_kr, o_ga, o_u, o_gs = 0, 256, 384, 416, 928, 1440
    krope_cols = jnp.pad(w_in_f[:, o_kr:o_ga], ((0, 0), (QK_NOPE, HEAD_PAD - QK_NOPE - QK_ROPE)))
    w_in_p = jnp.concatenate([_expand_heads(w_in_f[:, o_ga:o_u], 1, V_HEAD), w_in_f[:, o_u:o_gs], w_in_f[:, o_gs:],
                              w_in_f[:, o_q:o_kv], w_in_f[:, o_kv:o_kr], krope_cols], axis=1)
    wq_p = _expand_heads(w_q_f, 1, QK_NOPE + QK_ROPE)
    kv3 = w_kv_f.reshape(KV_LORA, HEADS, QK_NOPE + V_HEAD)
    wk_p = _expand_heads(kv3[:, :, :QK_NOPE].reshape(KV_LORA, HEADS * QK_NOPE), 1, QK_NOPE)
    wv_c = kv3[:, :, QK_NOPE:].reshape(KV_LORA, HEADS * V_HEAD)
    wv_p = _expand_heads(wv_c, 1, V_HEAD)
    wv_t = jnp.pad(wv_c.T.reshape(HEADS, V_HEAD, KV_LORA), ((0, 0), (0, VT_ROWS - V_HEAD), (0, 0))).reshape(HEADS * VT_ROWS, KV_LORA)
    w_out_a = _expand_heads(w_out_f[:D_ATTN], 0, V_HEAD)
    w_out_s = w_out_f[D_ATTN:]
    attn_norm_e = _expand_heads(attn_out_norm_w, 1, V_HEAD)

    pos = jnp.arange(lp, dtype=jnp.int32)
    half = QK_ROPE // 2
    inv = ROPE_THETA ** (-jnp.arange(half, dtype=F32) / half)
    ang = pos.astype(F32)[:, None] * inv[None, :]
    cos16, sin16 = jnp.cos(ang), jnp.sin(ang)
    ones, zeros = jnp.ones((lp, QK_NOPE), F32), jnp.zeros((lp, QK_NOPE), F32)
    tail1, tail0 = jnp.ones((lp, HEAD_PAD - MASK_LANE), F32), jnp.zeros((lp, HEAD_PAD - MASK_LANE), F32)
    z16 = jnp.zeros((lp, half), F32)
    cos = jnp.concatenate([ones, cos16, cos16, tail1], axis=1)
    sina = jnp.concatenate([zeros, z16, sin16, tail0], axis=1)
    sinb = jnp.concatenate([zeros, -sin16, z16, tail0], axis=1)

    disc_in = (ssm_a_re[0], ssm_a_im[0], ssm_log_dt[0], ssm_b_re[0], ssm_b_im[0])
    disc = lambda a_re, a_im, ldt, b_re, b_im: jax.vmap(_discretise)(a_re, a_im, ldt, b_re, b_im)
    (abar_re, abar_im, bbar_re, bbar_im), disc_vjp = jax.vjp(disc, *disc_in)
    ssm = []
    for d in range(2):
        rev = d == 1
        b_re_bd = _block_diag(jnp.swapaxes(bbar_re[d], 1, 2)).astype(BF16)
        b_im_bd = _block_diag(jnp.swapaxes(bbar_im[d], 1, 2)).astype(BF16)
        c_re_bd = _block_diag(jnp.swapaxes(ssm_c_re[0, d], 1, 2)).astype(BF16)
        c_im_bd = _block_diag(jnp.swapaxes(-ssm_c_im[0, d], 1, 2)).astype(BF16)
        ssm.append(dict(rev=rev, coef=_scan_coef(abar_re[d], abar_im[d], rev, _ssm_tile(lp) // 8),
                        coef_adj=_scan_coef(abar_re[d], -abar_im[d], not rev, _ssm_tile(lp) // 8),
                        b_re=b_re_bd, b_im=b_im_bd, c_re=c_re_bd, c_im=c_im_bd))

    t_ssm = _ssm_tile(lp)
    src = (jnp.arange(t_ssm) % 8) * (t_ssm // 8) + jnp.arange(t_ssm) // 8
    perm = (src[:, None] == jnp.arange(t_ssm)[None, :]).astype(BF16)

    h = jnp.concatenate([meta_f, x[0], jnp.zeros((lp - l_real, D_MODEL), F32)], axis=0)
    proj = _in_proj_fwd(h, pre_norm_w, w_in_p)
    q, k, v, vt, q_t, k_t = _attn_prep_fwd(proj, q_norm_w, kv_norm_w, wq_p, wk_p, wv_p, wv_t, cos, sina, sinb, l_real)
    o_exp, lse, mblk, p_all = _flash_fwd(q, k, vt)
    ys, states = [], []
    for s in ssm:
        y_d, st_d = _ssm_fwd(proj, perm, s["coef"], s["b_re"], s["b_im"], s["c_re"], s["c_im"], s["rev"])
        ys.append(y_d)
        states.append(st_d)

    (d_o, do_t, delta, dga, dyp, dsg, dres, dwoa, dwos, dwglu, vec_mid) = _mid(
        h, loss_target[0], o_exp, proj, ys[0], ys[1], ssm_d, w_glu_f, w_glu_f.T, b_glu, ssm_out_norm_w, attn_norm_e, w_out_a, w_out_s,
        w_out_a.T, w_out_s.T, post_norm_w, l_real)
    dus, dssm = [], []
    tr = lambda a: jnp.swapaxes(a, 1, 2)
    for s, st_d in zip(ssm, states):
        du_d, dbre, dbim, dcre, dcim, da = _ssm_bwd(proj, dyp, st_d, perm, s["coef"], s["coef_adj"], s["b_re"], s["b_im"],
                                                    tr(s["b_re"]), tr(s["b_im"]), tr(s["c_re"]), tr(s["c_im"]), s["rev"])
        dus.append(du_d)
        dssm.append((dbre, dbim, dcre, dcim, da))
    dq, dk_t, dv_t = _flash_bwd(q, v, d_o, q_t, k_t, do_t, lse, delta.T.reshape(HEADS, 1, lp), mblk, p_all)
    dql, dkvl, dkr, dwq_p, dwk_p, dwv_p, vec_prep = _attn_prep_bwd(
        dq, dk_t, dv_t, proj, q_norm_w, kv_norm_w, wq_p.T, wk_p.T, wv_p.T, cos, sina, sinb)
    dh, dwin_p, vec_in = _in_proj_bwd(h, pre_norm_w, dres, dga, dus[0], dus[1], dyp, ssm_d, dsg, dql, dkvl, dkr, w_in_p.T)

    grads = {}
    grads["w_in"] = jnp.concatenate([
        dwin_p[:, P_QLAT[0]:P_QLAT[0] + 256], dwin_p[:, P_KVLAT[0]:P_KVLAT[0] + 128],
        dwin_p[:, P_KROPE[0] + QK_NOPE:P_KROPE[0] + QK_NOPE + QK_ROPE], _compact_heads(dwin_p[:, 0:D_EXP], 1, 0, V_HEAD),
        dwin_p[:, P_U[0]:P_U[0] + 512], dwin_p[:, P_GATE_S[0]:P_GATE_S[0] + 512]], axis=1)[None]
    grads["w_q_up"] = _compact_heads(dwq_p, 1, 0, QK_NOPE + QK_ROPE)[None]
    dwk3 = _compact_heads(dwk_p, 1, 0, QK_NOPE).reshape(KV_LORA, HEADS, QK_NOPE)
    dwv3 = _compact_heads(dwv_p, 1, 0, V_HEAD).reshape(KV_LORA, HEADS, V_HEAD)
    grads["w_kv_up"] = jnp.concatenate([dwk3, dwv3], axis=2).reshape(1, KV_LORA, HEADS * (QK_NOPE + V_HEAD))
    grads["w_glu"] = dwglu[None]
    grads["w_out"] = jnp.concatenate([_compact_heads(dwoa, 0, 0, V_HEAD), dwos], axis=0)[None]
    grads["meta_tokens"] = dh[:N_META]
    grads["pre_norm_w"] = vec_in[0:1]
    grads["post_norm_w"] = vec_mid[0:1]
    grads["q_norm_w"] = vec_prep[0:1]
    grads["kv_norm_w"] = vec_prep[1:2, :KV_LORA]
    grads["attn_out_norm_w"] = _compact_heads(vec_mid[1:2], 1, 0, V_HEAD)
    grads["ssm_out_norm_w"] = vec_mid[2:3, :D_SSM]
    grads["ssm_d"] = vec_mid[3:4, :D_SSM]
    grads["b_glu"] = vec_mid[4:5]
    d_abar_re = jnp.stack([dssm[d][4][0].sum(axis=0).reshape(N_GROUPS, SSM_STATE) for d in range(2)])
    d_abar_im = jnp.stack([dssm[d][4][1].sum(axis=0).reshape(N_GROUPS, SSM_STATE) for d in range(2)])
    d_bbar_re = jnp.stack([jnp.swapaxes(_block_diag_extract(dssm[d][0], SSM_GROUP, SSM_STATE), 1, 2) for d in range(2)])
    d_bbar_im = jnp.stack([jnp.swapaxes(_block_diag_extract(dssm[d][1], SSM_GROUP, SSM_STATE), 1, 2) for d in range(2)])
    da_re, da_im, dlog_dt, db_re, db_im = disc_vjp((d_abar_re, d_abar_im, d_bbar_re, d_bbar_im))
    grads["ssm_a_re"], grads["ssm_a_im"], grads["ssm_log_dt"] = da_re[None], da_im[None], dlog_dt[None]
    grads["ssm_b_re"], grads["ssm_b_im"] = db_re[None], db_im[None]
    grads["ssm_c_re"] = jnp.stack([_block_diag_extract(dssm[d][2], SSM_GROUP, SSM_STATE) for d in range(2)])[None]
    grads["ssm_c_im"] = jnp.stack([_block_diag_extract(dssm[d][3], SSM_GROUP, SSM_STATE) for d in range(2)])[None]

    def shard_of(n, a, kk):
        return a[:, kk * 256:(kk + 1) * 256] if n == "w_out" else _shard_cols(a, kk)

    slices = [_pack({n: shard_of(n, grads[n], kk) for n in BIG}, BIG) for kk in range(4)]
    grads["loss"] = vec_mid[5:6, 0:1]
    small = _pack({n: grads[n] for n in SMALL + ("loss",)}, SMALL + ("loss",))
    loss_row = slices[0].shape[0] + sum(-(-math.prod(shapes[n]) // 1024) for n in SMALL)
    rs, rsm = slices[0].shape[0], small.shape[0]
    g_pack = jnp.concatenate(slices + [small], axis=0)
    g_pair = _pair_sum(g_pack, _swap_sibling(g_pack))
    parts = _scatter_chips(g_pair, rs, rsm)

    order = BIG + SMALL
    big_shapes = {n: shapes[n] for n in BIG}
    small_shapes = {n: shapes[n] for n in SMALL}

    def pack_state(named):
        return jnp.concatenate([_pack({n: named[n] for n in BIG}, BIG), _pack({n: named[n] for n in SMALL}, SMALL)], axis=0)

    g_out, d_out, m_out, v_out = _adamw(parts, pack_state(local), pack_state(mom_m), pack_state(mom_v))

    def unpack_state(p):
        out = _unpack(p[:rs], big_shapes, BIG)
        out.update(_unpack(p[rs:], small_shapes, SMALL))
        return out

    g_fin, d_fin, m_fin, v_fin = unpack_state(g_out), unpack_state(d_out), unpack_state(m_out), unpack_state(v_out)
    loss = g_out[loss_row, 0]
    grad_x = dh[N_META:l_real][None]
    return (loss, grad_x, *[g_fin[n] for n in WEIGHTS], *[d_fin[n] for n in WEIGHTS], *[m_fin[n] for n in WEIGHTS],
            *[v_fin[n] for n in WEIGHTS])
```

```python
import functools
import math

import jax
import jax.numpy as jnp
from jax import lax
from jax.experimental import pallas as pl
from jax.experimental.pallas import tpu as pltpu

F32 = jnp.float32
BF16 = jnp.bfloat16
MESH = pl.DeviceIdType.MESH

D_MODEL = 1024
N_META = 16
EPS = 1e-6
HEADS = 8
QK_NOPE = 64
QK_ROPE = 32
V_HEAD = 64
VT_ROWS = 80
Q_LORA = 256
KV_LORA = 128
D_ATTN = 512
D_SSM = 512
SSM_GROUP = 16
N_GROUPS = 32
SSM_STATE = 64
N_STATE = N_GROUPS * SSM_STATE
ROPE_THETA = 10000.0
HEAD_PAD = 128
D_EXP = HEADS * HEAD_PAD
D_QK = QK_NOPE + QK_ROPE
MASK_LANE = D_QK
NEG_BIG = -1e30
SCALE = 1.0 / math.sqrt(QK_NOPE + QK_ROPE)
LOG2E = math.log2(math.e)
SCALE2 = SCALE * LOG2E
QBLK = 256
QUAD = 4
P_AHEAD = 2
SCAN_COLS = 1024
SCAN_UNROLL = 2
SSM_BLOCKS = 4
BLK_CH = D_SSM // SSM_BLOCKS
BLK_ST = N_STATE // SSM_BLOCKS

P_GATE_A = (0, 1024)
P_U = (1024, 512)
P_GATE_S = (1536, 512)
P_QLAT = (2048, 256)
P_KVLAT = (2304, 128)
P_KROPE = (2432, 128)
D_PROJ = 2560

ADAM_LR = 0.001
ADAM_B1 = 0.9
ADAM_B2 = 0.999
ADAM_EPS = 1e-08
ADAM_WD = 0.01
ADAM_STEP = 10

VMEM_LIMIT = 60 * 1024 * 1024

BIG = ("w_in", "w_q_up", "w_kv_up", "w_glu", "w_out", "meta_tokens")
SMALL = ("pre_norm_w", "post_norm_w", "q_norm_w", "kv_norm_w", "attn_out_norm_w", "ssm_a_re", "ssm_a_im",
         "ssm_log_dt", "ssm_b_re", "ssm_b_im", "ssm_c_re", "ssm_c_im", "ssm_d", "b_glu", "ssm_out_norm_w")
WEIGHTS = ("meta_tokens", "pre_norm_w", "post_norm_w", "w_in", "q_norm_w", "w_q_up", "kv_norm_w", "w_kv_up",
           "attn_out_norm_w", "ssm_a_re", "ssm_a_im", "ssm_log_dt", "ssm_b_re", "ssm_b_im", "ssm_c_re", "ssm_c_im",
           "ssm_d", "w_glu", "b_glu", "ssm_out_norm_w", "w_out")


def _cparams(sem=None):
    return pltpu.CompilerParams(dimension_semantics=sem, vmem_limit_bytes=VMEM_LIMIT)


def _dot(a, b):
    return jnp.dot(a, b, preferred_element_type=F32)


def _dot_nt(a, b):
    return lax.dot_general(a, b, (((1,), (1,)), ((), ())), preferred_element_type=F32)


def _dot_tn(a, b):
    return lax.dot_general(a, b, (((0,), (0,)), ((), ())), preferred_element_type=F32)


def _sigmoid(x):
    return 1.0 / (1.0 + jnp.exp(-x))


def _rms_fwd(x, w, n):
    r = lax.rsqrt(jnp.sum(x * x, axis=-1, keepdims=True) * (1.0 / n) + EPS)
    return x * r * w, r


def _rms_bwd(x, r, w, dy, n):
    dyw = dy * w
    dx = r * dyw - x * (r * r * r) * (jnp.sum(dyw * x, axis=-1, keepdims=True) * (1.0 / n))
    dw = jnp.sum(dy * (x * r), axis=0, keepdims=True)
    return dx, dw


def _rope_apply(x, cos, sina, sinb):
    return x * cos + pltpu.roll(x, 16, 1) * sina + pltpu.roll(x, HEAD_PAD - 16, 1) * sinb


def _rope_transpose(g, cos, sina, sinb):
    return g * cos + pltpu.roll(g * sina, HEAD_PAD - 16, 1) + pltpu.roll(g * sinb, 16, 1)


def _row_tile(lp):
    return 640 if lp % 640 == 0 else 128


def _ssm_tile(lp):
    return 320 if lp % 320 == 0 else 128


def _rows(tm, off_width):
    off, width = off_width
    return pl.BlockSpec((tm, width), lambda i: (i, off // width))


def _whole(shape, single=True):
    nd = len(shape)
    if single:
        return pl.BlockSpec(shape, lambda *_: (0,) * nd, pipeline_mode=pl.Buffered(1))
    return pl.BlockSpec(shape, lambda *_: (0,) * nd)


def _out_whole(shape):
    return _whole(shape, single=False)


def _pick_tile(rows, cap):
    best = 8
    for t in range(8, cap + 1, 8):
        if rows % t == 0:
            best = t
    return best


def _in_proj_fwd(h, pre_w, w_in_p):
    lp = h.shape[0]
    tm = _row_tile(lp)

    def body(h_ref, w_ref, win_ref, proj_ref):
        xn, _ = _rms_fwd(h_ref[...], w_ref[...], D_MODEL)
        proj_ref[...] = _dot(xn.astype(BF16), win_ref[...])

    return pl.pallas_call(
        body, name="in_proj_fwd", grid=(lp // tm,),
        in_specs=[_rows(tm, (0, D_MODEL)), _whole((1, D_MODEL)), _whole((D_MODEL, D_PROJ))],
        out_specs=_rows(tm, (0, D_PROJ)),
        out_shape=jax.ShapeDtypeStruct((lp, D_PROJ), F32),
        compiler_params=_cparams(("parallel",)),
    )(h, pre_w, w_in_p)


def _attn_prep_fwd(proj, q_norm_w, kv_norm_w, wq_p, wk_p, wv_p, wv_t, cos, sina, sinb, l_real):
    lp = proj.shape[0]
    tm = _row_tile(lp)

    def body(ql_ref, kvl_ref, kr_ref, qw_ref, kw_ref, wq_ref, wk_ref, wv_ref, wvt_ref, cos_ref, sa_ref, sb_ref,
             q_ref, k_ref, v_ref, vt_ref, qt_ref, kt_ref):
        cos_t, sa_t, sb_t = cos_ref[...], sa_ref[...], sb_ref[...]
        qn, _ = _rms_fwd(ql_ref[...], qw_ref[...], Q_LORA)
        kvn, _ = _rms_fwd(kvl_ref[...], kw_ref[...], KV_LORA)
        kvn_b = kvn.astype(BF16)
        qp = _dot(qn.astype(BF16), wq_ref[...])
        kp = _dot(kvn_b, wk_ref[...])
        v_ref[...] = _dot(kvn_b, wv_ref[...]).astype(BF16)
        ones_row = lax.broadcasted_iota(jnp.int32, (HEADS * VT_ROWS, 1), 0) % VT_ROWS == V_HEAD
        vt_ref[...] = jnp.where(ones_row, 1.0, _dot_nt(wvt_ref[...], kvn_b)).astype(BF16)
        lane = lax.broadcasted_iota(jnp.int32, (tm, HEAD_PAD), 1)
        row = lax.broadcasted_iota(jnp.int32, (tm, HEAD_PAD), 0) + pl.program_id(0) * tm
        q_one = jnp.where(lane == MASK_LANE, 1.0, 0.0)
        k_add = _rope_apply(kr_ref[...], cos_t, sa_t, sb_t) + jnp.where((lane == MASK_LANE) & (row >= l_real), NEG_BIG, 0.0)
        for hd in range(HEADS):
            blk = slice(hd * HEAD_PAD, (hd + 1) * HEAD_PAD)
            q_h = _rope_apply(qp[:, blk], cos_t, sa_t, sb_t) * SCALE2 + q_one
            k_h = kp[:, blk] + k_add
            q_ref[:, blk] = q_h.astype(BF16)
            k_ref[:, blk] = k_h.astype(BF16)
            qt_ref[hd * D_QK:(hd + 1) * D_QK, :] = q_h.T[:D_QK].astype(BF16)
            kt_ref[hd * D_QK:(hd + 1) * D_QK, :] = k_h.T[:D_QK].astype(BF16)

    tab = _rows(tm, (0, HEAD_PAD))
    out = jax.ShapeDtypeStruct((lp, D_EXP), BF16)
    out_t = jax.ShapeDtypeStruct((HEADS * D_QK, lp), BF16)
    cols_t = pl.BlockSpec((HEADS * D_QK, tm), lambda i: (0, i))
    return pl.pallas_call(
        body, name="attn_prep_fwd", grid=(lp // tm,),
        in_specs=[_rows(tm, P_QLAT), _rows(tm, P_KVLAT), _rows(tm, P_KROPE), _whole((1, Q_LORA)), _whole((1, KV_LORA)),
                  _whole((Q_LORA, D_EXP)), _whole((KV_LORA, D_EXP)), _whole((KV_LORA, D_EXP)),
                  _whole((HEADS * VT_ROWS, KV_LORA)), tab, tab, tab],
        out_specs=[_rows(tm, (0, D_EXP))] * 3 + [pl.BlockSpec((HEADS * VT_ROWS, tm), lambda i: (0, i)), cols_t, cols_t],
        out_shape=[out, out, out, jax.ShapeDtypeStruct((HEADS * VT_ROWS, lp), BF16), out_t, out_t],
        compiler_params=_cparams(("parallel",)),
    )(proj, proj, proj, q_norm_w, kv_norm_w, wq_p, wk_p, wv_p, wv_t, cos, sina, sinb)


def _flash_fwd(q, k, vt):
    lp = q.shape[0]
    tq = 1280 if lp % 1280 == 0 else 256
    tk = QBLK
    nk = lp // tk

    def body(q_ref, k_ref, vt_ref, o_ref, lse_ref, mblk_ref, p_hbm, acc, m_s, s_a, s_b, p_buf, p_sem):
        hd, qi = pl.program_id(0), pl.program_id(1)

        def p_copy(j, n, slot):
            return [pltpu.make_async_copy(p_buf.at[slot, b], p_hbm.at[hd, qi, j + b], p_sem.at[slot]) for b in range(n)]

        acc[...] = jnp.zeros_like(acc)
        m_s[...] = jnp.full(m_s.shape, NEG_BIG, F32)
        blocks = [slice(c * QBLK, (c + 1) * QBLK) for c in range(tq // QBLK)]

        def scores(j, buf):
            kt = k_ref[pl.ds(pl.multiple_of(j * tk, tk), tk), :]
            for cols in blocks:
                buf[:, cols] = _dot_nt(kt, q_ref[cols, :])

        def consume(j, buf, slot, b):
            vt_t = vt_ref[:, pl.ds(pl.multiple_of(j * tk, tk), tk)]
            m_old, acc_old = m_s[...], acc[...]
            s = [buf[:, cols] for cols in blocks]
            m_new = [jnp.maximum(m_old[:, cols], jnp.max(s_c, axis=0, keepdims=True)) for cols, s_c in zip(blocks, s)]
            p = [jnp.exp2(s_c - m_c).astype(BF16) for s_c, m_c in zip(s, m_new)]
            pv = [_dot(vt_t, p_c) for p_c in p]
            m_new = jnp.concatenate(m_new, axis=1)
            alpha = jnp.exp2(m_old - m_new)
            acc[...] = alpha * acc_old + jnp.concatenate(pv, axis=1)
            m_s[...] = m_new
            mblk_ref[j] = m_new
            p_buf[slot, b] = jnp.concatenate(p, axis=1)

        quads = (nk - 1) // QUAD
        scores(0, s_a)

        def quad(t, _):
            j, slot = QUAD * t, t % 2

            @pl.when(t >= 2)
            def _():
                for cp in p_copy(j - 2 * QUAD, QUAD, slot):
                    cp.wait()

            for b in range(QUAD):
                scores(j + b + 1, s_a if b % 2 else s_b)
                consume(j + b, s_b if b % 2 else s_a, slot, b)
            for b, cp in enumerate(p_copy(j, QUAD, slot)):
                cp.start(priority=b % 2)
            return 0

        lax.fori_loop(0, quads, quad, 0)
        for back in (2, 1):
            if quads >= back:
                for cp in p_copy(QUAD * (quads - back), QUAD, (quads - back) % 2):
                    cp.wait()
        rest = nk - QUAD * quads
        for b in range(rest):
            if b + 1 < rest:
                scores(QUAD * quads + b + 1, s_a if b % 2 else s_b)
            consume(QUAD * quads + b, s_b if b % 2 else s_a, 0, b)
        for cp in p_copy(QUAD * quads, rest, 0):
            cp.start()
        for cp in p_copy(QUAD * quads, rest, 0):
            cp.wait()
        l = acc[V_HEAD:V_HEAD + 1, :]
        o_t = acc[0:V_HEAD, :] / l
        o_ref[...] = jnp.concatenate([o_t, jnp.zeros_like(o_t)], axis=0).T
        lse_ref[...] = m_s[...] + jnp.log2(l)

    return pl.pallas_call(
        body, name="flash_fwd", grid=(HEADS, lp // tq),
        in_specs=[pl.BlockSpec((tq, HEAD_PAD), lambda hd, i: (i, hd)),
                  pl.BlockSpec((lp, HEAD_PAD), lambda hd, i: (0, hd)),
                  pl.BlockSpec((VT_ROWS, lp), lambda hd, i: (hd, 0))],
        out_specs=[pl.BlockSpec((tq, HEAD_PAD), lambda hd, i: (i, hd)),
                   pl.BlockSpec((None, 1, tq), lambda hd, i: (hd, 0, i)),
                   pl.BlockSpec((None, None, nk, 1, tq), lambda hd, i: (hd, i, 0, 0, 0)),
                   pl.BlockSpec(memory_space=pl.ANY)],
        out_shape=[jax.ShapeDtypeStruct((lp, D_EXP), F32), jax.ShapeDtypeStruct((HEADS, 1, lp), F32),
                   jax.ShapeDtypeStruct((HEADS, lp // tq, nk, 1, tq), F32),
                   jax.ShapeDtypeStruct((HEADS, lp // tq, nk, tk, tq), BF16)],
        scratch_shapes=[pltpu.VMEM((VT_ROWS, tq), F32), pltpu.VMEM((1, tq), F32),
                        pltpu.VMEM((tk, tq), F32), pltpu.VMEM((tk, tq), F32),
                        pltpu.VMEM((2, QUAD, tk, tq), BF16), pltpu.SemaphoreType.DMA((2,))],
        compiler_params=_cparams(("parallel", "parallel")),
    )(q, k, vt)


def _unpermute_rows(val, scr, out_ref, seg):
    for c in range(val.shape[1] // 128):
        scr[c] = val[:, c * 128:(c + 1) * 128]
    for k in range(8):
        for c in range(val.shape[1] // 128):
            out_ref[k * seg:(k + 1) * seg, c * 128:(c + 1) * 128] = scr[c, pl.ds(k, seg, stride=8), :]


def _scan_rows(xr_ref, xi_ref, base, n_rows, coef_ref, carry_ref, reverse, tile_fn=None, acc_refs=(), halo=False):
    seg = n_rows // 8
    shifts = (7, 6, 4) if reverse else (1, 2, 4)
    row8 = lax.broadcasted_iota(jnp.int32, (8, SCAN_COLS), 0)
    edge, shift = (7, 7) if reverse else (0, 1)
    for cg in range(N_STATE // SCAN_COLS):
        cols = slice(cg * SCAN_COLS, (cg + 1) * SCAN_COLS)
        ar, ai = coef_ref[8, :, cols], coef_ref[9, :, cols]

        def rows_at(i):
            tau = (seg - 1 - i) if reverse else i
            return tau, pl.ds(pl.multiple_of(base + tau * 8, 8), 8)

        def local(i, carry, cols=cols, ar=ar, ai=ai):
            pr, pi_ = carry
            _, rows = rows_at(i)
            nr = ar * pr - ai * pi_ + xr_ref[rows, cols]
            ni = ar * pi_ + ai * pr + xi_ref[rows, cols]
            xr_ref[rows, cols] = nr
            xi_ref[rows, cols] = ni
            return nr, ni

        zero = jnp.zeros((8, SCAN_COLS), F32)
        fr, fi = lax.fori_loop(0, seg, local, (zero, zero), unroll=SCAN_UNROLL)
        co = [coef_ref[k, :, cols] for k in range(8)]
        for lvl in range(3):
            pr, pi_ = co[2 * lvl], co[2 * lvl + 1]
            sr = pltpu.roll(fr, shifts[lvl], 0)
            si = pltpu.roll(fi, shifts[lvl], 0)
            fr, fi = fr + pr * sr - pi_ * si, fi + pr * si + pi_ * sr
        cr, ci = carry_ref[0:1, cols], carry_ref[1:2, cols]
        fr, fi = fr + co[6] * cr - co[7] * ci, fi + co[6] * ci + co[7] * cr
        carry_ref[0:1, cols] = fr[0:1] if reverse else fr[7:8]
        carry_ref[1:2, cols] = fi[0:1] if reverse else fi[7:8]
        in_r = jnp.where(row8 == edge, cr, pltpu.roll(fr, shift, 0))
        in_i = jnp.where(row8 == edge, ci, pltpu.roll(fi, shift, 0))
        if halo:
            rows = pl.ds(base + n_rows, 8) if reverse else pl.ds(base - 8, 8)
            xr_ref[rows, cols] = in_r
            xi_ref[rows, cols] = in_i

        def fix(i, carry, cols=cols, ar=ar, ai=ai):
            c_r, c_i = carry[0], carry[1]
            tau, rows = rows_at(i)
            nr = xr_ref[rows, cols] + c_r
            ni = xi_ref[rows, cols] + c_i
            xr_ref[rows, cols] = nr
            xi_ref[rows, cols] = ni
            accs = carry[2:]
            if tile_fn is not None:
                accs = tuple(a + d for a, d in zip(accs, tile_fn(tau, cols, nr, ni)))
            return (ar * c_r - ai * c_i, ar * c_i + ai * c_r) + accs

        init = (ar * in_r - ai * in_i, ar * in_i + ai * in_r) + tuple(a[:, cols] for a in acc_refs)
        out = lax.fori_loop(0, seg, fix, init, unroll=SCAN_UNROLL)
        for a, val in zip(acc_refs, out[2:]):
            a[:, cols] = val


def _ssm_fwd(proj, perm, coef, b_re, b_im, c_re, c_im_neg, reverse):
    lp = proj.shape[0]
    t = _ssm_tile(lp)
    n = lp // t
    order = (lambda i: n - 1 - i) if reverse else (lambda i: i)

    def body(u_ref, pm_ref, coef_ref, bre_ref, bim_ref, cre_ref, cim_ref, y_ref, st_ref, xr, xi, carry, stage):
        @pl.when(pl.program_id(0) == 0)
        def _():
            carry[...] = jnp.zeros_like(carry)

        st_ref[...] = carry[0:2, :]
        ub = _dot(pm_ref[...], u_ref[...].astype(BF16)).astype(BF16)
        for j in range(SSM_BLOCKS):
            ch, stt = slice(j * BLK_CH, (j + 1) * BLK_CH), slice(j * BLK_ST, (j + 1) * BLK_ST)
            xr[:, stt] = _dot(ub[:, ch], bre_ref[j])
            xi[:, stt] = _dot(ub[:, ch], bim_ref[j])
        _scan_rows(xr, xi, 0, t, coef_ref, carry, reverse)
        y = jnp.concatenate(
            [_dot(xr[:, j * BLK_ST:(j + 1) * BLK_ST].astype(BF16), cre_ref[j])
             + _dot(xi[:, j * BLK_ST:(j + 1) * BLK_ST].astype(BF16), cim_ref[j]) for j in range(SSM_BLOCKS)], axis=1)
        _unpermute_rows(y, stage, y_ref, t // 8)

    wb, wc = _whole((SSM_BLOCKS, BLK_CH, BLK_ST)), _whole((SSM_BLOCKS, BLK_ST, BLK_CH))
    return pl.pallas_call(
        body, name="ssm_fwd_rev" if reverse else "ssm_fwd", grid=(n,),
        in_specs=[pl.BlockSpec((t, D_SSM), lambda i: (order(i), P_U[0] // D_SSM)), _whole((t, t)), _whole((10, 8, N_STATE)),
                  wb, wb, wc, wc],
        out_specs=[pl.BlockSpec((t, D_SSM), lambda i: (order(i), 0)),
                   pl.BlockSpec((None, 2, N_STATE), lambda i: (order(i), 0, 0))],
        out_shape=[jax.ShapeDtypeStruct((lp, D_SSM), F32), jax.ShapeDtypeStruct((n, 2, N_STATE), F32)],
        scratch_shapes=[pltpu.VMEM((t, N_STATE), F32), pltpu.VMEM((t, N_STATE), F32), pltpu.VMEM((8, N_STATE), F32),
                        pltpu.VMEM((D_SSM // 128, t, 128), F32)],
        compiler_params=_cparams(("arbitrary",)),
    )(proj, perm, coef, b_re, b_im, c_re, c_im_neg)


GELU_C0 = math.sqrt(2.0 / math.pi)
GELU_C1 = 0.044715


def _mid(h, tgt, o_exp, proj, y0, y1, ssm_d, w_glu, w_glu_t, b_glu, ssm_norm_w, attn_norm_w_e, w_out_a, w_out_s,
         w_out_a_t, w_out_s_t, post_w, l_real):
    lp = h.shape[0]
    tm = 256

    def body(h_ref, tga_ref, tgb_ref, o_ref, ga_ref, u_ref, sg_ref, y0_ref, y1_ref, d_ref, wg_ref, wgt_ref, bg_ref, ws_ref,
             wa_ref, woa_ref, wos_ref, woat_ref, wost_ref, pw_ref,
             do_ref, dot_ref, delta_ref, dga_ref, dyp_ref, dsg_ref, dres_ref, dwoa_ref, dwos_ref, dwg_ref, vec_ref):
        @pl.when(pl.program_id(0) == 0)
        def _():
            dwoa_ref[...] = jnp.zeros_like(dwoa_ref)
            dwos_ref[...] = jnp.zeros_like(dwos_ref)
            dwg_ref[...] = jnp.zeros_like(dwg_ref)
            vec_ref[...] = jnp.zeros_like(vec_ref)

        u = u_ref[...]
        ypre = y0_ref[...] + y1_ref[...] + d_ref[...] * u
        th = jnp.tanh(GELU_C0 * (ypre + GELU_C1 * ypre * ypre * ypre))
        gel = 0.5 * ypre * (1.0 + th)
        gel_b = gel.astype(BF16)
        glu = _dot(gel_b, wg_ref[...]) + bg_ref[...]
        g1, g2 = glu[:, :D_SSM], glu[:, D_SSM:]
        sig2 = _sigmoid(g2)
        z = g1 * sig2
        sg = sg_ref[...]
        sgs = _sigmoid(sg)
        sil_s = sg * sgs
        s = z * sil_s
        ys, r_s = _rms_fwd(s, ws_ref[...], D_SSM)

        o = o_ref[...]
        ga = ga_ref[...]
        gas = _sigmoid(ga)
        sil_a = ga * gas
        a = o * sil_a
        ya, r_a = _rms_fwd(a, wa_ref[...], D_ATTN)

        ya_b, ys_b = ya.astype(BF16), ys.astype(BF16)
        y = _dot(ya_b, woa_ref[...]) + _dot(ys_b, wos_ref[...])
        yn, r_y = _rms_fwd(y, pw_ref[...], D_MODEL)
        row = lax.broadcasted_iota(jnp.int32, (tm, 1), 0) + pl.program_id(0) * tm
        valid = (row >= N_META) & (row < l_real)
        tgt = jnp.concatenate([tga_ref[tm - N_META:, :], tgb_ref[:tm - N_META, :]], axis=0)
        err = jnp.where(valid, h_ref[...] + yn - tgt, 0.0)
        loss = 0.5 * jnp.sum(jnp.sum(err * err, axis=-1, keepdims=True), axis=0, keepdims=True) * (1.0 / D_MODEL)
        dout = err * (1.0 / D_MODEL)
        dres_ref[...] = dout

        dy, d_pw = _rms_bwd(y, r_y, pw_ref[...], dout, D_MODEL)
        dy_b = dy.astype(BF16)
        dya = _dot(dy_b, woat_ref[...])
        dys = _dot(dy_b, wost_ref[...])
        dwoa_ref[...] += _dot_tn(ya_b, dy_b)
        dwos_ref[...] += _dot_tn(ys_b, dy_b)

        da, d_wa = _rms_bwd(a, r_a, wa_ref[...], dya, D_ATTN)
        d_o = da * sil_a
        dga_ref[...] = da * o * (gas * (1.0 + ga * (1.0 - gas)))
        do_ref[...] = d_o.astype(BF16)
        for hd in range(HEADS):
            dot_ref[hd * V_HEAD:(hd + 1) * V_HEAD, :] = d_o[:, hd * HEAD_PAD:(hd + 1) * HEAD_PAD].T[:V_HEAD].astype(BF16)
        prod = d_o * o
        lane8 = lax.broadcasted_iota(jnp.int32, (tm, HEADS), 1)
        delta = jnp.zeros((tm, HEADS), F32)
        for hd in range(HEADS):
            delta = jnp.where(lane8 == hd, jnp.sum(prod[:, hd * HEAD_PAD:(hd + 1) * HEAD_PAD], axis=-1, keepdims=True), delta)
        delta_ref[...] = delta

        ds, d_ws = _rms_bwd(s, r_s, ws_ref[...], dys, D_SSM)
        dz = ds * sil_s
        dsg_ref[...] = ds * z * (sgs * (1.0 + sg * (1.0 - sgs)))
        dglu = jnp.concatenate([dz * sig2, dz * g1 * sig2 * (1.0 - sig2)], axis=-1)
        dglu_b = dglu.astype(BF16)
        dwg_ref[...] += _dot_tn(gel_b, dglu_b)
        dgel = _dot(dglu_b, wgt_ref[...])
        dgelu = 0.5 * (1.0 + th) + 0.5 * ypre * (1.0 - th * th) * (GELU_C0 * (1.0 + 3.0 * GELU_C1 * ypre * ypre))
        dyp = dgel * dgelu
        dyp_ref[...] = dyp

        vec_ref[0:1, :] += d_pw
        vec_ref[1:2, :] += d_wa
        vec_ref[2:3, 0:D_SSM] += d_ws
        vec_ref[3:4, 0:D_SSM] += jnp.sum(dyp * u, axis=0, keepdims=True)
        vec_ref[4:5, :] += jnp.sum(dglu, axis=0, keepdims=True)
        vec_ref[5:6, :] += jnp.broadcast_to(loss, (1, D_MODEL))

    full = lambda off: _rows(tm, (off, D_MODEL))
    half = lambda off: _rows(tm, (off, D_SSM))
    last = tgt.shape[0] // tm - 1
    tg_a = pl.BlockSpec((tm, D_MODEL), lambda i: (jnp.clip(i - 1, 0, last), 0))
    tg_b = pl.BlockSpec((tm, D_MODEL), lambda i: (jnp.minimum(i, last), 0))
    return pl.pallas_call(
        body, name="mid", grid=(lp // tm,),
        in_specs=[full(0), tg_a, tg_b, full(0), _rows(tm, P_GATE_A), _rows(tm, P_U), _rows(tm, P_GATE_S), half(0), half(0),
                  _whole((1, D_SSM)), _whole((D_SSM, 2 * D_SSM)), _whole((2 * D_SSM, D_SSM)), _whole((1, 2 * D_SSM)),
                  _whole((1, D_SSM)), _whole((1, D_EXP)), _whole((D_EXP, D_MODEL)), _whole((D_SSM, D_MODEL)),
                  _whole((D_MODEL, D_EXP)), _whole((D_MODEL, D_SSM)), _whole((1, D_MODEL))],
        out_specs=[full(0), pl.BlockSpec((D_ATTN, tm), lambda i: (0, i)), _rows(tm, (0, HEADS)), full(0), half(0), half(0), full(0),
                   _out_whole((D_EXP, D_MODEL)), _out_whole((D_SSM, D_MODEL)), _out_whole((D_SSM, 2 * D_SSM)),
                   _out_whole((8, D_MODEL))],
        out_shape=[jax.ShapeDtypeStruct((lp, D_EXP), BF16), jax.ShapeDtypeStruct((D_ATTN, lp), BF16),
                   jax.ShapeDtypeStruct((lp, HEADS), F32),
                   jax.ShapeDtypeStruct((lp, D_EXP), F32), jax.ShapeDtypeStruct((lp, D_SSM), F32),
                   jax.ShapeDtypeStruct((lp, D_SSM), F32), jax.ShapeDtypeStruct((lp, D_MODEL), F32),
                   jax.ShapeDtypeStruct((D_EXP, D_MODEL), F32), jax.ShapeDtypeStruct((D_SSM, D_MODEL), F32),
                   jax.ShapeDtypeStruct((D_SSM, 2 * D_SSM), F32), jax.ShapeDtypeStruct((8, D_MODEL), F32)],
        compiler_params=_cparams(("arbitrary",)),
    )(h, tgt, tgt, o_exp, proj, proj, proj, y0, y1, ssm_d, w_glu, w_glu_t, b_glu, ssm_norm_w, attn_norm_w_e, w_out_a, w_out_s,
      w_out_a_t, w_out_s_t, post_w)


def _ssm_bwd(proj, dyp, states, perm, coef, coef_adj, b_re, b_im, b_re_t, b_im_t, c_re_t, c_im_neg_t, reverse):
    lp = proj.shape[0]
    t = _ssm_tile(lp)
    n = lp // t
    order = (lambda i: i) if reverse else (lambda i: n - 1 - i)

    def body(u_ref, dy_ref, st_ref, pm_ref, coef_ref, coefa_ref, bre_ref, bim_ref, bret_ref, bimt_ref, cret_ref, cimt_ref,
             du_ref, dbre_ref, dbim_ref, dcre_ref, dcim_ref, da_ref, xr, xi, gr, gi, carry_x, carry_g, stage):
        @pl.when(pl.program_id(0) == 0)
        def _():
            carry_g[...] = jnp.zeros_like(carry_g)
            carry_x[...] = jnp.zeros_like(carry_x)
            dbre_ref[...] = jnp.zeros_like(dbre_ref)
            dbim_ref[...] = jnp.zeros_like(dbim_ref)
            dcre_ref[...] = jnp.zeros_like(dcre_ref)
            dcim_ref[...] = jnp.zeros_like(dcim_ref)
            da_ref[...] = jnp.zeros_like(da_ref)
            for halo in (slice(0, 8), slice(t + 8, t + 16)):
                xr[halo, :] = jnp.zeros((8, N_STATE), F32)
                xi[halo, :] = jnp.zeros((8, N_STATE), F32)

        ub = _dot(pm_ref[...], u_ref[...].astype(BF16)).astype(BF16)
        dyb = _dot(pm_ref[...], dy_ref[...].astype(BF16)).astype(BF16)
        carry_x[0:2, :] = st_ref[...]
        blocks = [(slice(j * BLK_CH, (j + 1) * BLK_CH), slice(j * BLK_ST, (j + 1) * BLK_ST)) for j in range(SSM_BLOCKS)]
        for j, (ch, stt) in enumerate(blocks):
            xr[8:t + 8, stt] = _dot(ub[:, ch], bre_ref[j])
            xi[8:t + 8, stt] = _dot(ub[:, ch], bim_ref[j])
            gr[:, stt] = _dot(dyb[:, ch], cret_ref[j])
            gi[:, stt] = _dot(dyb[:, ch], cimt_ref[j])
        _scan_rows(xr, xi, 8, t, coef_ref, carry_x, reverse, halo=True)

        def tile_fn(tau, cols, g_re, g_im):
            nb = pl.ds(pl.multiple_of((tau + 2) * 8 if reverse else tau * 8, 8), 8)
            xn_r, xn_i = xr[nb, cols], xi[nb, cols]
            return g_re * xn_r + g_im * xn_i, g_im * xn_r - g_re * xn_i

        _scan_rows(gr, gi, 0, t, coefa_ref, carry_g, not reverse, tile_fn=tile_fn, acc_refs=(da_ref.at[0], da_ref.at[1]))

        du = []
        for j, (ch, stt) in enumerate(blocks):
            g_re_b, g_im_b = gr[:, stt].astype(BF16), gi[:, stt].astype(BF16)
            du.append(_dot(g_re_b, bret_ref[j]) + _dot(g_im_b, bimt_ref[j]))
            dbre_ref[j] += _dot_tn(ub[:, ch], g_re_b)
            dbim_ref[j] += _dot_tn(ub[:, ch], g_im_b)
            dcre_ref[j] += _dot_tn(dyb[:, ch], xr[8:t + 8, stt].astype(BF16))
            dcim_ref[j] -= _dot_tn(dyb[:, ch], xi[8:t + 8, stt].astype(BF16))
        _unpermute_rows(jnp.concatenate(du, axis=1), stage, du_ref, t // 8)

    dense = jax.ShapeDtypeStruct((SSM_BLOCKS, BLK_CH, BLK_ST), F32)
    wb, wc = _whole((SSM_BLOCKS, BLK_CH, BLK_ST)), _whole((SSM_BLOCKS, BLK_ST, BLK_CH))
    acc = _out_whole((SSM_BLOCKS, BLK_CH, BLK_ST))
    return pl.pallas_call(
        body, name="ssm_bwd_rev" if reverse else "ssm_bwd", grid=(n,),
        in_specs=[pl.BlockSpec((t, D_SSM), lambda i: (order(i), P_U[0] // D_SSM)),
                  pl.BlockSpec((t, D_SSM), lambda i: (order(i), 0)),
                  pl.BlockSpec((None, 2, N_STATE), lambda i: (order(i), 0, 0)), _whole((t, t)),
                  _whole((10, 8, N_STATE)), _whole((10, 8, N_STATE)), wb, wb, wc, wc, wb, wb],
        out_specs=[pl.BlockSpec((t, D_SSM), lambda i: (order(i), 0)), acc, acc, acc, acc, _out_whole((2, 8, N_STATE))],
        out_shape=[jax.ShapeDtypeStruct((lp, D_SSM), F32), dense, dense, dense, dense,
                   jax.ShapeDtypeStruct((2, 8, N_STATE), F32)],
        scratch_shapes=[pltpu.VMEM((t + 16, N_STATE), F32), pltpu.VMEM((t + 16, N_STATE), F32),
                        pltpu.VMEM((t, N_STATE), F32), pltpu.VMEM((t, N_STATE), F32),
                        pltpu.VMEM((8, N_STATE), F32), pltpu.VMEM((8, N_STATE), F32),
                        pltpu.VMEM((D_SSM // 128, t, 128), F32)],
        compiler_params=_cparams(("arbitrary",)),
    )(proj, dyp, states, perm, coef, coef_adj, b_re, b_im, b_re_t, b_im_t, c_re_t, c_im_neg_t)


def _flash_bwd(q, v, d_o, q_t, k_t, do_t, lse_row, delta_row, mblk, p_all):
    lp = q.shape[0]
    tq = 1280 if lp % 1280 == 0 else 256
    tk = QBLK
    nk = lp // tk
    d_qk = QK_NOPE + QK_ROPE
    grp = 5 if nk % 5 == 0 else 1
    n_groups = nk // grp

    def body(do_ref, qt_ref, dot_ref, lse_ref, delta_ref, mblk_ref, v_ref, kt_ref, p_hbm, dq_ref, dk_ref, dv_ref,
             dq_acc, p_buf, p_sem):
        hd, qi = pl.program_id(0), pl.program_id(1)

        def p_copy(t, slot):
            return [pltpu.make_async_copy(p_hbm.at[hd, qi, t * grp + u], p_buf.at[slot, u], p_sem.at[slot]) for u in range(grp)]

        for t0 in range(min(P_AHEAD, n_groups)):
            for u, cp in enumerate(p_copy(t0, t0)):
                cp.start(priority=u % 2)

        @pl.when(qi == 0)
        def _():
            dk_ref[...] = jnp.zeros_like(dk_ref)
            dv_ref[...] = jnp.zeros_like(dv_ref)

        dq_acc[...] = jnp.zeros_like(dq_acc)
        lse, delta = lse_ref[...], delta_ref[...]
        q_cols, do_cols = qt_ref[...], dot_ref[...]
        blocks = [slice(c * QBLK, (c + 1) * QBLK) for c in range(tq // QBLK)]

        def group(t, _):
            base = pl.multiple_of(t * (grp * tk), grp * tk)
            slot = t % (P_AHEAD + 1)
            for cp in p_copy(t, slot):
                cp.wait()

            @pl.when(t + P_AHEAD < n_groups)
            def _():
                for u, cp in enumerate(p_copy(t + P_AHEAD, (t + P_AHEAD) % (P_AHEAD + 1))):
                    cp.start(priority=u % 2)

            dq = dq_acc[...]
            dvs, dks = [], []
            for u in range(grp):
                j = t * grp + u
                ks = pl.multiple_of(base + u * tk, tk)
                v_rows = v_ref[pl.ds(ks, tk), :]
                dpt = [_dot_nt(v_rows, do_ref[cols, :]) for cols in blocks]
                pt = p_buf[slot, u].astype(F32) * jnp.exp2(mblk_ref[j] - lse)
                pt_b = pt.astype(BF16)
                dst_b = jnp.concatenate([(pt[:, cols] * (dp_c - delta[:, cols])).astype(BF16)
                                         for dp_c, cols in zip(dpt, blocks)], axis=1)
                dvs.append(_dot_nt(do_cols, pt_b))
                dks.append(_dot_nt(q_cols, dst_b))
                dq = dq + _dot(kt_ref[:, pl.ds(ks, tk)], dst_b)
            dq_acc[...] = dq
            dv_ref[:, pl.ds(base, grp * tk)] += jnp.concatenate(dvs, axis=1)
            dk_ref[:, pl.ds(base, grp * tk)] += jnp.concatenate(dks, axis=1) * (1.0 / LOG2E)
            return 0

        lax.fori_loop(0, n_groups, group, 0)
        dq_ref[...] = jnp.concatenate([dq_acc[...], jnp.zeros((HEAD_PAD - d_qk, tq), F32)], axis=0).T

    tile = pl.BlockSpec((tq, HEAD_PAD), lambda hd, i: (i, hd))
    head = pl.BlockSpec((lp, HEAD_PAD), lambda hd, i: (0, hd))
    rowv = pl.BlockSpec((None, 1, tq), lambda hd, i: (hd, 0, i))
    return pl.pallas_call(
        body, name="flash_bwd", grid=(HEADS, lp // tq),
        in_specs=[tile, pl.BlockSpec((d_qk, tq), lambda hd, i: (hd, i)), pl.BlockSpec((V_HEAD, tq), lambda hd, i: (hd, i)),
                  rowv, rowv, pl.BlockSpec((None, None, nk, 1, tq), lambda hd, i: (hd, i, 0, 0, 0)), head,
                  pl.BlockSpec((d_qk, lp), lambda hd, i: (hd, 0)), pl.BlockSpec(memory_space=pl.ANY)],
        out_specs=[tile, pl.BlockSpec((d_qk, lp), lambda hd, i: (hd, 0)), pl.BlockSpec((V_HEAD, lp), lambda hd, i: (hd, 0))],
        out_shape=[jax.ShapeDtypeStruct((lp, D_EXP), F32), jax.ShapeDtypeStruct((HEADS * d_qk, lp), F32),
                   jax.ShapeDtypeStruct((HEADS * V_HEAD, lp), F32)],
        scratch_shapes=[pltpu.VMEM((d_qk, tq), F32), pltpu.VMEM((P_AHEAD + 1, grp, tk, tq), BF16),
                        pltpu.SemaphoreType.DMA((P_AHEAD + 1,))],
        compiler_params=_cparams(("parallel", "arbitrary")),
    )(d_o, q_t, do_t, lse_row, delta_row, mblk, v, k_t, p_all)


def _attn_prep_bwd(dq, dk_t, dv_t, proj, q_norm_w, kv_norm_w, wq_pt, wk_pt, wv_pt, cos, sina, sinb):
    lp = proj.shape[0]
    tm = _row_tile(lp)

    def body(dq_ref, dk_ref, dv_ref, ql_ref, kvl_ref, qw_ref, kw_ref, wqt_ref, wkt_ref, wvt_ref, cos_ref, sa_ref, sb_ref,
             dql_ref, dkvl_ref, dkr_ref, dwq_ref, dwk_ref, dwv_ref, vec_ref):
        @pl.when(pl.program_id(0) == 0)
        def _():
            dwq_ref[...] = jnp.zeros_like(dwq_ref)
            dwk_ref[...] = jnp.zeros_like(dwk_ref)
            dwv_ref[...] = jnp.zeros_like(dwv_ref)
            vec_ref[...] = jnp.zeros_like(vec_ref)

        cos_t, sa_t, sb_t = cos_ref[...], sa_ref[...], sb_ref[...]

        def head_rows(t_ref, per):
            pad = jnp.zeros((HEAD_PAD - per, tm), F32)
            return jnp.concatenate(
                [jnp.concatenate([t_ref[hd * per:(hd + 1) * per, :], pad], axis=0).T for hd in range(HEADS)], axis=-1)

        dkp = head_rows(dk_ref, D_QK)
        dqp = jnp.concatenate(
            [_rope_transpose(dq_ref[:, hd * HEAD_PAD:(hd + 1) * HEAD_PAD] * SCALE, cos_t, sa_t, sb_t) for hd in range(HEADS)],
            axis=-1)
        dkr = dkp[:, 0:HEAD_PAD]
        for hd in range(1, HEADS):
            dkr = dkr + dkp[:, hd * HEAD_PAD:(hd + 1) * HEAD_PAD]
        dkr_ref[...] = _rope_transpose(dkr, cos_t, sa_t, sb_t)

        qn, r_q = _rms_fwd(ql_ref[...], qw_ref[...], Q_LORA)
        kvn, r_kv = _rms_fwd(kvl_ref[...], kw_ref[...], KV_LORA)
        dqp_b, dkp_b, dv_b = dqp.astype(BF16), dkp.astype(BF16), head_rows(dv_ref, V_HEAD).astype(BF16)
        dqn = _dot(dqp_b, wqt_ref[...])
        dkvn = _dot(dkp_b, wkt_ref[...]) + _dot(dv_b, wvt_ref[...])
        dwq_ref[...] += _dot_tn(qn.astype(BF16), dqp_b)
        dwk_ref[...] += _dot_tn(kvn.astype(BF16), dkp_b)
        dwv_ref[...] += _dot_tn(kvn.astype(BF16), dv_b)
        dql, d_qw = _rms_bwd(ql_ref[...], r_q, qw_ref[...], dqn, Q_LORA)
        dkvl, d_kw = _rms_bwd(kvl_ref[...], r_kv, kw_ref[...], dkvn, KV_LORA)
        dql_ref[...] = dql
        dkvl_ref[...] = dkvl
        vec_ref[0:1, :] += d_qw
        vec_ref[1:2, 0:KV_LORA] += d_kw

    tab = _rows(tm, (0, HEAD_PAD))
    full = _rows(tm, (0, D_EXP))
    return pl.pallas_call(
        body, name="attn_prep_bwd", grid=(lp // tm,),
        in_specs=[full, pl.BlockSpec((HEADS * D_QK, tm), lambda i: (0, i)), pl.BlockSpec((D_ATTN, tm), lambda i: (0, i)),
                  _rows(tm, P_QLAT), _rows(tm, P_KVLAT), _whole((1, Q_LORA)), _whole((1, KV_LORA)),
                  _whole((D_EXP, Q_LORA)), _whole((D_EXP, KV_LORA)), _whole((D_EXP, KV_LORA)), tab, tab, tab],
        out_specs=[_rows(tm, (0, Q_LORA)), _rows(tm, (0, KV_LORA)), _rows(tm, (0, HEAD_PAD)),
                   _out_whole((Q_LORA, D_EXP)), _out_whole((KV_LORA, D_EXP)), _out_whole((KV_LORA, D_EXP)),
                   _out_whole((8, Q_LORA))],
        out_shape=[jax.ShapeDtypeStruct((lp, Q_LORA), F32), jax.ShapeDtypeStruct((lp, KV_LORA), F32),
                   jax.ShapeDtypeStruct((lp, HEAD_PAD), F32), jax.ShapeDtypeStruct((Q_LORA, D_EXP), F32),
                   jax.ShapeDtypeStruct((KV_LORA, D_EXP), F32), jax.ShapeDtypeStruct((KV_LORA, D_EXP), F32),
                   jax.ShapeDtypeStruct((8, Q_LORA), F32)],
        compiler_params=_cparams(("arbitrary",)),
    )(dq, dk_t, dv_t, proj, proj, q_norm_w, kv_norm_w, wq_pt, wk_pt, wv_pt, cos, sina, sinb)


def _in_proj_bwd(h, pre_w, dres, dga, du0, du1, dyp, ssm_d, dsg, dql, dkvl, dkr, w_in_pt):
    lp = h.shape[0]
    tm = 256
    pieces = (P_GATE_A, P_U, P_GATE_S, P_QLAT, P_KVLAT, P_KROPE)

    def body(h_ref, w_ref, dres_ref, dga_ref, du0_ref, du1_ref, dyp_ref, d_ref, dsg_ref, dql_ref, dkvl_ref, dkr_ref, wt_ref,
             dh_ref, dw_ref, vec_ref):
        @pl.when(pl.program_id(0) == 0)
        def _():
            dw_ref[...] = jnp.zeros_like(dw_ref)
            vec_ref[...] = jnp.zeros_like(vec_ref)

        hv = h_ref[...]
        xn, r = _rms_fwd(hv, w_ref[...], D_MODEL)
        xn_b = xn.astype(BF16)
        du = du0_ref[...] + du1_ref[...] + dyp_ref[...] * d_ref[...]
        grads = (dga_ref[...], du, dsg_ref[...], dql_ref[...], dkvl_ref[...], dkr_ref[...])
        dxn = jnp.zeros((tm, D_MODEL), F32)
        for (off, width), g in zip(pieces, grads):
            g_b = g.astype(BF16)
            dxn = dxn + _dot(g_b, wt_ref[off:off + width, :])
            dw_ref[:, off:off + width] += _dot_tn(xn_b, g_b)
        dx, d_w = _rms_bwd(hv, r, w_ref[...], dxn, D_MODEL)
        dh_ref[...] = dres_ref[...] + dx
        vec_ref[0:1, :] += d_w

    full = _rows(tm, (0, D_MODEL))
    half = _rows(tm, (0, D_SSM))
    return pl.pallas_call(
        body, name="in_proj_bwd", grid=(lp // tm,),
        in_specs=[full, _whole((1, D_MODEL)), full, full, half, half, half, _whole((1, D_SSM)), half,
                  _rows(tm, (0, Q_LORA)), _rows(tm, (0, KV_LORA)), _rows(tm, (0, HEAD_PAD)), _whole((D_PROJ, D_MODEL))],
        out_specs=[full, _out_whole((D_MODEL, D_PROJ)), _out_whole((8, D_MODEL))],
        out_shape=[jax.ShapeDtypeStruct((lp, D_MODEL), F32), jax.ShapeDtypeStruct((D_MODEL, D_PROJ), F32),
                   jax.ShapeDtypeStruct((8, D_MODEL), F32)],
        compiler_params=_cparams(("arbitrary",)),
    )(h, pre_w, dres, dga, du0, du1, dyp, ssm_d, dsg, dql, dkvl, dkr, w_in_pt)


def _other_chips(x, y):
    return [(1 - x, y), (x, 1 - y), (1 - x, 1 - y)]


def _gather_weights(w_bf16, meta):
    any_spec = pl.BlockSpec(memory_space=pl.ANY)
    halves = (w_bf16.shape[0] // 2, meta.shape[0] // 2)

    def body(w_ref, m_ref, wout_ref, mout_ref, send_sems, recv_sems, local_sems):
        x, y, c = lax.axis_index("x"), lax.axis_index("y"), lax.axis_index("c")
        me, sibling = 2 * x + y, (x, y, 1 - c)
        srcs, dsts = (w_ref, m_ref), (wout_ref, mout_ref)

        def half(n, cc):
            return pl.ds(pl.multiple_of(cc * halves[n], 8), halves[n])

        def copy(n, sem, src, chip, cc, to):
            return pltpu.make_async_remote_copy(src_ref=src, dst_ref=dsts[n].at[chip, half(n, cc)], send_sem=send_sems.at[sem],
                                                recv_sem=recv_sems.at[sem], device_id=to, device_id_type=MESH)

        own = [pltpu.make_async_copy(srcs[n], dsts[n].at[me], local_sems.at[n]) for n in range(2)]
        for cp in own:
            cp.start()
        chips = _other_chips(x, y)
        first = [copy(n, 2 * j + n, srcs[n].at[half(n, c)], me, c, (tx, ty, c)) for j, (tx, ty) in enumerate(chips) for n in range(2)]
        for cp in first:
            cp.start()
        passed = []
        for j, (tx, ty) in enumerate(chips):
            for n in range(2):
                landed = dsts[n].at[2 * tx + ty, half(n, c)]
                copy(n, 2 * j + n, landed, 2 * tx + ty, c, (tx, ty, c)).wait_recv()
                passed.append(copy(n, 6 + 2 * j + n, landed, 2 * tx + ty, c, sibling))
                passed[-1].start()
        for j, (tx, ty) in enumerate(chips):
            for n in range(2):
                copy(n, 6 + 2 * j + n, dsts[n].at[2 * tx + ty, half(n, 1 - c)], 2 * tx + ty, 1 - c, sibling).wait_recv()
        for cp in first + passed:
            cp.wait_send()
        for cp in own:
            cp.wait()

    return pl.pallas_call(
        body, name="gather_weights",
        in_specs=[any_spec, any_spec], out_specs=[any_spec, any_spec],
        out_shape=[jax.ShapeDtypeStruct((4,) + w_bf16.shape, w_bf16.dtype), jax.ShapeDtypeStruct((4,) + meta.shape, meta.dtype)],
        scratch_shapes=[pltpu.SemaphoreType.DMA((12,)), pltpu.SemaphoreType.DMA((12,)), pltpu.SemaphoreType.DMA((2,))],
    )(w_bf16, meta)


def _swap_sibling(g):
    any_spec = pl.BlockSpec(memory_space=pl.ANY)

    def body(g_ref, out_ref, send_sem, recv_sem):
        x, y, c = lax.axis_index("x"), lax.axis_index("y"), lax.axis_index("c")
        cp = pltpu.make_async_remote_copy(src_ref=g_ref, dst_ref=out_ref, send_sem=send_sem, recv_sem=recv_sem,
                                          device_id=(x, y, 1 - c), device_id_type=MESH)
        cp.start()
        cp.wait()

    return pl.pallas_call(
        body, name="swap_sibling", in_specs=[any_spec], out_specs=any_spec,
        out_shape=jax.ShapeDtypeStruct(g.shape, g.dtype),
        scratch_shapes=[pltpu.SemaphoreType.DMA(()), pltpu.SemaphoreType.DMA(())],
    )(g)


def _pair_sum(a, b):
    rows = a.shape[0]
    tm = _pick_tile(rows, 1024)

    def body(a_ref, b_ref, o_ref):
        o_ref[...] = a_ref[...] + b_ref[...]

    spec = pl.BlockSpec((tm, 1024), lambda i: (i, 0))
    return pl.pallas_call(body, name="pair_sum", grid=(rows // tm,), in_specs=[spec, spec], out_specs=spec,
                          out_shape=jax.ShapeDtypeStruct(a.shape, F32), compiler_params=_cparams(("parallel",)))(a, b)


def _scatter_chips(s, rs, rsm):
    any_spec = pl.BlockSpec(memory_space=pl.ANY)
    lens = (rs // 2, rsm // 2)

    def body(s_ref, out_ref, send_sems, recv_sems, local_sems):
        x, y, c = lax.axis_index("x"), lax.axis_index("y"), lax.axis_index("c")
        me, sibling = 2 * x + y, (x, y, 1 - c)

        def src_rows(n, target, cc):
            start = (target * rs if n == 0 else 4 * rs) + cc * lens[n]
            return s_ref.at[pl.ds(pl.multiple_of(start, 8), lens[n])]

        def dst_rows(n, cc):
            return pl.ds(pl.multiple_of((0 if n == 0 else rs) + cc * lens[n], 8), lens[n])

        def copy(n, sem, src, chip, cc, to):
            return pltpu.make_async_remote_copy(src_ref=src, dst_ref=out_ref.at[chip, dst_rows(n, cc)], send_sem=send_sems.at[sem],
                                                recv_sem=recv_sems.at[sem], device_id=to, device_id_type=MESH)

        own = [pltpu.make_async_copy(s_ref.at[pl.ds(pl.multiple_of(me * rs, 8), rs)], out_ref.at[me, pl.ds(0, rs)], local_sems.at[0]),
               pltpu.make_async_copy(s_ref.at[pl.ds(4 * rs, rsm)], out_ref.at[me, pl.ds(rs, rsm)], local_sems.at[1])]
        for cp in own:
            cp.start()
        chips = _other_chips(x, y)
        first = [copy(n, 2 * j + n, src_rows(n, 2 * tx + ty, c), me, c, (tx, ty, c))
                 for j, (tx, ty) in enumerate(chips) for n in range(2)]
        for cp in first:
            cp.start()
        passed = []
        for j, (tx, ty) in enumerate(chips):
            for n in range(2):
                landed = out_ref.at[2 * tx + ty, dst_rows(n, c)]
                copy(n, 2 * j + n, landed, 2 * tx + ty, c, (tx, ty, c)).wait_recv()
                passed.append(copy(n, 6 + 2 * j + n, landed, 2 * tx + ty, c, sibling))
                passed[-1].start()
        for j, (tx, ty) in enumerate(chips):
            for n in range(2):
                copy(n, 6 + 2 * j + n, out_ref.at[2 * tx + ty, dst_rows(n, 1 - c)], 2 * tx + ty, 1 - c, sibling).wait_recv()
        for cp in first + passed:
            cp.wait_send()
        for cp in own:
            cp.wait()

    return pl.pallas_call(
        body, name="scatter_chips", in_specs=[any_spec], out_specs=any_spec,
        out_shape=jax.ShapeDtypeStruct((4, rs + rsm, 1024), F32),
        scratch_shapes=[pltpu.SemaphoreType.DMA((12,)), pltpu.SemaphoreType.DMA((12,)), pltpu.SemaphoreType.DMA((2,))],
    )(s)


def _adamw(parts, w, m, v):
    rows = w.shape[0]
    tm = _pick_tile(rows, 256)
    c1 = 1.0 / (1.0 - ADAM_B1 ** ADAM_STEP)
    c2 = 1.0 / (1.0 - ADAM_B2 ** ADAM_STEP)

    def body(p_ref, w_ref, m_ref, v_ref, g_ref, d_ref, nm_ref, nv_ref):
        g = ((p_ref[0] + p_ref[1]) + p_ref[2]) + p_ref[3]
        nm = ADAM_B1 * m_ref[...] + (1.0 - ADAM_B1) * g
        nv = ADAM_B2 * v_ref[...] + (1.0 - ADAM_B2) * (g * g)
        g_ref[...] = g
        nm_ref[...] = nm
        nv_ref[...] = nv
        d_ref[...] = -ADAM_LR * ((nm * c1) / (jnp.sqrt(nv * c2) + ADAM_EPS) + ADAM_WD * w_ref[...])

    spec = pl.BlockSpec((tm, 1024), lambda i: (i, 0))
    out = jax.ShapeDtypeStruct(w.shape, F32)
    return pl.pallas_call(
        body, name="adamw", grid=(rows // tm,),
        in_specs=[pl.BlockSpec((4, tm, 1024), lambda i: (0, i, 0)), spec, spec, spec],
        out_specs=[spec] * 4, out_shape=[out] * 4, compiler_params=_cparams(("parallel",)),
    )(parts, w, m, v)


def _expand_heads(a, axis, per_head):
    a = jnp.moveaxis(a, axis, -1)
    lead = a.shape[:-1]
    a = a.reshape(lead + (HEADS, per_head))
    a = jnp.pad(a, [(0, 0)] * len(lead) + [(0, 0), (0, HEAD_PAD - per_head)])
    return jnp.moveaxis(a.reshape(lead + (D_EXP,)), -1, axis)


def _compact_heads(a, axis, start, size):
    a = jnp.moveaxis(a, axis, -1)
    lead = a.shape[:-1]
    a = a.reshape(lead + (HEADS, HEAD_PAD))[..., start:start + size]
    return jnp.moveaxis(a.reshape(lead + (HEADS * size,)), -1, axis)


def _block_diag(w):
    g, a, b = w.shape
    per = g // SSM_BLOCKS
    eye = jnp.eye(per, dtype=w.dtype)
    return jnp.einsum("jgab,gk->jgakb", w.reshape(SSM_BLOCKS, per, a, b), eye).reshape(SSM_BLOCKS, per * a, per * b)


def _block_diag_extract(dense, a, b):
    per = N_GROUPS // SSM_BLOCKS
    d5 = dense.reshape(SSM_BLOCKS, per, a, per, b)
    return jnp.einsum("jgakb,gk->jgab", d5, jnp.eye(per, dtype=dense.dtype)).reshape(N_GROUPS, a, b)


def _discretise(a_re, a_im, log_dt, b_re, b_im):
    dt = jnp.exp(log_dt)[:, None]
    mag = jnp.exp(a_re * dt)
    abar_re = mag * jnp.cos(a_im * dt)
    abar_im = mag * jnp.sin(a_im * dt)
    num_re = abar_re - 1.0
    num_im = abar_im
    den = a_re * a_re + a_im * a_im
    coef_re = (num_re * a_re + num_im * a_im) / den
    coef_im = (num_im * a_re - num_re * a_im) / den
    bbar_re = coef_re[..., None] * b_re - coef_im[..., None] * b_im
    bbar_im = coef_re[..., None] * b_im + coef_im[..., None] * b_re
    return abar_re, abar_im, bbar_re, bbar_im


def _scan_coef(ar, ai, reverse, seg):
    ar, ai = ar.reshape(1, N_STATE), ai.reshape(1, N_STATE)
    cmul = lambda x, y: (x[0] * y[0] - x[1] * y[1], x[0] * y[1] + x[1] * y[0])
    p, sq, n = None, (ar, ai), seg
    while n:
        if n & 1:
            p = sq if p is None else cmul(p, sq)
        sq, n = cmul(sq, sq), n >> 1
    pows = [p]
    for _ in range(7):
        pows.append(cmul(pows[-1], p))
    row = jnp.arange(8)[:, None]
    out = []
    for k in (1, 2, 4):
        keep = (row < 8 - k) if reverse else (row >= k)
        out += [jnp.where(keep, pows[k - 1][0], 0.0), jnp.where(keep, pows[k - 1][1], 0.0)]
    order = list(range(7, -1, -1)) if reverse else list(range(8))
    out += [jnp.concatenate([pows[k][0] for k in order], axis=0), jnp.concatenate([pows[k][1] for k in order], axis=0)]
    out += [jnp.broadcast_to(ar, (8, N_STATE)), jnp.broadcast_to(ai, (8, N_STATE))]
    return jnp.stack(out).astype(F32)


def _flat_rows(a, rows):
    flat = a.reshape(-1)
    return jnp.pad(flat, (0, rows * 1024 - flat.shape[0])).reshape(rows, 1024)


def _pack(named, order):
    rows = [-(-math.prod(named[n].shape) // 1024) for n in order]
    total = -(-sum(rows) // 32) * 32
    parts = [_flat_rows(named[n], r) for n, r in zip(order, rows)]
    if total > sum(rows):
        parts.append(jnp.zeros((total - sum(rows), 1024), parts[0].dtype))
    return jnp.concatenate(parts, axis=0)


def _unpack(packed, shapes, order):
    out, at = {}, 0
    for n in order:
        size = math.prod(shapes[n])
        rows = -(-size // 1024)
        out[n] = packed[at:at + rows].reshape(-1)[:size].reshape(shapes[n])
        at += rows
    return out


def _shard_cols(a, k):
    w = a.shape[-1] // 4
    return a[..., k * w:(k + 1) * w]


def kernel(x, meta_tokens, pre_norm_w, post_norm_w, w_in, q_norm_w, w_q_up, kv_norm_w, w_kv_up, attn_out_norm_w, ssm_a_re, ssm_a_im, ssm_log_dt, ssm_b_re, ssm_b_im, ssm_c_re, ssm_c_im, ssm_d, w_glu, b_glu, ssm_out_norm_w, w_out, loss_target, m_meta_tokens, m_pre_norm_w, m_post_norm_w, m_w_in, m_q_norm_w, m_w_q_up, m_kv_norm_w, m_w_kv_up, m_attn_out_norm_w, m_ssm_a_re, m_ssm_a_im, m_ssm_log_dt, m_ssm_b_re, m_ssm_b_im, m_ssm_c_re, m_ssm_c_im, m_ssm_d, m_w_glu, m_b_glu, m_ssm_out_norm_w, m_w_out, v_meta_tokens, v_pre_norm_w, v_post_norm_w, v_w_in, v_q_norm_w, v_w_q_up, v_kv_norm_w, v_w_kv_up, v_attn_out_norm_w, v_ssm_a_re, v_ssm_a_im, v_ssm_log_dt, v_ssm_b_re, v_ssm_b_im, v_ssm_c_re, v_ssm_c_im, v_ssm_d, v_w_glu, v_b_glu, v_ssm_out_norm_w, v_w_out):
    local = dict(meta_tokens=meta_tokens, pre_norm_w=pre_norm_w, post_norm_w=post_norm_w, w_in=w_in, q_norm_w=q_norm_w,
                 w_q_up=w_q_up, kv_norm_w=kv_norm_w, w_kv_up=w_kv_up, attn_out_norm_w=attn_out_norm_w, ssm_a_re=ssm_a_re,
                 ssm_a_im=ssm_a_im, ssm_log_dt=ssm_log_dt, ssm_b_re=ssm_b_re, ssm_b_im=ssm_b_im, ssm_c_re=ssm_c_re,
                 ssm_c_im=ssm_c_im, ssm_d=ssm_d, w_glu=w_glu, b_glu=b_glu, ssm_out_norm_w=ssm_out_norm_w, w_out=w_out)
    mom_m = dict(meta_tokens=m_meta_tokens, pre_norm_w=m_pre_norm_w, post_norm_w=m_post_norm_w, w_in=m_w_in,
                 q_norm_w=m_q_norm_w, w_q_up=m_w_q_up, kv_norm_w=m_kv_norm_w, w_kv_up=m_w_kv_up,
                 attn_out_norm_w=m_attn_out_norm_w, ssm_a_re=m_ssm_a_re, ssm_a_im=m_ssm_a_im, ssm_log_dt=m_ssm_log_dt,
                 ssm_b_re=m_ssm_b_re, ssm_b_im=m_ssm_b_im, ssm_c_re=m_ssm_c_re, ssm_c_im=m_ssm_c_im, ssm_d=m_ssm_d,
                 w_glu=m_w_glu, b_glu=m_b_glu, ssm_out_norm_w=m_ssm_out_norm_w, w_out=m_w_out)
    mom_v = dict(meta_tokens=v_meta_tokens, pre_norm_w=v_pre_norm_w, post_norm_w=v_post_norm_w, w_in=v_w_in,
                 q_norm_w=v_q_norm_w, w_q_up=v_w_q_up, kv_norm_w=v_kv_norm_w, w_kv_up=v_w_kv_up,
                 attn_out_norm_w=v_attn_out_norm_w, ssm_a_re=v_ssm_a_re, ssm_a_im=v_ssm_a_im, ssm_log_dt=v_ssm_log_dt,
                 ssm_b_re=v_ssm_b_re, ssm_b_im=v_ssm_b_im, ssm_c_re=v_ssm_c_re, ssm_c_im=v_ssm_c_im, ssm_d=v_ssm_d,
                 w_glu=v_w_glu, b_glu=v_b_glu, ssm_out_norm_w=v_ssm_out_norm_w, w_out=v_w_out)
    shapes = {n: local[n].shape for n in WEIGHTS}
    mat = ("w_in", "w_q_up", "w_kv_up", "w_glu", "w_out")

    seq = x.shape[1]
    l_real = N_META + seq
    lp = -(-l_real // 1280) * 1280 if l_real > 1280 else -(-l_real // QBLK) * QBLK
    assert seq % 256 == 0 and lp % QBLK == 0

    w_shard = _pack({n: local[n].astype(BF16) for n in mat}, mat)
    w_shard = jnp.pad(w_shard, ((0, -w_shard.shape[0] % 16), (0, 0)))
    w_all, meta_all = _gather_weights(w_shard, meta_tokens)
    mat_shapes = {n: shapes[n] for n in mat}
    per_chip = [_unpack(w_all[k], mat_shapes, mat) for k in range(4)]
    w_in_f = jnp.concatenate([p["w_in"][0] for p in per_chip], axis=1)
    w_q_f = jnp.concatenate([p["w_q_up"][0] for p in per_chip], axis=1)
    w_kv_f = jnp.concatenate([p["w_kv_up"][0] for p in per_chip], axis=1)
    w_glu_f = jnp.concatenate([p["w_glu"][0] for p in per_chip], axis=1)
    w_out_f = jnp.concatenate([p["w_out"][0] for p in per_chip], axis=0)
    meta_f = jnp.concatenate([meta_all[k] for k in range(4)], axis=1)

    o_q, o_kv, o_kr, o_ga, o_u, o_gs = 0, 256, 384, 416, 928, 1440
    krope_cols = jnp.pad(w_in_f[:, o_kr:o_ga], ((0, 0), (QK_NOPE, HEAD_PAD - QK_NOPE - QK_ROPE)))
    w_in_p = jnp.concatenate([_expand_heads(w_in_f[:, o_ga:o_u], 1, V_HEAD), w_in_f[:, o_u:o_gs], w_in_f[:, o_gs:],
                              w_in_f[:, o_q:o_kv], w_in_f[:, o_kv:o_kr], krope_cols], axis=1)
    wq_p = _expand_heads(w_q_f, 1, QK_NOPE + QK_ROPE)
    kv3 = w_kv_f.reshape(KV_LORA, HEADS, QK_NOPE + V_HEAD)
    wk_p = _expand_heads(kv3[:, :, :QK_NOPE].reshape(KV_LORA, HEADS * QK_NOPE), 1, QK_NOPE)
    wv_c = kv3[:, :, QK_NOPE:].reshape(KV_LORA, HEADS * V_HEAD)
    wv_p = _expand_heads(wv_c, 1, V_HEAD)
    wv_t = jnp.pad(wv_c.T.reshape(HEADS, V_HEAD, KV_LORA), ((0, 0), (0, VT_ROWS - V_HEAD), (0, 0))).reshape(HEADS * VT_ROWS, KV_LORA)
    w_out_a = _expand_heads(w_out_f[:D_ATTN], 0, V_HEAD)
    w_out_s = w_out_f[D_ATTN:]
    attn_norm_e = _expand_heads(attn_out_norm_w, 1, V_HEAD)

    pos = jnp.arange(lp, dtype=jnp.int32)
    half = QK_ROPE // 2
    inv = ROPE_THETA ** (-jnp.arange(half, dtype=F32) / half)
    ang = pos.astype(F32)[:, None] * inv[None, :]
    cos16, sin16 = jnp.cos(ang), jnp.sin(ang)
    ones, zeros = jnp.ones((lp, QK_NOPE), F32), jnp.zeros((lp, QK_NOPE), F32)
    tail1, tail0 = jnp.ones((lp, HEAD_PAD - MASK_LANE), F32), jnp.zeros((lp, HEAD_PAD - MASK_LANE), F32)
    z16 = jnp.zeros((lp, half), F32)
    cos = jnp.concatenate([ones, cos16, cos16, tail1], axis=1)
    sina = jnp.concatenate([zeros, z16, sin16, tail0], axis=1)
    sinb = jnp.concatenate([zeros, -sin16, z16, tail0], axis=1)

    disc_in = (ssm_a_re[0], ssm_a_im[0], ssm_log_dt[0], ssm_b_re[0], ssm_b_im[0])
    disc = lambda a_re, a_im, ldt, b_re, b_im: jax.vmap(_discretise)(a_re, a_im, ldt, b_re, b_im)
    (abar_re, abar_im, bbar_re, bbar_im), disc_vjp = jax.vjp(disc, *disc_in)
    ssm = []
    for d in range(2):
        rev = d == 1
        b_re_bd = _block_diag(jnp.swapaxes(bbar_re[d], 1, 2)).astype(BF16)
        b_im_bd = _block_diag(jnp.swapaxes(bbar_im[d], 1, 2)).astype(BF16)
        c_re_bd = _block_diag(jnp.swapaxes(ssm_c_re[0, d], 1, 2)).astype(BF16)
        c_im_bd = _block_diag(jnp.swapaxes(-ssm_c_im[0, d], 1, 2)).astype(BF16)
        ssm.append(dict(rev=rev, coef=_scan_coef(abar_re[d], abar_im[d], rev, _ssm_tile(lp) // 8),
                        coef_adj=_scan_coef(abar_re[d], -abar_im[d], not rev, _ssm_tile(lp) // 8),
                        b_re=b_re_bd, b_im=b_im_bd, c_re=c_re_bd, c_im=c_im_bd))

    t_ssm = _ssm_tile(lp)
    src = (jnp.arange(t_ssm) % 8) * (t_ssm // 8) + jnp.arange(t_ssm) // 8
    perm = (src[:, None] == jnp.arange(t_ssm)[None, :]).astype(BF16)

    h = jnp.concatenate([meta_f, x[0], jnp.zeros((lp - l_real, D_MODEL), F32)], axis=0)
    proj = _in_proj_fwd(h, pre_norm_w, w_in_p)
    q, k, v, vt, q_t, k_t = _attn_prep_fwd(proj, q_norm_w, kv_norm_w, wq_p, wk_p, wv_p, wv_t, cos, sina, sinb, l_real)
    o_exp, lse, mblk, p_all = _flash_fwd(q, k, vt)
    ys, states = [], []
    for s in ssm:
        y_d, st_d = _ssm_fwd(proj, perm, s["coef"], s["b_re"], s["b_im"], s["c_re"], s["c_im"], s["rev"])
        ys.append(y_d)
        states.append(st_d)

    (d_o, do_t, delta, dga, dyp, dsg, dres, dwoa, dwos, dwglu, vec_mid) = _mid(
        h, loss_target[0], o_exp, proj, ys[0], ys[1], ssm_d, w_glu_f, w_glu_f.T, b_glu, ssm_out_norm_w, attn_norm_e, w_out_a, w_out_s,
        w_out_a.T, w_out_s.T, post_norm_w, l_real)
    dus, dssm = [], []
    tr = lambda a: jnp.swapaxes(a, 1, 2)
    for s, st_d in zip(ssm, states):
        du_d, dbre, dbim, dcre, dcim, da = _ssm_bwd(proj, dyp, st_d, perm, s["coef"], s["coef_adj"], s["b_re"], s["b_im"],
                                                    tr(s["b_re"]), tr(s["b_im"]), tr(s["c_re"]), tr(s["c_im"]), s["rev"])
        dus.append(du_d)
        dssm.append((dbre, dbim, dcre, dcim, da))
    dq, dk_t, dv_t = _flash_bwd(q, v, d_o, q_t, k_t, do_t, lse, delta.T.reshape(HEADS, 1, lp), mblk, p_all)
    dql, dkvl, dkr, dwq_p, dwk_p, dwv_p, vec_prep = _attn_prep_bwd(
        dq, dk_t, dv_t, proj, q_norm_w, kv_norm_w, wq_p.T, wk_p.T, wv_p.T, cos, sina, sinb)
    dh, dwin_p, vec_in = _in_proj_bwd(h, pre_norm_w, dres, dga, dus[0], dus[1], dyp, ssm_d, dsg, dql, dkvl, dkr, w_in_p.T)

    grads = {}
    grads["w_in"] = jnp.concatenate([
        dwin_p[:, P_QLAT[0]:P_QLAT[0] + 256], dwin_p[:, P_KVLAT[0]:P_KVLAT[0] + 128],
        dwin_p[:, P_KROPE[0] + QK_NOPE:P_KROPE[0] + QK_NOPE + QK_ROPE], _compact_heads(dwin_p[:, 0:D_EXP], 1, 0, V_HEAD),
        dwin_p[:, P_U[0]:P_U[0] + 512], dwin_p[:, P_GATE_S[0]:P_GATE_S[0] + 512]], axis=1)[None]
    grads["w_q_up"] = _compact_heads(dwq_p, 1, 0, QK_NOPE + QK_ROPE)[None]
    dwk3 = _compact_heads(dwk_p, 1, 0, QK_NOPE).reshape(KV_LORA, HEADS, QK_NOPE)
    dwv3 = _compact_heads(dwv_p, 1, 0, V_HEAD).reshape(KV_LORA, HEADS, V_HEAD)
    grads["w_kv_up"] = jnp.concatenate([dwk3, dwv3], axis=2).reshape(1, KV_LORA, HEADS * (QK_NOPE + V_HEAD))
    grads["w_glu"] = dwglu[None]
    grads["w_out"] = jnp.concatenate([_compact_heads(dwoa, 0, 0, V_HEAD), dwos], axis=0)[None]
    grads["meta_tokens"] = dh[:N_META]
    grads["pre_norm_w"] = vec_in[0:1]
    grads["post_norm_w"] = vec_mid[0:1]
    grads["q_norm_w"] = vec_prep[0:1]
    grads["kv_norm_w"] = vec_prep[1:2, :KV_LORA]
    grads["attn_out_norm_w"] = _compact_heads(vec_mid[1:2], 1, 0, V_HEAD)
    grads["ssm_out_norm_w"] = vec_mid[2:3, :D_SSM]
    grads["ssm_d"] = vec_mid[3:4, :D_SSM]
    grads["b_glu"] = vec_mid[4:5]
    d_abar_re = jnp.stack([dssm[d][4][0].sum(axis=0).reshape(N_GROUPS, SSM_STATE) for d in range(2)])
    d_abar_im = jnp.stack([dssm[d][4][1].sum(axis=0).reshape(N_GROUPS, SSM_STATE) for d in range(2)])
    d_bbar_re = jnp.stack([jnp.swapaxes(_block_diag_extract(dssm[d][0], SSM_GROUP, SSM_STATE), 1, 2) for d in range(2)])
    d_bbar_im = jnp.stack([jnp.swapaxes(_block_diag_extract(dssm[d][1], SSM_GROUP, SSM_STATE), 1, 2) for d in range(2)])
    da_re, da_im, dlog_dt, db_re, db_im = disc_vjp((d_abar_re, d_abar_im, d_bbar_re, d_bbar_im))
    grads["ssm_a_re"], grads["ssm_a_im"], grads["ssm_log_dt"] = da_re[None], da_im[None], dlog_dt[None]
    grads["ssm_b_re"], grads["ssm_b_im"] = db_re[None], db_im[None]
    grads["ssm_c_re"] = jnp.stack([_block_diag_extract(dssm[d][2], SSM_GROUP, SSM_STATE) for d in range(2)])[None]
    grads["ssm_c_im"] = jnp.stack([_block_diag_extract(dssm[d][3], SSM_GROUP, SSM_STATE) for d in range(2)])[None]

    def shard_of(n, a, kk):
        return a[:, kk * 256:(kk + 1) * 256] if n == "w_out" else _shard_cols(a, kk)

    slices = [_pack({n: shard_of(n, grads[n], kk) for n in BIG}, BIG) for kk in range(4)]
    grads["loss"] = vec_mid[5:6, 0:1]
    small = _pack({n: grads[n] for n in SMALL + ("loss",)}, SMALL + ("loss",))
    loss_row = slices[0].shape[0] + sum(-(-math.prod(shapes[n]) // 1024) for n in SMALL)
    rs, rsm = slices[0].shape[0], small.shape[0]
    g_pack = jnp.concatenate(slices + [small], axis=0)
    g_pair = _pair_sum(g_pack, _swap_sibling(g_pack))
    parts = _scatter_chips(g_pair, rs, rsm)

    order = BIG + SMALL
    big_shapes = {n: shapes[n] for n in BIG}
    small_shapes = {n: shapes[n] for n in SMALL}

    def pack_state(named):
        return jnp.concatenate([_pack({n: named[n] for n in BIG}, BIG), _pack({n: named[n] for n in SMALL}, SMALL)], axis=0)

    g_out, d_out, m_out, v_out = _adamw(parts, pack_state(local), pack_state(mom_m), pack_state(mom_v))

    def unpack_state(p):
        out = _unpack(p[:rs], big_shapes, BIG)
        out.update(_unpack(p[rs:], small_shapes, SMALL))
        return out

    g_fin, d_fin, m_fin, v_fin = unpack_state(g_out), unpack_state(d_out), unpack_state(m_out), unpack_state(v_out)
    loss = g_out[loss_row, 0]
    grad_x = dh[N_META:l_real][None]
    return (loss, grad_x, *[g_fin[n] for n in WEIGHTS], *[d_fin[n] for n in WEIGHTS], *[m_fin[n] for n in WEIGHTS],
            *[v_fin[n] for n in WEIGHTS])
```
